```python
import math
import jax, jax.numpy as jnp
from jax import lax
import numpy as np

D_MODEL = 1024
BATCH = 8
SEQ = 2048
DEPTH = 1

CHUNK = 64
Q_BLOCK = 128
ATTN_WIDTH = D_MODEL // 2
ATTN_HEAD_DIM = 64
ATTN_HEADS = ATTN_WIDTH // ATTN_HEAD_DIM
SSM_WIDTH = D_MODEL - ATTN_WIDTH
SSM_GROUP = 16
SSM_GROUPS = SSM_WIDTH // SSM_GROUP
SSM_STATE = 64
D_FF = 2816
CONV_WIDTH = 3
IN_COLS = 3 * ATTN_WIDTH + ATTN_HEADS + SSM_WIDTH
EPS = 1e-6
NEG_INF = -1e30

kernel_name = "hybrid_fox_s5_convffn_block"


def _rms_norm(x, g):
    xf = x.astype(jnp.float32)
    y = xf * lax.rsqrt(jnp.mean(xf * xf, axis=-1, keepdims=True) + EPS)
    return (y * g.astype(jnp.float32)).astype(x.dtype)


def _forgetting_attention(q, k, v, log_f):
    seq = q.shape[2]
    c = jnp.cumsum(log_f, axis=-1)
    scale = ATTN_HEAD_DIM ** -0.5
    outs = []
    for start in range(0, seq, Q_BLOCK):
        end = start + Q_BLOCK
        qb = q[:, :, start:end]
        kb = k[:, :, :end]
        vb = v[:, :, :end]
        s = jnp.einsum('bhqd,bhkd->bhqk', qb, kb) * scale
        s = s + c[:, :, start:end, None] - c[:, :, None, :end]
        qpos = jnp.arange(start, end)[:, None]
        kpos = jnp.arange(end)[None, :]
        s = jnp.where(kpos <= qpos, s, NEG_INF)
        p = jax.nn.softmax(s, axis=-1)
        outs.append(jnp.einsum('bhqk,bhkd->bhqd', p, vb))
    return jnp.concatenate(outs, axis=2)


def _ssm_combine(e1, e2):
    a1r, a1i, b1r, b1i = e1
    a2r, a2i, b2r, b2i = e2
    ar = a2r * a1r - a2i * a1i
    ai = a2r * a1i + a2i * a1r
    br = a2r * b1r - a2i * b1i + b2r
    bi = a2r * b1i + a2i * b1r + b2i
    return (ar, ai, br, bi)


def _s5_ssm(u, lambda_re, lambda_im, log_step, b_re, b_im, c_re, c_im, d_skip):
    seq = u.shape[1]
    lr = lambda_re.astype(jnp.float32)
    li = lambda_im.astype(jnp.float32)
    step = jnp.exp(log_step.astype(jnp.float32))[:, None]
    er = jnp.exp(lr * step)
    ab_re = er * jnp.cos(li * step)
    ab_im = er * jnp.sin(li * step)
    num_re = ab_re - 1.0
    num_im = ab_im
    den = lr * lr + li * li
    f_re = (num_re * lr + num_im * li) / den
    f_im = (num_im * lr - num_re * li) / den
    br = b_re.astype(jnp.float32)
    bi = b_im.astype(jnp.float32)
    bb_re = f_re[:, :, None] * br - f_im[:, :, None] * bi
    bb_im = f_re[:, :, None] * bi + f_im[:, :, None] * br
    bu_re = jnp.einsum('bsgh,gph->bsgp', u, bb_re)
    bu_im = jnp.einsum('bsgh,gph->bsgp', u, bb_im)
    a_re = jnp.broadcast_to(ab_re[None, None], (1, seq) + ab_re.shape)
    a_im = jnp.broadcast_to(ab_im[None, None], (1, seq) + ab_im.shape)
    _, _, x_re, x_im = lax.associative_scan(_ssm_combine, (a_re, a_im, bu_re, bu_im), axis=1)
    y = (jnp.einsum('bsgp,ghp->bsgh', x_re, c_re.astype(jnp.float32))
         - jnp.einsum('bsgp,ghp->bsgh', x_im, c_im.astype(jnp.float32)))
    return y + d_skip.astype(jnp.float32) * u


def _causal_dwconv(h, w, b):
    seq = h.shape[1]
    hp = jnp.pad(h, ((0, 0), (CONV_WIDTH - 1, 0), (0, 0)))
    y = b
    for j in range(CONV_WIDTH):
        y = y + hp[:, j:j + seq] * w[j]
    return y


def _fwd_setup_inputs(seed: int = 0) -> dict:
    key = jax.random.key(seed)
    ks = jax.random.split(key, 24)
    f32 = jnp.float32
    n = lambda k, shape, s: jax.random.normal(k, shape, f32) * s
    x = n(ks[0], (BATCH, SEQ, D_MODEL), 1.0)
    g_mix = 1.0 + n(ks[1], (D_MODEL,), 0.02)
    w_in = n(ks[2], (D_MODEL, IN_COLS), D_MODEL ** -0.5)
    b_f = 2.0 + n(ks[3], (ATTN_HEADS,), 0.1)
    g_q = 1.0 + n(ks[4], (ATTN_HEAD_DIM,), 0.02)
    g_k = 1.0 + n(ks[5], (ATTN_HEAD_DIM,), 0.02)
    lambda_re = -0.5 + n(ks[6], (SSM_GROUPS, SSM_STATE), 0.01)
    lambda_im = (math.pi * jnp.arange(SSM_STATE, dtype=f32))[None, :] + n(ks[7], (SSM_GROUPS, SSM_STATE), 0.01)
    log_step = jax.random.uniform(ks[8], (SSM_GROUPS,), f32, math.log(1e-3), math.log(1e-1))
    b_re = n(ks[9], (SSM_GROUPS, SSM_STATE, SSM_GROUP), (2.0 * SSM_GROUP) ** -0.5)
    b_im = n(ks[10], (SSM_GROUPS, SSM_STATE, SSM_GROUP), (2.0 * SSM_GROUP) ** -0.5)
    c_re = n(ks[11], (SSM_GROUPS, SSM_GROUP, SSM_STATE), (2.0 * SSM_STATE) ** -0.5)
    c_im = n(ks[12], (SSM_GROUPS, SSM_GROUP, SSM_STATE), (2.0 * SSM_STATE) ** -0.5)
    d_skip = n(ks[13], (SSM_GROUPS, SSM_GROUP), 1.0)
    w_glu = n(ks[14], (SSM_WIDTH, SSM_WIDTH), SSM_WIDTH ** -0.5)
    b_glu = n(ks[15], (SSM_WIDTH,), 0.02)
    g_attn_out = 1.0 + n(ks[16], (ATTN_WIDTH,), 0.02)
    g_ssm_out = 1.0 + n(ks[17], (SSM_WIDTH,), 0.02)
    w_out = n(ks[18], (D_MODEL, D_MODEL), D_MODEL ** -0.5)
    g_ffn = 1.0 + n(ks[19], (D_MODEL,), 0.02)
    w_up = n(ks[20], (D_MODEL, 2 * D_FF), D_MODEL ** -0.5)
    conv_w = n(ks[21], (CONV_WIDTH, 2 * D_FF), CONV_WIDTH ** -0.5)
    conv_b = n(ks[22], (2 * D_FF,), 0.02)
    w_down = n(ks[23], (D_FF, D_MODEL), D_FF ** -0.5)
    return {"x": x, "g_mix": g_mix, "w_in": w_in, "b_f": b_f, "g_q": g_q, "g_k": g_k,
            "lambda_re": lambda_re, "lambda_im": lambda_im, "log_step": log_step,
            "b_re": b_re, "b_im": b_im, "c_re": c_re, "c_im": c_im, "d_skip": d_skip,
            "w_glu": w_glu, "b_glu": b_glu, "g_attn_out": g_attn_out, "g_ssm_out": g_ssm_out,
            "w_out": w_out, "g_ffn": g_ffn, "w_up": w_up, "conv_w": conv_w, "conv_b": conv_b,
            "w_down": w_down}


def _fwd_reference(x, g_mix, w_in, b_f, g_q, g_k, lambda_re, lambda_im, log_step, b_re, b_im,
              c_re, c_im, d_skip, w_glu, b_glu, g_attn_out, g_ssm_out, w_out, g_ffn,
              w_up, conv_w, conv_b, w_down):
    bsz, seq, _ = x.shape
    f32 = jnp.float32
    for _layer in range(DEPTH):
        h = _rms_norm(x, g_mix)
        z = h @ w_in
        o = 0
        q = z[..., o:o + ATTN_WIDTH]; o += ATTN_WIDTH
        k = z[..., o:o + ATTN_WIDTH]; o += ATTN_WIDTH
        v = z[..., o:o + ATTN_WIDTH]; o += ATTN_WIDTH
        f_logit = z[..., o:o + ATTN_HEADS]; o += ATTN_HEADS
        u = z[..., o:o + SSM_WIDTH]

        heads = lambda t: t.reshape(bsz, seq, ATTN_HEADS, ATTN_HEAD_DIM)
        qh = _rms_norm(heads(q), g_q).astype(f32).transpose(0, 2, 1, 3)
        kh = _rms_norm(heads(k), g_k).astype(f32).transpose(0, 2, 1, 3)
        vh = heads(v).astype(f32).transpose(0, 2, 1, 3)
        log_f = jax.nn.log_sigmoid((f_logit + b_f).astype(f32)).transpose(0, 2, 1)
        att = _forgetting_attention(qh, kh, vh, log_f)
        att = att.transpose(0, 2, 1, 3).reshape(bsz, seq, ATTN_WIDTH).astype(x.dtype)

        ug = u.astype(f32).reshape(bsz, seq, SSM_GROUPS, SSM_GROUP)
        y = _s5_ssm(ug, lambda_re, lambda_im, log_step, b_re, b_im, c_re, c_im, d_skip)
        y = jax.nn.gelu(y.reshape(bsz, seq, SSM_WIDTH))
        y = y * jax.nn.sigmoid(y @ w_glu.astype(f32) + b_glu.astype(f32))
        ssm = y.astype(x.dtype)

        mixed = jnp.concatenate([_rms_norm(att, g_attn_out), _rms_norm(ssm, g_ssm_out)], axis=-1)
        x = x + mixed @ w_out

        h2 = _rms_norm(x, g_ffn)
        hh = _causal_dwconv(h2 @ w_up, conv_w, conv_b)
        gate = hh[..., :D_FF]
        val = hh[..., D_FF:]
        x = x + (jax.nn.silu(gate) * val) @ w_down
    return x


import jax as _jax
import jax.numpy as _jnp

TWIN_FORMAT = 'train_step'
FWD_PARAMS = ['x', 'g_mix', 'w_in', 'b_f', 'g_q', 'g_k', 'lambda_re', 'lambda_im', 'log_step', 'b_re', 'b_im', 'c_re', 'c_im', 'd_skip', 'w_glu', 'b_glu', 'g_attn_out', 'g_ssm_out', 'w_out', 'g_ffn', 'w_up', 'conv_w', 'conv_b', 'w_down']
TWIN_WEIGHTS = ['g_mix', 'w_in', 'b_f', 'g_q', 'g_k', 'lambda_re', 'lambda_im', 'log_step', 'b_re', 'b_im', 'c_re', 'c_im', 'd_skip', 'w_glu', 'b_glu', 'g_attn_out', 'g_ssm_out', 'w_out', 'g_ffn', 'w_up', 'conv_w', 'conv_b', 'w_down']
TWIN_DIFF_INPUT = 'x'
TWIN_INPUTS = ['x', 'g_mix', 'w_in', 'b_f', 'g_q', 'g_k', 'lambda_re', 'lambda_im', 'log_step', 'b_re', 'b_im', 'c_re', 'c_im', 'd_skip', 'w_glu', 'b_glu', 'g_attn_out', 'g_ssm_out', 'w_out', 'g_ffn', 'w_up', 'conv_w', 'conv_b', 'w_down', 'loss_target', 'm_g_mix', 'm_w_in', 'm_b_f', 'm_g_q', 'm_g_k', 'm_lambda_re', 'm_lambda_im', 'm_log_step', 'm_b_re', 'm_b_im', 'm_c_re', 'm_c_im', 'm_d_skip', 'm_w_glu', 'm_b_glu', 'm_g_attn_out', 'm_g_ssm_out', 'm_w_out', 'm_g_ffn', 'm_w_up', 'm_conv_w', 'm_conv_b', 'm_w_down', 'v_g_mix', 'v_w_in', 'v_b_f', 'v_g_q', 'v_g_k', 'v_lambda_re', 'v_lambda_im', 'v_log_step', 'v_b_re', 'v_b_im', 'v_c_re', 'v_c_im', 'v_d_skip', 'v_w_glu', 'v_b_glu', 'v_g_attn_out', 'v_g_ssm_out', 'v_w_out', 'v_g_ffn', 'v_w_up', 'v_conv_w', 'v_conv_b', 'v_w_down']
TWIN_OUTPUTS = ['loss', 'grad_x', 'grad_g_mix', 'grad_w_in', 'grad_b_f', 'grad_g_q', 'grad_g_k', 'grad_lambda_re', 'grad_lambda_im', 'grad_log_step', 'grad_b_re', 'grad_b_im', 'grad_c_re', 'grad_c_im', 'grad_d_skip', 'grad_w_glu', 'grad_b_glu', 'grad_g_attn_out', 'grad_g_ssm_out', 'grad_w_out', 'grad_g_ffn', 'grad_w_up', 'grad_conv_w', 'grad_conv_b', 'grad_w_down', 'delta_g_mix', 'delta_w_in', 'delta_b_f', 'delta_g_q', 'delta_g_k', 'delta_lambda_re', 'delta_lambda_im', 'delta_log_step', 'delta_b_re', 'delta_b_im', 'delta_c_re', 'delta_c_im', 'delta_d_skip', 'delta_w_glu', 'delta_b_glu', 'delta_g_attn_out', 'delta_g_ssm_out', 'delta_w_out', 'delta_g_ffn', 'delta_w_up', 'delta_conv_w', 'delta_conv_b', 'delta_w_down', 'new_m_g_mix', 'new_m_w_in', 'new_m_b_f', 'new_m_g_q', 'new_m_g_k', 'new_m_lambda_re', 'new_m_lambda_im', 'new_m_log_step', 'new_m_b_re', 'new_m_b_im', 'new_m_c_re', 'new_m_c_im', 'new_m_d_skip', 'new_m_w_glu', 'new_m_b_glu', 'new_m_g_attn_out', 'new_m_g_ssm_out', 'new_m_w_out', 'new_m_g_ffn', 'new_m_w_up', 'new_m_conv_w', 'new_m_conv_b', 'new_m_w_down', 'new_v_g_mix', 'new_v_w_in', 'new_v_b_f', 'new_v_g_q', 'new_v_g_k', 'new_v_lambda_re', 'new_v_lambda_im', 'new_v_log_step', 'new_v_b_re', 'new_v_b_im', 'new_v_c_re', 'new_v_c_im', 'new_v_d_skip', 'new_v_w_glu', 'new_v_b_glu', 'new_v_g_attn_out', 'new_v_g_ssm_out', 'new_v_w_out', 'new_v_g_ffn', 'new_v_w_up', 'new_v_conv_w', 'new_v_conv_b', 'new_v_w_down']
TWIN_LEAF_KINDS = {'loss': 'loss', 'grad_x': 'grad_x', 'grad_g_mix': 'grad_w', 'grad_w_in': 'grad_w', 'grad_b_f': 'grad_w', 'grad_g_q': 'grad_w', 'grad_g_k': 'grad_w', 'grad_lambda_re': 'grad_w', 'grad_lambda_im': 'grad_w', 'grad_log_step': 'grad_w', 'grad_b_re': 'grad_w', 'grad_b_im': 'grad_w', 'grad_c_re': 'grad_w', 'grad_c_im': 'grad_w', 'grad_d_skip': 'grad_w', 'grad_w_glu': 'grad_w', 'grad_b_glu': 'grad_w', 'grad_g_attn_out': 'grad_w', 'grad_g_ssm_out': 'grad_w', 'grad_w_out': 'grad_w', 'grad_g_ffn': 'grad_w', 'grad_w_up': 'grad_w', 'grad_conv_w': 'grad_w', 'grad_conv_b': 'grad_w', 'grad_w_down': 'grad_w', 'delta_g_mix': 'delta_w', 'delta_w_in': 'delta_w', 'delta_b_f': 'delta_w', 'delta_g_q': 'delta_w', 'delta_g_k': 'delta_w', 'delta_lambda_re': 'delta_w', 'delta_lambda_im': 'delta_w', 'delta_log_step': 'delta_w', 'delta_b_re': 'delta_w', 'delta_b_im': 'delta_w', 'delta_c_re': 'delta_w', 'delta_c_im': 'delta_w', 'delta_d_skip': 'delta_w', 'delta_w_glu': 'delta_w', 'delta_b_glu': 'delta_w', 'delta_g_attn_out': 'delta_w', 'delta_g_ssm_out': 'delta_w', 'delta_w_out': 'delta_w', 'delta_g_ffn': 'delta_w', 'delta_w_up': 'delta_w', 'delta_conv_w': 'delta_w', 'delta_conv_b': 'delta_w', 'delta_w_down': 'delta_w', 'new_m_g_mix': 'new_m', 'new_m_w_in': 'new_m', 'new_m_b_f': 'new_m', 'new_m_g_q': 'new_m', 'new_m_g_k': 'new_m', 'new_m_lambda_re': 'new_m', 'new_m_lambda_im': 'new_m', 'new_m_log_step': 'new_m', 'new_m_b_re': 'new_m', 'new_m_b_im': 'new_m', 'new_m_c_re': 'new_m', 'new_m_c_im': 'new_m', 'new_m_d_skip': 'new_m', 'new_m_w_glu': 'new_m', 'new_m_b_glu': 'new_m', 'new_m_g_attn_out': 'new_m', 'new_m_g_ssm_out': 'new_m', 'new_m_w_out': 'new_m', 'new_m_g_ffn': 'new_m', 'new_m_w_up': 'new_m', 'new_m_conv_w': 'new_m', 'new_m_conv_b': 'new_m', 'new_m_w_down': 'new_m', 'new_v_g_mix': 'new_v', 'new_v_w_in': 'new_v', 'new_v_b_f': 'new_v', 'new_v_g_q': 'new_v', 'new_v_g_k': 'new_v', 'new_v_lambda_re': 'new_v', 'new_v_lambda_im': 'new_v', 'new_v_log_step': 'new_v', 'new_v_b_re': 'new_v', 'new_v_b_im': 'new_v', 'new_v_c_re': 'new_v', 'new_v_c_im': 'new_v', 'new_v_d_skip': 'new_v', 'new_v_w_glu': 'new_v', 'new_v_b_glu': 'new_v', 'new_v_g_attn_out': 'new_v', 'new_v_g_ssm_out': 'new_v', 'new_v_w_out': 'new_v', 'new_v_g_ffn': 'new_v', 'new_v_w_up': 'new_v', 'new_v_conv_w': 'new_v', 'new_v_conv_b': 'new_v', 'new_v_w_down': 'new_v'}


def _forward(args):
    return _fwd_reference(*[args[k] for k in FWD_PARAMS])


def _output_shape():
    out = _jax.eval_shape(lambda: _forward(_fwd_setup_inputs(0)))
    return out.shape, out.dtype

N_MICROBATCH = 1
ADAM_LR = 0.001
ADAM_B1 = 0.9
ADAM_B2 = 0.999
ADAM_EPS = 1e-08
ADAM_WD = 0.01
ADAM_STEP = 10
PER_EXAMPLE_BATCH_AXIS = {'x': 0, 'loss_target': 0}
SHARED_INPUTS = []
_WEIGHT_DTYPES = {'g_mix': _jnp.float32, 'w_in': _jnp.float32, 'b_f': _jnp.float32, 'g_q': _jnp.float32, 'g_k': _jnp.float32, 'lambda_re': _jnp.float32, 'lambda_im': _jnp.float32, 'log_step': _jnp.float32, 'b_re': _jnp.float32, 'b_im': _jnp.float32, 'c_re': _jnp.float32, 'c_im': _jnp.float32, 'd_skip': _jnp.float32, 'w_glu': _jnp.float32, 'b_glu': _jnp.float32, 'g_attn_out': _jnp.float32, 'g_ssm_out': _jnp.float32, 'w_out': _jnp.float32, 'g_ffn': _jnp.float32, 'w_up': _jnp.float32, 'conv_w': _jnp.float32, 'conv_b': _jnp.float32, 'w_down': _jnp.float32}
MOMENT_SCALE = {'g_mix': 5.314287e-01, 'w_in': 3.789597e-01, 'b_f': 2.268201e+00, 'g_q': 7.699165e-01, 'g_k': 7.687246e-01, 'lambda_re': 1.659759e-02, 'lambda_im': 1.844282e-02, 'log_step': 7.947585e+00, 'b_re': 1.196990e-02, 'b_im': 1.199946e-02, 'c_re': 2.269420e-02, 'c_im': 2.527660e-02, 'd_skip': 3.895509e+00, 'w_glu': 5.525169e-01, 'b_glu': 1.633712e+00, 'g_attn_out': 1.610327e+01, 'g_ssm_out': 2.945813e+01, 'w_out': 2.580725e+00, 'g_ffn': 1.417842e+01, 'w_up': 7.079486e-01, 'conv_w': 2.299150e+00, 'conv_b': 2.101681e+00, 'w_down': 3.945840e-01}


def _to_microbatches(a, axis):
    t = _jnp.moveaxis(a, axis, 0)
    t = t.reshape((N_MICROBATCH, t.shape[0] // N_MICROBATCH) + t.shape[1:])
    return _jnp.moveaxis(t, 1, axis + 1)


def setup_inputs(seed: int = 0) -> dict:
    inp = _fwd_setup_inputs(seed)
    key = _jax.random.fold_in(_jax.random.key(seed), 7919)
    shape, _ = _output_shape()
    out = dict(inp)
    out["loss_target"] = _jax.random.normal(_jax.random.fold_in(key, 0), shape, _jnp.float32)
    for i, name in enumerate(TWIN_WEIGHTS):
        w = inp[name].astype(_jnp.float32)
        if MOMENT_SCALE is None:
            s = _jnp.sqrt(_jnp.mean(_jnp.square(w)) + 1e-30)
        else:
            s = MOMENT_SCALE[name]
        km, kv = _jax.random.split(_jax.random.fold_in(key, i + 1))
        out[name] = w
        out["m_" + name] = s * _jax.random.normal(km, w.shape, _jnp.float32)
        out["v_" + name] = (s * s) * _jax.random.uniform(kv, w.shape, _jnp.float32, 0.5, 1.5)
    if N_MICROBATCH > 1:
        for name, axis in PER_EXAMPLE_BATCH_AXIS.items():
            out[name] = _to_microbatches(out[name], axis)
    return {'x': out['x'], 'g_mix': out['g_mix'], 'w_in': out['w_in'], 'b_f': out['b_f'], 'g_q': out['g_q'], 'g_k': out['g_k'], 'lambda_re': out['lambda_re'], 'lambda_im': out['lambda_im'], 'log_step': out['log_step'], 'b_re': out['b_re'], 'b_im': out['b_im'], 'c_re': out['c_re'], 'c_im': out['c_im'], 'd_skip': out['d_skip'], 'w_glu': out['w_glu'], 'b_glu': out['b_glu'], 'g_attn_out': out['g_attn_out'], 'g_ssm_out': out['g_ssm_out'], 'w_out': out['w_out'], 'g_ffn': out['g_ffn'], 'w_up': out['w_up'], 'conv_w': out['conv_w'], 'conv_b': out['conv_b'], 'w_down': out['w_down'], 'loss_target': out['loss_target'], 'm_g_mix': out['m_g_mix'], 'm_w_in': out['m_w_in'], 'm_b_f': out['m_b_f'], 'm_g_q': out['m_g_q'], 'm_g_k': out['m_g_k'], 'm_lambda_re': out['m_lambda_re'], 'm_lambda_im': out['m_lambda_im'], 'm_log_step': out['m_log_step'], 'm_b_re': out['m_b_re'], 'm_b_im': out['m_b_im'], 'm_c_re': out['m_c_re'], 'm_c_im': out['m_c_im'], 'm_d_skip': out['m_d_skip'], 'm_w_glu': out['m_w_glu'], 'm_b_glu': out['m_b_glu'], 'm_g_attn_out': out['m_g_attn_out'], 'm_g_ssm_out': out['m_g_ssm_out'], 'm_w_out': out['m_w_out'], 'm_g_ffn': out['m_g_ffn'], 'm_w_up': out['m_w_up'], 'm_conv_w': out['m_conv_w'], 'm_conv_b': out['m_conv_b'], 'm_w_down': out['m_w_down'], 'v_g_mix': out['v_g_mix'], 'v_w_in': out['v_w_in'], 'v_b_f': out['v_b_f'], 'v_g_q': out['v_g_q'], 'v_g_k': out['v_g_k'], 'v_lambda_re': out['v_lambda_re'], 'v_lambda_im': out['v_lambda_im'], 'v_log_step': out['v_log_step'], 'v_b_re': out['v_b_re'], 'v_b_im': out['v_b_im'], 'v_c_re': out['v_c_re'], 'v_c_im': out['v_c_im'], 'v_d_skip': out['v_d_skip'], 'v_w_glu': out['v_w_glu'], 'v_b_glu': out['v_b_glu'], 'v_g_attn_out': out['v_g_attn_out'], 'v_g_ssm_out': out['v_g_ssm_out'], 'v_w_out': out['v_w_out'], 'v_g_ffn': out['v_g_ffn'], 'v_w_up': out['v_w_up'], 'v_conv_w': out['v_conv_w'], 'v_conv_b': out['v_conv_b'], 'v_w_down': out['v_w_down']}


def _loss(weights, diff, rest, loss_target):
    with _jax.named_scope("forward"):
        args = {**rest, TWIN_DIFF_INPUT: diff, **{k: w.astype(_WEIGHT_DTYPES[k]) for k, w in weights.items()}}
        y = _forward(args)
    with _jax.named_scope("loss_head"):
        err = _jnp.square(y.astype(_jnp.float32) - loss_target)
        return 0.5 * _jnp.sum(_jnp.mean(err, axis=-1)) if err.ndim else 0.5 * err


def _adamw(w, g, m, v):
    m = ADAM_B1 * m + (1.0 - ADAM_B1) * g
    v = ADAM_B2 * v + (1.0 - ADAM_B2) * _jnp.square(g)
    m_hat = m / (1.0 - ADAM_B1 ** ADAM_STEP)
    v_hat = v / (1.0 - ADAM_B2 ** ADAM_STEP)
    delta = -ADAM_LR * (m_hat / (_jnp.sqrt(v_hat) + ADAM_EPS) + ADAM_WD * w)
    return delta, m, v


def reference(x, g_mix, w_in, b_f, g_q, g_k, lambda_re, lambda_im, log_step, b_re, b_im, c_re, c_im, d_skip, w_glu, b_glu, g_attn_out, g_ssm_out, w_out, g_ffn, w_up, conv_w, conv_b, w_down, loss_target, m_g_mix, m_w_in, m_b_f, m_g_q, m_g_k, m_lambda_re, m_lambda_im, m_log_step, m_b_re, m_b_im, m_c_re, m_c_im, m_d_skip, m_w_glu, m_b_glu, m_g_attn_out, m_g_ssm_out, m_w_out, m_g_ffn, m_w_up, m_conv_w, m_conv_b, m_w_down, v_g_mix, v_w_in, v_b_f, v_g_q, v_g_k, v_lambda_re, v_lambda_im, v_log_step, v_b_re, v_b_im, v_c_re, v_c_im, v_d_skip, v_w_glu, v_b_glu, v_g_attn_out, v_g_ssm_out, v_w_out, v_g_ffn, v_w_up, v_conv_w, v_conv_b, v_w_down):
    given = dict(x=x, g_mix=g_mix, w_in=w_in, b_f=b_f, g_q=g_q, g_k=g_k, lambda_re=lambda_re, lambda_im=lambda_im, log_step=log_step, b_re=b_re, b_im=b_im, c_re=c_re, c_im=c_im, d_skip=d_skip, w_glu=w_glu, b_glu=b_glu, g_attn_out=g_attn_out, g_ssm_out=g_ssm_out, w_out=w_out, g_ffn=g_ffn, w_up=w_up, conv_w=conv_w, conv_b=conv_b, w_down=w_down, loss_target=loss_target, m_g_mix=m_g_mix, m_w_in=m_w_in, m_b_f=m_b_f, m_g_q=m_g_q, m_g_k=m_g_k, m_lambda_re=m_lambda_re, m_lambda_im=m_lambda_im, m_log_step=m_log_step, m_b_re=m_b_re, m_b_im=m_b_im, m_c_re=m_c_re, m_c_im=m_c_im, m_d_skip=m_d_skip, m_w_glu=m_w_glu, m_b_glu=m_b_glu, m_g_attn_out=m_g_attn_out, m_g_ssm_out=m_g_ssm_out, m_w_out=m_w_out, m_g_ffn=m_g_ffn, m_w_up=m_w_up, m_conv_w=m_conv_w, m_conv_b=m_conv_b, m_w_down=m_w_down, v_g_mix=v_g_mix, v_w_in=v_w_in, v_b_f=v_b_f, v_g_q=v_g_q, v_g_k=v_g_k, v_lambda_re=v_lambda_re, v_lambda_im=v_lambda_im, v_log_step=v_log_step, v_b_re=v_b_re, v_b_im=v_b_im, v_c_re=v_c_re, v_c_im=v_c_im, v_d_skip=v_d_skip, v_w_glu=v_w_glu, v_b_glu=v_b_glu, v_g_attn_out=v_g_attn_out, v_g_ssm_out=v_g_ssm_out, v_w_out=v_w_out, v_g_ffn=v_g_ffn, v_w_up=v_w_up, v_conv_w=v_conv_w, v_conv_b=v_conv_b, v_w_down=v_w_down)
    weights = {n: given[n] for n in TWIN_WEIGHTS}
    shared = {n: given[n] for n in SHARED_INPUTS}
    per_example = {n: given[n] for n in ['x']}
    grad_fn = _jax.value_and_grad(_loss, argnums=(0, 1))

    def one_microbatch(ex, loss_target):
        ex = dict(ex)
        diff = ex.pop(TWIN_DIFF_INPUT)
        return grad_fn(weights, diff, {**shared, **ex}, loss_target)

    if N_MICROBATCH == 1:
        loss, (grad_w, grad_x) = one_microbatch(per_example, given["loss_target"])
    else:
        def body(carry, xs):
            loss_sum, grad_sum = carry
            l_k, (gw_k, gx_k) = one_microbatch(xs[0], xs[1])
            with _jax.named_scope("update"):
                return (loss_sum + l_k, _jax.tree.map(_jnp.add, grad_sum, gw_k)), gx_k

        init = (_jnp.zeros((), _jnp.float32), _jax.tree.map(_jnp.zeros_like, weights))
        (loss, grad_w), grad_x = _jax.lax.scan(body, init, (per_example, given["loss_target"]))
    with _jax.named_scope("update"):
        delta_w, new_m, new_v = {}, {}, {}
        for n in TWIN_WEIGHTS:
            delta_w[n], new_m[n], new_v[n] = _adamw(weights[n], grad_w[n], given["m_" + n], given["v_" + n])
    return (loss, grad_x, *[grad_w[n] for n in TWIN_WEIGHTS], *[delta_w[n] for n in TWIN_WEIGHTS],
            *[new_m[n] for n in TWIN_WEIGHTS], *[new_v[n] for n in TWIN_WEIGHTS])
```

```python
import functools

import jax
import jax.numpy as jnp
from jax import lax
from jax.experimental import pallas as pl
from jax.experimental.pallas import tpu as pltpu

F32, BF16 = jnp.float32, jnp.bfloat16
T, D = 2048, 1024
NH, DH = 8, 64
AW, SW = 512, 512
NG, GS, NP = 32, 16, 64
DFF = 2816
NDEV = 8
EPS = 1e-6
NEG = -1e30
ZC = 4 * 512 + 128
SEG = 256
BQ = 256
VMEM_LIMIT = 56 * 1024 * 1024
LANES = 128
MESH = pl.DeviceIdType.MESH
HI = lax.Precision.HIGHEST

ADAM_LR, ADAM_B1, ADAM_B2, ADAM_EPS, ADAM_WD, ADAM_STEP = 0.001, 0.9, 0.999, 1e-08, 0.01, 10


def _cp(dims=None):
    if dims is None:
        return pltpu.CompilerParams(vmem_limit_bytes=VMEM_LIMIT)
    return pltpu.CompilerParams(dimension_semantics=dims, vmem_limit_bytes=VMEM_LIMIT)


def _sds(shape, dtype=F32):
    return jax.ShapeDtypeStruct(shape, dtype)


def _nt(a, b):
    return lax.dot_general(a, b, (((1,), (1,)), ((), ())), preferred_element_type=F32)


def _tn(a, b):
    return lax.dot_general(a, b, (((0,), (0,)), ((), ())), preferred_element_type=F32)


def _nn(a, b):
    return jnp.dot(a, b, preferred_element_type=F32)


def _rms_r(x):
    return lax.rsqrt(jnp.mean(x * x, axis=-1, keepdims=True) + EPS)


def _rms_bwd(n, r, g, dh):
    dn = dh * g
    dx = r * (dn - n * jnp.mean(dn * n, axis=-1, keepdims=True))
    return dx, jnp.sum(dh * n, axis=0, keepdims=True)


def _all_gather(shard, name):
    rows, lanes = shard.shape

    def body(x_ref, out_ref, send_sems, recv_sems, local_sem):
        x, y, c = lax.axis_index("x"), lax.axis_index("y"), lax.axis_index("c")
        me, sibling = (x, y, c), (x, y, 1 - c)
        chips = [(1 - x, y), (x, 1 - y), (1 - x, 1 - y)]

        def slot(px, py, pc):
            return out_ref.at[4 * px + 2 * py + pc]

        def copy(k, block, to, src=None):
            return pltpu.make_async_remote_copy(
                src_ref=slot(*block) if src is None else src, dst_ref=slot(*block),
                send_sem=send_sems.at[k], recv_sem=recv_sems.at[k], device_id=to, device_id_type=MESH)

        mine = pltpu.make_async_copy(x_ref, slot(*me), local_sem)
        mine.start()
        first = [copy(0, me, sibling, src=x_ref)]
        first += [copy(1 + j, me, (*chip, c), src=x_ref) for j, chip in enumerate(chips)]
        for cp in first:
            cp.start()
        passed = [copy(4 + j, (*chip, c), sibling) for j, chip in enumerate(chips)]
        for j, chip in enumerate(chips):
            copy(1 + j, (*chip, c), me).wait_recv()
            passed[j].start()
        copy(0, sibling, me).wait_recv()
        for j, chip in enumerate(chips):
            copy(4 + j, (*chip, 1 - c), me).wait_recv()
        for cp in first + passed:
            cp.wait_send()
        mine.wait()

    return pl.pallas_call(
        body, name=name,
        out_shape=_sds((NDEV, rows, lanes), shard.dtype),
        in_specs=[pl.BlockSpec(memory_space=pl.ANY)],
        out_specs=pl.BlockSpec(memory_space=pl.ANY),
        scratch_shapes=[pltpu.SemaphoreType.DMA((7,)), pltpu.SemaphoreType.DMA((7,)), pltpu.SemaphoreType.DMA],
    )(shard)


def _swap_sibling(g2, name):
    _, nchip, rows, lanes = g2.shape

    def body(g_ref, out_ref, send_sem, recv_sem):
        x, y, c = lax.axis_index("x"), lax.axis_index("y"), lax.axis_index("c")
        cp = pltpu.make_async_remote_copy(
            src_ref=g_ref.at[1 - c], dst_ref=out_ref, send_sem=send_sem, recv_sem=recv_sem,
            device_id=(x, y, 1 - c), device_id_type=MESH)
        cp.start()
        cp.wait()

    return pl.pallas_call(
        body, name=name,
        out_shape=_sds((nchip, rows, lanes), g2.dtype),
        in_specs=[pl.BlockSpec(memory_space=pl.ANY)],
        out_specs=pl.BlockSpec(memory_space=pl.ANY),
        scratch_shapes=[pltpu.SemaphoreType.DMA, pltpu.SemaphoreType.DMA],
    )(g2)


def _swap_chips(p, name):
    _, rows, lanes = p.shape

    def body(p_ref, out_ref, send_sems, recv_sems):
        x, y, c = lax.axis_index("x"), lax.axis_index("y"), lax.axis_index("c")
        chips = [(1 - x, y), (x, 1 - y), (1 - x, 1 - y)]
        cps = []
        for j, (px, py) in enumerate(chips):
            cps.append(pltpu.make_async_remote_copy(
                src_ref=p_ref.at[2 * px + py], dst_ref=out_ref.at[j],
                send_sem=send_sems.at[j], recv_sem=recv_sems.at[j],
                device_id=(px, py, c), device_id_type=MESH))
        for cp in cps:
            cp.start()
        for cp in cps:
            cp.wait()

    return pl.pallas_call(
        body, name=name,
        out_shape=_sds((3, rows, lanes), p.dtype),
        in_specs=[pl.BlockSpec(memory_space=pl.ANY)],
        out_specs=pl.BlockSpec(memory_space=pl.ANY),
        scratch_shapes=[pltpu.SemaphoreType.DMA((3,)), pltpu.SemaphoreType.DMA((3,))],
    )(p)


def _mm(a, b, mode, tm, tn, name, out_dtype=F32):
    if mode == "nn":
        (m, k), n = a.shape, b.shape[1]
        a_spec = pl.BlockSpec((tm, k), lambda i, j: (i, 0))
        b_spec = pl.BlockSpec((k, tn), lambda i, j: (0, j))
        op = _nn
    elif mode == "nt":
        (m, k), n = a.shape, b.shape[0]
        a_spec = pl.BlockSpec((tm, k), lambda i, j: (i, 0))
        b_spec = pl.BlockSpec((tn, k), lambda i, j: (j, 0))
        op = _nt
    else:
        (k, m), n = a.shape, b.shape[1]
        a_spec = pl.BlockSpec((k, tm), lambda i, j: (0, i))
        b_spec = pl.BlockSpec((k, tn), lambda i, j: (0, j))
        op = _tn
    assert m % tm == 0 and n % tn == 0, (name, m, n, tm, tn)

    def body(a_ref, b_ref, o_ref):
        o_ref[...] = op(a_ref[...].astype(BF16), b_ref[...].astype(BF16)).astype(out_dtype)

    return pl.pallas_call(
        body, name=name, grid=(m // tm, n // tn),
        in_specs=[a_spec, b_spec], out_specs=pl.BlockSpec((tm, tn), lambda i, j: (i, j)),
        out_shape=_sds((m, n), out_dtype), compiler_params=_cp(("parallel", "parallel")),
    )(a, b)


def _norm_proj(x, g, w, tm, tn, name):
    n = w.shape[1]
    assert T % tm == 0 and n % tn == 0

    def body(x_ref, g_ref, w_ref, h_ref, z_ref):
        xv = x_ref[...]
        h = (xv * _rms_r(xv) * g_ref[...]).astype(BF16)
        h_ref[...] = h
        z_ref[...] = _nn(h, w_ref[...])

    return pl.pallas_call(
        body, name=name, grid=(T // tm, n // tn),
        in_specs=[pl.BlockSpec((tm, D), lambda i, j: (i, 0)), pl.BlockSpec((1, D), lambda i, j: (0, 0)),
                  pl.BlockSpec((D, tn), lambda i, j: (0, j))],
        out_specs=[pl.BlockSpec((tm, D), lambda i, j: (i, 0)), pl.BlockSpec((tm, tn), lambda i, j: (i, j))],
        out_shape=[_sds((T, D), BF16), _sds((T, n))],
        compiler_params=_cp(("parallel", "arbitrary")),
    )(x, g, w)


def _norm_proj_bwd(dz, w, x, g, dres, tm, name):
    k = dz.shape[1]

    def body(dz_ref, w_ref, x_ref, g_ref, dres_ref, dx_ref, dg_ref):
        dh = _nt(dz_ref[...], w_ref[...])
        xv = x_ref[...]
        r = _rms_r(xv)
        dxr, dgp = _rms_bwd(xv * r, r, g_ref[...], dh)
        dx_ref[...] = dres_ref[...] + dxr

        @pl.when(pl.program_id(0) == 0)
        def _():
            dg_ref[...] = jnp.zeros_like(dg_ref)
        dg_ref[...] += dgp

    row = lambda i: (i, 0)
    fixed = lambda i: (0, 0)
    return pl.pallas_call(
        body, name=name, grid=(T // tm,),
        in_specs=[pl.BlockSpec((tm, k), row), pl.BlockSpec((D, k), fixed), pl.BlockSpec((tm, D), row),
                  pl.BlockSpec((1, D), fixed), pl.BlockSpec((tm, D), row)],
        out_specs=[pl.BlockSpec((tm, D), row), pl.BlockSpec((1, D), fixed)],
        out_shape=[_sds((T, D)), _sds((1, D))],
        compiler_params=_cp(("arbitrary",)),
    )(dz, w, x, g, dres)


def _stack_lanes(v):
    return jnp.concatenate([v[:, LANES * b:LANES * (b + 1)] for b in range(T // LANES)], axis=0)


def _unstack_lanes(s):
    return jnp.concatenate([s[8 * b:8 * b + 8, :] for b in range(T // LANES)], axis=1)


def _cumsum_lanes(v, reverse):
    st = _stack_lanes(v)
    ri = lax.broadcasted_iota(jnp.int32, (LANES, LANES), 0)
    ci = lax.broadcasted_iota(jnp.int32, (LANES, LANES), 1)
    tri = (ri >= ci) if reverse else (ri <= ci)
    intra = jnp.dot(st, tri.astype(F32), precision=HI, preferred_element_type=F32)
    tot = intra[:, 0:1] if reverse else intra[:, LANES - 1:LANES]
    same_head = (ci % 8) == (ri % 8)
    other = (ci // 8 > ri // 8) if reverse else (ci // 8 < ri // 8)
    off = jnp.dot((same_head & other).astype(F32), jnp.broadcast_to(tot, (LANES, LANES)), precision=HI,
                  preferred_element_type=F32)
    return _unstack_lanes(intra + off)


def _gates_fwd(f_t, b_f):
    def body(f_ref, b_ref, c_ref):
        z = f_ref[...] + b_ref[...]
        lf = jnp.minimum(z, 0.0) - jnp.log(1.0 + jnp.exp(-jnp.abs(z)))
        c_ref[...] = _cumsum_lanes(lf, reverse=False)

    return pl.pallas_call(body, name="gates_fwd", out_shape=_sds((NH, T)), compiler_params=_cp())(f_t, b_f)


def _gates_bwd(dck, f_t, b_f):
    def body(d_ref, f_ref, b_ref, df_ref, db_ref):
        z = f_ref[...] + b_ref[...]
        dlf = _cumsum_lanes(d_ref[...], reverse=True)
        df = dlf * jax.nn.sigmoid(-z)
        df_ref[...] = df
        db_ref[...] = jnp.sum(df, axis=1, keepdims=True)

    return pl.pallas_call(body, name="gates_bwd", out_shape=[_sds((NH, T)), _sds((NH, 1))],
                          compiler_params=_cp())(dck, f_t, b_f)


def _causal(s, row0, ncol):
    rows = row0 + lax.broadcasted_iota(jnp.int32, (BQ, ncol), 0)
    cols = lax.broadcasted_iota(jnp.int32, (BQ, ncol), 1)
    return jnp.where(cols <= rows, s, NEG)


def _attn_fwd(q, k, v, cq, ck, g_q, g_k):
    scale = DH ** -0.5

    def body(q_ref, k_ref, v_ref, cq_ref, ck_ref, gq_ref, gk_ref, o_ref, lse_ref):
        qv, kv = q_ref[0], k_ref[0]
        qb = (qv * _rms_r(qv) * gq_ref[...]).astype(BF16)
        kb = (kv * _rms_r(kv) * gk_ref[...]).astype(BF16)
        vb = v_ref[0].astype(BF16)
        for i in range(T // BQ):
            lo, hi = i * BQ, (i + 1) * BQ
            s = _nt(qb[lo:hi], kb[:hi]) * scale + cq_ref[0, lo:hi, :] - ck_ref[0, :, :hi]
            s = _causal(s, lo, hi)
            m = jnp.max(s, axis=-1, keepdims=True)
            p = jnp.exp(s - m)
            l = jnp.sum(p, axis=-1, keepdims=True)
            o_ref[0, lo:hi, :] = _nn(p.astype(BF16), vb[:hi]) / l
            lse_ref[0, lo:hi, :] = m + jnp.log(l)

    head = lambda h: (h, 0, 0)
    fixed = lambda h: (0, 0)
    return pl.pallas_call(
        body, name="attn_fwd", grid=(NH,),
        in_specs=[pl.BlockSpec((1, T, DH), head)] * 3 + [pl.BlockSpec((1, T, 1), head), pl.BlockSpec((1, 1, T), head),
                                                         pl.BlockSpec((1, DH), fixed), pl.BlockSpec((1, DH), fixed)],
        out_specs=[pl.BlockSpec((1, T, DH), head), pl.BlockSpec((1, T, 1), head)],
        out_shape=[_sds((NH, T, DH)), _sds((NH, T, 1))],
        compiler_params=_cp(("parallel",)),
    )(q, k, v, cq, ck, g_q, g_k)


def _attn_bwd(q, k, v, cq, ck, g_q, g_k, lse, do):
    scale = DH ** -0.5

    def body(q_ref, k_ref, v_ref, cq_ref, ck_ref, gq_ref, gk_ref, lse_ref, do_ref,
             dq_ref, dk_ref, dv_ref, dck_ref, dgq_ref, dgk_ref, dkn_acc, dv_acc, dc_acc):
        qv, kv = q_ref[0], k_ref[0]
        rq, rk = _rms_r(qv), _rms_r(kv)
        nq, nk = qv * rq, kv * rk
        qb = (nq * gq_ref[...]).astype(BF16)
        kb = (nk * gk_ref[...]).astype(BF16)
        vb = v_ref[0].astype(BF16)
        dob = do_ref[0].astype(BF16)
        dkn_acc[...] = jnp.zeros_like(dkn_acc)
        dv_acc[...] = jnp.zeros_like(dv_acc)
        dc_acc[...] = jnp.zeros_like(dc_acc)
        dgq = jnp.zeros((1, DH), F32)
        for i in range(T // BQ):
            lo, hi = i * BQ, (i + 1) * BQ
            s = _nt(qb[lo:hi], kb[:hi]) * scale + cq_ref[0, lo:hi, :] - ck_ref[0, :, :hi]
            s = _causal(s, lo, hi)
            p = jnp.exp(s - lse_ref[0, lo:hi, :])
            dp = _nt(dob[lo:hi], vb[:hi])
            ds = p * (dp - jnp.sum(p * dp, axis=-1, keepdims=True))
            dsb = ds.astype(BF16)
            dv_acc[0:hi, :] += _tn(p.astype(BF16), dob[lo:hi])
            dkn_acc[0:hi, :] += _tn(dsb, qb[lo:hi]) * scale
            dc_acc[:, 0:hi] -= jnp.sum(ds, axis=0, keepdims=True)
            dqn = _nn(dsb, kb[:hi]) * scale
            dqx, dgp = _rms_bwd(nq[lo:hi], rq[lo:hi], gq_ref[...], dqn)
            dq_ref[0, lo:hi, :] = dqx
            dgq = dgq + dgp
        dkx, dgk = _rms_bwd(nk, rk, gk_ref[...], dkn_acc[...])
        dk_ref[0] = dkx
        dv_ref[0] = dv_acc[...]
        dck_ref[0] = dc_acc[...]

        @pl.when(pl.program_id(0) == 0)
        def _():
            dgq_ref[...] = jnp.zeros_like(dgq_ref)
            dgk_ref[...] = jnp.zeros_like(dgk_ref)
        dgq_ref[...] += dgq
        dgk_ref[...] += dgk

    head = lambda h: (h, 0, 0)
    fixed = lambda h: (0, 0)
    thd = pl.BlockSpec((1, T, DH), head)
    col = pl.BlockSpec((1, T, 1), head)
    rowv = pl.BlockSpec((1, 1, T), head)
    gsp = pl.BlockSpec((1, DH), fixed)
    return pl.pallas_call(
        body, name="attn_bwd", grid=(NH,),
        in_specs=[thd, thd, thd, col, rowv, gsp, gsp, col, thd],
        out_specs=[thd, thd, thd, rowv, gsp, gsp],
        out_shape=[_sds((NH, T, DH))] * 3 + [_sds((NH, 1, T)), _sds((1, DH)), _sds((1, DH))],
        scratch_shapes=[pltpu.VMEM((T, DH), F32), pltpu.VMEM((T, DH), F32), pltpu.VMEM((1, T), F32)],
        compiler_params=_cp(("arbitrary",)),
    )(q, k, v, cq, ck, g_q, g_k, lse, do)


def _disc(lr, li, ls, br_t, bi_t):
    step = jnp.exp(ls)
    er = jnp.exp(lr * step)
    ab_re = er * jnp.cos(li * step)
    ab_im = er * jnp.sin(li * step)
    num_re = ab_re - 1.0
    num_im = ab_im
    den = lr * lr + li * li
    f_re = (num_re * lr + num_im * li) / den
    f_im = (num_im * lr - num_re * li) / den
    bb_re = f_re[:, None, :] * br_t - f_im[:, None, :] * bi_t
    bb_im = f_re[:, None, :] * bi_t + f_im[:, None, :] * br_t
    return ab_re, ab_im, bb_re, bb_im


def _disc_fwd(lr, li, ls, br_t, bi_t):
    def body(lr_ref, li_ref, ls_ref, br_ref, bi_ref, ar_ref, ai_ref, bbr_ref, bbi_ref):
        ar, ai, bbr, bbi = _disc(lr_ref[...], li_ref[...], ls_ref[...], br_ref[...], bi_ref[...])
        ar_ref[...] = ar
        ai_ref[...] = ai
        bbr_ref[...] = bbr
        bbi_ref[...] = bbi

    return pl.pallas_call(
        body, name="ssm_disc_fwd",
        out_shape=[_sds((NG, NP)), _sds((NG, NP)), _sds((NG, GS, NP)), _sds((NG, GS, NP))],
        compiler_params=_cp())(lr, li, ls, br_t, bi_t)


def _disc_bwd(lr, li, ls, br_t, bi_t, dar, dai, dbbr, dbbi):
    def body(lr_ref, li_ref, ls_ref, br_ref, bi_ref, dar_ref, dai_ref, dbbr_ref, dbbi_ref,
             dlr_ref, dli_ref, dls_ref, dbr_ref, dbi_ref):
        _, vjp = jax.vjp(_disc, lr_ref[...], li_ref[...], ls_ref[...], br_ref[...], bi_ref[...])
        dlr, dli, dls, dbr, dbi = vjp((dar_ref[...], dai_ref[...], dbbr_ref[...], dbbi_ref[...]))
        dlr_ref[...] = dlr
        dli_ref[...] = dli
        dls_ref[...] = dls
        dbr_ref[...] = dbr
        dbi_ref[...] = dbi

    return pl.pallas_call(
        body, name="ssm_disc_bwd",
        out_shape=[_sds((NG, NP)), _sds((NG, NP)), _sds((NG, 1)), _sds((NG, GS, NP)), _sds((NG, GS, NP))],
        compiler_params=_cp())(lr, li, ls, br_t, bi_t, dar, dai, dbbr, dbbi)


def _cmul(ar, ai, br, bi):
    return ar * br - ai * bi, ar * bi + ai * br


def _scan(buf, a_re, a_im, reverse, other=None):
    nseg = T // SEG
    ar = [jnp.broadcast_to(a_re[:, LANES * k:LANES * (k + 1)], (nseg, LANES)) for k in range(4)]
    ai = [jnp.broadcast_to(a_im[:, LANES * k:LANES * (k + 1)], (nseg, LANES)) for k in range(4)]

    def rows(i):
        return pl.ds(i, nseg, stride=SEG)

    def advance(i, xr, xi):
        nr = [ar[k] * xr[k] - ai[k] * xi[k] + buf.at[k][rows(i), :] for k in range(4)]
        ni = [ar[k] * xi[k] + ai[k] * xr[k] + buf.at[4 + k][rows(i), :] for k in range(4)]
        return nr, ni

    def index(i):
        return (SEG - 1 - i) if reverse else i

    zeros = tuple(jnp.zeros((nseg, LANES), F32) for _ in range(4))

    def sweep1(i, carry):
        nr, ni = advance(index(i), carry[0], carry[1])
        return tuple(nr), tuple(ni)

    er, ei = lax.fori_loop(0, SEG, sweep1, (zeros, zeros))

    row = lax.broadcasted_iota(jnp.int32, (nseg, LANES), 0)
    sr, si = [], []
    for k in range(4):
        pr, pi = ar[k], ai[k]
        for _ in range(SEG.bit_length() - 1):
            pr, pi = _cmul(pr, pi, pr, pi)
        ir, ii = er[k], ei[k]
        for d in (1, 2, 4):
            shift = (nseg - d) if reverse else d
            ok = (row < nseg - d) if reverse else (row >= d)
            mr, mi = _cmul(pr, pi, pltpu.roll(ir, shift, 0), pltpu.roll(ii, shift, 0))
            ir = ir + jnp.where(ok, mr, 0.0)
            ii = ii + jnp.where(ok, mi, 0.0)
            pr, pi = _cmul(pr, pi, pr, pi)
        shift = (nseg - 1) if reverse else 1
        ok = (row < nseg - 1) if reverse else (row >= 1)
        sr.append(jnp.where(ok, pltpu.roll(ir, shift, 0), 0.0))
        si.append(jnp.where(ok, pltpu.roll(ii, shift, 0), 0.0))

    def sweep2(i, carry):
        xr, xi, dar, dai = carry
        idx = index(i)
        if other is not None:
            orr = [other.at[k][rows(idx), :] for k in range(4)]
            oi = [other.at[4 + k][rows(idx), :] for k in range(4)]
            dar = tuple(dar[k] + xr[k] * orr[k] + xi[k] * oi[k] for k in range(4))
            dai = tuple(dai[k] + xi[k] * orr[k] - xr[k] * oi[k] for k in range(4))
        nr, ni = advance(idx, xr, xi)
        for k in range(4):
            buf.at[k][rows(idx), :] = nr[k]
            buf.at[4 + k][rows(idx), :] = ni[k]
        return tuple(nr), tuple(ni), dar, dai

    _, _, dar, dai = lax.fori_loop(0, SEG, sweep2, (tuple(sr), tuple(si), zeros, zeros))
    if other is None:
        return None
    da_re = jnp.concatenate([jnp.sum(d, axis=0, keepdims=True) for d in dar], axis=1)
    da_im = jnp.concatenate([jnp.sum(d, axis=0, keepdims=True) for d in dai], axis=1)
    return da_re, da_im


def _spread(val, buf):
    for j in range(8):
        buf[j] = val[:, LANES * j:LANES * (j + 1)]


def _gathered(buf):
    return jnp.concatenate([buf[j] for j in range(8)], axis=1)


_SSM_BLOCKS = 4


def _ssm_fwd(zqkvu, bb, ab, cc, dsk):
    def body(u_ref, bb_ref, a_ref, cc_ref, d_ref, y_ref, xb):
        ub = u_ref[...]
        _spread(_nn(ub.astype(BF16), bb_ref[0]), xb)
        _scan(xb, a_ref[0, 0:1, :], a_ref[0, 1:2, :], reverse=False)
        y_ref[...] = _nn(_gathered(xb).astype(BF16), cc_ref[0]) + d_ref[0] * ub

    blk = lambda j: (j, 0, 0)
    return pl.pallas_call(
        body, name="ssm_fwd", grid=(_SSM_BLOCKS,),
        in_specs=[pl.BlockSpec((T, LANES), lambda j: (0, 12 + j)), pl.BlockSpec((1, LANES, 1024), blk),
                  pl.BlockSpec((1, 2, 512), blk), pl.BlockSpec((1, 1024, LANES), blk), pl.BlockSpec((1, 1, LANES), blk)],
        out_specs=pl.BlockSpec((T, LANES), lambda j: (0, j)),
        out_shape=_sds((T, SW)),
        scratch_shapes=[pltpu.VMEM((8, T, LANES), F32)],
        compiler_params=_cp(("parallel",)),
    )(zqkvu, bb, ab, cc, dsk)


def _ssm_bwd(zqkvu, dy, bb, ab, cc, dsk):
    def body(u_ref, dy_ref, bb_ref, a_ref, cc_ref, d_ref, du_ref, dbb_ref, da_ref, dcc_ref, dd_ref, xb, gb):
        ub, dyv = u_ref[...], dy_ref[...]
        ubb, dyb = ub.astype(BF16), dyv.astype(BF16)
        a_re, a_im = a_ref[0, 0:1, :], a_ref[0, 1:2, :]
        _spread(_nn(ubb, bb_ref[0]), xb)
        _scan(xb, a_re, a_im, reverse=False)
        dcc_ref[0] = _tn(_gathered(xb).astype(BF16), dyb)
        _spread(_nt(dyb, cc_ref[0]), gb)
        da_re, da_im = _scan(gb, a_re, -a_im, reverse=True, other=xb)
        da_ref[0] = jnp.concatenate([da_re, da_im], axis=0)
        gc = _gathered(gb).astype(BF16)
        du_ref[...] = _nt(gc, bb_ref[0]) + d_ref[0] * dyv
        dbb_ref[0] = _tn(ubb, gc)
        dd_ref[0] = jnp.sum(dyv * ub, axis=0, keepdims=True)

    blk = lambda j: (j, 0, 0)
    return pl.pallas_call(
        body, name="ssm_bwd", grid=(_SSM_BLOCKS,),
        in_specs=[pl.BlockSpec((T, LANES), lambda j: (0, 12 + j)), pl.BlockSpec((T, LANES), lambda j: (0, j)),
                  pl.BlockSpec((1, LANES, 1024), blk), pl.BlockSpec((1, 2, 512), blk),
                  pl.BlockSpec((1, 1024, LANES), blk), pl.BlockSpec((1, 1, LANES), blk)],
        out_specs=[pl.BlockSpec((T, LANES), lambda j: (0, j)), pl.BlockSpec((1, LANES, 1024), blk),
                   pl.BlockSpec((1, 2, 512), blk), pl.BlockSpec((1, 1024, LANES), blk), pl.BlockSpec((1, 1, LANES), blk)],
        out_shape=[_sds((T, SW)), _sds((4, LANES, 1024)), _sds((4, 2, 512)), _sds((4, 1024, LANES)), _sds((4, 1, LANES))],
        scratch_shapes=[pltpu.VMEM((8, T, LANES), F32), pltpu.VMEM((8, T, LANES), F32)],
        compiler_params=_cp(("parallel",)),
    )(zqkvu, dy, bb, ab, cc, dsk)


_TM_MIX = 256


def _mix_parts(att, y, wg, bg):
    y2 = jax.nn.gelu(y)
    y2b = y2.astype(BF16)
    sg = jax.nn.sigmoid(_nn(y2b, wg) + bg)
    ssm = y2 * sg
    ra, rs = _rms_r(att), _rms_r(ssm)
    return y2, y2b, sg, ssm, ra, rs


def _mix_fwd(att, y, x, wg, bg, ga, gs, wo):
    def body(att_ref, y_ref, x_ref, wg_ref, bg_ref, ga_ref, gs_ref, wo_ref, x1_ref):
        attv = att_ref[...]
        _, _, _, ssm, ra, rs = _mix_parts(attv, y_ref[...], wg_ref[...], bg_ref[...])
        ma = (attv * ra * ga_ref[...]).astype(BF16)
        ms = (ssm * rs * gs_ref[...]).astype(BF16)
        x1_ref[...] = x_ref[...] + _nn(ma, wo_ref[0:AW, :]) + _nn(ms, wo_ref[AW:D, :])

    row = lambda i: (i, 0)
    fixed = lambda i: (0, 0)
    half = pl.BlockSpec((_TM_MIX, AW), row)
    vec = pl.BlockSpec((1, AW), fixed)
    return pl.pallas_call(
        body, name="mix_fwd", grid=(T // _TM_MIX,),
        in_specs=[half, half, pl.BlockSpec((_TM_MIX, D), row), pl.BlockSpec((SW, SW), fixed), vec, vec, vec,
                  pl.BlockSpec((D, D), fixed)],
        out_specs=pl.BlockSpec((_TM_MIX, D), row), out_shape=_sds((T, D)),
        compiler_params=_cp(("parallel",)),
    )(att, y, x, wg, bg, ga, gs, wo)


def _mix_bwd(att, y, dx1b, wg, bg, ga, gs, wo):
    def body(att_ref, y_ref, d_ref, wg_ref, bg_ref, ga_ref, gs_ref, wo_ref,
             datt_ref, dy_ref, dwg_ref, dbg_ref, dga_ref, dgs_ref, dwo_ref):
        attv, yv, db = att_ref[...], y_ref[...], d_ref[...]
        y2, y2b, sg, ssm, ra, rs = _mix_parts(attv, yv, wg_ref[...], bg_ref[...])
        na, ns = attv * ra, ssm * rs
        ma = (na * ga_ref[...]).astype(BF16)
        ms = (ns * gs_ref[...]).astype(BF16)
        dma = _nt(db, wo_ref[0:AW, :])
        dms = _nt(db, wo_ref[AW:D, :])
        datt, dga = _rms_bwd(na, ra, ga_ref[...], dma)
        dssm, dgs = _rms_bwd(ns, rs, gs_ref[...], dms)
        dgl = dssm * y2 * sg * (1.0 - sg)
        dglb = dgl.astype(BF16)
        dy2 = dssm * sg + _nt(dglb, wg_ref[...])
        _, gelu_vjp = jax.vjp(jax.nn.gelu, yv)
        datt_ref[...] = datt
        dy_ref[...] = gelu_vjp(dy2)[0]

        @pl.when(pl.program_id(0) == 0)
        def _():
            dwg_ref[...] = jnp.zeros_like(dwg_ref)
            dbg_ref[...] = jnp.zeros_like(dbg_ref)
            dga_ref[...] = jnp.zeros_like(dga_ref)
            dgs_ref[...] = jnp.zeros_like(dgs_ref)
            dwo_ref[...] = jnp.zeros_like(dwo_ref)
        dwg_ref[...] += _tn(y2b, dglb)
        dbg_ref[...] += jnp.sum(dgl, axis=0, keepdims=True)
        dga_ref[...] += dga
        dgs_ref[...] += dgs
        dwo_ref[0:AW, :] += _tn(ma, db)
        dwo_ref[AW:D, :] += _tn(ms, db)

    row = lambda i: (i, 0)
    fixed = lambda i: (0, 0)
    half = pl.BlockSpec((_TM_MIX, AW), row)
    vec = pl.BlockSpec((1, AW), fixed)
    sq = pl.BlockSpec((SW, SW), fixed)
    big = pl.BlockSpec((D, D), fixed)
    return pl.pallas_call(
        body, name="mix_bwd", grid=(T // _TM_MIX,),
        in_specs=[half, half, pl.BlockSpec((_TM_MIX, D), row), sq, vec, vec, vec, big],
        out_specs=[half, half, sq, vec, vec, vec, big],
        out_shape=[_sds((T, AW)), _sds((T, SW)), _sds((SW, SW)), _sds((1, SW)), _sds((1, AW)), _sds((1, SW)), _sds((D, D))],
        compiler_params=_cp(("arbitrary",)),
    )(att, y, dx1b, wg, bg, ga, gs, wo)


_TC_FF = 256
_NC_FF = DFF // _TC_FF


def _shift_down(h, k):
    row = lax.broadcasted_iota(jnp.int32, h.shape, 0)
    return jnp.where(row >= k, pltpu.roll(h, k, 0), 0.0)


def _shift_up(h, k):
    row = lax.broadcasted_iota(jnp.int32, h.shape, 0)
    return jnp.where(row < T - k, pltpu.roll(h, T - k, 0), 0.0)


def _conv(h, w, b):
    return w[0:1, :] * _shift_down(h, 2) + w[1:2, :] * _shift_down(h, 1) + w[2:3, :] * h + b


def _ff_specs():
    gate = lambda j: (0, j)
    val = lambda j: (0, _NC_FF + j)
    return gate, val


def _ffn_act(up, conv_w, conv_b):
    gate, val = _ff_specs()

    def body(ug_ref, uv_ref, wg_ref, wv_ref, bg_ref, bv_ref, act_ref):
        g = _conv(ug_ref[...], wg_ref[...], bg_ref[...])
        v = _conv(uv_ref[...], wv_ref[...], bv_ref[...])
        act_ref[...] = (g * jax.nn.sigmoid(g) * v).astype(BF16)

    tc = _TC_FF
    return pl.pallas_call(
        body, name="ffn_act", grid=(_NC_FF,),
        in_specs=[pl.BlockSpec((T, tc), gate), pl.BlockSpec((T, tc), val), pl.BlockSpec((3, tc), gate),
                  pl.BlockSpec((3, tc), val), pl.BlockSpec((1, tc), gate), pl.BlockSpec((1, tc), val)],
        out_specs=pl.BlockSpec((T, tc), gate), out_shape=_sds((T, DFF), BF16),
        compiler_params=_cp(("parallel",)),
    )(up, up, conv_w, conv_w, conv_b, conv_b)


def _ffn_act_bwd(up, dact, conv_w, conv_b):
    gate, val = _ff_specs()

    def conv_bwd(h, w, d):
        dup = w[2:3, :] * d + w[1:2, :] * _shift_up(d, 1) + w[0:1, :] * _shift_up(d, 2)
        dw = jnp.concatenate([jnp.sum(d * _shift_down(h, 2), axis=0, keepdims=True),
                              jnp.sum(d * _shift_down(h, 1), axis=0, keepdims=True),
                              jnp.sum(d * h, axis=0, keepdims=True)], axis=0)
        return dup, dw, jnp.sum(d, axis=0, keepdims=True)

    def body(ug_ref, uv_ref, da_ref, wg_ref, wv_ref, bg_ref, bv_ref,
             dug_ref, duv_ref, dwg_ref, dwv_ref, dbg_ref, dbv_ref):
        ug, uv, da = ug_ref[...], uv_ref[...], da_ref[...]
        g = _conv(ug, wg_ref[...], bg_ref[...])
        v = _conv(uv, wv_ref[...], bv_ref[...])
        sg = jax.nn.sigmoid(g)
        dg = da * v * (sg * (1.0 + g * (1.0 - sg)))
        dv = da * (g * sg)
        dug, dwg, dbg = conv_bwd(ug, wg_ref[...], dg)
        duv, dwv, dbv = conv_bwd(uv, wv_ref[...], dv)
        dug_ref[...] = dug.astype(BF16)
        duv_ref[...] = duv.astype(BF16)
        dwg_ref[...] = dwg
        dwv_ref[...] = dwv
        dbg_ref[...] = dbg
        dbv_ref[...] = dbv

    tc = _TC_FF
    tile = pl.BlockSpec((T, tc), gate)
    w3 = pl.BlockSpec((3, tc), gate)
    b1 = pl.BlockSpec((1, tc), gate)
    return pl.pallas_call(
        body, name="ffn_act_bwd", grid=(_NC_FF,),
        in_specs=[tile, pl.BlockSpec((T, tc), val), tile, w3, pl.BlockSpec((3, tc), val), b1, pl.BlockSpec((1, tc), val)],
        out_specs=[tile, tile, w3, w3, b1, b1],
        out_shape=[_sds((T, DFF), BF16), _sds((T, DFF), BF16), _sds((3, DFF)), _sds((3, DFF)), _sds((1, DFF)), _sds((1, DFF))],
        compiler_params=_cp(("parallel",)),
    )(up, up, dact, conv_w, conv_w, conv_b, conv_b)


def _ffn_down_loss(act, wd, x1, target):
    tm = 256

    def body(a_ref, w_ref, x1_ref, t_ref, loss_ref, dx_ref, dxb_ref):
        err = x1_ref[...] + _nn(a_ref[...], w_ref[...]) - t_ref[...]
        dx = err * (1.0 / D)
        dx_ref[...] = dx
        dxb_ref[...] = dx.astype(BF16)

        @pl.when(pl.program_id(0) == 0)
        def _():
            loss_ref[...] = jnp.zeros_like(loss_ref)
        loss_ref[...] += 0.5 * jnp.sum(jnp.mean(err * err, axis=-1, keepdims=True), axis=0, keepdims=True)

    row = lambda i: (i, 0)
    fixed = lambda i: (0, 0)
    return pl.pallas_call(
        body, name="ffn_down_loss", grid=(T // tm,),
        in_specs=[pl.BlockSpec((tm, DFF), row), pl.BlockSpec((DFF, D), fixed), pl.BlockSpec((tm, D), row),
                  pl.BlockSpec((tm, D), row)],
        out_specs=[pl.BlockSpec((1, 1), fixed), pl.BlockSpec((tm, D), row), pl.BlockSpec((tm, D), row)],
        out_shape=[_sds((1, 1)), _sds((T, D)), _sds((T, D), BF16)],
        compiler_params=_cp(("arbitrary",)),
    )(act, wd, x1, target)


def _adamw_math(w, g, m, v):
    m = ADAM_B1 * m + (1.0 - ADAM_B1) * g
    v = ADAM_B2 * v + (1.0 - ADAM_B2) * jnp.square(g)
    m_hat = m / (1.0 - ADAM_B1 ** ADAM_STEP)
    v_hat = v / (1.0 - ADAM_B2 ** ADAM_STEP)
    delta = -ADAM_LR * (m_hat / (jnp.sqrt(v_hat) + ADAM_EPS) + ADAM_WD * w)
    return delta, m, v


def _adamw(w, g, m, v, name):
    rows, cols = w.shape
    tr = rows
    while tr * cols * 4 > (1 << 20) and tr % 16 == 0:
        tr //= 2

    def body(w_ref, g_ref, m_ref, v_ref, d_ref, nm_ref, nv_ref):
        d, nm, nv = _adamw_math(w_ref[...], g_ref[...], m_ref[...], v_ref[...])
        d_ref[...] = d
        nm_ref[...] = nm
        nv_ref[...] = nv

    spec = pl.BlockSpec((tr, cols), lambda i: (i, 0))
    return pl.pallas_call(
        body, name=name, grid=(rows // tr,), in_specs=[spec] * 4, out_specs=[spec] * 3,
        out_shape=[_sds((rows, cols))] * 3, compiler_params=_cp(("parallel",)),
    )(w, g, m, v)


def _adamw_small(gall, w, m, v):
    rows = w.shape[0]

    def body(ga_ref, w_ref, m_ref, v_ref, g_ref, d_ref, nm_ref, nv_ref):
        g = ga_ref[0]
        for dev in range(1, NDEV):
            g = g + ga_ref[dev]
        d, nm, nv = _adamw_math(w_ref[...], g, m_ref[...], v_ref[...])
        g_ref[...] = g
        d_ref[...] = d
        nm_ref[...] = nm
        nv_ref[...] = nv

    return pl.pallas_call(body, name="adamw_small", out_shape=[_sds((rows, LANES))] * 4,
                          compiler_params=_cp())(gall, w, m, v)


_TR_SUM = 512


def _pair_sum(g2, recv):
    _, nchip, rows, lanes = g2.shape
    c = lax.axis_index("c")
    chip = 2 * lax.axis_index("x") + lax.axis_index("y")
    sel = jnp.stack([c, chip]).astype(jnp.int32)

    def body(sel_ref, own_ref, recv_ref, mine_ref, out_ref):
        s = own_ref[0].astype(F32) + recv_ref[...].astype(F32)
        out_ref[...] = s.astype(BF16)
        k = sel_ref[1]
        mine_ref[...] = own_ref[0, k].astype(F32) + recv_ref[k].astype(F32)

    tr = _TR_SUM
    assert rows % tr == 0
    grid_spec = pltpu.PrefetchScalarGridSpec(
        num_scalar_prefetch=1, grid=(rows // tr,),
        in_specs=[pl.BlockSpec((1, nchip, tr, lanes), lambda i, s: (s[0], 0, i, 0)),
                  pl.BlockSpec((nchip, tr, lanes), lambda i, s: (0, i, 0))],
        out_specs=[pl.BlockSpec((tr, lanes), lambda i, s: (i, 0)),
                   pl.BlockSpec((nchip, tr, lanes), lambda i, s: (0, i, 0))],
    )
    return pl.pallas_call(
        body, name="rs_pair_sum", grid_spec=grid_spec,
        out_shape=[_sds((rows, lanes)), _sds((nchip, rows, lanes), BF16)],
        compiler_params=_cp(("arbitrary",)),
    )(sel, g2, recv)


def _final_sum(mine, recv):
    rows, lanes = mine.shape

    def body(m_ref, r_ref, o_ref):
        o_ref[...] = ((m_ref[...] + r_ref[0].astype(F32)) + r_ref[1].astype(F32)) + r_ref[2].astype(F32)

    tr = _TR_SUM
    return pl.pallas_call(
        body, name="rs_final_sum", grid=(rows // tr,),
        in_specs=[pl.BlockSpec((tr, lanes), lambda i: (i, 0)), pl.BlockSpec((3, tr, lanes), lambda i: (0, i, 0))],
        out_specs=pl.BlockSpec((tr, lanes), lambda i: (i, 0)), out_shape=_sds((rows, lanes)),
        compiler_params=_cp(("parallel",)),
    )(mine, recv)


def _pad_rows(a, mult):
    r = (-a.shape[-2]) % mult
    if r == 0:
        return a
    pad = [(0, 0)] * (a.ndim - 2) + [(0, r), (0, 0)]
    return jnp.pad(a, pad)


def _rows128(a, mult):
    lead = a.shape[:-1]
    n = a.shape[-1]
    r = (-n) % LANES
    if r:
        a = jnp.pad(a, [(0, 0)] * len(lead) + [(0, r)])
    return _pad_rows(a.reshape(*lead, -1, LANES), mult)


_BIG_ROWS = (("w_in", 2056), ("w_glu", 256), ("w_out", 1024), ("w_up", 5632), ("w_down", 2816))
_ROW_MULT = 16


def _padded(n):
    return -(-n // _ROW_MULT) * _ROW_MULT


def _small_layout():
    sizes = (("g_mix", D), ("b_f", NH), ("g_q", DH), ("g_k", DH), ("lambda_re", NG * NP), ("lambda_im", NG * NP),
             ("log_step", NG), ("b_re", NG * NP * GS), ("b_im", NG * NP * GS), ("c_re", NG * GS * NP),
             ("c_im", NG * GS * NP), ("d_skip", NG * GS), ("b_glu", SW), ("g_attn_out", AW), ("g_ssm_out", SW),
             ("g_ffn", D), ("conv_b", 2 * DFF))
    out, off = {}, 0
    for name, n in sizes:
        rows = -(-n // 1024) * 8
        out[name] = (off, n, rows)
        off += rows
    return out, off


def _pack_small(d):
    layout, _ = _small_layout()
    return jnp.concatenate([_rows128(d[name].reshape(-1).astype(F32), 8) for name in layout], axis=0)


def _unpack_small(p, shapes):
    layout, _ = _small_layout()
    return {name: p[off:off + rows].reshape(-1)[:n].reshape(shapes[name]) for name, (off, n, rows) in layout.items()}


def _block_diag(m, ncol):
    eye = jnp.eye(8, dtype=m.dtype)
    r, c = m.shape[2], m.shape[3]
    return (m[:, :, :, None, :] * eye[None, :, None, :, None]).reshape(4, 8 * r, 8 * c)


def _diag_blocks(dense, r, c):
    eye = jnp.eye(8, dtype=dense.dtype)
    return jnp.sum(dense.reshape(4, 8, r, 8, c) * eye[None, :, None, :, None], axis=3)


def kernel(x, g_mix, w_in, b_f, g_q, g_k, lambda_re, lambda_im, log_step, b_re, b_im, c_re, c_im, d_skip, w_glu, b_glu, g_attn_out, g_ssm_out, w_out, g_ffn, w_up, conv_w, conv_b, w_down, loss_target, m_g_mix, m_w_in, m_b_f, m_g_q, m_g_k, m_lambda_re, m_lambda_im, m_log_step, m_b_re, m_b_im, m_c_re, m_c_im, m_d_skip, m_w_glu, m_b_glu, m_g_attn_out, m_g_ssm_out, m_w_out, m_g_ffn, m_w_up, m_conv_w, m_conv_b, m_w_down, v_g_mix, v_w_in, v_b_f, v_g_q, v_g_k, v_lambda_re, v_lambda_im, v_log_step, v_b_re, v_b_im, v_c_re, v_c_im, v_d_skip, v_w_glu, v_b_glu, v_g_attn_out, v_g_ssm_out, v_w_out, v_g_ffn, v_w_up, v_conv_w, v_conv_b, v_w_down):
    weights = dict(g_mix=g_mix, w_in=w_in, b_f=b_f, g_q=g_q, g_k=g_k, lambda_re=lambda_re, lambda_im=lambda_im,
                   log_step=log_step, b_re=b_re, b_im=b_im, c_re=c_re, c_im=c_im, d_skip=d_skip, w_glu=w_glu,
                   b_glu=b_glu, g_attn_out=g_attn_out, g_ssm_out=g_ssm_out, w_out=w_out, g_ffn=g_ffn, w_up=w_up,
                   conv_w=conv_w, conv_b=conv_b, w_down=w_down)
    mom_m = dict(g_mix=m_g_mix, w_in=m_w_in, b_f=m_b_f, g_q=m_g_q, g_k=m_g_k, lambda_re=m_lambda_re,
                 lambda_im=m_lambda_im, log_step=m_log_step, b_re=m_b_re, b_im=m_b_im, c_re=m_c_re, c_im=m_c_im,
                 d_skip=m_d_skip, w_glu=m_w_glu, b_glu=m_b_glu, g_attn_out=m_g_attn_out, g_ssm_out=m_g_ssm_out,
                 w_out=m_w_out, g_ffn=m_g_ffn, w_up=m_w_up, conv_w=m_conv_w, conv_b=m_conv_b, w_down=m_w_down)
    mom_v = dict(g_mix=v_g_mix, w_in=v_w_in, b_f=v_b_f, g_q=v_g_q, g_k=v_g_k, lambda_re=v_lambda_re,
                 lambda_im=v_lambda_im, log_step=v_log_step, b_re=v_b_re, b_im=v_b_im, c_re=v_c_re, c_im=v_c_im,
                 d_skip=v_d_skip, w_glu=v_w_glu, b_glu=v_b_glu, g_attn_out=v_g_attn_out, g_ssm_out=v_g_ssm_out,
                 w_out=v_w_out, g_ffn=v_g_ffn, w_up=v_w_up, conv_w=v_conv_w, conv_b=v_conv_b, w_down=v_w_down)
    names = list(weights)
    big = ("w_in", "w_glu", "w_out", "w_up", "conv_w", "w_down")

    xs = x[0]
    target = loss_target[0]

    conv_bits = lax.bitcast_convert_type(conv_w, BF16).reshape(-1)
    shard = jnp.concatenate(
        [_rows128(weights[n].astype(BF16).reshape(-1), _ROW_MULT) for n, _ in _BIG_ROWS]
        + [_rows128(conv_bits, _ROW_MULT)], axis=0)
    full = _all_gather(shard, "gather_weights")
    offs, off = {}, 0
    for n, r in _BIG_ROWS:
        offs[n] = off
        off += _padded(r)

    def piece(n, r):
        return full[:, offs[n]:offs[n] + r]

    w_in_f = piece("w_in", 2056).reshape(NDEV, D, 257).transpose(1, 0, 2).reshape(D, 8 * 257)
    wcat = jnp.concatenate([w_in_f[:, 0:1536], w_in_f[:, 1544:2056], w_in_f[:, 1536:1544],
                            jnp.zeros((D, LANES - NH), BF16)], axis=1)
    wg_f = piece("w_glu", 256).reshape(SW, SW)
    wo_f = piece("w_out", 1024).reshape(D, D)
    wu_f = piece("w_up", 5632).reshape(NDEV, D, 704).transpose(1, 0, 2).reshape(D, 2 * DFF)
    wd_f = piece("w_down", 2816).reshape(DFF, D)
    conv_f = lax.bitcast_convert_type(full[:, off:off + 33].reshape(NDEV, -1)[:, :3 * 704 * 2].reshape(NDEV, 3, 704, 2), F32)
    conv_f = conv_f.transpose(1, 0, 2).reshape(3, 2 * DFF)

    row = lambda a: a.reshape(1, -1)

    h1, z = _norm_proj(xs, row(g_mix), wcat, 256, ZC, "in_proj")
    zq = z[:, 0:2048]
    f_t = z[:, 2048:2048 + NH].T
    heads = lambda a: a.reshape(T, NH, DH).transpose(1, 0, 2)
    qh, kh, vh = heads(z[:, 0:512]), heads(z[:, 512:1024]), heads(z[:, 1024:1536])
    bf_col = b_f.reshape(NH, 1)
    cgate = _gates_fwd(f_t, bf_col)
    cq, ck = cgate[:, :, None], cgate[:, None, :]
    oh, lse = _attn_fwd(qh, kh, vh, cq, ck, row(g_q), row(g_k))
    att = oh.transpose(1, 0, 2).reshape(T, AW)

    ls_col = log_step.reshape(NG, 1)
    br_t, bi_t = b_re.transpose(0, 2, 1), b_im.transpose(0, 2, 1)
    ab_re, ab_im, bb_re, bb_im = _disc_fwd(lambda_re, lambda_im, ls_col, br_t, bi_t)
    bb = jnp.concatenate([_block_diag(bb_re.reshape(4, 8, GS, NP), NP), _block_diag(bb_im.reshape(4, 8, GS, NP), NP)],
                         axis=2).astype(BF16)
    ab = jnp.stack([ab_re.reshape(4, 512), ab_im.reshape(4, 512)], axis=1)
    cdiag = lambda cm: _block_diag(cm.reshape(4, 8, GS, NP).transpose(0, 1, 3, 2), GS)
    cc = jnp.concatenate([cdiag(c_re), -cdiag(c_im)], axis=1).astype(BF16)
    dsk = d_skip.reshape(4, 1, LANES)
    y = _ssm_fwd(zq, bb, ab, cc, dsk)

    x1 = _mix_fwd(att, y, xs, wg_f, row(b_glu), row(g_attn_out), row(g_ssm_out), wo_f)
    h2, up = _norm_proj(x1, row(g_ffn), wu_f, 512, 1408, "ffn_up")
    act = _ffn_act(up, conv_f, row(conv_b))
    loss_dev, dx2, dx2b = _ffn_down_loss(act, wd_f, x1, target)
    loss = lax.psum(loss_dev[0, 0], ("x", "y", "c"))

    dact = _mm(dx2b, wd_f, "nt", 512, 1408, "ffn_dact")
    d_wd = _mm(act, dx2b, "tn", 1408, D, "ffn_dwd")
    dup_g, dup_v, dcw_g, dcw_v, dcb_g, dcb_v = _ffn_act_bwd(up, dact, conv_f, row(conv_b))
    dup = jnp.concatenate([dup_g, dup_v], axis=1)
    d_wu = _mm(h2, dup, "tn", 512, 1408, "ffn_dwu")
    dx1, dg_ffn = _norm_proj_bwd(dup, wu_f, x1, row(g_ffn), dx2, 256, "ffn_up_bwd")
    datt, dy, d_wg, d_bg, d_ga, d_gs, d_wo = _mix_bwd(att, y, dx1.astype(BF16), wg_f, row(b_glu), row(g_attn_out),
                                                       row(g_ssm_out), wo_f)
    du, dbb, dab, dcc, ddsk = _ssm_bwd(zq, dy, bb, ab, cc, dsk)
    dbb_re = _diag_blocks(dbb[:, :, 0:512], GS, NP).reshape(NG, GS, NP)
    dbb_im = _diag_blocks(dbb[:, :, 512:1024], GS, NP).reshape(NG, GS, NP)
    dlr, dli, dls, dbr_t, dbi_t = _disc_bwd(lambda_re, lambda_im, ls_col, br_t, bi_t,
                                            dab[:, 0].reshape(NG, NP), dab[:, 1].reshape(NG, NP), dbb_re, dbb_im)
    d_cre = _diag_blocks(dcc[:, 0:512], NP, GS).transpose(0, 1, 3, 2).reshape(NG, GS, NP)
    d_cim = -_diag_blocks(dcc[:, 512:1024], NP, GS).transpose(0, 1, 3, 2).reshape(NG, GS, NP)

    doh = heads(datt)
    dqh, dkh, dvh, dck, d_gq, d_gk = _attn_bwd(qh, kh, vh, cq, ck, row(g_q), row(g_k), lse, doh)
    df_t, d_bf = _gates_bwd(dck.reshape(NH, T), f_t, bf_col)
    unheads = lambda a: a.transpose(1, 0, 2).reshape(T, AW)
    dz = jnp.concatenate([unheads(dqh), unheads(dkh), unheads(dvh), du, df_t.T, jnp.zeros((T, LANES - NH), F32)],
                         axis=1).astype(BF16)
    d_wcat = _mm(h1, dz, "tn", 512, ZC, "in_proj_dw")
    grad_x, dg_mix = _norm_proj_bwd(dz, wcat, xs, row(g_mix), dx1, 256, "in_proj_bwd")
    d_win = jnp.concatenate([d_wcat[:, 0:1536], d_wcat[:, 2048:2048 + NH], d_wcat[:, 1536:2048]], axis=1)

    by_cols = lambda g, n: g.reshape(g.shape[0], NDEV, n).transpose(1, 0, 2)
    dconv = jnp.concatenate([dcw_g, dcw_v], axis=1)
    gparts = [by_cols(d_win, 257).reshape(NDEV, -1), d_wg.reshape(NDEV, -1), d_wo.reshape(NDEV, -1),
              by_cols(d_wu, 704).reshape(NDEV, -1), d_wd.reshape(NDEV, -1), by_cols(dconv, 704).reshape(NDEV, -1)]
    gpack = jnp.concatenate([_rows128(g, _ROW_MULT) for g in gparts], axis=1)
    gpack = _pad_rows(gpack, _TR_SUM)
    rg = gpack.shape[1]
    g2 = gpack.astype(BF16).reshape(4, 2, rg, LANES).transpose(1, 0, 2, 3)
    from_sibling = _swap_sibling(g2, "rs_pair_swap")
    mine, chip_sums = _pair_sum(g2, from_sibling)
    from_chips = _swap_chips(chip_sums, "rs_chip_swap")
    gsum = _final_sum(mine, from_chips)

    grads = {}
    goff = 0
    for n, r in _BIG_ROWS:
        grads[n] = gsum[goff:goff + r].reshape(weights[n].shape)
        goff += _padded(r)
    grads["conv_w"] = gsum[goff:goff + 17].reshape(-1)[:3 * 704].reshape(3, 704)

    small = dict(g_mix=dg_mix, b_f=d_bf, g_q=d_gq, g_k=d_gk, lambda_re=dlr, lambda_im=dli, log_step=dls,
                 b_re=dbr_t.transpose(0, 2, 1), b_im=dbi_t.transpose(0, 2, 1), c_re=d_cre, c_im=d_cim, d_skip=ddsk,
                 b_glu=d_bg, g_attn_out=d_ga, g_ssm_out=d_gs, g_ffn=dg_ffn,
                 conv_b=jnp.concatenate([dcb_g, dcb_v], axis=1))
    small_all = _all_gather(_pack_small(small), "gather_small_grads")
    sw = {n: weights[n] for n in small}
    sg_p, sd_p, sm_p, sv_p = _adamw_small(small_all, _pack_small(sw), _pack_small({n: mom_m[n] for n in small}),
                                          _pack_small({n: mom_v[n] for n in small}))
    shapes = {n: weights[n].shape for n in small}
    sg, sd, sm, sv = (_unpack_small(p, shapes) for p in (sg_p, sd_p, sm_p, sv_p))

    deltas, new_m, new_v = {}, {}, {}
    for n in names:
        if n in big:
            deltas[n], new_m[n], new_v[n] = _adamw(weights[n], grads[n], mom_m[n], mom_v[n], "adamw_" + n)
        else:
            grads[n], deltas[n], new_m[n], new_v[n] = sg[n], sd[n], sm[n], sv[n]

    return (loss, grad_x[None], *[grads[n] for n in names], *[deltas[n] for n in names],
            *[new_m[n] for n in names], *[new_v[n] for n in names])
```

```python
import jax
import jax.numpy as jnp
from jax import lax
from jax.experimental import pallas as pl
from jax.experimental.pallas import tpu as pltpu

F32, BF16 = jnp.float32, jnp.bfloat16
T, D = 2048, 1024
NH, DH = 8, 64
AW, SW = 512, 512
NG, GS, NP = 32, 16, 64
DFF = 2816
NDEV = 8
CS = 2 * DFF // NDEV
EPS = 1e-6
NEG = -1e30
ZC = 4 * 512 + 128
SEG = 256
NSEG = T // SEG
BQ = 256
VMEM_LIMIT = 56 * 1024 * 1024
LANES = 128
MESH = pl.DeviceIdType.MESH
HI = lax.Precision.HIGHEST

ADAM_LR, ADAM_B1, ADAM_B2, ADAM_EPS, ADAM_WD, ADAM_STEP = 0.001, 0.9, 0.999, 1e-08, 0.01, 10


def _cp(dims=None):
    if dims is None:
        return pltpu.CompilerParams(vmem_limit_bytes=VMEM_LIMIT)
    return pltpu.CompilerParams(dimension_semantics=dims, vmem_limit_bytes=VMEM_LIMIT)


def _sds(shape, dtype=F32):
    return jax.ShapeDtypeStruct(shape, dtype)


def _nt(a, b):
    return lax.dot_general(a, b, (((1,), (1,)), ((), ())), preferred_element_type=F32)


def _tn(a, b):
    return lax.dot_general(a, b, (((0,), (0,)), ((), ())), preferred_element_type=F32)


def _nn(a, b):
    return jnp.dot(a, b, preferred_element_type=F32)


def _rms_r(x):
    return lax.rsqrt(jnp.mean(x * x, axis=-1, keepdims=True) + EPS)


def _rms_bwd(n, r, g, dh):
    dn = dh * g
    dx = r * (dn - n * jnp.mean(dn * n, axis=-1, keepdims=True))
    return dx, jnp.sum(dh * n, axis=0, keepdims=True)


def _row_tile(rows, cols, itemsize, unit):
    tr = rows
    while tr * cols * itemsize > (1 << 20) and tr % (2 * unit) == 0:
        tr //= 2
    return tr


def _place():
    return lax.axis_index("x"), lax.axis_index("y"), lax.axis_index("c")


def _all_gather(shards, name):
    n = len(shards)

    def body(*refs):
        x_refs, out_refs = refs[:n], refs[n:2 * n]
        send_sems, recv_sems, local_sems = refs[2 * n:]
        x, y, c = _place()
        me, sibling = (x, y, c), (x, y, 1 - c)
        chips = [(1 - x, y), (x, 1 - y), (1 - x, 1 - y)]

        def copy(a, k, block, to, src=None):
            px, py, pc = block
            slot = out_refs[a].at[4 * px + 2 * py + pc]
            return pltpu.make_async_remote_copy(
                src_ref=slot if src is None else src, dst_ref=slot,
                send_sem=send_sems.at[7 * a + k], recv_sem=recv_sems.at[7 * a + k], device_id=to, device_id_type=MESH)

        mine, first, passed = [], [], []
        for a in range(n):
            cp = pltpu.make_async_copy(x_refs[a], out_refs[a].at[4 * x + 2 * y + c], local_sems.at[a])
            cp.start()
            mine.append(cp)
            cps = [copy(a, 0, me, sibling, src=x_refs[a])]
            cps += [copy(a, 1 + j, me, (*chip, c), src=x_refs[a]) for j, chip in enumerate(chips)]
            for cp in cps:
                cp.start()
            first += cps
        for a in range(n):
            for j, chip in enumerate(chips):
                copy(a, 1 + j, (*chip, c), me).wait_recv()
                fwd = copy(a, 4 + j, (*chip, c), sibling)
                fwd.start()
                passed.append(fwd)
        for a in range(n):
            copy(a, 0, sibling, me).wait_recv()
            for j, chip in enumerate(chips):
                copy(a, 4 + j, (*chip, 1 - c), me).wait_recv()
        for cp in first + passed:
            cp.wait_send()
        for cp in mine:
            cp.wait()

    anyspec = pl.BlockSpec(memory_space=pl.ANY)
    return pl.pallas_call(
        body, name=name,
        out_shape=[_sds((NDEV, *s.shape), s.dtype) for s in shards],
        in_specs=[anyspec] * n, out_specs=[anyspec] * n,
        scratch_shapes=[pltpu.SemaphoreType.DMA((7 * n,)), pltpu.SemaphoreType.DMA((7 * n,)),
                        pltpu.SemaphoreType.DMA((n,))],
    )(*shards)


def _swap_sibling(arrs, name):
    n = len(arrs)

    def body(*refs):
        a_refs, out_refs = refs[:n], refs[n:2 * n]
        send_sems, recv_sems = refs[2 * n:]
        x, y, c = _place()
        cps = []
        for a in range(n):
            for k in range(4):
                cps.append(pltpu.make_async_remote_copy(
                    src_ref=a_refs[a].at[k, 1 - c], dst_ref=out_refs[a].at[k],
                    send_sem=send_sems.at[4 * a + k], recv_sem=recv_sems.at[4 * a + k],
                    device_id=(x, y, 1 - c), device_id_type=MESH))
        for cp in cps:
            cp.start()
        for cp in cps:
            cp.wait()

    anyspec = pl.BlockSpec(memory_space=pl.ANY)
    return pl.pallas_call(
        body, name=name,
        out_shape=[_sds((4, *a.shape[2:]), a.dtype) for a in arrs],
        in_specs=[anyspec] * n, out_specs=[anyspec] * n,
        scratch_shapes=[pltpu.SemaphoreType.DMA((4 * n,)), pltpu.SemaphoreType.DMA((4 * n,))],
    )(*arrs)


def _swap_chips(parts, name):
    n = len(parts)

    def body(*refs):
        p_refs, out_refs = refs[:n], refs[n:2 * n]
        send_sems, recv_sems = refs[2 * n:]
        x, y, c = _place()
        chips = [(1 - x, y), (x, 1 - y), (1 - x, 1 - y)]
        cps = []
        for a in range(n):
            for j, (px, py) in enumerate(chips):
                cps.append(pltpu.make_async_remote_copy(
                    src_ref=p_refs[a].at[2 * px + py], dst_ref=out_refs[a].at[j],
                    send_sem=send_sems.at[3 * a + j], recv_sem=recv_sems.at[3 * a + j],
                    device_id=(px, py, c), device_id_type=MESH))
        for cp in cps:
            cp.start()
        for cp in cps:
            cp.wait()

    anyspec = pl.BlockSpec(memory_space=pl.ANY)
    return pl.pallas_call(
        body, name=name,
        out_shape=[_sds((3, *p.shape[1:]), p.dtype) for p in parts],
        in_specs=[anyspec] * n, out_specs=[anyspec] * n,
        scratch_shapes=[pltpu.SemaphoreType.DMA((3 * n,)), pltpu.SemaphoreType.DMA((3 * n,))],
    )(*parts)


def _in_proj(x, g, wcat):
    tm = 256

    def body(x_ref, g_ref, w_ref, h_ref, z_ref):
        xv = x_ref[...]
        h = (xv * _rms_r(xv) * g_ref[...]).astype(BF16)
        h_ref[...] = h
        z_ref[...] = _nn(h, w_ref[...])

    row = lambda i: (i, 0)
    fixed = lambda i: (0, 0)
    return pl.pallas_call(
        body, name="in_proj", grid=(T // tm,),
        in_specs=[pl.BlockSpec((tm, D), row), pl.BlockSpec((1, D), fixed), pl.BlockSpec((D, ZC), fixed)],
        out_specs=[pl.BlockSpec((tm, D), row), pl.BlockSpec((tm, ZC), row)],
        out_shape=[_sds((T, D), BF16), _sds((T, ZC))],
        compiler_params=_cp(("parallel",)),
    )(x, g, wcat)


def _in_proj_dw(h, dq, dk, dv, du, df):
    tm = 512

    def body(h_ref, dq_ref, dk_ref, dv_ref, du_ref, df_ref, o_ref):
        hv = h_ref[...]
        for j, d_ref in enumerate((dq_ref, dk_ref, dv_ref, du_ref)):
            o_ref[:, 512 * j:512 * (j + 1)] = _tn(hv, d_ref[...])
        o_ref[:, 2048:ZC] = _tn(hv, df_ref[...])

    full = lambda w: pl.BlockSpec((T, w), lambda i: (0, 0))
    return pl.pallas_call(
        body, name="in_proj_dw", grid=(D // tm,),
        in_specs=[pl.BlockSpec((T, tm), lambda i: (0, i)), full(512), full(512), full(512), full(512), full(LANES)],
        out_specs=pl.BlockSpec((tm, ZC), lambda i: (i, 0)), out_shape=_sds((D, ZC)),
        compiler_params=_cp(("parallel",)),
    )(h, dq, dk, dv, du, df)


def _in_proj_bwd(dq, dk, dv, du, df, wcat, x, g, dres):
    tm = 256

    def body(dq_ref, dk_ref, dv_ref, du_ref, df_ref, w_ref, x_ref, g_ref, dres_ref, dx_ref, dg_ref):
        dh = _nt(df_ref[...], w_ref[:, 2048:ZC])
        for j, d_ref in enumerate((dq_ref, dk_ref, dv_ref, du_ref)):
            dh = dh + _nt(d_ref[...], w_ref[:, 512 * j:512 * (j + 1)])
        xv = x_ref[...]
        r = _rms_r(xv)
        dxr, dgp = _rms_bwd(xv * r, r, g_ref[...], dh)
        dx_ref[...] = dres_ref[...] + dxr

        @pl.when(pl.program_id(0) == 0)
        def _():
            dg_ref[...] = jnp.zeros_like(dg_ref)
        dg_ref[...] += dgp

    row = lambda i: (i, 0)
    fixed = lambda i: (0, 0)
    part = pl.BlockSpec((tm, 512), row)
    return pl.pallas_call(
        body, name="in_proj_bwd", grid=(T // tm,),
        in_specs=[part, part, part, part, pl.BlockSpec((tm, LANES), row), pl.BlockSpec((D, ZC), fixed),
                  pl.BlockSpec((tm, D), row), pl.BlockSpec((1, D), fixed), pl.BlockSpec((tm, D), row)],
        out_specs=[pl.BlockSpec((tm, D), row), pl.BlockSpec((1, D), fixed)],
        out_shape=[_sds((T, D)), _sds((1, D))],
        compiler_params=_cp(("arbitrary",)),
    )(dq, dk, dv, du, df, wcat, x, g, dres)


def _stack_lanes(v):
    return jnp.concatenate([v[:, LANES * b:LANES * (b + 1)] for b in range(T // LANES)], axis=0)


def _unstack_lanes(s):
    return jnp.concatenate([s[8 * b:8 * b + 8, :] for b in range(T // LANES)], axis=1)


def _cumsum_lanes(v, reverse):
    st = _stack_lanes(v)
    ri = lax.broadcasted_iota(jnp.int32, (LANES, LANES), 0)
    ci = lax.broadcasted_iota(jnp.int32, (LANES, LANES), 1)
    tri = (ri >= ci) if reverse else (ri <= ci)
    intra = jnp.dot(st, tri.astype(F32), precision=HI, preferred_element_type=F32)
    tot = intra[:, 0:1] if reverse else intra[:, LANES - 1:LANES]
    same_head = (ci % 8) == (ri % 8)
    other = (ci // 8 > ri // 8) if reverse else (ci // 8 < ri // 8)
    off = jnp.dot((same_head & other).astype(F32), jnp.broadcast_to(tot, (LANES, LANES)), precision=HI,
                  preferred_element_type=F32)
    return _unstack_lanes(intra + off)


def _gates_fwd(f_t, b_f):
    def body(f_ref, b_ref, c_ref):
        z = f_ref[...] + b_ref[...]
        lf = jnp.minimum(z, 0.0) - jnp.log(1.0 + jnp.exp(-jnp.abs(z)))
        c_ref[...] = _cumsum_lanes(lf, reverse=False)

    return pl.pallas_call(body, name="gates_fwd", out_shape=_sds((NH, T)), compiler_params=_cp())(f_t, b_f)


def _gates_bwd(dck, f_t, b_f):
    def body(d_ref, f_ref, b_ref, df_ref, db_ref):
        z = f_ref[...] + b_ref[...]
        dlf = _cumsum_lanes(d_ref[...], reverse=True)
        df = dlf * jax.nn.sigmoid(-z)
        df_ref[...] = df
        db_ref[...] = jnp.sum(df, axis=1, keepdims=True)

    return pl.pallas_call(body, name="gates_bwd", out_shape=[_sds((NH, T)), _sds((NH, 1))],
                          compiler_params=_cp())(dck, f_t, b_f)


def _causal(s, row0, ncol):
    rows = row0 + lax.broadcasted_iota(jnp.int32, (BQ, ncol), 0)
    cols = lax.broadcasted_iota(jnp.int32, (BQ, ncol), 1)
    return jnp.where(cols <= rows, s, NEG)


_HP = NH // 2


def _qkv_specs():
    return [pl.BlockSpec((T, LANES), lambda p: (0, p)), pl.BlockSpec((T, LANES), lambda p: (0, _HP + p)),
            pl.BlockSpec((T, LANES), lambda p: (0, 2 * _HP + p))]


def _attn_fwd(z, ck, g_q, g_k):
    scale = DH ** -0.5

    def body(q_ref, k_ref, v_ref, ck_ref, gq_ref, gk_ref, o_ref, lse_ref):
        qb, kb, vb = [], [], []
        for hh in range(2):
            cols = slice(DH * hh, DH * (hh + 1))
            qv, kv = q_ref[:, cols], k_ref[:, cols]
            qb.append((qv * _rms_r(qv) * gq_ref[...]).astype(BF16))
            kb.append((kv * _rms_r(kv) * gk_ref[...]).astype(BF16))
            vb.append(v_ref[:, cols].astype(BF16))
        for i in range(T // BQ):
            lo, hi = i * BQ, (i + 1) * BQ
            outs = []
            for hh in range(2):
                s = _nt(qb[hh][lo:hi], kb[hh][:hi]) * scale - ck_ref[0, hh:hh + 1, :hi]
                s = _causal(s, lo, hi)
                m = jnp.max(s, axis=-1, keepdims=True)
                p = jnp.exp(s - m)
                l = jnp.sum(p, axis=-1, keepdims=True)
                outs.append(_nn(p.astype(BF16), vb[hh][:hi]) / l)
                lse_ref[0, lo:hi, hh:hh + 1] = m + jnp.log(l)
            o_ref[lo:hi, :] = jnp.concatenate(outs, axis=1)

    fixed = lambda p: (0, 0)
    return pl.pallas_call(
        body, name="attn_fwd", grid=(_HP,),
        in_specs=_qkv_specs() + [pl.BlockSpec((1, 2, T), lambda p: (p, 0, 0)), pl.BlockSpec((1, DH), fixed),
                                 pl.BlockSpec((1, DH), fixed)],
        out_specs=[pl.BlockSpec((T, LANES), lambda p: (0, p)), pl.BlockSpec((1, T, 2), lambda p: (p, 0, 0))],
        out_shape=[_sds((T, AW)), _sds((_HP, T, 2))],
        compiler_params=_cp(("parallel",)),
    )(z, z, z, ck, g_q, g_k)


def _attn_bwd(z, ck, g_q, g_k, lse, datt):
    scale = DH ** -0.5

    def body(q_ref, k_ref, v_ref, ck_ref, gq_ref, gk_ref, lse_ref, do_ref,
             dq_ref, dk_ref, dv_ref, dck_ref, dgq_ref, dgk_ref, dkn_acc, dv_acc, dc_acc):
        nq, nk, rq, rk, qb, kb, vb, dob = [], [], [], [], [], [], [], []
        for hh in range(2):
            cols = slice(DH * hh, DH * (hh + 1))
            qv, kv = q_ref[:, cols], k_ref[:, cols]
            rq.append(_rms_r(qv))
            rk.append(_rms_r(kv))
            nq.append(qv * rq[hh])
            nk.append(kv * rk[hh])
            qb.append((nq[hh] * gq_ref[...]).astype(BF16))
            kb.append((nk[hh] * gk_ref[...]).astype(BF16))
            vb.append(v_ref[:, cols].astype(BF16))
            dob.append(do_ref[:, cols].astype(BF16))
        dkn_acc[...] = jnp.zeros_like(dkn_acc)
        dv_acc[...] = jnp.zeros_like(dv_acc)
        dc_acc[...] = jnp.zeros_like(dc_acc)
        dgq = jnp.zeros((1, DH), F32)
        for i in range(T // BQ):
            lo, hi = i * BQ, (i + 1) * BQ
            dqs = []
            for hh in range(2):
                s = _nt(qb[hh][lo:hi], kb[hh][:hi]) * scale - ck_ref[0, hh:hh + 1, :hi]
                s = _causal(s, lo, hi)
                p = jnp.exp(s - lse_ref[0, lo:hi, hh:hh + 1])
                dp = _nt(dob[hh][lo:hi], vb[hh][:hi])
                ds = p * (dp - jnp.sum(p * dp, axis=-1, keepdims=True))
                dsb = ds.astype(BF16)
                dv_acc[hh, 0:hi, :] += _tn(p.astype(BF16), dob[hh][lo:hi])
                dkn_acc[hh, 0:hi, :] += _tn(dsb, qb[hh][lo:hi]) * scale
                dc_acc[hh:hh + 1, 0:hi] -= jnp.sum(ds, axis=0, keepdims=True)
                dqn = _nn(dsb, kb[hh][:hi]) * scale
                dqx, dgp = _rms_bwd(nq[hh][lo:hi], rq[hh][lo:hi], gq_ref[...], dqn)
                dqs.append(dqx)
                dgq = dgq + dgp
            dq_ref[lo:hi, :] = jnp.concatenate(dqs, axis=1).astype(BF16)
        dks, dgk = [], jnp.zeros((1, DH), F32)
        for hh in range(2):
            dkx, dgp = _rms_bwd(nk[hh], rk[hh], gk_ref[...], dkn_acc[hh])
            dks.append(dkx)
            dgk = dgk + dgp
        dk_ref[...] = jnp.concatenate(dks, axis=1).astype(BF16)
        dv_ref[...] = jnp.concatenate([dv_acc[0], dv_acc[1]], axis=1).astype(BF16)
        dck_ref[0] = dc_acc[...]

        @pl.when(pl.program_id(0) == 0)
        def _():
            dgq_ref[...] = jnp.zeros_like(dgq_ref)
            dgk_ref[...] = jnp.zeros_like(dgk_ref)
        dgq_ref[...] += dgq
        dgk_ref[...] += dgk

    fixed = lambda p: (0, 0)
    pair = pl.BlockSpec((T, LANES), lambda p: (0, p))
    gsp = pl.BlockSpec((1, DH), fixed)
    ckspec = pl.BlockSpec((1, 2, T), lambda p: (p, 0, 0))
    return pl.pallas_call(
        body, name="attn_bwd", grid=(_HP,),
        in_specs=_qkv_specs() + [ckspec, gsp, gsp, pl.BlockSpec((1, T, 2), lambda p: (p, 0, 0)), pair],
        out_specs=[pair, pair, pair, ckspec, gsp, gsp],
        out_shape=[_sds((T, AW), BF16)] * 3 + [_sds((_HP, 2, T)), _sds((1, DH)), _sds((1, DH))],
        scratch_shapes=[pltpu.VMEM((2, T, DH), F32), pltpu.VMEM((2, T, DH), F32), pltpu.VMEM((2, T), F32)],
        compiler_params=_cp(("arbitrary",)),
    )(z, z, z, ck, g_q, g_k, lse, datt)


def _disc(lr, li, ls, br_t, bi_t):
    step = jnp.exp(ls)
    er = jnp.exp(lr * step)
    ab_re = er * jnp.cos(li * step)
    ab_im = er * jnp.sin(li * step)
    num_re = ab_re - 1.0
    num_im = ab_im
    den = lr * lr + li * li
    f_re = (num_re * lr + num_im * li) / den
    f_im = (num_im * lr - num_re * li) / den
    bb_re = f_re[:, None, :] * br_t - f_im[:, None, :] * bi_t
    bb_im = f_re[:, None, :] * bi_t + f_im[:, None, :] * br_t
    return ab_re, ab_im, bb_re, bb_im


def _disc_fwd(lr, li, ls, br_t, bi_t):
    def body(lr_ref, li_ref, ls_ref, br_ref, bi_ref, ar_ref, ai_ref, bbr_ref, bbi_ref):
        ar, ai, bbr, bbi = _disc(lr_ref[...], li_ref[...], ls_ref[...], br_ref[...], bi_ref[...])
        ar_ref[...] = ar
        ai_ref[...] = ai
        bbr_ref[...] = bbr
        bbi_ref[...] = bbi

    return pl.pallas_call(
        body, name="ssm_disc_fwd",
        out_shape=[_sds((NG, NP)), _sds((NG, NP)), _sds((NG, GS, NP)), _sds((NG, GS, NP))],
        compiler_params=_cp())(lr, li, ls, br_t, bi_t)


def _disc_bwd(lr, li, ls, br_t, bi_t, dar, dai, dbbr, dbbi):
    def body(lr_ref, li_ref, ls_ref, br_ref, bi_ref, dar_ref, dai_ref, dbbr_ref, dbbi_ref,
             dlr_ref, dli_ref, dls_ref, dbr_ref, dbi_ref):
        _, vjp = jax.vjp(_disc, lr_ref[...], li_ref[...], ls_ref[...], br_ref[...], bi_ref[...])
        dlr, dli, dls, dbr, dbi = vjp((dar_ref[...], dai_ref[...], dbbr_ref[...], dbbi_ref[...]))
        dlr_ref[...] = dlr
        dli_ref[...] = dli
        dls_ref[...] = dls
        dbr_ref[...] = dbr
        dbi_ref[...] = dbi

    return pl.pallas_call(
        body, name="ssm_disc_bwd",
        out_shape=[_sds((NG, NP)), _sds((NG, NP)), _sds((NG, 1)), _sds((NG, GS, NP)), _sds((NG, GS, NP))],
        compiler_params=_cp())(lr, li, ls, br_t, bi_t, dar, dai, dbbr, dbbi)


def _cmul(ar, ai, br, bi):
    return ar * br - ai * bi, ar * bi + ai * br


def _scan(buf, a_re, a_im, reverse, other=None):
    ar = [jnp.broadcast_to(a_re[:, LANES * k:LANES * (k + 1)], (NSEG, LANES)) for k in range(4)]
    ai = [jnp.broadcast_to(a_im[:, LANES * k:LANES * (k + 1)], (NSEG, LANES)) for k in range(4)]

    def tile(ref, i, k):
        return ref[pl.ds(pl.multiple_of(i * NSEG, NSEG), NSEG), LANES * k:LANES * (k + 1)]

    def advance(i, xr, xi):
        nr = [ar[k] * xr[k] - ai[k] * xi[k] + tile(buf, i, k) for k in range(4)]
        ni = [ar[k] * xi[k] + ai[k] * xr[k] + tile(buf, i, 4 + k) for k in range(4)]
        return nr, ni

    def index(i):
        return (SEG - 1 - i) if reverse else i

    zeros = tuple(jnp.zeros((NSEG, LANES), F32) for _ in range(4))

    def sweep1(i, carry):
        nr, ni = advance(index(i), carry[0], carry[1])
        return tuple(nr), tuple(ni)

    er, ei = lax.fori_loop(0, SEG, sweep1, (zeros, zeros), unroll=4)

    row = lax.broadcasted_iota(jnp.int32, (NSEG, LANES), 0)
    sr, si = [], []
    for k in range(4):
        pr, pi = ar[k], ai[k]
        for _ in range(SEG.bit_length() - 1):
            pr, pi = _cmul(pr, pi, pr, pi)
        ir, ii = er[k], ei[k]
        for d in (1, 2, 4):
            shift = (NSEG - d) if reverse else d
            ok = (row < NSEG - d) if reverse else (row >= d)
            mr, mi = _cmul(pr, pi, pltpu.roll(ir, shift, 0), pltpu.roll(ii, shift, 0))
            ir = ir + jnp.where(ok, mr, 0.0)
            ii = ii + jnp.where(ok, mi, 0.0)
            pr, pi = _cmul(pr, pi, pr, pi)
        shift = (NSEG - 1) if reverse else 1
        ok = (row < NSEG - 1) if reverse else (row >= 1)
        sr.append(jnp.where(ok, pltpu.roll(ir, shift, 0), 0.0))
        si.append(jnp.where(ok, pltpu.roll(ii, shift, 0), 0.0))

    def sweep2(i, carry):
        xr, xi, dar, dai = carry
        idx = index(i)
        if other is not None:
            orr = [tile(other, idx, k) for k in range(4)]
            oi = [tile(other, idx, 4 + k) for k in range(4)]
            dar = tuple(dar[k] + xr[k] * orr[k] + xi[k] * oi[k] for k in range(4))
            dai = tuple(dai[k] + xi[k] * orr[k] - xr[k] * oi[k] for k in range(4))
        nr, ni = advance(idx, xr, xi)
        start = pl.multiple_of(idx * NSEG, NSEG)
        for k in range(4):
            buf[pl.ds(start, NSEG), LANES * k:LANES * (k + 1)] = nr[k]
            buf[pl.ds(start, NSEG), LANES * (4 + k):LANES * (5 + k)] = ni[k]
        return tuple(nr), tuple(ni), dar, dai

    _, _, dar, dai = lax.fori_loop(0, SEG, sweep2, (tuple(sr), tuple(si), zeros, zeros), unroll=4)
    if other is None:
        return None
    da_re = jnp.concatenate([jnp.sum(d, axis=0, keepdims=True) for d in dar], axis=1)
    da_im = jnp.concatenate([jnp.sum(d, axis=0, keepdims=True) for d in dai], axis=1)
    return da_re, da_im


_SSM_BLOCKS = 4


def _ssm_fwd(u, bb, ab, cc, dsk):
    def body(u_ref, bb_ref, a_ref, cc_ref, d_ref, y_ref, xb):
        ub = u_ref[...]
        xb[...] = _nn(ub.astype(BF16), bb_ref[0])
        _scan(xb, a_ref[0, 0:1, :], a_ref[0, 1:2, :], reverse=False)
        y_ref[...] = _nn(xb[...].astype(BF16), cc_ref[0]) + d_ref[0] * ub

    blk = lambda j: (j, 0, 0)
    col = pl.BlockSpec((T, LANES), lambda j: (0, j))
    return pl.pallas_call(
        body, name="ssm_fwd", grid=(_SSM_BLOCKS,),
        in_specs=[col, pl.BlockSpec((1, LANES, 1024), blk), pl.BlockSpec((1, 2, 512), blk),
                  pl.BlockSpec((1, 1024, LANES), blk), pl.BlockSpec((1, 1, LANES), blk)],
        out_specs=col, out_shape=_sds((T, SW)),
        scratch_shapes=[pltpu.VMEM((T, 1024), F32)],
        compiler_params=_cp(("parallel",)),
    )(u, bb, ab, cc, dsk)


def _ssm_bwd(u, dy, bb, ab, cc, dsk):
    def body(u_ref, dy_ref, bb_ref, a_ref, cc_ref, d_ref, du_ref, dbb_ref, da_ref, dcc_ref, dd_ref, xb, gb):
        ub, dyv = u_ref[...], dy_ref[...]
        ubb, dyb = ub.astype(BF16), dyv.astype(BF16)
        a_re, a_im = a_ref[0, 0:1, :], a_ref[0, 1:2, :]
        xb[...] = _nn(ubb, bb_ref[0])
        _scan(xb, a_re, a_im, reverse=False)
        dcc_ref[0] = _tn(xb[...].astype(BF16), dyb)
        gb[...] = _nt(dyb, cc_ref[0])
        da_re, da_im = _scan(gb, a_re, -a_im, reverse=True, other=xb)
        da_ref[0] = jnp.concatenate([da_re, da_im], axis=0)
        gc = gb[...].astype(BF16)
        du_ref[...] = _nt(gc, bb_ref[0]) + d_ref[0] * dyv
        dbb_ref[0] = _tn(ubb, gc)
        dd_ref[0] = jnp.sum(dyv * ub, axis=0, keepdims=True)

    blk = lambda j: (j, 0, 0)
    col = pl.BlockSpec((T, LANES), lambda j: (0, j))
    return pl.pallas_call(
        body, name="ssm_bwd", grid=(_SSM_BLOCKS,),
        in_specs=[col, col, pl.BlockSpec((1, LANES, 1024), blk), pl.BlockSpec((1, 2, 512), blk),
                  pl.BlockSpec((1, 1024, LANES), blk), pl.BlockSpec((1, 1, LANES), blk)],
        out_specs=[col, pl.BlockSpec((1, LANES, 1024), blk), pl.BlockSpec((1, 2, 512), blk),
                   pl.BlockSpec((1, 1024, LANES), blk), pl.BlockSpec((1, 1, LANES), blk)],
        out_shape=[_sds((T, SW)), _sds((4, LANES, 1024)), _sds((4, 2, 512)), _sds((4, 1024, LANES)), _sds((4, 1, LANES))],
        scratch_shapes=[pltpu.VMEM((T, 1024), F32), pltpu.VMEM((T, 1024), F32)],
        compiler_params=_cp(("parallel",)),
    )(u, dy, bb, ab, cc, dsk)


def _interleave(a):
    return a.reshape(NSEG, SEG, a.shape[1]).transpose(1, 0, 2).reshape(a.shape)


def _deinterleave(a):
    return a.reshape(SEG, NSEG, a.shape[1]).transpose(1, 0, 2).reshape(a.shape)


_TM_MIX = 256


def _mix_parts(att, y, wg, bg):
    y2 = jax.nn.gelu(y)
    y2b = y2.astype(BF16)
    sg = jax.nn.sigmoid(_nn(y2b, wg) + bg)
    ssm = y2 * sg
    ra, rs = _rms_r(att), _rms_r(ssm)
    return y2, y2b, sg, ssm, ra, rs


def _mix_fwd(att, y, x, wg, bg, ga, gs, wo):
    def body(att_ref, y_ref, x_ref, wg_ref, bg_ref, ga_ref, gs_ref, wo_ref, x1_ref):
        attv = att_ref[...]
        _, _, _, ssm, ra, rs = _mix_parts(attv, y_ref[...], wg_ref[...], bg_ref[...])
        ma = (attv * ra * ga_ref[...]).astype(BF16)
        ms = (ssm * rs * gs_ref[...]).astype(BF16)
        x1_ref[...] = x_ref[...] + _nn(ma, wo_ref[0:AW, :]) + _nn(ms, wo_ref[AW:D, :])

    row = lambda i: (i, 0)
    fixed = lambda i: (0, 0)
    half = pl.BlockSpec((_TM_MIX, AW), row)
    vec = pl.BlockSpec((1, AW), fixed)
    return pl.pallas_call(
        body, name="mix_fwd", grid=(T // _TM_MIX,),
        in_specs=[half, half, pl.BlockSpec((_TM_MIX, D), row), pl.BlockSpec((SW, SW), fixed), vec, vec, vec,
                  pl.BlockSpec((D, D), fixed)],
        out_specs=pl.BlockSpec((_TM_MIX, D), row), out_shape=_sds((T, D)),
        compiler_params=_cp(("parallel",)),
    )(att, y, x, wg, bg, ga, gs, wo)


def _mix_bwd(att, y, dx1, wg, bg, ga, gs, wo):
    last = T // _TM_MIX - 1

    def body(att_ref, y_ref, d_ref, wg_ref, bg_ref, ga_ref, gs_ref, wo_ref,
             datt_ref, dy_ref, dwg_ref, dbg_ref, dga_ref, dgs_ref, dwo_ref, dwg_acc, dwo_acc):
        attv, yv, db = att_ref[...], y_ref[...], d_ref[...].astype(BF16)
        y2, y2b, sg, ssm, ra, rs = _mix_parts(attv, yv, wg_ref[...], bg_ref[...])
        na, ns = attv * ra, ssm * rs
        ma = (na * ga_ref[...]).astype(BF16)
        ms = (ns * gs_ref[...]).astype(BF16)
        dma = _nt(db, wo_ref[0:AW, :])
        dms = _nt(db, wo_ref[AW:D, :])
        datt, dga = _rms_bwd(na, ra, ga_ref[...], dma)
        dssm, dgs = _rms_bwd(ns, rs, gs_ref[...], dms)
        dgl = dssm * y2 * sg * (1.0 - sg)
        dglb = dgl.astype(BF16)
        dy2 = dssm * sg + _nt(dglb, wg_ref[...])
        _, gelu_vjp = jax.vjp(jax.nn.gelu, yv)
        datt_ref[...] = datt
        dy_ref[...] = gelu_vjp(dy2)[0]

        @pl.when(pl.program_id(0) == 0)
        def _():
            dwg_acc[...] = jnp.zeros_like(dwg_acc)
            dwo_acc[...] = jnp.zeros_like(dwo_acc)
            dbg_ref[...] = jnp.zeros_like(dbg_ref)
            dga_ref[...] = jnp.zeros_like(dga_ref)
            dgs_ref[...] = jnp.zeros_like(dgs_ref)
        dwg_acc[...] += _tn(y2b, dglb)
        dbg_ref[...] += jnp.sum(dgl, axis=0, keepdims=True)
        dga_ref[...] += dga
        dgs_ref[...] += dgs
        dwo_acc[0:AW, :] += _tn(ma, db)
        dwo_acc[AW:D, :] += _tn(ms, db)

        @pl.when(pl.program_id(0) == last)
        def _():
            dwg_ref[...] = dwg_acc[...].astype(BF16)
            dwo_ref[...] = dwo_acc[...].astype(BF16)

    row = lambda i: (i, 0)
    fixed = lambda i: (0, 0)
    half = pl.BlockSpec((_TM_MIX, AW), row)
    vec = pl.BlockSpec((1, AW), fixed)
    sq = pl.BlockSpec((SW, SW), fixed)
    big = pl.BlockSpec((D, D), fixed)
    return pl.pallas_call(
        body, name="mix_bwd", grid=(T // _TM_MIX,),
        in_specs=[half, half, pl.BlockSpec((_TM_MIX, D), row), sq, vec, vec, vec, big],
        out_specs=[half, half, sq, vec, vec, vec, big],
        out_shape=[_sds((T, AW)), _sds((T, SW)), _sds((SW, SW), BF16), _sds((1, SW)), _sds((1, AW)), _sds((1, SW)),
                   _sds((D, D), BF16)],
        scratch_shapes=[pltpu.VMEM((SW, SW), F32), pltpu.VMEM((D, D), F32)],
        compiler_params=_cp(("arbitrary",)),
    )(att, y, dx1, wg, bg, ga, gs, wo)


_NCB = -(-CS // LANES)


def _ffn_up(x1, g, wu4):
    tm = 512

    def body(x_ref, g_ref, w_ref, h_ref, up_ref):
        xv = x_ref[...]
        h = (xv * _rms_r(xv) * g_ref[...]).astype(BF16)
        h_ref[...] = h
        up_ref[0, 0] = _nn(h, w_ref[0, 0])

    return pl.pallas_call(
        body, name="ffn_up", grid=(T // tm, NDEV),
        in_specs=[pl.BlockSpec((tm, D), lambda i, e: (i, 0)), pl.BlockSpec((1, D), lambda i, e: (0, 0)),
                  pl.BlockSpec((1, 1, D, CS), lambda i, e: (e // 4, e % 4, 0, 0))],
        out_specs=[pl.BlockSpec((tm, D), lambda i, e: (i, 0)),
                   pl.BlockSpec((1, 1, tm, CS), lambda i, e: (e // 4, e % 4, i, 0))],
        out_shape=[_sds((T, D), BF16), _sds((2, 4, T, CS))],
        compiler_params=_cp(("parallel", "arbitrary")),
    )(x1, g, wu4)


def _shift_down(h, k):
    row = lax.broadcasted_iota(jnp.int32, h.shape, 0)
    return jnp.where(row >= k, pltpu.roll(h, k, 0), 0.0)


def _shift_up(h, k):
    row = lax.broadcasted_iota(jnp.int32, h.shape, 0)
    return jnp.where(row < T - k, pltpu.roll(h, T - k, 0), 0.0)


def _conv(h, w, b):
    return w[0:1, :] * _shift_down(h, 2) + w[1:2, :] * _shift_down(h, 1) + w[2:3, :] * h + b


def _chan(gv, rows):
    return pl.BlockSpec((1, 1, rows, LANES), lambda d, j: (gv, d, 0, j))


def _ffn_act(up4, cw4, cb4):
    def body(ug_ref, uv_ref, wg_ref, wv_ref, bg_ref, bv_ref, act_ref):
        g = _conv(ug_ref[0, 0], wg_ref[0, 0], bg_ref[0, 0])
        v = _conv(uv_ref[0, 0], wv_ref[0, 0], bv_ref[0, 0])
        act_ref[0] = (g * jax.nn.sigmoid(g) * v).astype(BF16)

    return pl.pallas_call(
        body, name="ffn_act", grid=(4, _NCB),
        in_specs=[_chan(0, T), _chan(1, T), _chan(0, 3), _chan(1, 3), _chan(0, 1), _chan(1, 1)],
        out_specs=pl.BlockSpec((1, T, LANES), lambda d, j: (d, 0, j)), out_shape=_sds((4, T, CS), BF16),
        compiler_params=_cp(("parallel", "parallel")),
    )(up4, up4, cw4, cw4, cb4, cb4)


def _ffn_act_bwd(up4, dact, cw4, cb4):
    def conv_bwd(h, w, d):
        dup = w[2:3, :] * d + w[1:2, :] * _shift_up(d, 1) + w[0:1, :] * _shift_up(d, 2)
        dw = jnp.concatenate([jnp.sum(d * _shift_down(h, 2), axis=0, keepdims=True),
                              jnp.sum(d * _shift_down(h, 1), axis=0, keepdims=True),
                              jnp.sum(d * h, axis=0, keepdims=True)], axis=0)
        return dup, dw, jnp.sum(d, axis=0, keepdims=True)

    def body(ug_ref, uv_ref, da_ref, wg_ref, wv_ref, bg_ref, bv_ref, dup_ref, dw_ref, db_ref):
        ug, uv, da = ug_ref[0, 0], uv_ref[0, 0], da_ref[0]
        g = _conv(ug, wg_ref[0, 0], bg_ref[0, 0])
        v = _conv(uv, wv_ref[0, 0], bv_ref[0, 0])
        sg = jax.nn.sigmoid(g)
        dg = da * v * (sg * (1.0 + g * (1.0 - sg)))
        dv = da * (g * sg)
        dug, dwg, dbg = conv_bwd(ug, wg_ref[0, 0], dg)
        duv, dwv, dbv = conv_bwd(uv, wv_ref[0, 0], dv)
        dup_ref[0, 0] = dug.astype(BF16)
        dup_ref[1, 0] = duv.astype(BF16)
        dw_ref[0, 0] = dwg
        dw_ref[1, 0] = dwv
        db_ref[0, 0] = dbg
        db_ref[1, 0] = dbv

    both = lambda rows: pl.BlockSpec((2, 1, rows, LANES), lambda d, j: (0, d, 0, j))
    return pl.pallas_call(
        body, name="ffn_act_bwd", grid=(4, _NCB),
        in_specs=[_chan(0, T), _chan(1, T), pl.BlockSpec((1, T, LANES), lambda d, j: (d, 0, j)),
                  _chan(0, 3), _chan(1, 3), _chan(0, 1), _chan(1, 1)],
        out_specs=[both(T), both(3), both(1)],
        out_shape=[_sds((2, 4, T, CS), BF16), _sds((2, 4, 3, CS)), _sds((2, 4, 1, CS))],
        compiler_params=_cp(("parallel", "parallel")),
    )(up4, up4, dact, cw4, cw4, cb4, cb4)


def _ffn_down_loss(act, wd4, x1, target):
    tm = 256

    def body(a_ref, w_ref, x1_ref, t_ref, loss_ref, dx_ref, dxb_ref):
        x2 = x1_ref[...]
        for d in range(4):
            x2 = x2 + _nn(a_ref[d], w_ref[d])
        err = x2 - t_ref[...]
        dx = err * (1.0 / D)
        dx_ref[...] = dx
        dxb_ref[...] = dx.astype(BF16)

        @pl.when(pl.program_id(0) == 0)
        def _():
            loss_ref[...] = jnp.zeros_like(loss_ref)
        loss_ref[...] += 0.5 * jnp.sum(jnp.mean(err * err, axis=-1, keepdims=True), axis=0, keepdims=True)

    row = lambda i: (i, 0)
    fixed = lambda i: (0, 0)
    return pl.pallas_call(
        body, name="ffn_down_loss", grid=(T // tm,),
        in_specs=[pl.BlockSpec((4, tm, CS), lambda i: (0, i, 0)), pl.BlockSpec((4, CS, D), lambda i: (0, 0, 0)),
                  pl.BlockSpec((tm, D), row), pl.BlockSpec((tm, D), row)],
        out_specs=[pl.BlockSpec((1, 1), fixed), pl.BlockSpec((tm, D), row), pl.BlockSpec((tm, D), row)],
        out_shape=[_sds((1, 1)), _sds((T, D)), _sds((T, D), BF16)],
        compiler_params=_cp(("arbitrary",)),
    )(act, wd4, x1, target)


def _ffn_dact(dx2b, wd4):
    tm = 512

    def body(d_ref, w_ref, o_ref):
        o_ref[0] = _nt(d_ref[...], w_ref[0])

    return pl.pallas_call(
        body, name="ffn_dact", grid=(4, T // tm),
        in_specs=[pl.BlockSpec((tm, D), lambda d, i: (i, 0)), pl.BlockSpec((1, CS, D), lambda d, i: (d, 0, 0))],
        out_specs=pl.BlockSpec((1, tm, CS), lambda d, i: (d, i, 0)), out_shape=_sds((4, T, CS)),
        compiler_params=_cp(("parallel", "parallel")),
    )(dx2b, wd4)


def _ffn_dwd(act, dx2b):
    def body(a_ref, d_ref, o_ref):
        o_ref[0] = _tn(a_ref[0], d_ref[...]).astype(BF16)

    return pl.pallas_call(
        body, name="ffn_dwd", grid=(4,),
        in_specs=[pl.BlockSpec((1, T, CS), lambda d: (d, 0, 0)), pl.BlockSpec((T, D), lambda d: (0, 0))],
        out_specs=pl.BlockSpec((1, CS, D), lambda d: (d, 0, 0)), out_shape=_sds((4, CS, D), BF16),
        compiler_params=_cp(("parallel",)),
    )(act, dx2b)


def _ffn_dwu(h2, dup4):
    tm = 512

    def body(h_ref, d_ref, o_ref):
        o_ref[0, 0] = _tn(h_ref[...], d_ref[0, 0]).astype(BF16)

    return pl.pallas_call(
        body, name="ffn_dwu", grid=(NDEV, D // tm),
        in_specs=[pl.BlockSpec((T, tm), lambda e, i: (0, i)),
                  pl.BlockSpec((1, 1, T, CS), lambda e, i: (e // 4, e % 4, 0, 0))],
        out_specs=pl.BlockSpec((1, 1, tm, CS), lambda e, i: (e // 4, e % 4, i, 0)),
        out_shape=_sds((2, 4, D, CS), BF16),
        compiler_params=_cp(("parallel", "parallel")),
    )(h2, dup4)


def _ffn_up_bwd(dup4, wu4, x1, g, dres):
    tm = 256

    def body(d_ref, w_ref, x_ref, g_ref, dres_ref, dx_ref, dg_ref):
        dh = jnp.zeros((tm, D), F32)
        for gv in range(2):
            for d in range(4):
                dh = dh + _nt(d_ref[gv, d], w_ref[gv, d])
        xv = x_ref[...]
        r = _rms_r(xv)
        dxr, dgp = _rms_bwd(xv * r, r, g_ref[...], dh)
        dx_ref[...] = dres_ref[...] + dxr

        @pl.when(pl.program_id(0) == 0)
        def _():
            dg_ref[...] = jnp.zeros_like(dg_ref)
        dg_ref[...] += dgp

    row = lambda i: (i, 0)
    fixed = lambda i: (0, 0)
    return pl.pallas_call(
        body, name="ffn_up_bwd", grid=(T // tm,),
        in_specs=[pl.BlockSpec((2, 4, tm, CS), lambda i: (0, 0, i, 0)), pl.BlockSpec((2, 4, D, CS), lambda i: (0, 0, 0, 0)),
                  pl.BlockSpec((tm, D), row), pl.BlockSpec((1, D), fixed), pl.BlockSpec((tm, D), row)],
        out_specs=[pl.BlockSpec((tm, D), row), pl.BlockSpec((1, D), fixed)],
        out_shape=[_sds((T, D)), _sds((1, D))],
        compiler_params=_cp(("arbitrary",)),
    )(dup4, wu4, x1, g, dres)


def _adamw_math(w, g, m, v):
    m = ADAM_B1 * m + (1.0 - ADAM_B1) * g
    v = ADAM_B2 * v + (1.0 - ADAM_B2) * jnp.square(g)
    m_hat = m / (1.0 - ADAM_B1 ** ADAM_STEP)
    v_hat = v / (1.0 - ADAM_B2 ** ADAM_STEP)
    delta = -ADAM_LR * (m_hat / (jnp.sqrt(v_hat) + ADAM_EPS) + ADAM_WD * w)
    return delta, m, v


def _where_am_i():
    x, y, c = _place()
    return jnp.stack([c, 2 * x + y]).astype(jnp.int32)


def _pair_sum(a4, recv, name):
    _, _, rows, cols = a4.shape
    tr = _row_tile(rows, cols, 4, 16)

    def body(sel_ref, own_ref, recv_ref, out_ref):
        out_ref[...] = (own_ref[0].astype(F32) + recv_ref[...].astype(F32)).astype(out_ref.dtype)

    grid_spec = pltpu.PrefetchScalarGridSpec(
        num_scalar_prefetch=1, grid=(4, rows // tr),
        in_specs=[pl.BlockSpec((1, 1, tr, cols), lambda k, i, s: (k, s[0], i, 0)),
                  pl.BlockSpec((1, tr, cols), lambda k, i, s: (k, i, 0))],
        out_specs=pl.BlockSpec((1, tr, cols), lambda k, i, s: (k, i, 0)),
    )
    return pl.pallas_call(
        body, name=name, grid_spec=grid_spec, out_shape=_sds((4, rows, cols), a4.dtype),
        compiler_params=_cp(("arbitrary", "arbitrary")),
    )(_where_am_i(), a4, recv)


def _adamw_rs(a4, recv, from_chips, w, m, v, name):
    rows, cols = w.shape
    tr = _row_tile(rows, cols, 4, 16)

    def body(sel_ref, own_ref, recv_ref, fc_ref, w_ref, m_ref, v_ref, g_ref, d_ref, nm_ref, nv_ref):
        g = own_ref[0, 0].astype(F32) + recv_ref[0].astype(F32)
        for j in range(3):
            g = g + fc_ref[j].astype(F32)
        d, nm, nv = _adamw_math(w_ref[...], g, m_ref[...], v_ref[...])
        g_ref[...] = g
        d_ref[...] = d
        nm_ref[...] = nm
        nv_ref[...] = nv

    flat = pl.BlockSpec((tr, cols), lambda i, s: (i, 0))
    grid_spec = pltpu.PrefetchScalarGridSpec(
        num_scalar_prefetch=1, grid=(rows // tr,),
        in_specs=[pl.BlockSpec((1, 1, tr, cols), lambda i, s: (s[1], s[0], i, 0)),
                  pl.BlockSpec((1, tr, cols), lambda i, s: (s[1], i, 0)),
                  pl.BlockSpec((3, tr, cols), lambda i, s: (0, i, 0)), flat, flat, flat],
        out_specs=[flat] * 4,
    )
    return pl.pallas_call(
        body, name=name, grid_spec=grid_spec, out_shape=[_sds((rows, cols))] * 4,
        compiler_params=_cp(("arbitrary",)),
    )(_where_am_i(), a4, recv, from_chips, w, m, v)


def _adamw_small(gall, w, m, v):
    rows = w.shape[0]

    def body(ga_ref, w_ref, m_ref, v_ref, g_ref, d_ref, nm_ref, nv_ref):
        g = ga_ref[0]
        for dev in range(1, NDEV):
            g = g + ga_ref[dev]
        d, nm, nv = _adamw_math(w_ref[...], g, m_ref[...], v_ref[...])
        g_ref[...] = g
        d_ref[...] = d
        nm_ref[...] = nm
        nv_ref[...] = nv

    return pl.pallas_call(body, name="adamw_small", out_shape=[_sds((rows, LANES))] * 4,
                          compiler_params=_cp())(gall, w, m, v)


def _rows128(a):
    n = a.shape[0]
    r = (-n) % (8 * LANES)
    if r:
        a = jnp.pad(a, [(0, r)])
    return a.reshape(-1, LANES)


_SMALL = (("g_mix", D), ("b_f", NH), ("g_q", DH), ("g_k", DH), ("lambda_re", NG * NP), ("lambda_im", NG * NP),
          ("log_step", NG), ("b_re", NG * NP * GS), ("b_im", NG * NP * GS), ("c_re", NG * GS * NP),
          ("c_im", NG * GS * NP), ("d_skip", NG * GS), ("b_glu", SW), ("g_attn_out", AW), ("g_ssm_out", SW),
          ("g_ffn", D), ("conv_b", 2 * DFF))


def _pack_small(d):
    return jnp.concatenate([_rows128(d[name].reshape(-1).astype(F32)) for name, _ in _SMALL], axis=0)


def _unpack_small(p, shapes):
    out, off = {}, 0
    for name, n in _SMALL:
        rows = -(-n // (8 * LANES)) * 8
        out[name] = p[off:off + rows].reshape(-1)[:n].reshape(shapes[name])
        off += rows
    return out


def _block_diag(m):
    eye = jnp.eye(8, dtype=m.dtype)
    r, c = m.shape[2], m.shape[3]
    return (m[:, :, :, None, :] * eye[None, :, None, :, None]).reshape(4, 8 * r, 8 * c)


def _diag_blocks(dense, r, c):
    eye = jnp.eye(8, dtype=dense.dtype)
    return jnp.sum(dense.reshape(4, 8, r, 8, c) * eye[None, :, None, :, None], axis=3)


def kernel(x, g_mix, w_in, b_f, g_q, g_k, lambda_re, lambda_im, log_step, b_re, b_im, c_re, c_im, d_skip, w_glu, b_glu, g_attn_out, g_ssm_out, w_out, g_ffn, w_up, conv_w, conv_b, w_down, loss_target, m_g_mix, m_w_in, m_b_f, m_g_q, m_g_k, m_lambda_re, m_lambda_im, m_log_step, m_b_re, m_b_im, m_c_re, m_c_im, m_d_skip, m_w_glu, m_b_glu, m_g_attn_out, m_g_ssm_out, m_w_out, m_g_ffn, m_w_up, m_conv_w, m_conv_b, m_w_down, v_g_mix, v_w_in, v_b_f, v_g_q, v_g_k, v_lambda_re, v_lambda_im, v_log_step, v_b_re, v_b_im, v_c_re, v_c_im, v_d_skip, v_w_glu, v_b_glu, v_g_attn_out, v_g_ssm_out, v_w_out, v_g_ffn, v_w_up, v_conv_w, v_conv_b, v_w_down):
    weights = dict(g_mix=g_mix, w_in=w_in, b_f=b_f, g_q=g_q, g_k=g_k, lambda_re=lambda_re, lambda_im=lambda_im,
                   log_step=log_step, b_re=b_re, b_im=b_im, c_re=c_re, c_im=c_im, d_skip=d_skip, w_glu=w_glu,
                   b_glu=b_glu, g_attn_out=g_attn_out, g_ssm_out=g_ssm_out, w_out=w_out, g_ffn=g_ffn, w_up=w_up,
                   conv_w=conv_w, conv_b=conv_b, w_down=w_down)
    mom_m = dict(g_mix=m_g_mix, w_in=m_w_in, b_f=m_b_f, g_q=m_g_q, g_k=m_g_k, lambda_re=m_lambda_re,
                 lambda_im=m_lambda_im, log_step=m_log_step, b_re=m_b_re, b_im=m_b_im, c_re=m_c_re, c_im=m_c_im,
                 d_skip=m_d_skip, w_glu=m_w_glu, b_glu=m_b_glu, g_attn_out=m_g_attn_out, g_ssm_out=m_g_ssm_out,
                 w_out=m_w_out, g_ffn=m_g_ffn, w_up=m_w_up, conv_w=m_conv_w, conv_b=m_conv_b, w_down=m_w_down)
    mom_v = dict(g_mix=v_g_mix, w_in=v_w_in, b_f=v_b_f, g_q=v_g_q, g_k=v_g_k, lambda_re=v_lambda_re,
                 lambda_im=v_lambda_im, log_step=v_log_step, b_re=v_b_re, b_im=v_b_im, c_re=v_c_re, c_im=v_c_im,
                 d_skip=v_d_skip, w_glu=v_w_glu, b_glu=v_b_glu, g_attn_out=v_g_attn_out, g_ssm_out=v_g_ssm_out,
                 w_out=v_w_out, g_ffn=v_g_ffn, w_up=v_w_up, conv_w=v_conv_w, conv_b=v_conv_b, w_down=v_w_down)
    names = list(weights)
    big = ("w_in", "w_glu", "w_out", "w_up", "w_down", "conv_w")

    xs = x[0]
    target = loss_target[0]
    row = lambda a: a.reshape(1, -1)

    win_g, wg_g, wo_g, wu_g, wd_g, cw_g = _all_gather(
        [w_in.astype(BF16), w_glu.astype(BF16), w_out.astype(BF16), w_up.astype(BF16), w_down.astype(BF16), conv_w],
        "gather_weights")
    w_in_f = win_g.transpose(1, 0, 2).reshape(D, 8 * 257)
    wcat = jnp.concatenate([w_in_f[:, 0:1536], w_in_f[:, 1544:2056], w_in_f[:, 1536:1544],
                            jnp.zeros((D, LANES - NH), BF16)], axis=1)
    wg_f = wg_g.reshape(SW, SW)
    wo_f = wo_g.reshape(D, D)
    wu4 = wu_g.reshape(2, 4, D, CS)
    wd4 = wd_g.reshape(4, CS, D)
    cw4 = cw_g.reshape(2, 4, 3, CS)
    cb4 = conv_b.reshape(2, 4, 1, CS)

    h1, z = _in_proj(xs, row(g_mix), wcat)
    f_t = z[:, 2048:2048 + NH].T
    bf_col = b_f.reshape(NH, 1)
    ck = _gates_fwd(f_t, bf_col).reshape(_HP, 2, T)
    att, lse = _attn_fwd(z, ck, row(g_q), row(g_k))

    ls_col = log_step.reshape(NG, 1)
    br_t, bi_t = b_re.transpose(0, 2, 1), b_im.transpose(0, 2, 1)
    ab_re, ab_im, bb_re, bb_im = _disc_fwd(lambda_re, lambda_im, ls_col, br_t, bi_t)
    bb = jnp.concatenate([_block_diag(bb_re.reshape(4, 8, GS, NP)), _block_diag(bb_im.reshape(4, 8, GS, NP))],
                         axis=2).astype(BF16)
    ab = jnp.stack([ab_re.reshape(4, 512), ab_im.reshape(4, 512)], axis=1)
    cdiag = lambda cm: _block_diag(cm.reshape(4, 8, GS, NP).transpose(0, 1, 3, 2))
    cc = jnp.concatenate([cdiag(c_re), -cdiag(c_im)], axis=1).astype(BF16)
    dsk = d_skip.reshape(4, 1, LANES)
    u_il = _interleave(z[:, 1536:2048])
    y = _deinterleave(_ssm_fwd(u_il, bb, ab, cc, dsk))

    x1 = _mix_fwd(att, y, xs, wg_f, row(b_glu), row(g_attn_out), row(g_ssm_out), wo_f)
    h2, up4 = _ffn_up(x1, row(g_ffn), wu4)
    act = _ffn_act(up4, cw4, cb4)
    loss_dev, dx2, dx2b = _ffn_down_loss(act, wd4, x1, target)
    loss = lax.psum(loss_dev[0, 0], ("x", "y", "c"))

    dact = _ffn_dact(dx2b, wd4)
    d_wd = _ffn_dwd(act, dx2b)
    dup4, dcw4, dcb4 = _ffn_act_bwd(up4, dact, cw4, cb4)
    d_wu = _ffn_dwu(h2, dup4)
    dx1, dg_ffn = _ffn_up_bwd(dup4, wu4, x1, row(g_ffn), dx2)
    datt, dy, d_wg, d_bg, d_ga, d_gs, d_wo = _mix_bwd(att, y, dx1, wg_f, row(b_glu), row(g_attn_out),
                                                       row(g_ssm_out), wo_f)
    du_il, dbb, dab, dcc, ddsk = _ssm_bwd(u_il, _interleave(dy), bb, ab, cc, dsk)
    du = _deinterleave(du_il).astype(BF16)
    dbb_re = _diag_blocks(dbb[:, :, 0:512], GS, NP).reshape(NG, GS, NP)
    dbb_im = _diag_blocks(dbb[:, :, 512:1024], GS, NP).reshape(NG, GS, NP)
    dlr, dli, dls, dbr_t, dbi_t = _disc_bwd(lambda_re, lambda_im, ls_col, br_t, bi_t,
                                            dab[:, 0].reshape(NG, NP), dab[:, 1].reshape(NG, NP), dbb_re, dbb_im)
    d_cre = _diag_blocks(dcc[:, 0:512], NP, GS).transpose(0, 1, 3, 2).reshape(NG, GS, NP)
    d_cim = -_diag_blocks(dcc[:, 512:1024], NP, GS).transpose(0, 1, 3, 2).reshape(NG, GS, NP)

    dq, dk, dv, dck, d_gq, d_gk = _attn_bwd(z, ck, row(g_q), row(g_k), lse, datt)
    df_t, d_bf = _gates_bwd(dck.reshape(NH, T), f_t, bf_col)
    df = jnp.concatenate([df_t.T, jnp.zeros((T, LANES - NH), F32)], axis=1).astype(BF16)
    d_wcat = _in_proj_dw(h1, dq, dk, dv, du, df)
    grad_x, dg_mix = _in_proj_bwd(dq, dk, dv, du, df, wcat, xs, row(g_mix), dx1)
    d_win = jnp.concatenate([d_wcat[:, 0:1536], d_wcat[:, 2048:2048 + NH], d_wcat[:, 1536:2048]], axis=1)
    d_win = d_win.astype(BF16).reshape(D, NDEV, 257).transpose(1, 0, 2)

    by_dest = {"w_up": d_wu.reshape(4, 2, D, CS), "w_down": d_wd.reshape(4, 2, DFF // NDEV, D),
               "w_out": d_wo.reshape(4, 2, D // NDEV, D), "w_glu": d_wg.reshape(4, 2, SW // NDEV, SW),
               "w_in": d_win.reshape(4, 2, D, 257), "conv_w": dcw4.reshape(4, 2, 3, CS)}
    order = ("w_up", "w_down", "w_out", "w_glu", "w_in", "conv_w")
    from_sibling = _swap_sibling([by_dest[n] for n in order], "rs_pair_swap")
    chip_sums = [_pair_sum(by_dest[n], r, "rs_pair_sum_" + n) for n, r in zip(order, from_sibling)]
    from_chips = _swap_chips(chip_sums, "rs_chip_swap")

    grads, deltas, new_m, new_v = {}, {}, {}, {}
    for n, r, fc in zip(order, from_sibling, from_chips):
        grads[n], deltas[n], new_m[n], new_v[n] = _adamw_rs(by_dest[n], r, fc, weights[n], mom_m[n], mom_v[n],
                                                            "adamw_" + n)

    small = dict(g_mix=dg_mix, b_f=d_bf, g_q=d_gq, g_k=d_gk, lambda_re=dlr, lambda_im=dli, log_step=dls,
                 b_re=dbr_t.transpose(0, 2, 1), b_im=dbi_t.transpose(0, 2, 1), c_re=d_cre, c_im=d_cim, d_skip=ddsk,
                 b_glu=d_bg, g_attn_out=d_ga, g_ssm_out=d_gs, g_ffn=dg_ffn, conv_b=dcb4)
    (small_all,) = _all_gather([_pack_small(small)], "gather_small_grads")
    sg_p, sd_p, sm_p, sv_p = _adamw_small(small_all, _pack_small({n: weights[n] for n in small}),
                                          _pack_small({n: mom_m[n] for n in small}),
                                          _pack_small({n: mom_v[n] for n in small}))
    shapes = {n: weights[n].shape for n in small}
    sg, sd, sm, sv = (_unpack_small(p, shapes) for p in (sg_p, sd_p, sm_p, sv_p))
    for n in small:
        grads[n], deltas[n], new_m[n], new_v[n] = sg[n], sd[n], sm[n], sv[n]

    return (loss, grad_x[None], *[grads[n] for n in names], *[deltas[n] for n in names],
            *[new_m[n] for n in names], *[new_v[n] for n in names])
```

```python
import jax
import jax.numpy as jnp
from jax import lax
from jax.experimental import pallas as pl
from jax.experimental.pallas import tpu as pltpu

F32, BF16 = jnp.float32, jnp.bfloat16
T, D = 2048, 1024
NH, DH = 8, 64
AW, SW = 512, 512
NG, GS, NP = 32, 16, 64
DFF = 2816
NDEV = 8
CS = 2 * DFF // NDEV
EPS = 1e-6
NEG = -1e30
ZC = 4 * 512 + 128
SEG = 256
NSEG = T // SEG
BQ = 256
VMEM_LIMIT = 56 * 1024 * 1024
LANES = 128
MESH = pl.DeviceIdType.MESH
HI = lax.Precision.HIGHEST

ADAM_LR, ADAM_B1, ADAM_B2, ADAM_EPS, ADAM_WD, ADAM_STEP = 0.001, 0.9, 0.999, 1e-08, 0.01, 10


def _cp(dims=None):
    if dims is None:
        return pltpu.CompilerParams(vmem_limit_bytes=VMEM_LIMIT)
    return pltpu.CompilerParams(dimension_semantics=dims, vmem_limit_bytes=VMEM_LIMIT)


def _sds(shape, dtype=F32):
    return jax.ShapeDtypeStruct(shape, dtype)


def _nt(a, b):
    return lax.dot_general(a, b, (((1,), (1,)), ((), ())), preferred_element_type=F32)


def _tn(a, b):
    return lax.dot_general(a, b, (((0,), (0,)), ((), ())), preferred_element_type=F32)


def _nn(a, b):
    return jnp.dot(a, b, preferred_element_type=F32)


def _rms_r(x):
    return lax.rsqrt(jnp.mean(x * x, axis=-1, keepdims=True) + EPS)


def _rms_bwd(n, r, g, dh):
    dn = dh * g
    dx = r * (dn - n * jnp.mean(dn * n, axis=-1, keepdims=True))
    return dx, jnp.sum(dh * n, axis=0, keepdims=True)


def _row_tile(rows, cols, itemsize, unit):
    tr = rows
    while tr * cols * itemsize > (1 << 20) and tr % (2 * unit) == 0:
        tr //= 2
    return tr


def _place():
    return lax.axis_index("x"), lax.axis_index("y"), lax.axis_index("c")


def _all_gather(shards, name):
    n = len(shards)

    def body(*refs):
        x_refs, out_refs = refs[:n], refs[n:2 * n]
        send_sems, recv_sems, local_sems = refs[2 * n:]
        x, y, c = _place()
        me, sibling = (x, y, c), (x, y, 1 - c)
        chips = [(1 - x, y), (x, 1 - y), (1 - x, 1 - y)]

        def copy(a, k, block, to, src=None):
            px, py, pc = block
            slot = out_refs[a].at[4 * px + 2 * py + pc]
            return pltpu.make_async_remote_copy(
                src_ref=slot if src is None else src, dst_ref=slot,
                send_sem=send_sems.at[7 * a + k], recv_sem=recv_sems.at[7 * a + k], device_id=to, device_id_type=MESH)

        mine, first, passed = [], [], []
        for a in range(n):
            cp = pltpu.make_async_copy(x_refs[a], out_refs[a].at[4 * x + 2 * y + c], local_sems.at[a])
            cp.start()
            mine.append(cp)
            cps = [copy(a, 0, me, sibling, src=x_refs[a])]
            cps += [copy(a, 1 + j, me, (*chip, c), src=x_refs[a]) for j, chip in enumerate(chips)]
            for cp in cps:
                cp.start()
            first += cps
        for a in range(n):
            for j, chip in enumerate(chips):
                copy(a, 1 + j, (*chip, c), me).wait_recv()
                fwd = copy(a, 4 + j, (*chip, c), sibling)
                fwd.start()
                passed.append(fwd)
        for a in range(n):
            copy(a, 0, sibling, me).wait_recv()
            for j, chip in enumerate(chips):
                copy(a, 4 + j, (*chip, 1 - c), me).wait_recv()
        for cp in first + passed:
            cp.wait_send()
        for cp in mine:
            cp.wait()

    anyspec = pl.BlockSpec(memory_space=pl.ANY)
    return pl.pallas_call(
        body, name=name,
        out_shape=[_sds((NDEV, *s.shape), s.dtype) for s in shards],
        in_specs=[anyspec] * n, out_specs=[anyspec] * n,
        scratch_shapes=[pltpu.SemaphoreType.DMA((7 * n,)), pltpu.SemaphoreType.DMA((7 * n,)),
                        pltpu.SemaphoreType.DMA((n,))],
    )(*shards)


def _gather_first_copies(x_refs, out_refs, send_sems, recv_sems, local_sems):
    x, y, c = _place()
    slot = 4 * x + 2 * y + c
    peers = [(x, y, 1 - c), (1 - x, y, c), (x, 1 - y, c), (1 - x, 1 - y, c)]
    local, remote = [], []
    for a, (x_ref, out_ref) in enumerate(zip(x_refs, out_refs)):
        local.append(pltpu.make_async_copy(x_ref, out_ref.at[slot], local_sems.at[a]))
        for k, peer in enumerate(peers):
            remote.append(pltpu.make_async_remote_copy(
                src_ref=x_ref, dst_ref=out_ref.at[slot], send_sem=send_sems.at[4 * a + k],
                recv_sem=recv_sems.at[4 * a + k], device_id=peer, device_id_type=MESH))
    return local, remote


def _gather_first_scratch(n):
    return [pltpu.SemaphoreType.DMA((4 * n,)), pltpu.SemaphoreType.DMA((4 * n,)), pltpu.SemaphoreType.DMA((n,))]


def _gather_second(bufs, name):
    n = len(bufs)

    def body(*refs):
        out_refs = refs[n:2 * n]
        send_sems, recv_sems = refs[2 * n:]
        x, y, c = _place()
        chips = [(1 - x, y), (x, 1 - y), (1 - x, 1 - y)]
        cps = []
        for a in range(n):
            for j, (px, py) in enumerate(chips):
                block = out_refs[a].at[4 * px + 2 * py + c]
                cps.append(pltpu.make_async_remote_copy(
                    src_ref=block, dst_ref=block, send_sem=send_sems.at[3 * a + j], recv_sem=recv_sems.at[3 * a + j],
                    device_id=(x, y, 1 - c), device_id_type=MESH))
        for cp in cps:
            cp.start()
        for cp in cps:
            cp.wait()

    anyspec = pl.BlockSpec(memory_space=pl.ANY)
    return pl.pallas_call(
        body, name=name,
        out_shape=[_sds(b.shape, b.dtype) for b in bufs],
        in_specs=[anyspec] * n, out_specs=[anyspec] * n,
        input_output_aliases={a: a for a in range(n)},
        scratch_shapes=[pltpu.SemaphoreType.DMA((3 * n,)), pltpu.SemaphoreType.DMA((3 * n,))],
    )(*bufs)


def _chip_swap_copies(p_refs, out_refs, send_sems, recv_sems):
    x, y, c = _place()
    chips = [(1 - x, y), (x, 1 - y), (1 - x, 1 - y)]
    cps = []
    for a, (p_ref, out_ref) in enumerate(zip(p_refs, out_refs)):
        for j, (px, py) in enumerate(chips):
            cps.append(pltpu.make_async_remote_copy(
                src_ref=p_ref.at[2 * px + py], dst_ref=out_ref.at[j],
                send_sem=send_sems.at[3 * a + j], recv_sem=recv_sems.at[3 * a + j],
                device_id=(px, py, c), device_id_type=MESH))
    return cps


def _swap_sibling(arrs, name):
    n = len(arrs)

    def body(*refs):
        a_refs, out_refs = refs[:n], refs[n:2 * n]
        send_sems, recv_sems = refs[2 * n:]
        x, y, c = _place()
        cps = []
        for a in range(n):
            for k in range(4):
                cps.append(pltpu.make_async_remote_copy(
                    src_ref=a_refs[a].at[k, 1 - c], dst_ref=out_refs[a].at[k],
                    send_sem=send_sems.at[4 * a + k], recv_sem=recv_sems.at[4 * a + k],
                    device_id=(x, y, 1 - c), device_id_type=MESH))
        for cp in cps:
            cp.start()
        for cp in cps:
            cp.wait()

    anyspec = pl.BlockSpec(memory_space=pl.ANY)
    return pl.pallas_call(
        body, name=name,
        out_shape=[_sds((4, *a.shape[2:]), a.dtype) for a in arrs],
        in_specs=[anyspec] * n, out_specs=[anyspec] * n,
        scratch_shapes=[pltpu.SemaphoreType.DMA((4 * n,)), pltpu.SemaphoreType.DMA((4 * n,))],
    )(*arrs)


def _swap_chips(parts, name):
    n = len(parts)

    def body(*refs):
        cps = _chip_swap_copies(refs[:n], refs[n:2 * n], *refs[2 * n:])
        for cp in cps:
            cp.start()
        for cp in cps:
            cp.wait()

    anyspec = pl.BlockSpec(memory_space=pl.ANY)
    return pl.pallas_call(
        body, name=name,
        out_shape=[_sds((3, *p.shape[1:]), p.dtype) for p in parts],
        in_specs=[anyspec] * n, out_specs=[anyspec] * n,
        scratch_shapes=[pltpu.SemaphoreType.DMA((3 * n,)), pltpu.SemaphoreType.DMA((3 * n,))],
    )(*parts)


def _in_proj(x, g, wcat):
    tm = 256

    def body(x_ref, g_ref, w_ref, h_ref, z_ref, ft_ref):
        xv = x_ref[...]
        h = (xv * _rms_r(xv) * g_ref[...]).astype(BF16)
        h_ref[...] = h
        z = _nn(h, w_ref[...])
        z_ref[...] = z
        ft_ref[...] = z[:, 2048:ZC].T[0:NH, :]

    row = lambda i: (i, 0)
    fixed = lambda i: (0, 0)
    return pl.pallas_call(
        body, name="in_proj", grid=(T // tm,),
        in_specs=[pl.BlockSpec((tm, D), row), pl.BlockSpec((1, D), fixed), pl.BlockSpec((D, ZC), fixed)],
        out_specs=[pl.BlockSpec((tm, D), row), pl.BlockSpec((tm, ZC), row), pl.BlockSpec((NH, tm), lambda i: (0, i))],
        out_shape=[_sds((T, D), BF16), _sds((T, ZC)), _sds((NH, T))],
        compiler_params=_cp(("parallel",)),
    )(x, g, wcat)


def _in_proj_dw(h, dq, dk, dv, du, df):
    tm = 512

    def body(h_ref, dq_ref, dk_ref, dv_ref, du_ref, df_ref, o_ref):
        hv = h_ref[...]
        for j, d_ref in enumerate((dq_ref, dk_ref, dv_ref, du_ref)):
            o_ref[:, 512 * j:512 * (j + 1)] = _tn(hv, d_ref[...])
        o_ref[:, 2048:ZC] = _tn(hv, df_ref[...])

    full = lambda w: pl.BlockSpec((T, w), lambda i: (0, 0))
    return pl.pallas_call(
        body, name="in_proj_dw", grid=(D // tm,),
        in_specs=[pl.BlockSpec((T, tm), lambda i: (0, i)), full(512), full(512), full(512), full(512), full(LANES)],
        out_specs=pl.BlockSpec((tm, ZC), lambda i: (i, 0)), out_shape=_sds((D, ZC)),
        compiler_params=_cp(("parallel",)),
    )(h, dq, dk, dv, du, df)


def _in_proj_bwd(dq, dk, dv, du, df, wcat, x, g, dres):
    tm = 256

    def body(dq_ref, dk_ref, dv_ref, du_ref, df_ref, w_ref, x_ref, g_ref, dres_ref, dx_ref, dg_ref):
        dh = _nt(df_ref[...], w_ref[:, 2048:ZC])
        for j, d_ref in enumerate((dq_ref, dk_ref, dv_ref, du_ref)):
            dh = dh + _nt(d_ref[...], w_ref[:, 512 * j:512 * (j + 1)])
        xv = x_ref[...]
        r = _rms_r(xv)
        dxr, dgp = _rms_bwd(xv * r, r, g_ref[...], dh)
        dx_ref[...] = dres_ref[...] + dxr

        @pl.when(pl.program_id(0) == 0)
        def _():
            dg_ref[...] = jnp.zeros_like(dg_ref)
        dg_ref[...] += dgp

    row = lambda i: (i, 0)
    fixed = lambda i: (0, 0)
    part = pl.BlockSpec((tm, 512), row)
    return pl.pallas_call(
        body, name="in_proj_bwd", grid=(T // tm,),
        in_specs=[part, part, part, part, pl.BlockSpec((tm, LANES), row), pl.BlockSpec((D, ZC), fixed),
                  pl.BlockSpec((tm, D), row), pl.BlockSpec((1, D), fixed), pl.BlockSpec((tm, D), row)],
        out_specs=[pl.BlockSpec((tm, D), row), pl.BlockSpec((1, D), fixed)],
        out_shape=[_sds((T, D)), _sds((1, D))],
        compiler_params=_cp(("arbitrary",)),
    )(dq, dk, dv, du, df, wcat, x, g, dres)


def _stack_lanes(v):
    return jnp.concatenate([v[:, LANES * b:LANES * (b + 1)] for b in range(T // LANES)], axis=0)


def _unstack_lanes(s):
    return jnp.concatenate([s[8 * b:8 * b + 8, :] for b in range(T // LANES)], axis=1)


def _cumsum_lanes(v, reverse):
    st = _stack_lanes(v)
    ri = lax.broadcasted_iota(jnp.int32, (LANES, LANES), 0)
    ci = lax.broadcasted_iota(jnp.int32, (LANES, LANES), 1)
    tri = (ri >= ci) if reverse else (ri <= ci)
    intra = jnp.dot(st, tri.astype(F32), precision=HI, preferred_element_type=F32)
    tot = intra[:, 0:1] if reverse else intra[:, LANES - 1:LANES]
    same_head = (ci % 8) == (ri % 8)
    other = (ci // 8 > ri // 8) if reverse else (ci // 8 < ri // 8)
    off = jnp.dot((same_head & other).astype(F32), jnp.broadcast_to(tot, (LANES, LANES)), precision=HI,
                  preferred_element_type=F32)
    return _unstack_lanes(intra + off)


def _gates_fwd(f_t, b_f):
    def body(f_ref, b_ref, c_ref):
        z = f_ref[...] + b_ref[...]
        lf = jnp.minimum(z, 0.0) - jnp.log(1.0 + jnp.exp(-jnp.abs(z)))
        c_ref[...] = _cumsum_lanes(lf, reverse=False)

    return pl.pallas_call(body, name="gates_fwd", out_shape=_sds((NH, T)), compiler_params=_cp())(f_t, b_f)


def _gates_bwd(dck, f_t, b_f):
    def body(d_ref, f_ref, b_ref, df_ref, db_ref):
        z = f_ref[...] + b_ref[...]
        dlf = _cumsum_lanes(d_ref[...], reverse=True)
        df = dlf * jax.nn.sigmoid(-z)
        df_ref[...] = df
        db_ref[...] = jnp.sum(df, axis=1, keepdims=True)

    return pl.pallas_call(body, name="gates_bwd", out_shape=[_sds((NH, T)), _sds((NH, 1))],
                          compiler_params=_cp())(dck, f_t, b_f)


def _causal(s, row0, ncol):
    rows = row0 + lax.broadcasted_iota(jnp.int32, (BQ, ncol), 0)
    cols = lax.broadcasted_iota(jnp.int32, (BQ, ncol), 1)
    return jnp.where(cols <= rows, s, NEG)


_HP = NH // 2


def _qkv_specs():
    return [pl.BlockSpec((T, LANES), lambda p: (0, p)), pl.BlockSpec((T, LANES), lambda p: (0, _HP + p)),
            pl.BlockSpec((T, LANES), lambda p: (0, 2 * _HP + p))]


def _attn_fwd(z, ck, g_q, g_k, shards):
    scale = DH ** -0.5
    n = len(shards)

    def body(q_ref, k_ref, v_ref, ck_ref, gq_ref, gk_ref, *rest):
        x_refs, (o_ref, lse_ref), out_refs, sems = rest[:n], rest[n:n + 2], rest[n + 2:2 * n + 2], rest[2 * n + 2:]

        @pl.when(pl.program_id(0) == 0)
        def _():
            local, remote = _gather_first_copies(x_refs, out_refs, *sems)
            for cp in local + remote:
                cp.start()

        qb, kb, vb = [], [], []
        for hh in range(2):
            cols = slice(DH * hh, DH * (hh + 1))
            qv, kv = q_ref[:, cols], k_ref[:, cols]
            qb.append((qv * _rms_r(qv) * gq_ref[...]).astype(BF16))
            kb.append((kv * _rms_r(kv) * gk_ref[...]).astype(BF16))
            vb.append(v_ref[:, cols].astype(BF16))
        for i in range(T // BQ):
            lo, hi = i * BQ, (i + 1) * BQ
            outs = []
            for hh in range(2):
                s = _nt(qb[hh][lo:hi], kb[hh][:hi]) * scale - ck_ref[0, hh:hh + 1, :hi]
                s = _causal(s, lo, hi)
                m = jnp.max(s, axis=-1, keepdims=True)
                p = jnp.exp(s - m)
                l = jnp.sum(p, axis=-1, keepdims=True)
                outs.append(_nn(p.astype(BF16), vb[hh][:hi]) / l)
                lse_ref[0, lo:hi, hh:hh + 1] = m + jnp.log(l)
            o_ref[lo:hi, :] = jnp.concatenate(outs, axis=1)

        @pl.when(pl.program_id(0) == _HP - 1)
        def _():
            local, remote = _gather_first_copies(x_refs, out_refs, *sems)
            for cp in remote + local:
                cp.wait()

    fixed = lambda p: (0, 0)
    anyspec = pl.BlockSpec(memory_space=pl.ANY)
    res = pl.pallas_call(
        body, name="attn_fwd", grid=(_HP,),
        in_specs=_qkv_specs() + [pl.BlockSpec((1, 2, T), lambda p: (p, 0, 0)), pl.BlockSpec((1, DH), fixed),
                                 pl.BlockSpec((1, DH), fixed)] + [anyspec] * n,
        out_specs=[pl.BlockSpec((T, LANES), lambda p: (0, p)), pl.BlockSpec((1, T, 2), lambda p: (p, 0, 0))]
        + [anyspec] * n,
        out_shape=[_sds((T, AW)), _sds((_HP, T, 2))] + [_sds((NDEV, *s.shape), s.dtype) for s in shards],
        scratch_shapes=_gather_first_scratch(n),
        compiler_params=_cp(("arbitrary",)),
    )(z, z, z, ck, g_q, g_k, *shards)
    return res[0], res[1], res[2:]


def _attn_bwd(z, ck, g_q, g_k, lse, datt, parts):
    scale = DH ** -0.5
    n = len(parts)

    def body(q_ref, k_ref, v_ref, ck_ref, gq_ref, gk_ref, lse_ref, do_ref, *rest):
        p_refs, rest = rest[:n], rest[n:]
        dq_ref, dk_ref, dv_ref, dck_ref, dgq_ref, dgk_ref = rest[:6]
        fc_refs, (dkn_acc, dv_acc, dc_acc), sems = rest[6:6 + n], rest[6 + n:9 + n], rest[9 + n:]

        @pl.when(pl.program_id(0) == 0)
        def _():
            for cp in _chip_swap_copies(p_refs, fc_refs, *sems):
                cp.start()

        nq, nk, rq, rk, qb, kb, vb, dob = [], [], [], [], [], [], [], []
        for hh in range(2):
            cols = slice(DH * hh, DH * (hh + 1))
            qv, kv = q_ref[:, cols], k_ref[:, cols]
            rq.append(_rms_r(qv))
            rk.append(_rms_r(kv))
            nq.append(qv * rq[hh])
            nk.append(kv * rk[hh])
            qb.append((nq[hh] * gq_ref[...]).astype(BF16))
            kb.append((nk[hh] * gk_ref[...]).astype(BF16))
            vb.append(v_ref[:, cols].astype(BF16))
            dob.append(do_ref[:, cols].astype(BF16))
        dkn_acc[...] = jnp.zeros_like(dkn_acc)
        dv_acc[...] = jnp.zeros_like(dv_acc)
        dc_acc[...] = jnp.zeros_like(dc_acc)
        dgq = jnp.zeros((1, DH), F32)
        for i in range(T // BQ):
            lo, hi = i * BQ, (i + 1) * BQ
            dqs = []
            for hh in range(2):
                s = _nt(qb[hh][lo:hi], kb[hh][:hi]) * scale - ck_ref[0, hh:hh + 1, :hi]
                s = _causal(s, lo, hi)
                p = jnp.exp(s - lse_ref[0, lo:hi, hh:hh + 1])
                dp = _nt(dob[hh][lo:hi], vb[hh][:hi])
                ds = p * (dp - jnp.sum(p * dp, axis=-1, keepdims=True))
                dsb = ds.astype(BF16)
                dv_acc[hh, 0:hi, :] += _tn(p.astype(BF16), dob[hh][lo:hi])
                dkn_acc[hh, 0:hi, :] += _tn(dsb, qb[hh][lo:hi]) * scale
                dc_acc[hh:hh + 1, 0:hi] -= jnp.sum(ds, axis=0, keepdims=True)
                dqn = _nn(dsb, kb[hh][:hi]) * scale
                dqx, dgp = _rms_bwd(nq[hh][lo:hi], rq[hh][lo:hi], gq_ref[...], dqn)
                dqs.append(dqx)
                dgq = dgq + dgp
            dq_ref[lo:hi, :] = jnp.concatenate(dqs, axis=1).astype(BF16)
        dks, dgk = [], jnp.zeros((1, DH), F32)
        for hh in range(2):
            dkx, dgp = _rms_bwd(nk[hh], rk[hh], gk_ref[...], dkn_acc[hh])
            dks.append(dkx)
            dgk = dgk + dgp
        dk_ref[...] = jnp.concatenate(dks, axis=1).astype(BF16)
        dv_ref[...] = jnp.concatenate([dv_acc[0], dv_acc[1]], axis=1).astype(BF16)
        dck_ref[0] = dc_acc[...]

        @pl.when(pl.program_id(0) == 0)
        def _():
            dgq_ref[...] = jnp.zeros_like(dgq_ref)
            dgk_ref[...] = jnp.zeros_like(dgk_ref)
        dgq_ref[...] += dgq
        dgk_ref[...] += dgk

        @pl.when(pl.program_id(0) == _HP - 1)
        def _():
            for cp in _chip_swap_copies(p_refs, fc_refs, *sems):
                cp.wait()

    fixed = lambda p: (0, 0)
    pair = pl.BlockSpec((T, LANES), lambda p: (0, p))
    gsp = pl.BlockSpec((1, DH), fixed)
    ckspec = pl.BlockSpec((1, 2, T), lambda p: (p, 0, 0))
    anyspec = pl.BlockSpec(memory_space=pl.ANY)
    res = pl.pallas_call(
        body, name="attn_bwd", grid=(_HP,),
        in_specs=_qkv_specs() + [ckspec, gsp, gsp, pl.BlockSpec((1, T, 2), lambda p: (p, 0, 0)), pair] + [anyspec] * n,
        out_specs=[pair, pair, pair, ckspec, gsp, gsp] + [anyspec] * n,
        out_shape=[_sds((T, AW), BF16)] * 3 + [_sds((_HP, 2, T)), _sds((1, DH)), _sds((1, DH))]
        + [_sds((3, *p.shape[1:]), p.dtype) for p in parts],
        scratch_shapes=[pltpu.VMEM((2, T, DH), F32), pltpu.VMEM((2, T, DH), F32), pltpu.VMEM((2, T), F32),
                        pltpu.SemaphoreType.DMA((3 * n,)), pltpu.SemaphoreType.DMA((3 * n,))],
        compiler_params=_cp(("arbitrary",)),
    )(z, z, z, ck, g_q, g_k, lse, datt, *parts)
    return (*res[:6], res[6:])


def _disc(lr, li, ls, br_t, bi_t):
    step = jnp.exp(ls)
    er = jnp.exp(lr * step)
    ab_re = er * jnp.cos(li * step)
    ab_im = er * jnp.sin(li * step)
    num_re = ab_re - 1.0
    num_im = ab_im
    den = lr * lr + li * li
    f_re = (num_re * lr + num_im * li) / den
    f_im = (num_im * lr - num_re * li) / den
    bb_re = f_re[:, None, :] * br_t - f_im[:, None, :] * bi_t
    bb_im = f_re[:, None, :] * bi_t + f_im[:, None, :] * br_t
    return ab_re, ab_im, bb_re, bb_im


def _disc_fwd(lr, li, ls, br_t, bi_t):
    def body(lr_ref, li_ref, ls_ref, br_ref, bi_ref, ar_ref, ai_ref, bbr_ref, bbi_ref):
        ar, ai, bbr, bbi = _disc(lr_ref[...], li_ref[...], ls_ref[...], br_ref[...], bi_ref[...])
        ar_ref[...] = ar
        ai_ref[...] = ai
        bbr_ref[...] = bbr
        bbi_ref[...] = bbi

    return pl.pallas_call(
        body, name="ssm_disc_fwd",
        out_shape=[_sds((NG, NP)), _sds((NG, NP)), _sds((NG, GS, NP)), _sds((NG, GS, NP))],
        compiler_params=_cp())(lr, li, ls, br_t, bi_t)


def _disc_bwd(lr, li, ls, br_t, bi_t, dar, dai, dbbr, dbbi):
    def body(lr_ref, li_ref, ls_ref, br_ref, bi_ref, dar_ref, dai_ref, dbbr_ref, dbbi_ref,
             dlr_ref, dli_ref, dls_ref, dbr_ref, dbi_ref):
        _, vjp = jax.vjp(_disc, lr_ref[...], li_ref[...], ls_ref[...], br_ref[...], bi_ref[...])
        dlr, dli, dls, dbr, dbi = vjp((dar_ref[...], dai_ref[...], dbbr_ref[...], dbbi_ref[...]))
        dlr_ref[...] = dlr
        dli_ref[...] = dli
        dls_ref[...] = dls
        dbr_ref[...] = dbr
        dbi_ref[...] = dbi

    return pl.pallas_call(
        body, name="ssm_disc_bwd",
        out_shape=[_sds((NG, NP)), _sds((NG, NP)), _sds((NG, 1)), _sds((NG, GS, NP)), _sds((NG, GS, NP))],
        compiler_params=_cp())(lr, li, ls, br_t, bi_t, dar, dai, dbbr, dbbi)


def _cmul(ar, ai, br, bi):
    return ar * br - ai * bi, ar * bi + ai * br


def _scan(buf, a_re, a_im, reverse, other=None):
    ar = [jnp.broadcast_to(a_re[:, LANES * k:LANES * (k + 1)], (NSEG, LANES)) for k in range(4)]
    ai = [jnp.broadcast_to(a_im[:, LANES * k:LANES * (k + 1)], (NSEG, LANES)) for k in range(4)]

    def tile(ref, i, k):
        return ref[pl.ds(pl.multiple_of(i * NSEG, NSEG), NSEG), LANES * k:LANES * (k + 1)]

    def advance(i, xr, xi):
        nr = [ar[k] * xr[k] - ai[k] * xi[k] + tile(buf, i, k) for k in range(4)]
        ni = [ar[k] * xi[k] + ai[k] * xr[k] + tile(buf, i, 4 + k) for k in range(4)]
        return nr, ni

    def index(i):
        return (SEG - 1 - i) if reverse else i

    zeros = tuple(jnp.zeros((NSEG, LANES), F32) for _ in range(4))

    def sweep1(i, carry):
        nr, ni = advance(index(i), carry[0], carry[1])
        return tuple(nr), tuple(ni)

    er, ei = lax.fori_loop(0, SEG, sweep1, (zeros, zeros), unroll=4)

    row = lax.broadcasted_iota(jnp.int32, (NSEG, LANES), 0)
    sr, si = [], []
    for k in range(4):
        pr, pi = ar[k], ai[k]
        for _ in range(SEG.bit_length() - 1):
            pr, pi = _cmul(pr, pi, pr, pi)
        ir, ii = er[k], ei[k]
        for d in (1, 2, 4):
            shift = (NSEG - d) if reverse else d
            ok = (row < NSEG - d) if reverse else (row >= d)
            mr, mi = _cmul(pr, pi, pltpu.roll(ir, shift, 0), pltpu.roll(ii, shift, 0))
            ir = ir + jnp.where(ok, mr, 0.0)
            ii = ii + jnp.where(ok, mi, 0.0)
            pr, pi = _cmul(pr, pi, pr, pi)
        shift = (NSEG - 1) if reverse else 1
        ok = (row < NSEG - 1) if reverse else (row >= 1)
        sr.append(jnp.where(ok, pltpu.roll(ir, shift, 0), 0.0))
        si.append(jnp.where(ok, pltpu.roll(ii, shift, 0), 0.0))

    def sweep2(i, carry):
        xr, xi, dar, dai = carry
        idx = index(i)
        if other is not None:
            orr = [tile(other, idx, k) for k in range(4)]
            oi = [tile(other, idx, 4 + k) for k in range(4)]
            dar = tuple(dar[k] + xr[k] * orr[k] + xi[k] * oi[k] for k in range(4))
            dai = tuple(dai[k] + xi[k] * orr[k] - xr[k] * oi[k] for k in range(4))
        nr, ni = advance(idx, xr, xi)
        start = pl.multiple_of(idx * NSEG, NSEG)
        for k in range(4):
            buf[pl.ds(start, NSEG), LANES * k:LANES * (k + 1)] = nr[k]
            buf[pl.ds(start, NSEG), LANES * (4 + k):LANES * (5 + k)] = ni[k]
        return tuple(nr), tuple(ni), dar, dai

    _, _, dar, dai = lax.fori_loop(0, SEG, sweep2, (tuple(sr), tuple(si), zeros, zeros), unroll=4)
    if other is None:
        return None
    da_re = jnp.concatenate([jnp.sum(d, axis=0, keepdims=True) for d in dar], axis=1)
    da_im = jnp.concatenate([jnp.sum(d, axis=0, keepdims=True) for d in dai], axis=1)
    return da_re, da_im


_SSM_BLOCKS = 4


def _ssm_fwd(u, bb, ab, cc, dsk, shards):
    n = len(shards)

    def body(u_ref, bb_ref, a_ref, cc_ref, d_ref, *rest):
        x_refs, y_ref, out_refs, xb, sems = rest[:n], rest[n], rest[n + 1:2 * n + 1], rest[2 * n + 1], rest[2 * n + 2:]

        @pl.when(pl.program_id(0) == 0)
        def _():
            local, remote = _gather_first_copies(x_refs, out_refs, *sems)
            for cp in local + remote:
                cp.start()

        ub = u_ref[...]
        xb[...] = _nn(ub.astype(BF16), bb_ref[0])
        _scan(xb, a_ref[0, 0:1, :], a_ref[0, 1:2, :], reverse=False)
        y_ref[...] = _nn(xb[...].astype(BF16), cc_ref[0]) + d_ref[0] * ub

        @pl.when(pl.program_id(0) == _SSM_BLOCKS - 1)
        def _():
            local, remote = _gather_first_copies(x_refs, out_refs, *sems)
            for cp in remote + local:
                cp.wait()

    blk = lambda j: (j, 0, 0)
    col = pl.BlockSpec((T, LANES), lambda j: (0, j))
    anyspec = pl.BlockSpec(memory_space=pl.ANY)
    res = pl.pallas_call(
        body, name="ssm_fwd", grid=(_SSM_BLOCKS,),
        in_specs=[col, pl.BlockSpec((1, LANES, 1024), blk), pl.BlockSpec((1, 2, 512), blk),
                  pl.BlockSpec((1, 1024, LANES), blk), pl.BlockSpec((1, 1, LANES), blk)] + [anyspec] * n,
        out_specs=[col] + [anyspec] * n,
        out_shape=[_sds((T, SW))] + [_sds((NDEV, *s.shape), s.dtype) for s in shards],
        scratch_shapes=[pltpu.VMEM((T, 1024), F32)] + _gather_first_scratch(n),
        compiler_params=_cp(("arbitrary",)),
    )(u, bb, ab, cc, dsk, *shards)
    return res[0], res[1:]


def _ssm_bwd(u, dy, bb, ab, cc, dsk):
    def body(u_ref, dy_ref, bb_ref, a_ref, cc_ref, d_ref, du_ref, dbb_ref, da_ref, dcc_ref, dd_ref, xb, gb):
        ub, dyv = u_ref[...], dy_ref[...]
        ubb, dyb = ub.astype(BF16), dyv.astype(BF16)
        a_re, a_im = a_ref[0, 0:1, :], a_ref[0, 1:2, :]
        xb[...] = _nn(ubb, bb_ref[0])
        _scan(xb, a_re, a_im, reverse=False)
        dcc_ref[0] = _tn(xb[...].astype(BF16), dyb)
        gb[...] = _nt(dyb, cc_ref[0])
        da_re, da_im = _scan(gb, a_re, -a_im, reverse=True, other=xb)
        da_ref[0] = jnp.concatenate([da_re, da_im], axis=0)
        gc = gb[...].astype(BF16)
        du_ref[...] = _nt(gc, bb_ref[0]) + d_ref[0] * dyv
        dbb_ref[0] = _tn(ubb, gc)
        dd_ref[0] = jnp.sum(dyv * ub, axis=0, keepdims=True)

    blk = lambda j: (j, 0, 0)
    col = pl.BlockSpec((T, LANES), lambda j: (0, j))
    return pl.pallas_call(
        body, name="ssm_bwd", grid=(_SSM_BLOCKS,),
        in_specs=[col, col, pl.BlockSpec((1, LANES, 1024), blk), pl.BlockSpec((1, 2, 512), blk),
                  pl.BlockSpec((1, 1024, LANES), blk), pl.BlockSpec((1, 1, LANES), blk)],
        out_specs=[col, pl.BlockSpec((1, LANES, 1024), blk), pl.BlockSpec((1, 2, 512), blk),
                   pl.BlockSpec((1, 1024, LANES), blk), pl.BlockSpec((1, 1, LANES), blk)],
        out_shape=[_sds((T, SW)), _sds((4, LANES, 1024)), _sds((4, 2, 512)), _sds((4, 1024, LANES)), _sds((4, 1, LANES))],
        scratch_shapes=[pltpu.VMEM((T, 1024), F32), pltpu.VMEM((T, 1024), F32)],
        compiler_params=_cp(("parallel",)),
    )(u, dy, bb, ab, cc, dsk)


def _interleave(a):
    return a.reshape(NSEG, SEG, a.shape[1]).transpose(1, 0, 2).reshape(a.shape)


def _deinterleave(a):
    return a.reshape(SEG, NSEG, a.shape[1]).transpose(1, 0, 2).reshape(a.shape)


_TM_MIX = 256


def _mix_parts(att, y, wg, bg):
    y2 = jax.nn.gelu(y)
    y2b = y2.astype(BF16)
    sg = jax.nn.sigmoid(_nn(y2b, wg) + bg)
    ssm = y2 * sg
    ra, rs = _rms_r(att), _rms_r(ssm)
    return y2, y2b, sg, ssm, ra, rs


def _mix_fwd(att, y, x, wg, bg, ga, gs, wo):
    def body(att_ref, y_ref, x_ref, wg_ref, bg_ref, ga_ref, gs_ref, wo_ref, x1_ref):
        attv = att_ref[...]
        _, _, _, ssm, ra, rs = _mix_parts(attv, y_ref[...], wg_ref[...], bg_ref[...])
        ma = (attv * ra * ga_ref[...]).astype(BF16)
        ms = (ssm * rs * gs_ref[...]).astype(BF16)
        x1_ref[...] = x_ref[...] + _nn(ma, wo_ref[0:AW, :]) + _nn(ms, wo_ref[AW:D, :])

    row = lambda i: (i, 0)
    fixed = lambda i: (0, 0)
    half = pl.BlockSpec((_TM_MIX, AW), row)
    vec = pl.BlockSpec((1, AW), fixed)
    return pl.pallas_call(
        body, name="mix_fwd", grid=(T // _TM_MIX,),
        in_specs=[half, half, pl.BlockSpec((_TM_MIX, D), row), pl.BlockSpec((SW, SW), fixed), vec, vec, vec,
                  pl.BlockSpec((D, D), fixed)],
        out_specs=pl.BlockSpec((_TM_MIX, D), row), out_shape=_sds((T, D)),
        compiler_params=_cp(("parallel",)),
    )(att, y, x, wg, bg, ga, gs, wo)


def _mix_bwd(att, y, dx1, wg, bg, ga, gs, wo):
    last = T // _TM_MIX - 1

    def body(att_ref, y_ref, d_ref, wg_ref, bg_ref, ga_ref, gs_ref, wo_ref,
             datt_ref, dy_ref, dwg_ref, dbg_ref, dga_ref, dgs_ref, dwo_ref, dwg_acc, dwo_acc):
        attv, yv, db = att_ref[...], y_ref[...], d_ref[...].astype(BF16)
        y2, y2b, sg, ssm, ra, rs = _mix_parts(attv, yv, wg_ref[...], bg_ref[...])
        na, ns = attv * ra, ssm * rs
        ma = (na * ga_ref[...]).astype(BF16)
        ms = (ns * gs_ref[...]).astype(BF16)
        dma = _nt(db, wo_ref[0:AW, :])
        dms = _nt(db, wo_ref[AW:D, :])
        datt, dga = _rms_bwd(na, ra, ga_ref[...], dma)
        dssm, dgs = _rms_bwd(ns, rs, gs_ref[...], dms)
        dgl = dssm * y2 * sg * (1.0 - sg)
        dglb = dgl.astype(BF16)
        dy2 = dssm * sg + _nt(dglb, wg_ref[...])
        _, gelu_vjp = jax.vjp(jax.nn.gelu, yv)
        datt_ref[...] = datt
        dy_ref[...] = gelu_vjp(dy2)[0]

        @pl.when(pl.program_id(0) == 0)
        def _():
            dwg_acc[...] = jnp.zeros_like(dwg_acc)
            dwo_acc[...] = jnp.zeros_like(dwo_acc)
            dbg_ref[...] = jnp.zeros_like(dbg_ref)
            dga_ref[...] = jnp.zeros_like(dga_ref)
            dgs_ref[...] = jnp.zeros_like(dgs_ref)
        dwg_acc[...] += _tn(y2b, dglb)
        dbg_ref[...] += jnp.sum(dgl, axis=0, keepdims=True)
        dga_ref[...] += dga
        dgs_ref[...] += dgs
        dwo_acc[0:AW, :] += _tn(ma, db)
        dwo_acc[AW:D, :] += _tn(ms, db)

        @pl.when(pl.program_id(0) == last)
        def _():
            dwg_ref[...] = dwg_acc[...].astype(BF16)
            dwo_ref[...] = dwo_acc[...].astype(BF16)

    row = lambda i: (i, 0)
    fixed = lambda i: (0, 0)
    half = pl.BlockSpec((_TM_MIX, AW), row)
    vec = pl.BlockSpec((1, AW), fixed)
    sq = pl.BlockSpec((SW, SW), fixed)
    big = pl.BlockSpec((D, D), fixed)
    return pl.pallas_call(
        body, name="mix_bwd", grid=(T // _TM_MIX,),
        in_specs=[half, half, pl.BlockSpec((_TM_MIX, D), row), sq, vec, vec, vec, big],
        out_specs=[half, half, sq, vec, vec, vec, big],
        out_shape=[_sds((T, AW)), _sds((T, SW)), _sds((SW, SW), BF16), _sds((1, SW)), _sds((1, AW)), _sds((1, SW)),
                   _sds((D, D), BF16)],
        scratch_shapes=[pltpu.VMEM((SW, SW), F32), pltpu.VMEM((D, D), F32)],
        compiler_params=_cp(("arbitrary",)),
    )(att, y, dx1, wg, bg, ga, gs, wo)


_NCB = -(-CS // LANES)


def _ffn_up(x1, g, wu4):
    tm = 512

    def body(x_ref, g_ref, w_ref, h_ref, up_ref):
        xv = x_ref[...]
        h = (xv * _rms_r(xv) * g_ref[...]).astype(BF16)
        h_ref[...] = h
        up_ref[0, 0] = _nn(h, w_ref[0, 0])

    return pl.pallas_call(
        body, name="ffn_up", grid=(T // tm, NDEV),
        in_specs=[pl.BlockSpec((tm, D), lambda i, e: (i, 0)), pl.BlockSpec((1, D), lambda i, e: (0, 0)),
                  pl.BlockSpec((1, 1, D, CS), lambda i, e: (e // 4, e % 4, 0, 0))],
        out_specs=[pl.BlockSpec((tm, D), lambda i, e: (i, 0)),
                   pl.BlockSpec((1, 1, tm, CS), lambda i, e: (e // 4, e % 4, i, 0))],
        out_shape=[_sds((T, D), BF16), _sds((2, 4, T, CS))],
        compiler_params=_cp(("parallel", "arbitrary")),
    )(x1, g, wu4)


def _shift_down(h, k):
    row = lax.broadcasted_iota(jnp.int32, h.shape, 0)
    return jnp.where(row >= k, pltpu.roll(h, k, 0), 0.0)


def _shift_up(h, k):
    row = lax.broadcasted_iota(jnp.int32, h.shape, 0)
    return jnp.where(row < T - k, pltpu.roll(h, T - k, 0), 0.0)


def _conv(h, w, b):
    return w[0:1, :] * _shift_down(h, 2) + w[1:2, :] * _shift_down(h, 1) + w[2:3, :] * h + b


def _chan(gv, rows):
    return pl.BlockSpec((1, 1, rows, LANES), lambda d, j: (gv, d, 0, j))


def _ffn_act(up4, cw4, cb4):
    def body(ug_ref, uv_ref, wg_ref, wv_ref, bg_ref, bv_ref, act_ref):
        g = _conv(ug_ref[0, 0], wg_ref[0, 0], bg_ref[0, 0])
        v = _conv(uv_ref[0, 0], wv_ref[0, 0], bv_ref[0, 0])
        act_ref[0] = (g * jax.nn.sigmoid(g) * v).astype(BF16)

    return pl.pallas_call(
        body, name="ffn_act", grid=(4, _NCB),
        in_specs=[_chan(0, T), _chan(1, T), _chan(0, 3), _chan(1, 3), _chan(0, 1), _chan(1, 1)],
        out_specs=pl.BlockSpec((1, T, LANES), lambda d, j: (d, 0, j)), out_shape=_sds((4, T, CS), BF16),
        compiler_params=_cp(("parallel", "parallel")),
    )(up4, up4, cw4, cw4, cb4, cb4)


def _ffn_act_bwd(up4, dact, cw4, cb4):
    def conv_bwd(h, w, d):
        dup = w[2:3, :] * d + w[1:2, :] * _shift_up(d, 1) + w[0:1, :] * _shift_up(d, 2)
        dw = jnp.concatenate([jnp.sum(d * _shift_down(h, 2), axis=0, keepdims=True),
                              jnp.sum(d * _shift_down(h, 1), axis=0, keepdims=True),
                              jnp.sum(d * h, axis=0, keepdims=True)], axis=0)
        return dup, dw, jnp.sum(d, axis=0, keepdims=True)

    def body(ug_ref, uv_ref, da_ref, wg_ref, wv_ref, bg_ref, bv_ref, dup_ref, dw_ref, db_ref):
        ug, uv, da = ug_ref[0, 0], uv_ref[0, 0], da_ref[0]
        g = _conv(ug, wg_ref[0, 0], bg_ref[0, 0])
        v = _conv(uv, wv_ref[0, 0], bv_ref[0, 0])
        sg = jax.nn.sigmoid(g)
        dg = da * v * (sg * (1.0 + g * (1.0 - sg)))
        dv = da * (g * sg)
        dug, dwg, dbg = conv_bwd(ug, wg_ref[0, 0], dg)
        duv, dwv, dbv = conv_bwd(uv, wv_ref[0, 0], dv)
        dup_ref[0, 0] = dug.astype(BF16)
        dup_ref[1, 0] = duv.astype(BF16)
        dw_ref[0, 0] = dwg
        dw_ref[1, 0] = dwv
        db_ref[0, 0] = dbg
        db_ref[1, 0] = dbv

    both = lambda rows: pl.BlockSpec((2, 1, rows, LANES), lambda d, j: (0, d, 0, j))
    return pl.pallas_call(
        body, name="ffn_act_bwd", grid=(4, _NCB),
        in_specs=[_chan(0, T), _chan(1, T), pl.BlockSpec((1, T, LANES), lambda d, j: (d, 0, j)),
                  _chan(0, 3), _chan(1, 3), _chan(0, 1), _chan(1, 1)],
        out_specs=[both(T), both(3), both(1)],
        out_shape=[_sds((2, 4, T, CS), BF16), _sds((2, 4, 3, CS)), _sds((2, 4, 1, CS))],
        compiler_params=_cp(("parallel", "parallel")),
    )(up4, up4, dact, cw4, cw4, cb4, cb4)


def _ffn_down_loss(act, wd4, x1, target):
    tm = 256

    def body(a_ref, w_ref, x1_ref, t_ref, loss_ref, dx_ref, dxb_ref):
        x2 = x1_ref[...]
        for d in range(4):
            x2 = x2 + _nn(a_ref[d], w_ref[d])
        err = x2 - t_ref[...]
        dx = err * (1.0 / D)
        dx_ref[...] = dx
        dxb_ref[...] = dx.astype(BF16)

        @pl.when(pl.program_id(0) == 0)
        def _():
            loss_ref[...] = jnp.zeros_like(loss_ref)
        loss_ref[...] += 0.5 * jnp.sum(jnp.mean(err * err, axis=-1, keepdims=True), axis=0, keepdims=True)

    row = lambda i: (i, 0)
    fixed = lambda i: (0, 0)
    return pl.pallas_call(
        body, name="ffn_down_loss", grid=(T // tm,),
        in_specs=[pl.BlockSpec((4, tm, CS), lambda i: (0, i, 0)), pl.BlockSpec((4, CS, D), lambda i: (0, 0, 0)),
                  pl.BlockSpec((tm, D), row), pl.BlockSpec((tm, D), row)],
        out_specs=[pl.BlockSpec((1, 1), fixed), pl.BlockSpec((tm, D), row), pl.BlockSpec((tm, D), row)],
        out_shape=[_sds((1, 1)), _sds((T, D)), _sds((T, D), BF16)],
        compiler_params=_cp(("arbitrary",)),
    )(act, wd4, x1, target)


def _ffn_dact(dx2b, wd4):
    tm = 512

    def body(d_ref, w_ref, o_ref):
        o_ref[0] = _nt(d_ref[...], w_ref[0])

    return pl.pallas_call(
        body, name="ffn_dact", grid=(4, T // tm),
        in_specs=[pl.BlockSpec((tm, D), lambda d, i: (i, 0)), pl.BlockSpec((1, CS, D), lambda d, i: (d, 0, 0))],
        out_specs=pl.BlockSpec((1, tm, CS), lambda d, i: (d, i, 0)), out_shape=_sds((4, T, CS)),
        compiler_params=_cp(("parallel", "parallel")),
    )(dx2b, wd4)


def _ffn_dwd(act, dx2b):
    def body(a_ref, d_ref, o_ref):
        o_ref[0] = _tn(a_ref[0], d_ref[...]).astype(BF16)

    return pl.pallas_call(
        body, name="ffn_dwd", grid=(4,),
        in_specs=[pl.BlockSpec((1, T, CS), lambda d: (d, 0, 0)), pl.BlockSpec((T, D), lambda d: (0, 0))],
        out_specs=pl.BlockSpec((1, CS, D), lambda d: (d, 0, 0)), out_shape=_sds((4, CS, D), BF16),
        compiler_params=_cp(("parallel",)),
    )(act, dx2b)


def _ffn_dwu(h2, dup4):
    tm = 512

    def body(h_ref, d_ref, o_ref):
        o_ref[0, 0] = _tn(h_ref[...], d_ref[0, 0]).astype(BF16)

    return pl.pallas_call(
        body, name="ffn_dwu", grid=(NDEV, D // tm),
        in_specs=[pl.BlockSpec((T, tm), lambda e, i: (0, i)),
                  pl.BlockSpec((1, 1, T, CS), lambda e, i: (e // 4, e % 4, 0, 0))],
        out_specs=pl.BlockSpec((1, 1, tm, CS), lambda e, i: (e // 4, e % 4, i, 0)),
        out_shape=_sds((2, 4, D, CS), BF16),
        compiler_params=_cp(("parallel", "parallel")),
    )(h2, dup4)


def _ffn_up_bwd(dup4, wu4, x1, g, dres):
    tm = 256

    def body(d_ref, w_ref, x_ref, g_ref, dres_ref, dx_ref, dg_ref):
        dh = jnp.zeros((tm, D), F32)
        for gv in range(2):
            for d in range(4):
                dh = dh + _nt(d_ref[gv, d], w_ref[gv, d])
        xv = x_ref[...]
        r = _rms_r(xv)
        dxr, dgp = _rms_bwd(xv * r, r, g_ref[...], dh)
        dx_ref[...] = dres_ref[...] + dxr

        @pl.when(pl.program_id(0) == 0)
        def _():
            dg_ref[...] = jnp.zeros_like(dg_ref)
        dg_ref[...] += dgp

    row = lambda i: (i, 0)
    fixed = lambda i: (0, 0)
    return pl.pallas_call(
        body, name="ffn_up_bwd", grid=(T // tm,),
        in_specs=[pl.BlockSpec((2, 4, tm, CS), lambda i: (0, 0, i, 0)), pl.BlockSpec((2, 4, D, CS), lambda i: (0, 0, 0, 0)),
                  pl.BlockSpec((tm, D), row), pl.BlockSpec((1, D), fixed), pl.BlockSpec((tm, D), row)],
        out_specs=[pl.BlockSpec((tm, D), row), pl.BlockSpec((1, D), fixed)],
        out_shape=[_sds((T, D)), _sds((1, D))],
        compiler_params=_cp(("arbitrary",)),
    )(dup4, wu4, x1, g, dres)


def _adamw_math(w, g, m, v):
    m = ADAM_B1 * m + (1.0 - ADAM_B1) * g
    v = ADAM_B2 * v + (1.0 - ADAM_B2) * jnp.square(g)
    m_hat = m / (1.0 - ADAM_B1 ** ADAM_STEP)
    v_hat = v / (1.0 - ADAM_B2 ** ADAM_STEP)
    delta = -ADAM_LR * (m_hat / (jnp.sqrt(v_hat) + ADAM_EPS) + ADAM_WD * w)
    return delta, m, v


def _where_am_i():
    x, y, c = _place()
    return jnp.stack([c, 2 * x + y]).astype(jnp.int32)


def _pair_sum(a4, recv, name):
    _, _, rows, cols = a4.shape
    tr = _row_tile(rows, cols, 4, 16)

    def body(sel_ref, own_ref, recv_ref, out_ref):
        out_ref[...] = (own_ref[0].astype(F32) + recv_ref[...].astype(F32)).astype(out_ref.dtype)

    grid_spec = pltpu.PrefetchScalarGridSpec(
        num_scalar_prefetch=1, grid=(4, rows // tr),
        in_specs=[pl.BlockSpec((1, 1, tr, cols), lambda k, i, s: (k, s[0], i, 0)),
                  pl.BlockSpec((1, tr, cols), lambda k, i, s: (k, i, 0))],
        out_specs=pl.BlockSpec((1, tr, cols), lambda k, i, s: (k, i, 0)),
    )
    return pl.pallas_call(
        body, name=name, grid_spec=grid_spec, out_shape=_sds((4, rows, cols), a4.dtype),
        compiler_params=_cp(("arbitrary", "arbitrary")),
    )(_where_am_i(), a4, recv)


def _adamw_rs(a4, recv, from_chips, w, m, v, name):
    rows, cols = w.shape
    tr = _row_tile(rows, cols, 4, 16)

    def body(sel_ref, own_ref, recv_ref, fc_ref, w_ref, m_ref, v_ref, g_ref, d_ref, nm_ref, nv_ref):
        g = own_ref[0, 0].astype(F32) + recv_ref[0].astype(F32)
        for j in range(3):
            g = g + fc_ref[j].astype(F32)
        d, nm, nv = _adamw_math(w_ref[...], g, m_ref[...], v_ref[...])
        g_ref[...] = g
        d_ref[...] = d
        nm_ref[...] = nm
        nv_ref[...] = nv

    flat = pl.BlockSpec((tr, cols), lambda i, s: (i, 0))
    grid_spec = pltpu.PrefetchScalarGridSpec(
        num_scalar_prefetch=1, grid=(rows // tr,),
        in_specs=[pl.BlockSpec((1, 1, tr, cols), lambda i, s: (s[1], s[0], i, 0)),
                  pl.BlockSpec((1, tr, cols), lambda i, s: (s[1], i, 0)),
                  pl.BlockSpec((3, tr, cols), lambda i, s: (0, i, 0)), flat, flat, flat],
        out_specs=[flat] * 4,
    )
    return pl.pallas_call(
        body, name=name, grid_spec=grid_spec, out_shape=[_sds((rows, cols))] * 4,
        compiler_params=_cp(("arbitrary",)),
    )(_where_am_i(), a4, recv, from_chips, w, m, v)


def _adamw_small(gall, w, m, v):
    rows = w.shape[0]

    def body(ga_ref, w_ref, m_ref, v_ref, g_ref, d_ref, nm_ref, nv_ref):
        g = ga_ref[0]
        for dev in range(1, NDEV):
            g = g + ga_ref[dev]
        d, nm, nv = _adamw_math(w_ref[...], g, m_ref[...], v_ref[...])
        g_ref[...] = g
        d_ref[...] = d
        nm_ref[...] = nm
        nv_ref[...] = nv

    return pl.pallas_call(body, name="adamw_small", out_shape=[_sds((rows, LANES))] * 4,
                          compiler_params=_cp())(gall, w, m, v)


def _rows128(a):
    n = a.shape[0]
    r = (-n) % (8 * LANES)
    if r:
        a = jnp.pad(a, [(0, r)])
    return a.reshape(-1, LANES)


_SMALL = (("g_mix", D), ("b_f", NH), ("g_q", DH), ("g_k", DH), ("lambda_re", NG * NP), ("lambda_im", NG * NP),
          ("log_step", NG), ("b_re", NG * NP * GS), ("b_im", NG * NP * GS), ("c_re", NG * GS * NP),
          ("c_im", NG * GS * NP), ("d_skip", NG * GS), ("b_glu", SW), ("g_attn_out", AW), ("g_ssm_out", SW),
          ("g_ffn", D), ("conv_b", 2 * DFF))


def _pack_small(d):
    return jnp.concatenate([_rows128(d[name].reshape(-1).astype(F32)) for name, _ in _SMALL], axis=0)


def _unpack_small(p, shapes):
    out, off = {}, 0
    for name, n in _SMALL:
        rows = -(-n // (8 * LANES)) * 8
        out[name] = p[off:off + rows].reshape(-1)[:n].reshape(shapes[name])
        off += rows
    return out


def _block_diag(m):
    eye = jnp.eye(8, dtype=m.dtype)
    r, c = m.shape[2], m.shape[3]
    return (m[:, :, :, None, :] * eye[None, :, None, :, None]).reshape(4, 8 * r, 8 * c)


def _diag_blocks(dense, r, c):
    eye = jnp.eye(8, dtype=dense.dtype)
    return jnp.sum(dense.reshape(4, 8, r, 8, c) * eye[None, :, None, :, None], axis=3)


def kernel(x, g_mix, w_in, b_f, g_q, g_k, lambda_re, lambda_im, log_step, b_re, b_im, c_re, c_im, d_skip, w_glu, b_glu, g_attn_out, g_ssm_out, w_out, g_ffn, w_up, conv_w, conv_b, w_down, loss_target, m_g_mix, m_w_in, m_b_f, m_g_q, m_g_k, m_lambda_re, m_lambda_im, m_log_step, m_b_re, m_b_im, m_c_re, m_c_im, m_d_skip, m_w_glu, m_b_glu, m_g_attn_out, m_g_ssm_out, m_w_out, m_g_ffn, m_w_up, m_conv_w, m_conv_b, m_w_down, v_g_mix, v_w_in, v_b_f, v_g_q, v_g_k, v_lambda_re, v_lambda_im, v_log_step, v_b_re, v_b_im, v_c_re, v_c_im, v_d_skip, v_w_glu, v_b_glu, v_g_attn_out, v_g_ssm_out, v_w_out, v_g_ffn, v_w_up, v_conv_w, v_conv_b, v_w_down):
    weights = dict(g_mix=g_mix, w_in=w_in, b_f=b_f, g_q=g_q, g_k=g_k, lambda_re=lambda_re, lambda_im=lambda_im,
                   log_step=log_step, b_re=b_re, b_im=b_im, c_re=c_re, c_im=c_im, d_skip=d_skip, w_glu=w_glu,
                   b_glu=b_glu, g_attn_out=g_attn_out, g_ssm_out=g_ssm_out, w_out=w_out, g_ffn=g_ffn, w_up=w_up,
                   conv_w=conv_w, conv_b=conv_b, w_down=w_down)
    mom_m = dict(g_mix=m_g_mix, w_in=m_w_in, b_f=m_b_f, g_q=m_g_q, g_k=m_g_k, lambda_re=m_lambda_re,
                 lambda_im=m_lambda_im, log_step=m_log_step, b_re=m_b_re, b_im=m_b_im, c_re=m_c_re, c_im=m_c_im,
                 d_skip=m_d_skip, w_glu=m_w_glu, b_glu=m_b_glu, g_attn_out=m_g_attn_out, g_ssm_out=m_g_ssm_out,
                 w_out=m_w_out, g_ffn=m_g_ffn, w_up=m_w_up, conv_w=m_conv_w, conv_b=m_conv_b, w_down=m_w_down)
    mom_v = dict(g_mix=v_g_mix, w_in=v_w_in, b_f=v_b_f, g_q=v_g_q, g_k=v_g_k, lambda_re=v_lambda_re,
                 lambda_im=v_lambda_im, log_step=v_log_step, b_re=v_b_re, b_im=v_b_im, c_re=v_c_re, c_im=v_c_im,
                 d_skip=v_d_skip, w_glu=v_w_glu, b_glu=v_b_glu, g_attn_out=v_g_attn_out, g_ssm_out=v_g_ssm_out,
                 w_out=v_w_out, g_ffn=v_g_ffn, w_up=v_w_up, conv_w=v_conv_w, conv_b=v_conv_b, w_down=v_w_down)
    names = list(weights)
    big = ("w_in", "w_glu", "w_out", "w_up", "w_down", "conv_w")

    xs = x[0]
    target = loss_target[0]
    row = lambda a: a.reshape(1, -1)

    win_g, wg_g, wo_g = _all_gather([w_in.astype(BF16), w_glu.astype(BF16), w_out.astype(BF16)], "gather_mixer_weights")
    w_in_f = win_g.transpose(1, 0, 2).reshape(D, 8 * 257)
    wcat = jnp.concatenate([w_in_f[:, 0:1536], w_in_f[:, 1544:2056], w_in_f[:, 1536:1544],
                            jnp.zeros((D, LANES - NH), BF16)], axis=1)
    wg_f = wg_g.reshape(SW, SW)
    wo_f = wo_g.reshape(D, D)
    cb4 = conv_b.reshape(2, 4, 1, CS)

    h1, z, f_t = _in_proj(xs, row(g_mix), wcat)
    bf_col = b_f.reshape(NH, 1)
    ck = _gates_fwd(f_t, bf_col).reshape(_HP, 2, T)
    att, lse, (wu_g, cw_g) = _attn_fwd(z, ck, row(g_q), row(g_k), [w_up.astype(BF16), conv_w])

    ls_col = log_step.reshape(NG, 1)
    br_t, bi_t = b_re.transpose(0, 2, 1), b_im.transpose(0, 2, 1)
    ab_re, ab_im, bb_re, bb_im = _disc_fwd(lambda_re, lambda_im, ls_col, br_t, bi_t)
    bb = jnp.concatenate([_block_diag(bb_re.reshape(4, 8, GS, NP)), _block_diag(bb_im.reshape(4, 8, GS, NP))],
                         axis=2).astype(BF16)
    ab = jnp.stack([ab_re.reshape(4, 512), ab_im.reshape(4, 512)], axis=1)
    cdiag = lambda cm: _block_diag(cm.reshape(4, 8, GS, NP).transpose(0, 1, 3, 2))
    cc = jnp.concatenate([cdiag(c_re), -cdiag(c_im)], axis=1).astype(BF16)
    dsk = d_skip.reshape(4, 1, LANES)
    u_il = _interleave(z[:, 1536:2048])
    y_il, (wd_g,) = _ssm_fwd(u_il, bb, ab, cc, dsk, [w_down.astype(BF16)])
    y = _deinterleave(y_il)
    wu_g, cw_g, wd_g = _gather_second([wu_g, cw_g, wd_g], "gather_mlp_weights_pass_on")
    wu4 = wu_g.reshape(2, 4, D, CS)
    wd4 = wd_g.reshape(4, CS, D)
    cw4 = cw_g.reshape(2, 4, 3, CS)

    x1 = _mix_fwd(att, y, xs, wg_f, row(b_glu), row(g_attn_out), row(g_ssm_out), wo_f)
    h2, up4 = _ffn_up(x1, row(g_ffn), wu4)
    act = _ffn_act(up4, cw4, cb4)
    loss_dev, dx2, dx2b = _ffn_down_loss(act, wd4, x1, target)
    loss = lax.psum(loss_dev[0, 0], ("x", "y", "c"))

    dact = _ffn_dact(dx2b, wd4)
    d_wd = _ffn_dwd(act, dx2b)
    dup4, dcw4, dcb4 = _ffn_act_bwd(up4, dact, cw4, cb4)
    d_wu = _ffn_dwu(h2, dup4)
    mlp = ("w_up", "w_down", "conv_w")
    by_dest = {"w_up": d_wu.reshape(4, 2, D, CS), "w_down": d_wd.reshape(4, 2, DFF // NDEV, D),
               "conv_w": dcw4.reshape(4, 2, 3, CS)}
    from_sibling = dict(zip(mlp, _swap_sibling([by_dest[n] for n in mlp], "rs_pair_swap_mlp")))
    chip_sums = {n: _pair_sum(by_dest[n], from_sibling[n], "rs_pair_sum_" + n) for n in mlp}
    dx1, dg_ffn = _ffn_up_bwd(dup4, wu4, x1, row(g_ffn), dx2)
    datt, dy, d_wg, d_bg, d_ga, d_gs, d_wo = _mix_bwd(att, y, dx1, wg_f, row(b_glu), row(g_attn_out),
                                                       row(g_ssm_out), wo_f)
    du_il, dbb, dab, dcc, ddsk = _ssm_bwd(u_il, _interleave(dy), bb, ab, cc, dsk)
    du = _deinterleave(du_il).astype(BF16)
    dbb_re = _diag_blocks(dbb[:, :, 0:512], GS, NP).reshape(NG, GS, NP)
    dbb_im = _diag_blocks(dbb[:, :, 512:1024], GS, NP).reshape(NG, GS, NP)
    dlr, dli, dls, dbr_t, dbi_t = _disc_bwd(lambda_re, lambda_im, ls_col, br_t, bi_t,
                                            dab[:, 0].reshape(NG, NP), dab[:, 1].reshape(NG, NP), dbb_re, dbb_im)
    d_cre = _diag_blocks(dcc[:, 0:512], NP, GS).transpose(0, 1, 3, 2).reshape(NG, GS, NP)
    d_cim = -_diag_blocks(dcc[:, 512:1024], NP, GS).transpose(0, 1, 3, 2).reshape(NG, GS, NP)

    dq, dk, dv, dck, d_gq, d_gk, fc_mlp = _attn_bwd(z, ck, row(g_q), row(g_k), lse, datt, [chip_sums[n] for n in mlp])
    from_chips = dict(zip(mlp, fc_mlp))
    df_t, d_bf = _gates_bwd(dck.reshape(NH, T), f_t, bf_col)
    df = jnp.concatenate([df_t.T, jnp.zeros((T, LANES - NH), F32)], axis=1).astype(BF16)
    d_wcat = _in_proj_dw(h1, dq, dk, dv, du, df)
    grad_x, dg_mix = _in_proj_bwd(dq, dk, dv, du, df, wcat, xs, row(g_mix), dx1)
    d_win = jnp.concatenate([d_wcat[:, 0:1536], d_wcat[:, 2048:2048 + NH], d_wcat[:, 1536:2048]], axis=1)
    d_win = d_win.astype(BF16).reshape(D, NDEV, 257).transpose(1, 0, 2)

    mixer = ("w_out", "w_glu", "w_in")
    by_dest.update({"w_out": d_wo.reshape(4, 2, D // NDEV, D), "w_glu": d_wg.reshape(4, 2, SW // NDEV, SW),
                    "w_in": d_win.reshape(4, 2, D, 257)})
    from_sibling.update(zip(mixer, _swap_sibling([by_dest[n] for n in mixer], "rs_pair_swap_mixer")))
    chip_sums.update({n: _pair_sum(by_dest[n], from_sibling[n], "rs_pair_sum_" + n) for n in mixer})
    from_chips.update(zip(mixer, _swap_chips([chip_sums[n] for n in mixer], "rs_chip_swap_mixer")))

    grads, deltas, new_m, new_v = {}, {}, {}, {}
    for n in mlp + mixer:
        grads[n], deltas[n], new_m[n], new_v[n] = _adamw_rs(by_dest[n], from_sibling[n], from_chips[n], weights[n],
                                                            mom_m[n], mom_v[n], "adamw_" + n)

    small = dict(g_mix=dg_mix, b_f=d_bf, g_q=d_gq, g_k=d_gk, lambda_re=dlr, lambda_im=dli, log_step=dls,
                 b_re=dbr_t.transpose(0, 2, 1), b_im=dbi_t.transpose(0, 2, 1), c_re=d_cre, c_im=d_cim, d_skip=ddsk,
                 b_glu=d_bg, g_attn_out=d_ga, g_ssm_out=d_gs, g_ffn=dg_ffn, conv_b=dcb4)
    (small_all,) = _all_gather([_pack_small(small)], "gather_small_grads")
    sg_p, sd_p, sm_p, sv_p = _adamw_small(small_all, _pack_small({n: weights[n] for n in small}),
                                          _pack_small({n: mom_m[n] for n in small}),
                                          _pack_small({n: mom_v[n] for n in small}))
    shapes = {n: weights[n].shape for n in small}
    sg, sd, sm, sv = (_unpack_small(p, shapes) for p in (sg_p, sd_p, sm_p, sv_p))
    for n in small:
        grads[n], deltas[n], new_m[n], new_v[n] = sg[n], sd[n], sm[n], sv[n]

    return (loss, grad_x[None], *[grads[n] for n in names], *[deltas[n] for n in names],
            *[new_m[n] for n in names], *[new_v[n] for n in names])
```

```python
import jax
import jax.numpy as jnp
from jax import lax
from jax.experimental import pallas as pl
from jax.experimental.pallas import tpu as pltpu

F32, BF16 = jnp.float32, jnp.bfloat16
T, D = 2048, 1024
NH, DH = 8, 64
AW, SW = 512, 512
NG, GS, NP = 32, 16, 64
DFF = 2816
NDEV = 8
CS = 2 * DFF // NDEV
EPS = 1e-6
NEG = -1e30
ZC = 4 * 512 + 128
SEG = 256
NSEG = T // SEG
BQ = 256
VMEM_LIMIT = 56 * 1024 * 1024
LANES = 128
MESH = pl.DeviceIdType.MESH
HI = lax.Precision.HIGHEST

ADAM_LR, ADAM_B1, ADAM_B2, ADAM_EPS, ADAM_WD, ADAM_STEP = 0.001, 0.9, 0.999, 1e-08, 0.01, 10


def _cp(dims=None):
    if dims is None:
        return pltpu.CompilerParams(vmem_limit_bytes=VMEM_LIMIT)
    return pltpu.CompilerParams(dimension_semantics=dims, vmem_limit_bytes=VMEM_LIMIT)


def _sds(shape, dtype=F32):
    return jax.ShapeDtypeStruct(shape, dtype)


def _nt(a, b):
    return lax.dot_general(a, b, (((1,), (1,)), ((), ())), preferred_element_type=F32)


def _tn(a, b):
    return lax.dot_general(a, b, (((0,), (0,)), ((), ())), preferred_element_type=F32)


def _nn(a, b):
    return jnp.dot(a, b, preferred_element_type=F32)


def _rms_r(x):
    return lax.rsqrt(jnp.mean(x * x, axis=-1, keepdims=True) + EPS)


def _rms_bwd(n, r, g, dh):
    dn = dh * g
    dx = r * (dn - n * jnp.mean(dn * n, axis=-1, keepdims=True))
    return dx, jnp.sum(dh * n, axis=0, keepdims=True)


def _row_tile(rows, cols, limit):
    tr = rows
    while tr * cols * 4 > limit and tr % 32 == 0:
        tr //= 2
    return tr


def _place():
    return lax.axis_index("x"), lax.axis_index("y"), lax.axis_index("c")


def _all_gather(shards, name):
    n = len(shards)

    def body(*refs):
        x_refs, out_refs = refs[:n], refs[n:2 * n]
        send_sems, recv_sems, local_sems = refs[2 * n:]
        x, y, c = _place()
        me, sibling = (x, y, c), (x, y, 1 - c)
        chips = [(1 - x, y), (x, 1 - y), (1 - x, 1 - y)]

        def copy(a, k, block, to, src=None):
            px, py, pc = block
            slot = out_refs[a].at[4 * px + 2 * py + pc]
            return pltpu.make_async_remote_copy(
                src_ref=slot if src is None else src, dst_ref=slot,
                send_sem=send_sems.at[7 * a + k], recv_sem=recv_sems.at[7 * a + k], device_id=to, device_id_type=MESH)

        mine, first, passed = [], [], []
        for a in range(n):
            cp = pltpu.make_async_copy(x_refs[a], out_refs[a].at[4 * x + 2 * y + c], local_sems.at[a])
            cp.start()
            mine.append(cp)
            cps = [copy(a, 0, me, sibling, src=x_refs[a])]
            cps += [copy(a, 1 + j, me, (*chip, c), src=x_refs[a]) for j, chip in enumerate(chips)]
            for cp in cps:
                cp.start()
            first += cps
        for a in range(n):
            for j, chip in enumerate(chips):
                copy(a, 1 + j, (*chip, c), me).wait_recv()
                fwd = copy(a, 4 + j, (*chip, c), sibling)
                fwd.start()
                passed.append(fwd)
        for a in range(n):
            copy(a, 0, sibling, me).wait_recv()
            for j, chip in enumerate(chips):
                copy(a, 4 + j, (*chip, 1 - c), me).wait_recv()
        for cp in first + passed:
            cp.wait_send()
        for cp in mine:
            cp.wait()

    anyspec = pl.BlockSpec(memory_space=pl.ANY)
    return pl.pallas_call(
        body, name=name,
        out_shape=[_sds((NDEV, *s.shape), s.dtype) for s in shards],
        in_specs=[anyspec] * n, out_specs=[anyspec] * n,
        scratch_shapes=[pltpu.SemaphoreType.DMA((7 * n,)), pltpu.SemaphoreType.DMA((7 * n,)),
                        pltpu.SemaphoreType.DMA((n,))],
    )(*shards)


def _gather_first_copies(x_refs, out_refs, send_sems, recv_sems, local_sems):
    x, y, c = _place()
    slot = 4 * x + 2 * y + c
    peers = [(x, y, 1 - c), (1 - x, y, c), (x, 1 - y, c), (1 - x, 1 - y, c)]
    local, remote = [], []
    for a, (x_ref, out_ref) in enumerate(zip(x_refs, out_refs)):
        local.append(pltpu.make_async_copy(x_ref, out_ref.at[slot], local_sems.at[a]))
        for k, peer in enumerate(peers):
            remote.append(pltpu.make_async_remote_copy(
                src_ref=x_ref, dst_ref=out_ref.at[slot], send_sem=send_sems.at[4 * a + k],
                recv_sem=recv_sems.at[4 * a + k], device_id=peer, device_id_type=MESH))
    return local, remote


def _gather_first_scratch(n):
    return [pltpu.SemaphoreType.DMA((4 * n,)), pltpu.SemaphoreType.DMA((4 * n,)), pltpu.SemaphoreType.DMA((n,))]


def _gather_second(bufs, name):
    n = len(bufs)

    def body(*refs):
        out_refs = refs[n:2 * n]
        send_sems, recv_sems = refs[2 * n:]
        x, y, c = _place()
        chips = [(1 - x, y), (x, 1 - y), (1 - x, 1 - y)]
        cps = []
        for a in range(n):
            for j, (px, py) in enumerate(chips):
                block = out_refs[a].at[4 * px + 2 * py + c]
                cps.append(pltpu.make_async_remote_copy(
                    src_ref=block, dst_ref=block, send_sem=send_sems.at[3 * a + j], recv_sem=recv_sems.at[3 * a + j],
                    device_id=(x, y, 1 - c), device_id_type=MESH))
        for cp in cps:
            cp.start()
        for cp in cps:
            cp.wait()

    anyspec = pl.BlockSpec(memory_space=pl.ANY)
    return pl.pallas_call(
        body, name=name,
        out_shape=[_sds(b.shape, b.dtype) for b in bufs],
        in_specs=[anyspec] * n, out_specs=[anyspec] * n,
        input_output_aliases={a: a for a in range(n)},
        scratch_shapes=[pltpu.SemaphoreType.DMA((3 * n,)), pltpu.SemaphoreType.DMA((3 * n,))],
    )(*bufs)


def _chip_swap_copies(p_refs, out_refs, send_sems, recv_sems):
    x, y, c = _place()
    chips = [(1 - x, y), (x, 1 - y), (1 - x, 1 - y)]
    cps = []
    for a, (p_ref, out_ref) in enumerate(zip(p_refs, out_refs)):
        for j, (px, py) in enumerate(chips):
            cps.append(pltpu.make_async_remote_copy(
                src_ref=p_ref.at[2 * px + py], dst_ref=out_ref.at[j],
                send_sem=send_sems.at[3 * a + j], recv_sem=recv_sems.at[3 * a + j],
                device_id=(px, py, c), device_id_type=MESH))
    return cps


def _swap_sibling(arrs, name):
    n = len(arrs)

    def body(*refs):
        a_refs, out_refs = refs[:n], refs[n:2 * n]
        send_sems, recv_sems = refs[2 * n:]
        x, y, c = _place()
        cps = []
        for a in range(n):
            for k in range(4):
                cps.append(pltpu.make_async_remote_copy(
                    src_ref=a_refs[a].at[k, 1 - c], dst_ref=out_refs[a].at[k],
                    send_sem=send_sems.at[4 * a + k], recv_sem=recv_sems.at[4 * a + k],
                    device_id=(x, y, 1 - c), device_id_type=MESH))
        for cp in cps:
            cp.start()
        for cp in cps:
            cp.wait()

    anyspec = pl.BlockSpec(memory_space=pl.ANY)
    return pl.pallas_call(
        body, name=name,
        out_shape=[_sds((4, *a.shape[2:]), a.dtype) for a in arrs],
        in_specs=[anyspec] * n, out_specs=[anyspec] * n,
        scratch_shapes=[pltpu.SemaphoreType.DMA((4 * n,)), pltpu.SemaphoreType.DMA((4 * n,))],
    )(*arrs)


def _swap_chips(parts, name):
    n = len(parts)

    def body(*refs):
        cps = _chip_swap_copies(refs[:n], refs[n:2 * n], *refs[2 * n:])
        for cp in cps:
            cp.start()
        for cp in cps:
            cp.wait()

    anyspec = pl.BlockSpec(memory_space=pl.ANY)
    return pl.pallas_call(
        body, name=name,
        out_shape=[_sds((3, *p.shape[1:]), p.dtype) for p in parts],
        in_specs=[anyspec] * n, out_specs=[anyspec] * n,
        scratch_shapes=[pltpu.SemaphoreType.DMA((3 * n,)), pltpu.SemaphoreType.DMA((3 * n,))],
    )(*parts)


def _in_proj(x, g, wcat):
    tm = 256

    def body(x_ref, g_ref, w_ref, h_ref, z_ref, ft_ref):
        xv = x_ref[...]
        h = (xv * _rms_r(xv) * g_ref[...]).astype(BF16)
        h_ref[...] = h
        z = _nn(h, w_ref[...])
        z_ref[...] = z
        ft_ref[...] = z[:, 2048:ZC].T[0:NH, :]

    row = lambda i: (i, 0)
    fixed = lambda i: (0, 0)
    return pl.pallas_call(
        body, name="in_proj", grid=(T // tm,),
        in_specs=[pl.BlockSpec((tm, D), row), pl.BlockSpec((1, D), fixed), pl.BlockSpec((D, ZC), fixed)],
        out_specs=[pl.BlockSpec((tm, D), row), pl.BlockSpec((tm, ZC), row), pl.BlockSpec((NH, tm), lambda i: (0, i))],
        out_shape=[_sds((T, D), BF16), _sds((T, ZC)), _sds((NH, T))],
        compiler_params=_cp(("parallel",)),
    )(x, g, wcat)


def _in_proj_dw(h, dq, dk, dv, du, df):
    tm = 512

    def body(h_ref, dq_ref, dk_ref, dv_ref, du_ref, df_ref, o_ref):
        hv = h_ref[...]
        for j, d_ref in enumerate((dq_ref, dk_ref, dv_ref, du_ref)):
            o_ref[:, 512 * j:512 * (j + 1)] = _tn(hv, d_ref[...])
        o_ref[:, 2048:ZC] = _tn(hv, df_ref[...])

    full = lambda w: pl.BlockSpec((T, w), lambda i: (0, 0))
    return pl.pallas_call(
        body, name="in_proj_dw", grid=(D // tm,),
        in_specs=[pl.BlockSpec((T, tm), lambda i: (0, i)), full(512), full(512), full(512), full(512), full(LANES)],
        out_specs=pl.BlockSpec((tm, ZC), lambda i: (i, 0)), out_shape=_sds((D, ZC)),
        compiler_params=_cp(("parallel",)),
    )(h, dq, dk, dv, du, df)


def _in_proj_bwd(dq, dk, dv, du, df, wcat, x, g, dres):
    tm = 256

    def body(dq_ref, dk_ref, dv_ref, du_ref, df_ref, w_ref, x_ref, g_ref, dres_ref, dx_ref, dg_ref):
        dh = _nt(df_ref[...], w_ref[:, 2048:ZC])
        for j, d_ref in enumerate((dq_ref, dk_ref, dv_ref, du_ref)):
            dh = dh + _nt(d_ref[...], w_ref[:, 512 * j:512 * (j + 1)])
        xv = x_ref[...]
        r = _rms_r(xv)
        dxr, dgp = _rms_bwd(xv * r, r, g_ref[...], dh)
        dx_ref[...] = dres_ref[...] + dxr

        @pl.when(pl.program_id(0) == 0)
        def _():
            dg_ref[...] = jnp.zeros_like(dg_ref)
        dg_ref[...] += dgp

    row = lambda i: (i, 0)
    fixed = lambda i: (0, 0)
    part = pl.BlockSpec((tm, 512), row)
    return pl.pallas_call(
        body, name="in_proj_bwd", grid=(T // tm,),
        in_specs=[part, part, part, part, pl.BlockSpec((tm, LANES), row), pl.BlockSpec((D, ZC), fixed),
                  pl.BlockSpec((tm, D), row), pl.BlockSpec((1, D), fixed), pl.BlockSpec((tm, D), row)],
        out_specs=[pl.BlockSpec((tm, D), row), pl.BlockSpec((1, D), fixed)],
        out_shape=[_sds((T, D)), _sds((1, D))],
        compiler_params=_cp(("arbitrary",)),
    )(dq, dk, dv, du, df, wcat, x, g, dres)


def _stack_lanes(v):
    return jnp.concatenate([v[:, LANES * b:LANES * (b + 1)] for b in range(T // LANES)], axis=0)


def _unstack_lanes(s):
    return jnp.concatenate([s[8 * b:8 * b + 8, :] for b in range(T // LANES)], axis=1)


def _cumsum_lanes(v, reverse):
    st = _stack_lanes(v)
    ri = lax.broadcasted_iota(jnp.int32, (LANES, LANES), 0)
    ci = lax.broadcasted_iota(jnp.int32, (LANES, LANES), 1)
    tri = (ri >= ci) if reverse else (ri <= ci)
    intra = jnp.dot(st, tri.astype(F32), precision=HI, preferred_element_type=F32)
    tot = intra[:, 0:1] if reverse else intra[:, LANES - 1:LANES]
    same_head = (ci % 8) == (ri % 8)
    other = (ci // 8 > ri // 8) if reverse else (ci // 8 < ri // 8)
    off = jnp.dot((same_head & other).astype(F32), jnp.broadcast_to(tot, (LANES, LANES)), precision=HI,
                  preferred_element_type=F32)
    return _unstack_lanes(intra + off)


def _gates_fwd(f_t, b_f):
    def body(f_ref, b_ref, c_ref):
        z = f_ref[...] + b_ref[...]
        lf = jnp.minimum(z, 0.0) - jnp.log(1.0 + jnp.exp(-jnp.abs(z)))
        c_ref[...] = _cumsum_lanes(lf, reverse=False)

    return pl.pallas_call(body, name="gates_fwd", out_shape=_sds((NH, T)), compiler_params=_cp())(f_t, b_f)


def _gates_bwd(dck, f_t, b_f):
    def body(d_ref, f_ref, b_ref, df_ref, db_ref):
        z = f_ref[...] + b_ref[...]
        dlf = _cumsum_lanes(d_ref[...], reverse=True)
        df = dlf * jax.nn.sigmoid(-z)
        df_ref[...] = df
        db_ref[...] = jnp.sum(df, axis=1, keepdims=True)

    return pl.pallas_call(body, name="gates_bwd", out_shape=[_sds((NH, T)), _sds((NH, 1))],
                          compiler_params=_cp())(dck, f_t, b_f)


def _lower_triangle():
    rows = lax.broadcasted_iota(jnp.int32, (BQ, BQ), 0)
    cols = lax.broadcasted_iota(jnp.int32, (BQ, BQ), 1)
    return cols <= rows


def _logit_parts(qb, kb, ck_row, lo, hi, tri):
    parts = []
    if lo > 0:
        parts.append((_nt(qb[lo:hi], kb[0:lo]) - ck_row[:, 0:lo], 0, lo))
    parts.append((jnp.where(tri, _nt(qb[lo:hi], kb[lo:hi]) - ck_row[:, lo:hi], NEG), lo, hi))
    return parts


_HP = NH // 2


def _qkv_specs():
    return [pl.BlockSpec((T, LANES), lambda p: (0, p)), pl.BlockSpec((T, LANES), lambda p: (0, _HP + p)),
            pl.BlockSpec((T, LANES), lambda p: (0, 2 * _HP + p))]


def _attn_fwd(z, ck, g_q, g_k, shards):
    scale = DH ** -0.5
    n = len(shards)

    def body(q_ref, k_ref, v_ref, ck_ref, gq_ref, gk_ref, *rest):
        x_refs, (o_ref, lse_ref), out_refs, sems = rest[:n], rest[n:n + 2], rest[n + 2:2 * n + 2], rest[2 * n + 2:]

        @pl.when(pl.program_id(0) == 0)
        def _():
            local, remote = _gather_first_copies(x_refs, out_refs, *sems)
            for cp in local + remote:
                cp.start()

        qb, kb, vb = [], [], []
        for hh in range(2):
            cols = slice(DH * hh, DH * (hh + 1))
            qv, kv = q_ref[:, cols], k_ref[:, cols]
            qb.append((qv * _rms_r(qv) * gq_ref[...] * scale).astype(BF16))
            kb.append((kv * _rms_r(kv) * gk_ref[...]).astype(BF16))
            vb.append(v_ref[:, cols].astype(BF16))
        tri = _lower_triangle()
        for i in range(T // BQ):
            lo, hi = i * BQ, (i + 1) * BQ
            outs = []
            for hh in range(2):
                parts = _logit_parts(qb[hh], kb[hh], ck_ref[0, hh:hh + 1, :], lo, hi, tri)
                m = jnp.max(parts[-1][0], axis=-1, keepdims=True)
                if lo > 0:
                    m = jnp.maximum(m, jnp.max(parts[0][0], axis=-1, keepdims=True))
                l, o = 0.0, 0.0
                for s, k0, k1 in parts:
                    p = jnp.exp(s - m)
                    l = l + jnp.sum(p, axis=-1, keepdims=True)
                    o = o + _nn(p.astype(BF16), vb[hh][k0:k1])
                outs.append(o / l)
                lse_ref[0, lo:hi, hh:hh + 1] = m + jnp.log(l)
            o_ref[lo:hi, :] = jnp.concatenate(outs, axis=1)

        @pl.when(pl.program_id(0) == _HP - 1)
        def _():
            local, remote = _gather_first_copies(x_refs, out_refs, *sems)
            for cp in remote + local:
                cp.wait()

    fixed = lambda p: (0, 0)
    anyspec = pl.BlockSpec(memory_space=pl.ANY)
    res = pl.pallas_call(
        body, name="attn_fwd", grid=(_HP,),
        in_specs=_qkv_specs() + [pl.BlockSpec((1, 2, T), lambda p: (p, 0, 0)), pl.BlockSpec((1, DH), fixed),
                                 pl.BlockSpec((1, DH), fixed)] + [anyspec] * n,
        out_specs=[pl.BlockSpec((T, LANES), lambda p: (0, p)), pl.BlockSpec((1, T, 2), lambda p: (p, 0, 0))]
        + [anyspec] * n,
        out_shape=[_sds((T, AW)), _sds((_HP, T, 2))] + [_sds((NDEV, *s.shape), s.dtype) for s in shards],
        scratch_shapes=_gather_first_scratch(n),
        compiler_params=_cp(("arbitrary",)),
    )(z, z, z, ck, g_q, g_k, *shards)
    return res[0], res[1], res[2:]


def _attn_bwd(z, ck, g_q, g_k, lse, datt, parts):
    scale = DH ** -0.5
    n = len(parts)

    def body(q_ref, k_ref, v_ref, ck_ref, gq_ref, gk_ref, lse_ref, do_ref, *rest):
        p_refs, rest = rest[:n], rest[n:]
        dq_ref, dk_ref, dv_ref, dck_ref, dgq_ref, dgk_ref = rest[:6]
        fc_refs, (dkn_acc, dv_acc, dc_acc), sems = rest[6:6 + n], rest[6 + n:9 + n], rest[9 + n:]

        @pl.when(pl.program_id(0) == 0)
        def _():
            for cp in _chip_swap_copies(p_refs, fc_refs, *sems):
                cp.start()

        nq, nk, rq, rk, qb, kb, vb, dob = [], [], [], [], [], [], [], []
        for hh in range(2):
            cols = slice(DH * hh, DH * (hh + 1))
            qv, kv = q_ref[:, cols], k_ref[:, cols]
            rq.append(_rms_r(qv))
            rk.append(_rms_r(kv))
            nq.append(qv * rq[hh])
            nk.append(kv * rk[hh])
            qb.append((nq[hh] * gq_ref[...] * scale).astype(BF16))
            kb.append((nk[hh] * gk_ref[...]).astype(BF16))
            vb.append(v_ref[:, cols].astype(BF16))
            dob.append(do_ref[:, cols].astype(BF16))
        dkn_acc[...] = jnp.zeros_like(dkn_acc)
        dv_acc[...] = jnp.zeros_like(dv_acc)
        dc_acc[...] = jnp.zeros_like(dc_acc)
        dgq = jnp.zeros((1, DH), F32)
        tri = _lower_triangle()
        for i in range(T // BQ):
            lo, hi = i * BQ, (i + 1) * BQ
            dqs = []
            for hh in range(2):
                lse = lse_ref[0, lo:hi, hh:hh + 1]
                pd = []
                for s, k0, k1 in _logit_parts(qb[hh], kb[hh], ck_ref[0, hh:hh + 1, :], lo, hi, tri):
                    pd.append((jnp.exp(s - lse), _nt(dob[hh][lo:hi], vb[hh][k0:k1]), k0, k1))
                mean = sum(jnp.sum(p * dp, axis=-1, keepdims=True) for p, dp, _, _ in pd)
                dqn = 0.0
                for p, dp, k0, k1 in pd:
                    ds = p * (dp - mean)
                    dsb = ds.astype(BF16)
                    dv_acc[hh, k0:k1, :] += _tn(p.astype(BF16), dob[hh][lo:hi])
                    dkn_acc[hh, k0:k1, :] += _tn(dsb, qb[hh][lo:hi])
                    dc_acc[hh:hh + 1, k0:k1] -= jnp.sum(ds, axis=0, keepdims=True)
                    dqn = dqn + _nn(dsb, kb[hh][k0:k1])
                dqx, dgp = _rms_bwd(nq[hh][lo:hi], rq[hh][lo:hi], gq_ref[...], dqn * scale)
                dqs.append(dqx)
                dgq = dgq + dgp
            dq_ref[lo:hi, :] = jnp.concatenate(dqs, axis=1).astype(BF16)
        dks, dgk = [], jnp.zeros((1, DH), F32)
        for hh in range(2):
            dkx, dgp = _rms_bwd(nk[hh], rk[hh], gk_ref[...], dkn_acc[hh])
            dks.append(dkx)
            dgk = dgk + dgp
        dk_ref[...] = jnp.concatenate(dks, axis=1).astype(BF16)
        dv_ref[...] = jnp.concatenate([dv_acc[0], dv_acc[1]], axis=1).astype(BF16)
        dck_ref[0] = dc_acc[...]

        @pl.when(pl.program_id(0) == 0)
        def _():
            dgq_ref[...] = jnp.zeros_like(dgq_ref)
            dgk_ref[...] = jnp.zeros_like(dgk_ref)
        dgq_ref[...] += dgq
        dgk_ref[...] += dgk

        @pl.when(pl.program_id(0) == _HP - 1)
        def _():
            for cp in _chip_swap_copies(p_refs, fc_refs, *sems):
                cp.wait()

    fixed = lambda p: (0, 0)
    pair = pl.BlockSpec((T, LANES), lambda p: (0, p))
    gsp = pl.BlockSpec((1, DH), fixed)
    ckspec = pl.BlockSpec((1, 2, T), lambda p: (p, 0, 0))
    anyspec = pl.BlockSpec(memory_space=pl.ANY)
    res = pl.pallas_call(
        body, name="attn_bwd", grid=(_HP,),
        in_specs=_qkv_specs() + [ckspec, gsp, gsp, pl.BlockSpec((1, T, 2), lambda p: (p, 0, 0)), pair] + [anyspec] * n,
        out_specs=[pair, pair, pair, ckspec, gsp, gsp] + [anyspec] * n,
        out_shape=[_sds((T, AW), BF16)] * 3 + [_sds((_HP, 2, T)), _sds((1, DH)), _sds((1, DH))]
        + [_sds((3, *p.shape[1:]), p.dtype) for p in parts],
        scratch_shapes=[pltpu.VMEM((2, T, DH), F32), pltpu.VMEM((2, T, DH), F32), pltpu.VMEM((2, T), F32),
                        pltpu.SemaphoreType.DMA((3 * n,)), pltpu.SemaphoreType.DMA((3 * n,))],
        compiler_params=_cp(("arbitrary",)),
    )(z, z, z, ck, g_q, g_k, lse, datt, *parts)
    return (*res[:6], res[6:])


def _disc(lr, li, ls, br_t, bi_t):
    step = jnp.exp(ls)
    er = jnp.exp(lr * step)
    ab_re = er * jnp.cos(li * step)
    ab_im = er * jnp.sin(li * step)
    num_re = ab_re - 1.0
    num_im = ab_im
    den = lr * lr + li * li
    f_re = (num_re * lr + num_im * li) / den
    f_im = (num_im * lr - num_re * li) / den
    bb_re = f_re[:, None, :] * br_t - f_im[:, None, :] * bi_t
    bb_im = f_re[:, None, :] * bi_t + f_im[:, None, :] * br_t
    return ab_re, ab_im, bb_re, bb_im


def _disc_fwd(lr, li, ls, br_t, bi_t):
    def body(lr_ref, li_ref, ls_ref, br_ref, bi_ref, ar_ref, ai_ref, bbr_ref, bbi_ref):
        ar, ai, bbr, bbi = _disc(lr_ref[...], li_ref[...], ls_ref[...], br_ref[...], bi_ref[...])
        ar_ref[...] = ar
        ai_ref[...] = ai
        bbr_ref[...] = bbr
        bbi_ref[...] = bbi

    return pl.pallas_call(
        body, name="ssm_disc_fwd",
        out_shape=[_sds((NG, NP)), _sds((NG, NP)), _sds((NG, GS, NP)), _sds((NG, GS, NP))],
        compiler_params=_cp())(lr, li, ls, br_t, bi_t)


def _disc_bwd(lr, li, ls, br_t, bi_t, dar, dai, dbbr, dbbi):
    def body(lr_ref, li_ref, ls_ref, br_ref, bi_ref, dar_ref, dai_ref, dbbr_ref, dbbi_ref,
             dlr_ref, dli_ref, dls_ref, dbr_ref, dbi_ref):
        _, vjp = jax.vjp(_disc, lr_ref[...], li_ref[...], ls_ref[...], br_ref[...], bi_ref[...])
        dlr, dli, dls, dbr, dbi = vjp((dar_ref[...], dai_ref[...], dbbr_ref[...], dbbi_ref[...]))
        dlr_ref[...] = dlr
        dli_ref[...] = dli
        dls_ref[...] = dls
        dbr_ref[...] = dbr
        dbi_ref[...] = dbi

    return pl.pallas_call(
        body, name="ssm_disc_bwd",
        out_shape=[_sds((NG, NP)), _sds((NG, NP)), _sds((NG, 1)), _sds((NG, GS, NP)), _sds((NG, GS, NP))],
        compiler_params=_cp())(lr, li, ls, br_t, bi_t, dar, dai, dbbr, dbbi)


def _cmul(ar, ai, br, bi):
    return ar * br - ai * bi, ar * bi + ai * br


def _scan(buf, a_re, a_im, reverse, other=None):
    ar = [jnp.broadcast_to(a_re[:, LANES * k:LANES * (k + 1)], (NSEG, LANES)) for k in range(4)]
    ai = [jnp.broadcast_to(a_im[:, LANES * k:LANES * (k + 1)], (NSEG, LANES)) for k in range(4)]

    def tile(ref, i, k):
        return ref[pl.ds(pl.multiple_of(i * NSEG, NSEG), NSEG), LANES * k:LANES * (k + 1)]

    def advance(i, xr, xi):
        nr = [ar[k] * xr[k] - ai[k] * xi[k] + tile(buf, i, k) for k in range(4)]
        ni = [ar[k] * xi[k] + ai[k] * xr[k] + tile(buf, i, 4 + k) for k in range(4)]
        return nr, ni

    def index(i):
        return (SEG - 1 - i) if reverse else i

    zeros = tuple(jnp.zeros((NSEG, LANES), F32) for _ in range(4))

    def sweep1(i, carry):
        nr, ni = advance(index(i), carry[0], carry[1])
        return tuple(nr), tuple(ni)

    er, ei = lax.fori_loop(0, SEG, sweep1, (zeros, zeros), unroll=4)

    row = lax.broadcasted_iota(jnp.int32, (NSEG, LANES), 0)
    sr, si = [], []
    for k in range(4):
        pr, pi = ar[k], ai[k]
        for _ in range(SEG.bit_length() - 1):
            pr, pi = _cmul(pr, pi, pr, pi)
        ir, ii = er[k], ei[k]
        for d in (1, 2, 4):
            shift = (NSEG - d) if reverse else d
            ok = (row < NSEG - d) if reverse else (row >= d)
            mr, mi = _cmul(pr, pi, pltpu.roll(ir, shift, 0), pltpu.roll(ii, shift, 0))
            ir = ir + jnp.where(ok, mr, 0.0)
            ii = ii + jnp.where(ok, mi, 0.0)
            pr, pi = _cmul(pr, pi, pr, pi)
        shift = (NSEG - 1) if reverse else 1
        ok = (row < NSEG - 1) if reverse else (row >= 1)
        sr.append(jnp.where(ok, pltpu.roll(ir, shift, 0), 0.0))
        si.append(jnp.where(ok, pltpu.roll(ii, shift, 0), 0.0))

    def sweep2(i, carry):
        xr, xi, dar, dai = carry
        idx = index(i)
        if other is not None:
            orr = [tile(other, idx, k) for k in range(4)]
            oi = [tile(other, idx, 4 + k) for k in range(4)]
            dar = tuple(dar[k] + xr[k] * orr[k] + xi[k] * oi[k] for k in range(4))
            dai = tuple(dai[k] + xi[k] * orr[k] - xr[k] * oi[k] for k in range(4))
        nr, ni = advance(idx, xr, xi)
        start = pl.multiple_of(idx * NSEG, NSEG)
        for k in range(4):
            buf[pl.ds(start, NSEG), LANES * k:LANES * (k + 1)] = nr[k]
            buf[pl.ds(start, NSEG), LANES * (4 + k):LANES * (5 + k)] = ni[k]
        return tuple(nr), tuple(ni), dar, dai

    _, _, dar, dai = lax.fori_loop(0, SEG, sweep2, (tuple(sr), tuple(si), zeros, zeros), unroll=4)
    if other is None:
        return None
    da_re = jnp.concatenate([jnp.sum(d, axis=0, keepdims=True) for d in dar], axis=1)
    da_im = jnp.concatenate([jnp.sum(d, axis=0, keepdims=True) for d in dai], axis=1)
    return da_re, da_im


_SSM_BLOCKS = 4


def _ssm_fwd(u, bb, ab, cc, dsk, shards):
    n = len(shards)

    def body(u_ref, bb_ref, a_ref, cc_ref, d_ref, *rest):
        x_refs, y_ref, out_refs, xb, sems = rest[:n], rest[n], rest[n + 1:2 * n + 1], rest[2 * n + 1], rest[2 * n + 2:]

        @pl.when(pl.program_id(0) == 0)
        def _():
            local, remote = _gather_first_copies(x_refs, out_refs, *sems)
            for cp in local + remote:
                cp.start()

        ub = u_ref[...]
        xb[...] = _nn(ub.astype(BF16), bb_ref[0])
        _scan(xb, a_ref[0, 0:1, :], a_ref[0, 1:2, :], reverse=False)
        y_ref[...] = _nn(xb[...].astype(BF16), cc_ref[0]) + d_ref[0] * ub

        @pl.when(pl.program_id(0) == _SSM_BLOCKS - 1)
        def _():
            local, remote = _gather_first_copies(x_refs, out_refs, *sems)
            for cp in remote + local:
                cp.wait()

    blk = lambda j: (j, 0, 0)
    col = pl.BlockSpec((T, LANES), lambda j: (0, j))
    anyspec = pl.BlockSpec(memory_space=pl.ANY)
    res = pl.pallas_call(
        body, name="ssm_fwd", grid=(_SSM_BLOCKS,),
        in_specs=[col, pl.BlockSpec((1, LANES, 1024), blk), pl.BlockSpec((1, 2, 512), blk),
                  pl.BlockSpec((1, 1024, LANES), blk), pl.BlockSpec((1, 1, LANES), blk)] + [anyspec] * n,
        out_specs=[col] + [anyspec] * n,
        out_shape=[_sds((T, SW))] + [_sds((NDEV, *s.shape), s.dtype) for s in shards],
        scratch_shapes=[pltpu.VMEM((T, 1024), F32)] + _gather_first_scratch(n),
        compiler_params=_cp(("arbitrary",)),
    )(u, bb, ab, cc, dsk, *shards)
    return res[0], res[1:]


def _ssm_bwd(u, dy, bb, ab, cc, dsk):
    def body(u_ref, dy_ref, bb_ref, a_ref, cc_ref, d_ref, du_ref, dbb_ref, da_ref, dcc_ref, dd_ref, xb, gb):
        ub, dyv = u_ref[...], dy_ref[...]
        ubb, dyb = ub.astype(BF16), dyv.astype(BF16)
        a_re, a_im = a_ref[0, 0:1, :], a_ref[0, 1:2, :]
        xb[...] = _nn(ubb, bb_ref[0])
        _scan(xb, a_re, a_im, reverse=False)
        dcc_ref[0] = _tn(xb[...].astype(BF16), dyb)
        gb[...] = _nt(dyb, cc_ref[0])
        da_re, da_im = _scan(gb, a_re, -a_im, reverse=True, other=xb)
        da_ref[0] = jnp.concatenate([da_re, da_im], axis=0)
        gc = gb[...].astype(BF16)
        du_ref[...] = _nt(gc, bb_ref[0]) + d_ref[0] * dyv
        dbb_ref[0] = _tn(ubb, gc)
        dd_ref[0] = jnp.sum(dyv * ub, axis=0, keepdims=True)

    blk = lambda j: (j, 0, 0)
    col = pl.BlockSpec((T, LANES), lambda j: (0, j))
    return pl.pallas_call(
        body, name="ssm_bwd", grid=(_SSM_BLOCKS,),
        in_specs=[col, col, pl.BlockSpec((1, LANES, 1024), blk), pl.BlockSpec((1, 2, 512), blk),
                  pl.BlockSpec((1, 1024, LANES), blk), pl.BlockSpec((1, 1, LANES), blk)],
        out_specs=[col, pl.BlockSpec((1, LANES, 1024), blk), pl.BlockSpec((1, 2, 512), blk),
                   pl.BlockSpec((1, 1024, LANES), blk), pl.BlockSpec((1, 1, LANES), blk)],
        out_shape=[_sds((T, SW)), _sds((4, LANES, 1024)), _sds((4, 2, 512)), _sds((4, 1024, LANES)), _sds((4, 1, LANES))],
        scratch_shapes=[pltpu.VMEM((T, 1024), F32), pltpu.VMEM((T, 1024), F32)],
        compiler_params=_cp(("parallel",)),
    )(u, dy, bb, ab, cc, dsk)


def _interleave(a):
    return a.reshape(NSEG, SEG, a.shape[1]).transpose(1, 0, 2).reshape(a.shape)


def _deinterleave(a):
    return a.reshape(SEG, NSEG, a.shape[1]).transpose(1, 0, 2).reshape(a.shape)


_TM_MIX = 256


def _mix_parts(att, y, wg, bg):
    y2 = jax.nn.gelu(y)
    y2b = y2.astype(BF16)
    sg = jax.nn.sigmoid(_nn(y2b, wg) + bg)
    ssm = y2 * sg
    ra, rs = _rms_r(att), _rms_r(ssm)
    return y2, y2b, sg, ssm, ra, rs


def _mix_fwd(att, y, x, wg, bg, ga, gs, wo):
    def body(att_ref, y_ref, x_ref, wg_ref, bg_ref, ga_ref, gs_ref, wo_ref, x1_ref):
        attv = att_ref[...]
        _, _, _, ssm, ra, rs = _mix_parts(attv, y_ref[...], wg_ref[...], bg_ref[...])
        ma = (attv * ra * ga_ref[...]).astype(BF16)
        ms = (ssm * rs * gs_ref[...]).astype(BF16)
        x1_ref[...] = x_ref[...] + _nn(ma, wo_ref[0:AW, :]) + _nn(ms, wo_ref[AW:D, :])

    row = lambda i: (i, 0)
    fixed = lambda i: (0, 0)
    half = pl.BlockSpec((_TM_MIX, AW), row)
    vec = pl.BlockSpec((1, AW), fixed)
    return pl.pallas_call(
        body, name="mix_fwd", grid=(T // _TM_MIX,),
        in_specs=[half, half, pl.BlockSpec((_TM_MIX, D), row), pl.BlockSpec((SW, SW), fixed), vec, vec, vec,
                  pl.BlockSpec((D, D), fixed)],
        out_specs=pl.BlockSpec((_TM_MIX, D), row), out_shape=_sds((T, D)),
        compiler_params=_cp(("parallel",)),
    )(att, y, x, wg, bg, ga, gs, wo)


def _mix_bwd(att, y, dx1, wg, bg, ga, gs, wo):
    last = T // _TM_MIX - 1

    def body(att_ref, y_ref, d_ref, wg_ref, bg_ref, ga_ref, gs_ref, wo_ref,
             datt_ref, dy_ref, dwg_ref, dbg_ref, dga_ref, dgs_ref, dwo_ref, dwg_acc, dwo_acc):
        attv, yv, db = att_ref[...], y_ref[...], d_ref[...].astype(BF16)
        y2, y2b, sg, ssm, ra, rs = _mix_parts(attv, yv, wg_ref[...], bg_ref[...])
        na, ns = attv * ra, ssm * rs
        ma = (na * ga_ref[...]).astype(BF16)
        ms = (ns * gs_ref[...]).astype(BF16)
        dma = _nt(db, wo_ref[0:AW, :])
        dms = _nt(db, wo_ref[AW:D, :])
        datt, dga = _rms_bwd(na, ra, ga_ref[...], dma)
        dssm, dgs = _rms_bwd(ns, rs, gs_ref[...], dms)
        dgl = dssm * y2 * sg * (1.0 - sg)
        dglb = dgl.astype(BF16)
        dy2 = dssm * sg + _nt(dglb, wg_ref[...])
        _, gelu_vjp = jax.vjp(jax.nn.gelu, yv)
        datt_ref[...] = datt
        dy_ref[...] = gelu_vjp(dy2)[0]

        @pl.when(pl.program_id(0) == 0)
        def _():
            dwg_acc[...] = jnp.zeros_like(dwg_acc)
            dwo_acc[...] = jnp.zeros_like(dwo_acc)
            dbg_ref[...] = jnp.zeros_like(dbg_ref)
            dga_ref[...] = jnp.zeros_like(dga_ref)
            dgs_ref[...] = jnp.zeros_like(dgs_ref)
        dwg_acc[...] += _tn(y2b, dglb)
        dbg_ref[...] += jnp.sum(dgl, axis=0, keepdims=True)
        dga_ref[...] += dga
        dgs_ref[...] += dgs
        dwo_acc[0:AW, :] += _tn(ma, db)
        dwo_acc[AW:D, :] += _tn(ms, db)

        @pl.when(pl.program_id(0) == last)
        def _():
            dwg_ref[...] = dwg_acc[...].astype(BF16)
            dwo_ref[...] = dwo_acc[...].astype(BF16)

    row = lambda i: (i, 0)
    fixed = lambda i: (0, 0)
    half = pl.BlockSpec((_TM_MIX, AW), row)
    vec = pl.BlockSpec((1, AW), fixed)
    sq = pl.BlockSpec((SW, SW), fixed)
    big = pl.BlockSpec((D, D), fixed)
    return pl.pallas_call(
        body, name="mix_bwd", grid=(T // _TM_MIX,),
        in_specs=[half, half, pl.BlockSpec((_TM_MIX, D), row), sq, vec, vec, vec, big],
        out_specs=[half, half, sq, vec, vec, vec, big],
        out_shape=[_sds((T, AW)), _sds((T, SW)), _sds((SW, SW), BF16), _sds((1, SW)), _sds((1, AW)), _sds((1, SW)),
                   _sds((D, D), BF16)],
        scratch_shapes=[pltpu.VMEM((SW, SW), F32), pltpu.VMEM((D, D), F32)],
        compiler_params=_cp(("arbitrary",)),
    )(att, y, dx1, wg, bg, ga, gs, wo)


_NCB = -(-CS // LANES)


def _ffn_up(x1, g, wu4):
    tm = 512

    def body(x_ref, g_ref, w_ref, h_ref, up_ref):
        @pl.when(pl.program_id(1) == 0)
        def _():
            xv = x_ref[...]
            h_ref[...] = (xv * _rms_r(xv) * g_ref[...]).astype(BF16)
        up_ref[0, 0] = _nn(h_ref[...], w_ref[0, 0])

    return pl.pallas_call(
        body, name="ffn_up", grid=(T // tm, NDEV),
        in_specs=[pl.BlockSpec((tm, D), lambda i, e: (i, 0)), pl.BlockSpec((1, D), lambda i, e: (0, 0)),
                  pl.BlockSpec((1, 1, D, CS), lambda i, e: (e // 4, e % 4, 0, 0))],
        out_specs=[pl.BlockSpec((tm, D), lambda i, e: (i, 0)),
                   pl.BlockSpec((1, 1, tm, CS), lambda i, e: (e // 4, e % 4, i, 0))],
        out_shape=[_sds((T, D), BF16), _sds((2, 4, T, CS))],
        compiler_params=_cp(("parallel", "arbitrary")),
    )(x1, g, wu4)


def _shift_down(h, k):
    row = lax.broadcasted_iota(jnp.int32, h.shape, 0)
    return jnp.where(row >= k, pltpu.roll(h, k, 0), 0.0)


def _shift_up(h, k):
    row = lax.broadcasted_iota(jnp.int32, h.shape, 0)
    return jnp.where(row < T - k, pltpu.roll(h, T - k, 0), 0.0)


def _conv(h, w, b):
    return w[0:1, :] * _shift_down(h, 2) + w[1:2, :] * _shift_down(h, 1) + w[2:3, :] * h + b


def _chan(gv, rows):
    return pl.BlockSpec((1, 1, rows, LANES), lambda d, j: (gv, d, 0, j))


def _ffn_act(up4, cw4, cb4):
    def body(ug_ref, uv_ref, wg_ref, wv_ref, bg_ref, bv_ref, act_ref):
        g = _conv(ug_ref[0, 0], wg_ref[0, 0], bg_ref[0, 0])
        v = _conv(uv_ref[0, 0], wv_ref[0, 0], bv_ref[0, 0])
        act_ref[0] = (g * jax.nn.sigmoid(g) * v).astype(BF16)

    return pl.pallas_call(
        body, name="ffn_act", grid=(4, _NCB),
        in_specs=[_chan(0, T), _chan(1, T), _chan(0, 3), _chan(1, 3), _chan(0, 1), _chan(1, 1)],
        out_specs=pl.BlockSpec((1, T, LANES), lambda d, j: (d, 0, j)), out_shape=_sds((4, T, CS), BF16),
        compiler_params=_cp(("parallel", "parallel")),
    )(up4, up4, cw4, cw4, cb4, cb4)


def _ffn_act_bwd(up4, dact, cw4, cb4):
    def conv_bwd(h, w, d):
        dup = w[2:3, :] * d + w[1:2, :] * _shift_up(d, 1) + w[0:1, :] * _shift_up(d, 2)
        dw = jnp.concatenate([jnp.sum(d * _shift_down(h, 2), axis=0, keepdims=True),
                              jnp.sum(d * _shift_down(h, 1), axis=0, keepdims=True),
                              jnp.sum(d * h, axis=0, keepdims=True)], axis=0)
        return dup, dw, jnp.sum(d, axis=0, keepdims=True)

    def body(ug_ref, uv_ref, da_ref, wg_ref, wv_ref, bg_ref, bv_ref, dup_ref, dw_ref, db_ref):
        ug, uv, da = ug_ref[0, 0], uv_ref[0, 0], da_ref[0]
        g = _conv(ug, wg_ref[0, 0], bg_ref[0, 0])
        v = _conv(uv, wv_ref[0, 0], bv_ref[0, 0])
        sg = jax.nn.sigmoid(g)
        dg = da * v * (sg * (1.0 + g * (1.0 - sg)))
        dv = da * (g * sg)
        dug, dwg, dbg = conv_bwd(ug, wg_ref[0, 0], dg)
        duv, dwv, dbv = conv_bwd(uv, wv_ref[0, 0], dv)
        dup_ref[0, 0] = dug.astype(BF16)
        dup_ref[1, 0] = duv.astype(BF16)
        dw_ref[0, 0] = dwg
        dw_ref[1, 0] = dwv
        db_ref[0, 0] = dbg
        db_ref[1, 0] = dbv

    both = lambda rows: pl.BlockSpec((2, 1, rows, LANES), lambda d, j: (0, d, 0, j))
    return pl.pallas_call(
        body, name="ffn_act_bwd", grid=(4, _NCB),
        in_specs=[_chan(0, T), _chan(1, T), pl.BlockSpec((1, T, LANES), lambda d, j: (d, 0, j)),
                  _chan(0, 3), _chan(1, 3), _chan(0, 1), _chan(1, 1)],
        out_specs=[both(T), both(3), both(1)],
        out_shape=[_sds((2, 4, T, CS), BF16), _sds((2, 4, 3, CS)), _sds((2, 4, 1, CS))],
        compiler_params=_cp(("parallel", "parallel")),
    )(up4, up4, dact, cw4, cw4, cb4, cb4)


def _ffn_down_loss(act, wd4, x1, target):
    tm = 256

    def body(a_ref, w_ref, x1_ref, t_ref, loss_ref, dx_ref, dxb_ref):
        x2 = x1_ref[...]
        for d in range(4):
            x2 = x2 + _nn(a_ref[d], w_ref[d])
        err = x2 - t_ref[...]
        dx = err * (1.0 / D)
        dx_ref[...] = dx
        dxb_ref[...] = dx.astype(BF16)

        @pl.when(pl.program_id(0) == 0)
        def _():
            loss_ref[...] = jnp.zeros_like(loss_ref)
        loss_ref[...] += 0.5 * jnp.sum(jnp.mean(err * err, axis=-1, keepdims=True), axis=0, keepdims=True)

    row = lambda i: (i, 0)
    fixed = lambda i: (0, 0)
    return pl.pallas_call(
        body, name="ffn_down_loss", grid=(T // tm,),
        in_specs=[pl.BlockSpec((4, tm, CS), lambda i: (0, i, 0)), pl.BlockSpec((4, CS, D), lambda i: (0, 0, 0)),
                  pl.BlockSpec((tm, D), row), pl.BlockSpec((tm, D), row)],
        out_specs=[pl.BlockSpec((1, 1), fixed), pl.BlockSpec((tm, D), row), pl.BlockSpec((tm, D), row)],
        out_shape=[_sds((1, 1)), _sds((T, D)), _sds((T, D), BF16)],
        compiler_params=_cp(("arbitrary",)),
    )(act, wd4, x1, target)


def _ffn_dact(dx2b, wd4):
    tm = 512

    def body(d_ref, w_ref, o_ref):
        o_ref[0] = _nt(d_ref[...], w_ref[0])

    return pl.pallas_call(
        body, name="ffn_dact", grid=(4, T // tm),
        in_specs=[pl.BlockSpec((tm, D), lambda d, i: (i, 0)), pl.BlockSpec((1, CS, D), lambda d, i: (d, 0, 0))],
        out_specs=pl.BlockSpec((1, tm, CS), lambda d, i: (d, i, 0)), out_shape=_sds((4, T, CS)),
        compiler_params=_cp(("parallel", "parallel")),
    )(dx2b, wd4)


def _ffn_dwd(act, dx2b):
    def body(a_ref, d_ref, o_ref):
        o_ref[0] = _tn(a_ref[0], d_ref[...]).astype(BF16)

    return pl.pallas_call(
        body, name="ffn_dwd", grid=(4,),
        in_specs=[pl.BlockSpec((1, T, CS), lambda d: (d, 0, 0)), pl.BlockSpec((T, D), lambda d: (0, 0))],
        out_specs=pl.BlockSpec((1, CS, D), lambda d: (d, 0, 0)), out_shape=_sds((4, CS, D), BF16),
        compiler_params=_cp(("parallel",)),
    )(act, dx2b)


def _ffn_dwu(h2, dup4):
    tm = 512

    def body(h_ref, d_ref, o_ref):
        o_ref[0, 0] = _tn(h_ref[...], d_ref[0, 0]).astype(BF16)

    return pl.pallas_call(
        body, name="ffn_dwu", grid=(NDEV, D // tm),
        in_specs=[pl.BlockSpec((T, tm), lambda e, i: (0, i)),
                  pl.BlockSpec((1, 1, T, CS), lambda e, i: (e // 4, e % 4, 0, 0))],
        out_specs=pl.BlockSpec((1, 1, tm, CS), lambda e, i: (e // 4, e % 4, i, 0)),
        out_shape=_sds((2, 4, D, CS), BF16),
        compiler_params=_cp(("parallel", "parallel")),
    )(h2, dup4)


def _ffn_up_bwd(dup4, wu4, x1, g, dres):
    tm = 256

    def body(d_ref, w_ref, x_ref, g_ref, dres_ref, dx_ref, dg_ref):
        dh = jnp.zeros((tm, D), F32)
        for gv in range(2):
            for d in range(4):
                dh = dh + _nt(d_ref[gv, d], w_ref[gv, d])
        xv = x_ref[...]
        r = _rms_r(xv)
        dxr, dgp = _rms_bwd(xv * r, r, g_ref[...], dh)
        dx_ref[...] = dres_ref[...] + dxr

        @pl.when(pl.program_id(0) == 0)
        def _():
            dg_ref[...] = jnp.zeros_like(dg_ref)
        dg_ref[...] += dgp

    row = lambda i: (i, 0)
    fixed = lambda i: (0, 0)
    return pl.pallas_call(
        body, name="ffn_up_bwd", grid=(T // tm,),
        in_specs=[pl.BlockSpec((2, 4, tm, CS), lambda i: (0, 0, i, 0)), pl.BlockSpec((2, 4, D, CS), lambda i: (0, 0, 0, 0)),
                  pl.BlockSpec((tm, D), row), pl.BlockSpec((1, D), fixed), pl.BlockSpec((tm, D), row)],
        out_specs=[pl.BlockSpec((tm, D), row), pl.BlockSpec((1, D), fixed)],
        out_shape=[_sds((T, D)), _sds((1, D))],
        compiler_params=_cp(("arbitrary",)),
    )(dup4, wu4, x1, g, dres)


def _adamw_math(w, g, m, v):
    m = ADAM_B1 * m + (1.0 - ADAM_B1) * g
    v = ADAM_B2 * v + (1.0 - ADAM_B2) * jnp.square(g)
    m_hat = m / (1.0 - ADAM_B1 ** ADAM_STEP)
    v_hat = v / (1.0 - ADAM_B2 ** ADAM_STEP)
    delta = -ADAM_LR * (m_hat / (jnp.sqrt(v_hat) + ADAM_EPS) + ADAM_WD * w)
    return delta, m, v


def _where_am_i():
    x, y, c = _place()
    return jnp.stack([c, 2 * x + y]).astype(jnp.int32)


def _pair_sum(a4, recv, name):
    _, _, rows, cols = a4.shape
    tr = _row_tile(rows, cols, 3 << 20)

    def body(sel_ref, own_ref, recv_ref, out_ref):
        out_ref[...] = (own_ref[0].astype(F32) + recv_ref[...].astype(F32)).astype(out_ref.dtype)

    grid_spec = pltpu.PrefetchScalarGridSpec(
        num_scalar_prefetch=1, grid=(4, rows // tr),
        in_specs=[pl.BlockSpec((1, 1, tr, cols), lambda k, i, s: (k, s[0], i, 0)),
                  pl.BlockSpec((1, tr, cols), lambda k, i, s: (k, i, 0))],
        out_specs=pl.BlockSpec((1, tr, cols), lambda k, i, s: (k, i, 0)),
    )
    return pl.pallas_call(
        body, name=name, grid_spec=grid_spec, out_shape=_sds((4, rows, cols), a4.dtype),
        compiler_params=_cp(("arbitrary", "arbitrary")),
    )(_where_am_i(), a4, recv)


def _adamw_rs(a4, recv, from_chips, w, m, v, name):
    rows, cols = w.shape
    tr = _row_tile(rows, cols, 3 << 19)

    def body(sel_ref, own_ref, recv_ref, fc_ref, w_ref, m_ref, v_ref, g_ref, d_ref, nm_ref, nv_ref):
        g = own_ref[0, 0].astype(F32) + recv_ref[0].astype(F32)
        for j in range(3):
            g = g + fc_ref[j].astype(F32)
        d, nm, nv = _adamw_math(w_ref[...], g, m_ref[...], v_ref[...])
        g_ref[...] = g
        d_ref[...] = d
        nm_ref[...] = nm
        nv_ref[...] = nv

    flat = pl.BlockSpec((tr, cols), lambda i, s: (i, 0))
    grid_spec = pltpu.PrefetchScalarGridSpec(
        num_scalar_prefetch=1, grid=(rows // tr,),
        in_specs=[pl.BlockSpec((1, 1, tr, cols), lambda i, s: (s[1], s[0], i, 0)),
                  pl.BlockSpec((1, tr, cols), lambda i, s: (s[1], i, 0)),
                  pl.BlockSpec((3, tr, cols), lambda i, s: (0, i, 0)), flat, flat, flat],
        out_specs=[flat] * 4,
    )
    return pl.pallas_call(
        body, name=name, grid_spec=grid_spec, out_shape=[_sds((rows, cols))] * 4,
        compiler_params=_cp(("arbitrary",)),
    )(_where_am_i(), a4, recv, from_chips, w, m, v)


def _adamw_small(gall, w, m, v):
    rows = w.shape[0]

    def body(ga_ref, w_ref, m_ref, v_ref, g_ref, d_ref, nm_ref, nv_ref):
        g = ga_ref[0]
        for dev in range(1, NDEV):
            g = g + ga_ref[dev]
        d, nm, nv = _adamw_math(w_ref[...], g[0:rows], m_ref[...], v_ref[...])
        g_ref[...] = g
        d_ref[...] = d
        nm_ref[...] = nm
        nv_ref[...] = nv

    return pl.pallas_call(body, name="adamw_small", out_shape=[_sds((rows + 8, LANES))] + [_sds((rows, LANES))] * 3,
                          compiler_params=_cp())(gall, w, m, v)


def _rows128(a):
    n = a.shape[0]
    r = (-n) % (8 * LANES)
    if r:
        a = jnp.pad(a, [(0, r)])
    return a.reshape(-1, LANES)


_SMALL = (("g_mix", D), ("b_f", NH), ("g_q", DH), ("g_k", DH), ("lambda_re", NG * NP), ("lambda_im", NG * NP),
          ("log_step", NG), ("b_re", NG * NP * GS), ("b_im", NG * NP * GS), ("c_re", NG * GS * NP),
          ("c_im", NG * GS * NP), ("d_skip", NG * GS), ("b_glu", SW), ("g_attn_out", AW), ("g_ssm_out", SW),
          ("g_ffn", D), ("conv_b", 2 * DFF))


def _pack_small(d):
    return jnp.concatenate([_rows128(d[name].reshape(-1).astype(F32)) for name, _ in _SMALL], axis=0)


def _unpack_small(p, shapes):
    out, off = {}, 0
    for name, n in _SMALL:
        rows = -(-n // (8 * LANES)) * 8
        out[name] = p[off:off + rows].reshape(-1)[:n].reshape(shapes[name])
        off += rows
    return out


def _block_diag(m):
    eye = jnp.eye(8, dtype=m.dtype)
    r, c = m.shape[2], m.shape[3]
    return (m[:, :, :, None, :] * eye[None, :, None, :, None]).reshape(4, 8 * r, 8 * c)


def _diag_blocks(dense, r, c):
    eye = jnp.eye(8, dtype=dense.dtype)
    return jnp.sum(dense.reshape(4, 8, r, 8, c) * eye[None, :, None, :, None], axis=3)


def kernel(x, g_mix, w_in, b_f, g_q, g_k, lambda_re, lambda_im, log_step, b_re, b_im, c_re, c_im, d_skip, w_glu, b_glu, g_attn_out, g_ssm_out, w_out, g_ffn, w_up, conv_w, conv_b, w_down, loss_target, m_g_mix, m_w_in, m_b_f, m_g_q, m_g_k, m_lambda_re, m_lambda_im, m_log_step, m_b_re, m_b_im, m_c_re, m_c_im, m_d_skip, m_w_glu, m_b_glu, m_g_attn_out, m_g_ssm_out, m_w_out, m_g_ffn, m_w_up, m_conv_w, m_conv_b, m_w_down, v_g_mix, v_w_in, v_b_f, v_g_q, v_g_k, v_lambda_re, v_lambda_im, v_log_step, v_b_re, v_b_im, v_c_re, v_c_im, v_d_skip, v_w_glu, v_b_glu, v_g_attn_out, v_g_ssm_out, v_w_out, v_g_ffn, v_w_up, v_conv_w, v_conv_b, v_w_down):
    weights = dict(g_mix=g_mix, w_in=w_in, b_f=b_f, g_q=g_q, g_k=g_k, lambda_re=lambda_re, lambda_im=lambda_im,
                   log_step=log_step, b_re=b_re, b_im=b_im, c_re=c_re, c_im=c_im, d_skip=d_skip, w_glu=w_glu,
                   b_glu=b_glu, g_attn_out=g_attn_out, g_ssm_out=g_ssm_out, w_out=w_out, g_ffn=g_ffn, w_up=w_up,
                   conv_w=conv_w, conv_b=conv_b, w_down=w_down)
    mom_m = dict(g_mix=m_g_mix, w_in=m_w_in, b_f=m_b_f, g_q=m_g_q, g_k=m_g_k, lambda_re=m_lambda_re,
                 lambda_im=m_lambda_im, log_step=m_log_step, b_re=m_b_re, b_im=m_b_im, c_re=m_c_re, c_im=m_c_im,
                 d_skip=m_d_skip, w_glu=m_w_glu, b_glu=m_b_glu, g_attn_out=m_g_attn_out, g_ssm_out=m_g_ssm_out,
                 w_out=m_w_out, g_ffn=m_g_ffn, w_up=m_w_up, conv_w=m_conv_w, conv_b=m_conv_b, w_down=m_w_down)
    mom_v = dict(g_mix=v_g_mix, w_in=v_w_in, b_f=v_b_f, g_q=v_g_q, g_k=v_g_k, lambda_re=v_lambda_re,
                 lambda_im=v_lambda_im, log_step=v_log_step, b_re=v_b_re, b_im=v_b_im, c_re=v_c_re, c_im=v_c_im,
                 d_skip=v_d_skip, w_glu=v_w_glu, b_glu=v_b_glu, g_attn_out=v_g_attn_out, g_ssm_out=v_g_ssm_out,
                 w_out=v_w_out, g_ffn=v_g_ffn, w_up=v_w_up, conv_w=v_conv_w, conv_b=v_conv_b, w_down=v_w_down)
    names = list(weights)
    big = ("w_in", "w_glu", "w_out", "w_up", "w_down", "conv_w")

    xs = x[0]
    target = loss_target[0]
    row = lambda a: a.reshape(1, -1)

    (win_g,) = _all_gather([w_in.astype(BF16)], "gather_w_in")
    w_in_f = win_g.transpose(1, 0, 2).reshape(D, 8 * 257)
    wcat = jnp.concatenate([w_in_f[:, 0:1536], w_in_f[:, 1544:2056], w_in_f[:, 1536:1544],
                            jnp.zeros((D, LANES - NH), BF16)], axis=1)
    cb4 = conv_b.reshape(2, 4, 1, CS)

    h1, z, f_t = _in_proj(xs, row(g_mix), wcat)
    bf_col = b_f.reshape(NH, 1)
    ck = _gates_fwd(f_t, bf_col).reshape(_HP, 2, T)
    att, lse, (wg_g, wo_g, wu_g, cw_g) = _attn_fwd(
        z, ck, row(g_q), row(g_k), [w_glu.astype(BF16), w_out.astype(BF16), w_up.astype(BF16), conv_w])

    ls_col = log_step.reshape(NG, 1)
    br_t, bi_t = b_re.transpose(0, 2, 1), b_im.transpose(0, 2, 1)
    ab_re, ab_im, bb_re, bb_im = _disc_fwd(lambda_re, lambda_im, ls_col, br_t, bi_t)
    bb = jnp.concatenate([_block_diag(bb_re.reshape(4, 8, GS, NP)), _block_diag(bb_im.reshape(4, 8, GS, NP))],
                         axis=2).astype(BF16)
    ab = jnp.stack([ab_re.reshape(4, 512), ab_im.reshape(4, 512)], axis=1)
    cdiag = lambda cm: _block_diag(cm.reshape(4, 8, GS, NP).transpose(0, 1, 3, 2))
    cc = jnp.concatenate([cdiag(c_re), -cdiag(c_im)], axis=1).astype(BF16)
    dsk = d_skip.reshape(4, 1, LANES)
    u_il = _interleave(z[:, 1536:2048])
    y_il, (wd_g,) = _ssm_fwd(u_il, bb, ab, cc, dsk, [w_down.astype(BF16)])
    y = _deinterleave(y_il)
    wg_g, wo_g, wu_g, cw_g, wd_g = _gather_second([wg_g, wo_g, wu_g, cw_g, wd_g], "gather_pass_on")
    wg_f = wg_g.reshape(SW, SW)
    wo_f = wo_g.reshape(D, D)
    wu4 = wu_g.reshape(2, 4, D, CS)
    wd4 = wd_g.reshape(4, CS, D)
    cw4 = cw_g.reshape(2, 4, 3, CS)

    x1 = _mix_fwd(att, y, xs, wg_f, row(b_glu), row(g_attn_out), row(g_ssm_out), wo_f)
    h2, up4 = _ffn_up(x1, row(g_ffn), wu4)
    act = _ffn_act(up4, cw4, cb4)
    loss_dev, dx2, dx2b = _ffn_down_loss(act, wd4, x1, target)

    dact = _ffn_dact(dx2b, wd4)
    d_wd = _ffn_dwd(act, dx2b)
    dup4, dcw4, dcb4 = _ffn_act_bwd(up4, dact, cw4, cb4)
    d_wu = _ffn_dwu(h2, dup4)
    mlp = ("w_up", "w_down", "conv_w")
    by_dest = {"w_up": d_wu.reshape(4, 2, D, CS), "w_down": d_wd.reshape(4, 2, DFF // NDEV, D),
               "conv_w": dcw4.reshape(4, 2, 3, CS)}
    from_sibling = dict(zip(mlp, _swap_sibling([by_dest[n] for n in mlp], "rs_pair_swap_mlp")))
    chip_sums = {n: _pair_sum(by_dest[n], from_sibling[n], "rs_pair_sum_" + n) for n in mlp}
    dx1, dg_ffn = _ffn_up_bwd(dup4, wu4, x1, row(g_ffn), dx2)
    datt, dy, d_wg, d_bg, d_ga, d_gs, d_wo = _mix_bwd(att, y, dx1, wg_f, row(b_glu), row(g_attn_out),
                                                       row(g_ssm_out), wo_f)
    du_il, dbb, dab, dcc, ddsk = _ssm_bwd(u_il, _interleave(dy), bb, ab, cc, dsk)
    du = _deinterleave(du_il).astype(BF16)
    dbb_re = _diag_blocks(dbb[:, :, 0:512], GS, NP).reshape(NG, GS, NP)
    dbb_im = _diag_blocks(dbb[:, :, 512:1024], GS, NP).reshape(NG, GS, NP)
    dlr, dli, dls, dbr_t, dbi_t = _disc_bwd(lambda_re, lambda_im, ls_col, br_t, bi_t,
                                            dab[:, 0].reshape(NG, NP), dab[:, 1].reshape(NG, NP), dbb_re, dbb_im)
    d_cre = _diag_blocks(dcc[:, 0:512], NP, GS).transpose(0, 1, 3, 2).reshape(NG, GS, NP)
    d_cim = -_diag_blocks(dcc[:, 512:1024], NP, GS).transpose(0, 1, 3, 2).reshape(NG, GS, NP)

    dq, dk, dv, dck, d_gq, d_gk, fc_mlp = _attn_bwd(z, ck, row(g_q), row(g_k), lse, datt, [chip_sums[n] for n in mlp])
    from_chips = dict(zip(mlp, fc_mlp))
    df_t, d_bf = _gates_bwd(dck.reshape(NH, T), f_t, bf_col)
    df = jnp.concatenate([df_t.T, jnp.zeros((T, LANES - NH), F32)], axis=1).astype(BF16)
    d_wcat = _in_proj_dw(h1, dq, dk, dv, du, df)
    grad_x, dg_mix = _in_proj_bwd(dq, dk, dv, du, df, wcat, xs, row(g_mix), dx1)
    d_win = jnp.concatenate([d_wcat[:, 0:1536], d_wcat[:, 2048:2048 + NH], d_wcat[:, 1536:2048]], axis=1)
    d_win = d_win.astype(BF16).reshape(D, NDEV, 257).transpose(1, 0, 2)

    mixer = ("w_out", "w_glu", "w_in")
    by_dest.update({"w_out": d_wo.reshape(4, 2, D // NDEV, D), "w_glu": d_wg.reshape(4, 2, SW // NDEV, SW),
                    "w_in": d_win.reshape(4, 2, D, 257)})
    from_sibling.update(zip(mixer, _swap_sibling([by_dest[n] for n in mixer], "rs_pair_swap_mixer")))
    chip_sums.update({n: _pair_sum(by_dest[n], from_sibling[n], "rs_pair_sum_" + n) for n in mixer})
    from_chips.update(zip(mixer, _swap_chips([chip_sums[n] for n in mixer], "rs_chip_swap_mixer")))

    grads, deltas, new_m, new_v = {}, {}, {}, {}
    for n in mlp + mixer:
        grads[n], deltas[n], new_m[n], new_v[n] = _adamw_rs(by_dest[n], from_sibling[n], from_chips[n], weights[n],
                                                            mom_m[n], mom_v[n], "adamw_" + n)

    small = dict(g_mix=dg_mix, b_f=d_bf, g_q=d_gq, g_k=d_gk, lambda_re=dlr, lambda_im=dli, log_step=dls,
                 b_re=dbr_t.transpose(0, 2, 1), b_im=dbi_t.transpose(0, 2, 1), c_re=d_cre, c_im=d_cim, d_skip=ddsk,
                 b_glu=d_bg, g_attn_out=d_ga, g_ssm_out=d_gs, g_ffn=dg_ffn, conv_b=dcb4)
    loss_tile = jnp.pad(loss_dev, ((0, 7), (0, LANES - 1)))
    (small_all,) = _all_gather([jnp.concatenate([_pack_small(small), loss_tile], axis=0)], "gather_small_grads")
    sg_p, sd_p, sm_p, sv_p = _adamw_small(small_all, _pack_small({n: weights[n] for n in small}),
                                          _pack_small({n: mom_m[n] for n in small}),
                                          _pack_small({n: mom_v[n] for n in small}))
    loss = sg_p[sg_p.shape[0] - 8, 0]
    shapes = {n: weights[n].shape for n in small}
    sg, sd, sm, sv = (_unpack_small(p, shapes) for p in (sg_p, sd_p, sm_p, sv_p))
    for n in small:
        grads[n], deltas[n], new_m[n], new_v[n] = sg[n], sd[n], sm[n], sv[n]

    return (loss, grad_x[None], *[grads[n] for n in names], *[deltas[n] for n in names],
            *[new_m[n] for n in names], *[new_v[n] for n in names])
```

```python
import jax
import jax.numpy as jnp
from jax import lax
from jax.experimental import pallas as pl
from jax.experimental.pallas import tpu as pltpu

F32, BF16 = jnp.float32, jnp.bfloat16
T, D = 2048, 1024
NH, DH = 8, 64
AW, SW = 512, 512
NG, GS, NP = 32, 16, 64
DFF = 2816
NDEV = 8
CS = 2 * DFF // NDEV
EPS = 1e-6
NEG = -1e30
ZC = 4 * 512 + 128
SEG = 256
NSEG = T // SEG
BQ = 256
VMEM_LIMIT = 56 * 1024 * 1024
LANES = 128
MESH = pl.DeviceIdType.MESH
HI = lax.Precision.HIGHEST

ADAM_LR, ADAM_B1, ADAM_B2, ADAM_EPS, ADAM_WD, ADAM_STEP = 0.001, 0.9, 0.999, 1e-08, 0.01, 10


def _cp(dims=None):
    if dims is None:
        return pltpu.CompilerParams(vmem_limit_bytes=VMEM_LIMIT)
    return pltpu.CompilerParams(dimension_semantics=dims, vmem_limit_bytes=VMEM_LIMIT)


def _sds(shape, dtype=F32):
    return jax.ShapeDtypeStruct(shape, dtype)


def _nt(a, b):
    return lax.dot_general(a, b, (((1,), (1,)), ((), ())), preferred_element_type=F32)


def _tn(a, b):
    return lax.dot_general(a, b, (((0,), (0,)), ((), ())), preferred_element_type=F32)


def _nn(a, b):
    return jnp.dot(a, b, preferred_element_type=F32)


def _rms_r(x):
    return lax.rsqrt(jnp.mean(x * x, axis=-1, keepdims=True) + EPS)


def _rms_bwd(n, r, g, dh):
    dn = dh * g
    dx = r * (dn - n * jnp.mean(dn * n, axis=-1, keepdims=True))
    return dx, jnp.sum(dh * n, axis=0, keepdims=True)


def _row_tile(rows, cols, limit):
    tr = rows
    while tr * cols * 4 > limit and tr % 32 == 0:
        tr //= 2
    return tr


def _place():
    return lax.axis_index("x"), lax.axis_index("y"), lax.axis_index("c")


def _all_gather(shards, name):
    n = len(shards)

    def body(*refs):
        x_refs, out_refs = refs[:n], refs[n:2 * n]
        send_sems, recv_sems, local_sems = refs[2 * n:]
        x, y, c = _place()
        me, sibling = (x, y, c), (x, y, 1 - c)
        chips = [(1 - x, y), (x, 1 - y), (1 - x, 1 - y)]

        def copy(a, k, block, to, src=None):
            px, py, pc = block
            slot = out_refs[a].at[4 * px + 2 * py + pc]
            return pltpu.make_async_remote_copy(
                src_ref=slot if src is None else src, dst_ref=slot,
                send_sem=send_sems.at[7 * a + k], recv_sem=recv_sems.at[7 * a + k], device_id=to, device_id_type=MESH)

        mine, first, passed = [], [], []
        for a in range(n):
            cp = pltpu.make_async_copy(x_refs[a], out_refs[a].at[4 * x + 2 * y + c], local_sems.at[a])
            cp.start()
            mine.append(cp)
            cps = [copy(a, 0, me, sibling, src=x_refs[a])]
            cps += [copy(a, 1 + j, me, (*chip, c), src=x_refs[a]) for j, chip in enumerate(chips)]
            for cp in cps:
                cp.start()
            first += cps
        for a in range(n):
            for j, chip in enumerate(chips):
                copy(a, 1 + j, (*chip, c), me).wait_recv()
                fwd = copy(a, 4 + j, (*chip, c), sibling)
                fwd.start()
                passed.append(fwd)
        for a in range(n):
            copy(a, 0, sibling, me).wait_recv()
            for j, chip in enumerate(chips):
                copy(a, 4 + j, (*chip, 1 - c), me).wait_recv()
        for cp in first + passed:
            cp.wait_send()
        for cp in mine:
            cp.wait()

    anyspec = pl.BlockSpec(memory_space=pl.ANY)
    return pl.pallas_call(
        body, name=name,
        out_shape=[_sds((NDEV, *s.shape), s.dtype) for s in shards],
        in_specs=[anyspec] * n, out_specs=[anyspec] * n,
        scratch_shapes=[pltpu.SemaphoreType.DMA((7 * n,)), pltpu.SemaphoreType.DMA((7 * n,)),
                        pltpu.SemaphoreType.DMA((n,))],
    )(*shards)


def _gather_first_copies(x_refs, out_refs, send_sems, recv_sems, local_sems):
    x, y, c = _place()
    slot = 4 * x + 2 * y + c
    peers = [(x, y, 1 - c), (1 - x, y, c), (x, 1 - y, c), (1 - x, 1 - y, c)]
    local, remote = [], []
    for a, (x_ref, out_ref) in enumerate(zip(x_refs, out_refs)):
        local.append(pltpu.make_async_copy(x_ref, out_ref.at[slot], local_sems.at[a]))
        for k, peer in enumerate(peers):
            remote.append(pltpu.make_async_remote_copy(
                src_ref=x_ref, dst_ref=out_ref.at[slot], send_sem=send_sems.at[4 * a + k],
                recv_sem=recv_sems.at[4 * a + k], device_id=peer, device_id_type=MESH))
    return local, remote


def _gather_first_scratch(n):
    return [pltpu.SemaphoreType.DMA((4 * n,)), pltpu.SemaphoreType.DMA((4 * n,)), pltpu.SemaphoreType.DMA((n,))]


def _gather_second(bufs, name):
    n = len(bufs)

    def body(*refs):
        out_refs = refs[n:2 * n]
        send_sems, recv_sems = refs[2 * n:]
        x, y, c = _place()
        chips = [(1 - x, y), (x, 1 - y), (1 - x, 1 - y)]
        cps = []
        for a in range(n):
            for j, (px, py) in enumerate(chips):
                block = out_refs[a].at[4 * px + 2 * py + c]
                cps.append(pltpu.make_async_remote_copy(
                    src_ref=block, dst_ref=block, send_sem=send_sems.at[3 * a + j], recv_sem=recv_sems.at[3 * a + j],
                    device_id=(x, y, 1 - c), device_id_type=MESH))
        for cp in cps:
            cp.start()
        for cp in cps:
            cp.wait()

    anyspec = pl.BlockSpec(memory_space=pl.ANY)
    return pl.pallas_call(
        body, name=name,
        out_shape=[_sds(b.shape, b.dtype) for b in bufs],
        in_specs=[anyspec] * n, out_specs=[anyspec] * n,
        input_output_aliases={a: a for a in range(n)},
        scratch_shapes=[pltpu.SemaphoreType.DMA((3 * n,)), pltpu.SemaphoreType.DMA((3 * n,))],
    )(*bufs)


def _chip_swap_copies(p_refs, out_refs, send_sems, recv_sems):
    x, y, c = _place()
    chips = [(1 - x, y), (x, 1 - y), (1 - x, 1 - y)]
    cps = []
    for a, (p_ref, out_ref) in enumerate(zip(p_refs, out_refs)):
        for j, (px, py) in enumerate(chips):
            cps.append(pltpu.make_async_remote_copy(
                src_ref=p_ref.at[2 * px + py], dst_ref=out_ref.at[j],
                send_sem=send_sems.at[3 * a + j], recv_sem=recv_sems.at[3 * a + j],
                device_id=(px, py, c), device_id_type=MESH))
    return cps


def _swap_sibling(arrs, name):
    n = len(arrs)

    def body(*refs):
        a_refs, out_refs = refs[:n], refs[n:2 * n]
        send_sems, recv_sems = refs[2 * n:]
        x, y, c = _place()
        cps = []
        for a in range(n):
            for k in range(4):
                cps.append(pltpu.make_async_remote_copy(
                    src_ref=a_refs[a].at[k, 1 - c], dst_ref=out_refs[a].at[k],
                    send_sem=send_sems.at[4 * a + k], recv_sem=recv_sems.at[4 * a + k],
                    device_id=(x, y, 1 - c), device_id_type=MESH))
        for cp in cps:
            cp.start()
        for cp in cps:
            cp.wait()

    anyspec = pl.BlockSpec(memory_space=pl.ANY)
    return pl.pallas_call(
        body, name=name,
        out_shape=[_sds((4, *a.shape[2:]), a.dtype) for a in arrs],
        in_specs=[anyspec] * n, out_specs=[anyspec] * n,
        scratch_shapes=[pltpu.SemaphoreType.DMA((4 * n,)), pltpu.SemaphoreType.DMA((4 * n,))],
    )(*arrs)


def _swap_chips(parts, name):
    n = len(parts)

    def body(*refs):
        cps = _chip_swap_copies(refs[:n], refs[n:2 * n], *refs[2 * n:])
        for cp in cps:
            cp.start()
        for cp in cps:
            cp.wait()

    anyspec = pl.BlockSpec(memory_space=pl.ANY)
    return pl.pallas_call(
        body, name=name,
        out_shape=[_sds((3, *p.shape[1:]), p.dtype) for p in parts],
        in_specs=[anyspec] * n, out_specs=[anyspec] * n,
        scratch_shapes=[pltpu.SemaphoreType.DMA((3 * n,)), pltpu.SemaphoreType.DMA((3 * n,))],
    )(*parts)


def _in_proj(x, g, wcat):
    tm = 256

    def body(x_ref, g_ref, w_ref, h_ref, z_ref, ft_ref):
        xv = x_ref[...]
        h = (xv * _rms_r(xv) * g_ref[...]).astype(BF16)
        h_ref[...] = h
        z = _nn(h, w_ref[...])
        z_ref[...] = z
        ft_ref[...] = z[:, 2048:ZC].T[0:NH, :]

    row = lambda i: (i, 0)
    fixed = lambda i: (0, 0)
    return pl.pallas_call(
        body, name="in_proj", grid=(T // tm,),
        in_specs=[pl.BlockSpec((tm, D), row), pl.BlockSpec((1, D), fixed), pl.BlockSpec((D, ZC), fixed)],
        out_specs=[pl.BlockSpec((tm, D), row), pl.BlockSpec((tm, ZC), row), pl.BlockSpec((NH, tm), lambda i: (0, i))],
        out_shape=[_sds((T, D), BF16), _sds((T, ZC)), _sds((NH, T))],
        compiler_params=_cp(("parallel",)),
    )(x, g, wcat)


def _in_proj_dw(h, dq, dk, dv, du, df):
    tm = 512

    def body(h_ref, dq_ref, dk_ref, dv_ref, du_ref, df_ref, o_ref):
        hv = h_ref[...]
        for j, d_ref in enumerate((dq_ref, dk_ref, dv_ref, du_ref)):
            o_ref[:, 512 * j:512 * (j + 1)] = _tn(hv, d_ref[...])
        o_ref[:, 2048:ZC] = _tn(hv, df_ref[...])

    full = lambda w: pl.BlockSpec((T, w), lambda i: (0, 0))
    return pl.pallas_call(
        body, name="in_proj_dw", grid=(D // tm,),
        in_specs=[pl.BlockSpec((T, tm), lambda i: (0, i)), full(512), full(512), full(512), full(512), full(LANES)],
        out_specs=pl.BlockSpec((tm, ZC), lambda i: (i, 0)), out_shape=_sds((D, ZC)),
        compiler_params=_cp(("parallel",)),
    )(h, dq, dk, dv, du, df)


def _in_proj_bwd(dq, dk, dv, du, df, wcat, x, g, dres):
    tm = 256

    def body(dq_ref, dk_ref, dv_ref, du_ref, df_ref, w_ref, x_ref, g_ref, dres_ref, dx_ref, dg_ref):
        dh = _nt(df_ref[...], w_ref[:, 2048:ZC])
        for j, d_ref in enumerate((dq_ref, dk_ref, dv_ref, du_ref)):
            dh = dh + _nt(d_ref[...], w_ref[:, 512 * j:512 * (j + 1)])
        xv = x_ref[...]
        r = _rms_r(xv)
        dxr, dgp = _rms_bwd(xv * r, r, g_ref[...], dh)
        dx_ref[...] = dres_ref[...] + dxr

        @pl.when(pl.program_id(0) == 0)
        def _():
            dg_ref[...] = jnp.zeros_like(dg_ref)
        dg_ref[...] += dgp

    row = lambda i: (i, 0)
    fixed = lambda i: (0, 0)
    part = pl.BlockSpec((tm, 512), row)
    return pl.pallas_call(
        body, name="in_proj_bwd", grid=(T // tm,),
        in_specs=[part, part, part, part, pl.BlockSpec((tm, LANES), row), pl.BlockSpec((D, ZC), fixed),
                  pl.BlockSpec((tm, D), row), pl.BlockSpec((1, D), fixed), pl.BlockSpec((tm, D), row)],
        out_specs=[pl.BlockSpec((tm, D), row), pl.BlockSpec((1, D), fixed)],
        out_shape=[_sds((T, D)), _sds((1, D))],
        compiler_params=_cp(("arbitrary",)),
    )(dq, dk, dv, du, df, wcat, x, g, dres)


def _stack_lanes(v):
    return jnp.concatenate([v[:, LANES * b:LANES * (b + 1)] for b in range(T // LANES)], axis=0)


def _unstack_lanes(s):
    return jnp.concatenate([s[8 * b:8 * b + 8, :] for b in range(T // LANES)], axis=1)


def _cumsum_lanes(v, reverse):
    st = _stack_lanes(v)
    ri = lax.broadcasted_iota(jnp.int32, (LANES, LANES), 0)
    ci = lax.broadcasted_iota(jnp.int32, (LANES, LANES), 1)
    tri = (ri >= ci) if reverse else (ri <= ci)
    intra = jnp.dot(st, tri.astype(F32), precision=HI, preferred_element_type=F32)
    tot = intra[:, 0:1] if reverse else intra[:, LANES - 1:LANES]
    same_head = (ci % 8) == (ri % 8)
    other = (ci // 8 > ri // 8) if reverse else (ci // 8 < ri // 8)
    off = jnp.dot((same_head & other).astype(F32), jnp.broadcast_to(tot, (LANES, LANES)), precision=HI,
                  preferred_element_type=F32)
    return _unstack_lanes(intra + off)


def _gates_fwd(f_t, b_f):
    def body(f_ref, b_ref, c_ref):
        z = f_ref[...] + b_ref[...]
        lf = jnp.minimum(z, 0.0) - jnp.log(1.0 + jnp.exp(-jnp.abs(z)))
        c_ref[...] = _cumsum_lanes(lf, reverse=False)

    return pl.pallas_call(body, name="gates_fwd", out_shape=_sds((NH, T)), compiler_params=_cp())(f_t, b_f)


def _gates_bwd(dck, f_t, b_f):
    def body(d_ref, f_ref, b_ref, df_ref, db_ref):
        z = f_ref[...] + b_ref[...]
        dlf = _cumsum_lanes(d_ref[...], reverse=True)
        df = dlf * jax.nn.sigmoid(-z)
        df_ref[...] = df
        db_ref[...] = jnp.sum(df, axis=1, keepdims=True)

    return pl.pallas_call(body, name="gates_bwd", out_shape=[_sds((NH, T)), _sds((NH, 1))],
                          compiler_params=_cp())(dck, f_t, b_f)


def _lower_triangle():
    rows = lax.broadcasted_iota(jnp.int32, (BQ, BQ), 0)
    cols = lax.broadcasted_iota(jnp.int32, (BQ, BQ), 1)
    return cols <= rows


def _logit_parts(qb, kb, ck_row, lo, hi, tri):
    parts = []
    if lo > 0:
        parts.append((_nt(qb[lo:hi], kb[0:lo]) - ck_row[:, 0:lo], 0, lo))
    parts.append((jnp.where(tri, _nt(qb[lo:hi], kb[lo:hi]) - ck_row[:, lo:hi], NEG), lo, hi))
    return parts


_HP = NH // 2


def _qkv_specs():
    return [pl.BlockSpec((T, LANES), lambda p: (0, p)), pl.BlockSpec((T, LANES), lambda p: (0, _HP + p)),
            pl.BlockSpec((T, LANES), lambda p: (0, 2 * _HP + p))]


def _attn_fwd(z, ck, g_q, g_k, shards):
    scale = DH ** -0.5
    n = len(shards)

    def body(q_ref, k_ref, v_ref, ck_ref, gq_ref, gk_ref, *rest):
        x_refs, (o_ref, lse_ref), out_refs, sems = rest[:n], rest[n:n + 2], rest[n + 2:2 * n + 2], rest[2 * n + 2:]

        @pl.when(pl.program_id(0) == 0)
        def _():
            local, remote = _gather_first_copies(x_refs, out_refs, *sems)
            for cp in local + remote:
                cp.start()

        qb, kb, vb = [], [], []
        for hh in range(2):
            cols = slice(DH * hh, DH * (hh + 1))
            qv, kv = q_ref[:, cols], k_ref[:, cols]
            qb.append((qv * _rms_r(qv) * gq_ref[...] * scale).astype(BF16))
            kb.append((kv * _rms_r(kv) * gk_ref[...]).astype(BF16))
            vb.append(v_ref[:, cols].astype(BF16))
        tri = _lower_triangle()
        for i in range(T // BQ):
            lo, hi = i * BQ, (i + 1) * BQ
            outs = []
            for hh in range(2):
                parts = _logit_parts(qb[hh], kb[hh], ck_ref[0, hh:hh + 1, :], lo, hi, tri)
                m = jnp.max(parts[-1][0], axis=-1, keepdims=True)
                if lo > 0:
                    m = jnp.maximum(m, jnp.max(parts[0][0], axis=-1, keepdims=True))
                l, o = 0.0, 0.0
                for s, k0, k1 in parts:
                    p = jnp.exp(s - m)
                    l = l + jnp.sum(p, axis=-1, keepdims=True)
                    o = o + _nn(p.astype(BF16), vb[hh][k0:k1])
                outs.append(o / l)
                lse_ref[0, lo:hi, hh:hh + 1] = m + jnp.log(l)
            o_ref[lo:hi, :] = jnp.concatenate(outs, axis=1)

        @pl.when(pl.program_id(0) == _HP - 1)
        def _():
            local, remote = _gather_first_copies(x_refs, out_refs, *sems)
            for cp in remote + local:
                cp.wait()

    fixed = lambda p: (0, 0)
    anyspec = pl.BlockSpec(memory_space=pl.ANY)
    res = pl.pallas_call(
        body, name="attn_fwd", grid=(_HP,),
        in_specs=_qkv_specs() + [pl.BlockSpec((1, 2, T), lambda p: (p, 0, 0)), pl.BlockSpec((1, DH), fixed),
                                 pl.BlockSpec((1, DH), fixed)] + [anyspec] * n,
        out_specs=[pl.BlockSpec((T, LANES), lambda p: (0, p)), pl.BlockSpec((1, T, 2), lambda p: (p, 0, 0))]
        + [anyspec] * n,
        out_shape=[_sds((T, AW)), _sds((_HP, T, 2))] + [_sds((NDEV, *s.shape), s.dtype) for s in shards],
        scratch_shapes=_gather_first_scratch(n),
        compiler_params=_cp(("arbitrary",)),
    )(z, z, z, ck, g_q, g_k, *shards)
    return res[0], res[1], res[2:]


def _attn_bwd(z, ck, g_q, g_k, lse, datt, parts):
    scale = DH ** -0.5
    n = len(parts)

    def body(q_ref, k_ref, v_ref, ck_ref, gq_ref, gk_ref, lse_ref, do_ref, *rest):
        p_refs, rest = rest[:n], rest[n:]
        dq_ref, dk_ref, dv_ref, dck_ref, dgq_ref, dgk_ref = rest[:6]
        fc_refs, (dkn_acc, dv_acc, dc_acc), sems = rest[6:6 + n], rest[6 + n:9 + n], rest[9 + n:]

        @pl.when(pl.program_id(0) == 0)
        def _():
            for cp in _chip_swap_copies(p_refs, fc_refs, *sems):
                cp.start()

        nq, nk, rq, rk, qb, kb, vb, dob = [], [], [], [], [], [], [], []
        for hh in range(2):
            cols = slice(DH * hh, DH * (hh + 1))
            qv, kv = q_ref[:, cols], k_ref[:, cols]
            rq.append(_rms_r(qv))
            rk.append(_rms_r(kv))
            nq.append(qv * rq[hh])
            nk.append(kv * rk[hh])
            qb.append((nq[hh] * gq_ref[...] * scale).astype(BF16))
            kb.append((nk[hh] * gk_ref[...]).astype(BF16))
            vb.append(v_ref[:, cols].astype(BF16))
            dob.append(do_ref[:, cols].astype(BF16))
        dkn_acc[...] = jnp.zeros_like(dkn_acc)
        dv_acc[...] = jnp.zeros_like(dv_acc)
        dc_acc[...] = jnp.zeros_like(dc_acc)
        dgq = jnp.zeros((1, DH), F32)
        tri = _lower_triangle()
        for i in range(T // BQ):
            lo, hi = i * BQ, (i + 1) * BQ
            dqs = []
            for hh in range(2):
                lse = lse_ref[0, lo:hi, hh:hh + 1]
                pd = []
                for s, k0, k1 in _logit_parts(qb[hh], kb[hh], ck_ref[0, hh:hh + 1, :], lo, hi, tri):
                    pd.append((jnp.exp(s - lse), _nt(dob[hh][lo:hi], vb[hh][k0:k1]), k0, k1))
                mean = sum(jnp.sum(p * dp, axis=-1, keepdims=True) for p, dp, _, _ in pd)
                dqn = 0.0
                for p, dp, k0, k1 in pd:
                    ds = p * (dp - mean)
                    dsb = ds.astype(BF16)
                    dv_acc[hh, k0:k1, :] += _tn(p.astype(BF16), dob[hh][lo:hi])
                    dkn_acc[hh, k0:k1, :] += _tn(dsb, qb[hh][lo:hi])
                    dc_acc[hh:hh + 1, k0:k1] -= jnp.sum(ds, axis=0, keepdims=True)
                    dqn = dqn + _nn(dsb, kb[hh][k0:k1])
                dqx, dgp = _rms_bwd(nq[hh][lo:hi], rq[hh][lo:hi], gq_ref[...], dqn * scale)
                dqs.append(dqx)
                dgq = dgq + dgp
            dq_ref[lo:hi, :] = jnp.concatenate(dqs, axis=1).astype(BF16)
        dks, dgk = [], jnp.zeros((1, DH), F32)
        for hh in range(2):
            dkx, dgp = _rms_bwd(nk[hh], rk[hh], gk_ref[...], dkn_acc[hh])
            dks.append(dkx)
            dgk = dgk + dgp
        dk_ref[...] = jnp.concatenate(dks, axis=1).astype(BF16)
        dv_ref[...] = jnp.concatenate([dv_acc[0], dv_acc[1]], axis=1).astype(BF16)
        dck_ref[0] = dc_acc[...]

        @pl.when(pl.program_id(0) == 0)
        def _():
            dgq_ref[...] = jnp.zeros_like(dgq_ref)
            dgk_ref[...] = jnp.zeros_like(dgk_ref)
        dgq_ref[...] += dgq
        dgk_ref[...] += dgk

        @pl.when(pl.program_id(0) == _HP - 1)
        def _():
            for cp in _chip_swap_copies(p_refs, fc_refs, *sems):
                cp.wait()

    fixed = lambda p: (0, 0)
    pair = pl.BlockSpec((T, LANES), lambda p: (0, p))
    gsp = pl.BlockSpec((1, DH), fixed)
    ckspec = pl.BlockSpec((1, 2, T), lambda p: (p, 0, 0))
    anyspec = pl.BlockSpec(memory_space=pl.ANY)
    res = pl.pallas_call(
        body, name="attn_bwd", grid=(_HP,),
        in_specs=_qkv_specs() + [ckspec, gsp, gsp, pl.BlockSpec((1, T, 2), lambda p: (p, 0, 0)), pair] + [anyspec] * n,
        out_specs=[pair, pair, pair, ckspec, gsp, gsp] + [anyspec] * n,
        out_shape=[_sds((T, AW), BF16)] * 3 + [_sds((_HP, 2, T)), _sds((1, DH)), _sds((1, DH))]
        + [_sds((3, *p.shape[1:]), p.dtype) for p in parts],
        scratch_shapes=[pltpu.VMEM((2, T, DH), F32), pltpu.VMEM((2, T, DH), F32), pltpu.VMEM((2, T), F32),
                        pltpu.SemaphoreType.DMA((3 * n,)), pltpu.SemaphoreType.DMA((3 * n,))],
        compiler_params=_cp(("arbitrary",)),
    )(z, z, z, ck, g_q, g_k, lse, datt, *parts)
    return (*res[:6], res[6:])


def _disc(lr, li, ls, br_t, bi_t):
    step = jnp.exp(ls)
    er = jnp.exp(lr * step)
    ab_re = er * jnp.cos(li * step)
    ab_im = er * jnp.sin(li * step)
    num_re = ab_re - 1.0
    num_im = ab_im
    den = lr * lr + li * li
    f_re = (num_re * lr + num_im * li) / den
    f_im = (num_im * lr - num_re * li) / den
    bb_re = f_re[:, None, :] * br_t - f_im[:, None, :] * bi_t
    bb_im = f_re[:, None, :] * bi_t + f_im[:, None, :] * br_t
    return ab_re, ab_im, bb_re, bb_im


def _disc_fwd(lr, li, ls, br_t, bi_t):
    def body(lr_ref, li_ref, ls_ref, br_ref, bi_ref, ar_ref, ai_ref, bbr_ref, bbi_ref):
        ar, ai, bbr, bbi = _disc(lr_ref[...], li_ref[...], ls_ref[...], br_ref[...], bi_ref[...])
        ar_ref[...] = ar
        ai_ref[...] = ai
        bbr_ref[...] = bbr
        bbi_ref[...] = bbi

    return pl.pallas_call(
        body, name="ssm_disc_fwd",
        out_shape=[_sds((NG, NP)), _sds((NG, NP)), _sds((NG, GS, NP)), _sds((NG, GS, NP))],
        compiler_params=_cp())(lr, li, ls, br_t, bi_t)


def _disc_bwd(lr, li, ls, br_t, bi_t, dar, dai, dbbr, dbbi):
    def body(lr_ref, li_ref, ls_ref, br_ref, bi_ref, dar_ref, dai_ref, dbbr_ref, dbbi_ref,
             dlr_ref, dli_ref, dls_ref, dbr_ref, dbi_ref):
        _, vjp = jax.vjp(_disc, lr_ref[...], li_ref[...], ls_ref[...], br_ref[...], bi_ref[...])
        dlr, dli, dls, dbr, dbi = vjp((dar_ref[...], dai_ref[...], dbbr_ref[...], dbbi_ref[...]))
        dlr_ref[...] = dlr
        dli_ref[...] = dli
        dls_ref[...] = dls
        dbr_ref[...] = dbr
        dbi_ref[...] = dbi

    return pl.pallas_call(
        body, name="ssm_disc_bwd",
        out_shape=[_sds((NG, NP)), _sds((NG, NP)), _sds((NG, 1)), _sds((NG, GS, NP)), _sds((NG, GS, NP))],
        compiler_params=_cp())(lr, li, ls, br_t, bi_t, dar, dai, dbbr, dbbi)


def _cmul(ar, ai, br, bi):
    return ar * br - ai * bi, ar * bi + ai * br


def _scan(buf, a_re, a_im, reverse, other=None):
    ar = [jnp.broadcast_to(a_re[:, LANES * k:LANES * (k + 1)], (NSEG, LANES)) for k in range(4)]
    ai = [jnp.broadcast_to(a_im[:, LANES * k:LANES * (k + 1)], (NSEG, LANES)) for k in range(4)]

    def tile(ref, i, k):
        return ref[pl.ds(pl.multiple_of(i * NSEG, NSEG), NSEG), LANES * k:LANES * (k + 1)]

    def advance(i, xr, xi):
        nr = [ar[k] * xr[k] - ai[k] * xi[k] + tile(buf, i, k) for k in range(4)]
        ni = [ar[k] * xi[k] + ai[k] * xr[k] + tile(buf, i, 4 + k) for k in range(4)]
        return nr, ni

    def index(i):
        return (SEG - 1 - i) if reverse else i

    zeros = tuple(jnp.zeros((NSEG, LANES), F32) for _ in range(4))

    def sweep1(i, carry):
        nr, ni = advance(index(i), carry[0], carry[1])
        return tuple(nr), tuple(ni)

    er, ei = lax.fori_loop(0, SEG, sweep1, (zeros, zeros), unroll=4)

    row = lax.broadcasted_iota(jnp.int32, (NSEG, LANES), 0)
    sr, si = [], []
    for k in range(4):
        pr, pi = ar[k], ai[k]
        for _ in range(SEG.bit_length() - 1):
            pr, pi = _cmul(pr, pi, pr, pi)
        ir, ii = er[k], ei[k]
        for d in (1, 2, 4):
            shift = (NSEG - d) if reverse else d
            ok = (row < NSEG - d) if reverse else (row >= d)
            mr, mi = _cmul(pr, pi, pltpu.roll(ir, shift, 0), pltpu.roll(ii, shift, 0))
            ir = ir + jnp.where(ok, mr, 0.0)
            ii = ii + jnp.where(ok, mi, 0.0)
            pr, pi = _cmul(pr, pi, pr, pi)
        shift = (NSEG - 1) if reverse else 1
        ok = (row < NSEG - 1) if reverse else (row >= 1)
        sr.append(jnp.where(ok, pltpu.roll(ir, shift, 0), 0.0))
        si.append(jnp.where(ok, pltpu.roll(ii, shift, 0), 0.0))

    def sweep2(i, carry):
        xr, xi, dar, dai = carry
        idx = index(i)
        if other is not None:
            orr = [tile(other, idx, k) for k in range(4)]
            oi = [tile(other, idx, 4 + k) for k in range(4)]
            dar = tuple(dar[k] + xr[k] * orr[k] + xi[k] * oi[k] for k in range(4))
            dai = tuple(dai[k] + xi[k] * orr[k] - xr[k] * oi[k] for k in range(4))
        nr, ni = advance(idx, xr, xi)
        start = pl.multiple_of(idx * NSEG, NSEG)
        for k in range(4):
            buf[pl.ds(start, NSEG), LANES * k:LANES * (k + 1)] = nr[k]
            buf[pl.ds(start, NSEG), LANES * (4 + k):LANES * (5 + k)] = ni[k]
        return tuple(nr), tuple(ni), dar, dai

    _, _, dar, dai = lax.fori_loop(0, SEG, sweep2, (tuple(sr), tuple(si), zeros, zeros), unroll=4)
    if other is None:
        return None
    da_re = jnp.concatenate([jnp.sum(d, axis=0, keepdims=True) for d in dar], axis=1)
    da_im = jnp.concatenate([jnp.sum(d, axis=0, keepdims=True) for d in dai], axis=1)
    return da_re, da_im


_SSM_BLOCKS = 4


def _ssm_fwd(u, bb, ab, cc, dsk, shards):
    n = len(shards)

    def body(u_ref, bb_ref, a_ref, cc_ref, d_ref, *rest):
        x_refs, y_ref, out_refs, xb, sems = rest[:n], rest[n], rest[n + 1:2 * n + 1], rest[2 * n + 1], rest[2 * n + 2:]

        @pl.when(pl.program_id(0) == 0)
        def _():
            local, remote = _gather_first_copies(x_refs, out_refs, *sems)
            for cp in local + remote:
                cp.start()

        ub = u_ref[...]
        xb[...] = _nn(ub.astype(BF16), bb_ref[0])
        _scan(xb, a_ref[0, 0:1, :], a_ref[0, 1:2, :], reverse=False)
        y_ref[...] = _nn(xb[...].astype(BF16), cc_ref[0]) + d_ref[0] * ub

        @pl.when(pl.program_id(0) == _SSM_BLOCKS - 1)
        def _():
            local, remote = _gather_first_copies(x_refs, out_refs, *sems)
            for cp in remote + local:
                cp.wait()

    blk = lambda j: (j, 0, 0)
    col = pl.BlockSpec((T, LANES), lambda j: (0, j))
    anyspec = pl.BlockSpec(memory_space=pl.ANY)
    res = pl.pallas_call(
        body, name="ssm_fwd", grid=(_SSM_BLOCKS,),
        in_specs=[col, pl.BlockSpec((1, LANES, 1024), blk), pl.BlockSpec((1, 2, 512), blk),
                  pl.BlockSpec((1, 1024, LANES), blk), pl.BlockSpec((1, 1, LANES), blk)] + [anyspec] * n,
        out_specs=[col] + [anyspec] * n,
        out_shape=[_sds((T, SW))] + [_sds((NDEV, *s.shape), s.dtype) for s in shards],
        scratch_shapes=[pltpu.VMEM((T, 1024), F32)] + _gather_first_scratch(n),
        compiler_params=_cp(("arbitrary",)),
    )(u, bb, ab, cc, dsk, *shards)
    return res[0], res[1:]


def _ssm_bwd(u, dy, bb, ab, cc, dsk):
    def body(u_ref, dy_ref, bb_ref, a_ref, cc_ref, d_ref, du_ref, dbb_ref, da_ref, dcc_ref, dd_ref, xb, gb):
        ub, dyv = u_ref[...], dy_ref[...]
        ubb, dyb = ub.astype(BF16), dyv.astype(BF16)
        a_re, a_im = a_ref[0, 0:1, :], a_ref[0, 1:2, :]
        xb[...] = _nn(ubb, bb_ref[0])
        _scan(xb, a_re, a_im, reverse=False)
        dcc_ref[0] = _tn(xb[...].astype(BF16), dyb)
        gb[...] = _nt(dyb, cc_ref[0])
        da_re, da_im = _scan(gb, a_re, -a_im, reverse=True, other=xb)
        da_ref[0] = jnp.concatenate([da_re, da_im], axis=0)
        gc = gb[...].astype(BF16)
        du_ref[...] = _nt(gc, bb_ref[0]) + d_ref[0] * dyv
        dbb_ref[0] = _tn(ubb, gc)
        dd_ref[0] = jnp.sum(dyv * ub, axis=0, keepdims=True)

    blk = lambda j: (j, 0, 0)
    col = pl.BlockSpec((T, LANES), lambda j: (0, j))
    return pl.pallas_call(
        body, name="ssm_bwd", grid=(_SSM_BLOCKS,),
        in_specs=[col, col, pl.BlockSpec((1, LANES, 1024), blk), pl.BlockSpec((1, 2, 512), blk),
                  pl.BlockSpec((1, 1024, LANES), blk), pl.BlockSpec((1, 1, LANES), blk)],
        out_specs=[col, pl.BlockSpec((1, LANES, 1024), blk), pl.BlockSpec((1, 2, 512), blk),
                   pl.BlockSpec((1, 1024, LANES), blk), pl.BlockSpec((1, 1, LANES), blk)],
        out_shape=[_sds((T, SW)), _sds((4, LANES, 1024)), _sds((4, 2, 512)), _sds((4, 1024, LANES)), _sds((4, 1, LANES))],
        scratch_shapes=[pltpu.VMEM((T, 1024), F32), pltpu.VMEM((T, 1024), F32)],
        compiler_params=_cp(("parallel",)),
    )(u, dy, bb, ab, cc, dsk)


def _interleave(a):
    return a.reshape(NSEG, SEG, a.shape[1]).transpose(1, 0, 2).reshape(a.shape)


def _deinterleave(a):
    return a.reshape(SEG, NSEG, a.shape[1]).transpose(1, 0, 2).reshape(a.shape)


_TM_MIX = 256


def _mix_parts(att, y, wg, bg):
    y2 = jax.nn.gelu(y)
    y2b = y2.astype(BF16)
    sg = jax.nn.sigmoid(_nn(y2b, wg) + bg)
    ssm = y2 * sg
    ra, rs = _rms_r(att), _rms_r(ssm)
    return y2, y2b, sg, ssm, ra, rs


def _mix_fwd(att, y, x, wg, bg, ga, gs, wo):
    def body(att_ref, y_ref, x_ref, wg_ref, bg_ref, ga_ref, gs_ref, wo_ref, x1_ref):
        attv = att_ref[...]
        _, _, _, ssm, ra, rs = _mix_parts(attv, y_ref[...], wg_ref[...], bg_ref[...])
        ma = (attv * ra * ga_ref[...]).astype(BF16)
        ms = (ssm * rs * gs_ref[...]).astype(BF16)
        x1_ref[...] = x_ref[...] + _nn(ma, wo_ref[0:AW, :]) + _nn(ms, wo_ref[AW:D, :])

    row = lambda i: (i, 0)
    fixed = lambda i: (0, 0)
    half = pl.BlockSpec((_TM_MIX, AW), row)
    vec = pl.BlockSpec((1, AW), fixed)
    return pl.pallas_call(
        body, name="mix_fwd", grid=(T // _TM_MIX,),
        in_specs=[half, half, pl.BlockSpec((_TM_MIX, D), row), pl.BlockSpec((SW, SW), fixed), vec, vec, vec,
                  pl.BlockSpec((D, D), fixed)],
        out_specs=pl.BlockSpec((_TM_MIX, D), row), out_shape=_sds((T, D)),
        compiler_params=_cp(("parallel",)),
    )(att, y, x, wg, bg, ga, gs, wo)


def _mix_bwd(att, y, dx1, wg, bg, ga, gs, wo):
    last = T // _TM_MIX - 1

    def body(att_ref, y_ref, d_ref, wg_ref, bg_ref, ga_ref, gs_ref, wo_ref,
             datt_ref, dy_ref, dwg_ref, dbg_ref, dga_ref, dgs_ref, dwo_ref, dwg_acc, dwo_acc):
        attv, yv, db = att_ref[...], y_ref[...], d_ref[...].astype(BF16)
        y2, y2b, sg, ssm, ra, rs = _mix_parts(attv, yv, wg_ref[...], bg_ref[...])
        na, ns = attv * ra, ssm * rs
        ma = (na * ga_ref[...]).astype(BF16)
        ms = (ns * gs_ref[...]).astype(BF16)
        dma = _nt(db, wo_ref[0:AW, :])
        dms = _nt(db, wo_ref[AW:D, :])
        datt, dga = _rms_bwd(na, ra, ga_ref[...], dma)
        dssm, dgs = _rms_bwd(ns, rs, gs_ref[...], dms)
        dgl = dssm * y2 * sg * (1.0 - sg)
        dglb = dgl.astype(BF16)
        dy2 = dssm * sg + _nt(dglb, wg_ref[...])
        _, gelu_vjp = jax.vjp(jax.nn.gelu, yv)
        datt_ref[...] = datt
        dy_ref[...] = gelu_vjp(dy2)[0]

        @pl.when(pl.program_id(0) == 0)
        def _():
            dwg_acc[...] = jnp.zeros_like(dwg_acc)
            dwo_acc[...] = jnp.zeros_like(dwo_acc)
            dbg_ref[...] = jnp.zeros_like(dbg_ref)
            dga_ref[...] = jnp.zeros_like(dga_ref)
            dgs_ref[...] = jnp.zeros_like(dgs_ref)
        dwg_acc[...] += _tn(y2b, dglb)
        dbg_ref[...] += jnp.sum(dgl, axis=0, keepdims=True)
        dga_ref[...] += dga
        dgs_ref[...] += dgs
        dwo_acc[0:AW, :] += _tn(ma, db)
        dwo_acc[AW:D, :] += _tn(ms, db)

        @pl.when(pl.program_id(0) == last)
        def _():
            dwg_ref[...] = dwg_acc[...].astype(BF16)
            dwo_ref[...] = dwo_acc[...].astype(BF16)

    row = lambda i: (i, 0)
    fixed = lambda i: (0, 0)
    half = pl.BlockSpec((_TM_MIX, AW), row)
    vec = pl.BlockSpec((1, AW), fixed)
    sq = pl.BlockSpec((SW, SW), fixed)
    big = pl.BlockSpec((D, D), fixed)
    return pl.pallas_call(
        body, name="mix_bwd", grid=(T // _TM_MIX,),
        in_specs=[half, half, pl.BlockSpec((_TM_MIX, D), row), sq, vec, vec, vec, big],
        out_specs=[half, half, sq, vec, vec, vec, big],
        out_shape=[_sds((T, AW)), _sds((T, SW)), _sds((SW, SW), BF16), _sds((1, SW)), _sds((1, AW)), _sds((1, SW)),
                   _sds((D, D), BF16)],
        scratch_shapes=[pltpu.VMEM((SW, SW), F32), pltpu.VMEM((D, D), F32)],
        compiler_params=_cp(("arbitrary",)),
    )(att, y, dx1, wg, bg, ga, gs, wo)


_NCB = -(-CS // LANES)


def _ffn_up(x1, g, wu4):
    tm = 512

    def body(x_ref, g_ref, w_ref, h_ref, up_ref):
        @pl.when(pl.program_id(1) == 0)
        def _():
            xv = x_ref[...]
            h_ref[...] = (xv * _rms_r(xv) * g_ref[...]).astype(BF16)
        up_ref[0, 0] = _nt(h_ref[...], w_ref[0, 0])

    return pl.pallas_call(
        body, name="ffn_up", grid=(T // tm, NDEV),
        in_specs=[pl.BlockSpec((tm, D), lambda i, e: (i, 0)), pl.BlockSpec((1, D), lambda i, e: (0, 0)),
                  pl.BlockSpec((1, 1, CS, D), lambda i, e: (e // 4, e % 4, 0, 0))],
        out_specs=[pl.BlockSpec((tm, D), lambda i, e: (i, 0)),
                   pl.BlockSpec((1, 1, tm, CS), lambda i, e: (e // 4, e % 4, i, 0))],
        out_shape=[_sds((T, D), BF16), _sds((2, 4, T, CS))],
        compiler_params=_cp(("parallel", "arbitrary")),
    )(x1, g, wu4)


_PAD = 8


def _padded_block():
    return pltpu.VMEM((T + 2 * _PAD, LANES), F32)


def _fill(pad, val):
    pad[0:_PAD, :] = jnp.zeros((_PAD, LANES), F32)
    pad[_PAD:_PAD + T, :] = val
    pad[_PAD + T:2 * _PAD + T, :] = jnp.zeros((_PAD, LANES), F32)


def _at(pad, k):
    return pad[_PAD + k:_PAD + k + T, :]


def _conv(pad, w, b):
    return w[0:1, :] * _at(pad, -2) + w[1:2, :] * _at(pad, -1) + w[2:3, :] * _at(pad, 0) + b


def _chan(gv, rows):
    return pl.BlockSpec((1, 1, rows, LANES), lambda d, j: (gv, d, 0, j))


def _ffn_act(up4, cw4, cb4):
    def body(ug_ref, uv_ref, wg_ref, wv_ref, bg_ref, bv_ref, act_ref, pg, pv):
        _fill(pg, ug_ref[0, 0])
        _fill(pv, uv_ref[0, 0])
        g = _conv(pg, wg_ref[0, 0], bg_ref[0, 0])
        v = _conv(pv, wv_ref[0, 0], bv_ref[0, 0])
        act_ref[0] = (g * jax.nn.sigmoid(g) * v).astype(BF16)

    return pl.pallas_call(
        body, name="ffn_act", grid=(4, _NCB),
        in_specs=[_chan(0, T), _chan(1, T), _chan(0, 3), _chan(1, 3), _chan(0, 1), _chan(1, 1)],
        out_specs=pl.BlockSpec((1, T, LANES), lambda d, j: (d, 0, j)), out_shape=_sds((4, T, CS), BF16),
        scratch_shapes=[_padded_block(), _padded_block()],
        compiler_params=_cp(("parallel", "parallel")),
    )(up4, up4, cw4, cw4, cb4, cb4)


def _ffn_act_bwd(up4, dact, cw4, cb4):
    def conv_bwd(ph, w, d, pd):
        _fill(pd, d)
        dup = w[2:3, :] * d + w[1:2, :] * _at(pd, 1) + w[0:1, :] * _at(pd, 2)
        dw = jnp.concatenate([jnp.sum(d * _at(ph, -2), axis=0, keepdims=True),
                              jnp.sum(d * _at(ph, -1), axis=0, keepdims=True),
                              jnp.sum(d * _at(ph, 0), axis=0, keepdims=True)], axis=0)
        return dup, dw, jnp.sum(d, axis=0, keepdims=True)

    def body(ug_ref, uv_ref, da_ref, wg_ref, wv_ref, bg_ref, bv_ref, dup_ref, dw_ref, db_ref, pg, pv, pdg, pdv):
        da = da_ref[0]
        _fill(pg, ug_ref[0, 0])
        _fill(pv, uv_ref[0, 0])
        g = _conv(pg, wg_ref[0, 0], bg_ref[0, 0])
        v = _conv(pv, wv_ref[0, 0], bv_ref[0, 0])
        sg = jax.nn.sigmoid(g)
        dg = da * v * (sg * (1.0 + g * (1.0 - sg)))
        dv = da * (g * sg)
        dug, dwg, dbg = conv_bwd(pg, wg_ref[0, 0], dg, pdg)
        duv, dwv, dbv = conv_bwd(pv, wv_ref[0, 0], dv, pdv)
        dup_ref[0, 0] = dug.astype(BF16)
        dup_ref[1, 0] = duv.astype(BF16)
        dw_ref[0, 0] = dwg
        dw_ref[1, 0] = dwv
        db_ref[0, 0] = dbg
        db_ref[1, 0] = dbv

    both = lambda rows: pl.BlockSpec((2, 1, rows, LANES), lambda d, j: (0, d, 0, j))
    return pl.pallas_call(
        body, name="ffn_act_bwd", grid=(4, _NCB),
        in_specs=[_chan(0, T), _chan(1, T), pl.BlockSpec((1, T, LANES), lambda d, j: (d, 0, j)),
                  _chan(0, 3), _chan(1, 3), _chan(0, 1), _chan(1, 1)],
        out_specs=[both(T), both(3), both(1)],
        out_shape=[_sds((2, 4, T, CS), BF16), _sds((2, 4, 3, CS)), _sds((2, 4, 1, CS))],
        scratch_shapes=[_padded_block()] * 4,
        compiler_params=_cp(("parallel", "parallel")),
    )(up4, up4, dact, cw4, cw4, cb4, cb4)


def _ffn_down_loss(act, wd4, x1, target):
    tm = 256

    def body(a_ref, w_ref, x1_ref, t_ref, loss_ref, dx_ref, dxb_ref):
        x2 = x1_ref[...]
        for d in range(4):
            x2 = x2 + _nn(a_ref[d], w_ref[d])
        err = x2 - t_ref[...]
        dx = err * (1.0 / D)
        dx_ref[...] = dx
        dxb_ref[...] = dx.astype(BF16)

        @pl.when(pl.program_id(0) == 0)
        def _():
            loss_ref[...] = jnp.zeros_like(loss_ref)
        loss_ref[...] += 0.5 * jnp.sum(jnp.mean(err * err, axis=-1, keepdims=True), axis=0, keepdims=True)

    row = lambda i: (i, 0)
    fixed = lambda i: (0, 0)
    return pl.pallas_call(
        body, name="ffn_down_loss", grid=(T // tm,),
        in_specs=[pl.BlockSpec((4, tm, CS), lambda i: (0, i, 0)), pl.BlockSpec((4, CS, D), lambda i: (0, 0, 0)),
                  pl.BlockSpec((tm, D), row), pl.BlockSpec((tm, D), row)],
        out_specs=[pl.BlockSpec((1, 1), fixed), pl.BlockSpec((tm, D), row), pl.BlockSpec((tm, D), row)],
        out_shape=[_sds((1, 1)), _sds((T, D)), _sds((T, D), BF16)],
        compiler_params=_cp(("arbitrary",)),
    )(act, wd4, x1, target)


def _ffn_dact(dx2b, wd4):
    tm = 512

    def body(d_ref, w_ref, o_ref):
        o_ref[0] = _nt(d_ref[...], w_ref[0])

    return pl.pallas_call(
        body, name="ffn_dact", grid=(4, T // tm),
        in_specs=[pl.BlockSpec((tm, D), lambda d, i: (i, 0)), pl.BlockSpec((1, CS, D), lambda d, i: (d, 0, 0))],
        out_specs=pl.BlockSpec((1, tm, CS), lambda d, i: (d, i, 0)), out_shape=_sds((4, T, CS)),
        compiler_params=_cp(("parallel", "parallel")),
    )(dx2b, wd4)


def _ffn_dwd(act, dx2b):
    def body(a_ref, d_ref, o_ref):
        o_ref[0] = _tn(a_ref[0], d_ref[...]).astype(BF16)

    return pl.pallas_call(
        body, name="ffn_dwd", grid=(4,),
        in_specs=[pl.BlockSpec((1, T, CS), lambda d: (d, 0, 0)), pl.BlockSpec((T, D), lambda d: (0, 0))],
        out_specs=pl.BlockSpec((1, CS, D), lambda d: (d, 0, 0)), out_shape=_sds((4, CS, D), BF16),
        compiler_params=_cp(("parallel",)),
    )(act, dx2b)


def _ffn_dwu(h2, dup4):
    tn = 512

    def body(h_ref, d_ref, o_ref):
        o_ref[0, 0] = _tn(d_ref[0, 0], h_ref[...]).astype(BF16)

    return pl.pallas_call(
        body, name="ffn_dwu", grid=(NDEV, D // tn),
        in_specs=[pl.BlockSpec((T, tn), lambda e, i: (0, i)),
                  pl.BlockSpec((1, 1, T, CS), lambda e, i: (e // 4, e % 4, 0, 0))],
        out_specs=pl.BlockSpec((1, 1, CS, tn), lambda e, i: (e // 4, e % 4, 0, i)),
        out_shape=_sds((2, 4, CS, D), BF16),
        compiler_params=_cp(("parallel", "parallel")),
    )(h2, dup4)


def _ffn_up_bwd(dup4, wu4, x1, g, dres):
    tm = 256

    def body(d_ref, w_ref, x_ref, g_ref, dres_ref, dx_ref, dg_ref):
        dh = jnp.zeros((tm, D), F32)
        for gv in range(2):
            for d in range(4):
                dh = dh + _nn(d_ref[gv, d], w_ref[gv, d])
        xv = x_ref[...]
        r = _rms_r(xv)
        dxr, dgp = _rms_bwd(xv * r, r, g_ref[...], dh)
        dx_ref[...] = dres_ref[...] + dxr

        @pl.when(pl.program_id(0) == 0)
        def _():
            dg_ref[...] = jnp.zeros_like(dg_ref)
        dg_ref[...] += dgp

    row = lambda i: (i, 0)
    fixed = lambda i: (0, 0)
    return pl.pallas_call(
        body, name="ffn_up_bwd", grid=(T // tm,),
        in_specs=[pl.BlockSpec((2, 4, tm, CS), lambda i: (0, 0, i, 0)), pl.BlockSpec((2, 4, CS, D), lambda i: (0, 0, 0, 0)),
                  pl.BlockSpec((tm, D), row), pl.BlockSpec((1, D), fixed), pl.BlockSpec((tm, D), row)],
        out_specs=[pl.BlockSpec((tm, D), row), pl.BlockSpec((1, D), fixed)],
        out_shape=[_sds((T, D)), _sds((1, D))],
        compiler_params=_cp(("arbitrary",)),
    )(dup4, wu4, x1, g, dres)


def _adamw_math(w, g, m, v):
    m = ADAM_B1 * m + (1.0 - ADAM_B1) * g
    v = ADAM_B2 * v + (1.0 - ADAM_B2) * jnp.square(g)
    m_hat = m / (1.0 - ADAM_B1 ** ADAM_STEP)
    v_hat = v / (1.0 - ADAM_B2 ** ADAM_STEP)
    delta = -ADAM_LR * (m_hat / (jnp.sqrt(v_hat) + ADAM_EPS) + ADAM_WD * w)
    return delta, m, v


def _where_am_i():
    x, y, c = _place()
    return jnp.stack([c, 2 * x + y]).astype(jnp.int32)


def _pair_sum(a4, recv, name):
    _, _, rows, cols = a4.shape
    tr = _row_tile(rows, cols, 3 << 20)

    def body(sel_ref, own_ref, recv_ref, out_ref):
        out_ref[...] = (own_ref[0].astype(F32) + recv_ref[...].astype(F32)).astype(out_ref.dtype)

    grid_spec = pltpu.PrefetchScalarGridSpec(
        num_scalar_prefetch=1, grid=(4, rows // tr),
        in_specs=[pl.BlockSpec((1, 1, tr, cols), lambda k, i, s: (k, s[0], i, 0)),
                  pl.BlockSpec((1, tr, cols), lambda k, i, s: (k, i, 0))],
        out_specs=pl.BlockSpec((1, tr, cols), lambda k, i, s: (k, i, 0)),
    )
    return pl.pallas_call(
        body, name=name, grid_spec=grid_spec, out_shape=_sds((4, rows, cols), a4.dtype),
        compiler_params=_cp(("arbitrary", "arbitrary")),
    )(_where_am_i(), a4, recv)


def _adamw_rs(a4, recv, from_chips, w, m, v, name):
    rows, cols = w.shape
    tr = _row_tile(rows, cols, 3 << 19)

    def body(sel_ref, own_ref, recv_ref, fc_ref, w_ref, m_ref, v_ref, g_ref, d_ref, nm_ref, nv_ref):
        g = own_ref[0, 0].astype(F32) + recv_ref[0].astype(F32)
        for j in range(3):
            g = g + fc_ref[j].astype(F32)
        d, nm, nv = _adamw_math(w_ref[...], g, m_ref[...], v_ref[...])
        g_ref[...] = g
        d_ref[...] = d
        nm_ref[...] = nm
        nv_ref[...] = nv

    flat = pl.BlockSpec((tr, cols), lambda i, s: (i, 0))
    grid_spec = pltpu.PrefetchScalarGridSpec(
        num_scalar_prefetch=1, grid=(rows // tr,),
        in_specs=[pl.BlockSpec((1, 1, tr, cols), lambda i, s: (s[1], s[0], i, 0)),
                  pl.BlockSpec((1, tr, cols), lambda i, s: (s[1], i, 0)),
                  pl.BlockSpec((3, tr, cols), lambda i, s: (0, i, 0)), flat, flat, flat],
        out_specs=[flat] * 4,
    )
    return pl.pallas_call(
        body, name=name, grid_spec=grid_spec, out_shape=[_sds((rows, cols))] * 4,
        compiler_params=_cp(("arbitrary",)),
    )(_where_am_i(), a4, recv, from_chips, w, m, v)


def _adamw_small(gall, w, m, v):
    rows = w.shape[0]

    def body(ga_ref, w_ref, m_ref, v_ref, g_ref, d_ref, nm_ref, nv_ref):
        g = ga_ref[0]
        for dev in range(1, NDEV):
            g = g + ga_ref[dev]
        d, nm, nv = _adamw_math(w_ref[...], g[0:rows], m_ref[...], v_ref[...])
        g_ref[...] = g
        d_ref[...] = d
        nm_ref[...] = nm
        nv_ref[...] = nv

    return pl.pallas_call(body, name="adamw_small", out_shape=[_sds((rows + 8, LANES))] + [_sds((rows, LANES))] * 3,
                          compiler_params=_cp())(gall, w, m, v)


def _rows128(a):
    n = a.shape[0]
    r = (-n) % (8 * LANES)
    if r:
        a = jnp.pad(a, [(0, r)])
    return a.reshape(-1, LANES)


_SMALL = (("g_mix", D), ("b_f", NH), ("g_q", DH), ("g_k", DH), ("lambda_re", NG * NP), ("lambda_im", NG * NP),
          ("log_step", NG), ("b_re", NG * NP * GS), ("b_im", NG * NP * GS), ("c_re", NG * GS * NP),
          ("c_im", NG * GS * NP), ("d_skip", NG * GS), ("b_glu", SW), ("g_attn_out", AW), ("g_ssm_out", SW),
          ("g_ffn", D), ("conv_b", 2 * DFF))


def _pack_small(d):
    return jnp.concatenate([_rows128(d[name].reshape(-1).astype(F32)) for name, _ in _SMALL], axis=0)


def _unpack_small(p, shapes):
    out, off = {}, 0
    for name, n in _SMALL:
        rows = -(-n // (8 * LANES)) * 8
        out[name] = p[off:off + rows].reshape(-1)[:n].reshape(shapes[name])
        off += rows
    return out


def _block_diag(m):
    eye = jnp.eye(8, dtype=m.dtype)
    r, c = m.shape[2], m.shape[3]
    return (m[:, :, :, None, :] * eye[None, :, None, :, None]).reshape(4, 8 * r, 8 * c)


def _diag_blocks(dense, r, c):
    eye = jnp.eye(8, dtype=dense.dtype)
    return jnp.sum(dense.reshape(4, 8, r, 8, c) * eye[None, :, None, :, None], axis=3)


def kernel(x, g_mix, w_in, b_f, g_q, g_k, lambda_re, lambda_im, log_step, b_re, b_im, c_re, c_im, d_skip, w_glu, b_glu, g_attn_out, g_ssm_out, w_out, g_ffn, w_up, conv_w, conv_b, w_down, loss_target, m_g_mix, m_w_in, m_b_f, m_g_q, m_g_k, m_lambda_re, m_lambda_im, m_log_step, m_b_re, m_b_im, m_c_re, m_c_im, m_d_skip, m_w_glu, m_b_glu, m_g_attn_out, m_g_ssm_out, m_w_out, m_g_ffn, m_w_up, m_conv_w, m_conv_b, m_w_down, v_g_mix, v_w_in, v_b_f, v_g_q, v_g_k, v_lambda_re, v_lambda_im, v_log_step, v_b_re, v_b_im, v_c_re, v_c_im, v_d_skip, v_w_glu, v_b_glu, v_g_attn_out, v_g_ssm_out, v_w_out, v_g_ffn, v_w_up, v_conv_w, v_conv_b, v_w_down):
    weights = dict(g_mix=g_mix, w_in=w_in, b_f=b_f, g_q=g_q, g_k=g_k, lambda_re=lambda_re, lambda_im=lambda_im,
                   log_step=log_step, b_re=b_re, b_im=b_im, c_re=c_re, c_im=c_im, d_skip=d_skip, w_glu=w_glu,
                   b_glu=b_glu, g_attn_out=g_attn_out, g_ssm_out=g_ssm_out, w_out=w_out, g_ffn=g_ffn, w_up=w_up,
                   conv_w=conv_w, conv_b=conv_b, w_down=w_down)
    mom_m = dict(g_mix=m_g_mix, w_in=m_w_in, b_f=m_b_f, g_q=m_g_q, g_k=m_g_k, lambda_re=m_lambda_re,
                 lambda_im=m_lambda_im, log_step=m_log_step, b_re=m_b_re, b_im=m_b_im, c_re=m_c_re, c_im=m_c_im,
                 d_skip=m_d_skip, w_glu=m_w_glu, b_glu=m_b_glu, g_attn_out=m_g_attn_out, g_ssm_out=m_g_ssm_out,
                 w_out=m_w_out, g_ffn=m_g_ffn, w_up=m_w_up, conv_w=m_conv_w, conv_b=m_conv_b, w_down=m_w_down)
    mom_v = dict(g_mix=v_g_mix, w_in=v_w_in, b_f=v_b_f, g_q=v_g_q, g_k=v_g_k, lambda_re=v_lambda_re,
                 lambda_im=v_lambda_im, log_step=v_log_step, b_re=v_b_re, b_im=v_b_im, c_re=v_c_re, c_im=v_c_im,
                 d_skip=v_d_skip, w_glu=v_w_glu, b_glu=v_b_glu, g_attn_out=v_g_attn_out, g_ssm_out=v_g_ssm_out,
                 w_out=v_w_out, g_ffn=v_g_ffn, w_up=v_w_up, conv_w=v_conv_w, conv_b=v_conv_b, w_down=v_w_down)
    names = list(weights)
    big = ("w_in", "w_glu", "w_out", "w_up", "w_down", "conv_w")

    xs = x[0]
    target = loss_target[0]
    row = lambda a: a.reshape(1, -1)

    (win_g,) = _all_gather([w_in.astype(BF16)], "gather_w_in")
    w_in_f = win_g.transpose(1, 0, 2).reshape(D, 8 * 257)
    wcat = jnp.concatenate([w_in_f[:, 0:1536], w_in_f[:, 1544:2056], w_in_f[:, 1536:1544],
                            jnp.zeros((D, LANES - NH), BF16)], axis=1)
    cb4 = conv_b.reshape(2, 4, 1, CS)

    h1, z, f_t = _in_proj(xs, row(g_mix), wcat)
    bf_col = b_f.reshape(NH, 1)
    ck = _gates_fwd(f_t, bf_col).reshape(_HP, 2, T)
    att, lse, (wg_g, wo_g, wu_g, cw_g) = _attn_fwd(
        z, ck, row(g_q), row(g_k), [w_glu.astype(BF16), w_out.astype(BF16), w_up.T.astype(BF16), conv_w])

    ls_col = log_step.reshape(NG, 1)
    br_t, bi_t = b_re.transpose(0, 2, 1), b_im.transpose(0, 2, 1)
    ab_re, ab_im, bb_re, bb_im = _disc_fwd(lambda_re, lambda_im, ls_col, br_t, bi_t)
    bb = jnp.concatenate([_block_diag(bb_re.reshape(4, 8, GS, NP)), _block_diag(bb_im.reshape(4, 8, GS, NP))],
                         axis=2).astype(BF16)
    ab = jnp.stack([ab_re.reshape(4, 512), ab_im.reshape(4, 512)], axis=1)
    cdiag = lambda cm: _block_diag(cm.reshape(4, 8, GS, NP).transpose(0, 1, 3, 2))
    cc = jnp.concatenate([cdiag(c_re), -cdiag(c_im)], axis=1).astype(BF16)
    dsk = d_skip.reshape(4, 1, LANES)
    u_il = _interleave(z[:, 1536:2048])
    y_il, (wd_g,) = _ssm_fwd(u_il, bb, ab, cc, dsk, [w_down.astype(BF16)])
    y = _deinterleave(y_il)
    wg_g, wo_g, wu_g, cw_g, wd_g = _gather_second([wg_g, wo_g, wu_g, cw_g, wd_g], "gather_pass_on")
    wg_f = wg_g.reshape(SW, SW)
    wo_f = wo_g.reshape(D, D)
    wu4 = wu_g.reshape(2, 4, CS, D)
    wd4 = wd_g.reshape(4, CS, D)
    cw4 = cw_g.reshape(2, 4, 3, CS)

    x1 = _mix_fwd(att, y, xs, wg_f, row(b_glu), row(g_attn_out), row(g_ssm_out), wo_f)
    h2, up4 = _ffn_up(x1, row(g_ffn), wu4)
    act = _ffn_act(up4, cw4, cb4)
    loss_dev, dx2, dx2b = _ffn_down_loss(act, wd4, x1, target)

    dact = _ffn_dact(dx2b, wd4)
    d_wd = _ffn_dwd(act, dx2b)
    dup4, dcw4, dcb4 = _ffn_act_bwd(up4, dact, cw4, cb4)
    d_wu = _ffn_dwu(h2, dup4)
    mlp = ("w_up", "w_down", "conv_w")
    by_dest = {"w_up": d_wu.reshape(4, 2, CS, D), "w_down": d_wd.reshape(4, 2, DFF // NDEV, D),
               "conv_w": dcw4.reshape(4, 2, 3, CS)}
    from_sibling = dict(zip(mlp, _swap_sibling([by_dest[n] for n in mlp], "rs_pair_swap_mlp")))
    chip_sums = {n: _pair_sum(by_dest[n], from_sibling[n], "rs_pair_sum_" + n) for n in mlp}
    dx1, dg_ffn = _ffn_up_bwd(dup4, wu4, x1, row(g_ffn), dx2)
    datt, dy, d_wg, d_bg, d_ga, d_gs, d_wo = _mix_bwd(att, y, dx1, wg_f, row(b_glu), row(g_attn_out),
                                                       row(g_ssm_out), wo_f)
    du_il, dbb, dab, dcc, ddsk = _ssm_bwd(u_il, _interleave(dy), bb, ab, cc, dsk)
    du = _deinterleave(du_il).astype(BF16)
    dbb_re = _diag_blocks(dbb[:, :, 0:512], GS, NP).reshape(NG, GS, NP)
    dbb_im = _diag_blocks(dbb[:, :, 512:1024], GS, NP).reshape(NG, GS, NP)
    dlr, dli, dls, dbr_t, dbi_t = _disc_bwd(lambda_re, lambda_im, ls_col, br_t, bi_t,
                                            dab[:, 0].reshape(NG, NP), dab[:, 1].reshape(NG, NP), dbb_re, dbb_im)
    d_cre = _diag_blocks(dcc[:, 0:512], NP, GS).transpose(0, 1, 3, 2).reshape(NG, GS, NP)
    d_cim = -_diag_blocks(dcc[:, 512:1024], NP, GS).transpose(0, 1, 3, 2).reshape(NG, GS, NP)

    dq, dk, dv, dck, d_gq, d_gk, fc_mlp = _attn_bwd(z, ck, row(g_q), row(g_k), lse, datt, [chip_sums[n] for n in mlp])
    from_chips = dict(zip(mlp, fc_mlp))
    df_t, d_bf = _gates_bwd(dck.reshape(NH, T), f_t, bf_col)
    df = jnp.concatenate([df_t.T, jnp.zeros((T, LANES - NH), F32)], axis=1).astype(BF16)
    d_wcat = _in_proj_dw(h1, dq, dk, dv, du, df)
    grad_x, dg_mix = _in_proj_bwd(dq, dk, dv, du, df, wcat, xs, row(g_mix), dx1)
    d_win = jnp.concatenate([d_wcat[:, 0:1536], d_wcat[:, 2048:2048 + NH], d_wcat[:, 1536:2048]], axis=1)
    d_win = d_win.astype(BF16).reshape(D, NDEV, 257).transpose(1, 0, 2)

    mixer = ("w_out", "w_glu", "w_in")
    by_dest.update({"w_out": d_wo.reshape(4, 2, D // NDEV, D), "w_glu": d_wg.reshape(4, 2, SW // NDEV, SW),
                    "w_in": d_win.reshape(4, 2, D, 257)})
    from_sibling.update(zip(mixer, _swap_sibling([by_dest[n] for n in mixer], "rs_pair_swap_mixer")))
    chip_sums.update({n: _pair_sum(by_dest[n], from_sibling[n], "rs_pair_sum_" + n) for n in mixer})
    from_chips.update(zip(mixer, _swap_chips([chip_sums[n] for n in mixer], "rs_chip_swap_mixer")))

    grads, deltas, new_m, new_v = {}, {}, {}, {}
    for n in mlp + mixer:
        flip = (lambda a: a.T) if n == "w_up" else (lambda a: a)
        outs = _adamw_rs(by_dest[n], from_sibling[n], from_chips[n], flip(weights[n]), flip(mom_m[n]), flip(mom_v[n]),
                         "adamw_" + n)
        grads[n], deltas[n], new_m[n], new_v[n] = (flip(o) for o in outs)

    small = dict(g_mix=dg_mix, b_f=d_bf, g_q=d_gq, g_k=d_gk, lambda_re=dlr, lambda_im=dli, log_step=dls,
                 b_re=dbr_t.transpose(0, 2, 1), b_im=dbi_t.transpose(0, 2, 1), c_re=d_cre, c_im=d_cim, d_skip=ddsk,
                 b_glu=d_bg, g_attn_out=d_ga, g_ssm_out=d_gs, g_ffn=dg_ffn, conv_b=dcb4)
    loss_tile = jnp.pad(loss_dev, ((0, 7), (0, LANES - 1)))
    (small_all,) = _all_gather([jnp.concatenate([_pack_small(small), loss_tile], axis=0)], "gather_small_grads")
    sg_p, sd_p, sm_p, sv_p = _adamw_small(small_all, _pack_small({n: weights[n] for n in small}),
                                          _pack_small({n: mom_m[n] for n in small}),
                                          _pack_small({n: mom_v[n] for n in small}))
    loss = sg_p[sg_p.shape[0] - 8, 0]
    shapes = {n: weights[n].shape for n in small}
    sg, sd, sm, sv = (_unpack_small(p, shapes) for p in (sg_p, sd_p, sm_p, sv_p))
    for n in small:
        grads[n], deltas[n], new_m[n], new_v[n] = sg[n], sd[n], sm[n], sv[n]

    return (loss, grad_x[None], *[grads[n] for n in names], *[deltas[n] for n in names],
            *[new_m[n] for n in names], *[new_v[n] for n in names])
```

```python
import jax
import jax.numpy as jnp
from jax import lax
from jax.experimental import pallas as pl
from jax.experimental.pallas import tpu as pltpu

F32, BF16 = jnp.float32, jnp.bfloat16
T, D = 2048, 1024
NH, DH = 8, 64
AW, SW = 512, 512
NG, GS, NP = 32, 16, 64
DFF = 2816
NDEV = 8
CS = 2 * DFF // NDEV
EPS = 1e-6
NEG = -1e30
ZC = 4 * 512 + 128
SEG = 256
NSEG = T // SEG
BQ = 256
VMEM_LIMIT = 56 * 1024 * 1024
LANES = 128
MESH = pl.DeviceIdType.MESH
HI = lax.Precision.HIGHEST

ADAM_LR, ADAM_B1, ADAM_B2, ADAM_EPS, ADAM_WD, ADAM_STEP = 0.001, 0.9, 0.999, 1e-08, 0.01, 10


def _cp(dims=None):
    if dims is None:
        return pltpu.CompilerParams(vmem_limit_bytes=VMEM_LIMIT)
    return pltpu.CompilerParams(dimension_semantics=dims, vmem_limit_bytes=VMEM_LIMIT)


def _sds(shape, dtype=F32):
    return jax.ShapeDtypeStruct(shape, dtype)


def _nt(a, b):
    return lax.dot_general(a, b, (((1,), (1,)), ((), ())), preferred_element_type=F32)


def _tn(a, b):
    return lax.dot_general(a, b, (((0,), (0,)), ((), ())), preferred_element_type=F32)


def _nn(a, b):
    return jnp.dot(a, b, preferred_element_type=F32)


def _rms_r(x):
    return lax.rsqrt(jnp.mean(x * x, axis=-1, keepdims=True) + EPS)


def _rms_bwd(n, r, g, dh):
    dn = dh * g
    dx = r * (dn - n * jnp.mean(dn * n, axis=-1, keepdims=True))
    return dx, jnp.sum(dh * n, axis=0, keepdims=True)


def _row_tile(rows, cols, limit):
    tr = rows
    while tr * cols * 4 > limit and tr % 32 == 0:
        tr //= 2
    return tr


def _place():
    return lax.axis_index("x"), lax.axis_index("y"), lax.axis_index("c")


def _all_gather(shards, name):
    n = len(shards)

    def body(*refs):
        x_refs, out_refs = refs[:n], refs[n:2 * n]
        send_sems, recv_sems, local_sems = refs[2 * n:]
        x, y, c = _place()
        me, sibling = (x, y, c), (x, y, 1 - c)
        chips = [(1 - x, y), (x, 1 - y), (1 - x, 1 - y)]

        def copy(a, k, block, to, src=None):
            px, py, pc = block
            slot = out_refs[a].at[4 * px + 2 * py + pc]
            return pltpu.make_async_remote_copy(
                src_ref=slot if src is None else src, dst_ref=slot,
                send_sem=send_sems.at[7 * a + k], recv_sem=recv_sems.at[7 * a + k], device_id=to, device_id_type=MESH)

        mine, first, passed = [], [], []
        for a in range(n):
            cp = pltpu.make_async_copy(x_refs[a], out_refs[a].at[4 * x + 2 * y + c], local_sems.at[a])
            cp.start()
            mine.append(cp)
            cps = [copy(a, 0, me, sibling, src=x_refs[a])]
            cps += [copy(a, 1 + j, me, (*chip, c), src=x_refs[a]) for j, chip in enumerate(chips)]
            for cp in cps:
                cp.start()
            first += cps
        for a in range(n):
            for j, chip in enumerate(chips):
                copy(a, 1 + j, (*chip, c), me).wait_recv()
                fwd = copy(a, 4 + j, (*chip, c), sibling)
                fwd.start()
                passed.append(fwd)
        for a in range(n):
            copy(a, 0, sibling, me).wait_recv()
            for j, chip in enumerate(chips):
                copy(a, 4 + j, (*chip, 1 - c), me).wait_recv()
        for cp in first + passed:
            cp.wait_send()
        for cp in mine:
            cp.wait()

    anyspec = pl.BlockSpec(memory_space=pl.ANY)
    return pl.pallas_call(
        body, name=name,
        out_shape=[_sds((NDEV, *s.shape), s.dtype) for s in shards],
        in_specs=[anyspec] * n, out_specs=[anyspec] * n,
        scratch_shapes=[pltpu.SemaphoreType.DMA((7 * n,)), pltpu.SemaphoreType.DMA((7 * n,)),
                        pltpu.SemaphoreType.DMA((n,))],
    )(*shards)


def _gather_first_copies(x_refs, out_refs, send_sems, recv_sems, local_sems):
    x, y, c = _place()
    slot = 4 * x + 2 * y + c
    peers = [(x, y, 1 - c), (1 - x, y, c), (x, 1 - y, c), (1 - x, 1 - y, c)]
    local, remote = [], []
    for a, (x_ref, out_ref) in enumerate(zip(x_refs, out_refs)):
        local.append(pltpu.make_async_copy(x_ref, out_ref.at[slot], local_sems.at[a]))
        for k, peer in enumerate(peers):
            remote.append(pltpu.make_async_remote_copy(
                src_ref=x_ref, dst_ref=out_ref.at[slot], send_sem=send_sems.at[4 * a + k],
                recv_sem=recv_sems.at[4 * a + k], device_id=peer, device_id_type=MESH))
    return local, remote


def _gather_first_scratch(n):
    return [pltpu.SemaphoreType.DMA((4 * n,)), pltpu.SemaphoreType.DMA((4 * n,)), pltpu.SemaphoreType.DMA((n,))]


def _gather_second(bufs, name):
    n = len(bufs)

    def body(*refs):
        out_refs = refs[n:2 * n]
        send_sems, recv_sems = refs[2 * n:]
        x, y, c = _place()
        chips = [(1 - x, y), (x, 1 - y), (1 - x, 1 - y)]
        cps = []
        for a in range(n):
            for j, (px, py) in enumerate(chips):
                block = out_refs[a].at[4 * px + 2 * py + c]
                cps.append(pltpu.make_async_remote_copy(
                    src_ref=block, dst_ref=block, send_sem=send_sems.at[3 * a + j], recv_sem=recv_sems.at[3 * a + j],
                    device_id=(x, y, 1 - c), device_id_type=MESH))
        for cp in cps:
            cp.start()
        for cp in cps:
            cp.wait()

    anyspec = pl.BlockSpec(memory_space=pl.ANY)
    return pl.pallas_call(
        body, name=name,
        out_shape=[_sds(b.shape, b.dtype) for b in bufs],
        in_specs=[anyspec] * n, out_specs=[anyspec] * n,
        input_output_aliases={a: a for a in range(n)},
        scratch_shapes=[pltpu.SemaphoreType.DMA((3 * n,)), pltpu.SemaphoreType.DMA((3 * n,))],
    )(*bufs)


def _chip_swap_copies(p_refs, out_refs, send_sems, recv_sems):
    x, y, c = _place()
    chips = [(1 - x, y), (x, 1 - y), (1 - x, 1 - y)]
    cps = []
    for a, (p_ref, out_ref) in enumerate(zip(p_refs, out_refs)):
        for j, (px, py) in enumerate(chips):
            cps.append(pltpu.make_async_remote_copy(
                src_ref=p_ref.at[2 * px + py], dst_ref=out_ref.at[j],
                send_sem=send_sems.at[3 * a + j], recv_sem=recv_sems.at[3 * a + j],
                device_id=(px, py, c), device_id_type=MESH))
    return cps


def _swap_sibling(arrs, name):
    n = len(arrs)

    def body(*refs):
        a_refs, out_refs = refs[:n], refs[n:2 * n]
        send_sems, recv_sems = refs[2 * n:]
        x, y, c = _place()
        cps = []
        for a in range(n):
            for k in range(4):
                cps.append(pltpu.make_async_remote_copy(
                    src_ref=a_refs[a].at[k, 1 - c], dst_ref=out_refs[a].at[k],
                    send_sem=send_sems.at[4 * a + k], recv_sem=recv_sems.at[4 * a + k],
                    device_id=(x, y, 1 - c), device_id_type=MESH))
        for cp in cps:
            cp.start()
        for cp in cps:
            cp.wait()

    anyspec = pl.BlockSpec(memory_space=pl.ANY)
    return pl.pallas_call(
        body, name=name,
        out_shape=[_sds((4, *a.shape[2:]), a.dtype) for a in arrs],
        in_specs=[anyspec] * n, out_specs=[anyspec] * n,
        scratch_shapes=[pltpu.SemaphoreType.DMA((4 * n,)), pltpu.SemaphoreType.DMA((4 * n,))],
    )(*arrs)


def _swap_chips(parts, name):
    n = len(parts)

    def body(*refs):
        cps = _chip_swap_copies(refs[:n], refs[n:2 * n], *refs[2 * n:])
        for cp in cps:
            cp.start()
        for cp in cps:
            cp.wait()

    anyspec = pl.BlockSpec(memory_space=pl.ANY)
    return pl.pallas_call(
        body, name=name,
        out_shape=[_sds((3, *p.shape[1:]), p.dtype) for p in parts],
        in_specs=[anyspec] * n, out_specs=[anyspec] * n,
        scratch_shapes=[pltpu.SemaphoreType.DMA((3 * n,)), pltpu.SemaphoreType.DMA((3 * n,))],
    )(*parts)


def _in_proj(x, g, wcat):
    tm = 256

    def body(x_ref, g_ref, w_ref, h_ref, z_ref, ft_ref):
        xv = x_ref[...]
        h = (xv * _rms_r(xv) * g_ref[...]).astype(BF16)
        h_ref[...] = h
        z = _nt(h, w_ref[...])
        z_ref[...] = z
        ft_ref[...] = z[:, 2048:ZC].T[0:NH, :]

    row = lambda i: (i, 0)
    fixed = lambda i: (0, 0)
    return pl.pallas_call(
        body, name="in_proj", grid=(T // tm,),
        in_specs=[pl.BlockSpec((tm, D), row), pl.BlockSpec((1, D), fixed), pl.BlockSpec((ZC, D), fixed)],
        out_specs=[pl.BlockSpec((tm, D), row), pl.BlockSpec((tm, ZC), row), pl.BlockSpec((NH, tm), lambda i: (0, i))],
        out_shape=[_sds((T, D), BF16), _sds((T, ZC)), _sds((NH, T))],
        compiler_params=_cp(("parallel",)),
    )(x, g, wcat)


def _in_proj_dw(h, dq, dk, dv, du, df):
    tm = 512

    def body(h_ref, dq_ref, dk_ref, dv_ref, du_ref, df_ref, o_ref):
        hv = h_ref[...]
        for j, d_ref in enumerate((dq_ref, dk_ref, dv_ref, du_ref)):
            o_ref[512 * j:512 * (j + 1), :] = _tn(d_ref[...], hv)
        o_ref[2048:ZC, :] = _tn(df_ref[...], hv)

    full = lambda w: pl.BlockSpec((T, w), lambda i: (0, 0))
    return pl.pallas_call(
        body, name="in_proj_dw", grid=(D // tm,),
        in_specs=[pl.BlockSpec((T, tm), lambda i: (0, i)), full(512), full(512), full(512), full(512), full(LANES)],
        out_specs=pl.BlockSpec((ZC, tm), lambda i: (0, i)), out_shape=_sds((ZC, D)),
        compiler_params=_cp(("parallel",)),
    )(h, dq, dk, dv, du, df)


def _in_proj_bwd(dq, dk, dv, du, df, wcat, x, g, dres):
    tm = 256

    def body(dq_ref, dk_ref, dv_ref, du_ref, df_ref, w_ref, x_ref, g_ref, dres_ref, dx_ref, dg_ref):
        dh = _nn(df_ref[...], w_ref[2048:ZC, :])
        for j, d_ref in enumerate((dq_ref, dk_ref, dv_ref, du_ref)):
            dh = dh + _nn(d_ref[...], w_ref[512 * j:512 * (j + 1), :])
        xv = x_ref[...]
        r = _rms_r(xv)
        dxr, dgp = _rms_bwd(xv * r, r, g_ref[...], dh)
        dx_ref[...] = dres_ref[...] + dxr

        @pl.when(pl.program_id(0) == 0)
        def _():
            dg_ref[...] = jnp.zeros_like(dg_ref)
        dg_ref[...] += dgp

    row = lambda i: (i, 0)
    fixed = lambda i: (0, 0)
    part = pl.BlockSpec((tm, 512), row)
    return pl.pallas_call(
        body, name="in_proj_bwd", grid=(T // tm,),
        in_specs=[part, part, part, part, pl.BlockSpec((tm, LANES), row), pl.BlockSpec((ZC, D), fixed),
                  pl.BlockSpec((tm, D), row), pl.BlockSpec((1, D), fixed), pl.BlockSpec((tm, D), row)],
        out_specs=[pl.BlockSpec((tm, D), row), pl.BlockSpec((1, D), fixed)],
        out_shape=[_sds((T, D)), _sds((1, D))],
        compiler_params=_cp(("arbitrary",)),
    )(dq, dk, dv, du, df, wcat, x, g, dres)


def _stack_lanes(v):
    return jnp.concatenate([v[:, LANES * b:LANES * (b + 1)] for b in range(T // LANES)], axis=0)


def _unstack_lanes(s):
    return jnp.concatenate([s[8 * b:8 * b + 8, :] for b in range(T // LANES)], axis=1)


def _cumsum_lanes(v, reverse):
    st = _stack_lanes(v)
    ri = lax.broadcasted_iota(jnp.int32, (LANES, LANES), 0)
    ci = lax.broadcasted_iota(jnp.int32, (LANES, LANES), 1)
    tri = (ri >= ci) if reverse else (ri <= ci)
    intra = jnp.dot(st, tri.astype(F32), precision=HI, preferred_element_type=F32)
    tot = intra[:, 0:1] if reverse else intra[:, LANES - 1:LANES]
    same_head = (ci % 8) == (ri % 8)
    other = (ci // 8 > ri // 8) if reverse else (ci // 8 < ri // 8)
    off = jnp.dot((same_head & other).astype(F32), jnp.broadcast_to(tot, (LANES, LANES)), precision=HI,
                  preferred_element_type=F32)
    return _unstack_lanes(intra + off)


def _gates_fwd(f_t, b_f):
    def body(f_ref, b_ref, c_ref):
        z = f_ref[...] + b_ref[...]
        lf = jnp.minimum(z, 0.0) - jnp.log(1.0 + jnp.exp(-jnp.abs(z)))
        c_ref[...] = _cumsum_lanes(lf, reverse=False)

    return pl.pallas_call(body, name="gates_fwd", out_shape=_sds((NH, T)), compiler_params=_cp())(f_t, b_f)


def _gates_bwd(dck, f_t, b_f):
    def body(d_ref, f_ref, b_ref, df_ref, db_ref):
        z = f_ref[...] + b_ref[...]
        dlf = _cumsum_lanes(d_ref[...], reverse=True)
        df = dlf * jax.nn.sigmoid(-z)
        df_ref[...] = df
        db_ref[...] = jnp.sum(df, axis=1, keepdims=True)

    return pl.pallas_call(body, name="gates_bwd", out_shape=[_sds((NH, T)), _sds((NH, 1))],
                          compiler_params=_cp())(dck, f_t, b_f)


def _lower_triangle():
    rows = lax.broadcasted_iota(jnp.int32, (BQ, BQ), 0)
    cols = lax.broadcasted_iota(jnp.int32, (BQ, BQ), 1)
    return cols <= rows


def _logit_parts(qb, kb, ck_row, lo, hi, tri):
    parts = []
    if lo > 0:
        parts.append((_nt(qb[lo:hi], kb[0:lo]) - ck_row[:, 0:lo], 0, lo))
    parts.append((jnp.where(tri, _nt(qb[lo:hi], kb[lo:hi]) - ck_row[:, lo:hi], NEG), lo, hi))
    return parts


_HP = NH // 2


def _qkv_specs():
    return [pl.BlockSpec((T, LANES), lambda p: (0, p)), pl.BlockSpec((T, LANES), lambda p: (0, _HP + p)),
            pl.BlockSpec((T, LANES), lambda p: (0, 2 * _HP + p))]


def _attn_fwd(z, ck, g_q, g_k, shards):
    scale = DH ** -0.5
    n = len(shards)

    def body(q_ref, k_ref, v_ref, ck_ref, gq_ref, gk_ref, *rest):
        x_refs, (o_ref, lse_ref), out_refs, sems = rest[:n], rest[n:n + 2], rest[n + 2:2 * n + 2], rest[2 * n + 2:]

        @pl.when(pl.program_id(0) == 0)
        def _():
            local, remote = _gather_first_copies(x_refs, out_refs, *sems)
            for cp in local + remote:
                cp.start()

        qb, kb, vb = [], [], []
        for hh in range(2):
            cols = slice(DH * hh, DH * (hh + 1))
            qv, kv = q_ref[:, cols], k_ref[:, cols]
            qb.append((qv * _rms_r(qv) * gq_ref[...] * scale).astype(BF16))
            kb.append((kv * _rms_r(kv) * gk_ref[...]).astype(BF16))
            vb.append(v_ref[:, cols].astype(BF16))
        tri = _lower_triangle()
        for i in range(T // BQ):
            lo, hi = i * BQ, (i + 1) * BQ
            outs = []
            for hh in range(2):
                parts = _logit_parts(qb[hh], kb[hh], ck_ref[0, hh:hh + 1, :], lo, hi, tri)
                m = jnp.max(parts[-1][0], axis=-1, keepdims=True)
                if lo > 0:
                    m = jnp.maximum(m, jnp.max(parts[0][0], axis=-1, keepdims=True))
                l, o = 0.0, 0.0
                for s, k0, k1 in parts:
                    p = jnp.exp(s - m)
                    l = l + jnp.sum(p, axis=-1, keepdims=True)
                    o = o + _nn(p.astype(BF16), vb[hh][k0:k1])
                outs.append(o / l)
                lse_ref[0, lo:hi, hh:hh + 1] = m + jnp.log(l)
            o_ref[lo:hi, :] = jnp.concatenate(outs, axis=1)

        @pl.when(pl.program_id(0) == _HP - 1)
        def _():
            local, remote = _gather_first_copies(x_refs, out_refs, *sems)
            for cp in remote + local:
                cp.wait()

    fixed = lambda p: (0, 0)
    anyspec = pl.BlockSpec(memory_space=pl.ANY)
    res = pl.pallas_call(
        body, name="attn_fwd", grid=(_HP,),
        in_specs=_qkv_specs() + [pl.BlockSpec((1, 2, T), lambda p: (p, 0, 0)), pl.BlockSpec((1, DH), fixed),
                                 pl.BlockSpec((1, DH), fixed)] + [anyspec] * n,
        out_specs=[pl.BlockSpec((T, LANES), lambda p: (0, p)), pl.BlockSpec((1, T, 2), lambda p: (p, 0, 0))]
        + [anyspec] * n,
        out_shape=[_sds((T, AW)), _sds((_HP, T, 2))] + [_sds((NDEV, *s.shape), s.dtype) for s in shards],
        scratch_shapes=_gather_first_scratch(n),
        compiler_params=_cp(("arbitrary",)),
    )(z, z, z, ck, g_q, g_k, *shards)
    return res[0], res[1], res[2:]


def _attn_bwd(z, ck, g_q, g_k, lse, datt, parts):
    scale = DH ** -0.5
    n = len(parts)

    def body(q_ref, k_ref, v_ref, ck_ref, gq_ref, gk_ref, lse_ref, do_ref, *rest):
        p_refs, rest = rest[:n], rest[n:]
        dq_ref, dk_ref, dv_ref, dck_ref, dgq_ref, dgk_ref = rest[:6]
        fc_refs, (dkn_acc, dv_acc, dc_acc), sems = rest[6:6 + n], rest[6 + n:9 + n], rest[9 + n:]

        @pl.when(pl.program_id(0) == 0)
        def _():
            for cp in _chip_swap_copies(p_refs, fc_refs, *sems):
                cp.start()

        nq, nk, rq, rk, qb, kb, vb, dob = [], [], [], [], [], [], [], []
        for hh in range(2):
            cols = slice(DH * hh, DH * (hh + 1))
            qv, kv = q_ref[:, cols], k_ref[:, cols]
            rq.append(_rms_r(qv))
            rk.append(_rms_r(kv))
            nq.append(qv * rq[hh])
            nk.append(kv * rk[hh])
            qb.append((nq[hh] * gq_ref[...] * scale).astype(BF16))
            kb.append((nk[hh] * gk_ref[...]).astype(BF16))
            vb.append(v_ref[:, cols].astype(BF16))
            dob.append(do_ref[:, cols].astype(BF16))
        dkn_acc[...] = jnp.zeros_like(dkn_acc)
        dv_acc[...] = jnp.zeros_like(dv_acc)
        dc_acc[...] = jnp.zeros_like(dc_acc)
        dgq = jnp.zeros((1, DH), F32)
        tri = _lower_triangle()
        for i in range(T // BQ):
            lo, hi = i * BQ, (i + 1) * BQ
            dqs = []
            for hh in range(2):
                lse = lse_ref[0, lo:hi, hh:hh + 1]
                pd = []
                for s, k0, k1 in _logit_parts(qb[hh], kb[hh], ck_ref[0, hh:hh + 1, :], lo, hi, tri):
                    pd.append((jnp.exp(s - lse), _nt(dob[hh][lo:hi], vb[hh][k0:k1]), k0, k1))
                mean = sum(jnp.sum(p * dp, axis=-1, keepdims=True) for p, dp, _, _ in pd)
                dqn = 0.0
                for p, dp, k0, k1 in pd:
                    ds = p * (dp - mean)
                    dsb = ds.astype(BF16)
                    dv_acc[hh, k0:k1, :] += _tn(p.astype(BF16), dob[hh][lo:hi])
                    dkn_acc[hh, k0:k1, :] += _tn(dsb, qb[hh][lo:hi])
                    dc_acc[hh:hh + 1, k0:k1] -= jnp.sum(ds, axis=0, keepdims=True)
                    dqn = dqn + _nn(dsb, kb[hh][k0:k1])
                dqx, dgp = _rms_bwd(nq[hh][lo:hi], rq[hh][lo:hi], gq_ref[...], dqn * scale)
                dqs.append(dqx)
                dgq = dgq + dgp
            dq_ref[lo:hi, :] = jnp.concatenate(dqs, axis=1).astype(BF16)
        dks, dgk = [], jnp.zeros((1, DH), F32)
        for hh in range(2):
            dkx, dgp = _rms_bwd(nk[hh], rk[hh], gk_ref[...], dkn_acc[hh])
            dks.append(dkx)
            dgk = dgk + dgp
        dk_ref[...] = jnp.concatenate(dks, axis=1).astype(BF16)
        dv_ref[...] = jnp.concatenate([dv_acc[0], dv_acc[1]], axis=1).astype(BF16)
        dck_ref[0] = dc_acc[...]

        @pl.when(pl.program_id(0) == 0)
        def _():
            dgq_ref[...] = jnp.zeros_like(dgq_ref)
            dgk_ref[...] = jnp.zeros_like(dgk_ref)
        dgq_ref[...] += dgq
        dgk_ref[...] += dgk

        @pl.when(pl.program_id(0) == _HP - 1)
        def _():
            for cp in _chip_swap_copies(p_refs, fc_refs, *sems):
                cp.wait()

    fixed = lambda p: (0, 0)
    pair = pl.BlockSpec((T, LANES), lambda p: (0, p))
    gsp = pl.BlockSpec((1, DH), fixed)
    ckspec = pl.BlockSpec((1, 2, T), lambda p: (p, 0, 0))
    anyspec = pl.BlockSpec(memory_space=pl.ANY)
    res = pl.pallas_call(
        body, name="attn_bwd", grid=(_HP,),
        in_specs=_qkv_specs() + [ckspec, gsp, gsp, pl.BlockSpec((1, T, 2), lambda p: (p, 0, 0)), pair] + [anyspec] * n,
        out_specs=[pair, pair, pair, ckspec, gsp, gsp] + [anyspec] * n,
        out_shape=[_sds((T, AW), BF16)] * 3 + [_sds((_HP, 2, T)), _sds((1, DH)), _sds((1, DH))]
        + [_sds((3, *p.shape[1:]), p.dtype) for p in parts],
        scratch_shapes=[pltpu.VMEM((2, T, DH), F32), pltpu.VMEM((2, T, DH), F32), pltpu.VMEM((2, T), F32),
                        pltpu.SemaphoreType.DMA((3 * n,)), pltpu.SemaphoreType.DMA((3 * n,))],
        compiler_params=_cp(("arbitrary",)),
    )(z, z, z, ck, g_q, g_k, lse, datt, *parts)
    return (*res[:6], res[6:])


def _disc(lr, li, ls, br_t, bi_t):
    step = jnp.exp(ls)
    er = jnp.exp(lr * step)
    ab_re = er * jnp.cos(li * step)
    ab_im = er * jnp.sin(li * step)
    num_re = ab_re - 1.0
    num_im = ab_im
    den = lr * lr + li * li
    f_re = (num_re * lr + num_im * li) / den
    f_im = (num_im * lr - num_re * li) / den
    bb_re = f_re[:, None, :] * br_t - f_im[:, None, :] * bi_t
    bb_im = f_re[:, None, :] * bi_t + f_im[:, None, :] * br_t
    return ab_re, ab_im, bb_re, bb_im


def _disc_fwd(lr, li, ls, br_t, bi_t):
    def body(lr_ref, li_ref, ls_ref, br_ref, bi_ref, ar_ref, ai_ref, bbr_ref, bbi_ref):
        ar, ai, bbr, bbi = _disc(lr_ref[...], li_ref[...], ls_ref[...], br_ref[...], bi_ref[...])
        ar_ref[...] = ar
        ai_ref[...] = ai
        bbr_ref[...] = bbr
        bbi_ref[...] = bbi

    return pl.pallas_call(
        body, name="ssm_disc_fwd",
        out_shape=[_sds((NG, NP)), _sds((NG, NP)), _sds((NG, GS, NP)), _sds((NG, GS, NP))],
        compiler_params=_cp())(lr, li, ls, br_t, bi_t)


def _disc_bwd(lr, li, ls, br_t, bi_t, dar, dai, dbbr, dbbi):
    def body(lr_ref, li_ref, ls_ref, br_ref, bi_ref, dar_ref, dai_ref, dbbr_ref, dbbi_ref,
             dlr_ref, dli_ref, dls_ref, dbr_ref, dbi_ref):
        _, vjp = jax.vjp(_disc, lr_ref[...], li_ref[...], ls_ref[...], br_ref[...], bi_ref[...])
        dlr, dli, dls, dbr, dbi = vjp((dar_ref[...], dai_ref[...], dbbr_ref[...], dbbi_ref[...]))
        dlr_ref[...] = dlr
        dli_ref[...] = dli
        dls_ref[...] = dls
        dbr_ref[...] = dbr
        dbi_ref[...] = dbi

    return pl.pallas_call(
        body, name="ssm_disc_bwd",
        out_shape=[_sds((NG, NP)), _sds((NG, NP)), _sds((NG, 1)), _sds((NG, GS, NP)), _sds((NG, GS, NP))],
        compiler_params=_cp())(lr, li, ls, br_t, bi_t, dar, dai, dbbr, dbbi)


def _cmul(ar, ai, br, bi):
    return ar * br - ai * bi, ar * bi + ai * br


def _scan(buf, a_re, a_im, reverse, other=None):
    ar = [jnp.broadcast_to(a_re[:, LANES * k:LANES * (k + 1)], (NSEG, LANES)) for k in range(4)]
    ai = [jnp.broadcast_to(a_im[:, LANES * k:LANES * (k + 1)], (NSEG, LANES)) for k in range(4)]

    def tile(ref, i, k):
        return ref[pl.ds(pl.multiple_of(i * NSEG, NSEG), NSEG), LANES * k:LANES * (k + 1)]

    def advance(i, xr, xi):
        nr = [ar[k] * xr[k] - ai[k] * xi[k] + tile(buf, i, k) for k in range(4)]
        ni = [ar[k] * xi[k] + ai[k] * xr[k] + tile(buf, i, 4 + k) for k in range(4)]
        return nr, ni

    def index(i):
        return (SEG - 1 - i) if reverse else i

    zeros = tuple(jnp.zeros((NSEG, LANES), F32) for _ in range(4))

    def sweep1(i, carry):
        nr, ni = advance(index(i), carry[0], carry[1])
        return tuple(nr), tuple(ni)

    er, ei = lax.fori_loop(0, SEG, sweep1, (zeros, zeros), unroll=4)

    row = lax.broadcasted_iota(jnp.int32, (NSEG, LANES), 0)
    sr, si = [], []
    for k in range(4):
        pr, pi = ar[k], ai[k]
        for _ in range(SEG.bit_length() - 1):
            pr, pi = _cmul(pr, pi, pr, pi)
        ir, ii = er[k], ei[k]
        for d in (1, 2, 4):
            shift = (NSEG - d) if reverse else d
            ok = (row < NSEG - d) if reverse else (row >= d)
            mr, mi = _cmul(pr, pi, pltpu.roll(ir, shift, 0), pltpu.roll(ii, shift, 0))
            ir = ir + jnp.where(ok, mr, 0.0)
            ii = ii + jnp.where(ok, mi, 0.0)
            pr, pi = _cmul(pr, pi, pr, pi)
        shift = (NSEG - 1) if reverse else 1
        ok = (row < NSEG - 1) if reverse else (row >= 1)
        sr.append(jnp.where(ok, pltpu.roll(ir, shift, 0), 0.0))
        si.append(jnp.where(ok, pltpu.roll(ii, shift, 0), 0.0))

    def sweep2(i, carry):
        xr, xi, dar, dai = carry
        idx = index(i)
        if other is not None:
            orr = [tile(other, idx, k) for k in range(4)]
            oi = [tile(other, idx, 4 + k) for k in range(4)]
            dar = tuple(dar[k] + xr[k] * orr[k] + xi[k] * oi[k] for k in range(4))
            dai = tuple(dai[k] + xi[k] * orr[k] - xr[k] * oi[k] for k in range(4))
        nr, ni = advance(idx, xr, xi)
        start = pl.multiple_of(idx * NSEG, NSEG)
        for k in range(4):
            buf[pl.ds(start, NSEG), LANES * k:LANES * (k + 1)] = nr[k]
            buf[pl.ds(start, NSEG), LANES * (4 + k):LANES * (5 + k)] = ni[k]
        return tuple(nr), tuple(ni), dar, dai

    _, _, dar, dai = lax.fori_loop(0, SEG, sweep2, (tuple(sr), tuple(si), zeros, zeros), unroll=4)
    if other is None:
        return None
    da_re = jnp.concatenate([jnp.sum(d, axis=0, keepdims=True) for d in dar], axis=1)
    da_im = jnp.concatenate([jnp.sum(d, axis=0, keepdims=True) for d in dai], axis=1)
    return da_re, da_im


_SSM_BLOCKS = 4


def _ssm_fwd(u, bb, ab, cc, dsk, shards):
    n = len(shards)

    def body(u_ref, bb_ref, a_ref, cc_ref, d_ref, *rest):
        x_refs, y_ref, out_refs, xb, sems = rest[:n], rest[n], rest[n + 1:2 * n + 1], rest[2 * n + 1], rest[2 * n + 2:]

        @pl.when(pl.program_id(0) == 0)
        def _():
            local, remote = _gather_first_copies(x_refs, out_refs, *sems)
            for cp in local + remote:
                cp.start()

        ub = u_ref[...]
        xb[...] = _nn(ub.astype(BF16), bb_ref[0])
        _scan(xb, a_ref[0, 0:1, :], a_ref[0, 1:2, :], reverse=False)
        y_ref[...] = _nn(xb[...].astype(BF16), cc_ref[0]) + d_ref[0] * ub

        @pl.when(pl.program_id(0) == _SSM_BLOCKS - 1)
        def _():
            local, remote = _gather_first_copies(x_refs, out_refs, *sems)
            for cp in remote + local:
                cp.wait()

    blk = lambda j: (j, 0, 0)
    col = pl.BlockSpec((T, LANES), lambda j: (0, j))
    anyspec = pl.BlockSpec(memory_space=pl.ANY)
    res = pl.pallas_call(
        body, name="ssm_fwd", grid=(_SSM_BLOCKS,),
        in_specs=[col, pl.BlockSpec((1, LANES, 1024), blk), pl.BlockSpec((1, 2, 512), blk),
                  pl.BlockSpec((1, 1024, LANES), blk), pl.BlockSpec((1, 1, LANES), blk)] + [anyspec] * n,
        out_specs=[col] + [anyspec] * n,
        out_shape=[_sds((T, SW))] + [_sds((NDEV, *s.shape), s.dtype) for s in shards],
        scratch_shapes=[pltpu.VMEM((T, 1024), F32)] + _gather_first_scratch(n),
        compiler_params=_cp(("arbitrary",)),
    )(u, bb, ab, cc, dsk, *shards)
    return res[0], res[1:]


def _ssm_bwd(u, dy, bb, ab, cc, dsk):
    def body(u_ref, dy_ref, bb_ref, a_ref, cc_ref, d_ref, du_ref, dbb_ref, da_ref, dcc_ref, dd_ref, xb, gb):
        ub, dyv = u_ref[...], dy_ref[...]
        ubb, dyb = ub.astype(BF16), dyv.astype(BF16)
        a_re, a_im = a_ref[0, 0:1, :], a_ref[0, 1:2, :]
        xb[...] = _nn(ubb, bb_ref[0])
        _scan(xb, a_re, a_im, reverse=False)
        dcc_ref[0] = _tn(xb[...].astype(BF16), dyb)
        gb[...] = _nt(dyb, cc_ref[0])
        da_re, da_im = _scan(gb, a_re, -a_im, reverse=True, other=xb)
        da_ref[0] = jnp.concatenate([da_re, da_im], axis=0)
        gc = gb[...].astype(BF16)
        du_ref[...] = _nt(gc, bb_ref[0]) + d_ref[0] * dyv
        dbb_ref[0] = _tn(ubb, gc)
        dd_ref[0] = jnp.sum(dyv * ub, axis=0, keepdims=True)

    blk = lambda j: (j, 0, 0)
    col = pl.BlockSpec((T, LANES), lambda j: (0, j))
    return pl.pallas_call(
        body, name="ssm_bwd", grid=(_SSM_BLOCKS,),
        in_specs=[col, col, pl.BlockSpec((1, LANES, 1024), blk), pl.BlockSpec((1, 2, 512), blk),
                  pl.BlockSpec((1, 1024, LANES), blk), pl.BlockSpec((1, 1, LANES), blk)],
        out_specs=[col, pl.BlockSpec((1, LANES, 1024), blk), pl.BlockSpec((1, 2, 512), blk),
                   pl.BlockSpec((1, 1024, LANES), blk), pl.BlockSpec((1, 1, LANES), blk)],
        out_shape=[_sds((T, SW)), _sds((4, LANES, 1024)), _sds((4, 2, 512)), _sds((4, 1024, LANES)), _sds((4, 1, LANES))],
        scratch_shapes=[pltpu.VMEM((T, 1024), F32), pltpu.VMEM((T, 1024), F32)],
        compiler_params=_cp(("parallel",)),
    )(u, dy, bb, ab, cc, dsk)


def _interleave(a):
    return a.reshape(NSEG, SEG, a.shape[1]).transpose(1, 0, 2).reshape(a.shape)


def _deinterleave(a):
    return a.reshape(SEG, NSEG, a.shape[1]).transpose(1, 0, 2).reshape(a.shape)


_TM_MIX = 256


def _mix_parts(att, y, wg, bg):
    y2 = jax.nn.gelu(y)
    y2b = y2.astype(BF16)
    sg = jax.nn.sigmoid(_nn(y2b, wg) + bg)
    ssm = y2 * sg
    ra, rs = _rms_r(att), _rms_r(ssm)
    return y2, y2b, sg, ssm, ra, rs


def _mix_fwd(att, y, x, wg, bg, ga, gs, wo):
    def body(att_ref, y_ref, x_ref, wg_ref, bg_ref, ga_ref, gs_ref, wo_ref, x1_ref):
        attv = att_ref[...]
        _, _, _, ssm, ra, rs = _mix_parts(attv, y_ref[...], wg_ref[...], bg_ref[...])
        ma = (attv * ra * ga_ref[...]).astype(BF16)
        ms = (ssm * rs * gs_ref[...]).astype(BF16)
        x1_ref[...] = x_ref[...] + _nn(ma, wo_ref[0:AW, :]) + _nn(ms, wo_ref[AW:D, :])

    row = lambda i: (i, 0)
    fixed = lambda i: (0, 0)
    half = pl.BlockSpec((_TM_MIX, AW), row)
    vec = pl.BlockSpec((1, AW), fixed)
    return pl.pallas_call(
        body, name="mix_fwd", grid=(T // _TM_MIX,),
        in_specs=[half, half, pl.BlockSpec((_TM_MIX, D), row), pl.BlockSpec((SW, SW), fixed), vec, vec, vec,
                  pl.BlockSpec((D, D), fixed)],
        out_specs=pl.BlockSpec((_TM_MIX, D), row), out_shape=_sds((T, D)),
        compiler_params=_cp(("parallel",)),
    )(att, y, x, wg, bg, ga, gs, wo)


def _mix_bwd(att, y, dx1, wg, bg, ga, gs, wo):
    last = T // _TM_MIX - 1

    def body(att_ref, y_ref, d_ref, wg_ref, bg_ref, ga_ref, gs_ref, wo_ref,
             datt_ref, dy_ref, dwg_ref, dbg_ref, dga_ref, dgs_ref, dwo_ref, dwg_acc, dwo_acc):
        attv, yv, db = att_ref[...], y_ref[...], d_ref[...].astype(BF16)
        y2, y2b, sg, ssm, ra, rs = _mix_parts(attv, yv, wg_ref[...], bg_ref[...])
        na, ns = attv * ra, ssm * rs
        ma = (na * ga_ref[...]).astype(BF16)
        ms = (ns * gs_ref[...]).astype(BF16)
        dma = _nt(db, wo_ref[0:AW, :])
        dms = _nt(db, wo_ref[AW:D, :])
        datt, dga = _rms_bwd(na, ra, ga_ref[...], dma)
        dssm, dgs = _rms_bwd(ns, rs, gs_ref[...], dms)
        dgl = dssm * y2 * sg * (1.0 - sg)
        dglb = dgl.astype(BF16)
        dy2 = dssm * sg + _nt(dglb, wg_ref[...])
        _, gelu_vjp = jax.vjp(jax.nn.gelu, yv)
        datt_ref[...] = datt
        dy_ref[...] = gelu_vjp(dy2)[0]

        @pl.when(pl.program_id(0) == 0)
        def _():
            dwg_acc[...] = jnp.zeros_like(dwg_acc)
            dwo_acc[...] = jnp.zeros_like(dwo_acc)
            dbg_ref[...] = jnp.zeros_like(dbg_ref)
            dga_ref[...] = jnp.zeros_like(dga_ref)
            dgs_ref[...] = jnp.zeros_like(dgs_ref)
        dwg_acc[...] += _tn(y2b, dglb)
        dbg_ref[...] += jnp.sum(dgl, axis=0, keepdims=True)
        dga_ref[...] += dga
        dgs_ref[...] += dgs
        dwo_acc[0:AW, :] += _tn(ma, db)
        dwo_acc[AW:D, :] += _tn(ms, db)

        @pl.when(pl.program_id(0) == last)
        def _():
            dwg_ref[...] = dwg_acc[...].astype(BF16)
            dwo_ref[...] = dwo_acc[...].astype(BF16)

    row = lambda i: (i, 0)
    fixed = lambda i: (0, 0)
    half = pl.BlockSpec((_TM_MIX, AW), row)
    vec = pl.BlockSpec((1, AW), fixed)
    sq = pl.BlockSpec((SW, SW), fixed)
    big = pl.BlockSpec((D, D), fixed)
    return pl.pallas_call(
        body, name="mix_bwd", grid=(T // _TM_MIX,),
        in_specs=[half, half, pl.BlockSpec((_TM_MIX, D), row), sq, vec, vec, vec, big],
        out_specs=[half, half, sq, vec, vec, vec, big],
        out_shape=[_sds((T, AW)), _sds((T, SW)), _sds((SW, SW), BF16), _sds((1, SW)), _sds((1, AW)), _sds((1, SW)),
                   _sds((D, D), BF16)],
        scratch_shapes=[pltpu.VMEM((SW, SW), F32), pltpu.VMEM((D, D), F32)],
        compiler_params=_cp(("arbitrary",)),
    )(att, y, dx1, wg, bg, ga, gs, wo)


_NCB = -(-CS // LANES)


def _ffn_up(x1, g, wu4):
    tm = 512

    def body(x_ref, g_ref, w_ref, h_ref, up_ref):
        @pl.when(pl.program_id(1) == 0)
        def _():
            xv = x_ref[...]
            h_ref[...] = (xv * _rms_r(xv) * g_ref[...]).astype(BF16)
        up_ref[0, 0] = _nt(h_ref[...], w_ref[0, 0])

    return pl.pallas_call(
        body, name="ffn_up", grid=(T // tm, NDEV),
        in_specs=[pl.BlockSpec((tm, D), lambda i, e: (i, 0)), pl.BlockSpec((1, D), lambda i, e: (0, 0)),
                  pl.BlockSpec((1, 1, CS, D), lambda i, e: (e // 4, e % 4, 0, 0))],
        out_specs=[pl.BlockSpec((tm, D), lambda i, e: (i, 0)),
                   pl.BlockSpec((1, 1, tm, CS), lambda i, e: (e // 4, e % 4, i, 0))],
        out_shape=[_sds((T, D), BF16), _sds((2, 4, T, CS))],
        compiler_params=_cp(("parallel", "arbitrary")),
    )(x1, g, wu4)


def _shift_down(h, k):
    row = lax.broadcasted_iota(jnp.int32, h.shape, 0)
    return jnp.where(row >= k, pltpu.roll(h, k, 0), 0.0)


def _shift_up(h, k):
    row = lax.broadcasted_iota(jnp.int32, h.shape, 0)
    return jnp.where(row < T - k, pltpu.roll(h, T - k, 0), 0.0)


def _conv(h, w, b):
    return w[0:1, :] * _shift_down(h, 2) + w[1:2, :] * _shift_down(h, 1) + w[2:3, :] * h + b


def _chan(gv, rows):
    return pl.BlockSpec((1, 1, rows, LANES), lambda d, j: (gv, d, 0, j))


def _ffn_act(up4, cw4, cb4):
    def body(ug_ref, uv_ref, wg_ref, wv_ref, bg_ref, bv_ref, act_ref):
        g = _conv(ug_ref[0, 0], wg_ref[0, 0], bg_ref[0, 0])
        v = _conv(uv_ref[0, 0], wv_ref[0, 0], bv_ref[0, 0])
        act_ref[0] = (g * jax.nn.sigmoid(g) * v).astype(BF16)

    return pl.pallas_call(
        body, name="ffn_act", grid=(4, _NCB),
        in_specs=[_chan(0, T), _chan(1, T), _chan(0, 3), _chan(1, 3), _chan(0, 1), _chan(1, 1)],
        out_specs=pl.BlockSpec((1, T, LANES), lambda d, j: (d, 0, j)), out_shape=_sds((4, T, CS), BF16),
        compiler_params=_cp(("parallel", "parallel")),
    )(up4, up4, cw4, cw4, cb4, cb4)


def _ffn_act_bwd(up4, dact, cw4, cb4):
    def conv_bwd(h, w, d):
        dup = w[2:3, :] * d + w[1:2, :] * _shift_up(d, 1) + w[0:1, :] * _shift_up(d, 2)
        dw = jnp.concatenate([jnp.sum(d * _shift_down(h, 2), axis=0, keepdims=True),
                              jnp.sum(d * _shift_down(h, 1), axis=0, keepdims=True),
                              jnp.sum(d * h, axis=0, keepdims=True)], axis=0)
        return dup, dw, jnp.sum(d, axis=0, keepdims=True)

    def body(ug_ref, uv_ref, da_ref, wg_ref, wv_ref, bg_ref, bv_ref, dup_ref, dw_ref, db_ref):
        ug, uv, da = ug_ref[0, 0], uv_ref[0, 0], da_ref[0]
        g = _conv(ug, wg_ref[0, 0], bg_ref[0, 0])
        v = _conv(uv, wv_ref[0, 0], bv_ref[0, 0])
        sg = jax.nn.sigmoid(g)
        dg = da * v * (sg * (1.0 + g * (1.0 - sg)))
        dv = da * (g * sg)
        dug, dwg, dbg = conv_bwd(ug, wg_ref[0, 0], dg)
        duv, dwv, dbv = conv_bwd(uv, wv_ref[0, 0], dv)
        dup_ref[0, 0] = dug.astype(BF16)
        dup_ref[1, 0] = duv.astype(BF16)
        dw_ref[0, 0] = dwg
        dw_ref[1, 0] = dwv
        db_ref[0, 0] = dbg
        db_ref[1, 0] = dbv

    both = lambda rows: pl.BlockSpec((2, 1, rows, LANES), lambda d, j: (0, d, 0, j))
    return pl.pallas_call(
        body, name="ffn_act_bwd", grid=(4, _NCB),
        in_specs=[_chan(0, T), _chan(1, T), pl.BlockSpec((1, T, LANES), lambda d, j: (d, 0, j)),
                  _chan(0, 3), _chan(1, 3), _chan(0, 1), _chan(1, 1)],
        out_specs=[both(T), both(3), both(1)],
        out_shape=[_sds((2, 4, T, CS), BF16), _sds((2, 4, 3, CS)), _sds((2, 4, 1, CS))],
        compiler_params=_cp(("parallel", "parallel")),
    )(up4, up4, dact, cw4, cw4, cb4, cb4)


def _ffn_down_loss(act, wd4, x1, target):
    tm = 256

    def body(a_ref, w_ref, x1_ref, t_ref, loss_ref, dx_ref, dxb_ref):
        x2 = x1_ref[...]
        for d in range(4):
            x2 = x2 + _nn(a_ref[d], w_ref[d])
        err = x2 - t_ref[...]
        dx = err * (1.0 / D)
        dx_ref[...] = dx
        dxb_ref[...] = dx.astype(BF16)

        @pl.when(pl.program_id(0) == 0)
        def _():
            loss_ref[...] = jnp.zeros_like(loss_ref)
        loss_ref[...] += 0.5 * jnp.sum(jnp.mean(err * err, axis=-1, keepdims=True), axis=0, keepdims=True)

    row = lambda i: (i, 0)
    fixed = lambda i: (0, 0)
    return pl.pallas_call(
        body, name="ffn_down_loss", grid=(T // tm,),
        in_specs=[pl.BlockSpec((4, tm, CS), lambda i: (0, i, 0)), pl.BlockSpec((4, CS, D), lambda i: (0, 0, 0)),
                  pl.BlockSpec((tm, D), row), pl.BlockSpec((tm, D), row)],
        out_specs=[pl.BlockSpec((1, 1), fixed), pl.BlockSpec((tm, D), row), pl.BlockSpec((tm, D), row)],
        out_shape=[_sds((1, 1)), _sds((T, D)), _sds((T, D), BF16)],
        compiler_params=_cp(("arbitrary",)),
    )(act, wd4, x1, target)


def _ffn_dact(dx2b, wd4):
    tm = 512

    def body(d_ref, w_ref, o_ref):
        o_ref[0] = _nt(d_ref[...], w_ref[0])

    return pl.pallas_call(
        body, name="ffn_dact", grid=(4, T // tm),
        in_specs=[pl.BlockSpec((tm, D), lambda d, i: (i, 0)), pl.BlockSpec((1, CS, D), lambda d, i: (d, 0, 0))],
        out_specs=pl.BlockSpec((1, tm, CS), lambda d, i: (d, i, 0)), out_shape=_sds((4, T, CS)),
        compiler_params=_cp(("parallel", "parallel")),
    )(dx2b, wd4)


def _ffn_dwd(act, dx2b):
    def body(a_ref, d_ref, o_ref):
        o_ref[0] = _tn(a_ref[0], d_ref[...]).astype(BF16)

    return pl.pallas_call(
        body, name="ffn_dwd", grid=(4,),
        in_specs=[pl.BlockSpec((1, T, CS), lambda d: (d, 0, 0)), pl.BlockSpec((T, D), lambda d: (0, 0))],
        out_specs=pl.BlockSpec((1, CS, D), lambda d: (d, 0, 0)), out_shape=_sds((4, CS, D), BF16),
        compiler_params=_cp(("parallel",)),
    )(act, dx2b)


def _ffn_dwu(h2, dup4):
    tn = 512

    def body(h_ref, d_ref, o_ref):
        o_ref[0, 0] = _tn(d_ref[0, 0], h_ref[...]).astype(BF16)

    return pl.pallas_call(
        body, name="ffn_dwu", grid=(NDEV, D // tn),
        in_specs=[pl.BlockSpec((T, tn), lambda e, i: (0, i)),
                  pl.BlockSpec((1, 1, T, CS), lambda e, i: (e // 4, e % 4, 0, 0))],
        out_specs=pl.BlockSpec((1, 1, CS, tn), lambda e, i: (e // 4, e % 4, 0, i)),
        out_shape=_sds((2, 4, CS, D), BF16),
        compiler_params=_cp(("parallel", "parallel")),
    )(h2, dup4)


def _ffn_up_bwd(dup4, wu4, x1, g, dres):
    tm = 256

    def body(d_ref, w_ref, x_ref, g_ref, dres_ref, dx_ref, dg_ref):
        dh = jnp.zeros((tm, D), F32)
        for gv in range(2):
            for d in range(4):
                dh = dh + _nn(d_ref[gv, d], w_ref[gv, d])
        xv = x_ref[...]
        r = _rms_r(xv)
        dxr, dgp = _rms_bwd(xv * r, r, g_ref[...], dh)
        dx_ref[...] = dres_ref[...] + dxr

        @pl.when(pl.program_id(0) == 0)
        def _():
            dg_ref[...] = jnp.zeros_like(dg_ref)
        dg_ref[...] += dgp

    row = lambda i: (i, 0)
    fixed = lambda i: (0, 0)
    return pl.pallas_call(
        body, name="ffn_up_bwd", grid=(T // tm,),
        in_specs=[pl.BlockSpec((2, 4, tm, CS), lambda i: (0, 0, i, 0)), pl.BlockSpec((2, 4, CS, D), lambda i: (0, 0, 0, 0)),
                  pl.BlockSpec((tm, D), row), pl.BlockSpec((1, D), fixed), pl.BlockSpec((tm, D), row)],
        out_specs=[pl.BlockSpec((tm, D), row), pl.BlockSpec((1, D), fixed)],
        out_shape=[_sds((T, D)), _sds((1, D))],
        compiler_params=_cp(("arbitrary",)),
    )(dup4, wu4, x1, g, dres)


def _adamw_math(w, g, m, v):
    m = ADAM_B1 * m + (1.0 - ADAM_B1) * g
    v = ADAM_B2 * v + (1.0 - ADAM_B2) * jnp.square(g)
    m_hat = m / (1.0 - ADAM_B1 ** ADAM_STEP)
    v_hat = v / (1.0 - ADAM_B2 ** ADAM_STEP)
    delta = -ADAM_LR * (m_hat / (jnp.sqrt(v_hat) + ADAM_EPS) + ADAM_WD * w)
    return delta, m, v


def _where_am_i():
    x, y, c = _place()
    return jnp.stack([c, 2 * x + y]).astype(jnp.int32)


def _pair_sum(a4, recv, name):
    _, _, rows, cols = a4.shape
    tr = _row_tile(rows, cols, 3 << 20)

    def body(sel_ref, own_ref, recv_ref, out_ref):
        out_ref[...] = (own_ref[0].astype(F32) + recv_ref[...].astype(F32)).astype(out_ref.dtype)

    grid_spec = pltpu.PrefetchScalarGridSpec(
        num_scalar_prefetch=1, grid=(4, rows // tr),
        in_specs=[pl.BlockSpec((1, 1, tr, cols), lambda k, i, s: (k, s[0], i, 0)),
                  pl.BlockSpec((1, tr, cols), lambda k, i, s: (k, i, 0))],
        out_specs=pl.BlockSpec((1, tr, cols), lambda k, i, s: (k, i, 0)),
    )
    return pl.pallas_call(
        body, name=name, grid_spec=grid_spec, out_shape=_sds((4, rows, cols), a4.dtype),
        compiler_params=_cp(("arbitrary", "arbitrary")),
    )(_where_am_i(), a4, recv)


def _adamw_rs(a4, recv, from_chips, w, m, v, name):
    rows, cols = w.shape
    tr = _row_tile(rows, cols, 3 << 19)

    def body(sel_ref, own_ref, recv_ref, fc_ref, w_ref, m_ref, v_ref, g_ref, d_ref, nm_ref, nv_ref):
        g = own_ref[0, 0].astype(F32) + recv_ref[0].astype(F32)
        for j in range(3):
            g = g + fc_ref[j].astype(F32)
        d, nm, nv = _adamw_math(w_ref[...], g, m_ref[...], v_ref[...])
        g_ref[...] = g
        d_ref[...] = d
        nm_ref[...] = nm
        nv_ref[...] = nv

    flat = pl.BlockSpec((tr, cols), lambda i, s: (i, 0))
    grid_spec = pltpu.PrefetchScalarGridSpec(
        num_scalar_prefetch=1, grid=(rows // tr,),
        in_specs=[pl.BlockSpec((1, 1, tr, cols), lambda i, s: (s[1], s[0], i, 0)),
                  pl.BlockSpec((1, tr, cols), lambda i, s: (s[1], i, 0)),
                  pl.BlockSpec((3, tr, cols), lambda i, s: (0, i, 0)), flat, flat, flat],
        out_specs=[flat] * 4,
    )
    return pl.pallas_call(
        body, name=name, grid_spec=grid_spec, out_shape=[_sds((rows, cols))] * 4,
        compiler_params=_cp(("arbitrary",)),
    )(_where_am_i(), a4, recv, from_chips, w, m, v)


def _adamw_small(gall, w, m, v):
    rows = w.shape[0]

    def body(ga_ref, w_ref, m_ref, v_ref, g_ref, d_ref, nm_ref, nv_ref):
        g = ga_ref[0]
        for dev in range(1, NDEV):
            g = g + ga_ref[dev]
        d, nm, nv = _adamw_math(w_ref[...], g[0:rows], m_ref[...], v_ref[...])
        g_ref[...] = g
        d_ref[...] = d
        nm_ref[...] = nm
        nv_ref[...] = nv

    return pl.pallas_call(body, name="adamw_small", out_shape=[_sds((rows + 8, LANES))] + [_sds((rows, LANES))] * 3,
                          compiler_params=_cp())(gall, w, m, v)


def _rows128(a):
    n = a.shape[0]
    r = (-n) % (8 * LANES)
    if r:
        a = jnp.pad(a, [(0, r)])
    return a.reshape(-1, LANES)


_SMALL = (("g_mix", D), ("b_f", NH), ("g_q", DH), ("g_k", DH), ("lambda_re", NG * NP), ("lambda_im", NG * NP),
          ("log_step", NG), ("b_re", NG * NP * GS), ("b_im", NG * NP * GS), ("c_re", NG * GS * NP),
          ("c_im", NG * GS * NP), ("d_skip", NG * GS), ("b_glu", SW), ("g_attn_out", AW), ("g_ssm_out", SW),
          ("g_ffn", D), ("conv_b", 2 * DFF))


def _pack_small(d):
    return jnp.concatenate([_rows128(d[name].reshape(-1).astype(F32)) for name, _ in _SMALL], axis=0)


def _unpack_small(p, shapes):
    out, off = {}, 0
    for name, n in _SMALL:
        rows = -(-n // (8 * LANES)) * 8
        out[name] = p[off:off + rows].reshape(-1)[:n].reshape(shapes[name])
        off += rows
    return out


def _block_diag(m):
    eye = jnp.eye(8, dtype=m.dtype)
    r, c = m.shape[2], m.shape[3]
    return (m[:, :, :, None, :] * eye[None, :, None, :, None]).reshape(4, 8 * r, 8 * c)


def _diag_blocks(dense, r, c):
    eye = jnp.eye(8, dtype=dense.dtype)
    return jnp.sum(dense.reshape(4, 8, r, 8, c) * eye[None, :, None, :, None], axis=3)


def kernel(x, g_mix, w_in, b_f, g_q, g_k, lambda_re, lambda_im, log_step, b_re, b_im, c_re, c_im, d_skip, w_glu, b_glu, g_attn_out, g_ssm_out, w_out, g_ffn, w_up, conv_w, conv_b, w_down, loss_target, m_g_mix, m_w_in, m_b_f, m_g_q, m_g_k, m_lambda_re, m_lambda_im, m_log_step, m_b_re, m_b_im, m_c_re, m_c_im, m_d_skip, m_w_glu, m_b_glu, m_g_attn_out, m_g_ssm_out, m_w_out, m_g_ffn, m_w_up, m_conv_w, m_conv_b, m_w_down, v_g_mix, v_w_in, v_b_f, v_g_q, v_g_k, v_lambda_re, v_lambda_im, v_log_step, v_b_re, v_b_im, v_c_re, v_c_im, v_d_skip, v_w_glu, v_b_glu, v_g_attn_out, v_g_ssm_out, v_w_out, v_g_ffn, v_w_up, v_conv_w, v_conv_b, v_w_down):
    weights = dict(g_mix=g_mix, w_in=w_in, b_f=b_f, g_q=g_q, g_k=g_k, lambda_re=lambda_re, lambda_im=lambda_im,
                   log_step=log_step, b_re=b_re, b_im=b_im, c_re=c_re, c_im=c_im, d_skip=d_skip, w_glu=w_glu,
                   b_glu=b_glu, g_attn_out=g_attn_out, g_ssm_out=g_ssm_out, w_out=w_out, g_ffn=g_ffn, w_up=w_up,
                   conv_w=conv_w, conv_b=conv_b, w_down=w_down)
    mom_m = dict(g_mix=m_g_mix, w_in=m_w_in, b_f=m_b_f, g_q=m_g_q, g_k=m_g_k, lambda_re=m_lambda_re,
                 lambda_im=m_lambda_im, log_step=m_log_step, b_re=m_b_re, b_im=m_b_im, c_re=m_c_re, c_im=m_c_im,
                 d_skip=m_d_skip, w_glu=m_w_glu, b_glu=m_b_glu, g_attn_out=m_g_attn_out, g_ssm_out=m_g_ssm_out,
                 w_out=m_w_out, g_ffn=m_g_ffn, w_up=m_w_up, conv_w=m_conv_w, conv_b=m_conv_b, w_down=m_w_down)
    mom_v = dict(g_mix=v_g_mix, w_in=v_w_in, b_f=v_b_f, g_q=v_g_q, g_k=v_g_k, lambda_re=v_lambda_re,
                 lambda_im=v_lambda_im, log_step=v_log_step, b_re=v_b_re, b_im=v_b_im, c_re=v_c_re, c_im=v_c_im,
                 d_skip=v_d_skip, w_glu=v_w_glu, b_glu=v_b_glu, g_attn_out=v_g_attn_out, g_ssm_out=v_g_ssm_out,
                 w_out=v_w_out, g_ffn=v_g_ffn, w_up=v_w_up, conv_w=v_conv_w, conv_b=v_conv_b, w_down=v_w_down)
    names = list(weights)
    big = ("w_in", "w_glu", "w_out", "w_up", "w_down", "conv_w")

    xs = x[0]
    target = loss_target[0]
    row = lambda a: a.reshape(1, -1)

    (win_g,) = _all_gather([w_in.T.astype(BF16)], "gather_w_in")
    w_in_t = win_g.reshape(8 * 257, D)
    wcat = jnp.concatenate([w_in_t[0:1536], w_in_t[1544:2056], jnp.pad(w_in_t[1536:1544], ((0, LANES - NH), (0, 0)))],
                           axis=0)
    cb4 = conv_b.reshape(2, 4, 1, CS)

    h1, z, f_t = _in_proj(xs, row(g_mix), wcat)
    bf_col = b_f.reshape(NH, 1)
    ck = _gates_fwd(f_t, bf_col).reshape(_HP, 2, T)
    att, lse, (wg_g, wo_g, wu_g, cw_g) = _attn_fwd(
        z, ck, row(g_q), row(g_k), [w_glu.astype(BF16), w_out.astype(BF16), w_up.T.astype(BF16), conv_w])

    ls_col = log_step.reshape(NG, 1)
    br_t, bi_t = b_re.transpose(0, 2, 1), b_im.transpose(0, 2, 1)
    ab_re, ab_im, bb_re, bb_im = _disc_fwd(lambda_re, lambda_im, ls_col, br_t, bi_t)
    bb = jnp.concatenate([_block_diag(bb_re.reshape(4, 8, GS, NP)), _block_diag(bb_im.reshape(4, 8, GS, NP))],
                         axis=2).astype(BF16)
    ab = jnp.stack([ab_re.reshape(4, 512), ab_im.reshape(4, 512)], axis=1)
    cdiag = lambda cm: _block_diag(cm.reshape(4, 8, GS, NP).transpose(0, 1, 3, 2))
    cc = jnp.concatenate([cdiag(c_re), -cdiag(c_im)], axis=1).astype(BF16)
    dsk = d_skip.reshape(4, 1, LANES)
    u_il = _interleave(z[:, 1536:2048])
    y_il, (wd_g,) = _ssm_fwd(u_il, bb, ab, cc, dsk, [w_down.astype(BF16)])
    y = _deinterleave(y_il)
    wg_g, wo_g, wu_g, cw_g, wd_g = _gather_second([wg_g, wo_g, wu_g, cw_g, wd_g], "gather_pass_on")
    wg_f = wg_g.reshape(SW, SW)
    wo_f = wo_g.reshape(D, D)
    wu4 = wu_g.reshape(2, 4, CS, D)
    wd4 = wd_g.reshape(4, CS, D)
    cw4 = cw_g.reshape(2, 4, 3, CS)

    x1 = _mix_fwd(att, y, xs, wg_f, row(b_glu), row(g_attn_out), row(g_ssm_out), wo_f)
    h2, up4 = _ffn_up(x1, row(g_ffn), wu4)
    act = _ffn_act(up4, cw4, cb4)
    loss_dev, dx2, dx2b = _ffn_down_loss(act, wd4, x1, target)

    dact = _ffn_dact(dx2b, wd4)
    d_wd = _ffn_dwd(act, dx2b)
    dup4, dcw4, dcb4 = _ffn_act_bwd(up4, dact, cw4, cb4)
    d_wu = _ffn_dwu(h2, dup4)
    mlp = ("w_up", "w_down", "conv_w")
    by_dest = {"w_up": d_wu.reshape(4, 2, CS, D), "w_down": d_wd.reshape(4, 2, DFF // NDEV, D),
               "conv_w": dcw4.reshape(4, 2, 3, CS)}
    from_sibling = dict(zip(mlp, _swap_sibling([by_dest[n] for n in mlp], "rs_pair_swap_mlp")))
    chip_sums = {n: _pair_sum(by_dest[n], from_sibling[n], "rs_pair_sum_" + n) for n in mlp}
    dx1, dg_ffn = _ffn_up_bwd(dup4, wu4, x1, row(g_ffn), dx2)
    datt, dy, d_wg, d_bg, d_ga, d_gs, d_wo = _mix_bwd(att, y, dx1, wg_f, row(b_glu), row(g_attn_out),
                                                       row(g_ssm_out), wo_f)
    du_il, dbb, dab, dcc, ddsk = _ssm_bwd(u_il, _interleave(dy), bb, ab, cc, dsk)
    du = _deinterleave(du_il).astype(BF16)
    dbb_re = _diag_blocks(dbb[:, :, 0:512], GS, NP).reshape(NG, GS, NP)
    dbb_im = _diag_blocks(dbb[:, :, 512:1024], GS, NP).reshape(NG, GS, NP)
    dlr, dli, dls, dbr_t, dbi_t = _disc_bwd(lambda_re, lambda_im, ls_col, br_t, bi_t,
                                            dab[:, 0].reshape(NG, NP), dab[:, 1].reshape(NG, NP), dbb_re, dbb_im)
    d_cre = _diag_blocks(dcc[:, 0:512], NP, GS).transpose(0, 1, 3, 2).reshape(NG, GS, NP)
    d_cim = -_diag_blocks(dcc[:, 512:1024], NP, GS).transpose(0, 1, 3, 2).reshape(NG, GS, NP)

    dq, dk, dv, dck, d_gq, d_gk, fc_mlp = _attn_bwd(z, ck, row(g_q), row(g_k), lse, datt, [chip_sums[n] for n in mlp])
    from_chips = dict(zip(mlp, fc_mlp))
    df_t, d_bf = _gates_bwd(dck.reshape(NH, T), f_t, bf_col)
    df = jnp.concatenate([df_t.T, jnp.zeros((T, LANES - NH), F32)], axis=1).astype(BF16)
    d_wcat = _in_proj_dw(h1, dq, dk, dv, du, df)
    grad_x, dg_mix = _in_proj_bwd(dq, dk, dv, du, df, wcat, xs, row(g_mix), dx1)
    d_win = jnp.concatenate([d_wcat[0:1536], d_wcat[2048:2048 + NH], d_wcat[1536:2048]], axis=0)
    d_win = d_win.astype(BF16).reshape(NDEV, 257, D)

    mixer = ("w_out", "w_glu", "w_in")
    by_dest.update({"w_out": d_wo.reshape(4, 2, D // NDEV, D), "w_glu": d_wg.reshape(4, 2, SW // NDEV, SW),
                    "w_in": d_win.reshape(4, 2, 257, D)})
    from_sibling.update(zip(mixer, _swap_sibling([by_dest[n] for n in mixer], "rs_pair_swap_mixer")))
    chip_sums.update({n: _pair_sum(by_dest[n], from_sibling[n], "rs_pair_sum_" + n) for n in mixer})
    from_chips.update(zip(mixer, _swap_chips([chip_sums[n] for n in mixer], "rs_chip_swap_mixer")))

    grads, deltas, new_m, new_v = {}, {}, {}, {}
    for n in mlp + mixer:
        flip = (lambda a: a.T) if n in ("w_up", "w_in") else (lambda a: a)
        outs = _adamw_rs(by_dest[n], from_sibling[n], from_chips[n], flip(weights[n]), flip(mom_m[n]), flip(mom_v[n]),
                         "adamw_" + n)
        grads[n], deltas[n], new_m[n], new_v[n] = (flip(o) for o in outs)

    small = dict(g_mix=dg_mix, b_f=d_bf, g_q=d_gq, g_k=d_gk, lambda_re=dlr, lambda_im=dli, log_step=dls,
                 b_re=dbr_t.transpose(0, 2, 1), b_im=dbi_t.transpose(0, 2, 1), c_re=d_cre, c_im=d_cim, d_skip=ddsk,
                 b_glu=d_bg, g_attn_out=d_ga, g_ssm_out=d_gs, g_ffn=dg_ffn, conv_b=dcb4)
    loss_tile = jnp.pad(loss_dev, ((0, 7), (0, LANES - 1)))
    (small_all,) = _all_gather([jnp.concatenate([_pack_small(small), loss_tile], axis=0)], "gather_small_grads")
    sg_p, sd_p, sm_p, sv_p = _adamw_small(small_all, _pack_small({n: weights[n] for n in small}),
                                          _pack_small({n: mom_m[n] for n in small}),
                                          _pack_small({n: mom_v[n] for n in small}))
    loss = sg_p[sg_p.shape[0] - 8, 0]
    shapes = {n: weights[n].shape for n in small}
    sg, sd, sm, sv = (_unpack_small(p, shapes) for p in (sg_p, sd_p, sm_p, sv_p))
    for n in small:
        grads[n], deltas[n], new_m[n], new_v[n] = sg[n], sd[n], sm[n], sv[n]

    return (loss, grad_x[None], *[grads[n] for n in names], *[deltas[n] for n in names],
            *[new_m[n] for n in names], *[new_v[n] for n in names])
```

```python
import jax
import jax.numpy as jnp
from jax import lax
from jax.experimental import pallas as pl
from jax.experimental.pallas import tpu as pltpu

F32, BF16 = jnp.float32, jnp.bfloat16
T, D = 2048, 1024
NH, DH = 8, 64
AW, SW = 512, 512
NG, GS, NP = 32, 16, 64
DFF = 2816
NDEV = 8
CS = 2 * DFF // NDEV
EPS = 1e-6
NEG = -1e30
ZC = 4 * 512 + 128
SEG = 256
NSEG = T // SEG
BQ = 256
VMEM_LIMIT = 56 * 1024 * 1024
LANES = 128
MESH = pl.DeviceIdType.MESH
HI = lax.Precision.HIGHEST

ADAM_LR, ADAM_B1, ADAM_B2, ADAM_EPS, ADAM_WD, ADAM_STEP = 0.001, 0.9, 0.999, 1e-08, 0.01, 10


def _cp(dims=None):
    if dims is None:
        return pltpu.CompilerParams(vmem_limit_bytes=VMEM_LIMIT)
    return pltpu.CompilerParams(dimension_semantics=dims, vmem_limit_bytes=VMEM_LIMIT)


def _sds(shape, dtype=F32):
    return jax.ShapeDtypeStruct(shape, dtype)


def _nt(a, b):
    return lax.dot_general(a, b, (((1,), (1,)), ((), ())), preferred_element_type=F32)


def _tn(a, b):
    return lax.dot_general(a, b, (((0,), (0,)), ((), ())), preferred_element_type=F32)


def _nn(a, b):
    return jnp.dot(a, b, preferred_element_type=F32)


def _rms_r(x):
    return lax.rsqrt(jnp.mean(x * x, axis=-1, keepdims=True) + EPS)


def _rms_bwd(n, r, g, dh):
    dn = dh * g
    dx = r * (dn - n * jnp.mean(dn * n, axis=-1, keepdims=True))
    return dx, jnp.sum(dh * n, axis=0, keepdims=True)


def _row_tile(rows, cols, limit):
    tr = rows
    while tr * cols * 4 > limit and tr % 32 == 0:
        tr //= 2
    return tr


def _place():
    return lax.axis_index("x"), lax.axis_index("y"), lax.axis_index("c")


def _all_gather(shards, name):
    n = len(shards)

    def body(*refs):
        x_refs, out_refs = refs[:n], refs[n:2 * n]
        send_sems, recv_sems, local_sems = refs[2 * n:]
        x, y, c = _place()
        me, sibling = (x, y, c), (x, y, 1 - c)
        chips = [(1 - x, y), (x, 1 - y), (1 - x, 1 - y)]

        def copy(a, k, block, to, src=None):
            px, py, pc = block
            slot = out_refs[a].at[4 * px + 2 * py + pc]
            return pltpu.make_async_remote_copy(
                src_ref=slot if src is None else src, dst_ref=slot,
                send_sem=send_sems.at[7 * a + k], recv_sem=recv_sems.at[7 * a + k], device_id=to, device_id_type=MESH)

        mine, first, passed = [], [], []
        for a in range(n):
            cp = pltpu.make_async_copy(x_refs[a], out_refs[a].at[4 * x + 2 * y + c], local_sems.at[a])
            cp.start()
            mine.append(cp)
            cps = [copy(a, 0, me, sibling, src=x_refs[a])]
            cps += [copy(a, 1 + j, me, (*chip, c), src=x_refs[a]) for j, chip in enumerate(chips)]
            for cp in cps:
                cp.start()
            first += cps
        for a in range(n):
            for j, chip in enumerate(chips):
                copy(a, 1 + j, (*chip, c), me).wait_recv()
                fwd = copy(a, 4 + j, (*chip, c), sibling)
                fwd.start()
                passed.append(fwd)
        for a in range(n):
            copy(a, 0, sibling, me).wait_recv()
            for j, chip in enumerate(chips):
                copy(a, 4 + j, (*chip, 1 - c), me).wait_recv()
        for cp in first + passed:
            cp.wait_send()
        for cp in mine:
            cp.wait()

    anyspec = pl.BlockSpec(memory_space=pl.ANY)
    return pl.pallas_call(
        body, name=name,
        out_shape=[_sds((NDEV, *s.shape), s.dtype) for s in shards],
        in_specs=[anyspec] * n, out_specs=[anyspec] * n,
        scratch_shapes=[pltpu.SemaphoreType.DMA((7 * n,)), pltpu.SemaphoreType.DMA((7 * n,)),
                        pltpu.SemaphoreType.DMA((n,))],
    )(*shards)


def _gather_first_copies(x_refs, out_refs, send_sems, recv_sems, local_sems):
    x, y, c = _place()
    slot = 4 * x + 2 * y + c
    peers = [(x, y, 1 - c), (1 - x, y, c), (x, 1 - y, c), (1 - x, 1 - y, c)]
    local, remote = [], []
    for a, (x_ref, out_ref) in enumerate(zip(x_refs, out_refs)):
        local.append(pltpu.make_async_copy(x_ref, out_ref.at[slot], local_sems.at[a]))
        for k, peer in enumerate(peers):
            remote.append(pltpu.make_async_remote_copy(
                src_ref=x_ref, dst_ref=out_ref.at[slot], send_sem=send_sems.at[4 * a + k],
                recv_sem=recv_sems.at[4 * a + k], device_id=peer, device_id_type=MESH))
    return local, remote


def _gather_first_scratch(n):
    return [pltpu.SemaphoreType.DMA((4 * n,)), pltpu.SemaphoreType.DMA((4 * n,)), pltpu.SemaphoreType.DMA((n,))]


def _gather_second(bufs, name):
    n = len(bufs)

    def body(*refs):
        out_refs = refs[n:2 * n]
        send_sems, recv_sems = refs[2 * n:]
        x, y, c = _place()
        chips = [(1 - x, y), (x, 1 - y), (1 - x, 1 - y)]
        cps = []
        for a in range(n):
            for j, (px, py) in enumerate(chips):
                block = out_refs[a].at[4 * px + 2 * py + c]
                cps.append(pltpu.make_async_remote_copy(
                    src_ref=block, dst_ref=block, send_sem=send_sems.at[3 * a + j], recv_sem=recv_sems.at[3 * a + j],
                    device_id=(x, y, 1 - c), device_id_type=MESH))
        for cp in cps:
            cp.start()
        for cp in cps:
            cp.wait()

    anyspec = pl.BlockSpec(memory_space=pl.ANY)
    return pl.pallas_call(
        body, name=name,
        out_shape=[_sds(b.shape, b.dtype) for b in bufs],
        in_specs=[anyspec] * n, out_specs=[anyspec] * n,
        input_output_aliases={a: a for a in range(n)},
        scratch_shapes=[pltpu.SemaphoreType.DMA((3 * n,)), pltpu.SemaphoreType.DMA((3 * n,))],
    )(*bufs)


def _chip_swap_copies(p_refs, out_refs, send_sems, recv_sems):
    x, y, c = _place()
    chips = [(1 - x, y), (x, 1 - y), (1 - x, 1 - y)]
    cps = []
    for a, (p_ref, out_ref) in enumerate(zip(p_refs, out_refs)):
        for j, (px, py) in enumerate(chips):
            cps.append(pltpu.make_async_remote_copy(
                src_ref=p_ref.at[2 * px + py], dst_ref=out_ref.at[j],
                send_sem=send_sems.at[3 * a + j], recv_sem=recv_sems.at[3 * a + j],
                device_id=(px, py, c), device_id_type=MESH))
    return cps


def _swap_sibling(arrs, name):
    n = len(arrs)

    def body(*refs):
        a_refs, out_refs = refs[:n], refs[n:2 * n]
        send_sems, recv_sems = refs[2 * n:]
        x, y, c = _place()
        cps = []
        for a in range(n):
            for k in range(4):
                cps.append(pltpu.make_async_remote_copy(
                    src_ref=a_refs[a].at[k, 1 - c], dst_ref=out_refs[a].at[k],
                    send_sem=send_sems.at[4 * a + k], recv_sem=recv_sems.at[4 * a + k],
                    device_id=(x, y, 1 - c), device_id_type=MESH))
        for cp in cps:
            cp.start()
        for cp in cps:
            cp.wait()

    anyspec = pl.BlockSpec(memory_space=pl.ANY)
    return pl.pallas_call(
        body, name=name,
        out_shape=[_sds((4, *a.shape[2:]), a.dtype) for a in arrs],
        in_specs=[anyspec] * n, out_specs=[anyspec] * n,
        scratch_shapes=[pltpu.SemaphoreType.DMA((4 * n,)), pltpu.SemaphoreType.DMA((4 * n,))],
    )(*arrs)


def _in_proj(x, g, wcat, shards):
    tm = 256
    n = len(shards)

    def body(x_ref, g_ref, w_ref, *rest):
        x_refs, (h_ref, z_ref, ft_ref), out_refs, sems = rest[:n], rest[n:n + 3], rest[n + 3:2 * n + 3], rest[2 * n + 3:]

        @pl.when(pl.program_id(0) == 0)
        def _():
            local, remote = _gather_first_copies(x_refs, out_refs, *sems)
            for cp in local + remote:
                cp.start()

        xv = x_ref[...]
        h = (xv * _rms_r(xv) * g_ref[...]).astype(BF16)
        h_ref[...] = h
        z = _nt(h, w_ref[...])
        z_ref[...] = z
        ft_ref[...] = z[:, 2048:ZC].T[0:NH, :]

        @pl.when(pl.program_id(0) == T // tm - 1)
        def _():
            local, remote = _gather_first_copies(x_refs, out_refs, *sems)
            for cp in remote + local:
                cp.wait()

    row = lambda i: (i, 0)
    fixed = lambda i: (0, 0)
    anyspec = pl.BlockSpec(memory_space=pl.ANY)
    res = pl.pallas_call(
        body, name="in_proj", grid=(T // tm,),
        in_specs=[pl.BlockSpec((tm, D), row), pl.BlockSpec((1, D), fixed), pl.BlockSpec((ZC, D), fixed)] + [anyspec] * n,
        out_specs=[pl.BlockSpec((tm, D), row), pl.BlockSpec((tm, ZC), row), pl.BlockSpec((NH, tm), lambda i: (0, i))]
        + [anyspec] * n,
        out_shape=[_sds((T, D), BF16), _sds((T, ZC)), _sds((NH, T))] + [_sds((NDEV, *s.shape), s.dtype) for s in shards],
        scratch_shapes=_gather_first_scratch(n),
        compiler_params=_cp(("arbitrary",)),
    )(x, g, wcat, *shards)
    return res[0], res[1], res[2], res[3:]


def _in_proj_dw(h, dq, dk, dv, du, df):
    tm = 512

    def body(h_ref, dq_ref, dk_ref, dv_ref, du_ref, df_ref, o_ref):
        hv = h_ref[...]
        for j, d_ref in enumerate((dq_ref, dk_ref, dv_ref, du_ref)):
            o_ref[512 * j:512 * (j + 1), :] = _tn(d_ref[...], hv)
        o_ref[2048:ZC, :] = _tn(df_ref[...], hv)

    full = lambda w: pl.BlockSpec((T, w), lambda i: (0, 0))
    return pl.pallas_call(
        body, name="in_proj_dw", grid=(D // tm,),
        in_specs=[pl.BlockSpec((T, tm), lambda i: (0, i)), full(512), full(512), full(512), full(512), full(LANES)],
        out_specs=pl.BlockSpec((ZC, tm), lambda i: (0, i)), out_shape=_sds((ZC, D)),
        compiler_params=_cp(("parallel",)),
    )(h, dq, dk, dv, du, df)


def _in_proj_bwd(dq, dk, dv, du, df, wcat, x, g, dres, parts):
    tm = 256
    n = len(parts)

    def body(dq_ref, dk_ref, dv_ref, du_ref, df_ref, w_ref, x_ref, g_ref, dres_ref, *rest):
        p_refs, (dx_ref, dg_ref), fc_refs, sems = rest[:n], rest[n:n + 2], rest[n + 2:2 * n + 2], rest[2 * n + 2:]

        @pl.when(pl.program_id(0) == 0)
        def _():
            for cp in _chip_swap_copies(p_refs, fc_refs, *sems):
                cp.start()

        dh = _nn(df_ref[...], w_ref[2048:ZC, :])
        for j, d_ref in enumerate((dq_ref, dk_ref, dv_ref, du_ref)):
            dh = dh + _nn(d_ref[...], w_ref[512 * j:512 * (j + 1), :])
        xv = x_ref[...]
        r = _rms_r(xv)
        dxr, dgp = _rms_bwd(xv * r, r, g_ref[...], dh)
        dx_ref[...] = dres_ref[...] + dxr

        @pl.when(pl.program_id(0) == 0)
        def _():
            dg_ref[...] = jnp.zeros_like(dg_ref)
        dg_ref[...] += dgp

        @pl.when(pl.program_id(0) == T // tm - 1)
        def _():
            for cp in _chip_swap_copies(p_refs, fc_refs, *sems):
                cp.wait()

    row = lambda i: (i, 0)
    fixed = lambda i: (0, 0)
    part = pl.BlockSpec((tm, 512), row)
    anyspec = pl.BlockSpec(memory_space=pl.ANY)
    res = pl.pallas_call(
        body, name="in_proj_bwd", grid=(T // tm,),
        in_specs=[part, part, part, part, pl.BlockSpec((tm, LANES), row), pl.BlockSpec((ZC, D), fixed),
                  pl.BlockSpec((tm, D), row), pl.BlockSpec((1, D), fixed), pl.BlockSpec((tm, D), row)] + [anyspec] * n,
        out_specs=[pl.BlockSpec((tm, D), row), pl.BlockSpec((1, D), fixed)] + [anyspec] * n,
        out_shape=[_sds((T, D)), _sds((1, D))] + [_sds((3, *p.shape[1:]), p.dtype) for p in parts],
        scratch_shapes=[pltpu.SemaphoreType.DMA((3 * n,)), pltpu.SemaphoreType.DMA((3 * n,))],
        compiler_params=_cp(("arbitrary",)),
    )(dq, dk, dv, du, df, wcat, x, g, dres, *parts)
    return res[0], res[1], res[2:]


def _stack_lanes(v):
    return jnp.concatenate([v[:, LANES * b:LANES * (b + 1)] for b in range(T // LANES)], axis=0)


def _unstack_lanes(s):
    return jnp.concatenate([s[8 * b:8 * b + 8, :] for b in range(T // LANES)], axis=1)


def _cumsum_lanes(v, reverse):
    st = _stack_lanes(v)
    ri = lax.broadcasted_iota(jnp.int32, (LANES, LANES), 0)
    ci = lax.broadcasted_iota(jnp.int32, (LANES, LANES), 1)
    tri = (ri >= ci) if reverse else (ri <= ci)
    intra = jnp.dot(st, tri.astype(F32), precision=HI, preferred_element_type=F32)
    tot = intra[:, 0:1] if reverse else intra[:, LANES - 1:LANES]
    same_head = (ci % 8) == (ri % 8)
    other = (ci // 8 > ri // 8) if reverse else (ci // 8 < ri // 8)
    off = jnp.dot((same_head & other).astype(F32), jnp.broadcast_to(tot, (LANES, LANES)), precision=HI,
                  preferred_element_type=F32)
    return _unstack_lanes(intra + off)


def _gates_fwd(f_t, b_f):
    def body(f_ref, b_ref, c_ref):
        z = f_ref[...] + b_ref[...]
        lf = jnp.minimum(z, 0.0) - jnp.log(1.0 + jnp.exp(-jnp.abs(z)))
        c_ref[...] = _cumsum_lanes(lf, reverse=False)

    return pl.pallas_call(body, name="gates_fwd", out_shape=_sds((NH, T)), compiler_params=_cp())(f_t, b_f)


def _gates_bwd(dck, f_t, b_f):
    def body(d_ref, f_ref, b_ref, df_ref, db_ref):
        z = f_ref[...] + b_ref[...]
        dlf = _cumsum_lanes(d_ref[...], reverse=True)
        df = dlf * jax.nn.sigmoid(-z)
        df_ref[...] = df
        db_ref[...] = jnp.sum(df, axis=1, keepdims=True)

    return pl.pallas_call(body, name="gates_bwd", out_shape=[_sds((NH, T)), _sds((NH, 1))],
                          compiler_params=_cp())(dck, f_t, b_f)


def _lower_triangle():
    rows = lax.broadcasted_iota(jnp.int32, (BQ, BQ), 0)
    cols = lax.broadcasted_iota(jnp.int32, (BQ, BQ), 1)
    return cols <= rows


def _logit_parts(qb, kb, ck_row, lo, hi, tri):
    parts = []
    if lo > 0:
        parts.append((_nt(qb[lo:hi], kb[0:lo]) - ck_row[:, 0:lo], 0, lo))
    parts.append((jnp.where(tri, _nt(qb[lo:hi], kb[lo:hi]) - ck_row[:, lo:hi], NEG), lo, hi))
    return parts


_HP = NH // 2


def _qkv_specs():
    return [pl.BlockSpec((T, LANES), lambda p: (0, p)), pl.BlockSpec((T, LANES), lambda p: (0, _HP + p)),
            pl.BlockSpec((T, LANES), lambda p: (0, 2 * _HP + p))]


def _attn_fwd(z, ck, g_q, g_k, shards):
    scale = DH ** -0.5
    n = len(shards)

    def body(q_ref, k_ref, v_ref, ck_ref, gq_ref, gk_ref, *rest):
        x_refs, (o_ref, lse_ref), out_refs, sems = rest[:n], rest[n:n + 2], rest[n + 2:2 * n + 2], rest[2 * n + 2:]

        @pl.when(pl.program_id(0) == 0)
        def _():
            local, remote = _gather_first_copies(x_refs, out_refs, *sems)
            for cp in local + remote:
                cp.start()

        qb, kb, vb = [], [], []
        for hh in range(2):
            cols = slice(DH * hh, DH * (hh + 1))
            qv, kv = q_ref[:, cols], k_ref[:, cols]
            qb.append((qv * _rms_r(qv) * gq_ref[...] * scale).astype(BF16))
            kb.append((kv * _rms_r(kv) * gk_ref[...]).astype(BF16))
            vb.append(v_ref[:, cols].astype(BF16))
        tri = _lower_triangle()
        for i in range(T // BQ):
            lo, hi = i * BQ, (i + 1) * BQ
            outs = []
            for hh in range(2):
                parts = _logit_parts(qb[hh], kb[hh], ck_ref[0, hh:hh + 1, :], lo, hi, tri)
                m = jnp.max(parts[-1][0], axis=-1, keepdims=True)
                if lo > 0:
                    m = jnp.maximum(m, jnp.max(parts[0][0], axis=-1, keepdims=True))
                l, o = 0.0, 0.0
                for s, k0, k1 in parts:
                    p = jnp.exp(s - m)
                    l = l + jnp.sum(p, axis=-1, keepdims=True)
                    o = o + _nn(p.astype(BF16), vb[hh][k0:k1])
                outs.append(o / l)
                lse_ref[0, lo:hi, hh:hh + 1] = m + jnp.log(l)
            o_ref[lo:hi, :] = jnp.concatenate(outs, axis=1)

        @pl.when(pl.program_id(0) == _HP - 1)
        def _():
            local, remote = _gather_first_copies(x_refs, out_refs, *sems)
            for cp in remote + local:
                cp.wait()

    fixed = lambda p: (0, 0)
    anyspec = pl.BlockSpec(memory_space=pl.ANY)
    res = pl.pallas_call(
        body, name="attn_fwd", grid=(_HP,),
        in_specs=_qkv_specs() + [pl.BlockSpec((1, 2, T), lambda p: (p, 0, 0)), pl.BlockSpec((1, DH), fixed),
                                 pl.BlockSpec((1, DH), fixed)] + [anyspec] * n,
        out_specs=[pl.BlockSpec((T, LANES), lambda p: (0, p)), pl.BlockSpec((1, T, 2), lambda p: (p, 0, 0))]
        + [anyspec] * n,
        out_shape=[_sds((T, AW)), _sds((_HP, T, 2))] + [_sds((NDEV, *s.shape), s.dtype) for s in shards],
        scratch_shapes=_gather_first_scratch(n),
        compiler_params=_cp(("arbitrary",)),
    )(z, z, z, ck, g_q, g_k, *shards)
    return res[0], res[1], res[2:]


def _attn_bwd(z, ck, g_q, g_k, lse, datt, parts):
    scale = DH ** -0.5
    n = len(parts)

    def body(q_ref, k_ref, v_ref, ck_ref, gq_ref, gk_ref, lse_ref, do_ref, *rest):
        p_refs, rest = rest[:n], rest[n:]
        dq_ref, dk_ref, dv_ref, dck_ref, dgq_ref, dgk_ref = rest[:6]
        fc_refs, (dkn_acc, dv_acc, dc_acc), sems = rest[6:6 + n], rest[6 + n:9 + n], rest[9 + n:]

        @pl.when(pl.program_id(0) == 0)
        def _():
            for cp in _chip_swap_copies(p_refs, fc_refs, *sems):
                cp.start()

        nq, nk, rq, rk, qb, kb, vb, dob = [], [], [], [], [], [], [], []
        for hh in range(2):
            cols = slice(DH * hh, DH * (hh + 1))
            qv, kv = q_ref[:, cols], k_ref[:, cols]
            rq.append(_rms_r(qv))
            rk.append(_rms_r(kv))
            nq.append(qv * rq[hh])
            nk.append(kv * rk[hh])
            qb.append((nq[hh] * gq_ref[...] * scale).astype(BF16))
            kb.append((nk[hh] * gk_ref[...]).astype(BF16))
            vb.append(v_ref[:, cols].astype(BF16))
            dob.append(do_ref[:, cols].astype(BF16))
        dkn_acc[...] = jnp.zeros_like(dkn_acc)
        dv_acc[...] = jnp.zeros_like(dv_acc)
        dc_acc[...] = jnp.zeros_like(dc_acc)
        dgq = jnp.zeros((1, DH), F32)
        tri = _lower_triangle()
        for i in range(T // BQ):
            lo, hi = i * BQ, (i + 1) * BQ
            dqs = []
            for hh in range(2):
                lse = lse_ref[0, lo:hi, hh:hh + 1]
                pd = []
                for s, k0, k1 in _logit_parts(qb[hh], kb[hh], ck_ref[0, hh:hh + 1, :], lo, hi, tri):
                    pd.append((jnp.exp(s - lse), _nt(dob[hh][lo:hi], vb[hh][k0:k1]), k0, k1))
                mean = sum(jnp.sum(p * dp, axis=-1, keepdims=True) for p, dp, _, _ in pd)
                dqn = 0.0
                for p, dp, k0, k1 in pd:
                    ds = p * (dp - mean)
                    dsb = ds.astype(BF16)
                    dv_acc[hh, k0:k1, :] += _tn(p.astype(BF16), dob[hh][lo:hi])
                    dkn_acc[hh, k0:k1, :] += _tn(dsb, qb[hh][lo:hi])
                    dc_acc[hh:hh + 1, k0:k1] -= jnp.sum(ds, axis=0, keepdims=True)
                    dqn = dqn + _nn(dsb, kb[hh][k0:k1])
                dqx, dgp = _rms_bwd(nq[hh][lo:hi], rq[hh][lo:hi], gq_ref[...], dqn * scale)
                dqs.append(dqx)
                dgq = dgq + dgp
            dq_ref[lo:hi, :] = jnp.concatenate(dqs, axis=1).astype(BF16)
        dks, dgk = [], jnp.zeros((1, DH), F32)
        for hh in range(2):
            dkx, dgp = _rms_bwd(nk[hh], rk[hh], gk_ref[...], dkn_acc[hh])
            dks.append(dkx)
            dgk = dgk + dgp
        dk_ref[...] = jnp.concatenate(dks, axis=1).astype(BF16)
        dv_ref[...] = jnp.concatenate([dv_acc[0], dv_acc[1]], axis=1).astype(BF16)
        dck_ref[0] = dc_acc[...]

        @pl.when(pl.program_id(0) == 0)
        def _():
            dgq_ref[...] = jnp.zeros_like(dgq_ref)
            dgk_ref[...] = jnp.zeros_like(dgk_ref)
        dgq_ref[...] += dgq
        dgk_ref[...] += dgk

        @pl.when(pl.program_id(0) == _HP - 1)
        def _():
            for cp in _chip_swap_copies(p_refs, fc_refs, *sems):
                cp.wait()

    fixed = lambda p: (0, 0)
    pair = pl.BlockSpec((T, LANES), lambda p: (0, p))
    gsp = pl.BlockSpec((1, DH), fixed)
    ckspec = pl.BlockSpec((1, 2, T), lambda p: (p, 0, 0))
    anyspec = pl.BlockSpec(memory_space=pl.ANY)
    res = pl.pallas_call(
        body, name="attn_bwd", grid=(_HP,),
        in_specs=_qkv_specs() + [ckspec, gsp, gsp, pl.BlockSpec((1, T, 2), lambda p: (p, 0, 0)), pair] + [anyspec] * n,
        out_specs=[pair, pair, pair, ckspec, gsp, gsp] + [anyspec] * n,
        out_shape=[_sds((T, AW), BF16)] * 3 + [_sds((_HP, 2, T)), _sds((1, DH)), _sds((1, DH))]
        + [_sds((3, *p.shape[1:]), p.dtype) for p in parts],
        scratch_shapes=[pltpu.VMEM((2, T, DH), F32), pltpu.VMEM((2, T, DH), F32), pltpu.VMEM((2, T), F32),
                        pltpu.SemaphoreType.DMA((3 * n,)), pltpu.SemaphoreType.DMA((3 * n,))],
        compiler_params=_cp(("arbitrary",)),
    )(z, z, z, ck, g_q, g_k, lse, datt, *parts)
    return (*res[:6], res[6:])


def _disc(lr, li, ls, br_t, bi_t):
    step = jnp.exp(ls)
    er = jnp.exp(lr * step)
    ab_re = er * jnp.cos(li * step)
    ab_im = er * jnp.sin(li * step)
    num_re = ab_re - 1.0
    num_im = ab_im
    den = lr * lr + li * li
    f_re = (num_re * lr + num_im * li) / den
    f_im = (num_im * lr - num_re * li) / den
    bb_re = f_re[:, None, :] * br_t - f_im[:, None, :] * bi_t
    bb_im = f_re[:, None, :] * bi_t + f_im[:, None, :] * br_t
    return ab_re, ab_im, bb_re, bb_im


def _disc_fwd(lr, li, ls, br_t, bi_t):
    def body(lr_ref, li_ref, ls_ref, br_ref, bi_ref, ar_ref, ai_ref, bbr_ref, bbi_ref):
        ar, ai, bbr, bbi = _disc(lr_ref[...], li_ref[...], ls_ref[...], br_ref[...], bi_ref[...])
        ar_ref[...] = ar
        ai_ref[...] = ai
        bbr_ref[...] = bbr
        bbi_ref[...] = bbi

    return pl.pallas_call(
        body, name="ssm_disc_fwd",
        out_shape=[_sds((NG, NP)), _sds((NG, NP)), _sds((NG, GS, NP)), _sds((NG, GS, NP))],
        compiler_params=_cp())(lr, li, ls, br_t, bi_t)


def _disc_bwd(lr, li, ls, br_t, bi_t, dar, dai, dbbr, dbbi):
    def body(lr_ref, li_ref, ls_ref, br_ref, bi_ref, dar_ref, dai_ref, dbbr_ref, dbbi_ref,
             dlr_ref, dli_ref, dls_ref, dbr_ref, dbi_ref):
        _, vjp = jax.vjp(_disc, lr_ref[...], li_ref[...], ls_ref[...], br_ref[...], bi_ref[...])
        dlr, dli, dls, dbr, dbi = vjp((dar_ref[...], dai_ref[...], dbbr_ref[...], dbbi_ref[...]))
        dlr_ref[...] = dlr
        dli_ref[...] = dli
        dls_ref[...] = dls
        dbr_ref[...] = dbr
        dbi_ref[...] = dbi

    return pl.pallas_call(
        body, name="ssm_disc_bwd",
        out_shape=[_sds((NG, NP)), _sds((NG, NP)), _sds((NG, 1)), _sds((NG, GS, NP)), _sds((NG, GS, NP))],
        compiler_params=_cp())(lr, li, ls, br_t, bi_t, dar, dai, dbbr, dbbi)


def _cmul(ar, ai, br, bi):
    return ar * br - ai * bi, ar * bi + ai * br


def _scan(buf, a_re, a_im, reverse, other=None):
    ar = [jnp.broadcast_to(a_re[:, LANES * k:LANES * (k + 1)], (NSEG, LANES)) for k in range(4)]
    ai = [jnp.broadcast_to(a_im[:, LANES * k:LANES * (k + 1)], (NSEG, LANES)) for k in range(4)]

    def tile(ref, i, k):
        return ref[pl.ds(pl.multiple_of(i * NSEG, NSEG), NSEG), LANES * k:LANES * (k + 1)]

    def advance(i, xr, xi):
        nr = [ar[k] * xr[k] - ai[k] * xi[k] + tile(buf, i, k) for k in range(4)]
        ni = [ar[k] * xi[k] + ai[k] * xr[k] + tile(buf, i, 4 + k) for k in range(4)]
        return nr, ni

    def index(i):
        return (SEG - 1 - i) if reverse else i

    zeros = tuple(jnp.zeros((NSEG, LANES), F32) for _ in range(4))

    def sweep1(i, carry):
        nr, ni = advance(index(i), carry[0], carry[1])
        return tuple(nr), tuple(ni)

    er, ei = lax.fori_loop(0, SEG, sweep1, (zeros, zeros), unroll=4)

    row = lax.broadcasted_iota(jnp.int32, (NSEG, LANES), 0)
    sr, si = [], []
    for k in range(4):
        pr, pi = ar[k], ai[k]
        for _ in range(SEG.bit_length() - 1):
            pr, pi = _cmul(pr, pi, pr, pi)
        ir, ii = er[k], ei[k]
        for d in (1, 2, 4):
            shift = (NSEG - d) if reverse else d
            ok = (row < NSEG - d) if reverse else (row >= d)
            mr, mi = _cmul(pr, pi, pltpu.roll(ir, shift, 0), pltpu.roll(ii, shift, 0))
            ir = ir + jnp.where(ok, mr, 0.0)
            ii = ii + jnp.where(ok, mi, 0.0)
            pr, pi = _cmul(pr, pi, pr, pi)
        shift = (NSEG - 1) if reverse else 1
        ok = (row < NSEG - 1) if reverse else (row >= 1)
        sr.append(jnp.where(ok, pltpu.roll(ir, shift, 0), 0.0))
        si.append(jnp.where(ok, pltpu.roll(ii, shift, 0), 0.0))

    def sweep2(i, carry):
        xr, xi, dar, dai = carry
        idx = index(i)
        if other is not None:
            orr = [tile(other, idx, k) for k in range(4)]
            oi = [tile(other, idx, 4 + k) for k in range(4)]
            dar = tuple(dar[k] + xr[k] * orr[k] + xi[k] * oi[k] for k in range(4))
            dai = tuple(dai[k] + xi[k] * orr[k] - xr[k] * oi[k] for k in range(4))
        nr, ni = advance(idx, xr, xi)
        start = pl.multiple_of(idx * NSEG, NSEG)
        for k in range(4):
            buf[pl.ds(start, NSEG), LANES * k:LANES * (k + 1)] = nr[k]
            buf[pl.ds(start, NSEG), LANES * (4 + k):LANES * (5 + k)] = ni[k]
        return tuple(nr), tuple(ni), dar, dai

    _, _, dar, dai = lax.fori_loop(0, SEG, sweep2, (tuple(sr), tuple(si), zeros, zeros), unroll=4)
    if other is None:
        return None
    da_re = jnp.concatenate([jnp.sum(d, axis=0, keepdims=True) for d in dar], axis=1)
    da_im = jnp.concatenate([jnp.sum(d, axis=0, keepdims=True) for d in dai], axis=1)
    return da_re, da_im


_SSM_BLOCKS = 4


def _ssm_fwd(u, bb, ab, cc, dsk, shards):
    n = len(shards)

    def body(u_ref, bb_ref, a_ref, cc_ref, d_ref, *rest):
        x_refs, y_ref, out_refs, xb, sems = rest[:n], rest[n], rest[n + 1:2 * n + 1], rest[2 * n + 1], rest[2 * n + 2:]

        @pl.when(pl.program_id(0) == 0)
        def _():
            local, remote = _gather_first_copies(x_refs, out_refs, *sems)
            for cp in local + remote:
                cp.start()

        ub = u_ref[...]
        xb[...] = _nn(ub.astype(BF16), bb_ref[0])
        _scan(xb, a_ref[0, 0:1, :], a_ref[0, 1:2, :], reverse=False)
        y_ref[...] = _nn(xb[...].astype(BF16), cc_ref[0]) + d_ref[0] * ub

        @pl.when(pl.program_id(0) == _SSM_BLOCKS - 1)
        def _():
            local, remote = _gather_first_copies(x_refs, out_refs, *sems)
            for cp in remote + local:
                cp.wait()

    blk = lambda j: (j, 0, 0)
    col = pl.BlockSpec((T, LANES), lambda j: (0, j))
    anyspec = pl.BlockSpec(memory_space=pl.ANY)
    res = pl.pallas_call(
        body, name="ssm_fwd", grid=(_SSM_BLOCKS,),
        in_specs=[col, pl.BlockSpec((1, LANES, 1024), blk), pl.BlockSpec((1, 2, 512), blk),
                  pl.BlockSpec((1, 1024, LANES), blk), pl.BlockSpec((1, 1, LANES), blk)] + [anyspec] * n,
        out_specs=[col] + [anyspec] * n,
        out_shape=[_sds((T, SW))] + [_sds((NDEV, *s.shape), s.dtype) for s in shards],
        scratch_shapes=[pltpu.VMEM((T, 1024), F32)] + _gather_first_scratch(n),
        compiler_params=_cp(("arbitrary",)),
    )(u, bb, ab, cc, dsk, *shards)
    return res[0], res[1:]


def _ssm_bwd(u, dy, bb, ab, cc, dsk):
    def body(u_ref, dy_ref, bb_ref, a_ref, cc_ref, d_ref, du_ref, dbb_ref, da_ref, dcc_ref, dd_ref, xb, gb):
        ub, dyv = u_ref[...], dy_ref[...]
        ubb, dyb = ub.astype(BF16), dyv.astype(BF16)
        a_re, a_im = a_ref[0, 0:1, :], a_ref[0, 1:2, :]
        xb[...] = _nn(ubb, bb_ref[0])
        _scan(xb, a_re, a_im, reverse=False)
        dcc_ref[0] = _tn(xb[...].astype(BF16), dyb)
        gb[...] = _nt(dyb, cc_ref[0])
        da_re, da_im = _scan(gb, a_re, -a_im, reverse=True, other=xb)
        da_ref[0] = jnp.concatenate([da_re, da_im], axis=0)
        gc = gb[...].astype(BF16)
        du_ref[...] = _nt(gc, bb_ref[0]) + d_ref[0] * dyv
        dbb_ref[0] = _tn(ubb, gc)
        dd_ref[0] = jnp.sum(dyv * ub, axis=0, keepdims=True)

    blk = lambda j: (j, 0, 0)
    col = pl.BlockSpec((T, LANES), lambda j: (0, j))
    return pl.pallas_call(
        body, name="ssm_bwd", grid=(_SSM_BLOCKS,),
        in_specs=[col, col, pl.BlockSpec((1, LANES, 1024), blk), pl.BlockSpec((1, 2, 512), blk),
                  pl.BlockSpec((1, 1024, LANES), blk), pl.BlockSpec((1, 1, LANES), blk)],
        out_specs=[col, pl.BlockSpec((1, LANES, 1024), blk), pl.BlockSpec((1, 2, 512), blk),
                   pl.BlockSpec((1, 1024, LANES), blk), pl.BlockSpec((1, 1, LANES), blk)],
        out_shape=[_sds((T, SW)), _sds((4, LANES, 1024)), _sds((4, 2, 512)), _sds((4, 1024, LANES)), _sds((4, 1, LANES))],
        scratch_shapes=[pltpu.VMEM((T, 1024), F32), pltpu.VMEM((T, 1024), F32)],
        compiler_params=_cp(("parallel",)),
    )(u, dy, bb, ab, cc, dsk)


def _interleave(a):
    return a.reshape(NSEG, SEG, a.shape[1]).transpose(1, 0, 2).reshape(a.shape)


def _deinterleave(a):
    return a.reshape(SEG, NSEG, a.shape[1]).transpose(1, 0, 2).reshape(a.shape)


_TM_MIX = 256


def _mix_parts(att, y, wg, bg):
    y2 = jax.nn.gelu(y)
    y2b = y2.astype(BF16)
    sg = jax.nn.sigmoid(_nn(y2b, wg) + bg)
    ssm = y2 * sg
    ra, rs = _rms_r(att), _rms_r(ssm)
    return y2, y2b, sg, ssm, ra, rs


def _mix_fwd(att, y, x, wg, bg, ga, gs, wo):
    def body(att_ref, y_ref, x_ref, wg_ref, bg_ref, ga_ref, gs_ref, wo_ref, x1_ref):
        attv = att_ref[...]
        _, _, _, ssm, ra, rs = _mix_parts(attv, y_ref[...], wg_ref[...], bg_ref[...])
        ma = (attv * ra * ga_ref[...]).astype(BF16)
        ms = (ssm * rs * gs_ref[...]).astype(BF16)
        x1_ref[...] = x_ref[...] + _nn(ma, wo_ref[0:AW, :]) + _nn(ms, wo_ref[AW:D, :])

    row = lambda i: (i, 0)
    fixed = lambda i: (0, 0)
    half = pl.BlockSpec((_TM_MIX, AW), row)
    vec = pl.BlockSpec((1, AW), fixed)
    return pl.pallas_call(
        body, name="mix_fwd", grid=(T // _TM_MIX,),
        in_specs=[half, half, pl.BlockSpec((_TM_MIX, D), row), pl.BlockSpec((SW, SW), fixed), vec, vec, vec,
                  pl.BlockSpec((D, D), fixed)],
        out_specs=pl.BlockSpec((_TM_MIX, D), row), out_shape=_sds((T, D)),
        compiler_params=_cp(("parallel",)),
    )(att, y, x, wg, bg, ga, gs, wo)


def _mix_bwd(att, y, dx1, wg, bg, ga, gs, wo):
    last = T // _TM_MIX - 1

    def body(att_ref, y_ref, d_ref, wg_ref, bg_ref, ga_ref, gs_ref, wo_ref,
             datt_ref, dy_ref, dwg_ref, dbg_ref, dga_ref, dgs_ref, dwo_ref, dwg_acc, dwo_acc):
        attv, yv, db = att_ref[...], y_ref[...], d_ref[...].astype(BF16)
        y2, y2b, sg, ssm, ra, rs = _mix_parts(attv, yv, wg_ref[...], bg_ref[...])
        na, ns = attv * ra, ssm * rs
        ma = (na * ga_ref[...]).astype(BF16)
        ms = (ns * gs_ref[...]).astype(BF16)
        dma = _nt(db, wo_ref[0:AW, :])
        dms = _nt(db, wo_ref[AW:D, :])
        datt, dga = _rms_bwd(na, ra, ga_ref[...], dma)
        dssm, dgs = _rms_bwd(ns, rs, gs_ref[...], dms)
        dgl = dssm * y2 * sg * (1.0 - sg)
        dglb = dgl.astype(BF16)
        dy2 = dssm * sg + _nt(dglb, wg_ref[...])
        _, gelu_vjp = jax.vjp(jax.nn.gelu, yv)
        datt_ref[...] = datt
        dy_ref[...] = gelu_vjp(dy2)[0]

        @pl.when(pl.program_id(0) == 0)
        def _():
            dwg_acc[...] = jnp.zeros_like(dwg_acc)
            dwo_acc[...] = jnp.zeros_like(dwo_acc)
            dbg_ref[...] = jnp.zeros_like(dbg_ref)
            dga_ref[...] = jnp.zeros_like(dga_ref)
            dgs_ref[...] = jnp.zeros_like(dgs_ref)
        dwg_acc[...] += _tn(y2b, dglb)
        dbg_ref[...] += jnp.sum(dgl, axis=0, keepdims=True)
        dga_ref[...] += dga
        dgs_ref[...] += dgs
        dwo_acc[0:AW, :] += _tn(ma, db)
        dwo_acc[AW:D, :] += _tn(ms, db)

        @pl.when(pl.program_id(0) == last)
        def _():
            dwg_ref[...] = dwg_acc[...].astype(BF16)
            dwo_ref[...] = dwo_acc[...].astype(BF16)

    row = lambda i: (i, 0)
    fixed = lambda i: (0, 0)
    half = pl.BlockSpec((_TM_MIX, AW), row)
    vec = pl.BlockSpec((1, AW), fixed)
    sq = pl.BlockSpec((SW, SW), fixed)
    big = pl.BlockSpec((D, D), fixed)
    return pl.pallas_call(
        body, name="mix_bwd", grid=(T // _TM_MIX,),
        in_specs=[half, half, pl.BlockSpec((_TM_MIX, D), row), sq, vec, vec, vec, big],
        out_specs=[half, half, sq, vec, vec, vec, big],
        out_shape=[_sds((T, AW)), _sds((T, SW)), _sds((SW, SW), BF16), _sds((1, SW)), _sds((1, AW)), _sds((1, SW)),
                   _sds((D, D), BF16)],
        scratch_shapes=[pltpu.VMEM((SW, SW), F32), pltpu.VMEM((D, D), F32)],
        compiler_params=_cp(("arbitrary",)),
    )(att, y, dx1, wg, bg, ga, gs, wo)


_NCB = -(-CS // LANES)


def _ffn_up(x1, g, wu4):
    tm = 512

    def body(x_ref, g_ref, w_ref, h_ref, up_ref):
        @pl.when(pl.program_id(1) == 0)
        def _():
            xv = x_ref[...]
            h_ref[...] = (xv * _rms_r(xv) * g_ref[...]).astype(BF16)
        up_ref[0, 0] = _nt(h_ref[...], w_ref[0, 0])

    return pl.pallas_call(
        body, name="ffn_up", grid=(T // tm, NDEV),
        in_specs=[pl.BlockSpec((tm, D), lambda i, e: (i, 0)), pl.BlockSpec((1, D), lambda i, e: (0, 0)),
                  pl.BlockSpec((1, 1, CS, D), lambda i, e: (e // 4, e % 4, 0, 0))],
        out_specs=[pl.BlockSpec((tm, D), lambda i, e: (i, 0)),
                   pl.BlockSpec((1, 1, tm, CS), lambda i, e: (e // 4, e % 4, i, 0))],
        out_shape=[_sds((T, D), BF16), _sds((2, 4, T, CS))],
        compiler_params=_cp(("parallel", "arbitrary")),
    )(x1, g, wu4)


def _shift_down(h, k):
    row = lax.broadcasted_iota(jnp.int32, h.shape, 0)
    return jnp.where(row >= k, pltpu.roll(h, k, 0), 0.0)


def _shift_up(h, k):
    row = lax.broadcasted_iota(jnp.int32, h.shape, 0)
    return jnp.where(row < T - k, pltpu.roll(h, T - k, 0), 0.0)


def _conv(h, w, b):
    return w[0:1, :] * _shift_down(h, 2) + w[1:2, :] * _shift_down(h, 1) + w[2:3, :] * h + b


def _chan(gv, rows):
    return pl.BlockSpec((1, 1, rows, LANES), lambda d, j: (gv, d, 0, j))


def _ffn_act(up4, cw4, cb4):
    def body(ug_ref, uv_ref, wg_ref, wv_ref, bg_ref, bv_ref, act_ref):
        g = _conv(ug_ref[0, 0], wg_ref[0, 0], bg_ref[0, 0])
        v = _conv(uv_ref[0, 0], wv_ref[0, 0], bv_ref[0, 0])
        act_ref[0] = (g * jax.nn.sigmoid(g) * v).astype(BF16)

    return pl.pallas_call(
        body, name="ffn_act", grid=(4, _NCB),
        in_specs=[_chan(0, T), _chan(1, T), _chan(0, 3), _chan(1, 3), _chan(0, 1), _chan(1, 1)],
        out_specs=pl.BlockSpec((1, T, LANES), lambda d, j: (d, 0, j)), out_shape=_sds((4, T, CS), BF16),
        compiler_params=_cp(("parallel", "parallel")),
    )(up4, up4, cw4, cw4, cb4, cb4)


def _ffn_act_bwd(up4, dact, cw4, cb4):
    def conv_bwd(h, w, d):
        dup = w[2:3, :] * d + w[1:2, :] * _shift_up(d, 1) + w[0:1, :] * _shift_up(d, 2)
        dw = jnp.concatenate([jnp.sum(d * _shift_down(h, 2), axis=0, keepdims=True),
                              jnp.sum(d * _shift_down(h, 1), axis=0, keepdims=True),
                              jnp.sum(d * h, axis=0, keepdims=True)], axis=0)
        return dup, dw, jnp.sum(d, axis=0, keepdims=True)

    def body(ug_ref, uv_ref, da_ref, wg_ref, wv_ref, bg_ref, bv_ref, dup_ref, dw_ref, db_ref):
        ug, uv, da = ug_ref[0, 0], uv_ref[0, 0], da_ref[0]
        g = _conv(ug, wg_ref[0, 0], bg_ref[0, 0])
        v = _conv(uv, wv_ref[0, 0], bv_ref[0, 0])
        sg = jax.nn.sigmoid(g)
        dg = da * v * (sg * (1.0 + g * (1.0 - sg)))
        dv = da * (g * sg)
        dug, dwg, dbg = conv_bwd(ug, wg_ref[0, 0], dg)
        duv, dwv, dbv = conv_bwd(uv, wv_ref[0, 0], dv)
        dup_ref[0, 0] = dug.astype(BF16)
        dup_ref[1, 0] = duv.astype(BF16)
        dw_ref[0, 0] = dwg
        dw_ref[1, 0] = dwv
        db_ref[0, 0] = dbg
        db_ref[1, 0] = dbv

    both = lambda rows: pl.BlockSpec((2, 1, rows, LANES), lambda d, j: (0, d, 0, j))
    return pl.pallas_call(
        body, name="ffn_act_bwd", grid=(4, _NCB),
        in_specs=[_chan(0, T), _chan(1, T), pl.BlockSpec((1, T, LANES), lambda d, j: (d, 0, j)),
                  _chan(0, 3), _chan(1, 3), _chan(0, 1), _chan(1, 1)],
        out_specs=[both(T), both(3), both(1)],
        out_shape=[_sds((2, 4, T, CS), BF16), _sds((2, 4, 3, CS)), _sds((2, 4, 1, CS))],
        compiler_params=_cp(("parallel", "parallel")),
    )(up4, up4, dact, cw4, cw4, cb4, cb4)


def _ffn_down_loss(act, wd4, x1, target):
    tm = 256

    def body(a_ref, w_ref, x1_ref, t_ref, loss_ref, dx_ref, dxb_ref):
        x2 = x1_ref[...]
        for d in range(4):
            x2 = x2 + _nn(a_ref[d], w_ref[d])
        err = x2 - t_ref[...]
        dx = err * (1.0 / D)
        dx_ref[...] = dx
        dxb_ref[...] = dx.astype(BF16)

        @pl.when(pl.program_id(0) == 0)
        def _():
            loss_ref[...] = jnp.zeros_like(loss_ref)
        loss_ref[...] += 0.5 * jnp.sum(jnp.mean(err * err, axis=-1, keepdims=True), axis=0, keepdims=True)

    row = lambda i: (i, 0)
    fixed = lambda i: (0, 0)
    return pl.pallas_call(
        body, name="ffn_down_loss", grid=(T // tm,),
        in_specs=[pl.BlockSpec((4, tm, CS), lambda i: (0, i, 0)), pl.BlockSpec((4, CS, D), lambda i: (0, 0, 0)),
                  pl.BlockSpec((tm, D), row), pl.BlockSpec((tm, D), row)],
        out_specs=[pl.BlockSpec((1, 1), fixed), pl.BlockSpec((tm, D), row), pl.BlockSpec((tm, D), row)],
        out_shape=[_sds((1, 1)), _sds((T, D)), _sds((T, D), BF16)],
        compiler_params=_cp(("arbitrary",)),
    )(act, wd4, x1, target)


def _ffn_dact(dx2b, wd4):
    tm = 512

    def body(d_ref, w_ref, o_ref):
        o_ref[0] = _nt(d_ref[...], w_ref[0])

    return pl.pallas_call(
        body, name="ffn_dact", grid=(4, T // tm),
        in_specs=[pl.BlockSpec((tm, D), lambda d, i: (i, 0)), pl.BlockSpec((1, CS, D), lambda d, i: (d, 0, 0))],
        out_specs=pl.BlockSpec((1, tm, CS), lambda d, i: (d, i, 0)), out_shape=_sds((4, T, CS)),
        compiler_params=_cp(("parallel", "parallel")),
    )(dx2b, wd4)


def _ffn_dwd(act, dx2b):
    def body(a_ref, d_ref, o_ref):
        o_ref[0] = _tn(a_ref[0], d_ref[...]).astype(BF16)

    return pl.pallas_call(
        body, name="ffn_dwd", grid=(4,),
        in_specs=[pl.BlockSpec((1, T, CS), lambda d: (d, 0, 0)), pl.BlockSpec((T, D), lambda d: (0, 0))],
        out_specs=pl.BlockSpec((1, CS, D), lambda d: (d, 0, 0)), out_shape=_sds((4, CS, D), BF16),
        compiler_params=_cp(("parallel",)),
    )(act, dx2b)


def _ffn_dwu(h2, dup4):
    tn = 512

    def body(h_ref, d_ref, o_ref):
        o_ref[0, 0] = _tn(d_ref[0, 0], h_ref[...]).astype(BF16)

    return pl.pallas_call(
        body, name="ffn_dwu", grid=(NDEV, D // tn),
        in_specs=[pl.BlockSpec((T, tn), lambda e, i: (0, i)),
                  pl.BlockSpec((1, 1, T, CS), lambda e, i: (e // 4, e % 4, 0, 0))],
        out_specs=pl.BlockSpec((1, 1, CS, tn), lambda e, i: (e // 4, e % 4, 0, i)),
        out_shape=_sds((2, 4, CS, D), BF16),
        compiler_params=_cp(("parallel", "parallel")),
    )(h2, dup4)


def _ffn_up_bwd(dup4, wu4, x1, g, dres):
    tm = 256

    def body(d_ref, w_ref, x_ref, g_ref, dres_ref, dx_ref, dg_ref):
        dh = jnp.zeros((tm, D), F32)
        for gv in range(2):
            for d in range(4):
                dh = dh + _nn(d_ref[gv, d], w_ref[gv, d])
        xv = x_ref[...]
        r = _rms_r(xv)
        dxr, dgp = _rms_bwd(xv * r, r, g_ref[...], dh)
        dx_ref[...] = dres_ref[...] + dxr

        @pl.when(pl.program_id(0) == 0)
        def _():
            dg_ref[...] = jnp.zeros_like(dg_ref)
        dg_ref[...] += dgp

    row = lambda i: (i, 0)
    fixed = lambda i: (0, 0)
    return pl.pallas_call(
        body, name="ffn_up_bwd", grid=(T // tm,),
        in_specs=[pl.BlockSpec((2, 4, tm, CS), lambda i: (0, 0, i, 0)), pl.BlockSpec((2, 4, CS, D), lambda i: (0, 0, 0, 0)),
                  pl.BlockSpec((tm, D), row), pl.BlockSpec((1, D), fixed), pl.BlockSpec((tm, D), row)],
        out_specs=[pl.BlockSpec((tm, D), row), pl.BlockSpec((1, D), fixed)],
        out_shape=[_sds((T, D)), _sds((1, D))],
        compiler_params=_cp(("arbitrary",)),
    )(dup4, wu4, x1, g, dres)


def _adamw_math(w, g, m, v):
    m = ADAM_B1 * m + (1.0 - ADAM_B1) * g
    v = ADAM_B2 * v + (1.0 - ADAM_B2) * jnp.square(g)
    m_hat = m / (1.0 - ADAM_B1 ** ADAM_STEP)
    v_hat = v / (1.0 - ADAM_B2 ** ADAM_STEP)
    delta = -ADAM_LR * (m_hat / (jnp.sqrt(v_hat) + ADAM_EPS) + ADAM_WD * w)
    return delta, m, v


def _where_am_i():
    x, y, c = _place()
    return jnp.stack([c, 2 * x + y]).astype(jnp.int32)


def _pair_sum(a4, recv, name):
    _, _, rows, cols = a4.shape
    tr = _row_tile(rows, cols, 3 << 20)

    def body(sel_ref, own_ref, recv_ref, out_ref):
        out_ref[...] = (own_ref[0].astype(F32) + recv_ref[...].astype(F32)).astype(out_ref.dtype)

    grid_spec = pltpu.PrefetchScalarGridSpec(
        num_scalar_prefetch=1, grid=(4, rows // tr),
        in_specs=[pl.BlockSpec((1, 1, tr, cols), lambda k, i, s: (k, s[0], i, 0)),
                  pl.BlockSpec((1, tr, cols), lambda k, i, s: (k, i, 0))],
        out_specs=pl.BlockSpec((1, tr, cols), lambda k, i, s: (k, i, 0)),
    )
    return pl.pallas_call(
        body, name=name, grid_spec=grid_spec, out_shape=_sds((4, rows, cols), a4.dtype),
        compiler_params=_cp(("arbitrary", "arbitrary")),
    )(_where_am_i(), a4, recv)


def _adamw_rs(a4, recv, from_chips, w, m, v, name):
    rows, cols = w.shape
    tr = _row_tile(rows, cols, 3 << 19)

    def body(sel_ref, own_ref, recv_ref, fc_ref, w_ref, m_ref, v_ref, g_ref, d_ref, nm_ref, nv_ref):
        g = own_ref[0, 0].astype(F32) + recv_ref[0].astype(F32)
        for j in range(3):
            g = g + fc_ref[j].astype(F32)
        d, nm, nv = _adamw_math(w_ref[...], g, m_ref[...], v_ref[...])
        g_ref[...] = g
        d_ref[...] = d
        nm_ref[...] = nm
        nv_ref[...] = nv

    flat = pl.BlockSpec((tr, cols), lambda i, s: (i, 0))
    grid_spec = pltpu.PrefetchScalarGridSpec(
        num_scalar_prefetch=1, grid=(rows // tr,),
        in_specs=[pl.BlockSpec((1, 1, tr, cols), lambda i, s: (s[1], s[0], i, 0)),
                  pl.BlockSpec((1, tr, cols), lambda i, s: (s[1], i, 0)),
                  pl.BlockSpec((3, tr, cols), lambda i, s: (0, i, 0)), flat, flat, flat],
        out_specs=[flat] * 4,
    )
    return pl.pallas_call(
        body, name=name, grid_spec=grid_spec, out_shape=[_sds((rows, cols))] * 4,
        compiler_params=_cp(("arbitrary",)),
    )(_where_am_i(), a4, recv, from_chips, w, m, v)


def _adamw_small(gall, gall_b, w, m, v):
    rows = w.shape[0]
    rf = rows - gall_b.shape[1]

    def body(ga_ref, gb_ref, w_ref, m_ref, v_ref, g_ref, loss_ref, d_ref, nm_ref, nv_ref):
        ga, gb = ga_ref[0], gb_ref[0].astype(F32)
        for dev in range(1, NDEV):
            ga = ga + ga_ref[dev]
            gb = gb + gb_ref[dev].astype(F32)
        g = jnp.concatenate([ga[0:rf], gb], axis=0)
        d, nm, nv = _adamw_math(w_ref[...], g, m_ref[...], v_ref[...])
        g_ref[...] = g
        loss_ref[...] = ga[rf:rf + 8]
        d_ref[...] = d
        nm_ref[...] = nm
        nv_ref[...] = nv

    return pl.pallas_call(body, name="adamw_small",
                          out_shape=[_sds((rows, LANES)), _sds((8, LANES))] + [_sds((rows, LANES))] * 3,
                          compiler_params=_cp())(gall, gall_b, w, m, v)


def _rows128(a):
    n = a.shape[0]
    r = (-n) % (8 * LANES)
    if r:
        a = jnp.pad(a, [(0, r)])
    return a.reshape(-1, LANES)


_SMALL = (("g_mix", D), ("b_f", NH), ("g_q", DH), ("g_k", DH), ("lambda_re", NG * NP), ("lambda_im", NG * NP),
          ("log_step", NG), ("d_skip", NG * GS), ("b_glu", SW), ("g_attn_out", AW), ("g_ssm_out", SW),
          ("g_ffn", D), ("conv_b", 2 * DFF), ("b_re", NG * NP * GS), ("b_im", NG * NP * GS), ("c_re", NG * GS * NP),
          ("c_im", NG * GS * NP))
_SMALL_BF16_ROWS = 4 * NG * NP * GS // LANES


def _pack_small(d):
    return jnp.concatenate([_rows128(d[name].reshape(-1).astype(F32)) for name, _ in _SMALL], axis=0)


def _unpack_small(p, shapes):
    out, off = {}, 0
    for name, n in _SMALL:
        rows = -(-n // (8 * LANES)) * 8
        out[name] = p[off:off + rows].reshape(-1)[:n].reshape(shapes[name])
        off += rows
    return out


def _block_diag(m):
    eye = jnp.eye(8, dtype=m.dtype)
    r, c = m.shape[2], m.shape[3]
    return (m[:, :, :, None, :] * eye[None, :, None, :, None]).reshape(4, 8 * r, 8 * c)


def _diag_blocks(dense, r, c):
    eye = jnp.eye(8, dtype=dense.dtype)
    return jnp.sum(dense.reshape(4, 8, r, 8, c) * eye[None, :, None, :, None], axis=3)


def kernel(x, g_mix, w_in, b_f, g_q, g_k, lambda_re, lambda_im, log_step, b_re, b_im, c_re, c_im, d_skip, w_glu, b_glu, g_attn_out, g_ssm_out, w_out, g_ffn, w_up, conv_w, conv_b, w_down, loss_target, m_g_mix, m_w_in, m_b_f, m_g_q, m_g_k, m_lambda_re, m_lambda_im, m_log_step, m_b_re, m_b_im, m_c_re, m_c_im, m_d_skip, m_w_glu, m_b_glu, m_g_attn_out, m_g_ssm_out, m_w_out, m_g_ffn, m_w_up, m_conv_w, m_conv_b, m_w_down, v_g_mix, v_w_in, v_b_f, v_g_q, v_g_k, v_lambda_re, v_lambda_im, v_log_step, v_b_re, v_b_im, v_c_re, v_c_im, v_d_skip, v_w_glu, v_b_glu, v_g_attn_out, v_g_ssm_out, v_w_out, v_g_ffn, v_w_up, v_conv_w, v_conv_b, v_w_down):
    weights = dict(g_mix=g_mix, w_in=w_in, b_f=b_f, g_q=g_q, g_k=g_k, lambda_re=lambda_re, lambda_im=lambda_im,
                   log_step=log_step, b_re=b_re, b_im=b_im, c_re=c_re, c_im=c_im, d_skip=d_skip, w_glu=w_glu,
                   b_glu=b_glu, g_attn_out=g_attn_out, g_ssm_out=g_ssm_out, w_out=w_out, g_ffn=g_ffn, w_up=w_up,
                   conv_w=conv_w, conv_b=conv_b, w_down=w_down)
    mom_m = dict(g_mix=m_g_mix, w_in=m_w_in, b_f=m_b_f, g_q=m_g_q, g_k=m_g_k, lambda_re=m_lambda_re,
                 lambda_im=m_lambda_im, log_step=m_log_step, b_re=m_b_re, b_im=m_b_im, c_re=m_c_re, c_im=m_c_im,
                 d_skip=m_d_skip, w_glu=m_w_glu, b_glu=m_b_glu, g_attn_out=m_g_attn_out, g_ssm_out=m_g_ssm_out,
                 w_out=m_w_out, g_ffn=m_g_ffn, w_up=m_w_up, conv_w=m_conv_w, conv_b=m_conv_b, w_down=m_w_down)
    mom_v = dict(g_mix=v_g_mix, w_in=v_w_in, b_f=v_b_f, g_q=v_g_q, g_k=v_g_k, lambda_re=v_lambda_re,
                 lambda_im=v_lambda_im, log_step=v_log_step, b_re=v_b_re, b_im=v_b_im, c_re=v_c_re, c_im=v_c_im,
                 d_skip=v_d_skip, w_glu=v_w_glu, b_glu=v_b_glu, g_attn_out=v_g_attn_out, g_ssm_out=v_g_ssm_out,
                 w_out=v_w_out, g_ffn=v_g_ffn, w_up=v_w_up, conv_w=v_conv_w, conv_b=v_conv_b, w_down=v_w_down)
    names = list(weights)
    big = ("w_in", "w_glu", "w_out", "w_up", "w_down", "conv_w")

    xs = x[0]
    target = loss_target[0]
    row = lambda a: a.reshape(1, -1)

    (win_g,) = _all_gather([w_in.T.astype(BF16)], "gather_w_in")
    w_in_t = win_g.reshape(8 * 257, D)
    wcat = jnp.concatenate([w_in_t[0:1536], w_in_t[1544:2056], jnp.pad(w_in_t[1536:1544], ((0, LANES - NH), (0, 0)))],
                           axis=0)
    cb4 = conv_b.reshape(2, 4, 1, CS)

    h1, z, f_t, (wg_g, wo_g) = _in_proj(xs, row(g_mix), wcat, [w_glu.astype(BF16), w_out.astype(BF16)])
    bf_col = b_f.reshape(NH, 1)
    ck = _gates_fwd(f_t, bf_col).reshape(_HP, 2, T)
    att, lse, (wu_g, cw_g) = _attn_fwd(z, ck, row(g_q), row(g_k), [w_up.T.astype(BF16), conv_w])

    ls_col = log_step.reshape(NG, 1)
    br_t, bi_t = b_re.transpose(0, 2, 1), b_im.transpose(0, 2, 1)
    ab_re, ab_im, bb_re, bb_im = _disc_fwd(lambda_re, lambda_im, ls_col, br_t, bi_t)
    bb = jnp.concatenate([_block_diag(bb_re.reshape(4, 8, GS, NP)), _block_diag(bb_im.reshape(4, 8, GS, NP))],
                         axis=2).astype(BF16)
    ab = jnp.stack([ab_re.reshape(4, 512), ab_im.reshape(4, 512)], axis=1)
    cdiag = lambda cm: _block_diag(cm.reshape(4, 8, GS, NP).transpose(0, 1, 3, 2))
    cc = jnp.concatenate([cdiag(c_re), -cdiag(c_im)], axis=1).astype(BF16)
    dsk = d_skip.reshape(4, 1, LANES)
    u_il = _interleave(z[:, 1536:2048])
    y_il, (wd_g,) = _ssm_fwd(u_il, bb, ab, cc, dsk, [w_down.astype(BF16)])
    y = _deinterleave(y_il)
    wg_g, wo_g, wu_g, cw_g, wd_g = _gather_second([wg_g, wo_g, wu_g, cw_g, wd_g], "gather_pass_on")
    wg_f = wg_g.reshape(SW, SW)
    wo_f = wo_g.reshape(D, D)
    wu4 = wu_g.reshape(2, 4, CS, D)
    wd4 = wd_g.reshape(4, CS, D)
    cw4 = cw_g.reshape(2, 4, 3, CS)

    x1 = _mix_fwd(att, y, xs, wg_f, row(b_glu), row(g_attn_out), row(g_ssm_out), wo_f)
    h2, up4 = _ffn_up(x1, row(g_ffn), wu4)
    act = _ffn_act(up4, cw4, cb4)
    loss_dev, dx2, dx2b = _ffn_down_loss(act, wd4, x1, target)

    dact = _ffn_dact(dx2b, wd4)
    d_wd = _ffn_dwd(act, dx2b)
    dup4, dcw4, dcb4 = _ffn_act_bwd(up4, dact, cw4, cb4)
    d_wu = _ffn_dwu(h2, dup4)
    dx1, dg_ffn = _ffn_up_bwd(dup4, wu4, x1, row(g_ffn), dx2)
    datt, dy, d_wg, d_bg, d_ga, d_gs, d_wo = _mix_bwd(att, y, dx1, wg_f, row(b_glu), row(g_attn_out),
                                                       row(g_ssm_out), wo_f)
    early = ("w_up", "w_down", "conv_w", "w_out", "w_glu")
    by_dest = {"w_up": d_wu.reshape(4, 2, CS, D), "w_down": d_wd.reshape(4, 2, DFF // NDEV, D),
               "conv_w": dcw4.reshape(4, 2, 3, CS), "w_out": d_wo.reshape(4, 2, D // NDEV, D),
               "w_glu": d_wg.reshape(4, 2, SW // NDEV, SW)}
    from_sibling = dict(zip(early, _swap_sibling([by_dest[n] for n in early], "rs_pair_swap")))
    chip_sums = {n: _pair_sum(by_dest[n], from_sibling[n], "rs_pair_sum_" + n) for n in early}
    du_il, dbb, dab, dcc, ddsk = _ssm_bwd(u_il, _interleave(dy), bb, ab, cc, dsk)
    du = _deinterleave(du_il).astype(BF16)
    dbb_re = _diag_blocks(dbb[:, :, 0:512], GS, NP).reshape(NG, GS, NP)
    dbb_im = _diag_blocks(dbb[:, :, 512:1024], GS, NP).reshape(NG, GS, NP)
    dlr, dli, dls, dbr_t, dbi_t = _disc_bwd(lambda_re, lambda_im, ls_col, br_t, bi_t,
                                            dab[:, 0].reshape(NG, NP), dab[:, 1].reshape(NG, NP), dbb_re, dbb_im)
    d_cre = _diag_blocks(dcc[:, 0:512], NP, GS).transpose(0, 1, 3, 2).reshape(NG, GS, NP)
    d_cim = -_diag_blocks(dcc[:, 512:1024], NP, GS).transpose(0, 1, 3, 2).reshape(NG, GS, NP)

    dq, dk, dv, dck, d_gq, d_gk, fc = _attn_bwd(z, ck, row(g_q), row(g_k), lse, datt, [chip_sums[n] for n in early])
    from_chips = dict(zip(early, fc))
    df_t, d_bf = _gates_bwd(dck.reshape(NH, T), f_t, bf_col)
    df = jnp.concatenate([df_t.T, jnp.zeros((T, LANES - NH), F32)], axis=1).astype(BF16)
    d_wcat = _in_proj_dw(h1, dq, dk, dv, du, df)
    d_win = jnp.concatenate([d_wcat[0:1536], d_wcat[2048:2048 + NH], d_wcat[1536:2048]], axis=0)
    by_dest["w_in"] = d_win.astype(BF16).reshape(4, 2, 257, D)
    (from_sibling["w_in"],) = _swap_sibling([by_dest["w_in"]], "rs_pair_swap_w_in")
    chip_sums["w_in"] = _pair_sum(by_dest["w_in"], from_sibling["w_in"], "rs_pair_sum_w_in")
    grad_x, dg_mix, (from_chips["w_in"],) = _in_proj_bwd(dq, dk, dv, du, df, wcat, xs, row(g_mix), dx1, [chip_sums["w_in"]])

    grads, deltas, new_m, new_v = {}, {}, {}, {}
    for n in early + ("w_in",):
        flip = (lambda a: a.T) if n in ("w_up", "w_in") else (lambda a: a)
        outs = _adamw_rs(by_dest[n], from_sibling[n], from_chips[n], flip(weights[n]), flip(mom_m[n]), flip(mom_v[n]),
                         "adamw_" + n)
        grads[n], deltas[n], new_m[n], new_v[n] = (flip(o) for o in outs)

    small = dict(g_mix=dg_mix, b_f=d_bf, g_q=d_gq, g_k=d_gk, lambda_re=dlr, lambda_im=dli, log_step=dls,
                 b_re=dbr_t.transpose(0, 2, 1), b_im=dbi_t.transpose(0, 2, 1), c_re=d_cre, c_im=d_cim, d_skip=ddsk,
                 b_glu=d_bg, g_attn_out=d_ga, g_ssm_out=d_gs, g_ffn=dg_ffn, conv_b=dcb4)
    loss_tile = jnp.pad(loss_dev, ((0, 7), (0, LANES - 1)))
    small_p = _pack_small(small)
    rf = small_p.shape[0] - _SMALL_BF16_ROWS
    small_all, small_all_b = _all_gather([jnp.concatenate([small_p[0:rf], loss_tile], axis=0), small_p[rf:].astype(BF16)],
                                         "gather_small_grads")
    sg_p, loss_p, sd_p, sm_p, sv_p = _adamw_small(small_all, small_all_b, _pack_small({n: weights[n] for n in small}),
                                                  _pack_small({n: mom_m[n] for n in small}),
                                                  _pack_small({n: mom_v[n] for n in small}))
    loss = loss_p[0, 0]
    shapes = {n: weights[n].shape for n in small}
    sg, sd, sm, sv = (_unpack_small(p, shapes) for p in (sg_p, sd_p, sm_p, sv_p))
    for n in small:
        grads[n], deltas[n], new_m[n], new_v[n] = sg[n], sd[n], sm[n], sv[n]

    return (loss, grad_x[None], *[grads[n] for n in names], *[deltas[n] for n in names],
            *[new_m[n] for n in names], *[new_v[n] for n in names])
```

```python
import jax
import jax.numpy as jnp
from jax import lax
from jax.experimental import pallas as pl
from jax.experimental.pallas import tpu as pltpu

F32, BF16 = jnp.float32, jnp.bfloat16
T, D = 2048, 1024
NH, DH = 8, 64
AW, SW = 512, 512
NG, GS, NP = 32, 16, 64
DFF = 2816
NDEV = 8
CS = 2 * DFF // NDEV
EPS = 1e-6
NEG = -1e30
ZC = 4 * 512 + 128
SEG = 256
NSEG = T // SEG
BQ = 256
VMEM_LIMIT = 56 * 1024 * 1024
LANES = 128
MESH = pl.DeviceIdType.MESH
HI = lax.Precision.HIGHEST

ADAM_LR, ADAM_B1, ADAM_B2, ADAM_EPS, ADAM_WD, ADAM_STEP = 0.001, 0.9, 0.999, 1e-08, 0.01, 10


def _cp(dims=None):
    if dims is None:
        return pltpu.CompilerParams(vmem_limit_bytes=VMEM_LIMIT)
    return pltpu.CompilerParams(dimension_semantics=dims, vmem_limit_bytes=VMEM_LIMIT)


def _sds(shape, dtype=F32):
    return jax.ShapeDtypeStruct(shape, dtype)


def _nt(a, b):
    return lax.dot_general(a, b, (((1,), (1,)), ((), ())), preferred_element_type=F32)


def _tn(a, b):
    return lax.dot_general(a, b, (((0,), (0,)), ((), ())), preferred_element_type=F32)


def _nn(a, b):
    return jnp.dot(a, b, preferred_element_type=F32)


def _rms_r(x):
    return lax.rsqrt(jnp.mean(x * x, axis=-1, keepdims=True) + EPS)


def _rms_bwd(n, r, g, dh):
    dn = dh * g
    dx = r * (dn - n * jnp.mean(dn * n, axis=-1, keepdims=True))
    return dx, jnp.sum(dh * n, axis=0, keepdims=True)


def _row_tile(rows, cols, limit):
    tr = rows
    while tr * cols * 4 > limit and tr % 32 == 0:
        tr //= 2
    return tr


def _place():
    return lax.axis_index("x"), lax.axis_index("y"), lax.axis_index("c")


def _all_gather(shards, name):
    n = len(shards)

    def body(*refs):
        x_refs, out_refs = refs[:n], refs[n:2 * n]
        send_sems, recv_sems, local_sems = refs[2 * n:]
        x, y, c = _place()
        me, sibling = (x, y, c), (x, y, 1 - c)
        chips = [(1 - x, y), (x, 1 - y), (1 - x, 1 - y)]

        def copy(a, k, block, to, src=None):
            px, py, pc = block
            slot = out_refs[a].at[4 * px + 2 * py + pc]
            return pltpu.make_async_remote_copy(
                src_ref=slot if src is None else src, dst_ref=slot,
                send_sem=send_sems.at[7 * a + k], recv_sem=recv_sems.at[7 * a + k], device_id=to, device_id_type=MESH)

        mine, first, passed = [], [], []
        for a in range(n):
            cp = pltpu.make_async_copy(x_refs[a], out_refs[a].at[4 * x + 2 * y + c], local_sems.at[a])
            cp.start()
            mine.append(cp)
            cps = [copy(a, 0, me, sibling, src=x_refs[a])]
            cps += [copy(a, 1 + j, me, (*chip, c), src=x_refs[a]) for j, chip in enumerate(chips)]
            for cp in cps:
                cp.start()
            first += cps
        for a in range(n):
            for j, chip in enumerate(chips):
                copy(a, 1 + j, (*chip, c), me).wait_recv()
                fwd = copy(a, 4 + j, (*chip, c), sibling)
                fwd.start()
                passed.append(fwd)
        for a in range(n):
            copy(a, 0, sibling, me).wait_recv()
            for j, chip in enumerate(chips):
                copy(a, 4 + j, (*chip, 1 - c), me).wait_recv()
        for cp in first + passed:
            cp.wait_send()
        for cp in mine:
            cp.wait()

    anyspec = pl.BlockSpec(memory_space=pl.ANY)
    return pl.pallas_call(
        body, name=name,
        out_shape=[_sds((NDEV, *s.shape), s.dtype) for s in shards],
        in_specs=[anyspec] * n, out_specs=[anyspec] * n,
        scratch_shapes=[pltpu.SemaphoreType.DMA((7 * n,)), pltpu.SemaphoreType.DMA((7 * n,)),
                        pltpu.SemaphoreType.DMA((n,))],
    )(*shards)


def _gather_first_copies(x_refs, out_refs, send_sems, recv_sems, local_sems):
    x, y, c = _place()
    slot = 4 * x + 2 * y + c
    peers = [(x, y, 1 - c), (1 - x, y, c), (x, 1 - y, c), (1 - x, 1 - y, c)]
    local, remote = [], []
    for a, (x_ref, out_ref) in enumerate(zip(x_refs, out_refs)):
        local.append(pltpu.make_async_copy(x_ref, out_ref.at[slot], local_sems.at[a]))
        for k, peer in enumerate(peers):
            remote.append(pltpu.make_async_remote_copy(
                src_ref=x_ref, dst_ref=out_ref.at[slot], send_sem=send_sems.at[4 * a + k],
                recv_sem=recv_sems.at[4 * a + k], device_id=peer, device_id_type=MESH))
    return local, remote


def _gather_first_scratch(n):
    return [pltpu.SemaphoreType.DMA((4 * n,)), pltpu.SemaphoreType.DMA((4 * n,)), pltpu.SemaphoreType.DMA((n,))]


def _chip_swap_copies(p_refs, out_refs, send_sems, recv_sems):
    x, y, c = _place()
    chips = [(1 - x, y), (x, 1 - y), (1 - x, 1 - y)]
    cps = []
    for a, (p_ref, out_ref) in enumerate(zip(p_refs, out_refs)):
        for j, (px, py) in enumerate(chips):
            cps.append(pltpu.make_async_remote_copy(
                src_ref=p_ref.at[2 * px + py], dst_ref=out_ref.at[j],
                send_sem=send_sems.at[3 * a + j], recv_sem=recv_sems.at[3 * a + j],
                device_id=(px, py, c), device_id_type=MESH))
    return cps


def _sibling_swap_copies(a_refs, out_refs, send_sems, recv_sems):
    x, y, c = _place()
    cps = []
    for a, (a_ref, out_ref) in enumerate(zip(a_refs, out_refs)):
        for k in range(4):
            cps.append(pltpu.make_async_remote_copy(
                src_ref=a_ref.at[k, 1 - c], dst_ref=out_ref.at[k],
                send_sem=send_sems.at[4 * a + k], recv_sem=recv_sems.at[4 * a + k],
                device_id=(x, y, 1 - c), device_id_type=MESH))
    return cps


def _pass_on_copies(buf_refs, send_sems, recv_sems):
    x, y, c = _place()
    chips = [(1 - x, y), (x, 1 - y), (1 - x, 1 - y)]
    cps = []
    for a, buf in enumerate(buf_refs):
        for j, (px, py) in enumerate(chips):
            block = buf.at[4 * px + 2 * py + c]
            cps.append(pltpu.make_async_remote_copy(
                src_ref=block, dst_ref=block, send_sem=send_sems.at[3 * a + j], recv_sem=recv_sems.at[3 * a + j],
                device_id=(x, y, 1 - c), device_id_type=MESH))
    return cps


def _swap_sibling(arrs, name):
    n = len(arrs)

    def body(*refs):
        cps = _sibling_swap_copies(refs[:n], refs[n:2 * n], *refs[2 * n:])
        for cp in cps:
            cp.start()
        for cp in cps:
            cp.wait()

    anyspec = pl.BlockSpec(memory_space=pl.ANY)
    return pl.pallas_call(
        body, name=name,
        out_shape=[_sds((4, *a.shape[2:]), a.dtype) for a in arrs],
        in_specs=[anyspec] * n, out_specs=[anyspec] * n,
        scratch_shapes=[pltpu.SemaphoreType.DMA((4 * n,)), pltpu.SemaphoreType.DMA((4 * n,))],
    )(*arrs)


def _in_proj(x, g, wcat, shards):
    tm = 256
    n = len(shards)

    def body(x_ref, g_ref, w_ref, *rest):
        x_refs, (h_ref, z_ref, ft_ref), out_refs, sems = rest[:n], rest[n:n + 3], rest[n + 3:2 * n + 3], rest[2 * n + 3:]

        @pl.when(pl.program_id(0) == 0)
        def _():
            local, remote = _gather_first_copies(x_refs, out_refs, *sems)
            for cp in local + remote:
                cp.start()

        xv = x_ref[...]
        h = (xv * _rms_r(xv) * g_ref[...]).astype(BF16)
        h_ref[...] = h
        z = _nt(h, w_ref[...])
        z_ref[...] = z
        ft_ref[...] = z[:, 2048:ZC].T[0:NH, :]

        @pl.when(pl.program_id(0) == T // tm - 1)
        def _():
            local, remote = _gather_first_copies(x_refs, out_refs, *sems)
            for cp in remote + local:
                cp.wait()

    row = lambda i: (i, 0)
    fixed = lambda i: (0, 0)
    anyspec = pl.BlockSpec(memory_space=pl.ANY)
    res = pl.pallas_call(
        body, name="in_proj", grid=(T // tm,),
        in_specs=[pl.BlockSpec((tm, D), row), pl.BlockSpec((1, D), fixed), pl.BlockSpec((ZC, D), fixed)] + [anyspec] * n,
        out_specs=[pl.BlockSpec((tm, D), row), pl.BlockSpec((tm, ZC), row), pl.BlockSpec((NH, tm), lambda i: (0, i))]
        + [anyspec] * n,
        out_shape=[_sds((T, D), BF16), _sds((T, ZC)), _sds((NH, T))] + [_sds((NDEV, *s.shape), s.dtype) for s in shards],
        scratch_shapes=_gather_first_scratch(n),
        compiler_params=_cp(("arbitrary",)),
    )(x, g, wcat, *shards)
    return res[0], res[1], res[2], res[3:]


def _in_proj_dw(h, dq, dk, dv, du, df):
    tm = 512

    def body(h_ref, dq_ref, dk_ref, dv_ref, du_ref, df_ref, o_ref):
        hv = h_ref[...]
        for j, d_ref in enumerate((dq_ref, dk_ref, dv_ref, du_ref)):
            o_ref[512 * j:512 * (j + 1), :] = _tn(d_ref[...], hv)
        o_ref[2048:ZC, :] = _tn(df_ref[...], hv)

    full = lambda w: pl.BlockSpec((T, w), lambda i: (0, 0))
    return pl.pallas_call(
        body, name="in_proj_dw", grid=(D // tm,),
        in_specs=[pl.BlockSpec((T, tm), lambda i: (0, i)), full(512), full(512), full(512), full(512), full(LANES)],
        out_specs=pl.BlockSpec((ZC, tm), lambda i: (0, i)), out_shape=_sds((ZC, D)),
        compiler_params=_cp(("parallel",)),
    )(h, dq, dk, dv, du, df)


def _in_proj_bwd(dq, dk, dv, du, df, wcat, x, g, dres, parts):
    tm = 256
    n = len(parts)

    def body(dq_ref, dk_ref, dv_ref, du_ref, df_ref, w_ref, x_ref, g_ref, dres_ref, *rest):
        p_refs, (dx_ref, dg_ref), fc_refs, sems = rest[:n], rest[n:n + 2], rest[n + 2:2 * n + 2], rest[2 * n + 2:]

        @pl.when(pl.program_id(0) == 0)
        def _():
            for cp in _chip_swap_copies(p_refs, fc_refs, *sems):
                cp.start()

        dh = _nn(df_ref[...], w_ref[2048:ZC, :])
        for j, d_ref in enumerate((dq_ref, dk_ref, dv_ref, du_ref)):
            dh = dh + _nn(d_ref[...], w_ref[512 * j:512 * (j + 1), :])
        xv = x_ref[...]
        r = _rms_r(xv)
        dxr, dgp = _rms_bwd(xv * r, r, g_ref[...], dh)
        dx_ref[...] = dres_ref[...] + dxr

        @pl.when(pl.program_id(0) == 0)
        def _():
            dg_ref[...] = jnp.zeros_like(dg_ref)
        dg_ref[...] += dgp

        @pl.when(pl.program_id(0) == T // tm - 1)
        def _():
            for cp in _chip_swap_copies(p_refs, fc_refs, *sems):
                cp.wait()

    row = lambda i: (i, 0)
    fixed = lambda i: (0, 0)
    part = pl.BlockSpec((tm, 512), row)
    anyspec = pl.BlockSpec(memory_space=pl.ANY)
    res = pl.pallas_call(
        body, name="in_proj_bwd", grid=(T // tm,),
        in_specs=[part, part, part, part, pl.BlockSpec((tm, LANES), row), pl.BlockSpec((ZC, D), fixed),
                  pl.BlockSpec((tm, D), row), pl.BlockSpec((1, D), fixed), pl.BlockSpec((tm, D), row)] + [anyspec] * n,
        out_specs=[pl.BlockSpec((tm, D), row), pl.BlockSpec((1, D), fixed)] + [anyspec] * n,
        out_shape=[_sds((T, D)), _sds((1, D))] + [_sds((3, *p.shape[1:]), p.dtype) for p in parts],
        scratch_shapes=[pltpu.SemaphoreType.DMA((3 * n,)), pltpu.SemaphoreType.DMA((3 * n,))],
        compiler_params=_cp(("arbitrary",)),
    )(dq, dk, dv, du, df, wcat, x, g, dres, *parts)
    return res[0], res[1], res[2:]


def _stack_lanes(v):
    return jnp.concatenate([v[:, LANES * b:LANES * (b + 1)] for b in range(T // LANES)], axis=0)


def _unstack_lanes(s):
    return jnp.concatenate([s[8 * b:8 * b + 8, :] for b in range(T // LANES)], axis=1)


def _cumsum_lanes(v, reverse):
    st = _stack_lanes(v)
    ri = lax.broadcasted_iota(jnp.int32, (LANES, LANES), 0)
    ci = lax.broadcasted_iota(jnp.int32, (LANES, LANES), 1)
    tri = (ri >= ci) if reverse else (ri <= ci)
    intra = jnp.dot(st, tri.astype(F32), precision=HI, preferred_element_type=F32)
    tot = intra[:, 0:1] if reverse else intra[:, LANES - 1:LANES]
    same_head = (ci % 8) == (ri % 8)
    other = (ci // 8 > ri // 8) if reverse else (ci // 8 < ri // 8)
    off = jnp.dot((same_head & other).astype(F32), jnp.broadcast_to(tot, (LANES, LANES)), precision=HI,
                  preferred_element_type=F32)
    return _unstack_lanes(intra + off)


def _gates_fwd(f_t, b_f):
    def body(f_ref, b_ref, c_ref):
        z = f_ref[...] + b_ref[...]
        lf = jnp.minimum(z, 0.0) - jnp.log(1.0 + jnp.exp(-jnp.abs(z)))
        c_ref[...] = _cumsum_lanes(lf, reverse=False)

    return pl.pallas_call(body, name="gates_fwd", out_shape=_sds((NH, T)), compiler_params=_cp())(f_t, b_f)


def _gates_bwd(dck, f_t, b_f):
    def body(d_ref, f_ref, b_ref, df_ref, db_ref):
        z = f_ref[...] + b_ref[...]
        dlf = _cumsum_lanes(d_ref[...], reverse=True)
        df = dlf * jax.nn.sigmoid(-z)
        df_ref[...] = df
        db_ref[...] = jnp.sum(df, axis=1, keepdims=True)

    return pl.pallas_call(body, name="gates_bwd", out_shape=[_sds((NH, T)), _sds((NH, 1))],
                          compiler_params=_cp())(dck, f_t, b_f)


def _lower_triangle():
    rows = lax.broadcasted_iota(jnp.int32, (BQ, BQ), 0)
    cols = lax.broadcasted_iota(jnp.int32, (BQ, BQ), 1)
    return cols <= rows


def _logit_parts(qb, kb, ck_row, lo, hi, tri):
    parts = []
    if lo > 0:
        parts.append((_nt(qb[lo:hi], kb[0:lo]) - ck_row[:, 0:lo], 0, lo))
    parts.append((jnp.where(tri, _nt(qb[lo:hi], kb[lo:hi]) - ck_row[:, lo:hi], NEG), lo, hi))
    return parts


_HP = NH // 2


def _qkv_specs():
    return [pl.BlockSpec((T, LANES), lambda p: (0, p)), pl.BlockSpec((T, LANES), lambda p: (0, _HP + p)),
            pl.BlockSpec((T, LANES), lambda p: (0, 2 * _HP + p))]


def _attn_fwd(z, ck, g_q, g_k, shards, bufs):
    scale = DH ** -0.5
    n, m = len(shards), len(bufs)

    def body(q_ref, k_ref, v_ref, ck_ref, gq_ref, gk_ref, *rest):
        x_refs, (o_ref, lse_ref) = rest[:n], rest[n + m:n + m + 2]
        out_refs, buf_refs, sems = rest[n + m + 2:2 * n + m + 2], rest[2 * n + m + 2:2 * n + 2 * m + 2], rest[2 * n + 2 * m + 2:]

        @pl.when(pl.program_id(0) == 0)
        def _():
            local, remote = _gather_first_copies(x_refs, out_refs, *sems[:3])
            for cp in local + remote + _pass_on_copies(buf_refs, *sems[3:]):
                cp.start()

        qb, kb, vb = [], [], []
        for hh in range(2):
            cols = slice(DH * hh, DH * (hh + 1))
            qv, kv = q_ref[:, cols], k_ref[:, cols]
            qb.append((qv * _rms_r(qv) * gq_ref[...] * scale).astype(BF16))
            kb.append((kv * _rms_r(kv) * gk_ref[...]).astype(BF16))
            vb.append(v_ref[:, cols].astype(BF16))
        tri = _lower_triangle()
        for i in range(T // BQ):
            lo, hi = i * BQ, (i + 1) * BQ
            outs = []
            for hh in range(2):
                parts = _logit_parts(qb[hh], kb[hh], ck_ref[0, hh:hh + 1, :], lo, hi, tri)
                top = jnp.max(parts[-1][0], axis=-1, keepdims=True)
                if lo > 0:
                    top = jnp.maximum(top, jnp.max(parts[0][0], axis=-1, keepdims=True))
                l, o = 0.0, 0.0
                for s, k0, k1 in parts:
                    p = jnp.exp(s - top)
                    l = l + jnp.sum(p, axis=-1, keepdims=True)
                    o = o + _nn(p.astype(BF16), vb[hh][k0:k1])
                outs.append(o / l)
                lse_ref[0, lo:hi, hh:hh + 1] = top + jnp.log(l)
            o_ref[lo:hi, :] = jnp.concatenate(outs, axis=1)

        @pl.when(pl.program_id(0) == _HP - 1)
        def _():
            local, remote = _gather_first_copies(x_refs, out_refs, *sems[:3])
            for cp in remote + local + _pass_on_copies(buf_refs, *sems[3:]):
                cp.wait()

    fixed = lambda p: (0, 0)
    anyspec = pl.BlockSpec(memory_space=pl.ANY)
    res = pl.pallas_call(
        body, name="attn_fwd", grid=(_HP,),
        in_specs=_qkv_specs() + [pl.BlockSpec((1, 2, T), lambda p: (p, 0, 0)), pl.BlockSpec((1, DH), fixed),
                                 pl.BlockSpec((1, DH), fixed)] + [anyspec] * (n + m),
        out_specs=[pl.BlockSpec((T, LANES), lambda p: (0, p)), pl.BlockSpec((1, T, 2), lambda p: (p, 0, 0))]
        + [anyspec] * (n + m),
        out_shape=[_sds((T, AW)), _sds((_HP, T, 2))] + [_sds((NDEV, *s.shape), s.dtype) for s in shards]
        + [_sds(b.shape, b.dtype) for b in bufs],
        input_output_aliases={6 + n + i: 2 + n + i for i in range(m)},
        scratch_shapes=_gather_first_scratch(n) + [pltpu.SemaphoreType.DMA((3 * m,)), pltpu.SemaphoreType.DMA((3 * m,))],
        compiler_params=_cp(("arbitrary",)),
    )(z, z, z, ck, g_q, g_k, *shards, *bufs)
    return res[0], res[1], res[2:2 + n], res[2 + n:]


def _attn_bwd(z, ck, g_q, g_k, lse, datt, parts):
    scale = DH ** -0.5
    n = len(parts)

    def body(q_ref, k_ref, v_ref, ck_ref, gq_ref, gk_ref, lse_ref, do_ref, *rest):
        p_refs, rest = rest[:n], rest[n:]
        dq_ref, dk_ref, dv_ref, dck_ref, dgq_ref, dgk_ref = rest[:6]
        fc_refs, (dkn_acc, dv_acc, dc_acc), sems = rest[6:6 + n], rest[6 + n:9 + n], rest[9 + n:]

        @pl.when(pl.program_id(0) == 0)
        def _():
            for cp in _chip_swap_copies(p_refs, fc_refs, *sems):
                cp.start()

        nq, nk, rq, rk, qb, kb, vb, dob = [], [], [], [], [], [], [], []
        for hh in range(2):
            cols = slice(DH * hh, DH * (hh + 1))
            qv, kv = q_ref[:, cols], k_ref[:, cols]
            rq.append(_rms_r(qv))
            rk.append(_rms_r(kv))
            nq.append(qv * rq[hh])
            nk.append(kv * rk[hh])
            qb.append((nq[hh] * gq_ref[...] * scale).astype(BF16))
            kb.append((nk[hh] * gk_ref[...]).astype(BF16))
            vb.append(v_ref[:, cols].astype(BF16))
            dob.append(do_ref[:, cols].astype(BF16))
        dkn_acc[...] = jnp.zeros_like(dkn_acc)
        dv_acc[...] = jnp.zeros_like(dv_acc)
        dc_acc[...] = jnp.zeros_like(dc_acc)
        dgq = jnp.zeros((1, DH), F32)
        tri = _lower_triangle()
        for i in range(T // BQ):
            lo, hi = i * BQ, (i + 1) * BQ
            dqs = []
            for hh in range(2):
                lse = lse_ref[0, lo:hi, hh:hh + 1]
                pd = []
                for s, k0, k1 in _logit_parts(qb[hh], kb[hh], ck_ref[0, hh:hh + 1, :], lo, hi, tri):
                    pd.append((jnp.exp(s - lse), _nt(dob[hh][lo:hi], vb[hh][k0:k1]), k0, k1))
                mean = sum(jnp.sum(p * dp, axis=-1, keepdims=True) for p, dp, _, _ in pd)
                dqn = 0.0
                for p, dp, k0, k1 in pd:
                    ds = p * (dp - mean)
                    dsb = ds.astype(BF16)
                    dv_acc[hh, k0:k1, :] += _tn(p.astype(BF16), dob[hh][lo:hi])
                    dkn_acc[hh, k0:k1, :] += _tn(dsb, qb[hh][lo:hi])
                    dc_acc[hh:hh + 1, k0:k1] -= jnp.sum(ds, axis=0, keepdims=True)
                    dqn = dqn + _nn(dsb, kb[hh][k0:k1])
                dqx, dgp = _rms_bwd(nq[hh][lo:hi], rq[hh][lo:hi], gq_ref[...], dqn * scale)
                dqs.append(dqx)
                dgq = dgq + dgp
            dq_ref[lo:hi, :] = jnp.concatenate(dqs, axis=1).astype(BF16)
        dks, dgk = [], jnp.zeros((1, DH), F32)
        for hh in range(2):
            dkx, dgp = _rms_bwd(nk[hh], rk[hh], gk_ref[...], dkn_acc[hh])
            dks.append(dkx)
            dgk = dgk + dgp
        dk_ref[...] = jnp.concatenate(dks, axis=1).astype(BF16)
        dv_ref[...] = jnp.concatenate([dv_acc[0], dv_acc[1]], axis=1).astype(BF16)
        dck_ref[0] = dc_acc[...]

        @pl.when(pl.program_id(0) == 0)
        def _():
            dgq_ref[...] = jnp.zeros_like(dgq_ref)
            dgk_ref[...] = jnp.zeros_like(dgk_ref)
        dgq_ref[...] += dgq
        dgk_ref[...] += dgk

        @pl.when(pl.program_id(0) == _HP - 1)
        def _():
            for cp in _chip_swap_copies(p_refs, fc_refs, *sems):
                cp.wait()

    fixed = lambda p: (0, 0)
    pair = pl.BlockSpec((T, LANES), lambda p: (0, p))
    gsp = pl.BlockSpec((1, DH), fixed)
    ckspec = pl.BlockSpec((1, 2, T), lambda p: (p, 0, 0))
    anyspec = pl.BlockSpec(memory_space=pl.ANY)
    res = pl.pallas_call(
        body, name="attn_bwd", grid=(_HP,),
        in_specs=_qkv_specs() + [ckspec, gsp, gsp, pl.BlockSpec((1, T, 2), lambda p: (p, 0, 0)), pair] + [anyspec] * n,
        out_specs=[pair, pair, pair, ckspec, gsp, gsp] + [anyspec] * n,
        out_shape=[_sds((T, AW), BF16)] * 3 + [_sds((_HP, 2, T)), _sds((1, DH)), _sds((1, DH))]
        + [_sds((3, *p.shape[1:]), p.dtype) for p in parts],
        scratch_shapes=[pltpu.VMEM((2, T, DH), F32), pltpu.VMEM((2, T, DH), F32), pltpu.VMEM((2, T), F32),
                        pltpu.SemaphoreType.DMA((3 * n,)), pltpu.SemaphoreType.DMA((3 * n,))],
        compiler_params=_cp(("arbitrary",)),
    )(z, z, z, ck, g_q, g_k, lse, datt, *parts)
    return (*res[:6], res[6:])


def _disc(lr, li, ls, br_t, bi_t):
    step = jnp.exp(ls)
    er = jnp.exp(lr * step)
    ab_re = er * jnp.cos(li * step)
    ab_im = er * jnp.sin(li * step)
    num_re = ab_re - 1.0
    num_im = ab_im
    den = lr * lr + li * li
    f_re = (num_re * lr + num_im * li) / den
    f_im = (num_im * lr - num_re * li) / den
    bb_re = f_re[:, None, :] * br_t - f_im[:, None, :] * bi_t
    bb_im = f_re[:, None, :] * bi_t + f_im[:, None, :] * br_t
    return ab_re, ab_im, bb_re, bb_im


def _disc_fwd(lr, li, ls, br_t, bi_t):
    def body(lr_ref, li_ref, ls_ref, br_ref, bi_ref, ar_ref, ai_ref, bbr_ref, bbi_ref):
        ar, ai, bbr, bbi = _disc(lr_ref[...], li_ref[...], ls_ref[...], br_ref[...], bi_ref[...])
        ar_ref[...] = ar
        ai_ref[...] = ai
        bbr_ref[...] = bbr
        bbi_ref[...] = bbi

    return pl.pallas_call(
        body, name="ssm_disc_fwd",
        out_shape=[_sds((NG, NP)), _sds((NG, NP)), _sds((NG, GS, NP)), _sds((NG, GS, NP))],
        compiler_params=_cp())(lr, li, ls, br_t, bi_t)


def _disc_bwd(lr, li, ls, br_t, bi_t, dar, dai, dbbr, dbbi):
    def body(lr_ref, li_ref, ls_ref, br_ref, bi_ref, dar_ref, dai_ref, dbbr_ref, dbbi_ref,
             dlr_ref, dli_ref, dls_ref, dbr_ref, dbi_ref):
        _, vjp = jax.vjp(_disc, lr_ref[...], li_ref[...], ls_ref[...], br_ref[...], bi_ref[...])
        dlr, dli, dls, dbr, dbi = vjp((dar_ref[...], dai_ref[...], dbbr_ref[...], dbbi_ref[...]))
        dlr_ref[...] = dlr
        dli_ref[...] = dli
        dls_ref[...] = dls
        dbr_ref[...] = dbr
        dbi_ref[...] = dbi

    return pl.pallas_call(
        body, name="ssm_disc_bwd",
        out_shape=[_sds((NG, NP)), _sds((NG, NP)), _sds((NG, 1)), _sds((NG, GS, NP)), _sds((NG, GS, NP))],
        compiler_params=_cp())(lr, li, ls, br_t, bi_t, dar, dai, dbbr, dbbi)


def _cmul(ar, ai, br, bi):
    return ar * br - ai * bi, ar * bi + ai * br


def _scan(buf, a_re, a_im, reverse, other=None):
    ar = [jnp.broadcast_to(a_re[:, LANES * k:LANES * (k + 1)], (NSEG, LANES)) for k in range(4)]
    ai = [jnp.broadcast_to(a_im[:, LANES * k:LANES * (k + 1)], (NSEG, LANES)) for k in range(4)]

    def tile(ref, i, k):
        return ref[pl.ds(pl.multiple_of(i * NSEG, NSEG), NSEG), LANES * k:LANES * (k + 1)]

    def advance(i, xr, xi):
        nr = [ar[k] * xr[k] - ai[k] * xi[k] + tile(buf, i, k) for k in range(4)]
        ni = [ar[k] * xi[k] + ai[k] * xr[k] + tile(buf, i, 4 + k) for k in range(4)]
        return nr, ni

    def index(i):
        return (SEG - 1 - i) if reverse else i

    zeros = tuple(jnp.zeros((NSEG, LANES), F32) for _ in range(4))

    def sweep1(i, carry):
        nr, ni = advance(index(i), carry[0], carry[1])
        return tuple(nr), tuple(ni)

    er, ei = lax.fori_loop(0, SEG, sweep1, (zeros, zeros), unroll=4)

    row = lax.broadcasted_iota(jnp.int32, (NSEG, LANES), 0)
    sr, si = [], []
    for k in range(4):
        pr, pi = ar[k], ai[k]
        for _ in range(SEG.bit_length() - 1):
            pr, pi = _cmul(pr, pi, pr, pi)
        ir, ii = er[k], ei[k]
        for d in (1, 2, 4):
            shift = (NSEG - d) if reverse else d
            ok = (row < NSEG - d) if reverse else (row >= d)
            mr, mi = _cmul(pr, pi, pltpu.roll(ir, shift, 0), pltpu.roll(ii, shift, 0))
            ir = ir + jnp.where(ok, mr, 0.0)
            ii = ii + jnp.where(ok, mi, 0.0)
            pr, pi = _cmul(pr, pi, pr, pi)
        shift = (NSEG - 1) if reverse else 1
        ok = (row < NSEG - 1) if reverse else (row >= 1)
        sr.append(jnp.where(ok, pltpu.roll(ir, shift, 0), 0.0))
        si.append(jnp.where(ok, pltpu.roll(ii, shift, 0), 0.0))

    def sweep2(i, carry):
        xr, xi, dar, dai = carry
        idx = index(i)
        if other is not None:
            orr = [tile(other, idx, k) for k in range(4)]
            oi = [tile(other, idx, 4 + k) for k in range(4)]
            dar = tuple(dar[k] + xr[k] * orr[k] + xi[k] * oi[k] for k in range(4))
            dai = tuple(dai[k] + xi[k] * orr[k] - xr[k] * oi[k] for k in range(4))
        nr, ni = advance(idx, xr, xi)
        start = pl.multiple_of(idx * NSEG, NSEG)
        for k in range(4):
            buf[pl.ds(start, NSEG), LANES * k:LANES * (k + 1)] = nr[k]
            buf[pl.ds(start, NSEG), LANES * (4 + k):LANES * (5 + k)] = ni[k]
        return tuple(nr), tuple(ni), dar, dai

    _, _, dar, dai = lax.fori_loop(0, SEG, sweep2, (tuple(sr), tuple(si), zeros, zeros), unroll=4)
    if other is None:
        return None
    da_re = jnp.concatenate([jnp.sum(d, axis=0, keepdims=True) for d in dar], axis=1)
    da_im = jnp.concatenate([jnp.sum(d, axis=0, keepdims=True) for d in dai], axis=1)
    return da_re, da_im


_SSM_BLOCKS = 4


def _ssm_fwd(u, bb, ab, cc, dsk, shards):
    n = len(shards)

    def body(u_ref, bb_ref, a_ref, cc_ref, d_ref, *rest):
        x_refs, y_ref, xs_ref, out_refs, sems = rest[:n], rest[n], rest[n + 1], rest[n + 2:2 * n + 2], rest[2 * n + 2:]

        @pl.when(pl.program_id(0) == 0)
        def _():
            local, remote = _gather_first_copies(x_refs, out_refs, *sems)
            for cp in local + remote:
                cp.start()

        ub = u_ref[...]
        xb = xs_ref.at[0]
        xb[...] = _nn(ub.astype(BF16), bb_ref[0])
        _scan(xb, a_ref[0, 0:1, :], a_ref[0, 1:2, :], reverse=False)
        y_ref[...] = _nn(xb[...].astype(BF16), cc_ref[0]) + d_ref[0] * ub

        @pl.when(pl.program_id(0) == _SSM_BLOCKS - 1)
        def _():
            local, remote = _gather_first_copies(x_refs, out_refs, *sems)
            for cp in remote + local:
                cp.wait()

    blk = lambda j: (j, 0, 0)
    col = pl.BlockSpec((T, LANES), lambda j: (0, j))
    anyspec = pl.BlockSpec(memory_space=pl.ANY)
    res = pl.pallas_call(
        body, name="ssm_fwd", grid=(_SSM_BLOCKS,),
        in_specs=[col, pl.BlockSpec((1, LANES, 1024), blk), pl.BlockSpec((1, 2, 512), blk),
                  pl.BlockSpec((1, 1024, LANES), blk), pl.BlockSpec((1, 1, LANES), blk)] + [anyspec] * n,
        out_specs=[col, pl.BlockSpec((1, T, 1024), blk)] + [anyspec] * n,
        out_shape=[_sds((T, SW)), _sds((_SSM_BLOCKS, T, 1024))] + [_sds((NDEV, *s.shape), s.dtype) for s in shards],
        scratch_shapes=_gather_first_scratch(n),
        compiler_params=_cp(("arbitrary",)),
    )(u, bb, ab, cc, dsk, *shards)
    return res[0], res[1], res[2:]


def _ssm_bwd(u, dy, xs, bb, ab, cc, dsk, arrs):
    n = len(arrs)

    def body(u_ref, dy_ref, x_ref, bb_ref, a_ref, cc_ref, d_ref, *rest):
        a_refs, rest = rest[:n], rest[n:]
        du_ref, dbb_ref, da_ref, dcc_ref, dd_ref = rest[:5]
        fs_refs, gb, sems = rest[5:5 + n], rest[5 + n], rest[6 + n:]

        @pl.when(pl.program_id(0) == 0)
        def _():
            for cp in _sibling_swap_copies(a_refs, fs_refs, *sems):
                cp.start()

        ub, dyv = u_ref[...], dy_ref[...]
        ubb, dyb = ub.astype(BF16), dyv.astype(BF16)
        a_re, a_im = a_ref[0, 0:1, :], a_ref[0, 1:2, :]
        dcc_ref[0] = _tn(x_ref[0].astype(BF16), dyb)
        gb[...] = _nt(dyb, cc_ref[0])
        da_re, da_im = _scan(gb, a_re, -a_im, reverse=True, other=x_ref.at[0])
        da_ref[0] = jnp.concatenate([da_re, da_im], axis=0)
        gc = gb[...].astype(BF16)
        du_ref[...] = _nt(gc, bb_ref[0]) + d_ref[0] * dyv
        dbb_ref[0] = _tn(ubb, gc)
        dd_ref[0] = jnp.sum(dyv * ub, axis=0, keepdims=True)

        @pl.when(pl.program_id(0) == _SSM_BLOCKS - 1)
        def _():
            for cp in _sibling_swap_copies(a_refs, fs_refs, *sems):
                cp.wait()

    blk = lambda j: (j, 0, 0)
    col = pl.BlockSpec((T, LANES), lambda j: (0, j))
    anyspec = pl.BlockSpec(memory_space=pl.ANY)
    res = pl.pallas_call(
        body, name="ssm_bwd", grid=(_SSM_BLOCKS,),
        in_specs=[col, col, pl.BlockSpec((1, T, 1024), blk), pl.BlockSpec((1, LANES, 1024), blk),
                  pl.BlockSpec((1, 2, 512), blk), pl.BlockSpec((1, 1024, LANES), blk), pl.BlockSpec((1, 1, LANES), blk)]
        + [anyspec] * n,
        out_specs=[col, pl.BlockSpec((1, LANES, 1024), blk), pl.BlockSpec((1, 2, 512), blk),
                   pl.BlockSpec((1, 1024, LANES), blk), pl.BlockSpec((1, 1, LANES), blk)] + [anyspec] * n,
        out_shape=[_sds((T, SW)), _sds((4, LANES, 1024)), _sds((4, 2, 512)), _sds((4, 1024, LANES)), _sds((4, 1, LANES))]
        + [_sds((4, *a.shape[2:]), a.dtype) for a in arrs],
        scratch_shapes=[pltpu.VMEM((T, 1024), F32), pltpu.SemaphoreType.DMA((4 * n,)), pltpu.SemaphoreType.DMA((4 * n,))],
        compiler_params=_cp(("arbitrary",)),
    )(u, dy, xs, bb, ab, cc, dsk, *arrs)
    return (*res[:5], res[5:])


def _interleave(a):
    return a.reshape(NSEG, SEG, a.shape[1]).transpose(1, 0, 2).reshape(a.shape)


def _deinterleave(a):
    return a.reshape(SEG, NSEG, a.shape[1]).transpose(1, 0, 2).reshape(a.shape)


_TM_MIX = 256


def _mix_parts(att, y, wg, bg):
    y2 = jax.nn.gelu(y)
    y2b = y2.astype(BF16)
    sg = jax.nn.sigmoid(_nn(y2b, wg) + bg)
    ssm = y2 * sg
    ra, rs = _rms_r(att), _rms_r(ssm)
    return y2, y2b, sg, ssm, ra, rs


def _mix_fwd(att, y, x, wg, bg, ga, gs, wo, bufs):
    m = len(bufs)

    def body(att_ref, y_ref, x_ref, wg_ref, bg_ref, ga_ref, gs_ref, wo_ref, *rest):
        x1_ref, buf_refs, sems = rest[m], rest[m + 1:2 * m + 1], rest[2 * m + 1:]

        @pl.when(pl.program_id(0) == 0)
        def _():
            for cp in _pass_on_copies(buf_refs, *sems):
                cp.start()

        attv = att_ref[...]
        _, _, _, ssm, ra, rs = _mix_parts(attv, y_ref[...], wg_ref[...], bg_ref[...])
        ma = (attv * ra * ga_ref[...]).astype(BF16)
        ms = (ssm * rs * gs_ref[...]).astype(BF16)
        x1_ref[...] = x_ref[...] + _nn(ma, wo_ref[0:AW, :]) + _nn(ms, wo_ref[AW:D, :])

        @pl.when(pl.program_id(0) == T // _TM_MIX - 1)
        def _():
            for cp in _pass_on_copies(buf_refs, *sems):
                cp.wait()

    row = lambda i: (i, 0)
    fixed = lambda i: (0, 0)
    half = pl.BlockSpec((_TM_MIX, AW), row)
    vec = pl.BlockSpec((1, AW), fixed)
    anyspec = pl.BlockSpec(memory_space=pl.ANY)
    res = pl.pallas_call(
        body, name="mix_fwd", grid=(T // _TM_MIX,),
        in_specs=[half, half, pl.BlockSpec((_TM_MIX, D), row), pl.BlockSpec((SW, SW), fixed), vec, vec, vec,
                  pl.BlockSpec((D, D), fixed)] + [anyspec] * m,
        out_specs=[pl.BlockSpec((_TM_MIX, D), row)] + [anyspec] * m,
        out_shape=[_sds((T, D))] + [_sds(b.shape, b.dtype) for b in bufs],
        input_output_aliases={8 + i: 1 + i for i in range(m)},
        scratch_shapes=[pltpu.SemaphoreType.DMA((3 * m,)), pltpu.SemaphoreType.DMA((3 * m,))],
        compiler_params=_cp(("arbitrary",)),
    )(att, y, x, wg, bg, ga, gs, wo, *bufs)
    return res[0], res[1:]


def _mix_bwd(att, y, dx1, wg, bg, ga, gs, wo):
    last = T // _TM_MIX - 1

    def body(att_ref, y_ref, d_ref, wg_ref, bg_ref, ga_ref, gs_ref, wo_ref,
             datt_ref, dy_ref, dwg_ref, dbg_ref, dga_ref, dgs_ref, dwo_ref, dwg_acc, dwo_acc):
        attv, yv, db = att_ref[...], y_ref[...], d_ref[...].astype(BF16)
        y2, y2b, sg, ssm, ra, rs = _mix_parts(attv, yv, wg_ref[...], bg_ref[...])
        na, ns = attv * ra, ssm * rs
        ma = (na * ga_ref[...]).astype(BF16)
        ms = (ns * gs_ref[...]).astype(BF16)
        dma = _nt(db, wo_ref[0:AW, :])
        dms = _nt(db, wo_ref[AW:D, :])
        datt, dga = _rms_bwd(na, ra, ga_ref[...], dma)
        dssm, dgs = _rms_bwd(ns, rs, gs_ref[...], dms)
        dgl = dssm * y2 * sg * (1.0 - sg)
        dglb = dgl.astype(BF16)
        dy2 = dssm * sg + _nt(dglb, wg_ref[...])
        _, gelu_vjp = jax.vjp(jax.nn.gelu, yv)
        datt_ref[...] = datt
        dy_ref[...] = gelu_vjp(dy2)[0]

        @pl.when(pl.program_id(0) == 0)
        def _():
            dwg_acc[...] = jnp.zeros_like(dwg_acc)
            dwo_acc[...] = jnp.zeros_like(dwo_acc)
            dbg_ref[...] = jnp.zeros_like(dbg_ref)
            dga_ref[...] = jnp.zeros_like(dga_ref)
            dgs_ref[...] = jnp.zeros_like(dgs_ref)
        dwg_acc[...] += _tn(y2b, dglb)
        dbg_ref[...] += jnp.sum(dgl, axis=0, keepdims=True)
        dga_ref[...] += dga
        dgs_ref[...] += dgs
        dwo_acc[0:AW, :] += _tn(ma, db)
        dwo_acc[AW:D, :] += _tn(ms, db)

        @pl.when(pl.program_id(0) == last)
        def _():
            dwg_ref[...] = dwg_acc[...].astype(BF16)
            dwo_ref[...] = dwo_acc[...].astype(BF16)

    row = lambda i: (i, 0)
    fixed = lambda i: (0, 0)
    half = pl.BlockSpec((_TM_MIX, AW), row)
    vec = pl.BlockSpec((1, AW), fixed)
    sq = pl.BlockSpec((SW, SW), fixed)
    big = pl.BlockSpec((D, D), fixed)
    return pl.pallas_call(
        body, name="mix_bwd", grid=(T // _TM_MIX,),
        in_specs=[half, half, pl.BlockSpec((_TM_MIX, D), row), sq, vec, vec, vec, big],
        out_specs=[half, half, sq, vec, vec, vec, big],
        out_shape=[_sds((T, AW)), _sds((T, SW)), _sds((SW, SW), BF16), _sds((1, SW)), _sds((1, AW)), _sds((1, SW)),
                   _sds((D, D), BF16)],
        scratch_shapes=[pltpu.VMEM((SW, SW), F32), pltpu.VMEM((D, D), F32)],
        compiler_params=_cp(("arbitrary",)),
    )(att, y, dx1, wg, bg, ga, gs, wo)


_NCB = -(-CS // LANES)


def _ffn_up(x1, g, wu4):
    tm = 512

    def body(x_ref, g_ref, w_ref, h_ref, up_ref):
        @pl.when(pl.program_id(1) == 0)
        def _():
            xv = x_ref[...]
            h_ref[...] = (xv * _rms_r(xv) * g_ref[...]).astype(BF16)
        up_ref[0, 0] = _nt(h_ref[...], w_ref[0, 0])

    return pl.pallas_call(
        body, name="ffn_up", grid=(T // tm, NDEV),
        in_specs=[pl.BlockSpec((tm, D), lambda i, e: (i, 0)), pl.BlockSpec((1, D), lambda i, e: (0, 0)),
                  pl.BlockSpec((1, 1, CS, D), lambda i, e: (e // 4, e % 4, 0, 0))],
        out_specs=[pl.BlockSpec((tm, D), lambda i, e: (i, 0)),
                   pl.BlockSpec((1, 1, tm, CS), lambda i, e: (e // 4, e % 4, i, 0))],
        out_shape=[_sds((T, D), BF16), _sds((2, 4, T, CS))],
        compiler_params=_cp(("parallel", "arbitrary")),
    )(x1, g, wu4)


def _shift_down(h, k):
    row = lax.broadcasted_iota(jnp.int32, h.shape, 0)
    return jnp.where(row >= k, pltpu.roll(h, k, 0), 0.0)


def _shift_up(h, k):
    row = lax.broadcasted_iota(jnp.int32, h.shape, 0)
    return jnp.where(row < T - k, pltpu.roll(h, T - k, 0), 0.0)


def _conv(h, w, b):
    return w[0:1, :] * _shift_down(h, 2) + w[1:2, :] * _shift_down(h, 1) + w[2:3, :] * h + b


def _chan(gv, rows):
    return pl.BlockSpec((1, 1, rows, LANES), lambda d, j: (gv, d, 0, j))


def _ffn_act(up4, cw4, cb4):
    def body(ug_ref, uv_ref, wg_ref, wv_ref, bg_ref, bv_ref, act_ref):
        g = _conv(ug_ref[0, 0], wg_ref[0, 0], bg_ref[0, 0])
        v = _conv(uv_ref[0, 0], wv_ref[0, 0], bv_ref[0, 0])
        act_ref[0] = (g * jax.nn.sigmoid(g) * v).astype(BF16)

    return pl.pallas_call(
        body, name="ffn_act", grid=(4, _NCB),
        in_specs=[_chan(0, T), _chan(1, T), _chan(0, 3), _chan(1, 3), _chan(0, 1), _chan(1, 1)],
        out_specs=pl.BlockSpec((1, T, LANES), lambda d, j: (d, 0, j)), out_shape=_sds((4, T, CS), BF16),
        compiler_params=_cp(("parallel", "parallel")),
    )(up4, up4, cw4, cw4, cb4, cb4)


def _ffn_act_bwd(up4, dact, cw4, cb4):
    def conv_bwd(h, w, d):
        dup = w[2:3, :] * d + w[1:2, :] * _shift_up(d, 1) + w[0:1, :] * _shift_up(d, 2)
        dw = jnp.concatenate([jnp.sum(d * _shift_down(h, 2), axis=0, keepdims=True),
                              jnp.sum(d * _shift_down(h, 1), axis=0, keepdims=True),
                              jnp.sum(d * h, axis=0, keepdims=True)], axis=0)
        return dup, dw, jnp.sum(d, axis=0, keepdims=True)

    def body(ug_ref, uv_ref, da_ref, wg_ref, wv_ref, bg_ref, bv_ref, dup_ref, dw_ref, db_ref):
        ug, uv, da = ug_ref[0, 0], uv_ref[0, 0], da_ref[0]
        g = _conv(ug, wg_ref[0, 0], bg_ref[0, 0])
        v = _conv(uv, wv_ref[0, 0], bv_ref[0, 0])
        sg = jax.nn.sigmoid(g)
        dg = da * v * (sg * (1.0 + g * (1.0 - sg)))
        dv = da * (g * sg)
        dug, dwg, dbg = conv_bwd(ug, wg_ref[0, 0], dg)
        duv, dwv, dbv = conv_bwd(uv, wv_ref[0, 0], dv)
        dup_ref[0, 0] = dug.astype(BF16)
        dup_ref[1, 0] = duv.astype(BF16)
        dw_ref[0, 0] = dwg
        dw_ref[1, 0] = dwv
        db_ref[0, 0] = dbg
        db_ref[1, 0] = dbv

    both = lambda rows: pl.BlockSpec((2, 1, rows, LANES), lambda d, j: (0, d, 0, j))
    return pl.pallas_call(
        body, name="ffn_act_bwd", grid=(4, _NCB),
        in_specs=[_chan(0, T), _chan(1, T), pl.BlockSpec((1, T, LANES), lambda d, j: (d, 0, j)),
                  _chan(0, 3), _chan(1, 3), _chan(0, 1), _chan(1, 1)],
        out_specs=[both(T), both(3), both(1)],
        out_shape=[_sds((2, 4, T, CS), BF16), _sds((2, 4, 3, CS)), _sds((2, 4, 1, CS))],
        compiler_params=_cp(("parallel", "parallel")),
    )(up4, up4, dact, cw4, cw4, cb4, cb4)


def _ffn_down_loss(act, wd4, x1, target):
    tm = 256

    def body(a_ref, w_ref, x1_ref, t_ref, loss_ref, dx_ref, dxb_ref):
        x2 = x1_ref[...]
        for d in range(4):
            x2 = x2 + _nn(a_ref[d], w_ref[d])
        err = x2 - t_ref[...]
        dx = err * (1.0 / D)
        dx_ref[...] = dx
        dxb_ref[...] = dx.astype(BF16)

        @pl.when(pl.program_id(0) == 0)
        def _():
            loss_ref[...] = jnp.zeros_like(loss_ref)
        loss_ref[...] += 0.5 * jnp.sum(jnp.mean(err * err, axis=-1, keepdims=True), axis=0, keepdims=True)

    row = lambda i: (i, 0)
    fixed = lambda i: (0, 0)
    return pl.pallas_call(
        body, name="ffn_down_loss", grid=(T // tm,),
        in_specs=[pl.BlockSpec((4, tm, CS), lambda i: (0, i, 0)), pl.BlockSpec((4, CS, D), lambda i: (0, 0, 0)),
                  pl.BlockSpec((tm, D), row), pl.BlockSpec((tm, D), row)],
        out_specs=[pl.BlockSpec((1, 1), fixed), pl.BlockSpec((tm, D), row), pl.BlockSpec((tm, D), row)],
        out_shape=[_sds((1, 1)), _sds((T, D)), _sds((T, D), BF16)],
        compiler_params=_cp(("arbitrary",)),
    )(act, wd4, x1, target)


def _ffn_dact(dx2b, wd4):
    tm = 512

    def body(d_ref, w_ref, o_ref):
        o_ref[0] = _nt(d_ref[...], w_ref[0])

    return pl.pallas_call(
        body, name="ffn_dact", grid=(4, T // tm),
        in_specs=[pl.BlockSpec((tm, D), lambda d, i: (i, 0)), pl.BlockSpec((1, CS, D), lambda d, i: (d, 0, 0))],
        out_specs=pl.BlockSpec((1, tm, CS), lambda d, i: (d, i, 0)), out_shape=_sds((4, T, CS)),
        compiler_params=_cp(("parallel", "parallel")),
    )(dx2b, wd4)


def _ffn_dwd(act, dx2b):
    def body(a_ref, d_ref, o_ref):
        o_ref[0] = _tn(a_ref[0], d_ref[...]).astype(BF16)

    return pl.pallas_call(
        body, name="ffn_dwd", grid=(4,),
        in_specs=[pl.BlockSpec((1, T, CS), lambda d: (d, 0, 0)), pl.BlockSpec((T, D), lambda d: (0, 0))],
        out_specs=pl.BlockSpec((1, CS, D), lambda d: (d, 0, 0)), out_shape=_sds((4, CS, D), BF16),
        compiler_params=_cp(("parallel",)),
    )(act, dx2b)


def _ffn_dwu(h2, dup4):
    tn = 512

    def body(h_ref, d_ref, o_ref):
        o_ref[0, 0] = _tn(d_ref[0, 0], h_ref[...]).astype(BF16)

    return pl.pallas_call(
        body, name="ffn_dwu", grid=(NDEV, D // tn),
        in_specs=[pl.BlockSpec((T, tn), lambda e, i: (0, i)),
                  pl.BlockSpec((1, 1, T, CS), lambda e, i: (e // 4, e % 4, 0, 0))],
        out_specs=pl.BlockSpec((1, 1, CS, tn), lambda e, i: (e // 4, e % 4, 0, i)),
        out_shape=_sds((2, 4, CS, D), BF16),
        compiler_params=_cp(("parallel", "parallel")),
    )(h2, dup4)


def _ffn_up_bwd(dup4, wu4, x1, g, dres):
    tm = 256

    def body(d_ref, w_ref, x_ref, g_ref, dres_ref, dx_ref, dg_ref):
        dh = jnp.zeros((tm, D), F32)
        for gv in range(2):
            for d in range(4):
                dh = dh + _nn(d_ref[gv, d], w_ref[gv, d])
        xv = x_ref[...]
        r = _rms_r(xv)
        dxr, dgp = _rms_bwd(xv * r, r, g_ref[...], dh)
        dx_ref[...] = dres_ref[...] + dxr

        @pl.when(pl.program_id(0) == 0)
        def _():
            dg_ref[...] = jnp.zeros_like(dg_ref)
        dg_ref[...] += dgp

    row = lambda i: (i, 0)
    fixed = lambda i: (0, 0)
    return pl.pallas_call(
        body, name="ffn_up_bwd", grid=(T // tm,),
        in_specs=[pl.BlockSpec((2, 4, tm, CS), lambda i: (0, 0, i, 0)), pl.BlockSpec((2, 4, CS, D), lambda i: (0, 0, 0, 0)),
                  pl.BlockSpec((tm, D), row), pl.BlockSpec((1, D), fixed), pl.BlockSpec((tm, D), row)],
        out_specs=[pl.BlockSpec((tm, D), row), pl.BlockSpec((1, D), fixed)],
        out_shape=[_sds((T, D)), _sds((1, D))],
        compiler_params=_cp(("arbitrary",)),
    )(dup4, wu4, x1, g, dres)


def _adamw_math(w, g, m, v):
    m = ADAM_B1 * m + (1.0 - ADAM_B1) * g
    v = ADAM_B2 * v + (1.0 - ADAM_B2) * jnp.square(g)
    m_hat = m / (1.0 - ADAM_B1 ** ADAM_STEP)
    v_hat = v / (1.0 - ADAM_B2 ** ADAM_STEP)
    delta = -ADAM_LR * (m_hat / (jnp.sqrt(v_hat) + ADAM_EPS) + ADAM_WD * w)
    return delta, m, v


def _where_am_i():
    x, y, c = _place()
    return jnp.stack([c, 2 * x + y]).astype(jnp.int32)


def _pair_sum(a4, recv, name):
    _, _, rows, cols = a4.shape
    tr = _row_tile(rows, cols, 3 << 20)

    def body(sel_ref, own_ref, recv_ref, out_ref):
        out_ref[...] = (own_ref[0].astype(F32) + recv_ref[...].astype(F32)).astype(out_ref.dtype)

    grid_spec = pltpu.PrefetchScalarGridSpec(
        num_scalar_prefetch=1, grid=(4, rows // tr),
        in_specs=[pl.BlockSpec((1, 1, tr, cols), lambda k, i, s: (k, s[0], i, 0)),
                  pl.BlockSpec((1, tr, cols), lambda k, i, s: (k, i, 0))],
        out_specs=pl.BlockSpec((1, tr, cols), lambda k, i, s: (k, i, 0)),
    )
    return pl.pallas_call(
        body, name=name, grid_spec=grid_spec, out_shape=_sds((4, rows, cols), a4.dtype),
        compiler_params=_cp(("arbitrary", "arbitrary")),
    )(_where_am_i(), a4, recv)


def _adamw_rs(a4, recv, from_chips, w, m, v, name):
    rows, cols = w.shape
    tr = _row_tile(rows, cols, 3 << 19)

    def body(sel_ref, own_ref, recv_ref, fc_ref, w_ref, m_ref, v_ref, g_ref, d_ref, nm_ref, nv_ref):
        g = own_ref[0, 0].astype(F32) + recv_ref[0].astype(F32)
        for j in range(3):
            g = g + fc_ref[j].astype(F32)
        d, nm, nv = _adamw_math(w_ref[...], g, m_ref[...], v_ref[...])
        g_ref[...] = g
        d_ref[...] = d
        nm_ref[...] = nm
        nv_ref[...] = nv

    flat = pl.BlockSpec((tr, cols), lambda i, s: (i, 0))
    grid_spec = pltpu.PrefetchScalarGridSpec(
        num_scalar_prefetch=1, grid=(rows // tr,),
        in_specs=[pl.BlockSpec((1, 1, tr, cols), lambda i, s: (s[1], s[0], i, 0)),
                  pl.BlockSpec((1, tr, cols), lambda i, s: (s[1], i, 0)),
                  pl.BlockSpec((3, tr, cols), lambda i, s: (0, i, 0)), flat, flat, flat],
        out_specs=[flat] * 4,
    )
    return pl.pallas_call(
        body, name=name, grid_spec=grid_spec, out_shape=[_sds((rows, cols))] * 4,
        compiler_params=_cp(("arbitrary",)),
    )(_where_am_i(), a4, recv, from_chips, w, m, v)


def _adamw_small(gall, gall_b, wmv):
    rows = wmv.shape[1]
    rf = rows - gall_b.shape[1]

    def body(ga_ref, gb_ref, wmv_ref, loss_ref, out_ref):
        ga, gb = ga_ref[0], gb_ref[0].astype(F32)
        for dev in range(1, NDEV):
            ga = ga + ga_ref[dev]
            gb = gb + gb_ref[dev].astype(F32)
        g = jnp.concatenate([ga[0:rf], gb], axis=0)
        d, nm, nv = _adamw_math(wmv_ref[0], g, wmv_ref[1], wmv_ref[2])
        loss_ref[...] = ga[rf:rf + 8]
        out_ref[0] = g
        out_ref[1] = d
        out_ref[2] = nm
        out_ref[3] = nv

    return pl.pallas_call(body, name="adamw_small", out_shape=[_sds((8, LANES)), _sds((4, rows, LANES))],
                          compiler_params=_cp())(gall, gall_b, wmv)


def _rows128(a):
    r = (-a.shape[-1]) % (8 * LANES)
    if r:
        a = jnp.pad(a, [(0, 0)] * (a.ndim - 1) + [(0, r)])
    return a.reshape(*a.shape[:-1], -1, LANES)


_SMALL = (("g_mix", D), ("b_f", NH), ("g_q", DH), ("g_k", DH), ("lambda_re", NG * NP), ("lambda_im", NG * NP),
          ("log_step", NG), ("d_skip", NG * GS), ("b_glu", SW), ("g_attn_out", AW), ("g_ssm_out", SW),
          ("g_ffn", D), ("conv_b", 2 * DFF), ("b_re", NG * NP * GS), ("b_im", NG * NP * GS), ("c_re", NG * GS * NP),
          ("c_im", NG * GS * NP))
_SMALL_BF16_ROWS = 4 * NG * NP * GS // LANES


def _pack_small(*ds):
    parts = [_rows128(jnp.stack([d[name].reshape(-1).astype(F32) for d in ds])) for name, _ in _SMALL]
    p = jnp.concatenate(parts, axis=1)
    return p[0] if len(ds) == 1 else p


def _unpack_small(p, shapes):
    outs, off = [{} for _ in range(p.shape[0])], 0
    for name, n in _SMALL:
        rows = -(-n // (8 * LANES)) * 8
        for k, out in enumerate(outs):
            out[name] = p[k, off:off + rows].reshape(-1)[:n].reshape(shapes[name])
        off += rows
    return outs


def _block_diag(m):
    eye = jnp.eye(8, dtype=m.dtype)
    r, c = m.shape[2], m.shape[3]
    return (m[:, :, :, None, :] * eye[None, :, None, :, None]).reshape(4, 8 * r, 8 * c)


def _diag_blocks(dense, r, c):
    eye = jnp.eye(8, dtype=dense.dtype)
    return jnp.sum(dense.reshape(4, 8, r, 8, c) * eye[None, :, None, :, None], axis=3)


def kernel(x, g_mix, w_in, b_f, g_q, g_k, lambda_re, lambda_im, log_step, b_re, b_im, c_re, c_im, d_skip, w_glu, b_glu, g_attn_out, g_ssm_out, w_out, g_ffn, w_up, conv_w, conv_b, w_down, loss_target, m_g_mix, m_w_in, m_b_f, m_g_q, m_g_k, m_lambda_re, m_lambda_im, m_log_step, m_b_re, m_b_im, m_c_re, m_c_im, m_d_skip, m_w_glu, m_b_glu, m_g_attn_out, m_g_ssm_out, m_w_out, m_g_ffn, m_w_up, m_conv_w, m_conv_b, m_w_down, v_g_mix, v_w_in, v_b_f, v_g_q, v_g_k, v_lambda_re, v_lambda_im, v_log_step, v_b_re, v_b_im, v_c_re, v_c_im, v_d_skip, v_w_glu, v_b_glu, v_g_attn_out, v_g_ssm_out, v_w_out, v_g_ffn, v_w_up, v_conv_w, v_conv_b, v_w_down):
    weights = dict(g_mix=g_mix, w_in=w_in, b_f=b_f, g_q=g_q, g_k=g_k, lambda_re=lambda_re, lambda_im=lambda_im,
                   log_step=log_step, b_re=b_re, b_im=b_im, c_re=c_re, c_im=c_im, d_skip=d_skip, w_glu=w_glu,
                   b_glu=b_glu, g_attn_out=g_attn_out, g_ssm_out=g_ssm_out, w_out=w_out, g_ffn=g_ffn, w_up=w_up,
                   conv_w=conv_w, conv_b=conv_b, w_down=w_down)
    mom_m = dict(g_mix=m_g_mix, w_in=m_w_in, b_f=m_b_f, g_q=m_g_q, g_k=m_g_k, lambda_re=m_lambda_re,
                 lambda_im=m_lambda_im, log_step=m_log_step, b_re=m_b_re, b_im=m_b_im, c_re=m_c_re, c_im=m_c_im,
                 d_skip=m_d_skip, w_glu=m_w_glu, b_glu=m_b_glu, g_attn_out=m_g_attn_out, g_ssm_out=m_g_ssm_out,
                 w_out=m_w_out, g_ffn=m_g_ffn, w_up=m_w_up, conv_w=m_conv_w, conv_b=m_conv_b, w_down=m_w_down)
    mom_v = dict(g_mix=v_g_mix, w_in=v_w_in, b_f=v_b_f, g_q=v_g_q, g_k=v_g_k, lambda_re=v_lambda_re,
                 lambda_im=v_lambda_im, log_step=v_log_step, b_re=v_b_re, b_im=v_b_im, c_re=v_c_re, c_im=v_c_im,
                 d_skip=v_d_skip, w_glu=v_w_glu, b_glu=v_b_glu, g_attn_out=v_g_attn_out, g_ssm_out=v_g_ssm_out,
                 w_out=v_w_out, g_ffn=v_g_ffn, w_up=v_w_up, conv_w=v_conv_w, conv_b=v_conv_b, w_down=v_w_down)
    names = list(weights)
    big = ("w_in", "w_glu", "w_out", "w_up", "w_down", "conv_w")

    xs = x[0]
    target = loss_target[0]
    row = lambda a: a.reshape(1, -1)

    (win_g,) = _all_gather([w_in.T.astype(BF16)], "gather_w_in")
    w_in_t = win_g.reshape(8 * 257, D)
    wcat = jnp.concatenate([w_in_t[0:1536], w_in_t[1544:2056], jnp.pad(w_in_t[1536:1544], ((0, LANES - NH), (0, 0)))],
                           axis=0)
    cb4 = conv_b.reshape(2, 4, 1, CS)

    h1, z, f_t, (wg_g, wo_g) = _in_proj(xs, row(g_mix), wcat, [w_glu.astype(BF16), w_out.astype(BF16)])
    bf_col = b_f.reshape(NH, 1)
    ck = _gates_fwd(f_t, bf_col).reshape(_HP, 2, T)
    att, lse, (wu_g, cw_g), (wg_g, wo_g) = _attn_fwd(z, ck, row(g_q), row(g_k), [w_up.T.astype(BF16), conv_w],
                                                       [wg_g, wo_g])

    ls_col = log_step.reshape(NG, 1)
    br_t, bi_t = b_re.transpose(0, 2, 1), b_im.transpose(0, 2, 1)
    ab_re, ab_im, bb_re, bb_im = _disc_fwd(lambda_re, lambda_im, ls_col, br_t, bi_t)
    bb = jnp.concatenate([_block_diag(bb_re.reshape(4, 8, GS, NP)), _block_diag(bb_im.reshape(4, 8, GS, NP))],
                         axis=2).astype(BF16)
    ab = jnp.stack([ab_re.reshape(4, 512), ab_im.reshape(4, 512)], axis=1)
    cdiag = lambda cm: _block_diag(cm.reshape(4, 8, GS, NP).transpose(0, 1, 3, 2))
    cc = jnp.concatenate([cdiag(c_re), -cdiag(c_im)], axis=1).astype(BF16)
    dsk = d_skip.reshape(4, 1, LANES)
    u_il = _interleave(z[:, 1536:2048])
    y_il, xs_il, (wd_g,) = _ssm_fwd(u_il, bb, ab, cc, dsk, [w_down.astype(BF16)])
    y = _deinterleave(y_il)
    wg_f = wg_g.reshape(SW, SW)
    wo_f = wo_g.reshape(D, D)
    x1, (wu_g, cw_g, wd_g) = _mix_fwd(att, y, xs, wg_f, row(b_glu), row(g_attn_out), row(g_ssm_out), wo_f,
                                      [wu_g, cw_g, wd_g])
    wu4 = wu_g.reshape(2, 4, CS, D)
    wd4 = wd_g.reshape(4, CS, D)
    cw4 = cw_g.reshape(2, 4, 3, CS)
    h2, up4 = _ffn_up(x1, row(g_ffn), wu4)
    act = _ffn_act(up4, cw4, cb4)
    loss_dev, dx2, dx2b = _ffn_down_loss(act, wd4, x1, target)

    dact = _ffn_dact(dx2b, wd4)
    d_wd = _ffn_dwd(act, dx2b)
    dup4, dcw4, dcb4 = _ffn_act_bwd(up4, dact, cw4, cb4)
    d_wu = _ffn_dwu(h2, dup4)
    dx1, dg_ffn = _ffn_up_bwd(dup4, wu4, x1, row(g_ffn), dx2)
    datt, dy, d_wg, d_bg, d_ga, d_gs, d_wo = _mix_bwd(att, y, dx1, wg_f, row(b_glu), row(g_attn_out),
                                                       row(g_ssm_out), wo_f)
    early = ("w_up", "w_down", "conv_w", "w_out", "w_glu")
    by_dest = {"w_up": d_wu.reshape(4, 2, CS, D), "w_down": d_wd.reshape(4, 2, DFF // NDEV, D),
               "conv_w": dcw4.reshape(4, 2, 3, CS), "w_out": d_wo.reshape(4, 2, D // NDEV, D),
               "w_glu": d_wg.reshape(4, 2, SW // NDEV, SW)}
    du_il, dbb, dab, dcc, ddsk, fs = _ssm_bwd(u_il, _interleave(dy), xs_il, bb, ab, cc, dsk, [by_dest[n] for n in early])
    from_sibling = dict(zip(early, fs))
    chip_sums = {n: _pair_sum(by_dest[n], from_sibling[n], "rs_pair_sum_" + n) for n in early}
    du = _deinterleave(du_il).astype(BF16)
    dbb_re = _diag_blocks(dbb[:, :, 0:512], GS, NP).reshape(NG, GS, NP)
    dbb_im = _diag_blocks(dbb[:, :, 512:1024], GS, NP).reshape(NG, GS, NP)
    dlr, dli, dls, dbr_t, dbi_t = _disc_bwd(lambda_re, lambda_im, ls_col, br_t, bi_t,
                                            dab[:, 0].reshape(NG, NP), dab[:, 1].reshape(NG, NP), dbb_re, dbb_im)
    d_cre = _diag_blocks(dcc[:, 0:512], NP, GS).transpose(0, 1, 3, 2).reshape(NG, GS, NP)
    d_cim = -_diag_blocks(dcc[:, 512:1024], NP, GS).transpose(0, 1, 3, 2).reshape(NG, GS, NP)

    dq, dk, dv, dck, d_gq, d_gk, fc = _attn_bwd(z, ck, row(g_q), row(g_k), lse, datt, [chip_sums[n] for n in early])
    from_chips = dict(zip(early, fc))
    df_t, d_bf = _gates_bwd(dck.reshape(NH, T), f_t, bf_col)
    df = jnp.concatenate([df_t.T, jnp.zeros((T, LANES - NH), F32)], axis=1).astype(BF16)
    d_wcat = _in_proj_dw(h1, dq, dk, dv, du, df)
    d_win = jnp.concatenate([d_wcat[0:1536], d_wcat[2048:2048 + NH], d_wcat[1536:2048]], axis=0)
    by_dest["w_in"] = d_win.astype(BF16).reshape(4, 2, 257, D)
    (from_sibling["w_in"],) = _swap_sibling([by_dest["w_in"]], "rs_pair_swap_w_in")
    chip_sums["w_in"] = _pair_sum(by_dest["w_in"], from_sibling["w_in"], "rs_pair_sum_w_in")
    grad_x, dg_mix, (from_chips["w_in"],) = _in_proj_bwd(dq, dk, dv, du, df, wcat, xs, row(g_mix), dx1, [chip_sums["w_in"]])

    grads, deltas, new_m, new_v = {}, {}, {}, {}
    for n in early + ("w_in",):
        flip = (lambda a: a.T) if n in ("w_up", "w_in") else (lambda a: a)
        outs = _adamw_rs(by_dest[n], from_sibling[n], from_chips[n], flip(weights[n]), flip(mom_m[n]), flip(mom_v[n]),
                         "adamw_" + n)
        grads[n], deltas[n], new_m[n], new_v[n] = (flip(o) for o in outs)

    small = dict(g_mix=dg_mix, b_f=d_bf, g_q=d_gq, g_k=d_gk, lambda_re=dlr, lambda_im=dli, log_step=dls,
                 b_re=dbr_t.transpose(0, 2, 1), b_im=dbi_t.transpose(0, 2, 1), c_re=d_cre, c_im=d_cim, d_skip=ddsk,
                 b_glu=d_bg, g_attn_out=d_ga, g_ssm_out=d_gs, g_ffn=dg_ffn, conv_b=dcb4)
    loss_tile = jnp.pad(loss_dev, ((0, 7), (0, LANES - 1)))
    small_p = _pack_small(small)
    rf = small_p.shape[0] - _SMALL_BF16_ROWS
    small_all, small_all_b = _all_gather([jnp.concatenate([small_p[0:rf], loss_tile], axis=0), small_p[rf:].astype(BF16)],
                                         "gather_small_grads")
    loss_p, small_out = _adamw_small(small_all, small_all_b, _pack_small(weights, mom_m, mom_v))
    loss = loss_p[0, 0]
    sg, sd, sm, sv = _unpack_small(small_out, {n: weights[n].shape for n in small})
    for n in small:
        grads[n], deltas[n], new_m[n], new_v[n] = sg[n], sd[n], sm[n], sv[n]

    return (loss, grad_x[None], *[grads[n] for n in names], *[deltas[n] for n in names],
            *[new_m[n] for n in names], *[new_v[n] for n in names])
```

```python
import jax
import jax.numpy as jnp
from jax import lax
from jax.experimental import pallas as pl
from jax.experimental.pallas import tpu as pltpu

F32, BF16 = jnp.float32, jnp.bfloat16
T, D = 2048, 1024
NH, DH = 8, 64
AW, SW = 512, 512
NG, GS, NP = 32, 16, 64
DFF = 2816
NDEV = 8
CS = 2 * DFF // NDEV
EPS = 1e-6
NEG = -1e30
ZC = 4 * 512 + 128
SEG = 256
NSEG = T // SEG
BQ = 256
VMEM_LIMIT = 56 * 1024 * 1024
LANES = 128
MESH = pl.DeviceIdType.MESH
HI = lax.Precision.HIGHEST

ADAM_LR, ADAM_B1, ADAM_B2, ADAM_EPS, ADAM_WD, ADAM_STEP = 0.001, 0.9, 0.999, 1e-08, 0.01, 10


def _cp(dims=None):
    if dims is None:
        return pltpu.CompilerParams(vmem_limit_bytes=VMEM_LIMIT)
    return pltpu.CompilerParams(dimension_semantics=dims, vmem_limit_bytes=VMEM_LIMIT)


def _sds(shape, dtype=F32):
    return jax.ShapeDtypeStruct(shape, dtype)


def _nt(a, b):
    return lax.dot_general(a, b, (((1,), (1,)), ((), ())), preferred_element_type=F32)


def _tn(a, b):
    return lax.dot_general(a, b, (((0,), (0,)), ((), ())), preferred_element_type=F32)


def _nn(a, b):
    return jnp.dot(a, b, preferred_element_type=F32)


def _rms_r(x):
    return lax.rsqrt(jnp.mean(x * x, axis=-1, keepdims=True) + EPS)


def _rms_bwd(n, r, g, dh):
    dn = dh * g
    dx = r * (dn - n * jnp.mean(dn * n, axis=-1, keepdims=True))
    return dx, jnp.sum(dh * n, axis=0, keepdims=True)


def _row_tile(rows, cols, limit):
    tr = rows
    while tr * cols * 4 > limit and tr % 32 == 0:
        tr //= 2
    return tr


def _place():
    return lax.axis_index("x"), lax.axis_index("y"), lax.axis_index("c")


def _all_gather(shards, name):
    n = len(shards)

    def body(*refs):
        x_refs, out_refs = refs[:n], refs[n:2 * n]
        send_sems, recv_sems, local_sems = refs[2 * n:]
        x, y, c = _place()
        me, sibling = (x, y, c), (x, y, 1 - c)
        chips = [(1 - x, y), (x, 1 - y), (1 - x, 1 - y)]

        def copy(a, k, block, to, src=None):
            px, py, pc = block
            slot = out_refs[a].at[4 * px + 2 * py + pc]
            return pltpu.make_async_remote_copy(
                src_ref=slot if src is None else src, dst_ref=slot,
                send_sem=send_sems.at[7 * a + k], recv_sem=recv_sems.at[7 * a + k], device_id=to, device_id_type=MESH)

        mine, first, passed = [], [], []
        for a in range(n):
            cp = pltpu.make_async_copy(x_refs[a], out_refs[a].at[4 * x + 2 * y + c], local_sems.at[a])
            cp.start()
            mine.append(cp)
            cps = [copy(a, 0, me, sibling, src=x_refs[a])]
            cps += [copy(a, 1 + j, me, (*chip, c), src=x_refs[a]) for j, chip in enumerate(chips)]
            for cp in cps:
                cp.start()
            first += cps
        for a in range(n):
            for j, chip in enumerate(chips):
                copy(a, 1 + j, (*chip, c), me).wait_recv()
                fwd = copy(a, 4 + j, (*chip, c), sibling)
                fwd.start()
                passed.append(fwd)
        for a in range(n):
            copy(a, 0, sibling, me).wait_recv()
            for j, chip in enumerate(chips):
                copy(a, 4 + j, (*chip, 1 - c), me).wait_recv()
        for cp in first + passed:
            cp.wait_send()
        for cp in mine:
            cp.wait()

    anyspec = pl.BlockSpec(memory_space=pl.ANY)
    return pl.pallas_call(
        body, name=name,
        out_shape=[_sds((NDEV, *s.shape), s.dtype) for s in shards],
        in_specs=[anyspec] * n, out_specs=[anyspec] * n,
        scratch_shapes=[pltpu.SemaphoreType.DMA((7 * n,)), pltpu.SemaphoreType.DMA((7 * n,)),
                        pltpu.SemaphoreType.DMA((n,))],
    )(*shards)


def _gather_first_copies(x_refs, out_refs, send_sems, recv_sems, local_sems):
    x, y, c = _place()
    slot = 4 * x + 2 * y + c
    peers = [(x, y, 1 - c), (1 - x, y, c), (x, 1 - y, c), (1 - x, 1 - y, c)]
    local, remote = [], []
    for a, (x_ref, out_ref) in enumerate(zip(x_refs, out_refs)):
        local.append(pltpu.make_async_copy(x_ref, out_ref.at[slot], local_sems.at[a]))
        for k, peer in enumerate(peers):
            remote.append(pltpu.make_async_remote_copy(
                src_ref=x_ref, dst_ref=out_ref.at[slot], send_sem=send_sems.at[4 * a + k],
                recv_sem=recv_sems.at[4 * a + k], device_id=peer, device_id_type=MESH))
    return local, remote


def _gather_first_scratch(n):
    return [pltpu.SemaphoreType.DMA((4 * n,)), pltpu.SemaphoreType.DMA((4 * n,)), pltpu.SemaphoreType.DMA((n,))]


def _chip_swap_copies(p_refs, out_refs, send_sems, recv_sems):
    x, y, c = _place()
    chips = [(1 - x, y), (x, 1 - y), (1 - x, 1 - y)]
    cps = []
    for a, (p_ref, out_ref) in enumerate(zip(p_refs, out_refs)):
        for j, (px, py) in enumerate(chips):
            cps.append(pltpu.make_async_remote_copy(
                src_ref=p_ref.at[2 * px + py], dst_ref=out_ref.at[j],
                send_sem=send_sems.at[3 * a + j], recv_sem=recv_sems.at[3 * a + j],
                device_id=(px, py, c), device_id_type=MESH))
    return cps


def _sibling_swap_copies(a_refs, out_refs, send_sems, recv_sems):
    x, y, c = _place()
    cps = []
    for a, (a_ref, out_ref) in enumerate(zip(a_refs, out_refs)):
        for k in range(4):
            cps.append(pltpu.make_async_remote_copy(
                src_ref=a_ref.at[k, 1 - c], dst_ref=out_ref.at[k],
                send_sem=send_sems.at[4 * a + k], recv_sem=recv_sems.at[4 * a + k],
                device_id=(x, y, 1 - c), device_id_type=MESH))
    return cps


def _pass_on_copies(buf_refs, send_sems, recv_sems):
    x, y, c = _place()
    chips = [(1 - x, y), (x, 1 - y), (1 - x, 1 - y)]
    cps = []
    for a, buf in enumerate(buf_refs):
        for j, (px, py) in enumerate(chips):
            block = buf.at[4 * px + 2 * py + c]
            cps.append(pltpu.make_async_remote_copy(
                src_ref=block, dst_ref=block, send_sem=send_sems.at[3 * a + j], recv_sem=recv_sems.at[3 * a + j],
                device_id=(x, y, 1 - c), device_id_type=MESH))
    return cps


def _swap_sibling(arrs, name):
    n = len(arrs)

    def body(*refs):
        cps = _sibling_swap_copies(refs[:n], refs[n:2 * n], *refs[2 * n:])
        for cp in cps:
            cp.start()
        for cp in cps:
            cp.wait()

    anyspec = pl.BlockSpec(memory_space=pl.ANY)
    return pl.pallas_call(
        body, name=name,
        out_shape=[_sds((4, *a.shape[2:]), a.dtype) for a in arrs],
        in_specs=[anyspec] * n, out_specs=[anyspec] * n,
        scratch_shapes=[pltpu.SemaphoreType.DMA((4 * n,)), pltpu.SemaphoreType.DMA((4 * n,))],
    )(*arrs)


def _in_proj(x, g, wcat, shards):
    tm = 256
    n = len(shards)

    def body(x_ref, g_ref, w_ref, *rest):
        x_refs, (h_ref, z_ref, ft_ref), out_refs, sems = rest[:n], rest[n:n + 3], rest[n + 3:2 * n + 3], rest[2 * n + 3:]

        @pl.when(pl.program_id(0) == 0)
        def _():
            local, remote = _gather_first_copies(x_refs, out_refs, *sems)
            for cp in local + remote:
                cp.start()

        xv = x_ref[...]
        h = (xv * _rms_r(xv) * g_ref[...]).astype(BF16)
        h_ref[...] = h
        z = _nt(h, w_ref[...])
        z_ref[...] = z
        ft_ref[...] = z[:, 2048:ZC].T[0:NH, :]

        @pl.when(pl.program_id(0) == T // tm - 1)
        def _():
            local, remote = _gather_first_copies(x_refs, out_refs, *sems)
            for cp in remote + local:
                cp.wait()

    row = lambda i: (i, 0)
    fixed = lambda i: (0, 0)
    anyspec = pl.BlockSpec(memory_space=pl.ANY)
    res = pl.pallas_call(
        body, name="in_proj", grid=(T // tm,),
        in_specs=[pl.BlockSpec((tm, D), row), pl.BlockSpec((1, D), fixed), pl.BlockSpec((ZC, D), fixed)] + [anyspec] * n,
        out_specs=[pl.BlockSpec((tm, D), row), pl.BlockSpec((tm, ZC), row), pl.BlockSpec((NH, tm), lambda i: (0, i))]
        + [anyspec] * n,
        out_shape=[_sds((T, D), BF16), _sds((T, ZC)), _sds((NH, T))] + [_sds((NDEV, *s.shape), s.dtype) for s in shards],
        scratch_shapes=_gather_first_scratch(n),
        compiler_params=_cp(("arbitrary",)),
    )(x, g, wcat, *shards)
    return res[0], res[1], res[2], res[3:]


def _in_proj_dw(h, dq, dk, dv, du, df):
    tm = 512

    def body(h_ref, dq_ref, dk_ref, dv_ref, du_ref, df_ref, o_ref):
        hv = h_ref[...]
        for j, d_ref in enumerate((dq_ref, dk_ref, dv_ref, du_ref)):
            o_ref[512 * j:512 * (j + 1), :] = _tn(d_ref[...], hv)
        o_ref[2048:ZC, :] = _tn(df_ref[...], hv)

    full = lambda w: pl.BlockSpec((T, w), lambda i: (0, 0))
    return pl.pallas_call(
        body, name="in_proj_dw", grid=(D // tm,),
        in_specs=[pl.BlockSpec((T, tm), lambda i: (0, i)), full(512), full(512), full(512), full(512), full(LANES)],
        out_specs=pl.BlockSpec((ZC, tm), lambda i: (0, i)), out_shape=_sds((ZC, D)),
        compiler_params=_cp(("parallel",)),
    )(h, dq, dk, dv, du, df)


def _in_proj_bwd(dq, dk, dv, du, df, wcat, x, g, dres, parts):
    tm = 256
    n = len(parts)

    def body(dq_ref, dk_ref, dv_ref, du_ref, df_ref, w_ref, x_ref, g_ref, dres_ref, *rest):
        p_refs, (dx_ref, dg_ref), fc_refs, sems = rest[:n], rest[n:n + 2], rest[n + 2:2 * n + 2], rest[2 * n + 2:]

        @pl.when(pl.program_id(0) == 0)
        def _():
            for cp in _chip_swap_copies(p_refs, fc_refs, *sems):
                cp.start()

        dh = _nn(df_ref[...], w_ref[2048:ZC, :])
        for j, d_ref in enumerate((dq_ref, dk_ref, dv_ref, du_ref)):
            dh = dh + _nn(d_ref[...], w_ref[512 * j:512 * (j + 1), :])
        xv = x_ref[...]
        r = _rms_r(xv)
        dxr, dgp = _rms_bwd(xv * r, r, g_ref[...], dh)
        dx_ref[...] = dres_ref[...] + dxr

        @pl.when(pl.program_id(0) == 0)
        def _():
            dg_ref[...] = jnp.zeros_like(dg_ref)
        dg_ref[...] += dgp

        @pl.when(pl.program_id(0) == T // tm - 1)
        def _():
            for cp in _chip_swap_copies(p_refs, fc_refs, *sems):
                cp.wait()

    row = lambda i: (i, 0)
    fixed = lambda i: (0, 0)
    part = pl.BlockSpec((tm, 512), row)
    anyspec = pl.BlockSpec(memory_space=pl.ANY)
    res = pl.pallas_call(
        body, name="in_proj_bwd", grid=(T // tm,),
        in_specs=[part, part, part, part, pl.BlockSpec((tm, LANES), row), pl.BlockSpec((ZC, D), fixed),
                  pl.BlockSpec((tm, D), row), pl.BlockSpec((1, D), fixed), pl.BlockSpec((tm, D), row)] + [anyspec] * n,
        out_specs=[pl.BlockSpec((tm, D), row), pl.BlockSpec((1, D), fixed)] + [anyspec] * n,
        out_shape=[_sds((T, D)), _sds((1, D))] + [_sds((3, *p.shape[1:]), p.dtype) for p in parts],
        scratch_shapes=[pltpu.SemaphoreType.DMA((3 * n,)), pltpu.SemaphoreType.DMA((3 * n,))],
        compiler_params=_cp(("arbitrary",)),
    )(dq, dk, dv, du, df, wcat, x, g, dres, *parts)
    return res[0], res[1], res[2:]


def _stack_lanes(v):
    return jnp.concatenate([v[:, LANES * b:LANES * (b + 1)] for b in range(T // LANES)], axis=0)


def _unstack_lanes(s):
    return jnp.concatenate([s[8 * b:8 * b + 8, :] for b in range(T // LANES)], axis=1)


def _cumsum_lanes(v, reverse):
    st = _stack_lanes(v)
    ri = lax.broadcasted_iota(jnp.int32, (LANES, LANES), 0)
    ci = lax.broadcasted_iota(jnp.int32, (LANES, LANES), 1)
    tri = (ri >= ci) if reverse else (ri <= ci)
    intra = jnp.dot(st, tri.astype(F32), precision=HI, preferred_element_type=F32)
    tot = intra[:, 0:1] if reverse else intra[:, LANES - 1:LANES]
    same_head = (ci % 8) == (ri % 8)
    other = (ci // 8 > ri // 8) if reverse else (ci // 8 < ri // 8)
    off = jnp.dot((same_head & other).astype(F32), jnp.broadcast_to(tot, (LANES, LANES)), precision=HI,
                  preferred_element_type=F32)
    return _unstack_lanes(intra + off)


def _gates_fwd(f_t, b_f):
    def body(f_ref, b_ref, c_ref):
        z = f_ref[...] + b_ref[...]
        lf = jnp.minimum(z, 0.0) - jnp.log(1.0 + jnp.exp(-jnp.abs(z)))
        c_ref[...] = _cumsum_lanes(lf, reverse=False)

    return pl.pallas_call(body, name="gates_fwd", out_shape=_sds((NH, T)), compiler_params=_cp())(f_t, b_f)


def _gates_bwd(dck, f_t, b_f):
    def body(d_ref, f_ref, b_ref, df_ref, db_ref):
        z = f_ref[...] + b_ref[...]
        dlf = _cumsum_lanes(d_ref[...], reverse=True)
        df = dlf * jax.nn.sigmoid(-z)
        df_ref[...] = df
        db_ref[...] = jnp.sum(df, axis=1, keepdims=True)

    return pl.pallas_call(body, name="gates_bwd", out_shape=[_sds((NH, T)), _sds((NH, 1))],
                          compiler_params=_cp())(dck, f_t, b_f)


def _lower_triangle():
    rows = lax.broadcasted_iota(jnp.int32, (BQ, BQ), 0)
    cols = lax.broadcasted_iota(jnp.int32, (BQ, BQ), 1)
    return cols <= rows


def _logit_parts(qb, kb, ck_row, lo, hi, tri):
    parts = []
    if lo > 0:
        parts.append((_nt(qb[lo:hi], kb[0:lo]) - ck_row[:, 0:lo], 0, lo))
    parts.append((jnp.where(tri, _nt(qb[lo:hi], kb[lo:hi]) - ck_row[:, lo:hi], NEG), lo, hi))
    return parts


_HP = NH // 2


def _qkv_specs():
    return [pl.BlockSpec((T, LANES), lambda p: (0, p)), pl.BlockSpec((T, LANES), lambda p: (0, _HP + p)),
            pl.BlockSpec((T, LANES), lambda p: (0, 2 * _HP + p))]


def _attn_fwd(z, ck, g_q, g_k, shards, bufs):
    scale = DH ** -0.5
    n, m = len(shards), len(bufs)

    def body(q_ref, k_ref, v_ref, ck_ref, gq_ref, gk_ref, *rest):
        x_refs, (o_ref, lse_ref) = rest[:n], rest[n + m:n + m + 2]
        out_refs, buf_refs, sems = rest[n + m + 2:2 * n + m + 2], rest[2 * n + m + 2:2 * n + 2 * m + 2], rest[2 * n + 2 * m + 2:]

        @pl.when(pl.program_id(0) == 0)
        def _():
            local, remote = _gather_first_copies(x_refs, out_refs, *sems[:3])
            for cp in local + remote + _pass_on_copies(buf_refs, *sems[3:]):
                cp.start()

        qb, kb, vb = [], [], []
        for hh in range(2):
            cols = slice(DH * hh, DH * (hh + 1))
            qv, kv = q_ref[:, cols], k_ref[:, cols]
            qb.append((qv * _rms_r(qv) * gq_ref[...] * scale).astype(BF16))
            kb.append((kv * _rms_r(kv) * gk_ref[...]).astype(BF16))
            vb.append(v_ref[:, cols].astype(BF16))
        tri = _lower_triangle()
        for i in range(T // BQ):
            lo, hi = i * BQ, (i + 1) * BQ
            outs = []
            for hh in range(2):
                parts = _logit_parts(qb[hh], kb[hh], ck_ref[0, hh:hh + 1, :], lo, hi, tri)
                top = jnp.max(parts[-1][0], axis=-1, keepdims=True)
                if lo > 0:
                    top = jnp.maximum(top, jnp.max(parts[0][0], axis=-1, keepdims=True))
                l, o = 0.0, 0.0
                for s, k0, k1 in parts:
                    p = jnp.exp(s - top)
                    l = l + jnp.sum(p, axis=-1, keepdims=True)
                    o = o + _nn(p.astype(BF16), vb[hh][k0:k1])
                outs.append(o / l)
                lse_ref[0, lo:hi, hh:hh + 1] = top + jnp.log(l)
            o_ref[lo:hi, :] = jnp.concatenate(outs, axis=1)

        @pl.when(pl.program_id(0) == _HP - 1)
        def _():
            local, remote = _gather_first_copies(x_refs, out_refs, *sems[:3])
            for cp in remote + local + _pass_on_copies(buf_refs, *sems[3:]):
                cp.wait()

    fixed = lambda p: (0, 0)
    anyspec = pl.BlockSpec(memory_space=pl.ANY)
    res = pl.pallas_call(
        body, name="attn_fwd", grid=(_HP,),
        in_specs=_qkv_specs() + [pl.BlockSpec((1, 2, T), lambda p: (p, 0, 0)), pl.BlockSpec((1, DH), fixed),
                                 pl.BlockSpec((1, DH), fixed)] + [anyspec] * (n + m),
        out_specs=[pl.BlockSpec((T, LANES), lambda p: (0, p)), pl.BlockSpec((1, T, 2), lambda p: (p, 0, 0))]
        + [anyspec] * (n + m),
        out_shape=[_sds((T, AW)), _sds((_HP, T, 2))] + [_sds((NDEV, *s.shape), s.dtype) for s in shards]
        + [_sds(b.shape, b.dtype) for b in bufs],
        input_output_aliases={6 + n + i: 2 + n + i for i in range(m)},
        scratch_shapes=_gather_first_scratch(n) + [pltpu.SemaphoreType.DMA((3 * m,)), pltpu.SemaphoreType.DMA((3 * m,))],
        compiler_params=_cp(("arbitrary",)),
    )(z, z, z, ck, g_q, g_k, *shards, *bufs)
    return res[0], res[1], res[2:2 + n], res[2 + n:]


def _attn_bwd(z, ck, g_q, g_k, lse, datt, parts):
    scale = DH ** -0.5
    n = len(parts)

    def body(q_ref, k_ref, v_ref, ck_ref, gq_ref, gk_ref, lse_ref, do_ref, *rest):
        p_refs, rest = rest[:n], rest[n:]
        dq_ref, dk_ref, dv_ref, dck_ref, dgq_ref, dgk_ref = rest[:6]
        fc_refs, (dkn_acc, dv_acc, dc_acc), sems = rest[6:6 + n], rest[6 + n:9 + n], rest[9 + n:]

        @pl.when(pl.program_id(0) == 0)
        def _():
            for cp in _chip_swap_copies(p_refs, fc_refs, *sems):
                cp.start()

        nq, nk, rq, rk, qb, kb, vb, dob = [], [], [], [], [], [], [], []
        for hh in range(2):
            cols = slice(DH * hh, DH * (hh + 1))
            qv, kv = q_ref[:, cols], k_ref[:, cols]
            rq.append(_rms_r(qv))
            rk.append(_rms_r(kv))
            nq.append(qv * rq[hh])
            nk.append(kv * rk[hh])
            qb.append((nq[hh] * gq_ref[...] * scale).astype(BF16))
            kb.append((nk[hh] * gk_ref[...]).astype(BF16))
            vb.append(v_ref[:, cols].astype(BF16))
            dob.append(do_ref[:, cols].astype(BF16))
        dkn_acc[...] = jnp.zeros_like(dkn_acc)
        dv_acc[...] = jnp.zeros_like(dv_acc)
        dc_acc[...] = jnp.zeros_like(dc_acc)
        dgq = jnp.zeros((1, DH), F32)
        tri = _lower_triangle()
        for i in range(T // BQ):
            lo, hi = i * BQ, (i + 1) * BQ
            dqs = []
            for hh in range(2):
                lse = lse_ref[0, lo:hi, hh:hh + 1]
                pd = []
                for s, k0, k1 in _logit_parts(qb[hh], kb[hh], ck_ref[0, hh:hh + 1, :], lo, hi, tri):
                    pd.append((jnp.exp(s - lse), _nt(dob[hh][lo:hi], vb[hh][k0:k1]), k0, k1))
                mean = sum(jnp.sum(p * dp, axis=-1, keepdims=True) for p, dp, _, _ in pd)
                dqn = 0.0
                for p, dp, k0, k1 in pd:
                    ds = p * (dp - mean)
                    dsb = ds.astype(BF16)
                    dv_acc[hh, k0:k1, :] += _tn(p.astype(BF16), dob[hh][lo:hi])
                    dkn_acc[hh, k0:k1, :] += _tn(dsb, qb[hh][lo:hi])
                    dc_acc[hh:hh + 1, k0:k1] -= jnp.sum(ds, axis=0, keepdims=True)
                    dqn = dqn + _nn(dsb, kb[hh][k0:k1])
                dqx, dgp = _rms_bwd(nq[hh][lo:hi], rq[hh][lo:hi], gq_ref[...], dqn * scale)
                dqs.append(dqx)
                dgq = dgq + dgp
            dq_ref[lo:hi, :] = jnp.concatenate(dqs, axis=1).astype(BF16)
        dks, dgk = [], jnp.zeros((1, DH), F32)
        for hh in range(2):
            dkx, dgp = _rms_bwd(nk[hh], rk[hh], gk_ref[...], dkn_acc[hh])
            dks.append(dkx)
            dgk = dgk + dgp
        dk_ref[...] = jnp.concatenate(dks, axis=1).astype(BF16)
        dv_ref[...] = jnp.concatenate([dv_acc[0], dv_acc[1]], axis=1).astype(BF16)
        dck_ref[0] = dc_acc[...]

        @pl.when(pl.program_id(0) == 0)
        def _():
            dgq_ref[...] = jnp.zeros_like(dgq_ref)
            dgk_ref[...] = jnp.zeros_like(dgk_ref)
        dgq_ref[...] += dgq
        dgk_ref[...] += dgk

        @pl.when(pl.program_id(0) == _HP - 1)
        def _():
            for cp in _chip_swap_copies(p_refs, fc_refs, *sems):
                cp.wait()

    fixed = lambda p: (0, 0)
    pair = pl.BlockSpec((T, LANES), lambda p: (0, p))
    gsp = pl.BlockSpec((1, DH), fixed)
    ckspec = pl.BlockSpec((1, 2, T), lambda p: (p, 0, 0))
    anyspec = pl.BlockSpec(memory_space=pl.ANY)
    res = pl.pallas_call(
        body, name="attn_bwd", grid=(_HP,),
        in_specs=_qkv_specs() + [ckspec, gsp, gsp, pl.BlockSpec((1, T, 2), lambda p: (p, 0, 0)), pair] + [anyspec] * n,
        out_specs=[pair, pair, pair, ckspec, gsp, gsp] + [anyspec] * n,
        out_shape=[_sds((T, AW), BF16)] * 3 + [_sds((_HP, 2, T)), _sds((1, DH)), _sds((1, DH))]
        + [_sds((3, *p.shape[1:]), p.dtype) for p in parts],
        scratch_shapes=[pltpu.VMEM((2, T, DH), F32), pltpu.VMEM((2, T, DH), F32), pltpu.VMEM((2, T), F32),
                        pltpu.SemaphoreType.DMA((3 * n,)), pltpu.SemaphoreType.DMA((3 * n,))],
        compiler_params=_cp(("arbitrary",)),
    )(z, z, z, ck, g_q, g_k, lse, datt, *parts)
    return (*res[:6], res[6:])


def _disc(lr, li, ls, br_t, bi_t):
    step = jnp.exp(ls)
    er = jnp.exp(lr * step)
    ab_re = er * jnp.cos(li * step)
    ab_im = er * jnp.sin(li * step)
    num_re = ab_re - 1.0
    num_im = ab_im
    den = lr * lr + li * li
    f_re = (num_re * lr + num_im * li) / den
    f_im = (num_im * lr - num_re * li) / den
    bb_re = f_re[:, None, :] * br_t - f_im[:, None, :] * bi_t
    bb_im = f_re[:, None, :] * bi_t + f_im[:, None, :] * br_t
    return ab_re, ab_im, bb_re, bb_im


def _disc_fwd(lr, li, ls, br_t, bi_t):
    def body(lr_ref, li_ref, ls_ref, br_ref, bi_ref, ar_ref, ai_ref, bbr_ref, bbi_ref):
        ar, ai, bbr, bbi = _disc(lr_ref[...], li_ref[...], ls_ref[...], br_ref[...], bi_ref[...])
        ar_ref[...] = ar
        ai_ref[...] = ai
        bbr_ref[...] = bbr
        bbi_ref[...] = bbi

    return pl.pallas_call(
        body, name="ssm_disc_fwd",
        out_shape=[_sds((NG, NP)), _sds((NG, NP)), _sds((NG, GS, NP)), _sds((NG, GS, NP))],
        compiler_params=_cp())(lr, li, ls, br_t, bi_t)


def _disc_bwd(lr, li, ls, br_t, bi_t, dar, dai, dbbr, dbbi):
    def body(lr_ref, li_ref, ls_ref, br_ref, bi_ref, dar_ref, dai_ref, dbbr_ref, dbbi_ref,
             dlr_ref, dli_ref, dls_ref, dbr_ref, dbi_ref):
        _, vjp = jax.vjp(_disc, lr_ref[...], li_ref[...], ls_ref[...], br_ref[...], bi_ref[...])
        dlr, dli, dls, dbr, dbi = vjp((dar_ref[...], dai_ref[...], dbbr_ref[...], dbbi_ref[...]))
        dlr_ref[...] = dlr
        dli_ref[...] = dli
        dls_ref[...] = dls
        dbr_ref[...] = dbr
        dbi_ref[...] = dbi

    return pl.pallas_call(
        body, name="ssm_disc_bwd",
        out_shape=[_sds((NG, NP)), _sds((NG, NP)), _sds((NG, 1)), _sds((NG, GS, NP)), _sds((NG, GS, NP))],
        compiler_params=_cp())(lr, li, ls, br_t, bi_t, dar, dai, dbbr, dbbi)


def _cmul(ar, ai, br, bi):
    return ar * br - ai * bi, ar * bi + ai * br


def _scan(buf, a_re, a_im, reverse, other=None):
    ar = [jnp.broadcast_to(a_re[:, LANES * k:LANES * (k + 1)], (NSEG, LANES)) for k in range(4)]
    ai = [jnp.broadcast_to(a_im[:, LANES * k:LANES * (k + 1)], (NSEG, LANES)) for k in range(4)]

    def tile(ref, i, k):
        return ref[pl.ds(pl.multiple_of(i * NSEG, NSEG), NSEG), LANES * k:LANES * (k + 1)]

    def advance(i, xr, xi):
        nr = [ar[k] * xr[k] - ai[k] * xi[k] + tile(buf, i, k) for k in range(4)]
        ni = [ar[k] * xi[k] + ai[k] * xr[k] + tile(buf, i, 4 + k) for k in range(4)]
        return nr, ni

    def index(i):
        return (SEG - 1 - i) if reverse else i

    zeros = tuple(jnp.zeros((NSEG, LANES), F32) for _ in range(4))

    def sweep1(i, carry):
        nr, ni = advance(index(i), carry[0], carry[1])
        return tuple(nr), tuple(ni)

    er, ei = lax.fori_loop(0, SEG, sweep1, (zeros, zeros), unroll=4)

    row = lax.broadcasted_iota(jnp.int32, (NSEG, LANES), 0)
    sr, si = [], []
    for k in range(4):
        pr, pi = ar[k], ai[k]
        for _ in range(SEG.bit_length() - 1):
            pr, pi = _cmul(pr, pi, pr, pi)
        ir, ii = er[k], ei[k]
        for d in (1, 2, 4):
            shift = (NSEG - d) if reverse else d
            ok = (row < NSEG - d) if reverse else (row >= d)
            mr, mi = _cmul(pr, pi, pltpu.roll(ir, shift, 0), pltpu.roll(ii, shift, 0))
            ir = ir + jnp.where(ok, mr, 0.0)
            ii = ii + jnp.where(ok, mi, 0.0)
            pr, pi = _cmul(pr, pi, pr, pi)
        shift = (NSEG - 1) if reverse else 1
        ok = (row < NSEG - 1) if reverse else (row >= 1)
        sr.append(jnp.where(ok, pltpu.roll(ir, shift, 0), 0.0))
        si.append(jnp.where(ok, pltpu.roll(ii, shift, 0), 0.0))

    def sweep2(i, carry):
        xr, xi, dar, dai = carry
        idx = index(i)
        if other is not None:
            orr = [tile(other, idx, k) for k in range(4)]
            oi = [tile(other, idx, 4 + k) for k in range(4)]
            dar = tuple(dar[k] + xr[k] * orr[k] + xi[k] * oi[k] for k in range(4))
            dai = tuple(dai[k] + xi[k] * orr[k] - xr[k] * oi[k] for k in range(4))
        nr, ni = advance(idx, xr, xi)
        start = pl.multiple_of(idx * NSEG, NSEG)
        for k in range(4):
            buf[pl.ds(start, NSEG), LANES * k:LANES * (k + 1)] = nr[k]
            buf[pl.ds(start, NSEG), LANES * (4 + k):LANES * (5 + k)] = ni[k]
        return tuple(nr), tuple(ni), dar, dai

    _, _, dar, dai = lax.fori_loop(0, SEG, sweep2, (tuple(sr), tuple(si), zeros, zeros), unroll=4)
    if other is None:
        return None
    da_re = jnp.concatenate([jnp.sum(d, axis=0, keepdims=True) for d in dar], axis=1)
    da_im = jnp.concatenate([jnp.sum(d, axis=0, keepdims=True) for d in dai], axis=1)
    return da_re, da_im


_SSM_BLOCKS = 4


def _ssm_fwd(u, bb, ab, cc, dsk, shards):
    n = len(shards)

    def body(u_ref, bb_ref, a_ref, cc_ref, d_ref, *rest):
        x_refs, y_ref, xs_ref, out_refs, sems = rest[:n], rest[n], rest[n + 1], rest[n + 2:2 * n + 2], rest[2 * n + 2:]

        @pl.when(pl.program_id(0) == 0)
        def _():
            local, remote = _gather_first_copies(x_refs, out_refs, *sems)
            for cp in local + remote:
                cp.start()

        ub = u_ref[...]
        xb = xs_ref.at[0]
        xb[...] = _nn(ub.astype(BF16), bb_ref[0])
        _scan(xb, a_ref[0, 0:1, :], a_ref[0, 1:2, :], reverse=False)
        y_ref[...] = _nn(xb[...].astype(BF16), cc_ref[0]) + d_ref[0] * ub

        @pl.when(pl.program_id(0) == _SSM_BLOCKS - 1)
        def _():
            local, remote = _gather_first_copies(x_refs, out_refs, *sems)
            for cp in remote + local:
                cp.wait()

    blk = lambda j: (j, 0, 0)
    col = pl.BlockSpec((T, LANES), lambda j: (0, j))
    anyspec = pl.BlockSpec(memory_space=pl.ANY)
    res = pl.pallas_call(
        body, name="ssm_fwd", grid=(_SSM_BLOCKS,),
        in_specs=[col, pl.BlockSpec((1, LANES, 1024), blk), pl.BlockSpec((1, 2, 512), blk),
                  pl.BlockSpec((1, 1024, LANES), blk), pl.BlockSpec((1, 1, LANES), blk)] + [anyspec] * n,
        out_specs=[col, pl.BlockSpec((1, T, 1024), blk)] + [anyspec] * n,
        out_shape=[_sds((T, SW)), _sds((_SSM_BLOCKS, T, 1024))] + [_sds((NDEV, *s.shape), s.dtype) for s in shards],
        scratch_shapes=_gather_first_scratch(n),
        compiler_params=_cp(("arbitrary",)),
    )(u, bb, ab, cc, dsk, *shards)
    return res[0], res[1], res[2:]


def _ssm_bwd(u, dy, xs, bb, ab, cc, dsk, arrs):
    n = len(arrs)

    def body(u_ref, dy_ref, x_ref, bb_ref, a_ref, cc_ref, d_ref, *rest):
        a_refs, rest = rest[:n], rest[n:]
        du_ref, dbb_ref, da_ref, dcc_ref, dd_ref = rest[:5]
        fs_refs, gb, sems = rest[5:5 + n], rest[5 + n], rest[6 + n:]

        @pl.when(pl.program_id(0) == 0)
        def _():
            for cp in _sibling_swap_copies(a_refs, fs_refs, *sems):
                cp.start()

        ub, dyv = u_ref[...], dy_ref[...]
        ubb, dyb = ub.astype(BF16), dyv.astype(BF16)
        a_re, a_im = a_ref[0, 0:1, :], a_ref[0, 1:2, :]
        dcc_ref[0] = _tn(x_ref[0].astype(BF16), dyb)
        gb[...] = _nt(dyb, cc_ref[0])
        da_re, da_im = _scan(gb, a_re, -a_im, reverse=True, other=x_ref.at[0])
        da_ref[0] = jnp.concatenate([da_re, da_im], axis=0)
        gc = gb[...].astype(BF16)
        du_ref[...] = _nt(gc, bb_ref[0]) + d_ref[0] * dyv
        dbb_ref[0] = _tn(ubb, gc)
        dd_ref[0] = jnp.sum(dyv * ub, axis=0, keepdims=True)

        @pl.when(pl.program_id(0) == _SSM_BLOCKS - 1)
        def _():
            for cp in _sibling_swap_copies(a_refs, fs_refs, *sems):
                cp.wait()

    blk = lambda j: (j, 0, 0)
    col = pl.BlockSpec((T, LANES), lambda j: (0, j))
    anyspec = pl.BlockSpec(memory_space=pl.ANY)
    res = pl.pallas_call(
        body, name="ssm_bwd", grid=(_SSM_BLOCKS,),
        in_specs=[col, col, pl.BlockSpec((1, T, 1024), blk), pl.BlockSpec((1, LANES, 1024), blk),
                  pl.BlockSpec((1, 2, 512), blk), pl.BlockSpec((1, 1024, LANES), blk), pl.BlockSpec((1, 1, LANES), blk)]
        + [anyspec] * n,
        out_specs=[col, pl.BlockSpec((1, LANES, 1024), blk), pl.BlockSpec((1, 2, 512), blk),
                   pl.BlockSpec((1, 1024, LANES), blk), pl.BlockSpec((1, 1, LANES), blk)] + [anyspec] * n,
        out_shape=[_sds((T, SW)), _sds((4, LANES, 1024)), _sds((4, 2, 512)), _sds((4, 1024, LANES)), _sds((4, 1, LANES))]
        + [_sds((4, *a.shape[2:]), a.dtype) for a in arrs],
        scratch_shapes=[pltpu.VMEM((T, 1024), F32), pltpu.SemaphoreType.DMA((4 * n,)), pltpu.SemaphoreType.DMA((4 * n,))],
        compiler_params=_cp(("arbitrary",)),
    )(u, dy, xs, bb, ab, cc, dsk, *arrs)
    return (*res[:5], res[5:])


def _interleave(a):
    return a.reshape(NSEG, SEG, a.shape[1]).transpose(1, 0, 2).reshape(a.shape)


def _deinterleave(a):
    return a.reshape(SEG, NSEG, a.shape[1]).transpose(1, 0, 2).reshape(a.shape)


_TM_MIX = 256


def _mix_parts(att, y, wg, bg):
    y2 = jax.nn.gelu(y)
    y2b = y2.astype(BF16)
    sg = jax.nn.sigmoid(_nn(y2b, wg) + bg)
    ssm = y2 * sg
    ra, rs = _rms_r(att), _rms_r(ssm)
    return y2, y2b, sg, ssm, ra, rs


def _mix_fwd(att, y, x, wg, bg, ga, gs, wo, bufs):
    m = len(bufs)

    def body(att_ref, y_ref, x_ref, wg_ref, bg_ref, ga_ref, gs_ref, wo_ref, *rest):
        x1_ref, buf_refs, sems = rest[m], rest[m + 1:2 * m + 1], rest[2 * m + 1:]

        @pl.when(pl.program_id(0) == 0)
        def _():
            for cp in _pass_on_copies(buf_refs, *sems):
                cp.start()

        attv = att_ref[...]
        _, _, _, ssm, ra, rs = _mix_parts(attv, y_ref[...], wg_ref[...], bg_ref[...])
        ma = (attv * ra * ga_ref[...]).astype(BF16)
        ms = (ssm * rs * gs_ref[...]).astype(BF16)
        x1_ref[...] = x_ref[...] + _nn(ma, wo_ref[0:AW, :]) + _nn(ms, wo_ref[AW:D, :])

        @pl.when(pl.program_id(0) == T // _TM_MIX - 1)
        def _():
            for cp in _pass_on_copies(buf_refs, *sems):
                cp.wait()

    row = lambda i: (i, 0)
    fixed = lambda i: (0, 0)
    half = pl.BlockSpec((_TM_MIX, AW), row)
    vec = pl.BlockSpec((1, AW), fixed)
    anyspec = pl.BlockSpec(memory_space=pl.ANY)
    res = pl.pallas_call(
        body, name="mix_fwd", grid=(T // _TM_MIX,),
        in_specs=[half, half, pl.BlockSpec((_TM_MIX, D), row), pl.BlockSpec((SW, SW), fixed), vec, vec, vec,
                  pl.BlockSpec((D, D), fixed)] + [anyspec] * m,
        out_specs=[pl.BlockSpec((_TM_MIX, D), row)] + [anyspec] * m,
        out_shape=[_sds((T, D))] + [_sds(b.shape, b.dtype) for b in bufs],
        input_output_aliases={8 + i: 1 + i for i in range(m)},
        scratch_shapes=[pltpu.SemaphoreType.DMA((3 * m,)), pltpu.SemaphoreType.DMA((3 * m,))],
        compiler_params=_cp(("arbitrary",)),
    )(att, y, x, wg, bg, ga, gs, wo, *bufs)
    return res[0], res[1:]


def _mix_bwd(att, y, dx1, wg, bg, ga, gs, wo):
    last = T // _TM_MIX - 1

    def body(att_ref, y_ref, d_ref, wg_ref, bg_ref, ga_ref, gs_ref, wo_ref,
             datt_ref, dy_ref, dwg_ref, dbg_ref, dga_ref, dgs_ref, dwo_ref, dwg_acc, dwo_acc):
        attv, yv, db = att_ref[...], y_ref[...], d_ref[...].astype(BF16)
        y2, y2b, sg, ssm, ra, rs = _mix_parts(attv, yv, wg_ref[...], bg_ref[...])
        na, ns = attv * ra, ssm * rs
        ma = (na * ga_ref[...]).astype(BF16)
        ms = (ns * gs_ref[...]).astype(BF16)
        dma = _nt(db, wo_ref[0:AW, :])
        dms = _nt(db, wo_ref[AW:D, :])
        datt, dga = _rms_bwd(na, ra, ga_ref[...], dma)
        dssm, dgs = _rms_bwd(ns, rs, gs_ref[...], dms)
        dgl = dssm * y2 * sg * (1.0 - sg)
        dglb = dgl.astype(BF16)
        dy2 = dssm * sg + _nt(dglb, wg_ref[...])
        _, gelu_vjp = jax.vjp(jax.nn.gelu, yv)
        datt_ref[...] = datt
        dy_ref[...] = gelu_vjp(dy2)[0]

        @pl.when(pl.program_id(0) == 0)
        def _():
            dwg_acc[...] = jnp.zeros_like(dwg_acc)
            dwo_acc[...] = jnp.zeros_like(dwo_acc)
            dbg_ref[...] = jnp.zeros_like(dbg_ref)
            dga_ref[...] = jnp.zeros_like(dga_ref)
            dgs_ref[...] = jnp.zeros_like(dgs_ref)
        dwg_acc[...] += _tn(y2b, dglb)
        dbg_ref[...] += jnp.sum(dgl, axis=0, keepdims=True)
        dga_ref[...] += dga
        dgs_ref[...] += dgs
        dwo_acc[0:AW, :] += _tn(ma, db)
        dwo_acc[AW:D, :] += _tn(ms, db)

        @pl.when(pl.program_id(0) == last)
        def _():
            dwg_ref[...] = dwg_acc[...].astype(BF16)
            dwo_ref[...] = dwo_acc[...].astype(BF16)

    row = lambda i: (i, 0)
    fixed = lambda i: (0, 0)
    half = pl.BlockSpec((_TM_MIX, AW), row)
    vec = pl.BlockSpec((1, AW), fixed)
    sq = pl.BlockSpec((SW, SW), fixed)
    big = pl.BlockSpec((D, D), fixed)
    return pl.pallas_call(
        body, name="mix_bwd", grid=(T // _TM_MIX,),
        in_specs=[half, half, pl.BlockSpec((_TM_MIX, D), row), sq, vec, vec, vec, big],
        out_specs=[half, half, sq, vec, vec, vec, big],
        out_shape=[_sds((T, AW)), _sds((T, SW)), _sds((SW, SW), BF16), _sds((1, SW)), _sds((1, AW)), _sds((1, SW)),
                   _sds((D, D), BF16)],
        scratch_shapes=[pltpu.VMEM((SW, SW), F32), pltpu.VMEM((D, D), F32)],
        compiler_params=_cp(("arbitrary",)),
    )(att, y, dx1, wg, bg, ga, gs, wo)


_NCB = -(-CS // LANES)


def _ffn_up(x1, g, wu4):
    tm = 512

    def body(x_ref, g_ref, w_ref, h_ref, up_ref):
        @pl.when(pl.program_id(1) == 0)
        def _():
            xv = x_ref[...]
            h_ref[...] = (xv * _rms_r(xv) * g_ref[...]).astype(BF16)
        up_ref[0, 0] = _nt(h_ref[...], w_ref[0, 0])

    return pl.pallas_call(
        body, name="ffn_up", grid=(T // tm, NDEV),
        in_specs=[pl.BlockSpec((tm, D), lambda i, e: (i, 0)), pl.BlockSpec((1, D), lambda i, e: (0, 0)),
                  pl.BlockSpec((1, 1, CS, D), lambda i, e: (e // 4, e % 4, 0, 0))],
        out_specs=[pl.BlockSpec((tm, D), lambda i, e: (i, 0)),
                   pl.BlockSpec((1, 1, tm, CS), lambda i, e: (e // 4, e % 4, i, 0))],
        out_shape=[_sds((T, D), BF16), _sds((2, 4, T, CS))],
        compiler_params=_cp(("parallel", "arbitrary")),
    )(x1, g, wu4)


def _shift_down(h, k):
    row = lax.broadcasted_iota(jnp.int32, h.shape, 0)
    return jnp.where(row >= k, pltpu.roll(h, k, 0), 0.0)


def _shift_up(h, k):
    row = lax.broadcasted_iota(jnp.int32, h.shape, 0)
    return jnp.where(row < T - k, pltpu.roll(h, T - k, 0), 0.0)


def _conv(h, w, b):
    return w[0:1, :] * _shift_down(h, 2) + w[1:2, :] * _shift_down(h, 1) + w[2:3, :] * h + b


def _chan(gv, rows):
    return pl.BlockSpec((1, 1, rows, LANES), lambda d, j: (gv, d, 0, j))


def _ffn_act(up4, cw4, cb4):
    def body(ug_ref, uv_ref, wg_ref, wv_ref, bg_ref, bv_ref, act_ref, act_t_ref):
        g = _conv(ug_ref[0, 0], wg_ref[0, 0], bg_ref[0, 0])
        v = _conv(uv_ref[0, 0], wv_ref[0, 0], bv_ref[0, 0])
        act = g * jax.nn.sigmoid(g) * v
        act_ref[0] = act.astype(BF16)
        act_t_ref[0] = act.T.astype(BF16)

    return pl.pallas_call(
        body, name="ffn_act", grid=(4, _NCB),
        in_specs=[_chan(0, T), _chan(1, T), _chan(0, 3), _chan(1, 3), _chan(0, 1), _chan(1, 1)],
        out_specs=[pl.BlockSpec((1, T, LANES), lambda d, j: (d, 0, j)), pl.BlockSpec((1, LANES, T), lambda d, j: (d, j, 0))],
        out_shape=[_sds((4, T, CS), BF16), _sds((4, CS, T), BF16)],
        compiler_params=_cp(("parallel", "parallel")),
    )(up4, up4, cw4, cw4, cb4, cb4)


def _ffn_act_bwd(up4, dact, cw4, cb4):
    def conv_bwd(h, w, d):
        dup = w[2:3, :] * d + w[1:2, :] * _shift_up(d, 1) + w[0:1, :] * _shift_up(d, 2)
        dw = jnp.concatenate([jnp.sum(d * _shift_down(h, 2), axis=0, keepdims=True),
                              jnp.sum(d * _shift_down(h, 1), axis=0, keepdims=True),
                              jnp.sum(d * h, axis=0, keepdims=True)], axis=0)
        return dup, dw, jnp.sum(d, axis=0, keepdims=True)

    def body(ug_ref, uv_ref, da_ref, wg_ref, wv_ref, bg_ref, bv_ref, dup_ref, dup_t_ref, dw_ref, db_ref):
        ug, uv, da = ug_ref[0, 0], uv_ref[0, 0], da_ref[0]
        g = _conv(ug, wg_ref[0, 0], bg_ref[0, 0])
        v = _conv(uv, wv_ref[0, 0], bv_ref[0, 0])
        sg = jax.nn.sigmoid(g)
        dg = da * v * (sg * (1.0 + g * (1.0 - sg)))
        dv = da * (g * sg)
        dug, dwg, dbg = conv_bwd(ug, wg_ref[0, 0], dg)
        duv, dwv, dbv = conv_bwd(uv, wv_ref[0, 0], dv)
        dup_ref[0, 0] = dug.astype(BF16)
        dup_ref[1, 0] = duv.astype(BF16)
        dup_t_ref[0, 0] = dug.T.astype(BF16)
        dup_t_ref[1, 0] = duv.T.astype(BF16)
        dw_ref[0, 0] = dwg
        dw_ref[1, 0] = dwv
        db_ref[0, 0] = dbg
        db_ref[1, 0] = dbv

    both = lambda rows: pl.BlockSpec((2, 1, rows, LANES), lambda d, j: (0, d, 0, j))
    return pl.pallas_call(
        body, name="ffn_act_bwd", grid=(4, _NCB),
        in_specs=[_chan(0, T), _chan(1, T), pl.BlockSpec((1, T, LANES), lambda d, j: (d, 0, j)),
                  _chan(0, 3), _chan(1, 3), _chan(0, 1), _chan(1, 1)],
        out_specs=[both(T), pl.BlockSpec((2, 1, LANES, T), lambda d, j: (0, d, j, 0)), both(3), both(1)],
        out_shape=[_sds((2, 4, T, CS), BF16), _sds((2, 4, CS, T), BF16), _sds((2, 4, 3, CS)), _sds((2, 4, 1, CS))],
        compiler_params=_cp(("parallel", "parallel")),
    )(up4, up4, dact, cw4, cw4, cb4, cb4)


def _ffn_down_loss(act, wd4, x1, target):
    tm = 256

    def body(a_ref, w_ref, x1_ref, t_ref, loss_ref, dx_ref, dxb_ref):
        x2 = x1_ref[...]
        for d in range(4):
            x2 = x2 + _nn(a_ref[d], w_ref[d])
        err = x2 - t_ref[...]
        dx = err * (1.0 / D)
        dx_ref[...] = dx
        dxb_ref[...] = dx.astype(BF16)

        @pl.when(pl.program_id(0) == 0)
        def _():
            loss_ref[...] = jnp.zeros_like(loss_ref)
        loss_ref[...] += 0.5 * jnp.sum(jnp.mean(err * err, axis=-1, keepdims=True), axis=0, keepdims=True)

    row = lambda i: (i, 0)
    fixed = lambda i: (0, 0)
    return pl.pallas_call(
        body, name="ffn_down_loss", grid=(T // tm,),
        in_specs=[pl.BlockSpec((4, tm, CS), lambda i: (0, i, 0)), pl.BlockSpec((4, CS, D), lambda i: (0, 0, 0)),
                  pl.BlockSpec((tm, D), row), pl.BlockSpec((tm, D), row)],
        out_specs=[pl.BlockSpec((1, 1), fixed), pl.BlockSpec((tm, D), row), pl.BlockSpec((tm, D), row)],
        out_shape=[_sds((1, 1)), _sds((T, D)), _sds((T, D), BF16)],
        compiler_params=_cp(("arbitrary",)),
    )(act, wd4, x1, target)


def _ffn_dact(dx2b, wd4):
    tm = 512

    def body(d_ref, w_ref, o_ref):
        o_ref[0] = _nt(d_ref[...], w_ref[0])

    return pl.pallas_call(
        body, name="ffn_dact", grid=(4, T // tm),
        in_specs=[pl.BlockSpec((tm, D), lambda d, i: (i, 0)), pl.BlockSpec((1, CS, D), lambda d, i: (d, 0, 0))],
        out_specs=pl.BlockSpec((1, tm, CS), lambda d, i: (d, i, 0)), out_shape=_sds((4, T, CS)),
        compiler_params=_cp(("parallel", "parallel")),
    )(dx2b, wd4)


def _ffn_dwd(act_t, dx2b):
    def body(a_ref, d_ref, o_ref):
        o_ref[0] = _nn(a_ref[0], d_ref[...]).astype(BF16)

    return pl.pallas_call(
        body, name="ffn_dwd", grid=(4,),
        in_specs=[pl.BlockSpec((1, CS, T), lambda d: (d, 0, 0)), pl.BlockSpec((T, D), lambda d: (0, 0))],
        out_specs=pl.BlockSpec((1, CS, D), lambda d: (d, 0, 0)), out_shape=_sds((4, CS, D), BF16),
        compiler_params=_cp(("parallel",)),
    )(act_t, dx2b)


def _ffn_dwu(h2, dup_t4):
    tn = 512

    def body(h_ref, d_ref, o_ref):
        o_ref[0, 0] = _nn(d_ref[0, 0], h_ref[...]).astype(BF16)

    return pl.pallas_call(
        body, name="ffn_dwu", grid=(NDEV, D // tn),
        in_specs=[pl.BlockSpec((T, tn), lambda e, i: (0, i)),
                  pl.BlockSpec((1, 1, CS, T), lambda e, i: (e // 4, e % 4, 0, 0))],
        out_specs=pl.BlockSpec((1, 1, CS, tn), lambda e, i: (e // 4, e % 4, 0, i)),
        out_shape=_sds((2, 4, CS, D), BF16),
        compiler_params=_cp(("parallel", "parallel")),
    )(h2, dup_t4)


def _ffn_up_bwd(dup4, wu4, x1, g, dres):
    tm = 256

    def body(d_ref, w_ref, x_ref, g_ref, dres_ref, dx_ref, dg_ref):
        dh = jnp.zeros((tm, D), F32)
        for gv in range(2):
            for d in range(4):
                dh = dh + _nn(d_ref[gv, d], w_ref[gv, d])
        xv = x_ref[...]
        r = _rms_r(xv)
        dxr, dgp = _rms_bwd(xv * r, r, g_ref[...], dh)
        dx_ref[...] = dres_ref[...] + dxr

        @pl.when(pl.program_id(0) == 0)
        def _():
            dg_ref[...] = jnp.zeros_like(dg_ref)
        dg_ref[...] += dgp

    row = lambda i: (i, 0)
    fixed = lambda i: (0, 0)
    return pl.pallas_call(
        body, name="ffn_up_bwd", grid=(T // tm,),
        in_specs=[pl.BlockSpec((2, 4, tm, CS), lambda i: (0, 0, i, 0)), pl.BlockSpec((2, 4, CS, D), lambda i: (0, 0, 0, 0)),
                  pl.BlockSpec((tm, D), row), pl.BlockSpec((1, D), fixed), pl.BlockSpec((tm, D), row)],
        out_specs=[pl.BlockSpec((tm, D), row), pl.BlockSpec((1, D), fixed)],
        out_shape=[_sds((T, D)), _sds((1, D))],
        compiler_params=_cp(("arbitrary",)),
    )(dup4, wu4, x1, g, dres)


def _adamw_math(w, g, m, v):
    m = ADAM_B1 * m + (1.0 - ADAM_B1) * g
    v = ADAM_B2 * v + (1.0 - ADAM_B2) * jnp.square(g)
    m_hat = m / (1.0 - ADAM_B1 ** ADAM_STEP)
    v_hat = v / (1.0 - ADAM_B2 ** ADAM_STEP)
    delta = -ADAM_LR * (m_hat / (jnp.sqrt(v_hat) + ADAM_EPS) + ADAM_WD * w)
    return delta, m, v


def _where_am_i():
    x, y, c = _place()
    return jnp.stack([c, 2 * x + y]).astype(jnp.int32)


def _pair_sum(a4, recv, name):
    _, _, rows, cols = a4.shape
    tr = _row_tile(rows, cols, 3 << 20)

    def body(sel_ref, own_ref, recv_ref, out_ref):
        out_ref[...] = (own_ref[0].astype(F32) + recv_ref[...].astype(F32)).astype(out_ref.dtype)

    grid_spec = pltpu.PrefetchScalarGridSpec(
        num_scalar_prefetch=1, grid=(4, rows // tr),
        in_specs=[pl.BlockSpec((1, 1, tr, cols), lambda k, i, s: (k, s[0], i, 0)),
                  pl.BlockSpec((1, tr, cols), lambda k, i, s: (k, i, 0))],
        out_specs=pl.BlockSpec((1, tr, cols), lambda k, i, s: (k, i, 0)),
    )
    return pl.pallas_call(
        body, name=name, grid_spec=grid_spec, out_shape=_sds((4, rows, cols), a4.dtype),
        compiler_params=_cp(("arbitrary", "arbitrary")),
    )(_where_am_i(), a4, recv)


def _adamw_rs(a4, recv, from_chips, w, m, v, name):
    rows, cols = w.shape
    tr = _row_tile(rows, cols, 3 << 19)

    def body(sel_ref, own_ref, recv_ref, fc_ref, w_ref, m_ref, v_ref, g_ref, d_ref, nm_ref, nv_ref):
        g = own_ref[0, 0].astype(F32) + recv_ref[0].astype(F32)
        for j in range(3):
            g = g + fc_ref[j].astype(F32)
        d, nm, nv = _adamw_math(w_ref[...], g, m_ref[...], v_ref[...])
        g_ref[...] = g
        d_ref[...] = d
        nm_ref[...] = nm
        nv_ref[...] = nv

    flat = pl.BlockSpec((tr, cols), lambda i, s: (i, 0))
    grid_spec = pltpu.PrefetchScalarGridSpec(
        num_scalar_prefetch=1, grid=(rows // tr,),
        in_specs=[pl.BlockSpec((1, 1, tr, cols), lambda i, s: (s[1], s[0], i, 0)),
                  pl.BlockSpec((1, tr, cols), lambda i, s: (s[1], i, 0)),
                  pl.BlockSpec((3, tr, cols), lambda i, s: (0, i, 0)), flat, flat, flat],
        out_specs=[flat] * 4,
    )
    return pl.pallas_call(
        body, name=name, grid_spec=grid_spec, out_shape=[_sds((rows, cols))] * 4,
        compiler_params=_cp(("arbitrary",)),
    )(_where_am_i(), a4, recv, from_chips, w, m, v)


_SMALL_F32 = (("g_mix", (8, 128)), ("b_f", (1, 8)), ("g_q", (1, 64)), ("g_k", (1, 64)), ("lambda_re", (32, 64)),
              ("lambda_im", (32, 64)), ("log_step", (1, 32)), ("d_skip", (32, 16)), ("b_glu", (4, 128)),
              ("g_attn_out", (4, 128)), ("g_ssm_out", (4, 128)), ("g_ffn", (8, 128)), ("conv_b", (44, 128)))
_SMALL_PAIRS = (("b_re", "b_im"), ("c_re", "c_im"))
_PAIR_SHAPE = (NG * GS, NP)


def _ceil8(r):
    return -(-r // 8) * 8


def _small_names():
    return [n for n, _ in _SMALL_F32] + [n for pair in _SMALL_PAIRS for n in pair]


def _pack_small_grads(g, loss_dev):
    parts = [jnp.pad(g[n].reshape(r, c), ((0, _ceil8(r) - r), (0, LANES - c))) for n, (r, c) in _SMALL_F32]
    parts.append(jnp.pad(loss_dev, ((0, 7), (0, LANES - 1))))
    pairs = [jnp.concatenate([g[a].reshape(_PAIR_SHAPE), g[b].reshape(_PAIR_SHAPE)], axis=1) for a, b in _SMALL_PAIRS]
    return jnp.concatenate(parts, axis=0), jnp.concatenate(pairs, axis=0).astype(BF16)


def _adamw_small(gall, gall_b, wmv):
    names = _small_names()
    shapes = [s for _, s in _SMALL_F32] + [_PAIR_SHAPE] * (2 * len(_SMALL_PAIRS))
    npar = len(names)

    def body(ga_ref, gb_ref, *rest):
        ins, loss_ref, outs = rest[:3 * npar], rest[3 * npar], rest[3 * npar + 1:]
        ga, gb = ga_ref[0], gb_ref[0].astype(F32)
        for dev in range(1, NDEV):
            ga = ga + ga_ref[dev]
            gb = gb + gb_ref[dev].astype(F32)
        grads, off = [], 0
        for _, (r, c) in _SMALL_F32:
            grads.append(ga[off:off + r, 0:c])
            off += _ceil8(r)
        loss_ref[...] = ga[off:off + 8]
        rows, cols = _PAIR_SHAPE
        for j in range(len(_SMALL_PAIRS)):
            for h in range(2):
                grads.append(gb[rows * j:rows * (j + 1), cols * h:cols * (h + 1)])
        for i, g in enumerate(grads):
            d, nm, nv = _adamw_math(ins[3 * i][...], g, ins[3 * i + 1][...], ins[3 * i + 2][...])
            outs[4 * i][...] = g
            outs[4 * i + 1][...] = d
            outs[4 * i + 2][...] = nm
            outs[4 * i + 3][...] = nv

    flat = [a for n in names for a in wmv[n]]
    out_shape = [_sds((8, LANES))] + [_sds(s) for s in shapes for _ in range(4)]
    res = pl.pallas_call(body, name="adamw_small", out_shape=out_shape, compiler_params=_cp())(gall, gall_b, *flat)
    return res[0], {n: res[1 + 4 * i:5 + 4 * i] for i, n in enumerate(names)}


def _block_diag(m):
    eye = jnp.eye(8, dtype=m.dtype)
    r, c = m.shape[2], m.shape[3]
    return (m[:, :, :, None, :] * eye[None, :, None, :, None]).reshape(4, 8 * r, 8 * c)


def _diag_blocks(dense, r, c):
    eye = jnp.eye(8, dtype=dense.dtype)
    return jnp.sum(dense.reshape(4, 8, r, 8, c) * eye[None, :, None, :, None], axis=3)


def kernel(x, g_mix, w_in, b_f, g_q, g_k, lambda_re, lambda_im, log_step, b_re, b_im, c_re, c_im, d_skip, w_glu, b_glu, g_attn_out, g_ssm_out, w_out, g_ffn, w_up, conv_w, conv_b, w_down, loss_target, m_g_mix, m_w_in, m_b_f, m_g_q, m_g_k, m_lambda_re, m_lambda_im, m_log_step, m_b_re, m_b_im, m_c_re, m_c_im, m_d_skip, m_w_glu, m_b_glu, m_g_attn_out, m_g_ssm_out, m_w_out, m_g_ffn, m_w_up, m_conv_w, m_conv_b, m_w_down, v_g_mix, v_w_in, v_b_f, v_g_q, v_g_k, v_lambda_re, v_lambda_im, v_log_step, v_b_re, v_b_im, v_c_re, v_c_im, v_d_skip, v_w_glu, v_b_glu, v_g_attn_out, v_g_ssm_out, v_w_out, v_g_ffn, v_w_up, v_conv_w, v_conv_b, v_w_down):
    weights = dict(g_mix=g_mix, w_in=w_in, b_f=b_f, g_q=g_q, g_k=g_k, lambda_re=lambda_re, lambda_im=lambda_im,
                   log_step=log_step, b_re=b_re, b_im=b_im, c_re=c_re, c_im=c_im, d_skip=d_skip, w_glu=w_glu,
                   b_glu=b_glu, g_attn_out=g_attn_out, g_ssm_out=g_ssm_out, w_out=w_out, g_ffn=g_ffn, w_up=w_up,
                   conv_w=conv_w, conv_b=conv_b, w_down=w_down)
    mom_m = dict(g_mix=m_g_mix, w_in=m_w_in, b_f=m_b_f, g_q=m_g_q, g_k=m_g_k, lambda_re=m_lambda_re,
                 lambda_im=m_lambda_im, log_step=m_log_step, b_re=m_b_re, b_im=m_b_im, c_re=m_c_re, c_im=m_c_im,
                 d_skip=m_d_skip, w_glu=m_w_glu, b_glu=m_b_glu, g_attn_out=m_g_attn_out, g_ssm_out=m_g_ssm_out,
                 w_out=m_w_out, g_ffn=m_g_ffn, w_up=m_w_up, conv_w=m_conv_w, conv_b=m_conv_b, w_down=m_w_down)
    mom_v = dict(g_mix=v_g_mix, w_in=v_w_in, b_f=v_b_f, g_q=v_g_q, g_k=v_g_k, lambda_re=v_lambda_re,
                 lambda_im=v_lambda_im, log_step=v_log_step, b_re=v_b_re, b_im=v_b_im, c_re=v_c_re, c_im=v_c_im,
                 d_skip=v_d_skip, w_glu=v_w_glu, b_glu=v_b_glu, g_attn_out=v_g_attn_out, g_ssm_out=v_g_ssm_out,
                 w_out=v_w_out, g_ffn=v_g_ffn, w_up=v_w_up, conv_w=v_conv_w, conv_b=v_conv_b, w_down=v_w_down)
    names = list(weights)
    big = ("w_in", "w_glu", "w_out", "w_up", "w_down", "conv_w")

    xs = x[0]
    target = loss_target[0]
    row = lambda a: a.reshape(1, -1)

    (win_g,) = _all_gather([w_in.T.astype(BF16)], "gather_w_in")
    w_in_t = win_g.reshape(8 * 257, D)
    wcat = jnp.concatenate([w_in_t[0:1536], w_in_t[1544:2056], jnp.pad(w_in_t[1536:1544], ((0, LANES - NH), (0, 0)))],
                           axis=0)
    cb4 = conv_b.reshape(2, 4, 1, CS)

    h1, z, f_t, (wg_g, wo_g) = _in_proj(xs, row(g_mix), wcat, [w_glu.astype(BF16), w_out.astype(BF16)])
    bf_col = b_f.reshape(NH, 1)
    ck = _gates_fwd(f_t, bf_col).reshape(_HP, 2, T)
    att, lse, (wu_g, cw_g), (wg_g, wo_g) = _attn_fwd(z, ck, row(g_q), row(g_k), [w_up.T.astype(BF16), conv_w],
                                                       [wg_g, wo_g])

    ls_col = log_step.reshape(NG, 1)
    br_t, bi_t = b_re.transpose(0, 2, 1), b_im.transpose(0, 2, 1)
    ab_re, ab_im, bb_re, bb_im = _disc_fwd(lambda_re, lambda_im, ls_col, br_t, bi_t)
    bb = jnp.concatenate([_block_diag(bb_re.reshape(4, 8, GS, NP)), _block_diag(bb_im.reshape(4, 8, GS, NP))],
                         axis=2).astype(BF16)
    ab = jnp.stack([ab_re.reshape(4, 512), ab_im.reshape(4, 512)], axis=1)
    cdiag = lambda cm: _block_diag(cm.reshape(4, 8, GS, NP).transpose(0, 1, 3, 2))
    cc = jnp.concatenate([cdiag(c_re), -cdiag(c_im)], axis=1).astype(BF16)
    dsk = d_skip.reshape(4, 1, LANES)
    u_il = _interleave(z[:, 1536:2048])
    y_il, xs_il, (wd_g,) = _ssm_fwd(u_il, bb, ab, cc, dsk, [w_down.astype(BF16)])
    y = _deinterleave(y_il)
    wg_f = wg_g.reshape(SW, SW)
    wo_f = wo_g.reshape(D, D)
    x1, (wu_g, cw_g, wd_g) = _mix_fwd(att, y, xs, wg_f, row(b_glu), row(g_attn_out), row(g_ssm_out), wo_f,
                                      [wu_g, cw_g, wd_g])
    wu4 = wu_g.reshape(2, 4, CS, D)
    wd4 = wd_g.reshape(4, CS, D)
    cw4 = cw_g.reshape(2, 4, 3, CS)
    h2, up4 = _ffn_up(x1, row(g_ffn), wu4)
    act, act_t = _ffn_act(up4, cw4, cb4)
    loss_dev, dx2, dx2b = _ffn_down_loss(act, wd4, x1, target)

    dact = _ffn_dact(dx2b, wd4)
    d_wd = _ffn_dwd(act_t, dx2b)
    dup4, dup_t4, dcw4, dcb4 = _ffn_act_bwd(up4, dact, cw4, cb4)
    d_wu = _ffn_dwu(h2, dup_t4)
    dx1, dg_ffn = _ffn_up_bwd(dup4, wu4, x1, row(g_ffn), dx2)
    datt, dy, d_wg, d_bg, d_ga, d_gs, d_wo = _mix_bwd(att, y, dx1, wg_f, row(b_glu), row(g_attn_out),
                                                       row(g_ssm_out), wo_f)
    early = ("w_up", "w_down", "conv_w", "w_out", "w_glu")
    by_dest = {"w_up": d_wu.reshape(4, 2, CS, D), "w_down": d_wd.reshape(4, 2, DFF // NDEV, D),
               "conv_w": dcw4.reshape(4, 2, 3, CS), "w_out": d_wo.reshape(4, 2, D // NDEV, D),
               "w_glu": d_wg.reshape(4, 2, SW // NDEV, SW)}
    du_il, dbb, dab, dcc, ddsk, fs = _ssm_bwd(u_il, _interleave(dy), xs_il, bb, ab, cc, dsk, [by_dest[n] for n in early])
    from_sibling = dict(zip(early, fs))
    chip_sums = {n: _pair_sum(by_dest[n], from_sibling[n], "rs_pair_sum_" + n) for n in early}
    du = _deinterleave(du_il).astype(BF16)
    dbb_re = _diag_blocks(dbb[:, :, 0:512], GS, NP).reshape(NG, GS, NP)
    dbb_im = _diag_blocks(dbb[:, :, 512:1024], GS, NP).reshape(NG, GS, NP)
    dlr, dli, dls, dbr_t, dbi_t = _disc_bwd(lambda_re, lambda_im, ls_col, br_t, bi_t,
                                            dab[:, 0].reshape(NG, NP), dab[:, 1].reshape(NG, NP), dbb_re, dbb_im)
    d_cre = _diag_blocks(dcc[:, 0:512], NP, GS).transpose(0, 1, 3, 2).reshape(NG, GS, NP)
    d_cim = -_diag_blocks(dcc[:, 512:1024], NP, GS).transpose(0, 1, 3, 2).reshape(NG, GS, NP)

    dq, dk, dv, dck, d_gq, d_gk, fc = _attn_bwd(z, ck, row(g_q), row(g_k), lse, datt, [chip_sums[n] for n in early])
    from_chips = dict(zip(early, fc))
    df_t, d_bf = _gates_bwd(dck.reshape(NH, T), f_t, bf_col)
    df = jnp.concatenate([df_t.T, jnp.zeros((T, LANES - NH), F32)], axis=1).astype(BF16)
    d_wcat = _in_proj_dw(h1, dq, dk, dv, du, df)
    d_win = jnp.concatenate([d_wcat[0:1536], d_wcat[2048:2048 + NH], d_wcat[1536:2048]], axis=0)
    by_dest["w_in"] = d_win.astype(BF16).reshape(4, 2, 257, D)
    (from_sibling["w_in"],) = _swap_sibling([by_dest["w_in"]], "rs_pair_swap_w_in")
    chip_sums["w_in"] = _pair_sum(by_dest["w_in"], from_sibling["w_in"], "rs_pair_sum_w_in")
    grad_x, dg_mix, (from_chips["w_in"],) = _in_proj_bwd(dq, dk, dv, du, df, wcat, xs, row(g_mix), dx1, [chip_sums["w_in"]])

    grads, deltas, new_m, new_v = {}, {}, {}, {}
    for n in early + ("w_in",):
        flip = (lambda a: a.T) if n in ("w_up", "w_in") else (lambda a: a)
        outs = _adamw_rs(by_dest[n], from_sibling[n], from_chips[n], flip(weights[n]), flip(mom_m[n]), flip(mom_v[n]),
                         "adamw_" + n)
        grads[n], deltas[n], new_m[n], new_v[n] = (flip(o) for o in outs)

    small = dict(g_mix=dg_mix, b_f=d_bf, g_q=d_gq, g_k=d_gk, lambda_re=dlr, lambda_im=dli, log_step=dls,
                 b_re=dbr_t, b_im=dbi_t, c_re=d_cre, c_im=d_cim, d_skip=ddsk, b_glu=d_bg, g_attn_out=d_ga,
                 g_ssm_out=d_gs, g_ffn=dg_ffn, conv_b=dcb4)
    small_all, small_all_b = _all_gather(list(_pack_small_grads(small, loss_dev)), "gather_small_grads")
    views = dict(_SMALL_F32)
    swapped = ("b_re", "b_im")

    def view(n, a):
        return a.transpose(0, 2, 1).reshape(_PAIR_SHAPE) if n in swapped else a.reshape(views.get(n, _PAIR_SHAPE))

    loss_p, small_out = _adamw_small(small_all, small_all_b,
                                     {n: (view(n, weights[n]), view(n, mom_m[n]), view(n, mom_v[n])) for n in small})
    loss = loss_p[0, 0]
    for n in small:
        back = (lambda a: a.reshape(NG, GS, NP).transpose(0, 2, 1)) if n in swapped else (lambda a: a.reshape(weights[n].shape))
        grads[n], deltas[n], new_m[n], new_v[n] = (back(o) for o in small_out[n])

    return (loss, grad_x[None], *[grads[n] for n in names], *[deltas[n] for n in names],
            *[new_m[n] for n in names], *[new_v[n] for n in names])
```

```python
import jax
import jax.numpy as jnp
from jax import lax
from jax.experimental import pallas as pl
from jax.experimental.pallas import tpu as pltpu

F32, BF16 = jnp.float32, jnp.bfloat16
T, D = 2048, 1024
NH, DH = 8, 64
AW, SW = 512, 512
NG, GS, NP = 32, 16, 64
DFF = 2816
NDEV = 8
CS = 2 * DFF // NDEV
EPS = 1e-6
NEG = -1e30
ZC = 4 * 512 + 128
SEG = 256
NSEG = T // SEG
BQ = 256
VMEM_LIMIT = 56 * 1024 * 1024
LANES = 128
MESH = pl.DeviceIdType.MESH
HI = lax.Precision.HIGHEST

ADAM_LR, ADAM_B1, ADAM_B2, ADAM_EPS, ADAM_WD, ADAM_STEP = 0.001, 0.9, 0.999, 1e-08, 0.01, 10


def _cp(dims=None):
    if dims is None:
        return pltpu.CompilerParams(vmem_limit_bytes=VMEM_LIMIT)
    return pltpu.CompilerParams(dimension_semantics=dims, vmem_limit_bytes=VMEM_LIMIT)


def _sds(shape, dtype=F32):
    return jax.ShapeDtypeStruct(shape, dtype)


def _nt(a, b):
    return lax.dot_general(a, b, (((1,), (1,)), ((), ())), preferred_element_type=F32)


def _tn(a, b):
    return lax.dot_general(a, b, (((0,), (0,)), ((), ())), preferred_element_type=F32)


def _nn(a, b):
    return jnp.dot(a, b, preferred_element_type=F32)


def _rms_r(x):
    return lax.rsqrt(jnp.mean(x * x, axis=-1, keepdims=True) + EPS)


def _rms_bwd(n, r, g, dh):
    dn = dh * g
    dx = r * (dn - n * jnp.mean(dn * n, axis=-1, keepdims=True))
    return dx, jnp.sum(dh * n, axis=0, keepdims=True)


def _row_tile(rows, cols, limit):
    tr = rows
    while tr * cols * 4 > limit and tr % 32 == 0:
        tr //= 2
    return tr


def _place():
    return lax.axis_index("x"), lax.axis_index("y"), lax.axis_index("c")


def _all_gather(shards, name):
    n = len(shards)

    def body(*refs):
        x_refs, out_refs = refs[:n], refs[n:2 * n]
        send_sems, recv_sems, local_sems = refs[2 * n:]
        x, y, c = _place()
        me, sibling = (x, y, c), (x, y, 1 - c)
        chips = [(1 - x, y), (x, 1 - y), (1 - x, 1 - y)]

        def copy(a, k, block, to, src=None):
            px, py, pc = block
            slot = out_refs[a].at[4 * px + 2 * py + pc]
            return pltpu.make_async_remote_copy(
                src_ref=slot if src is None else src, dst_ref=slot,
                send_sem=send_sems.at[7 * a + k], recv_sem=recv_sems.at[7 * a + k], device_id=to, device_id_type=MESH)

        mine, first, passed = [], [], []
        for a in range(n):
            cp = pltpu.make_async_copy(x_refs[a], out_refs[a].at[4 * x + 2 * y + c], local_sems.at[a])
            cp.start()
            mine.append(cp)
            cps = [copy(a, 0, me, sibling, src=x_refs[a])]
            cps += [copy(a, 1 + j, me, (*chip, c), src=x_refs[a]) for j, chip in enumerate(chips)]
            for cp in cps:
                cp.start()
            first += cps
        for a in range(n):
            for j, chip in enumerate(chips):
                copy(a, 1 + j, (*chip, c), me).wait_recv()
                fwd = copy(a, 4 + j, (*chip, c), sibling)
                fwd.start()
                passed.append(fwd)
        for a in range(n):
            copy(a, 0, sibling, me).wait_recv()
            for j, chip in enumerate(chips):
                copy(a, 4 + j, (*chip, 1 - c), me).wait_recv()
        for cp in first + passed:
            cp.wait_send()
        for cp in mine:
            cp.wait()

    anyspec = pl.BlockSpec(memory_space=pl.ANY)
    return pl.pallas_call(
        body, name=name,
        out_shape=[_sds((NDEV, *s.shape), s.dtype) for s in shards],
        in_specs=[anyspec] * n, out_specs=[anyspec] * n,
        scratch_shapes=[pltpu.SemaphoreType.DMA((7 * n,)), pltpu.SemaphoreType.DMA((7 * n,)),
                        pltpu.SemaphoreType.DMA((n,))],
    )(*shards)


def _gather_first_copies(x_refs, out_refs, send_sems, recv_sems, local_sems):
    x, y, c = _place()
    slot = 4 * x + 2 * y + c
    peers = [(x, y, 1 - c), (1 - x, y, c), (x, 1 - y, c), (1 - x, 1 - y, c)]
    local, remote = [], []
    for a, (x_ref, out_ref) in enumerate(zip(x_refs, out_refs)):
        local.append(pltpu.make_async_copy(x_ref, out_ref.at[slot], local_sems.at[a]))
        for k, peer in enumerate(peers):
            remote.append(pltpu.make_async_remote_copy(
                src_ref=x_ref, dst_ref=out_ref.at[slot], send_sem=send_sems.at[4 * a + k],
                recv_sem=recv_sems.at[4 * a + k], device_id=peer, device_id_type=MESH))
    return local, remote


def _gather_first_scratch(n):
    return [pltpu.SemaphoreType.DMA((4 * n,)), pltpu.SemaphoreType.DMA((4 * n,)), pltpu.SemaphoreType.DMA((n,))]


def _chip_swap_copies(p_refs, out_refs, send_sems, recv_sems):
    x, y, c = _place()
    chips = [(1 - x, y), (x, 1 - y), (1 - x, 1 - y)]
    cps = []
    for a, (p_ref, out_ref) in enumerate(zip(p_refs, out_refs)):
        for j, (px, py) in enumerate(chips):
            cps.append(pltpu.make_async_remote_copy(
                src_ref=p_ref.at[2 * px + py], dst_ref=out_ref.at[j],
                send_sem=send_sems.at[3 * a + j], recv_sem=recv_sems.at[3 * a + j],
                device_id=(px, py, c), device_id_type=MESH))
    return cps


def _sibling_swap_copies(a_refs, out_refs, send_sems, recv_sems):
    x, y, c = _place()
    cps = []
    for a, (a_ref, out_ref) in enumerate(zip(a_refs, out_refs)):
        for k in range(4):
            cps.append(pltpu.make_async_remote_copy(
                src_ref=a_ref.at[k, 1 - c], dst_ref=out_ref.at[k],
                send_sem=send_sems.at[4 * a + k], recv_sem=recv_sems.at[4 * a + k],
                device_id=(x, y, 1 - c), device_id_type=MESH))
    return cps


def _pass_on_copies(buf_refs, send_sems, recv_sems):
    x, y, c = _place()
    chips = [(1 - x, y), (x, 1 - y), (1 - x, 1 - y)]
    cps = []
    for a, buf in enumerate(buf_refs):
        for j, (px, py) in enumerate(chips):
            block = buf.at[4 * px + 2 * py + c]
            cps.append(pltpu.make_async_remote_copy(
                src_ref=block, dst_ref=block, send_sem=send_sems.at[3 * a + j], recv_sem=recv_sems.at[3 * a + j],
                device_id=(x, y, 1 - c), device_id_type=MESH))
    return cps


def _swap_sibling(arrs, name):
    n = len(arrs)

    def body(*refs):
        cps = _sibling_swap_copies(refs[:n], refs[n:2 * n], *refs[2 * n:])
        for cp in cps:
            cp.start()
        for cp in cps:
            cp.wait()

    anyspec = pl.BlockSpec(memory_space=pl.ANY)
    return pl.pallas_call(
        body, name=name,
        out_shape=[_sds((4, *a.shape[2:]), a.dtype) for a in arrs],
        in_specs=[anyspec] * n, out_specs=[anyspec] * n,
        scratch_shapes=[pltpu.SemaphoreType.DMA((4 * n,)), pltpu.SemaphoreType.DMA((4 * n,))],
    )(*arrs)


def _in_proj(x, g, wcat, shards):
    tm = 256
    n = len(shards)

    def body(x_ref, g_ref, w_ref, *rest):
        x_refs, (h_ref, z_ref, ft_ref), out_refs, sems = rest[:n], rest[n:n + 3], rest[n + 3:2 * n + 3], rest[2 * n + 3:]

        @pl.when(pl.program_id(0) == 0)
        def _():
            local, remote = _gather_first_copies(x_refs, out_refs, *sems)
            for cp in local + remote:
                cp.start()

        xv = x_ref[...]
        h = (xv * _rms_r(xv) * g_ref[...]).astype(BF16)
        h_ref[...] = h
        z = _nt(h, w_ref[...])
        z_ref[...] = z
        ft_ref[...] = z[:, 2048:ZC].T[0:NH, :]

        @pl.when(pl.program_id(0) == T // tm - 1)
        def _():
            local, remote = _gather_first_copies(x_refs, out_refs, *sems)
            for cp in remote + local:
                cp.wait()

    row = lambda i: (i, 0)
    fixed = lambda i: (0, 0)
    anyspec = pl.BlockSpec(memory_space=pl.ANY)
    res = pl.pallas_call(
        body, name="in_proj", grid=(T // tm,),
        in_specs=[pl.BlockSpec((tm, D), row), pl.BlockSpec((1, D), fixed), pl.BlockSpec((ZC, D), fixed)] + [anyspec] * n,
        out_specs=[pl.BlockSpec((tm, D), row), pl.BlockSpec((tm, ZC), row), pl.BlockSpec((NH, tm), lambda i: (0, i))]
        + [anyspec] * n,
        out_shape=[_sds((T, D), BF16), _sds((T, ZC)), _sds((NH, T))] + [_sds((NDEV, *s.shape), s.dtype) for s in shards],
        scratch_shapes=_gather_first_scratch(n),
        compiler_params=_cp(("arbitrary",)),
    )(x, g, wcat, *shards)
    return res[0], res[1], res[2], res[3:]


def _in_proj_dw(h, dq, dk, dv, du, df):
    tm = 512

    def body(h_ref, dq_ref, dk_ref, dv_ref, du_ref, df_ref, o_ref):
        hv = h_ref[...]
        for j, d_ref in enumerate((dq_ref, dk_ref, dv_ref, du_ref)):
            o_ref[512 * j:512 * (j + 1), :] = _tn(d_ref[...], hv)
        o_ref[2048:ZC, :] = _tn(df_ref[...], hv)

    full = lambda w: pl.BlockSpec((T, w), lambda i: (0, 0))
    return pl.pallas_call(
        body, name="in_proj_dw", grid=(D // tm,),
        in_specs=[pl.BlockSpec((T, tm), lambda i: (0, i)), full(512), full(512), full(512), full(512), full(LANES)],
        out_specs=pl.BlockSpec((ZC, tm), lambda i: (0, i)), out_shape=_sds((ZC, D)),
        compiler_params=_cp(("parallel",)),
    )(h, dq, dk, dv, du, df)


def _in_proj_bwd(dq, dk, dv, du, df, wcat, x, g, dres, parts):
    tm = 256
    n = len(parts)

    def body(dq_ref, dk_ref, dv_ref, du_ref, df_ref, w_ref, x_ref, g_ref, dres_ref, *rest):
        p_refs, (dx_ref, dg_ref), fc_refs, sems = rest[:n], rest[n:n + 2], rest[n + 2:2 * n + 2], rest[2 * n + 2:]

        @pl.when(pl.program_id(0) == 0)
        def _():
            for cp in _chip_swap_copies(p_refs, fc_refs, *sems):
                cp.start()

        dh = _nn(df_ref[...], w_ref[2048:ZC, :])
        for j, d_ref in enumerate((dq_ref, dk_ref, dv_ref, du_ref)):
            dh = dh + _nn(d_ref[...], w_ref[512 * j:512 * (j + 1), :])
        xv = x_ref[...]
        r = _rms_r(xv)
        dxr, dgp = _rms_bwd(xv * r, r, g_ref[...], dh)
        dx_ref[...] = dres_ref[...] + dxr

        @pl.when(pl.program_id(0) == 0)
        def _():
            dg_ref[...] = jnp.zeros_like(dg_ref)
        dg_ref[...] += dgp

        @pl.when(pl.program_id(0) == T // tm - 1)
        def _():
            for cp in _chip_swap_copies(p_refs, fc_refs, *sems):
                cp.wait()

    row = lambda i: (i, 0)
    fixed = lambda i: (0, 0)
    part = pl.BlockSpec((tm, 512), row)
    anyspec = pl.BlockSpec(memory_space=pl.ANY)
    res = pl.pallas_call(
        body, name="in_proj_bwd", grid=(T // tm,),
        in_specs=[part, part, part, part, pl.BlockSpec((tm, LANES), row), pl.BlockSpec((ZC, D), fixed),
                  pl.BlockSpec((tm, D), row), pl.BlockSpec((1, D), fixed), pl.BlockSpec((tm, D), row)] + [anyspec] * n,
        out_specs=[pl.BlockSpec((tm, D), row), pl.BlockSpec((1, D), fixed)] + [anyspec] * n,
        out_shape=[_sds((T, D)), _sds((1, D))] + [_sds((3, *p.shape[1:]), p.dtype) for p in parts],
        scratch_shapes=[pltpu.SemaphoreType.DMA((3 * n,)), pltpu.SemaphoreType.DMA((3 * n,))],
        compiler_params=_cp(("arbitrary",)),
    )(dq, dk, dv, du, df, wcat, x, g, dres, *parts)
    return res[0], res[1], res[2:]


def _stack_lanes(v):
    return jnp.concatenate([v[:, LANES * b:LANES * (b + 1)] for b in range(T // LANES)], axis=0)


def _unstack_lanes(s):
    return jnp.concatenate([s[8 * b:8 * b + 8, :] for b in range(T // LANES)], axis=1)


def _cumsum_lanes(v, reverse):
    st = _stack_lanes(v)
    ri = lax.broadcasted_iota(jnp.int32, (LANES, LANES), 0)
    ci = lax.broadcasted_iota(jnp.int32, (LANES, LANES), 1)
    tri = (ri >= ci) if reverse else (ri <= ci)
    intra = jnp.dot(st, tri.astype(F32), precision=HI, preferred_element_type=F32)
    tot = intra[:, 0:1] if reverse else intra[:, LANES - 1:LANES]
    same_head = (ci % 8) == (ri % 8)
    other = (ci // 8 > ri // 8) if reverse else (ci // 8 < ri // 8)
    off = jnp.dot((same_head & other).astype(F32), jnp.broadcast_to(tot, (LANES, LANES)), precision=HI,
                  preferred_element_type=F32)
    return _unstack_lanes(intra + off)


def _gates_fwd(f_t, b_f):
    def body(f_ref, b_ref, c_ref):
        z = f_ref[...] + b_ref[...]
        lf = jnp.minimum(z, 0.0) - jnp.log(1.0 + jnp.exp(-jnp.abs(z)))
        c_ref[...] = _cumsum_lanes(lf, reverse=False)

    return pl.pallas_call(body, name="gates_fwd", out_shape=_sds((NH, T)), compiler_params=_cp())(f_t, b_f)


def _gates_bwd(dck, f_t, b_f):
    def body(d_ref, f_ref, b_ref, df_ref, db_ref):
        z = f_ref[...] + b_ref[...]
        dlf = _cumsum_lanes(d_ref[...], reverse=True)
        df = dlf * jax.nn.sigmoid(-z)
        df_ref[...] = df
        db_ref[...] = jnp.sum(df, axis=1, keepdims=True)

    return pl.pallas_call(body, name="gates_bwd", out_shape=[_sds((NH, T)), _sds((NH, 1))],
                          compiler_params=_cp())(dck, f_t, b_f)


def _lower_triangle():
    rows = lax.broadcasted_iota(jnp.int32, (BQ, BQ), 0)
    cols = lax.broadcasted_iota(jnp.int32, (BQ, BQ), 1)
    return cols <= rows


def _logit_parts(qb, kb, ck_row, lo, hi, tri):
    parts = []
    if lo > 0:
        parts.append((_nt(qb[lo:hi], kb[0:lo]) - ck_row[:, 0:lo], 0, lo))
    parts.append((jnp.where(tri, _nt(qb[lo:hi], kb[lo:hi]) - ck_row[:, lo:hi], NEG), lo, hi))
    return parts


_HP = NH // 2


def _qkv_specs():
    return [pl.BlockSpec((T, LANES), lambda p: (0, p)), pl.BlockSpec((T, LANES), lambda p: (0, _HP + p)),
            pl.BlockSpec((T, LANES), lambda p: (0, 2 * _HP + p))]


def _attn_fwd(z, ck, g_q, g_k, shards, bufs):
    scale = DH ** -0.5
    n, m = len(shards), len(bufs)

    def body(q_ref, k_ref, v_ref, ck_ref, gq_ref, gk_ref, *rest):
        x_refs, (o_ref, lse_ref) = rest[:n], rest[n + m:n + m + 2]
        out_refs, buf_refs, sems = rest[n + m + 2:2 * n + m + 2], rest[2 * n + m + 2:2 * n + 2 * m + 2], rest[2 * n + 2 * m + 2:]

        @pl.when(pl.program_id(0) == 0)
        def _():
            local, remote = _gather_first_copies(x_refs, out_refs, *sems[:3])
            for cp in local + remote + _pass_on_copies(buf_refs, *sems[3:]):
                cp.start()

        qb, kb, vb = [], [], []
        for hh in range(2):
            cols = slice(DH * hh, DH * (hh + 1))
            qv, kv = q_ref[:, cols], k_ref[:, cols]
            qb.append((qv * _rms_r(qv) * gq_ref[...] * scale).astype(BF16))
            kb.append((kv * _rms_r(kv) * gk_ref[...]).astype(BF16))
            vb.append(v_ref[:, cols].astype(BF16))
        tri = _lower_triangle()
        for i in range(T // BQ):
            lo, hi = i * BQ, (i + 1) * BQ
            outs = []
            for hh in range(2):
                parts = _logit_parts(qb[hh], kb[hh], ck_ref[0, hh:hh + 1, :], lo, hi, tri)
                top = jnp.max(parts[-1][0], axis=-1, keepdims=True)
                if lo > 0:
                    top = jnp.maximum(top, jnp.max(parts[0][0], axis=-1, keepdims=True))
                l, o = 0.0, 0.0
                for s, k0, k1 in parts:
                    p = jnp.exp(s - top)
                    l = l + jnp.sum(p, axis=-1, keepdims=True)
                    o = o + _nn(p.astype(BF16), vb[hh][k0:k1])
                outs.append(o / l)
                lse_ref[0, lo:hi, hh:hh + 1] = top + jnp.log(l)
            o_ref[lo:hi, :] = jnp.concatenate(outs, axis=1)

        @pl.when(pl.program_id(0) == _HP - 1)
        def _():
            local, remote = _gather_first_copies(x_refs, out_refs, *sems[:3])
            for cp in remote + local + _pass_on_copies(buf_refs, *sems[3:]):
                cp.wait()

    fixed = lambda p: (0, 0)
    anyspec = pl.BlockSpec(memory_space=pl.ANY)
    res = pl.pallas_call(
        body, name="attn_fwd", grid=(_HP,),
        in_specs=_qkv_specs() + [pl.BlockSpec((1, 2, T), lambda p: (p, 0, 0)), pl.BlockSpec((1, DH), fixed),
                                 pl.BlockSpec((1, DH), fixed)] + [anyspec] * (n + m),
        out_specs=[pl.BlockSpec((T, LANES), lambda p: (0, p)), pl.BlockSpec((1, T, 2), lambda p: (p, 0, 0))]
        + [anyspec] * (n + m),
        out_shape=[_sds((T, AW)), _sds((_HP, T, 2))] + [_sds((NDEV, *s.shape), s.dtype) for s in shards]
        + [_sds(b.shape, b.dtype) for b in bufs],
        input_output_aliases={6 + n + i: 2 + n + i for i in range(m)},
        scratch_shapes=_gather_first_scratch(n) + [pltpu.SemaphoreType.DMA((3 * m,)), pltpu.SemaphoreType.DMA((3 * m,))],
        compiler_params=_cp(("arbitrary",)),
    )(z, z, z, ck, g_q, g_k, *shards, *bufs)
    return res[0], res[1], res[2:2 + n], res[2 + n:]


def _attn_bwd(z, ck, g_q, g_k, lse, datt, parts):
    scale = DH ** -0.5
    n = len(parts)

    def body(q_ref, k_ref, v_ref, ck_ref, gq_ref, gk_ref, lse_ref, do_ref, *rest):
        p_refs, rest = rest[:n], rest[n:]
        dq_ref, dk_ref, dv_ref, dck_ref, dgq_ref, dgk_ref = rest[:6]
        fc_refs, (dkn_acc, dv_acc, dc_acc), sems = rest[6:6 + n], rest[6 + n:9 + n], rest[9 + n:]

        @pl.when(pl.program_id(0) == 0)
        def _():
            for cp in _chip_swap_copies(p_refs, fc_refs, *sems):
                cp.start()

        nq, nk, rq, rk, qb, kb, vb, dob = [], [], [], [], [], [], [], []
        for hh in range(2):
            cols = slice(DH * hh, DH * (hh + 1))
            qv, kv = q_ref[:, cols], k_ref[:, cols]
            rq.append(_rms_r(qv))
            rk.append(_rms_r(kv))
            nq.append(qv * rq[hh])
            nk.append(kv * rk[hh])
            qb.append((nq[hh] * gq_ref[...] * scale).astype(BF16))
            kb.append((nk[hh] * gk_ref[...]).astype(BF16))
            vb.append(v_ref[:, cols].astype(BF16))
            dob.append(do_ref[:, cols].astype(BF16))
        dkn_acc[...] = jnp.zeros_like(dkn_acc)
        dv_acc[...] = jnp.zeros_like(dv_acc)
        dc_acc[...] = jnp.zeros_like(dc_acc)
        dgq = jnp.zeros((1, DH), F32)
        tri = _lower_triangle()
        for i in range(T // BQ):
            lo, hi = i * BQ, (i + 1) * BQ
            dqs = []
            for hh in range(2):
                lse = lse_ref[0, lo:hi, hh:hh + 1]
                pd = []
                for s, k0, k1 in _logit_parts(qb[hh], kb[hh], ck_ref[0, hh:hh + 1, :], lo, hi, tri):
                    pd.append((jnp.exp(s - lse), _nt(dob[hh][lo:hi], vb[hh][k0:k1]), k0, k1))
                mean = sum(jnp.sum(p * dp, axis=-1, keepdims=True) for p, dp, _, _ in pd)
                dqn = 0.0
                for p, dp, k0, k1 in pd:
                    ds = p * (dp - mean)
                    dsb = ds.astype(BF16)
                    dv_acc[hh, k0:k1, :] += _tn(p.astype(BF16), dob[hh][lo:hi])
                    dkn_acc[hh, k0:k1, :] += _tn(dsb, qb[hh][lo:hi])
                    dc_acc[hh:hh + 1, k0:k1] -= jnp.sum(ds, axis=0, keepdims=True)
                    dqn = dqn + _nn(dsb, kb[hh][k0:k1])
                dqx, dgp = _rms_bwd(nq[hh][lo:hi], rq[hh][lo:hi], gq_ref[...], dqn * scale)
                dqs.append(dqx)
                dgq = dgq + dgp
            dq_ref[lo:hi, :] = jnp.concatenate(dqs, axis=1).astype(BF16)
        dks, dgk = [], jnp.zeros((1, DH), F32)
        for hh in range(2):
            dkx, dgp = _rms_bwd(nk[hh], rk[hh], gk_ref[...], dkn_acc[hh])
            dks.append(dkx)
            dgk = dgk + dgp
        dk_ref[...] = jnp.concatenate(dks, axis=1).astype(BF16)
        dv_ref[...] = jnp.concatenate([dv_acc[0], dv_acc[1]], axis=1).astype(BF16)
        dck_ref[0] = dc_acc[...]

        @pl.when(pl.program_id(0) == 0)
        def _():
            dgq_ref[...] = jnp.zeros_like(dgq_ref)
            dgk_ref[...] = jnp.zeros_like(dgk_ref)
        dgq_ref[...] += dgq
        dgk_ref[...] += dgk

        @pl.when(pl.program_id(0) == _HP - 1)
        def _():
            for cp in _chip_swap_copies(p_refs, fc_refs, *sems):
                cp.wait()

    fixed = lambda p: (0, 0)
    pair = pl.BlockSpec((T, LANES), lambda p: (0, p))
    gsp = pl.BlockSpec((1, DH), fixed)
    ckspec = pl.BlockSpec((1, 2, T), lambda p: (p, 0, 0))
    anyspec = pl.BlockSpec(memory_space=pl.ANY)
    res = pl.pallas_call(
        body, name="attn_bwd", grid=(_HP,),
        in_specs=_qkv_specs() + [ckspec, gsp, gsp, pl.BlockSpec((1, T, 2), lambda p: (p, 0, 0)), pair] + [anyspec] * n,
        out_specs=[pair, pair, pair, ckspec, gsp, gsp] + [anyspec] * n,
        out_shape=[_sds((T, AW), BF16)] * 3 + [_sds((_HP, 2, T)), _sds((1, DH)), _sds((1, DH))]
        + [_sds((3, *p.shape[1:]), p.dtype) for p in parts],
        scratch_shapes=[pltpu.VMEM((2, T, DH), F32), pltpu.VMEM((2, T, DH), F32), pltpu.VMEM((2, T), F32),
                        pltpu.SemaphoreType.DMA((3 * n,)), pltpu.SemaphoreType.DMA((3 * n,))],
        compiler_params=_cp(("arbitrary",)),
    )(z, z, z, ck, g_q, g_k, lse, datt, *parts)
    return (*res[:6], res[6:])


def _disc(lr, li, ls, br_t, bi_t):
    step = jnp.exp(ls)
    er = jnp.exp(lr * step)
    ab_re = er * jnp.cos(li * step)
    ab_im = er * jnp.sin(li * step)
    num_re = ab_re - 1.0
    num_im = ab_im
    den = lr * lr + li * li
    f_re = (num_re * lr + num_im * li) / den
    f_im = (num_im * lr - num_re * li) / den
    bb_re = f_re[:, None, :] * br_t - f_im[:, None, :] * bi_t
    bb_im = f_re[:, None, :] * bi_t + f_im[:, None, :] * br_t
    return ab_re, ab_im, bb_re, bb_im


def _disc_fwd(lr, li, ls, br_t, bi_t):
    def body(lr_ref, li_ref, ls_ref, br_ref, bi_ref, ar_ref, ai_ref, bbr_ref, bbi_ref):
        ar, ai, bbr, bbi = _disc(lr_ref[...], li_ref[...], ls_ref[...], br_ref[...], bi_ref[...])
        ar_ref[...] = ar
        ai_ref[...] = ai
        bbr_ref[...] = bbr
        bbi_ref[...] = bbi

    return pl.pallas_call(
        body, name="ssm_disc_fwd",
        out_shape=[_sds((NG, NP)), _sds((NG, NP)), _sds((NG, GS, NP)), _sds((NG, GS, NP))],
        compiler_params=_cp())(lr, li, ls, br_t, bi_t)


def _disc_bwd(lr, li, ls, br_t, bi_t, dar, dai, dbbr, dbbi):
    def body(lr_ref, li_ref, ls_ref, br_ref, bi_ref, dar_ref, dai_ref, dbbr_ref, dbbi_ref,
             dlr_ref, dli_ref, dls_ref, dbr_ref, dbi_ref):
        _, vjp = jax.vjp(_disc, lr_ref[...], li_ref[...], ls_ref[...], br_ref[...], bi_ref[...])
        dlr, dli, dls, dbr, dbi = vjp((dar_ref[...], dai_ref[...], dbbr_ref[...], dbbi_ref[...]))
        dlr_ref[...] = dlr
        dli_ref[...] = dli
        dls_ref[...] = dls
        dbr_ref[...] = dbr
        dbi_ref[...] = dbi

    return pl.pallas_call(
        body, name="ssm_disc_bwd",
        out_shape=[_sds((NG, NP)), _sds((NG, NP)), _sds((NG, 1)), _sds((NG, GS, NP)), _sds((NG, GS, NP))],
        compiler_params=_cp())(lr, li, ls, br_t, bi_t, dar, dai, dbbr, dbbi)


def _cmul(ar, ai, br, bi):
    return ar * br - ai * bi, ar * bi + ai * br


def _scan(buf, a_re, a_im, reverse, other=None):
    ar = [jnp.broadcast_to(a_re[:, LANES * k:LANES * (k + 1)], (NSEG, LANES)) for k in range(4)]
    ai = [jnp.broadcast_to(a_im[:, LANES * k:LANES * (k + 1)], (NSEG, LANES)) for k in range(4)]

    def tile(ref, i, k):
        return ref[pl.ds(pl.multiple_of(i * NSEG, NSEG), NSEG), LANES * k:LANES * (k + 1)]

    def advance(i, xr, xi):
        nr = [ar[k] * xr[k] - ai[k] * xi[k] + tile(buf, i, k) for k in range(4)]
        ni = [ar[k] * xi[k] + ai[k] * xr[k] + tile(buf, i, 4 + k) for k in range(4)]
        return nr, ni

    def index(i):
        return (SEG - 1 - i) if reverse else i

    zeros = tuple(jnp.zeros((NSEG, LANES), F32) for _ in range(4))

    def sweep1(i, carry):
        nr, ni = advance(index(i), carry[0], carry[1])
        return tuple(nr), tuple(ni)

    er, ei = lax.fori_loop(0, SEG, sweep1, (zeros, zeros), unroll=4)

    row = lax.broadcasted_iota(jnp.int32, (NSEG, LANES), 0)
    sr, si = [], []
    for k in range(4):
        pr, pi = ar[k], ai[k]
        for _ in range(SEG.bit_length() - 1):
            pr, pi = _cmul(pr, pi, pr, pi)
        ir, ii = er[k], ei[k]
        for d in (1, 2, 4):
            shift = (NSEG - d) if reverse else d
            ok = (row < NSEG - d) if reverse else (row >= d)
            mr, mi = _cmul(pr, pi, pltpu.roll(ir, shift, 0), pltpu.roll(ii, shift, 0))
            ir = ir + jnp.where(ok, mr, 0.0)
            ii = ii + jnp.where(ok, mi, 0.0)
            pr, pi = _cmul(pr, pi, pr, pi)
        shift = (NSEG - 1) if reverse else 1
        ok = (row < NSEG - 1) if reverse else (row >= 1)
        sr.append(jnp.where(ok, pltpu.roll(ir, shift, 0), 0.0))
        si.append(jnp.where(ok, pltpu.roll(ii, shift, 0), 0.0))

    def sweep2(i, carry):
        xr, xi, dar, dai = carry
        idx = index(i)
        if other is not None:
            orr = [tile(other, idx, k) for k in range(4)]
            oi = [tile(other, idx, 4 + k) for k in range(4)]
            dar = tuple(dar[k] + xr[k] * orr[k] + xi[k] * oi[k] for k in range(4))
            dai = tuple(dai[k] + xi[k] * orr[k] - xr[k] * oi[k] for k in range(4))
        nr, ni = advance(idx, xr, xi)
        start = pl.multiple_of(idx * NSEG, NSEG)
        for k in range(4):
            buf[pl.ds(start, NSEG), LANES * k:LANES * (k + 1)] = nr[k]
            buf[pl.ds(start, NSEG), LANES * (4 + k):LANES * (5 + k)] = ni[k]
        return tuple(nr), tuple(ni), dar, dai

    _, _, dar, dai = lax.fori_loop(0, SEG, sweep2, (tuple(sr), tuple(si), zeros, zeros), unroll=4)
    if other is None:
        return None
    da_re = jnp.concatenate([jnp.sum(d, axis=0, keepdims=True) for d in dar], axis=1)
    da_im = jnp.concatenate([jnp.sum(d, axis=0, keepdims=True) for d in dai], axis=1)
    return da_re, da_im


_SSM_BLOCKS = 4


def _ssm_fwd(u, bb, ab, cc, dsk, shards):
    n = len(shards)

    def body(u_ref, bb_ref, a_ref, cc_ref, d_ref, *rest):
        x_refs, y_ref, xs_ref, out_refs, sems = rest[:n], rest[n], rest[n + 1], rest[n + 2:2 * n + 2], rest[2 * n + 2:]

        @pl.when(pl.program_id(0) == 0)
        def _():
            local, remote = _gather_first_copies(x_refs, out_refs, *sems)
            for cp in local + remote:
                cp.start()

        ub = u_ref[...]
        xb = xs_ref.at[0]
        xb[...] = _nn(ub.astype(BF16), bb_ref[0])
        _scan(xb, a_ref[0, 0:1, :], a_ref[0, 1:2, :], reverse=False)
        y_ref[...] = _nn(xb[...].astype(BF16), cc_ref[0]) + d_ref[0] * ub

        @pl.when(pl.program_id(0) == _SSM_BLOCKS - 1)
        def _():
            local, remote = _gather_first_copies(x_refs, out_refs, *sems)
            for cp in remote + local:
                cp.wait()

    blk = lambda j: (j, 0, 0)
    col = pl.BlockSpec((T, LANES), lambda j: (0, j))
    anyspec = pl.BlockSpec(memory_space=pl.ANY)
    res = pl.pallas_call(
        body, name="ssm_fwd", grid=(_SSM_BLOCKS,),
        in_specs=[col, pl.BlockSpec((1, LANES, 1024), blk), pl.BlockSpec((1, 2, 512), blk),
                  pl.BlockSpec((1, 1024, LANES), blk), pl.BlockSpec((1, 1, LANES), blk)] + [anyspec] * n,
        out_specs=[col, pl.BlockSpec((1, T, 1024), blk)] + [anyspec] * n,
        out_shape=[_sds((T, SW)), _sds((_SSM_BLOCKS, T, 1024))] + [_sds((NDEV, *s.shape), s.dtype) for s in shards],
        scratch_shapes=_gather_first_scratch(n),
        compiler_params=_cp(("arbitrary",)),
    )(u, bb, ab, cc, dsk, *shards)
    return res[0], res[1], res[2:]


def _ssm_bwd(u, dy, xs, bb, ab, cc, dsk, arrs):
    n = len(arrs)

    def body(u_ref, dy_ref, x_ref, bb_ref, a_ref, cc_ref, d_ref, *rest):
        a_refs, rest = rest[:n], rest[n:]
        du_ref, dbb_ref, da_ref, dcc_ref, dd_ref = rest[:5]
        fs_refs, gb, sems = rest[5:5 + n], rest[5 + n], rest[6 + n:]

        @pl.when(pl.program_id(0) == 0)
        def _():
            for cp in _sibling_swap_copies(a_refs, fs_refs, *sems):
                cp.start()

        ub, dyv = u_ref[...], dy_ref[...]
        ubb, dyb = ub.astype(BF16), dyv.astype(BF16)
        a_re, a_im = a_ref[0, 0:1, :], a_ref[0, 1:2, :]
        dcc_ref[0] = _tn(x_ref[0].astype(BF16), dyb)
        gb[...] = _nt(dyb, cc_ref[0])
        da_re, da_im = _scan(gb, a_re, -a_im, reverse=True, other=x_ref.at[0])
        da_ref[0] = jnp.concatenate([da_re, da_im], axis=0)
        gc = gb[...].astype(BF16)
        du_ref[...] = _nt(gc, bb_ref[0]) + d_ref[0] * dyv
        dbb_ref[0] = _tn(ubb, gc)
        dd_ref[0] = jnp.sum(dyv * ub, axis=0, keepdims=True)

        @pl.when(pl.program_id(0) == _SSM_BLOCKS - 1)
        def _():
            for cp in _sibling_swap_copies(a_refs, fs_refs, *sems):
                cp.wait()

    blk = lambda j: (j, 0, 0)
    col = pl.BlockSpec((T, LANES), lambda j: (0, j))
    anyspec = pl.BlockSpec(memory_space=pl.ANY)
    res = pl.pallas_call(
        body, name="ssm_bwd", grid=(_SSM_BLOCKS,),
        in_specs=[col, col, pl.BlockSpec((1, T, 1024), blk), pl.BlockSpec((1, LANES, 1024), blk),
                  pl.BlockSpec((1, 2, 512), blk), pl.BlockSpec((1, 1024, LANES), blk), pl.BlockSpec((1, 1, LANES), blk)]
        + [anyspec] * n,
        out_specs=[col, pl.BlockSpec((1, LANES, 1024), blk), pl.BlockSpec((1, 2, 512), blk),
                   pl.BlockSpec((1, 1024, LANES), blk), pl.BlockSpec((1, 1, LANES), blk)] + [anyspec] * n,
        out_shape=[_sds((T, SW)), _sds((4, LANES, 1024)), _sds((4, 2, 512)), _sds((4, 1024, LANES)), _sds((4, 1, LANES))]
        + [_sds((4, *a.shape[2:]), a.dtype) for a in arrs],
        scratch_shapes=[pltpu.VMEM((T, 1024), F32), pltpu.SemaphoreType.DMA((4 * n,)), pltpu.SemaphoreType.DMA((4 * n,))],
        compiler_params=_cp(("arbitrary",)),
    )(u, dy, xs, bb, ab, cc, dsk, *arrs)
    return (*res[:5], res[5:])


def _interleave(a):
    return a.reshape(NSEG, SEG, a.shape[1]).transpose(1, 0, 2).reshape(a.shape)


def _deinterleave(a):
    return a.reshape(SEG, NSEG, a.shape[1]).transpose(1, 0, 2).reshape(a.shape)


_TM_MIX = 256


def _mix_parts(att, y, wg, bg):
    y2 = jax.nn.gelu(y)
    y2b = y2.astype(BF16)
    sg = jax.nn.sigmoid(_nn(y2b, wg) + bg)
    ssm = y2 * sg
    ra, rs = _rms_r(att), _rms_r(ssm)
    return y2, y2b, sg, ssm, ra, rs


def _mix_fwd(att, y, x, wg, bg, ga, gs, wo, bufs):
    m = len(bufs)

    def body(att_ref, y_ref, x_ref, wg_ref, bg_ref, ga_ref, gs_ref, wo_ref, *rest):
        x1_ref, buf_refs, sems = rest[m], rest[m + 1:2 * m + 1], rest[2 * m + 1:]

        @pl.when(pl.program_id(0) == 0)
        def _():
            for cp in _pass_on_copies(buf_refs, *sems):
                cp.start()

        attv = att_ref[...]
        _, _, _, ssm, ra, rs = _mix_parts(attv, y_ref[...], wg_ref[...], bg_ref[...])
        ma = (attv * ra * ga_ref[...]).astype(BF16)
        ms = (ssm * rs * gs_ref[...]).astype(BF16)
        x1_ref[...] = x_ref[...] + _nn(ma, wo_ref[0:AW, :]) + _nn(ms, wo_ref[AW:D, :])

        @pl.when(pl.program_id(0) == T // _TM_MIX - 1)
        def _():
            for cp in _pass_on_copies(buf_refs, *sems):
                cp.wait()

    row = lambda i: (i, 0)
    fixed = lambda i: (0, 0)
    half = pl.BlockSpec((_TM_MIX, AW), row)
    vec = pl.BlockSpec((1, AW), fixed)
    anyspec = pl.BlockSpec(memory_space=pl.ANY)
    res = pl.pallas_call(
        body, name="mix_fwd", grid=(T // _TM_MIX,),
        in_specs=[half, half, pl.BlockSpec((_TM_MIX, D), row), pl.BlockSpec((SW, SW), fixed), vec, vec, vec,
                  pl.BlockSpec((D, D), fixed)] + [anyspec] * m,
        out_specs=[pl.BlockSpec((_TM_MIX, D), row)] + [anyspec] * m,
        out_shape=[_sds((T, D))] + [_sds(b.shape, b.dtype) for b in bufs],
        input_output_aliases={8 + i: 1 + i for i in range(m)},
        scratch_shapes=[pltpu.SemaphoreType.DMA((3 * m,)), pltpu.SemaphoreType.DMA((3 * m,))],
        compiler_params=_cp(("arbitrary",)),
    )(att, y, x, wg, bg, ga, gs, wo, *bufs)
    return res[0], res[1:]


def _mix_bwd(att, y, dx1, wg, bg, ga, gs, wo):
    last = T // _TM_MIX - 1

    def body(att_ref, y_ref, d_ref, wg_ref, bg_ref, ga_ref, gs_ref, wo_ref,
             datt_ref, dy_ref, dwg_ref, dbg_ref, dga_ref, dgs_ref, dwo_ref, dwg_acc, dwo_acc):
        attv, yv, db = att_ref[...], y_ref[...], d_ref[...].astype(BF16)
        y2, y2b, sg, ssm, ra, rs = _mix_parts(attv, yv, wg_ref[...], bg_ref[...])
        na, ns = attv * ra, ssm * rs
        ma = (na * ga_ref[...]).astype(BF16)
        ms = (ns * gs_ref[...]).astype(BF16)
        dma = _nt(db, wo_ref[0:AW, :])
        dms = _nt(db, wo_ref[AW:D, :])
        datt, dga = _rms_bwd(na, ra, ga_ref[...], dma)
        dssm, dgs = _rms_bwd(ns, rs, gs_ref[...], dms)
        dgl = dssm * y2 * sg * (1.0 - sg)
        dglb = dgl.astype(BF16)
        dy2 = dssm * sg + _nt(dglb, wg_ref[...])
        _, gelu_vjp = jax.vjp(jax.nn.gelu, yv)
        datt_ref[...] = datt
        dy_ref[...] = gelu_vjp(dy2)[0]

        @pl.when(pl.program_id(0) == 0)
        def _():
            dwg_acc[...] = jnp.zeros_like(dwg_acc)
            dwo_acc[...] = jnp.zeros_like(dwo_acc)
            dbg_ref[...] = jnp.zeros_like(dbg_ref)
            dga_ref[...] = jnp.zeros_like(dga_ref)
            dgs_ref[...] = jnp.zeros_like(dgs_ref)
        dwg_acc[...] += _tn(y2b, dglb)
        dbg_ref[...] += jnp.sum(dgl, axis=0, keepdims=True)
        dga_ref[...] += dga
        dgs_ref[...] += dgs
        dwo_acc[0:AW, :] += _tn(ma, db)
        dwo_acc[AW:D, :] += _tn(ms, db)

        @pl.when(pl.program_id(0) == last)
        def _():
            dwg_ref[...] = dwg_acc[...].astype(BF16)
            dwo_ref[...] = dwo_acc[...].astype(BF16)

    row = lambda i: (i, 0)
    fixed = lambda i: (0, 0)
    half = pl.BlockSpec((_TM_MIX, AW), row)
    vec = pl.BlockSpec((1, AW), fixed)
    sq = pl.BlockSpec((SW, SW), fixed)
    big = pl.BlockSpec((D, D), fixed)
    return pl.pallas_call(
        body, name="mix_bwd", grid=(T // _TM_MIX,),
        in_specs=[half, half, pl.BlockSpec((_TM_MIX, D), row), sq, vec, vec, vec, big],
        out_specs=[half, half, sq, vec, vec, vec, big],
        out_shape=[_sds((T, AW)), _sds((T, SW)), _sds((SW, SW), BF16), _sds((1, SW)), _sds((1, AW)), _sds((1, SW)),
                   _sds((D, D), BF16)],
        scratch_shapes=[pltpu.VMEM((SW, SW), F32), pltpu.VMEM((D, D), F32)],
        compiler_params=_cp(("arbitrary",)),
    )(att, y, dx1, wg, bg, ga, gs, wo)


_NCB = -(-CS // LANES)


def _ffn_up(x1, g, wu4):
    tm = 1024

    def body(x_ref, g_ref, w_ref, h_ref, up_ref):
        @pl.when(pl.program_id(1) == 0)
        def _():
            xv = x_ref[...]
            h_ref[...] = (xv * _rms_r(xv) * g_ref[...]).astype(BF16)
        up_ref[0, 0] = _nt(h_ref[...], w_ref[0, 0])

    return pl.pallas_call(
        body, name="ffn_up", grid=(T // tm, NDEV),
        in_specs=[pl.BlockSpec((tm, D), lambda i, e: (i, 0)), pl.BlockSpec((1, D), lambda i, e: (0, 0)),
                  pl.BlockSpec((1, 1, CS, D), lambda i, e: (e // 4, e % 4, 0, 0))],
        out_specs=[pl.BlockSpec((tm, D), lambda i, e: (i, 0)),
                   pl.BlockSpec((1, 1, tm, CS), lambda i, e: (e // 4, e % 4, i, 0))],
        out_shape=[_sds((T, D), BF16), _sds((2, 4, T, CS))],
        compiler_params=_cp(("parallel", "arbitrary")),
    )(x1, g, wu4)


def _shift_down(h, k):
    row = lax.broadcasted_iota(jnp.int32, h.shape, 0)
    return jnp.where(row >= k, pltpu.roll(h, k, 0), 0.0)


def _shift_up(h, k):
    row = lax.broadcasted_iota(jnp.int32, h.shape, 0)
    return jnp.where(row < T - k, pltpu.roll(h, T - k, 0), 0.0)


def _conv(h, w, b):
    return w[0:1, :] * _shift_down(h, 2) + w[1:2, :] * _shift_down(h, 1) + w[2:3, :] * h + b


def _chan(gv, rows):
    return pl.BlockSpec((1, 1, rows, LANES), lambda d, j: (gv, d, 0, j))


def _ffn_act(up4, cw4, cb4):
    def body(ug_ref, uv_ref, wg_ref, wv_ref, bg_ref, bv_ref, act_ref):
        g = _conv(ug_ref[0, 0], wg_ref[0, 0], bg_ref[0, 0])
        v = _conv(uv_ref[0, 0], wv_ref[0, 0], bv_ref[0, 0])
        act_ref[0] = (g * jax.nn.sigmoid(g) * v).astype(BF16)

    return pl.pallas_call(
        body, name="ffn_act", grid=(4, _NCB),
        in_specs=[_chan(0, T), _chan(1, T), _chan(0, 3), _chan(1, 3), _chan(0, 1), _chan(1, 1)],
        out_specs=pl.BlockSpec((1, T, LANES), lambda d, j: (d, 0, j)), out_shape=_sds((4, T, CS), BF16),
        compiler_params=_cp(("parallel", "parallel")),
    )(up4, up4, cw4, cw4, cb4, cb4)


def _ffn_act_bwd(up4, dact, cw4, cb4):
    def conv_bwd(h, w, d):
        dup = w[2:3, :] * d + w[1:2, :] * _shift_up(d, 1) + w[0:1, :] * _shift_up(d, 2)
        dw = jnp.concatenate([jnp.sum(d * _shift_down(h, 2), axis=0, keepdims=True),
                              jnp.sum(d * _shift_down(h, 1), axis=0, keepdims=True),
                              jnp.sum(d * h, axis=0, keepdims=True)], axis=0)
        return dup, dw, jnp.sum(d, axis=0, keepdims=True)

    def body(ug_ref, uv_ref, da_ref, wg_ref, wv_ref, bg_ref, bv_ref, dup_ref, dw_ref, db_ref):
        ug, uv, da = ug_ref[0, 0], uv_ref[0, 0], da_ref[0]
        g = _conv(ug, wg_ref[0, 0], bg_ref[0, 0])
        v = _conv(uv, wv_ref[0, 0], bv_ref[0, 0])
        sg = jax.nn.sigmoid(g)
        dg = da * v * (sg * (1.0 + g * (1.0 - sg)))
        dv = da * (g * sg)
        dug, dwg, dbg = conv_bwd(ug, wg_ref[0, 0], dg)
        duv, dwv, dbv = conv_bwd(uv, wv_ref[0, 0], dv)
        dup_ref[0, 0] = dug.astype(BF16)
        dup_ref[1, 0] = duv.astype(BF16)
        dw_ref[0, 0] = dwg
        dw_ref[1, 0] = dwv
        db_ref[0, 0] = dbg
        db_ref[1, 0] = dbv

    both = lambda rows: pl.BlockSpec((2, 1, rows, LANES), lambda d, j: (0, d, 0, j))
    return pl.pallas_call(
        body, name="ffn_act_bwd", grid=(4, _NCB),
        in_specs=[_chan(0, T), _chan(1, T), pl.BlockSpec((1, T, LANES), lambda d, j: (d, 0, j)),
                  _chan(0, 3), _chan(1, 3), _chan(0, 1), _chan(1, 1)],
        out_specs=[both(T), both(3), both(1)],
        out_shape=[_sds((2, 4, T, CS), BF16), _sds((2, 4, 3, CS)), _sds((2, 4, 1, CS))],
        compiler_params=_cp(("parallel", "parallel")),
    )(up4, up4, dact, cw4, cw4, cb4, cb4)


def _ffn_down_loss(act, wd4, x1, target):
    tm = 512

    def body(a_ref, w_ref, x1_ref, t_ref, loss_ref, dx_ref, dxb_ref):
        x2 = x1_ref[...]
        for d in range(4):
            x2 = x2 + _nn(a_ref[d], w_ref[d])
        err = x2 - t_ref[...]
        dx = err * (1.0 / D)
        dx_ref[...] = dx
        dxb_ref[...] = dx.astype(BF16)

        @pl.when(pl.program_id(0) == 0)
        def _():
            loss_ref[...] = jnp.zeros_like(loss_ref)
        loss_ref[...] += 0.5 * jnp.sum(jnp.mean(err * err, axis=-1, keepdims=True), axis=0, keepdims=True)

    row = lambda i: (i, 0)
    fixed = lambda i: (0, 0)
    return pl.pallas_call(
        body, name="ffn_down_loss", grid=(T // tm,),
        in_specs=[pl.BlockSpec((4, tm, CS), lambda i: (0, i, 0)), pl.BlockSpec((4, CS, D), lambda i: (0, 0, 0)),
                  pl.BlockSpec((tm, D), row), pl.BlockSpec((tm, D), row)],
        out_specs=[pl.BlockSpec((1, 1), fixed), pl.BlockSpec((tm, D), row), pl.BlockSpec((tm, D), row)],
        out_shape=[_sds((1, 1)), _sds((T, D)), _sds((T, D), BF16)],
        compiler_params=_cp(("arbitrary",)),
    )(act, wd4, x1, target)


def _ffn_dact(dx2b, wd4):
    tm = 1024

    def body(d_ref, w_ref, o_ref):
        o_ref[0] = _nt(d_ref[...], w_ref[0])

    return pl.pallas_call(
        body, name="ffn_dact", grid=(4, T // tm),
        in_specs=[pl.BlockSpec((tm, D), lambda d, i: (i, 0)), pl.BlockSpec((1, CS, D), lambda d, i: (d, 0, 0))],
        out_specs=pl.BlockSpec((1, tm, CS), lambda d, i: (d, i, 0)), out_shape=_sds((4, T, CS)),
        compiler_params=_cp(("parallel", "parallel")),
    )(dx2b, wd4)


def _ffn_dwd(act, dx2b):
    def body(a_ref, d_ref, o_ref):
        o_ref[0] = _tn(a_ref[0], d_ref[...]).astype(BF16)

    return pl.pallas_call(
        body, name="ffn_dwd", grid=(4,),
        in_specs=[pl.BlockSpec((1, T, CS), lambda d: (d, 0, 0)), pl.BlockSpec((T, D), lambda d: (0, 0))],
        out_specs=pl.BlockSpec((1, CS, D), lambda d: (d, 0, 0)), out_shape=_sds((4, CS, D), BF16),
        compiler_params=_cp(("parallel",)),
    )(act, dx2b)


def _ffn_dwu(h2, dup4):
    tn = 512

    def body(h_ref, d_ref, o_ref):
        o_ref[0, 0] = _tn(d_ref[0, 0], h_ref[...]).astype(BF16)

    return pl.pallas_call(
        body, name="ffn_dwu", grid=(NDEV, D // tn),
        in_specs=[pl.BlockSpec((T, tn), lambda e, i: (0, i)),
                  pl.BlockSpec((1, 1, T, CS), lambda e, i: (e // 4, e % 4, 0, 0))],
        out_specs=pl.BlockSpec((1, 1, CS, tn), lambda e, i: (e // 4, e % 4, 0, i)),
        out_shape=_sds((2, 4, CS, D), BF16),
        compiler_params=_cp(("parallel", "parallel")),
    )(h2, dup4)


def _ffn_up_bwd(dup4, wu4, x1, g, dres):
    tm = 256

    def body(d_ref, w_ref, x_ref, g_ref, dres_ref, dx_ref, dg_ref):
        dh = jnp.zeros((tm, D), F32)
        for gv in range(2):
            for d in range(4):
                dh = dh + _nn(d_ref[gv, d], w_ref[gv, d])
        xv = x_ref[...]
        r = _rms_r(xv)
        dxr, dgp = _rms_bwd(xv * r, r, g_ref[...], dh)
        dx_ref[...] = dres_ref[...] + dxr

        @pl.when(pl.program_id(0) == 0)
        def _():
            dg_ref[...] = jnp.zeros_like(dg_ref)
        dg_ref[...] += dgp

    row = lambda i: (i, 0)
    fixed = lambda i: (0, 0)
    return pl.pallas_call(
        body, name="ffn_up_bwd", grid=(T // tm,),
        in_specs=[pl.BlockSpec((2, 4, tm, CS), lambda i: (0, 0, i, 0)), pl.BlockSpec((2, 4, CS, D), lambda i: (0, 0, 0, 0)),
                  pl.BlockSpec((tm, D), row), pl.BlockSpec((1, D), fixed), pl.BlockSpec((tm, D), row)],
        out_specs=[pl.BlockSpec((tm, D), row), pl.BlockSpec((1, D), fixed)],
        out_shape=[_sds((T, D)), _sds((1, D))],
        compiler_params=_cp(("arbitrary",)),
    )(dup4, wu4, x1, g, dres)


def _adamw_math(w, g, m, v):
    m = ADAM_B1 * m + (1.0 - ADAM_B1) * g
    v = ADAM_B2 * v + (1.0 - ADAM_B2) * jnp.square(g)
    m_hat = m / (1.0 - ADAM_B1 ** ADAM_STEP)
    v_hat = v / (1.0 - ADAM_B2 ** ADAM_STEP)
    delta = -ADAM_LR * (m_hat / (jnp.sqrt(v_hat) + ADAM_EPS) + ADAM_WD * w)
    return delta, m, v


def _where_am_i():
    x, y, c = _place()
    return jnp.stack([c, 2 * x + y]).astype(jnp.int32)


def _pair_sum(a4, recv, name):
    _, _, rows, cols = a4.shape
    tr = _row_tile(rows, cols, 3 << 20)

    def body(sel_ref, own_ref, recv_ref, out_ref):
        out_ref[...] = (own_ref[0].astype(F32) + recv_ref[...].astype(F32)).astype(out_ref.dtype)

    grid_spec = pltpu.PrefetchScalarGridSpec(
        num_scalar_prefetch=1, grid=(4, rows // tr),
        in_specs=[pl.BlockSpec((1, 1, tr, cols), lambda k, i, s: (k, s[0], i, 0)),
                  pl.BlockSpec((1, tr, cols), lambda k, i, s: (k, i, 0))],
        out_specs=pl.BlockSpec((1, tr, cols), lambda k, i, s: (k, i, 0)),
    )
    return pl.pallas_call(
        body, name=name, grid_spec=grid_spec, out_shape=_sds((4, rows, cols), a4.dtype),
        compiler_params=_cp(("arbitrary", "arbitrary")),
    )(_where_am_i(), a4, recv)


def _adamw_rs(a4, recv, from_chips, w, m, v, name):
    rows, cols = w.shape
    tr = _row_tile(rows, cols, 3 << 19)

    def body(sel_ref, own_ref, recv_ref, fc_ref, w_ref, m_ref, v_ref, g_ref, d_ref, nm_ref, nv_ref):
        g = own_ref[0, 0].astype(F32) + recv_ref[0].astype(F32)
        for j in range(3):
            g = g + fc_ref[j].astype(F32)
        d, nm, nv = _adamw_math(w_ref[...], g, m_ref[...], v_ref[...])
        g_ref[...] = g
        d_ref[...] = d
        nm_ref[...] = nm
        nv_ref[...] = nv

    flat = pl.BlockSpec((tr, cols), lambda i, s: (i, 0))
    grid_spec = pltpu.PrefetchScalarGridSpec(
        num_scalar_prefetch=1, grid=(rows // tr,),
        in_specs=[pl.BlockSpec((1, 1, tr, cols), lambda i, s: (s[1], s[0], i, 0)),
                  pl.BlockSpec((1, tr, cols), lambda i, s: (s[1], i, 0)),
                  pl.BlockSpec((3, tr, cols), lambda i, s: (0, i, 0)), flat, flat, flat],
        out_specs=[flat] * 4,
    )
    return pl.pallas_call(
        body, name=name, grid_spec=grid_spec, out_shape=[_sds((rows, cols))] * 4,
        compiler_params=_cp(("arbitrary",)),
    )(_where_am_i(), a4, recv, from_chips, w, m, v)


_SMALL_F32 = (("g_mix", (8, 128)), ("b_f", (1, 8)), ("g_q", (1, 64)), ("g_k", (1, 64)), ("lambda_re", (32, 64)),
              ("lambda_im", (32, 64)), ("log_step", (1, 32)), ("d_skip", (32, 16)), ("b_glu", (4, 128)),
              ("g_attn_out", (4, 128)), ("g_ssm_out", (4, 128)), ("g_ffn", (8, 128)), ("conv_b", (44, 128)))
_SMALL_PAIRS = (("b_re", "b_im"), ("c_re", "c_im"))
_PAIR_SHAPE = (NG * GS, NP)


def _ceil8(r):
    return -(-r // 8) * 8


def _small_names():
    return [n for n, _ in _SMALL_F32] + [n for pair in _SMALL_PAIRS for n in pair]


def _pack_small_grads(g, loss_dev):
    parts = [jnp.pad(g[n].reshape(r, c), ((0, _ceil8(r) - r), (0, LANES - c))) for n, (r, c) in _SMALL_F32]
    parts.append(jnp.pad(loss_dev, ((0, 7), (0, LANES - 1))))
    pairs = [jnp.concatenate([g[a].reshape(_PAIR_SHAPE), g[b].reshape(_PAIR_SHAPE)], axis=1) for a, b in _SMALL_PAIRS]
    return jnp.concatenate(parts, axis=0), jnp.concatenate(pairs, axis=0).astype(BF16)


def _adamw_small(gall, gall_b, wmv):
    names = _small_names()
    shapes = [s for _, s in _SMALL_F32] + [_PAIR_SHAPE] * (2 * len(_SMALL_PAIRS))
    npar = len(names)

    def body(ga_ref, gb_ref, *rest):
        ins, loss_ref, outs = rest[:3 * npar], rest[3 * npar], rest[3 * npar + 1:]
        ga, gb = ga_ref[0], gb_ref[0].astype(F32)
        for dev in range(1, NDEV):
            ga = ga + ga_ref[dev]
            gb = gb + gb_ref[dev].astype(F32)
        grads, off = [], 0
        for _, (r, c) in _SMALL_F32:
            grads.append(ga[off:off + r, 0:c])
            off += _ceil8(r)
        loss_ref[...] = ga[off:off + 8]
        rows, cols = _PAIR_SHAPE
        for j in range(len(_SMALL_PAIRS)):
            for h in range(2):
                grads.append(gb[rows * j:rows * (j + 1), cols * h:cols * (h + 1)])
        for i, g in enumerate(grads):
            d, nm, nv = _adamw_math(ins[3 * i][...], g, ins[3 * i + 1][...], ins[3 * i + 2][...])
            outs[4 * i][...] = g
            outs[4 * i + 1][...] = d
            outs[4 * i + 2][...] = nm
            outs[4 * i + 3][...] = nv

    flat = [a for n in names for a in wmv[n]]
    out_shape = [_sds((8, LANES))] + [_sds(s) for s in shapes for _ in range(4)]
    res = pl.pallas_call(body, name="adamw_small", out_shape=out_shape, compiler_params=_cp())(gall, gall_b, *flat)
    return res[0], {n: res[1 + 4 * i:5 + 4 * i] for i, n in enumerate(names)}


def _block_diag(m):
    eye = jnp.eye(8, dtype=m.dtype)
    r, c = m.shape[2], m.shape[3]
    return (m[:, :, :, None, :] * eye[None, :, None, :, None]).reshape(4, 8 * r, 8 * c)


def _diag_blocks(dense, r, c):
    eye = jnp.eye(8, dtype=dense.dtype)
    return jnp.sum(dense.reshape(4, 8, r, 8, c) * eye[None, :, None, :, None], axis=3)


def kernel(x, g_mix, w_in, b_f, g_q, g_k, lambda_re, lambda_im, log_step, b_re, b_im, c_re, c_im, d_skip, w_glu, b_glu, g_attn_out, g_ssm_out, w_out, g_ffn, w_up, conv_w, conv_b, w_down, loss_target, m_g_mix, m_w_in, m_b_f, m_g_q, m_g_k, m_lambda_re, m_lambda_im, m_log_step, m_b_re, m_b_im, m_c_re, m_c_im, m_d_skip, m_w_glu, m_b_glu, m_g_attn_out, m_g_ssm_out, m_w_out, m_g_ffn, m_w_up, m_conv_w, m_conv_b, m_w_down, v_g_mix, v_w_in, v_b_f, v_g_q, v_g_k, v_lambda_re, v_lambda_im, v_log_step, v_b_re, v_b_im, v_c_re, v_c_im, v_d_skip, v_w_glu, v_b_glu, v_g_attn_out, v_g_ssm_out, v_w_out, v_g_ffn, v_w_up, v_conv_w, v_conv_b, v_w_down):
    weights = dict(g_mix=g_mix, w_in=w_in, b_f=b_f, g_q=g_q, g_k=g_k, lambda_re=lambda_re, lambda_im=lambda_im,
                   log_step=log_step, b_re=b_re, b_im=b_im, c_re=c_re, c_im=c_im, d_skip=d_skip, w_glu=w_glu,
                   b_glu=b_glu, g_attn_out=g_attn_out, g_ssm_out=g_ssm_out, w_out=w_out, g_ffn=g_ffn, w_up=w_up,
                   conv_w=conv_w, conv_b=conv_b, w_down=w_down)
    mom_m = dict(g_mix=m_g_mix, w_in=m_w_in, b_f=m_b_f, g_q=m_g_q, g_k=m_g_k, lambda_re=m_lambda_re,
                 lambda_im=m_lambda_im, log_step=m_log_step, b_re=m_b_re, b_im=m_b_im, c_re=m_c_re, c_im=m_c_im,
                 d_skip=m_d_skip, w_glu=m_w_glu, b_glu=m_b_glu, g_attn_out=m_g_attn_out, g_ssm_out=m_g_ssm_out,
                 w_out=m_w_out, g_ffn=m_g_ffn, w_up=m_w_up, conv_w=m_conv_w, conv_b=m_conv_b, w_down=m_w_down)
    mom_v = dict(g_mix=v_g_mix, w_in=v_w_in, b_f=v_b_f, g_q=v_g_q, g_k=v_g_k, lambda_re=v_lambda_re,
                 lambda_im=v_lambda_im, log_step=v_log_step, b_re=v_b_re, b_im=v_b_im, c_re=v_c_re, c_im=v_c_im,
                 d_skip=v_d_skip, w_glu=v_w_glu, b_glu=v_b_glu, g_attn_out=v_g_attn_out, g_ssm_out=v_g_ssm_out,
                 w_out=v_w_out, g_ffn=v_g_ffn, w_up=v_w_up, conv_w=v_conv_w, conv_b=v_conv_b, w_down=v_w_down)
    names = list(weights)
    big = ("w_in", "w_glu", "w_out", "w_up", "w_down", "conv_w")

    xs = x[0]
    target = loss_target[0]
    row = lambda a: a.reshape(1, -1)

    (win_g,) = _all_gather([w_in.T.astype(BF16)], "gather_w_in")
    w_in_t = win_g.reshape(8 * 257, D)
    wcat = jnp.concatenate([w_in_t[0:1536], w_in_t[1544:2056], jnp.pad(w_in_t[1536:1544], ((0, LANES - NH), (0, 0)))],
                           axis=0)
    cb4 = conv_b.reshape(2, 4, 1, CS)

    h1, z, f_t, (wg_g, wo_g) = _in_proj(xs, row(g_mix), wcat, [w_glu.astype(BF16), w_out.astype(BF16)])
    bf_col = b_f.reshape(NH, 1)
    ck = _gates_fwd(f_t, bf_col).reshape(_HP, 2, T)
    att, lse, (wu_g, cw_g), (wg_g, wo_g) = _attn_fwd(z, ck, row(g_q), row(g_k), [w_up.T.astype(BF16), conv_w],
                                                       [wg_g, wo_g])

    ls_col = log_step.reshape(NG, 1)
    br_t, bi_t = b_re.transpose(0, 2, 1), b_im.transpose(0, 2, 1)
    ab_re, ab_im, bb_re, bb_im = _disc_fwd(lambda_re, lambda_im, ls_col, br_t, bi_t)
    bb = jnp.concatenate([_block_diag(bb_re.reshape(4, 8, GS, NP)), _block_diag(bb_im.reshape(4, 8, GS, NP))],
                         axis=2).astype(BF16)
    ab = jnp.stack([ab_re.reshape(4, 512), ab_im.reshape(4, 512)], axis=1)
    cdiag = lambda cm: _block_diag(cm.reshape(4, 8, GS, NP).transpose(0, 1, 3, 2))
    cc = jnp.concatenate([cdiag(c_re), -cdiag(c_im)], axis=1).astype(BF16)
    dsk = d_skip.reshape(4, 1, LANES)
    u_il = _interleave(z[:, 1536:2048])
    y_il, xs_il, (wd_g,) = _ssm_fwd(u_il, bb, ab, cc, dsk, [w_down.astype(BF16)])
    y = _deinterleave(y_il)
    wg_f = wg_g.reshape(SW, SW)
    wo_f = wo_g.reshape(D, D)
    x1, (wu_g, cw_g, wd_g) = _mix_fwd(att, y, xs, wg_f, row(b_glu), row(g_attn_out), row(g_ssm_out), wo_f,
                                      [wu_g, cw_g, wd_g])
    wu4 = wu_g.reshape(2, 4, CS, D)
    wd4 = wd_g.reshape(4, CS, D)
    cw4 = cw_g.reshape(2, 4, 3, CS)
    h2, up4 = _ffn_up(x1, row(g_ffn), wu4)
    act = _ffn_act(up4, cw4, cb4)
    loss_dev, dx2, dx2b = _ffn_down_loss(act, wd4, x1, target)

    dact = _ffn_dact(dx2b, wd4)
    d_wd = _ffn_dwd(act, dx2b)
    dup4, dcw4, dcb4 = _ffn_act_bwd(up4, dact, cw4, cb4)
    d_wu = _ffn_dwu(h2, dup4)
    dx1, dg_ffn = _ffn_up_bwd(dup4, wu4, x1, row(g_ffn), dx2)
    datt, dy, d_wg, d_bg, d_ga, d_gs, d_wo = _mix_bwd(att, y, dx1, wg_f, row(b_glu), row(g_attn_out),
                                                       row(g_ssm_out), wo_f)
    early = ("w_up", "w_down", "conv_w", "w_out", "w_glu")
    by_dest = {"w_up": d_wu.reshape(4, 2, CS, D), "w_down": d_wd.reshape(4, 2, DFF // NDEV, D),
               "conv_w": dcw4.reshape(4, 2, 3, CS), "w_out": d_wo.reshape(4, 2, D // NDEV, D),
               "w_glu": d_wg.reshape(4, 2, SW // NDEV, SW)}
    du_il, dbb, dab, dcc, ddsk, fs = _ssm_bwd(u_il, _interleave(dy), xs_il, bb, ab, cc, dsk, [by_dest[n] for n in early])
    from_sibling = dict(zip(early, fs))
    chip_sums = {n: _pair_sum(by_dest[n], from_sibling[n], "rs_pair_sum_" + n) for n in early}
    du = _deinterleave(du_il).astype(BF16)
    dbb_re = _diag_blocks(dbb[:, :, 0:512], GS, NP).reshape(NG, GS, NP)
    dbb_im = _diag_blocks(dbb[:, :, 512:1024], GS, NP).reshape(NG, GS, NP)
    dlr, dli, dls, dbr_t, dbi_t = _disc_bwd(lambda_re, lambda_im, ls_col, br_t, bi_t,
                                            dab[:, 0].reshape(NG, NP), dab[:, 1].reshape(NG, NP), dbb_re, dbb_im)
    d_cre = _diag_blocks(dcc[:, 0:512], NP, GS).transpose(0, 1, 3, 2).reshape(NG, GS, NP)
    d_cim = -_diag_blocks(dcc[:, 512:1024], NP, GS).transpose(0, 1, 3, 2).reshape(NG, GS, NP)

    dq, dk, dv, dck, d_gq, d_gk, fc = _attn_bwd(z, ck, row(g_q), row(g_k), lse, datt, [chip_sums[n] for n in early])
    from_chips = dict(zip(early, fc))
    df_t, d_bf = _gates_bwd(dck.reshape(NH, T), f_t, bf_col)
    df = jnp.concatenate([df_t.T, jnp.zeros((T, LANES - NH), F32)], axis=1).astype(BF16)
    d_wcat = _in_proj_dw(h1, dq, dk, dv, du, df)
    d_win = jnp.concatenate([d_wcat[0:1536], d_wcat[2048:2048 + NH], d_wcat[1536:2048]], axis=0)
    by_dest["w_in"] = d_win.astype(BF16).reshape(4, 2, 257, D)
    (from_sibling["w_in"],) = _swap_sibling([by_dest["w_in"]], "rs_pair_swap_w_in")
    chip_sums["w_in"] = _pair_sum(by_dest["w_in"], from_sibling["w_in"], "rs_pair_sum_w_in")
    grad_x, dg_mix, (from_chips["w_in"],) = _in_proj_bwd(dq, dk, dv, du, df, wcat, xs, row(g_mix), dx1, [chip_sums["w_in"]])

    grads, deltas, new_m, new_v = {}, {}, {}, {}
    for n in early + ("w_in",):
        flip = (lambda a: a.T) if n in ("w_up", "w_in") else (lambda a: a)
        outs = _adamw_rs(by_dest[n], from_sibling[n], from_chips[n], flip(weights[n]), flip(mom_m[n]), flip(mom_v[n]),
                         "adamw_" + n)
        grads[n], deltas[n], new_m[n], new_v[n] = (flip(o) for o in outs)

    small = dict(g_mix=dg_mix, b_f=d_bf, g_q=d_gq, g_k=d_gk, lambda_re=dlr, lambda_im=dli, log_step=dls,
                 b_re=dbr_t, b_im=dbi_t, c_re=d_cre, c_im=d_cim, d_skip=ddsk, b_glu=d_bg, g_attn_out=d_ga,
                 g_ssm_out=d_gs, g_ffn=dg_ffn, conv_b=dcb4)
    small_all, small_all_b = _all_gather(list(_pack_small_grads(small, loss_dev)), "gather_small_grads")
    views = dict(_SMALL_F32)
    swapped = ("b_re", "b_im")

    def view(n, a):
        return a.transpose(0, 2, 1).reshape(_PAIR_SHAPE) if n in swapped else a.reshape(views.get(n, _PAIR_SHAPE))

    loss_p, small_out = _adamw_small(small_all, small_all_b,
                                     {n: (view(n, weights[n]), view(n, mom_m[n]), view(n, mom_v[n])) for n in small})
    loss = loss_p[0, 0]
    for n in small:
        back = (lambda a: a.reshape(NG, GS, NP).transpose(0, 2, 1)) if n in swapped else (lambda a: a.reshape(weights[n].shape))
        grads[n], deltas[n], new_m[n], new_v[n] = (back(o) for o in small_out[n])

    return (loss, grad_x[None], *[grads[n] for n in names], *[deltas[n] for n in names],
            *[new_m[n] for n in names], *[new_v[n] for n in names])
```

```python
import jax
import jax.numpy as jnp
from jax import lax
from jax.experimental import pallas as pl
from jax.experimental.pallas import tpu as pltpu

F32, BF16 = jnp.float32, jnp.bfloat16
T, D = 2048, 1024
NH, DH = 8, 64
AW, SW = 512, 512
NG, GS, NP = 32, 16, 64
DFF = 2816
NDEV = 8
CS = 2 * DFF // NDEV
EPS = 1e-6
NEG = -1e30
ZC = 4 * 512 + 128
SEG = 256
NSEG = T // SEG
BQ = 256
VMEM_LIMIT = 56 * 1024 * 1024
LANES = 128
MESH = pl.DeviceIdType.MESH
HI = lax.Precision.HIGHEST

ADAM_LR, ADAM_B1, ADAM_B2, ADAM_EPS, ADAM_WD, ADAM_STEP = 0.001, 0.9, 0.999, 1e-08, 0.01, 10


def _cp(dims=None):
    if dims is None:
        return pltpu.CompilerParams(vmem_limit_bytes=VMEM_LIMIT)
    return pltpu.CompilerParams(dimension_semantics=dims, vmem_limit_bytes=VMEM_LIMIT)


def _sds(shape, dtype=F32):
    return jax.ShapeDtypeStruct(shape, dtype)


def _nt(a, b):
    return lax.dot_general(a, b, (((1,), (1,)), ((), ())), preferred_element_type=F32)


def _tn(a, b):
    return lax.dot_general(a, b, (((0,), (0,)), ((), ())), preferred_element_type=F32)


def _nn(a, b):
    return jnp.dot(a, b, preferred_element_type=F32)


def _rms_r(x):
    return lax.rsqrt(jnp.mean(x * x, axis=-1, keepdims=True) + EPS)


def _rms_bwd(n, r, g, dh):
    dn = dh * g
    dx = r * (dn - n * jnp.mean(dn * n, axis=-1, keepdims=True))
    return dx, jnp.sum(dh * n, axis=0, keepdims=True)


def _row_tile(rows, cols, limit):
    tr = rows
    while tr * cols * 4 > limit and tr % 32 == 0:
        tr //= 2
    return tr


def _place():
    return lax.axis_index("x"), lax.axis_index("y"), lax.axis_index("c")


def _all_gather(shards, name):
    n = len(shards)

    def body(*refs):
        x_refs, out_refs = refs[:n], refs[n:2 * n]
        send_sems, recv_sems, local_sems = refs[2 * n:]
        x, y, c = _place()
        me, sibling = (x, y, c), (x, y, 1 - c)
        chips = [(1 - x, y), (x, 1 - y), (1 - x, 1 - y)]

        def copy(a, k, block, to, src=None):
            px, py, pc = block
            slot = out_refs[a].at[4 * px + 2 * py + pc]
            return pltpu.make_async_remote_copy(
                src_ref=slot if src is None else src, dst_ref=slot,
                send_sem=send_sems.at[7 * a + k], recv_sem=recv_sems.at[7 * a + k], device_id=to, device_id_type=MESH)

        mine, first, passed = [], [], []
        for a in range(n):
            cp = pltpu.make_async_copy(x_refs[a], out_refs[a].at[4 * x + 2 * y + c], local_sems.at[a])
            cp.start()
            mine.append(cp)
            cps = [copy(a, 0, me, sibling, src=x_refs[a])]
            cps += [copy(a, 1 + j, me, (*chip, c), src=x_refs[a]) for j, chip in enumerate(chips)]
            for cp in cps:
                cp.start()
            first += cps
        for a in range(n):
            for j, chip in enumerate(chips):
                copy(a, 1 + j, (*chip, c), me).wait_recv()
                fwd = copy(a, 4 + j, (*chip, c), sibling)
                fwd.start()
                passed.append(fwd)
        for a in range(n):
            copy(a, 0, sibling, me).wait_recv()
            for j, chip in enumerate(chips):
                copy(a, 4 + j, (*chip, 1 - c), me).wait_recv()
        for cp in first + passed:
            cp.wait_send()
        for cp in mine:
            cp.wait()

    anyspec = pl.BlockSpec(memory_space=pl.ANY)
    return pl.pallas_call(
        body, name=name,
        out_shape=[_sds((NDEV, *s.shape), s.dtype) for s in shards],
        in_specs=[anyspec] * n, out_specs=[anyspec] * n,
        scratch_shapes=[pltpu.SemaphoreType.DMA((7 * n,)), pltpu.SemaphoreType.DMA((7 * n,)),
                        pltpu.SemaphoreType.DMA((n,))],
    )(*shards)


def _gather_first_copies(x_refs, out_refs, send_sems, recv_sems, local_sems):
    x, y, c = _place()
    slot = 4 * x + 2 * y + c
    peers = [(x, y, 1 - c), (1 - x, y, c), (x, 1 - y, c), (1 - x, 1 - y, c)]
    local, remote = [], []
    for a, (x_ref, out_ref) in enumerate(zip(x_refs, out_refs)):
        local.append(pltpu.make_async_copy(x_ref, out_ref.at[slot], local_sems.at[a]))
        for k, peer in enumerate(peers):
            remote.append(pltpu.make_async_remote_copy(
                src_ref=x_ref, dst_ref=out_ref.at[slot], send_sem=send_sems.at[4 * a + k],
                recv_sem=recv_sems.at[4 * a + k], device_id=peer, device_id_type=MESH))
    return local, remote


def _gather_first_scratch(n):
    return [pltpu.SemaphoreType.DMA((4 * n,)), pltpu.SemaphoreType.DMA((4 * n,)), pltpu.SemaphoreType.DMA((n,))]


def _chip_swap_copies(p_refs, out_refs, send_sems, recv_sems):
    x, y, c = _place()
    chips = [(1 - x, y), (x, 1 - y), (1 - x, 1 - y)]
    cps = []
    for a, (p_ref, out_ref) in enumerate(zip(p_refs, out_refs)):
        for j, (px, py) in enumerate(chips):
            cps.append(pltpu.make_async_remote_copy(
                src_ref=p_ref.at[2 * px + py], dst_ref=out_ref.at[j],
                send_sem=send_sems.at[3 * a + j], recv_sem=recv_sems.at[3 * a + j],
                device_id=(px, py, c), device_id_type=MESH))
    return cps


def _sibling_swap_copies(a_refs, out_refs, send_sems, recv_sems):
    x, y, c = _place()
    cps = []
    for a, (a_ref, out_ref) in enumerate(zip(a_refs, out_refs)):
        for k in range(4):
            cps.append(pltpu.make_async_remote_copy(
                src_ref=a_ref.at[k, 1 - c], dst_ref=out_ref.at[k],
                send_sem=send_sems.at[4 * a + k], recv_sem=recv_sems.at[4 * a + k],
                device_id=(x, y, 1 - c), device_id_type=MESH))
    return cps


def _pass_on_copies(buf_refs, send_sems, recv_sems):
    x, y, c = _place()
    chips = [(1 - x, y), (x, 1 - y), (1 - x, 1 - y)]
    cps = []
    for a, buf in enumerate(buf_refs):
        for j, (px, py) in enumerate(chips):
            block = buf.at[4 * px + 2 * py + c]
            cps.append(pltpu.make_async_remote_copy(
                src_ref=block, dst_ref=block, send_sem=send_sems.at[3 * a + j], recv_sem=recv_sems.at[3 * a + j],
                device_id=(x, y, 1 - c), device_id_type=MESH))
    return cps


def _swap_sibling(arrs, name):
    n = len(arrs)

    def body(*refs):
        cps = _sibling_swap_copies(refs[:n], refs[n:2 * n], *refs[2 * n:])
        for cp in cps:
            cp.start()
        for cp in cps:
            cp.wait()

    anyspec = pl.BlockSpec(memory_space=pl.ANY)
    return pl.pallas_call(
        body, name=name,
        out_shape=[_sds((4, *a.shape[2:]), a.dtype) for a in arrs],
        in_specs=[anyspec] * n, out_specs=[anyspec] * n,
        scratch_shapes=[pltpu.SemaphoreType.DMA((4 * n,)), pltpu.SemaphoreType.DMA((4 * n,))],
    )(*arrs)


def _in_proj(x, g, wcat, shards):
    tm = 512
    n = len(shards)

    def body(x_ref, g_ref, w_ref, *rest):
        x_refs, (h_ref, z_ref, ft_ref), out_refs, sems = rest[:n], rest[n:n + 3], rest[n + 3:2 * n + 3], rest[2 * n + 3:]

        @pl.when(pl.program_id(0) == 0)
        def _():
            local, remote = _gather_first_copies(x_refs, out_refs, *sems)
            for cp in local + remote:
                cp.start()

        xv = x_ref[...]
        h = (xv * _rms_r(xv) * g_ref[...]).astype(BF16)
        h_ref[...] = h
        z = _nt(h, w_ref[...])
        z_ref[...] = z
        ft_ref[...] = z[:, 2048:ZC].T[0:NH, :]

        @pl.when(pl.program_id(0) == T // tm - 1)
        def _():
            local, remote = _gather_first_copies(x_refs, out_refs, *sems)
            for cp in remote + local:
                cp.wait()

    row = lambda i: (i, 0)
    fixed = lambda i: (0, 0)
    anyspec = pl.BlockSpec(memory_space=pl.ANY)
    res = pl.pallas_call(
        body, name="in_proj", grid=(T // tm,),
        in_specs=[pl.BlockSpec((tm, D), row), pl.BlockSpec((1, D), fixed), pl.BlockSpec((ZC, D), fixed)] + [anyspec] * n,
        out_specs=[pl.BlockSpec((tm, D), row), pl.BlockSpec((tm, ZC), row), pl.BlockSpec((NH, tm), lambda i: (0, i))]
        + [anyspec] * n,
        out_shape=[_sds((T, D), BF16), _sds((T, ZC)), _sds((NH, T))] + [_sds((NDEV, *s.shape), s.dtype) for s in shards],
        scratch_shapes=_gather_first_scratch(n),
        compiler_params=_cp(("arbitrary",)),
    )(x, g, wcat, *shards)
    return res[0], res[1], res[2], res[3:]


def _in_proj_dw(h, dq, dk, dv, du, df):
    tm = 1024

    def body(h_ref, dq_ref, dk_ref, dv_ref, du_ref, df_ref, o_ref):
        hv = h_ref[...]
        for j, d_ref in enumerate((dq_ref, dk_ref, dv_ref, du_ref)):
            o_ref[512 * j:512 * (j + 1), :] = _tn(d_ref[...], hv)
        o_ref[2048:ZC, :] = _tn(df_ref[...], hv)

    full = lambda w: pl.BlockSpec((T, w), lambda i: (0, 0))
    return pl.pallas_call(
        body, name="in_proj_dw", grid=(D // tm,),
        in_specs=[pl.BlockSpec((T, tm), lambda i: (0, i)), full(512), full(512), full(512), full(512), full(LANES)],
        out_specs=pl.BlockSpec((ZC, tm), lambda i: (0, i)), out_shape=_sds((ZC, D)),
        compiler_params=_cp(("parallel",)),
    )(h, dq, dk, dv, du, df)


def _in_proj_bwd(dq, dk, dv, du, df, wcat, x, g, dres, parts):
    tm = 512
    n = len(parts)

    def body(dq_ref, dk_ref, dv_ref, du_ref, df_ref, w_ref, x_ref, g_ref, dres_ref, *rest):
        p_refs, (dx_ref, dg_ref), fc_refs, sems = rest[:n], rest[n:n + 2], rest[n + 2:2 * n + 2], rest[2 * n + 2:]

        @pl.when(pl.program_id(0) == 0)
        def _():
            for cp in _chip_swap_copies(p_refs, fc_refs, *sems):
                cp.start()

        dh = _nn(df_ref[...], w_ref[2048:ZC, :])
        for j, d_ref in enumerate((dq_ref, dk_ref, dv_ref, du_ref)):
            dh = dh + _nn(d_ref[...], w_ref[512 * j:512 * (j + 1), :])
        xv = x_ref[...]
        r = _rms_r(xv)
        dxr, dgp = _rms_bwd(xv * r, r, g_ref[...], dh)
        dx_ref[...] = dres_ref[...] + dxr

        @pl.when(pl.program_id(0) == 0)
        def _():
            dg_ref[...] = jnp.zeros_like(dg_ref)
        dg_ref[...] += dgp

        @pl.when(pl.program_id(0) == T // tm - 1)
        def _():
            for cp in _chip_swap_copies(p_refs, fc_refs, *sems):
                cp.wait()

    row = lambda i: (i, 0)
    fixed = lambda i: (0, 0)
    part = pl.BlockSpec((tm, 512), row)
    anyspec = pl.BlockSpec(memory_space=pl.ANY)
    res = pl.pallas_call(
        body, name="in_proj_bwd", grid=(T // tm,),
        in_specs=[part, part, part, part, pl.BlockSpec((tm, LANES), row), pl.BlockSpec((ZC, D), fixed),
                  pl.BlockSpec((tm, D), row), pl.BlockSpec((1, D), fixed), pl.BlockSpec((tm, D), row)] + [anyspec] * n,
        out_specs=[pl.BlockSpec((tm, D), row), pl.BlockSpec((1, D), fixed)] + [anyspec] * n,
        out_shape=[_sds((T, D)), _sds((1, D))] + [_sds((3, *p.shape[1:]), p.dtype) for p in parts],
        scratch_shapes=[pltpu.SemaphoreType.DMA((3 * n,)), pltpu.SemaphoreType.DMA((3 * n,))],
        compiler_params=_cp(("arbitrary",)),
    )(dq, dk, dv, du, df, wcat, x, g, dres, *parts)
    return res[0], res[1], res[2:]


def _stack_lanes(v):
    return jnp.concatenate([v[:, LANES * b:LANES * (b + 1)] for b in range(T // LANES)], axis=0)


def _unstack_lanes(s):
    return jnp.concatenate([s[8 * b:8 * b + 8, :] for b in range(T // LANES)], axis=1)


def _cumsum_lanes(v, reverse):
    st = _stack_lanes(v)
    ri = lax.broadcasted_iota(jnp.int32, (LANES, LANES), 0)
    ci = lax.broadcasted_iota(jnp.int32, (LANES, LANES), 1)
    tri = (ri >= ci) if reverse else (ri <= ci)
    intra = jnp.dot(st, tri.astype(F32), precision=HI, preferred_element_type=F32)
    tot = intra[:, 0:1] if reverse else intra[:, LANES - 1:LANES]
    same_head = (ci % 8) == (ri % 8)
    other = (ci // 8 > ri // 8) if reverse else (ci // 8 < ri // 8)
    off = jnp.dot((same_head & other).astype(F32), jnp.broadcast_to(tot, (LANES, LANES)), precision=HI,
                  preferred_element_type=F32)
    return _unstack_lanes(intra + off)


def _gates_fwd(f_t, b_f):
    def body(f_ref, b_ref, c_ref):
        z = f_ref[...] + b_ref[...]
        lf = jnp.minimum(z, 0.0) - jnp.log(1.0 + jnp.exp(-jnp.abs(z)))
        c_ref[...] = _cumsum_lanes(lf, reverse=False)

    return pl.pallas_call(body, name="gates_fwd", out_shape=_sds((NH, T)), compiler_params=_cp())(f_t, b_f)


def _gates_bwd(dck, f_t, b_f):
    def body(d_ref, f_ref, b_ref, df_ref, db_ref):
        z = f_ref[...] + b_ref[...]
        dlf = _cumsum_lanes(d_ref[...], reverse=True)
        df = dlf * jax.nn.sigmoid(-z)
        df_ref[...] = df
        db_ref[...] = jnp.sum(df, axis=1, keepdims=True)

    return pl.pallas_call(body, name="gates_bwd", out_shape=[_sds((NH, T)), _sds((NH, 1))],
                          compiler_params=_cp())(dck, f_t, b_f)


def _lower_triangle():
    rows = lax.broadcasted_iota(jnp.int32, (BQ, BQ), 0)
    cols = lax.broadcasted_iota(jnp.int32, (BQ, BQ), 1)
    return cols <= rows


def _logit_parts(qb, kb, ck_row, lo, hi, tri):
    parts = []
    if lo > 0:
        parts.append((_nt(qb[lo:hi], kb[0:lo]) - ck_row[:, 0:lo], 0, lo))
    parts.append((jnp.where(tri, _nt(qb[lo:hi], kb[lo:hi]) - ck_row[:, lo:hi], NEG), lo, hi))
    return parts


_HP = NH // 2


def _qkv_specs():
    return [pl.BlockSpec((T, LANES), lambda p: (0, p)), pl.BlockSpec((T, LANES), lambda p: (0, _HP + p)),
            pl.BlockSpec((T, LANES), lambda p: (0, 2 * _HP + p))]


def _attn_fwd(z, ck, g_q, g_k, shards, bufs):
    scale = DH ** -0.5
    n, m = len(shards), len(bufs)

    def body(q_ref, k_ref, v_ref, ck_ref, gq_ref, gk_ref, *rest):
        x_refs, (o_ref, lse_ref) = rest[:n], rest[n + m:n + m + 2]
        out_refs, buf_refs, sems = rest[n + m + 2:2 * n + m + 2], rest[2 * n + m + 2:2 * n + 2 * m + 2], rest[2 * n + 2 * m + 2:]

        @pl.when(pl.program_id(0) == 0)
        def _():
            local, remote = _gather_first_copies(x_refs, out_refs, *sems[:3])
            for cp in local + remote + _pass_on_copies(buf_refs, *sems[3:]):
                cp.start()

        qb, kb, vb = [], [], []
        for hh in range(2):
            cols = slice(DH * hh, DH * (hh + 1))
            qv, kv = q_ref[:, cols], k_ref[:, cols]
            qb.append((qv * _rms_r(qv) * gq_ref[...] * scale).astype(BF16))
            kb.append((kv * _rms_r(kv) * gk_ref[...]).astype(BF16))
            vb.append(v_ref[:, cols].astype(BF16))
        tri = _lower_triangle()
        for i in range(T // BQ):
            lo, hi = i * BQ, (i + 1) * BQ
            outs = []
            for hh in range(2):
                parts = _logit_parts(qb[hh], kb[hh], ck_ref[0, hh:hh + 1, :], lo, hi, tri)
                top = jnp.max(parts[-1][0], axis=-1, keepdims=True)
                if lo > 0:
                    top = jnp.maximum(top, jnp.max(parts[0][0], axis=-1, keepdims=True))
                l, o = 0.0, 0.0
                for s, k0, k1 in parts:
                    p = jnp.exp(s - top)
                    l = l + jnp.sum(p, axis=-1, keepdims=True)
                    o = o + _nn(p.astype(BF16), vb[hh][k0:k1])
                outs.append(o / l)
                lse_ref[0, lo:hi, hh:hh + 1] = top + jnp.log(l)
            o_ref[lo:hi, :] = jnp.concatenate(outs, axis=1)

        @pl.when(pl.program_id(0) == _HP - 1)
        def _():
            local, remote = _gather_first_copies(x_refs, out_refs, *sems[:3])
            for cp in remote + local + _pass_on_copies(buf_refs, *sems[3:]):
                cp.wait()

    fixed = lambda p: (0, 0)
    anyspec = pl.BlockSpec(memory_space=pl.ANY)
    res = pl.pallas_call(
        body, name="attn_fwd", grid=(_HP,),
        in_specs=_qkv_specs() + [pl.BlockSpec((1, 2, T), lambda p: (p, 0, 0)), pl.BlockSpec((1, DH), fixed),
                                 pl.BlockSpec((1, DH), fixed)] + [anyspec] * (n + m),
        out_specs=[pl.BlockSpec((T, LANES), lambda p: (0, p)), pl.BlockSpec((1, T, 2), lambda p: (p, 0, 0))]
        + [anyspec] * (n + m),
        out_shape=[_sds((T, AW)), _sds((_HP, T, 2))] + [_sds((NDEV, *s.shape), s.dtype) for s in shards]
        + [_sds(b.shape, b.dtype) for b in bufs],
        input_output_aliases={6 + n + i: 2 + n + i for i in range(m)},
        scratch_shapes=_gather_first_scratch(n) + [pltpu.SemaphoreType.DMA((3 * m,)), pltpu.SemaphoreType.DMA((3 * m,))],
        compiler_params=_cp(("arbitrary",)),
    )(z, z, z, ck, g_q, g_k, *shards, *bufs)
    return res[0], res[1], res[2:2 + n], res[2 + n:]


def _attn_bwd(z, ck, g_q, g_k, lse, datt, parts):
    scale = DH ** -0.5
    n = len(parts)

    def body(q_ref, k_ref, v_ref, ck_ref, gq_ref, gk_ref, lse_ref, do_ref, *rest):
        p_refs, rest = rest[:n], rest[n:]
        dq_ref, dk_ref, dv_ref, dck_ref, dgq_ref, dgk_ref = rest[:6]
        fc_refs, (dkn_acc, dv_acc, dc_acc), sems = rest[6:6 + n], rest[6 + n:9 + n], rest[9 + n:]

        @pl.when(pl.program_id(0) == 0)
        def _():
            for cp in _chip_swap_copies(p_refs, fc_refs, *sems):
                cp.start()

        nq, nk, rq, rk, qb, kb, vb, dob = [], [], [], [], [], [], [], []
        for hh in range(2):
            cols = slice(DH * hh, DH * (hh + 1))
            qv, kv = q_ref[:, cols], k_ref[:, cols]
            rq.append(_rms_r(qv))
            rk.append(_rms_r(kv))
            nq.append(qv * rq[hh])
            nk.append(kv * rk[hh])
            qb.append((nq[hh] * gq_ref[...] * scale).astype(BF16))
            kb.append((nk[hh] * gk_ref[...]).astype(BF16))
            vb.append(v_ref[:, cols].astype(BF16))
            dob.append(do_ref[:, cols].astype(BF16))
        dkn_acc[...] = jnp.zeros_like(dkn_acc)
        dv_acc[...] = jnp.zeros_like(dv_acc)
        dc_acc[...] = jnp.zeros_like(dc_acc)
        dgq = jnp.zeros((1, DH), F32)
        tri = _lower_triangle()
        for i in range(T // BQ):
            lo, hi = i * BQ, (i + 1) * BQ
            dqs = []
            for hh in range(2):
                lse = lse_ref[0, lo:hi, hh:hh + 1]
                pd = []
                for s, k0, k1 in _logit_parts(qb[hh], kb[hh], ck_ref[0, hh:hh + 1, :], lo, hi, tri):
                    pd.append((jnp.exp(s - lse), _nt(dob[hh][lo:hi], vb[hh][k0:k1]), k0, k1))
                mean = sum(jnp.sum(p * dp, axis=-1, keepdims=True) for p, dp, _, _ in pd)
                dqn = 0.0
                for p, dp, k0, k1 in pd:
                    ds = p * (dp - mean)
                    dsb = ds.astype(BF16)
                    dv_acc[hh, k0:k1, :] += _tn(p.astype(BF16), dob[hh][lo:hi])
                    dkn_acc[hh, k0:k1, :] += _tn(dsb, qb[hh][lo:hi])
                    dc_acc[hh:hh + 1, k0:k1] -= jnp.sum(ds, axis=0, keepdims=True)
                    dqn = dqn + _nn(dsb, kb[hh][k0:k1])
                dqx, dgp = _rms_bwd(nq[hh][lo:hi], rq[hh][lo:hi], gq_ref[...], dqn * scale)
                dqs.append(dqx)
                dgq = dgq + dgp
            dq_ref[lo:hi, :] = jnp.concatenate(dqs, axis=1).astype(BF16)
        dks, dgk = [], jnp.zeros((1, DH), F32)
        for hh in range(2):
            dkx, dgp = _rms_bwd(nk[hh], rk[hh], gk_ref[...], dkn_acc[hh])
            dks.append(dkx)
            dgk = dgk + dgp
        dk_ref[...] = jnp.concatenate(dks, axis=1).astype(BF16)
        dv_ref[...] = jnp.concatenate([dv_acc[0], dv_acc[1]], axis=1).astype(BF16)
        dck_ref[0] = dc_acc[...]

        @pl.when(pl.program_id(0) == 0)
        def _():
            dgq_ref[...] = jnp.zeros_like(dgq_ref)
            dgk_ref[...] = jnp.zeros_like(dgk_ref)
        dgq_ref[...] += dgq
        dgk_ref[...] += dgk

        @pl.when(pl.program_id(0) == _HP - 1)
        def _():
            for cp in _chip_swap_copies(p_refs, fc_refs, *sems):
                cp.wait()

    fixed = lambda p: (0, 0)
    pair = pl.BlockSpec((T, LANES), lambda p: (0, p))
    gsp = pl.BlockSpec((1, DH), fixed)
    ckspec = pl.BlockSpec((1, 2, T), lambda p: (p, 0, 0))
    anyspec = pl.BlockSpec(memory_space=pl.ANY)
    res = pl.pallas_call(
        body, name="attn_bwd", grid=(_HP,),
        in_specs=_qkv_specs() + [ckspec, gsp, gsp, pl.BlockSpec((1, T, 2), lambda p: (p, 0, 0)), pair] + [anyspec] * n,
        out_specs=[pair, pair, pair, ckspec, gsp, gsp] + [anyspec] * n,
        out_shape=[_sds((T, AW), BF16)] * 3 + [_sds((_HP, 2, T)), _sds((1, DH)), _sds((1, DH))]
        + [_sds((3, *p.shape[1:]), p.dtype) for p in parts],
        scratch_shapes=[pltpu.VMEM((2, T, DH), F32), pltpu.VMEM((2, T, DH), F32), pltpu.VMEM((2, T), F32),
                        pltpu.SemaphoreType.DMA((3 * n,)), pltpu.SemaphoreType.DMA((3 * n,))],
        compiler_params=_cp(("arbitrary",)),
    )(z, z, z, ck, g_q, g_k, lse, datt, *parts)
    return (*res[:6], res[6:])


def _disc(lr, li, ls, br_t, bi_t):
    step = jnp.exp(ls)
    er = jnp.exp(lr * step)
    ab_re = er * jnp.cos(li * step)
    ab_im = er * jnp.sin(li * step)
    num_re = ab_re - 1.0
    num_im = ab_im
    den = lr * lr + li * li
    f_re = (num_re * lr + num_im * li) / den
    f_im = (num_im * lr - num_re * li) / den
    bb_re = f_re[:, None, :] * br_t - f_im[:, None, :] * bi_t
    bb_im = f_re[:, None, :] * bi_t + f_im[:, None, :] * br_t
    return ab_re, ab_im, bb_re, bb_im


def _disc_fwd(lr, li, ls, br_t, bi_t):
    def body(lr_ref, li_ref, ls_ref, br_ref, bi_ref, ar_ref, ai_ref, bbr_ref, bbi_ref):
        ar, ai, bbr, bbi = _disc(lr_ref[...], li_ref[...], ls_ref[...], br_ref[...], bi_ref[...])
        ar_ref[...] = ar
        ai_ref[...] = ai
        bbr_ref[...] = bbr
        bbi_ref[...] = bbi

    return pl.pallas_call(
        body, name="ssm_disc_fwd",
        out_shape=[_sds((NG, NP)), _sds((NG, NP)), _sds((NG, GS, NP)), _sds((NG, GS, NP))],
        compiler_params=_cp())(lr, li, ls, br_t, bi_t)


def _disc_bwd(lr, li, ls, br_t, bi_t, dar, dai, dbbr, dbbi):
    def body(lr_ref, li_ref, ls_ref, br_ref, bi_ref, dar_ref, dai_ref, dbbr_ref, dbbi_ref,
             dlr_ref, dli_ref, dls_ref, dbr_ref, dbi_ref):
        _, vjp = jax.vjp(_disc, lr_ref[...], li_ref[...], ls_ref[...], br_ref[...], bi_ref[...])
        dlr, dli, dls, dbr, dbi = vjp((dar_ref[...], dai_ref[...], dbbr_ref[...], dbbi_ref[...]))
        dlr_ref[...] = dlr
        dli_ref[...] = dli
        dls_ref[...] = dls
        dbr_ref[...] = dbr
        dbi_ref[...] = dbi

    return pl.pallas_call(
        body, name="ssm_disc_bwd",
        out_shape=[_sds((NG, NP)), _sds((NG, NP)), _sds((NG, 1)), _sds((NG, GS, NP)), _sds((NG, GS, NP))],
        compiler_params=_cp())(lr, li, ls, br_t, bi_t, dar, dai, dbbr, dbbi)


def _cmul(ar, ai, br, bi):
    return ar * br - ai * bi, ar * bi + ai * br


def _scan(buf, a_re, a_im, reverse, other=None):
    ar = [jnp.broadcast_to(a_re[:, LANES * k:LANES * (k + 1)], (NSEG, LANES)) for k in range(4)]
    ai = [jnp.broadcast_to(a_im[:, LANES * k:LANES * (k + 1)], (NSEG, LANES)) for k in range(4)]

    def tile(ref, i, k):
        return ref[pl.ds(pl.multiple_of(i * NSEG, NSEG), NSEG), LANES * k:LANES * (k + 1)]

    def advance(i, xr, xi):
        nr = [ar[k] * xr[k] - ai[k] * xi[k] + tile(buf, i, k) for k in range(4)]
        ni = [ar[k] * xi[k] + ai[k] * xr[k] + tile(buf, i, 4 + k) for k in range(4)]
        return nr, ni

    def index(i):
        return (SEG - 1 - i) if reverse else i

    zeros = tuple(jnp.zeros((NSEG, LANES), F32) for _ in range(4))

    def sweep1(i, carry):
        nr, ni = advance(index(i), carry[0], carry[1])
        return tuple(nr), tuple(ni)

    er, ei = lax.fori_loop(0, SEG, sweep1, (zeros, zeros), unroll=4)

    row = lax.broadcasted_iota(jnp.int32, (NSEG, LANES), 0)
    sr, si = [], []
    for k in range(4):
        pr, pi = ar[k], ai[k]
        for _ in range(SEG.bit_length() - 1):
            pr, pi = _cmul(pr, pi, pr, pi)
        ir, ii = er[k], ei[k]
        for d in (1, 2, 4):
            shift = (NSEG - d) if reverse else d
            ok = (row < NSEG - d) if reverse else (row >= d)
            mr, mi = _cmul(pr, pi, pltpu.roll(ir, shift, 0), pltpu.roll(ii, shift, 0))
            ir = ir + jnp.where(ok, mr, 0.0)
            ii = ii + jnp.where(ok, mi, 0.0)
            pr, pi = _cmul(pr, pi, pr, pi)
        shift = (NSEG - 1) if reverse else 1
        ok = (row < NSEG - 1) if reverse else (row >= 1)
        sr.append(jnp.where(ok, pltpu.roll(ir, shift, 0), 0.0))
        si.append(jnp.where(ok, pltpu.roll(ii, shift, 0), 0.0))

    def sweep2(i, carry):
        xr, xi, dar, dai = carry
        idx = index(i)
        if other is not None:
            orr = [tile(other, idx, k) for k in range(4)]
            oi = [tile(other, idx, 4 + k) for k in range(4)]
            dar = tuple(dar[k] + xr[k] * orr[k] + xi[k] * oi[k] for k in range(4))
            dai = tuple(dai[k] + xi[k] * orr[k] - xr[k] * oi[k] for k in range(4))
        nr, ni = advance(idx, xr, xi)
        start = pl.multiple_of(idx * NSEG, NSEG)
        for k in range(4):
            buf[pl.ds(start, NSEG), LANES * k:LANES * (k + 1)] = nr[k]
            buf[pl.ds(start, NSEG), LANES * (4 + k):LANES * (5 + k)] = ni[k]
        return tuple(nr), tuple(ni), dar, dai

    _, _, dar, dai = lax.fori_loop(0, SEG, sweep2, (tuple(sr), tuple(si), zeros, zeros), unroll=4)
    if other is None:
        return None
    da_re = jnp.concatenate([jnp.sum(d, axis=0, keepdims=True) for d in dar], axis=1)
    da_im = jnp.concatenate([jnp.sum(d, axis=0, keepdims=True) for d in dai], axis=1)
    return da_re, da_im


_SSM_BLOCKS = 4


def _ssm_fwd(u, bb, ab, cc, dsk, shards):
    n = len(shards)

    def body(u_ref, bb_ref, a_ref, cc_ref, d_ref, *rest):
        x_refs, y_ref, xs_ref, out_refs, sems = rest[:n], rest[n], rest[n + 1], rest[n + 2:2 * n + 2], rest[2 * n + 2:]

        @pl.when(pl.program_id(0) == 0)
        def _():
            local, remote = _gather_first_copies(x_refs, out_refs, *sems)
            for cp in local + remote:
                cp.start()

        ub = u_ref[...]
        xb = xs_ref.at[0]
        xb[...] = _nn(ub.astype(BF16), bb_ref[0])
        _scan(xb, a_ref[0, 0:1, :], a_ref[0, 1:2, :], reverse=False)
        y_ref[...] = _nn(xb[...].astype(BF16), cc_ref[0]) + d_ref[0] * ub

        @pl.when(pl.program_id(0) == _SSM_BLOCKS - 1)
        def _():
            local, remote = _gather_first_copies(x_refs, out_refs, *sems)
            for cp in remote + local:
                cp.wait()

    blk = lambda j: (j, 0, 0)
    col = pl.BlockSpec((T, LANES), lambda j: (0, j))
    anyspec = pl.BlockSpec(memory_space=pl.ANY)
    res = pl.pallas_call(
        body, name="ssm_fwd", grid=(_SSM_BLOCKS,),
        in_specs=[col, pl.BlockSpec((1, LANES, 1024), blk), pl.BlockSpec((1, 2, 512), blk),
                  pl.BlockSpec((1, 1024, LANES), blk), pl.BlockSpec((1, 1, LANES), blk)] + [anyspec] * n,
        out_specs=[col, pl.BlockSpec((1, T, 1024), blk)] + [anyspec] * n,
        out_shape=[_sds((T, SW)), _sds((_SSM_BLOCKS, T, 1024))] + [_sds((NDEV, *s.shape), s.dtype) for s in shards],
        scratch_shapes=_gather_first_scratch(n),
        compiler_params=_cp(("arbitrary",)),
    )(u, bb, ab, cc, dsk, *shards)
    return res[0], res[1], res[2:]


def _ssm_bwd(u, dy, xs, bb, ab, cc, dsk, arrs):
    n = len(arrs)

    def body(u_ref, dy_ref, x_ref, bb_ref, a_ref, cc_ref, d_ref, *rest):
        a_refs, rest = rest[:n], rest[n:]
        du_ref, dbb_ref, da_ref, dcc_ref, dd_ref = rest[:5]
        fs_refs, gb, sems = rest[5:5 + n], rest[5 + n], rest[6 + n:]

        @pl.when(pl.program_id(0) == 0)
        def _():
            for cp in _sibling_swap_copies(a_refs, fs_refs, *sems):
                cp.start()

        ub, dyv = u_ref[...], dy_ref[...]
        ubb, dyb = ub.astype(BF16), dyv.astype(BF16)
        a_re, a_im = a_ref[0, 0:1, :], a_ref[0, 1:2, :]
        dcc_ref[0] = _tn(x_ref[0].astype(BF16), dyb)
        gb[...] = _nt(dyb, cc_ref[0])
        da_re, da_im = _scan(gb, a_re, -a_im, reverse=True, other=x_ref.at[0])
        da_ref[0] = jnp.concatenate([da_re, da_im], axis=0)
        gc = gb[...].astype(BF16)
        du_ref[...] = _nt(gc, bb_ref[0]) + d_ref[0] * dyv
        dbb_ref[0] = _tn(ubb, gc)
        dd_ref[0] = jnp.sum(dyv * ub, axis=0, keepdims=True)

        @pl.when(pl.program_id(0) == _SSM_BLOCKS - 1)
        def _():
            for cp in _sibling_swap_copies(a_refs, fs_refs, *sems):
                cp.wait()

    blk = lambda j: (j, 0, 0)
    col = pl.BlockSpec((T, LANES), lambda j: (0, j))
    anyspec = pl.BlockSpec(memory_space=pl.ANY)
    res = pl.pallas_call(
        body, name="ssm_bwd", grid=(_SSM_BLOCKS,),
        in_specs=[col, col, pl.BlockSpec((1, T, 1024), blk), pl.BlockSpec((1, LANES, 1024), blk),
                  pl.BlockSpec((1, 2, 512), blk), pl.BlockSpec((1, 1024, LANES), blk), pl.BlockSpec((1, 1, LANES), blk)]
        + [anyspec] * n,
        out_specs=[col, pl.BlockSpec((1, LANES, 1024), blk), pl.BlockSpec((1, 2, 512), blk),
                   pl.BlockSpec((1, 1024, LANES), blk), pl.BlockSpec((1, 1, LANES), blk)] + [anyspec] * n,
        out_shape=[_sds((T, SW)), _sds((4, LANES, 1024)), _sds((4, 2, 512)), _sds((4, 1024, LANES)), _sds((4, 1, LANES))]
        + [_sds((4, *a.shape[2:]), a.dtype) for a in arrs],
        scratch_shapes=[pltpu.VMEM((T, 1024), F32), pltpu.SemaphoreType.DMA((4 * n,)), pltpu.SemaphoreType.DMA((4 * n,))],
        compiler_params=_cp(("arbitrary",)),
    )(u, dy, xs, bb, ab, cc, dsk, *arrs)
    return (*res[:5], res[5:])


def _interleave(a):
    return a.reshape(NSEG, SEG, a.shape[1]).transpose(1, 0, 2).reshape(a.shape)


def _deinterleave(a):
    return a.reshape(SEG, NSEG, a.shape[1]).transpose(1, 0, 2).reshape(a.shape)


_TM_MIX = 512


def _mix_parts(att, y, wg, bg):
    y2 = jax.nn.gelu(y)
    y2b = y2.astype(BF16)
    sg = jax.nn.sigmoid(_nn(y2b, wg) + bg)
    ssm = y2 * sg
    ra, rs = _rms_r(att), _rms_r(ssm)
    return y2, y2b, sg, ssm, ra, rs


def _mix_fwd(att, y, x, wg, bg, ga, gs, wo, bufs):
    m = len(bufs)

    def body(att_ref, y_ref, x_ref, wg_ref, bg_ref, ga_ref, gs_ref, wo_ref, *rest):
        x1_ref, buf_refs, sems = rest[m], rest[m + 1:2 * m + 1], rest[2 * m + 1:]

        @pl.when(pl.program_id(0) == 0)
        def _():
            for cp in _pass_on_copies(buf_refs, *sems):
                cp.start()

        attv = att_ref[...]
        _, _, _, ssm, ra, rs = _mix_parts(attv, y_ref[...], wg_ref[...], bg_ref[...])
        ma = (attv * ra * ga_ref[...]).astype(BF16)
        ms = (ssm * rs * gs_ref[...]).astype(BF16)
        x1_ref[...] = x_ref[...] + _nn(ma, wo_ref[0:AW, :]) + _nn(ms, wo_ref[AW:D, :])

        @pl.when(pl.program_id(0) == T // _TM_MIX - 1)
        def _():
            for cp in _pass_on_copies(buf_refs, *sems):
                cp.wait()

    row = lambda i: (i, 0)
    fixed = lambda i: (0, 0)
    half = pl.BlockSpec((_TM_MIX, AW), row)
    vec = pl.BlockSpec((1, AW), fixed)
    anyspec = pl.BlockSpec(memory_space=pl.ANY)
    res = pl.pallas_call(
        body, name="mix_fwd", grid=(T // _TM_MIX,),
        in_specs=[half, half, pl.BlockSpec((_TM_MIX, D), row), pl.BlockSpec((SW, SW), fixed), vec, vec, vec,
                  pl.BlockSpec((D, D), fixed)] + [anyspec] * m,
        out_specs=[pl.BlockSpec((_TM_MIX, D), row)] + [anyspec] * m,
        out_shape=[_sds((T, D))] + [_sds(b.shape, b.dtype) for b in bufs],
        input_output_aliases={8 + i: 1 + i for i in range(m)},
        scratch_shapes=[pltpu.SemaphoreType.DMA((3 * m,)), pltpu.SemaphoreType.DMA((3 * m,))],
        compiler_params=_cp(("arbitrary",)),
    )(att, y, x, wg, bg, ga, gs, wo, *bufs)
    return res[0], res[1:]


def _mix_bwd(att, y, dx1, wg, bg, ga, gs, wo):
    last = T // _TM_MIX - 1

    def body(att_ref, y_ref, d_ref, wg_ref, bg_ref, ga_ref, gs_ref, wo_ref,
             datt_ref, dy_ref, dwg_ref, dbg_ref, dga_ref, dgs_ref, dwo_ref, dwg_acc, dwo_acc):
        attv, yv, db = att_ref[...], y_ref[...], d_ref[...].astype(BF16)
        y2, y2b, sg, ssm, ra, rs = _mix_parts(attv, yv, wg_ref[...], bg_ref[...])
        na, ns = attv * ra, ssm * rs
        ma = (na * ga_ref[...]).astype(BF16)
        ms = (ns * gs_ref[...]).astype(BF16)
        dma = _nt(db, wo_ref[0:AW, :])
        dms = _nt(db, wo_ref[AW:D, :])
        datt, dga = _rms_bwd(na, ra, ga_ref[...], dma)
        dssm, dgs = _rms_bwd(ns, rs, gs_ref[...], dms)
        dgl = dssm * y2 * sg * (1.0 - sg)
        dglb = dgl.astype(BF16)
        dy2 = dssm * sg + _nt(dglb, wg_ref[...])
        _, gelu_vjp = jax.vjp(jax.nn.gelu, yv)
        datt_ref[...] = datt
        dy_ref[...] = gelu_vjp(dy2)[0]

        @pl.when(pl.program_id(0) == 0)
        def _():
            dwg_acc[...] = jnp.zeros_like(dwg_acc)
            dwo_acc[...] = jnp.zeros_like(dwo_acc)
            dbg_ref[...] = jnp.zeros_like(dbg_ref)
            dga_ref[...] = jnp.zeros_like(dga_ref)
            dgs_ref[...] = jnp.zeros_like(dgs_ref)
        dwg_acc[...] += _tn(y2b, dglb)
        dbg_ref[...] += jnp.sum(dgl, axis=0, keepdims=True)
        dga_ref[...] += dga
        dgs_ref[...] += dgs
        dwo_acc[0:AW, :] += _tn(ma, db)
        dwo_acc[AW:D, :] += _tn(ms, db)

        @pl.when(pl.program_id(0) == last)
        def _():
            dwg_ref[...] = dwg_acc[...].astype(BF16)
            dwo_ref[...] = dwo_acc[...].astype(BF16)

    row = lambda i: (i, 0)
    fixed = lambda i: (0, 0)
    half = pl.BlockSpec((_TM_MIX, AW), row)
    vec = pl.BlockSpec((1, AW), fixed)
    sq = pl.BlockSpec((SW, SW), fixed)
    big = pl.BlockSpec((D, D), fixed)
    return pl.pallas_call(
        body, name="mix_bwd", grid=(T // _TM_MIX,),
        in_specs=[half, half, pl.BlockSpec((_TM_MIX, D), row), sq, vec, vec, vec, big],
        out_specs=[half, half, sq, vec, vec, vec, big],
        out_shape=[_sds((T, AW)), _sds((T, SW)), _sds((SW, SW), BF16), _sds((1, SW)), _sds((1, AW)), _sds((1, SW)),
                   _sds((D, D), BF16)],
        scratch_shapes=[pltpu.VMEM((SW, SW), F32), pltpu.VMEM((D, D), F32)],
        compiler_params=_cp(("arbitrary",)),
    )(att, y, dx1, wg, bg, ga, gs, wo)


_NCB = -(-CS // LANES)


def _ffn_up(x1, g, wu4):
    tm = 1024

    def body(x_ref, g_ref, w_ref, h_ref, up_ref):
        @pl.when(pl.program_id(1) == 0)
        def _():
            xv = x_ref[...]
            h_ref[...] = (xv * _rms_r(xv) * g_ref[...]).astype(BF16)
        up_ref[0, 0] = _nt(h_ref[...], w_ref[0, 0])

    return pl.pallas_call(
        body, name="ffn_up", grid=(T // tm, NDEV),
        in_specs=[pl.BlockSpec((tm, D), lambda i, e: (i, 0)), pl.BlockSpec((1, D), lambda i, e: (0, 0)),
                  pl.BlockSpec((1, 1, CS, D), lambda i, e: (e // 4, e % 4, 0, 0))],
        out_specs=[pl.BlockSpec((tm, D), lambda i, e: (i, 0)),
                   pl.BlockSpec((1, 1, tm, CS), lambda i, e: (e // 4, e % 4, i, 0))],
        out_shape=[_sds((T, D), BF16), _sds((2, 4, T, CS))],
        compiler_params=_cp(("parallel", "arbitrary")),
    )(x1, g, wu4)


def _shift_down(h, k):
    row = lax.broadcasted_iota(jnp.int32, h.shape, 0)
    return jnp.where(row >= k, pltpu.roll(h, k, 0), 0.0)


def _shift_up(h, k):
    row = lax.broadcasted_iota(jnp.int32, h.shape, 0)
    return jnp.where(row < T - k, pltpu.roll(h, T - k, 0), 0.0)


def _conv(h, w, b):
    return w[0:1, :] * _shift_down(h, 2) + w[1:2, :] * _shift_down(h, 1) + w[2:3, :] * h + b


def _chan(gv, rows):
    return pl.BlockSpec((1, 1, rows, LANES), lambda d, j: (gv, d, 0, j))


def _ffn_act(up4, cw4, cb4):
    def body(ug_ref, uv_ref, wg_ref, wv_ref, bg_ref, bv_ref, act_ref):
        g = _conv(ug_ref[0, 0], wg_ref[0, 0], bg_ref[0, 0])
        v = _conv(uv_ref[0, 0], wv_ref[0, 0], bv_ref[0, 0])
        act_ref[0] = (g * jax.nn.sigmoid(g) * v).astype(BF16)

    return pl.pallas_call(
        body, name="ffn_act", grid=(4, _NCB),
        in_specs=[_chan(0, T), _chan(1, T), _chan(0, 3), _chan(1, 3), _chan(0, 1), _chan(1, 1)],
        out_specs=pl.BlockSpec((1, T, LANES), lambda d, j: (d, 0, j)), out_shape=_sds((4, T, CS), BF16),
        compiler_params=_cp(("parallel", "parallel")),
    )(up4, up4, cw4, cw4, cb4, cb4)


def _ffn_act_bwd(up4, dact, cw4, cb4):
    def conv_bwd(h, w, d):
        dup = w[2:3, :] * d + w[1:2, :] * _shift_up(d, 1) + w[0:1, :] * _shift_up(d, 2)
        dw = jnp.concatenate([jnp.sum(d * _shift_down(h, 2), axis=0, keepdims=True),
                              jnp.sum(d * _shift_down(h, 1), axis=0, keepdims=True),
                              jnp.sum(d * h, axis=0, keepdims=True)], axis=0)
        return dup, dw, jnp.sum(d, axis=0, keepdims=True)

    def body(ug_ref, uv_ref, da_ref, wg_ref, wv_ref, bg_ref, bv_ref, dup_ref, dw_ref, db_ref):
        ug, uv, da = ug_ref[0, 0], uv_ref[0, 0], da_ref[0]
        g = _conv(ug, wg_ref[0, 0], bg_ref[0, 0])
        v = _conv(uv, wv_ref[0, 0], bv_ref[0, 0])
        sg = jax.nn.sigmoid(g)
        dg = da * v * (sg * (1.0 + g * (1.0 - sg)))
        dv = da * (g * sg)
        dug, dwg, dbg = conv_bwd(ug, wg_ref[0, 0], dg)
        duv, dwv, dbv = conv_bwd(uv, wv_ref[0, 0], dv)
        dup_ref[0, 0] = dug.astype(BF16)
        dup_ref[1, 0] = duv.astype(BF16)
        dw_ref[0, 0] = dwg
        dw_ref[1, 0] = dwv
        db_ref[0, 0] = dbg
        db_ref[1, 0] = dbv

    both = lambda rows: pl.BlockSpec((2, 1, rows, LANES), lambda d, j: (0, d, 0, j))
    return pl.pallas_call(
        body, name="ffn_act_bwd", grid=(4, _NCB),
        in_specs=[_chan(0, T), _chan(1, T), pl.BlockSpec((1, T, LANES), lambda d, j: (d, 0, j)),
                  _chan(0, 3), _chan(1, 3), _chan(0, 1), _chan(1, 1)],
        out_specs=[both(T), both(3), both(1)],
        out_shape=[_sds((2, 4, T, CS), BF16), _sds((2, 4, 3, CS)), _sds((2, 4, 1, CS))],
        compiler_params=_cp(("parallel", "parallel")),
    )(up4, up4, dact, cw4, cw4, cb4, cb4)


def _ffn_down_loss(act, wd4, x1, target):
    tm = 512

    def body(a_ref, w_ref, x1_ref, t_ref, loss_ref, dx_ref, dxb_ref):
        x2 = x1_ref[...]
        for d in range(4):
            x2 = x2 + _nn(a_ref[d], w_ref[d])
        err = x2 - t_ref[...]
        dx = err * (1.0 / D)
        dx_ref[...] = dx
        dxb_ref[...] = dx.astype(BF16)

        @pl.when(pl.program_id(0) == 0)
        def _():
            loss_ref[...] = jnp.zeros_like(loss_ref)
        loss_ref[...] += 0.5 * jnp.sum(jnp.mean(err * err, axis=-1, keepdims=True), axis=0, keepdims=True)

    row = lambda i: (i, 0)
    fixed = lambda i: (0, 0)
    return pl.pallas_call(
        body, name="ffn_down_loss", grid=(T // tm,),
        in_specs=[pl.BlockSpec((4, tm, CS), lambda i: (0, i, 0)), pl.BlockSpec((4, CS, D), lambda i: (0, 0, 0)),
                  pl.BlockSpec((tm, D), row), pl.BlockSpec((tm, D), row)],
        out_specs=[pl.BlockSpec((1, 1), fixed), pl.BlockSpec((tm, D), row), pl.BlockSpec((tm, D), row)],
        out_shape=[_sds((1, 1)), _sds((T, D)), _sds((T, D), BF16)],
        compiler_params=_cp(("arbitrary",)),
    )(act, wd4, x1, target)


def _ffn_dact(dx2b, wd4):
    tm = 1024

    def body(d_ref, w_ref, o_ref):
        o_ref[0] = _nt(d_ref[...], w_ref[0])

    return pl.pallas_call(
        body, name="ffn_dact", grid=(4, T // tm),
        in_specs=[pl.BlockSpec((tm, D), lambda d, i: (i, 0)), pl.BlockSpec((1, CS, D), lambda d, i: (d, 0, 0))],
        out_specs=pl.BlockSpec((1, tm, CS), lambda d, i: (d, i, 0)), out_shape=_sds((4, T, CS)),
        compiler_params=_cp(("parallel", "parallel")),
    )(dx2b, wd4)


def _ffn_dwd(act, dx2b):
    def body(a_ref, d_ref, o_ref):
        o_ref[0] = _tn(a_ref[0], d_ref[...]).astype(BF16)

    return pl.pallas_call(
        body, name="ffn_dwd", grid=(4,),
        in_specs=[pl.BlockSpec((1, T, CS), lambda d: (d, 0, 0)), pl.BlockSpec((T, D), lambda d: (0, 0))],
        out_specs=pl.BlockSpec((1, CS, D), lambda d: (d, 0, 0)), out_shape=_sds((4, CS, D), BF16),
        compiler_params=_cp(("parallel",)),
    )(act, dx2b)


def _ffn_dwu(h2, dup4):
    tn = 1024

    def body(h_ref, d_ref, o_ref):
        o_ref[0, 0] = _tn(d_ref[0, 0], h_ref[...]).astype(BF16)

    return pl.pallas_call(
        body, name="ffn_dwu", grid=(NDEV, D // tn),
        in_specs=[pl.BlockSpec((T, tn), lambda e, i: (0, i)),
                  pl.BlockSpec((1, 1, T, CS), lambda e, i: (e // 4, e % 4, 0, 0))],
        out_specs=pl.BlockSpec((1, 1, CS, tn), lambda e, i: (e // 4, e % 4, 0, i)),
        out_shape=_sds((2, 4, CS, D), BF16),
        compiler_params=_cp(("parallel", "parallel")),
    )(h2, dup4)


def _ffn_up_bwd(dup4, wu4, x1, g, dres):
    tm = 512

    def body(d_ref, w_ref, x_ref, g_ref, dres_ref, dx_ref, dg_ref):
        dh = jnp.zeros((tm, D), F32)
        for gv in range(2):
            for d in range(4):
                dh = dh + _nn(d_ref[gv, d], w_ref[gv, d])
        xv = x_ref[...]
        r = _rms_r(xv)
        dxr, dgp = _rms_bwd(xv * r, r, g_ref[...], dh)
        dx_ref[...] = dres_ref[...] + dxr

        @pl.when(pl.program_id(0) == 0)
        def _():
            dg_ref[...] = jnp.zeros_like(dg_ref)
        dg_ref[...] += dgp

    row = lambda i: (i, 0)
    fixed = lambda i: (0, 0)
    return pl.pallas_call(
        body, name="ffn_up_bwd", grid=(T // tm,),
        in_specs=[pl.BlockSpec((2, 4, tm, CS), lambda i: (0, 0, i, 0)), pl.BlockSpec((2, 4, CS, D), lambda i: (0, 0, 0, 0), pipeline_mode=pl.Buffered(1)),
                  pl.BlockSpec((tm, D), row), pl.BlockSpec((1, D), fixed), pl.BlockSpec((tm, D), row)],
        out_specs=[pl.BlockSpec((tm, D), row), pl.BlockSpec((1, D), fixed)],
        out_shape=[_sds((T, D)), _sds((1, D))],
        compiler_params=_cp(("arbitrary",)),
    )(dup4, wu4, x1, g, dres)


def _adamw_math(w, g, m, v):
    m = ADAM_B1 * m + (1.0 - ADAM_B1) * g
    v = ADAM_B2 * v + (1.0 - ADAM_B2) * jnp.square(g)
    m_hat = m / (1.0 - ADAM_B1 ** ADAM_STEP)
    v_hat = v / (1.0 - ADAM_B2 ** ADAM_STEP)
    delta = -ADAM_LR * (m_hat / (jnp.sqrt(v_hat) + ADAM_EPS) + ADAM_WD * w)
    return delta, m, v


def _where_am_i():
    x, y, c = _place()
    return jnp.stack([c, 2 * x + y]).astype(jnp.int32)


def _pair_sum(a4, recv, name):
    _, _, rows, cols = a4.shape
    tr = _row_tile(rows, cols, 3 << 20)

    def body(sel_ref, own_ref, recv_ref, out_ref):
        out_ref[...] = (own_ref[0].astype(F32) + recv_ref[...].astype(F32)).astype(out_ref.dtype)

    grid_spec = pltpu.PrefetchScalarGridSpec(
        num_scalar_prefetch=1, grid=(4, rows // tr),
        in_specs=[pl.BlockSpec((1, 1, tr, cols), lambda k, i, s: (k, s[0], i, 0)),
                  pl.BlockSpec((1, tr, cols), lambda k, i, s: (k, i, 0))],
        out_specs=pl.BlockSpec((1, tr, cols), lambda k, i, s: (k, i, 0)),
    )
    return pl.pallas_call(
        body, name=name, grid_spec=grid_spec, out_shape=_sds((4, rows, cols), a4.dtype),
        compiler_params=_cp(("arbitrary", "arbitrary")),
    )(_where_am_i(), a4, recv)


def _adamw_rs(a4, recv, from_chips, w, m, v, name):
    rows, cols = w.shape
    tr = _row_tile(rows, cols, 3 << 19)

    def body(sel_ref, own_ref, recv_ref, fc_ref, w_ref, m_ref, v_ref, g_ref, d_ref, nm_ref, nv_ref):
        g = own_ref[0, 0].astype(F32) + recv_ref[0].astype(F32)
        for j in range(3):
            g = g + fc_ref[j].astype(F32)
        d, nm, nv = _adamw_math(w_ref[...], g, m_ref[...], v_ref[...])
        g_ref[...] = g
        d_ref[...] = d
        nm_ref[...] = nm
        nv_ref[...] = nv

    flat = pl.BlockSpec((tr, cols), lambda i, s: (i, 0))
    grid_spec = pltpu.PrefetchScalarGridSpec(
        num_scalar_prefetch=1, grid=(rows // tr,),
        in_specs=[pl.BlockSpec((1, 1, tr, cols), lambda i, s: (s[1], s[0], i, 0)),
                  pl.BlockSpec((1, tr, cols), lambda i, s: (s[1], i, 0)),
                  pl.BlockSpec((3, tr, cols), lambda i, s: (0, i, 0)), flat, flat, flat],
        out_specs=[flat] * 4,
    )
    return pl.pallas_call(
        body, name=name, grid_spec=grid_spec, out_shape=[_sds((rows, cols))] * 4,
        compiler_params=_cp(("arbitrary",)),
    )(_where_am_i(), a4, recv, from_chips, w, m, v)


_SMALL_F32 = (("g_mix", (8, 128)), ("b_f", (1, 8)), ("g_q", (1, 64)), ("g_k", (1, 64)), ("lambda_re", (32, 64)),
              ("lambda_im", (32, 64)), ("log_step", (1, 32)), ("d_skip", (32, 16)), ("b_glu", (4, 128)),
              ("g_attn_out", (4, 128)), ("g_ssm_out", (4, 128)), ("g_ffn", (8, 128)), ("conv_b", (44, 128)))
_SMALL_PAIRS = (("b_re", "b_im"), ("c_re", "c_im"))
_PAIR_SHAPE = (NG * GS, NP)


def _ceil8(r):
    return -(-r // 8) * 8


def _small_names():
    return [n for n, _ in _SMALL_F32] + [n for pair in _SMALL_PAIRS for n in pair]


def _pack_small_grads(g, loss_dev):
    parts = [jnp.pad(g[n].reshape(r, c), ((0, _ceil8(r) - r), (0, LANES - c))) for n, (r, c) in _SMALL_F32]
    parts.append(jnp.pad(loss_dev, ((0, 7), (0, LANES - 1))))
    pairs = [jnp.concatenate([g[a].reshape(_PAIR_SHAPE), g[b].reshape(_PAIR_SHAPE)], axis=1) for a, b in _SMALL_PAIRS]
    return jnp.concatenate(parts, axis=0), jnp.concatenate(pairs, axis=0).astype(BF16)


def _adamw_small(gall, gall_b, wmv):
    names = _small_names()
    shapes = [s for _, s in _SMALL_F32] + [_PAIR_SHAPE] * (2 * len(_SMALL_PAIRS))
    npar = len(names)

    def body(ga_ref, gb_ref, *rest):
        ins, loss_ref, outs = rest[:3 * npar], rest[3 * npar], rest[3 * npar + 1:]
        ga, gb = ga_ref[0], gb_ref[0].astype(F32)
        for dev in range(1, NDEV):
            ga = ga + ga_ref[dev]
            gb = gb + gb_ref[dev].astype(F32)
        grads, off = [], 0
        for _, (r, c) in _SMALL_F32:
            grads.append(ga[off:off + r, 0:c])
            off += _ceil8(r)
        loss_ref[...] = ga[off:off + 8]
        rows, cols = _PAIR_SHAPE
        for j in range(len(_SMALL_PAIRS)):
            for h in range(2):
                grads.append(gb[rows * j:rows * (j + 1), cols * h:cols * (h + 1)])
        for i, g in enumerate(grads):
            d, nm, nv = _adamw_math(ins[3 * i][...], g, ins[3 * i + 1][...], ins[3 * i + 2][...])
            outs[4 * i][...] = g
            outs[4 * i + 1][...] = d
            outs[4 * i + 2][...] = nm
            outs[4 * i + 3][...] = nv

    flat = [a for n in names for a in wmv[n]]
    out_shape = [_sds((8, LANES))] + [_sds(s) for s in shapes for _ in range(4)]
    res = pl.pallas_call(body, name="adamw_small", out_shape=out_shape, compiler_params=_cp())(gall, gall_b, *flat)
    return res[0], {n: res[1 + 4 * i:5 + 4 * i] for i, n in enumerate(names)}


def _block_diag(m):
    eye = jnp.eye(8, dtype=m.dtype)
    r, c = m.shape[2], m.shape[3]
    return (m[:, :, :, None, :] * eye[None, :, None, :, None]).reshape(4, 8 * r, 8 * c)


def _diag_blocks(dense, r, c):
    eye = jnp.eye(8, dtype=dense.dtype)
    return jnp.sum(dense.reshape(4, 8, r, 8, c) * eye[None, :, None, :, None], axis=3)


def kernel(x, g_mix, w_in, b_f, g_q, g_k, lambda_re, lambda_im, log_step, b_re, b_im, c_re, c_im, d_skip, w_glu, b_glu, g_attn_out, g_ssm_out, w_out, g_ffn, w_up, conv_w, conv_b, w_down, loss_target, m_g_mix, m_w_in, m_b_f, m_g_q, m_g_k, m_lambda_re, m_lambda_im, m_log_step, m_b_re, m_b_im, m_c_re, m_c_im, m_d_skip, m_w_glu, m_b_glu, m_g_attn_out, m_g_ssm_out, m_w_out, m_g_ffn, m_w_up, m_conv_w, m_conv_b, m_w_down, v_g_mix, v_w_in, v_b_f, v_g_q, v_g_k, v_lambda_re, v_lambda_im, v_log_step, v_b_re, v_b_im, v_c_re, v_c_im, v_d_skip, v_w_glu, v_b_glu, v_g_attn_out, v_g_ssm_out, v_w_out, v_g_ffn, v_w_up, v_conv_w, v_conv_b, v_w_down):
    weights = dict(g_mix=g_mix, w_in=w_in, b_f=b_f, g_q=g_q, g_k=g_k, lambda_re=lambda_re, lambda_im=lambda_im,
                   log_step=log_step, b_re=b_re, b_im=b_im, c_re=c_re, c_im=c_im, d_skip=d_skip, w_glu=w_glu,
                   b_glu=b_glu, g_attn_out=g_attn_out, g_ssm_out=g_ssm_out, w_out=w_out, g_ffn=g_ffn, w_up=w_up,
                   conv_w=conv_w, conv_b=conv_b, w_down=w_down)
    mom_m = dict(g_mix=m_g_mix, w_in=m_w_in, b_f=m_b_f, g_q=m_g_q, g_k=m_g_k, lambda_re=m_lambda_re,
                 lambda_im=m_lambda_im, log_step=m_log_step, b_re=m_b_re, b_im=m_b_im, c_re=m_c_re, c_im=m_c_im,
                 d_skip=m_d_skip, w_glu=m_w_glu, b_glu=m_b_glu, g_attn_out=m_g_attn_out, g_ssm_out=m_g_ssm_out,
                 w_out=m_w_out, g_ffn=m_g_ffn, w_up=m_w_up, conv_w=m_conv_w, conv_b=m_conv_b, w_down=m_w_down)
    mom_v = dict(g_mix=v_g_mix, w_in=v_w_in, b_f=v_b_f, g_q=v_g_q, g_k=v_g_k, lambda_re=v_lambda_re,
                 lambda_im=v_lambda_im, log_step=v_log_step, b_re=v_b_re, b_im=v_b_im, c_re=v_c_re, c_im=v_c_im,
                 d_skip=v_d_skip, w_glu=v_w_glu, b_glu=v_b_glu, g_attn_out=v_g_attn_out, g_ssm_out=v_g_ssm_out,
                 w_out=v_w_out, g_ffn=v_g_ffn, w_up=v_w_up, conv_w=v_conv_w, conv_b=v_conv_b, w_down=v_w_down)
    names = list(weights)
    big = ("w_in", "w_glu", "w_out", "w_up", "w_down", "conv_w")

    xs = x[0]
    target = loss_target[0]
    row = lambda a: a.reshape(1, -1)

    (win_g,) = _all_gather([w_in.T.astype(BF16)], "gather_w_in")
    w_in_t = win_g.reshape(8 * 257, D)
    wcat = jnp.concatenate([w_in_t[0:1536], w_in_t[1544:2056], jnp.pad(w_in_t[1536:1544], ((0, LANES - NH), (0, 0)))],
                           axis=0)
    cb4 = conv_b.reshape(2, 4, 1, CS)

    h1, z, f_t, (wg_g, wo_g) = _in_proj(xs, row(g_mix), wcat, [w_glu.astype(BF16), w_out.astype(BF16)])
    bf_col = b_f.reshape(NH, 1)
    ck = _gates_fwd(f_t, bf_col).reshape(_HP, 2, T)
    att, lse, (wu_g, cw_g), (wg_g, wo_g) = _attn_fwd(z, ck, row(g_q), row(g_k), [w_up.T.astype(BF16), conv_w],
                                                       [wg_g, wo_g])

    ls_col = log_step.reshape(NG, 1)
    br_t, bi_t = b_re.transpose(0, 2, 1), b_im.transpose(0, 2, 1)
    ab_re, ab_im, bb_re, bb_im = _disc_fwd(lambda_re, lambda_im, ls_col, br_t, bi_t)
    bb = jnp.concatenate([_block_diag(bb_re.reshape(4, 8, GS, NP)), _block_diag(bb_im.reshape(4, 8, GS, NP))],
                         axis=2).astype(BF16)
    ab = jnp.stack([ab_re.reshape(4, 512), ab_im.reshape(4, 512)], axis=1)
    cdiag = lambda cm: _block_diag(cm.reshape(4, 8, GS, NP).transpose(0, 1, 3, 2))
    cc = jnp.concatenate([cdiag(c_re), -cdiag(c_im)], axis=1).astype(BF16)
    dsk = d_skip.reshape(4, 1, LANES)
    u_il = _interleave(z[:, 1536:2048])
    y_il, xs_il, (wd_g,) = _ssm_fwd(u_il, bb, ab, cc, dsk, [w_down.astype(BF16)])
    y = _deinterleave(y_il)
    wg_f = wg_g.reshape(SW, SW)
    wo_f = wo_g.reshape(D, D)
    x1, (wu_g, cw_g, wd_g) = _mix_fwd(att, y, xs, wg_f, row(b_glu), row(g_attn_out), row(g_ssm_out), wo_f,
                                      [wu_g, cw_g, wd_g])
    wu4 = wu_g.reshape(2, 4, CS, D)
    wd4 = wd_g.reshape(4, CS, D)
    cw4 = cw_g.reshape(2, 4, 3, CS)
    h2, up4 = _ffn_up(x1, row(g_ffn), wu4)
    act = _ffn_act(up4, cw4, cb4)
    loss_dev, dx2, dx2b = _ffn_down_loss(act, wd4, x1, target)

    dact = _ffn_dact(dx2b, wd4)
    d_wd = _ffn_dwd(act, dx2b)
    dup4, dcw4, dcb4 = _ffn_act_bwd(up4, dact, cw4, cb4)
    d_wu = _ffn_dwu(h2, dup4)
    dx1, dg_ffn = _ffn_up_bwd(dup4, wu4, x1, row(g_ffn), dx2)
    datt, dy, d_wg, d_bg, d_ga, d_gs, d_wo = _mix_bwd(att, y, dx1, wg_f, row(b_glu), row(g_attn_out),
                                                       row(g_ssm_out), wo_f)
    early = ("w_up", "w_down", "conv_w", "w_out", "w_glu")
    by_dest = {"w_up": d_wu.reshape(4, 2, CS, D), "w_down": d_wd.reshape(4, 2, DFF // NDEV, D),
               "conv_w": dcw4.reshape(4, 2, 3, CS), "w_out": d_wo.reshape(4, 2, D // NDEV, D),
               "w_glu": d_wg.reshape(4, 2, SW // NDEV, SW)}
    du_il, dbb, dab, dcc, ddsk, fs = _ssm_bwd(u_il, _interleave(dy), xs_il, bb, ab, cc, dsk, [by_dest[n] for n in early])
    from_sibling = dict(zip(early, fs))
    chip_sums = {n: _pair_sum(by_dest[n], from_sibling[n], "rs_pair_sum_" + n) for n in early}
    du = _deinterleave(du_il).astype(BF16)
    dbb_re = _diag_blocks(dbb[:, :, 0:512], GS, NP).reshape(NG, GS, NP)
    dbb_im = _diag_blocks(dbb[:, :, 512:1024], GS, NP).reshape(NG, GS, NP)
    dlr, dli, dls, dbr_t, dbi_t = _disc_bwd(lambda_re, lambda_im, ls_col, br_t, bi_t,
                                            dab[:, 0].reshape(NG, NP), dab[:, 1].reshape(NG, NP), dbb_re, dbb_im)
    d_cre = _diag_blocks(dcc[:, 0:512], NP, GS).transpose(0, 1, 3, 2).reshape(NG, GS, NP)
    d_cim = -_diag_blocks(dcc[:, 512:1024], NP, GS).transpose(0, 1, 3, 2).reshape(NG, GS, NP)

    dq, dk, dv, dck, d_gq, d_gk, fc = _attn_bwd(z, ck, row(g_q), row(g_k), lse, datt, [chip_sums[n] for n in early])
    from_chips = dict(zip(early, fc))
    df_t, d_bf = _gates_bwd(dck.reshape(NH, T), f_t, bf_col)
    df = jnp.concatenate([df_t.T, jnp.zeros((T, LANES - NH), F32)], axis=1).astype(BF16)
    d_wcat = _in_proj_dw(h1, dq, dk, dv, du, df)
    d_win = jnp.concatenate([d_wcat[0:1536], d_wcat[2048:2048 + NH], d_wcat[1536:2048]], axis=0)
    by_dest["w_in"] = d_win.astype(BF16).reshape(4, 2, 257, D)
    (from_sibling["w_in"],) = _swap_sibling([by_dest["w_in"]], "rs_pair_swap_w_in")
    chip_sums["w_in"] = _pair_sum(by_dest["w_in"], from_sibling["w_in"], "rs_pair_sum_w_in")
    grad_x, dg_mix, (from_chips["w_in"],) = _in_proj_bwd(dq, dk, dv, du, df, wcat, xs, row(g_mix), dx1, [chip_sums["w_in"]])

    grads, deltas, new_m, new_v = {}, {}, {}, {}
    for n in early + ("w_in",):
        flip = (lambda a: a.T) if n in ("w_up", "w_in") else (lambda a: a)
        outs = _adamw_rs(by_dest[n], from_sibling[n], from_chips[n], flip(weights[n]), flip(mom_m[n]), flip(mom_v[n]),
                         "adamw_" + n)
        grads[n], deltas[n], new_m[n], new_v[n] = (flip(o) for o in outs)

    small = dict(g_mix=dg_mix, b_f=d_bf, g_q=d_gq, g_k=d_gk, lambda_re=dlr, lambda_im=dli, log_step=dls,
                 b_re=dbr_t, b_im=dbi_t, c_re=d_cre, c_im=d_cim, d_skip=ddsk, b_glu=d_bg, g_attn_out=d_ga,
                 g_ssm_out=d_gs, g_ffn=dg_ffn, conv_b=dcb4)
    small_all, small_all_b = _all_gather(list(_pack_small_grads(small, loss_dev)), "gather_small_grads")
    views = dict(_SMALL_F32)
    swapped = ("b_re", "b_im")

    def view(n, a):
        return a.transpose(0, 2, 1).reshape(_PAIR_SHAPE) if n in swapped else a.reshape(views.get(n, _PAIR_SHAPE))

    loss_p, small_out = _adamw_small(small_all, small_all_b,
                                     {n: (view(n, weights[n]), view(n, mom_m[n]), view(n, mom_v[n])) for n in small})
    loss = loss_p[0, 0]
    for n in small:
        back = (lambda a: a.reshape(NG, GS, NP).transpose(0, 2, 1)) if n in swapped else (lambda a: a.reshape(weights[n].shape))
        grads[n], deltas[n], new_m[n], new_v[n] = (back(o) for o in small_out[n])

    return (loss, grad_x[None], *[grads[n] for n in names], *[deltas[n] for n in names],
            *[new_m[n] for n in names], *[new_v[n] for n in names])
```

```python
import jax
import jax.numpy as jnp
from jax import lax
from jax.experimental import pallas as pl
from jax.experimental.pallas import tpu as pltpu

F32, BF16 = jnp.float32, jnp.bfloat16
T, D = 2048, 1024
NH, DH = 8, 64
AW, SW = 512, 512
NG, GS, NP = 32, 16, 64
DFF = 2816
NDEV = 8
CS = 2 * DFF // NDEV
EPS = 1e-6
NEG = -1e30
ZC = 4 * 512 + 128
SEG = 256
NSEG = T // SEG
BQ = 256
VMEM_LIMIT = 56 * 1024 * 1024
LANES = 128
MESH = pl.DeviceIdType.MESH
HI = lax.Precision.HIGHEST

ADAM_LR, ADAM_B1, ADAM_B2, ADAM_EPS, ADAM_WD, ADAM_STEP = 0.001, 0.9, 0.999, 1e-08, 0.01, 10


def _cp(dims=None):
    if dims is None:
        return pltpu.CompilerParams(vmem_limit_bytes=VMEM_LIMIT)
    return pltpu.CompilerParams(dimension_semantics=dims, vmem_limit_bytes=VMEM_LIMIT)


def _sds(shape, dtype=F32):
    return jax.ShapeDtypeStruct(shape, dtype)


def _nt(a, b):
    return lax.dot_general(a, b, (((1,), (1,)), ((), ())), preferred_element_type=F32)


def _tn(a, b):
    return lax.dot_general(a, b, (((0,), (0,)), ((), ())), preferred_element_type=F32)


def _nn(a, b):
    return jnp.dot(a, b, preferred_element_type=F32)


def _rms_r(x):
    return lax.rsqrt(jnp.mean(x * x, axis=-1, keepdims=True) + EPS)


def _rms_bwd(n, r, g, dh):
    dn = dh * g
    dx = r * (dn - n * jnp.mean(dn * n, axis=-1, keepdims=True))
    return dx, jnp.sum(dh * n, axis=0, keepdims=True)


def _row_tile(rows, cols, limit):
    tr = rows
    while tr * cols * 4 > limit and tr % 32 == 0:
        tr //= 2
    return tr


def _place():
    return lax.axis_index("x"), lax.axis_index("y"), lax.axis_index("c")


def _all_gather(shards, name):
    n = len(shards)

    def body(*refs):
        x_refs, out_refs = refs[:n], refs[n:2 * n]
        send_sems, recv_sems, local_sems = refs[2 * n:]
        x, y, c = _place()
        me, sibling = (x, y, c), (x, y, 1 - c)
        chips = [(1 - x, y), (x, 1 - y), (1 - x, 1 - y)]

        def copy(a, k, block, to, src=None):
            px, py, pc = block
            slot = out_refs[a].at[4 * px + 2 * py + pc]
            return pltpu.make_async_remote_copy(
                src_ref=slot if src is None else src, dst_ref=slot,
                send_sem=send_sems.at[7 * a + k], recv_sem=recv_sems.at[7 * a + k], device_id=to, device_id_type=MESH)

        mine, first, passed = [], [], []
        for a in range(n):
            cp = pltpu.make_async_copy(x_refs[a], out_refs[a].at[4 * x + 2 * y + c], local_sems.at[a])
            cp.start()
            mine.append(cp)
            cps = [copy(a, 0, me, sibling, src=x_refs[a])]
            cps += [copy(a, 1 + j, me, (*chip, c), src=x_refs[a]) for j, chip in enumerate(chips)]
            for cp in cps:
                cp.start()
            first += cps
        for a in range(n):
            for j, chip in enumerate(chips):
                copy(a, 1 + j, (*chip, c), me).wait_recv()
                fwd = copy(a, 4 + j, (*chip, c), sibling)
                fwd.start()
                passed.append(fwd)
        for a in range(n):
            copy(a, 0, sibling, me).wait_recv()
            for j, chip in enumerate(chips):
                copy(a, 4 + j, (*chip, 1 - c), me).wait_recv()
        for cp in first + passed:
            cp.wait_send()
        for cp in mine:
            cp.wait()

    anyspec = pl.BlockSpec(memory_space=pl.ANY)
    return pl.pallas_call(
        body, name=name,
        out_shape=[_sds((NDEV, *s.shape), s.dtype) for s in shards],
        in_specs=[anyspec] * n, out_specs=[anyspec] * n,
        scratch_shapes=[pltpu.SemaphoreType.DMA((7 * n,)), pltpu.SemaphoreType.DMA((7 * n,)),
                        pltpu.SemaphoreType.DMA((n,))],
    )(*shards)


def _gather_first_copies(x_refs, out_refs, send_sems, recv_sems, local_sems):
    x, y, c = _place()
    slot = 4 * x + 2 * y + c
    peers = [(x, y, 1 - c), (1 - x, y, c), (x, 1 - y, c), (1 - x, 1 - y, c)]
    local, remote = [], []
    for a, (x_ref, out_ref) in enumerate(zip(x_refs, out_refs)):
        local.append(pltpu.make_async_copy(x_ref, out_ref.at[slot], local_sems.at[a]))
        for k, peer in enumerate(peers):
            remote.append(pltpu.make_async_remote_copy(
                src_ref=x_ref, dst_ref=out_ref.at[slot], send_sem=send_sems.at[4 * a + k],
                recv_sem=recv_sems.at[4 * a + k], device_id=peer, device_id_type=MESH))
    return local, remote


def _gather_first_scratch(n):
    return [pltpu.SemaphoreType.DMA((4 * n,)), pltpu.SemaphoreType.DMA((4 * n,)), pltpu.SemaphoreType.DMA((n,))]


def _chip_swap_copies(p_refs, out_refs, send_sems, recv_sems):
    x, y, c = _place()
    chips = [(1 - x, y), (x, 1 - y), (1 - x, 1 - y)]
    cps = []
    for a, (p_ref, out_ref) in enumerate(zip(p_refs, out_refs)):
        for j, (px, py) in enumerate(chips):
            cps.append(pltpu.make_async_remote_copy(
                src_ref=p_ref.at[2 * px + py], dst_ref=out_ref.at[j],
                send_sem=send_sems.at[3 * a + j], recv_sem=recv_sems.at[3 * a + j],
                device_id=(px, py, c), device_id_type=MESH))
    return cps


def _sibling_swap_copies(a_refs, out_refs, send_sems, recv_sems):
    x, y, c = _place()
    cps = []
    for a, (a_ref, out_ref) in enumerate(zip(a_refs, out_refs)):
        for k in range(4):
            cps.append(pltpu.make_async_remote_copy(
                src_ref=a_ref.at[k, 1 - c], dst_ref=out_ref.at[k],
                send_sem=send_sems.at[4 * a + k], recv_sem=recv_sems.at[4 * a + k],
                device_id=(x, y, 1 - c), device_id_type=MESH))
    return cps


def _pass_on_copies(buf_refs, send_sems, recv_sems):
    x, y, c = _place()
    chips = [(1 - x, y), (x, 1 - y), (1 - x, 1 - y)]
    cps = []
    for a, buf in enumerate(buf_refs):
        for j, (px, py) in enumerate(chips):
            block = buf.at[4 * px + 2 * py + c]
            cps.append(pltpu.make_async_remote_copy(
                src_ref=block, dst_ref=block, send_sem=send_sems.at[3 * a + j], recv_sem=recv_sems.at[3 * a + j],
                device_id=(x, y, 1 - c), device_id_type=MESH))
    return cps


def _swap_sibling(arrs, name):
    n = len(arrs)

    def body(*refs):
        cps = _sibling_swap_copies(refs[:n], refs[n:2 * n], *refs[2 * n:])
        for cp in cps:
            cp.start()
        for cp in cps:
            cp.wait()

    anyspec = pl.BlockSpec(memory_space=pl.ANY)
    return pl.pallas_call(
        body, name=name,
        out_shape=[_sds((4, *a.shape[2:]), a.dtype) for a in arrs],
        in_specs=[anyspec] * n, out_specs=[anyspec] * n,
        scratch_shapes=[pltpu.SemaphoreType.DMA((4 * n,)), pltpu.SemaphoreType.DMA((4 * n,))],
    )(*arrs)


def _in_proj(x, g, wcat, shards):
    tm = 256
    n = len(shards)

    def body(x_ref, g_ref, w_ref, *rest):
        x_refs, (h_ref, z_ref, ft_ref), out_refs, sems = rest[:n], rest[n:n + 3], rest[n + 3:2 * n + 3], rest[2 * n + 3:]

        @pl.when(pl.program_id(0) == 0)
        def _():
            local, remote = _gather_first_copies(x_refs, out_refs, *sems)
            for cp in local + remote:
                cp.start()

        xv = x_ref[...]
        h = (xv * _rms_r(xv) * g_ref[...]).astype(BF16)
        h_ref[...] = h
        z = _nt(h, w_ref[...])
        z_ref[...] = z
        ft_ref[...] = z[:, 2048:ZC].T[0:NH, :]

        @pl.when(pl.program_id(0) == T // tm - 1)
        def _():
            local, remote = _gather_first_copies(x_refs, out_refs, *sems)
            for cp in remote + local:
                cp.wait()

    row = lambda i: (i, 0)
    fixed = lambda i: (0, 0)
    anyspec = pl.BlockSpec(memory_space=pl.ANY)
    res = pl.pallas_call(
        body, name="in_proj", grid=(T // tm,),
        in_specs=[pl.BlockSpec((tm, D), row), pl.BlockSpec((1, D), fixed), pl.BlockSpec((ZC, D), fixed)] + [anyspec] * n,
        out_specs=[pl.BlockSpec((tm, D), row), pl.BlockSpec((tm, ZC), row), pl.BlockSpec((NH, tm), lambda i: (0, i))]
        + [anyspec] * n,
        out_shape=[_sds((T, D), BF16), _sds((T, ZC)), _sds((NH, T))] + [_sds((NDEV, *s.shape), s.dtype) for s in shards],
        scratch_shapes=_gather_first_scratch(n),
        compiler_params=_cp(("arbitrary",)),
    )(x, g, wcat, *shards)
    return res[0], res[1], res[2], res[3:]


def _in_proj_dw(h, dq, dk, dv, du, df):
    tm = 512

    def body(h_ref, dq_ref, dk_ref, dv_ref, du_ref, df_ref, o_ref):
        hv = h_ref[...]
        for j, d_ref in enumerate((dq_ref, dk_ref, dv_ref, du_ref)):
            o_ref[512 * j:512 * (j + 1), :] = _tn(d_ref[...], hv)
        o_ref[2048:ZC, :] = _tn(df_ref[...], hv)

    full = lambda w: pl.BlockSpec((T, w), lambda i: (0, 0))
    return pl.pallas_call(
        body, name="in_proj_dw", grid=(D // tm,),
        in_specs=[pl.BlockSpec((T, tm), lambda i: (0, i)), full(512), full(512), full(512), full(512), full(LANES)],
        out_specs=pl.BlockSpec((ZC, tm), lambda i: (0, i)), out_shape=_sds((ZC, D)),
        compiler_params=_cp(("parallel",)),
    )(h, dq, dk, dv, du, df)


def _in_proj_bwd(dq, dk, dv, du, df, wcat, x, g, dres):
    tm = 256

    def body(dq_ref, dk_ref, dv_ref, du_ref, df_ref, w_ref, x_ref, g_ref, dres_ref, dx_ref, dg_ref):
        dh = _nn(df_ref[...], w_ref[2048:ZC, :])
        for j, d_ref in enumerate((dq_ref, dk_ref, dv_ref, du_ref)):
            dh = dh + _nn(d_ref[...], w_ref[512 * j:512 * (j + 1), :])
        xv = x_ref[...]
        r = _rms_r(xv)
        dxr, dgp = _rms_bwd(xv * r, r, g_ref[...], dh)
        dx_ref[...] = dres_ref[...] + dxr

        @pl.when(pl.program_id(0) == 0)
        def _():
            dg_ref[...] = jnp.zeros_like(dg_ref)
        dg_ref[...] += dgp

    row = lambda i: (i, 0)
    fixed = lambda i: (0, 0)
    part = pl.BlockSpec((tm, 512), row)
    return pl.pallas_call(
        body, name="in_proj_bwd", grid=(T // tm,),
        in_specs=[part, part, part, part, pl.BlockSpec((tm, LANES), row), pl.BlockSpec((ZC, D), fixed),
                  pl.BlockSpec((tm, D), row), pl.BlockSpec((1, D), fixed), pl.BlockSpec((tm, D), row)],
        out_specs=[pl.BlockSpec((tm, D), row), pl.BlockSpec((1, D), fixed)],
        out_shape=[_sds((T, D)), _sds((1, D))],
        compiler_params=_cp(("arbitrary",)),
    )(dq, dk, dv, du, df, wcat, x, g, dres)


def _stack_lanes(v):
    return jnp.concatenate([v[:, LANES * b:LANES * (b + 1)] for b in range(T // LANES)], axis=0)


def _unstack_lanes(s):
    return jnp.concatenate([s[8 * b:8 * b + 8, :] for b in range(T // LANES)], axis=1)


def _cumsum_lanes(v, reverse):
    st = _stack_lanes(v)
    ri = lax.broadcasted_iota(jnp.int32, (LANES, LANES), 0)
    ci = lax.broadcasted_iota(jnp.int32, (LANES, LANES), 1)
    tri = (ri >= ci) if reverse else (ri <= ci)
    intra = jnp.dot(st, tri.astype(F32), precision=HI, preferred_element_type=F32)
    tot = intra[:, 0:1] if reverse else intra[:, LANES - 1:LANES]
    same_head = (ci % 8) == (ri % 8)
    other = (ci // 8 > ri // 8) if reverse else (ci // 8 < ri // 8)
    off = jnp.dot((same_head & other).astype(F32), jnp.broadcast_to(tot, (LANES, LANES)), precision=HI,
                  preferred_element_type=F32)
    return _unstack_lanes(intra + off)


def _gates_fwd(f_t, b_f):
    def body(f_ref, b_ref, c_ref):
        z = f_ref[...] + b_ref[...]
        lf = jnp.minimum(z, 0.0) - jnp.log(1.0 + jnp.exp(-jnp.abs(z)))
        c_ref[...] = _cumsum_lanes(lf, reverse=False)

    return pl.pallas_call(body, name="gates_fwd", out_shape=_sds((NH, T)), compiler_params=_cp())(f_t, b_f)


def _gates_bwd(dck, f_t, b_f):
    def body(d_ref, f_ref, b_ref, df_ref, db_ref):
        z = f_ref[...] + b_ref[...]
        dlf = _cumsum_lanes(d_ref[...], reverse=True)
        df = dlf * jax.nn.sigmoid(-z)
        df_ref[...] = df
        db_ref[...] = jnp.sum(df, axis=1, keepdims=True)

    return pl.pallas_call(body, name="gates_bwd", out_shape=[_sds((NH, T)), _sds((NH, 1))],
                          compiler_params=_cp())(dck, f_t, b_f)


def _lower_triangle():
    rows = lax.broadcasted_iota(jnp.int32, (BQ, BQ), 0)
    cols = lax.broadcasted_iota(jnp.int32, (BQ, BQ), 1)
    return cols <= rows


def _logit_parts(qb, kb, ck_row, lo, hi, tri):
    parts = []
    if lo > 0:
        parts.append((_nt(qb[lo:hi], kb[0:lo]) - ck_row[:, 0:lo], 0, lo))
    parts.append((jnp.where(tri, _nt(qb[lo:hi], kb[lo:hi]) - ck_row[:, lo:hi], NEG), lo, hi))
    return parts


_HP = NH // 2


def _qkv_specs():
    return [pl.BlockSpec((T, LANES), lambda p: (0, p)), pl.BlockSpec((T, LANES), lambda p: (0, _HP + p)),
            pl.BlockSpec((T, LANES), lambda p: (0, 2 * _HP + p))]


def _attn_fwd(z, ck, g_q, g_k, shards, bufs):
    scale = DH ** -0.5
    n, m = len(shards), len(bufs)

    def body(q_ref, k_ref, v_ref, ck_ref, gq_ref, gk_ref, *rest):
        x_refs, (o_ref, lse_ref) = rest[:n], rest[n + m:n + m + 2]
        out_refs, buf_refs, sems = rest[n + m + 2:2 * n + m + 2], rest[2 * n + m + 2:2 * n + 2 * m + 2], rest[2 * n + 2 * m + 2:]

        @pl.when(pl.program_id(0) == 0)
        def _():
            local, remote = _gather_first_copies(x_refs, out_refs, *sems[:3])
            for cp in local + remote + _pass_on_copies(buf_refs, *sems[3:]):
                cp.start()

        qb, kb, vb = [], [], []
        for hh in range(2):
            cols = slice(DH * hh, DH * (hh + 1))
            qv, kv = q_ref[:, cols], k_ref[:, cols]
            qb.append((qv * _rms_r(qv) * gq_ref[...] * scale).astype(BF16))
            kb.append((kv * _rms_r(kv) * gk_ref[...]).astype(BF16))
            vb.append(v_ref[:, cols].astype(BF16))
        tri = _lower_triangle()
        for i in range(T // BQ):
            lo, hi = i * BQ, (i + 1) * BQ
            outs = []
            for hh in range(2):
                parts = _logit_parts(qb[hh], kb[hh], ck_ref[0, hh:hh + 1, :], lo, hi, tri)
                top = jnp.max(parts[-1][0], axis=-1, keepdims=True)
                if lo > 0:
                    top = jnp.maximum(top, jnp.max(parts[0][0], axis=-1, keepdims=True))
                l, o = 0.0, 0.0
                for s, k0, k1 in parts:
                    p = jnp.exp(s - top)
                    l = l + jnp.sum(p, axis=-1, keepdims=True)
                    o = o + _nn(p.astype(BF16), vb[hh][k0:k1])
                outs.append(o / l)
                lse_ref[0, lo:hi, hh:hh + 1] = top + jnp.log(l)
            o_ref[lo:hi, :] = jnp.concatenate(outs, axis=1)

        @pl.when(pl.program_id(0) == _HP - 1)
        def _():
            local, remote = _gather_first_copies(x_refs, out_refs, *sems[:3])
            for cp in remote + local + _pass_on_copies(buf_refs, *sems[3:]):
                cp.wait()

    fixed = lambda p: (0, 0)
    anyspec = pl.BlockSpec(memory_space=pl.ANY)
    res = pl.pallas_call(
        body, name="attn_fwd", grid=(_HP,),
        in_specs=_qkv_specs() + [pl.BlockSpec((1, 2, T), lambda p: (p, 0, 0)), pl.BlockSpec((1, DH), fixed),
                                 pl.BlockSpec((1, DH), fixed)] + [anyspec] * (n + m),
        out_specs=[pl.BlockSpec((T, LANES), lambda p: (0, p)), pl.BlockSpec((1, T, 2), lambda p: (p, 0, 0))]
        + [anyspec] * (n + m),
        out_shape=[_sds((T, AW)), _sds((_HP, T, 2))] + [_sds((NDEV, *s.shape), s.dtype) for s in shards]
        + [_sds(b.shape, b.dtype) for b in bufs],
        input_output_aliases={6 + n + i: 2 + n + i for i in range(m)},
        scratch_shapes=_gather_first_scratch(n) + [pltpu.SemaphoreType.DMA((3 * m,)), pltpu.SemaphoreType.DMA((3 * m,))],
        compiler_params=_cp(("arbitrary",)),
    )(z, z, z, ck, g_q, g_k, *shards, *bufs)
    return res[0], res[1], res[2:2 + n], res[2 + n:]


def _attn_bwd(z, ck, g_q, g_k, lse, datt, parts):
    scale = DH ** -0.5
    n = len(parts)

    def body(q_ref, k_ref, v_ref, ck_ref, gq_ref, gk_ref, lse_ref, do_ref, *rest):
        p_refs, rest = rest[:n], rest[n:]
        dq_ref, dk_ref, dv_ref, dck_ref, dgq_ref, dgk_ref = rest[:6]
        fc_refs, (dkn_acc, dv_acc, dc_acc), sems = rest[6:6 + n], rest[6 + n:9 + n], rest[9 + n:]

        @pl.when(pl.program_id(0) == 0)
        def _():
            for cp in _chip_swap_copies(p_refs, fc_refs, *sems):
                cp.start()

        nq, nk, rq, rk, qb, kb, vb, dob = [], [], [], [], [], [], [], []
        for hh in range(2):
            cols = slice(DH * hh, DH * (hh + 1))
            qv, kv = q_ref[:, cols], k_ref[:, cols]
            rq.append(_rms_r(qv))
            rk.append(_rms_r(kv))
            nq.append(qv * rq[hh])
            nk.append(kv * rk[hh])
            qb.append((nq[hh] * gq_ref[...] * scale).astype(BF16))
            kb.append((nk[hh] * gk_ref[...]).astype(BF16))
            vb.append(v_ref[:, cols].astype(BF16))
            dob.append(do_ref[:, cols].astype(BF16))
        dkn_acc[...] = jnp.zeros_like(dkn_acc)
        dv_acc[...] = jnp.zeros_like(dv_acc)
        dc_acc[...] = jnp.zeros_like(dc_acc)
        dgq = jnp.zeros((1, DH), F32)
        tri = _lower_triangle()
        for i in range(T // BQ):
            lo, hi = i * BQ, (i + 1) * BQ
            dqs = []
            for hh in range(2):
                lse = lse_ref[0, lo:hi, hh:hh + 1]
                pd = []
                for s, k0, k1 in _logit_parts(qb[hh], kb[hh], ck_ref[0, hh:hh + 1, :], lo, hi, tri):
                    pd.append((jnp.exp(s - lse), _nt(dob[hh][lo:hi], vb[hh][k0:k1]), k0, k1))
                mean = sum(jnp.sum(p * dp, axis=-1, keepdims=True) for p, dp, _, _ in pd)
                dqn = 0.0
                for p, dp, k0, k1 in pd:
                    ds = p * (dp - mean)
                    dsb = ds.astype(BF16)
                    dv_acc[hh, k0:k1, :] += _tn(p.astype(BF16), dob[hh][lo:hi])
                    dkn_acc[hh, k0:k1, :] += _tn(dsb, qb[hh][lo:hi])
                    dc_acc[hh:hh + 1, k0:k1] -= jnp.sum(ds, axis=0, keepdims=True)
                    dqn = dqn + _nn(dsb, kb[hh][k0:k1])
                dqx, dgp = _rms_bwd(nq[hh][lo:hi], rq[hh][lo:hi], gq_ref[...], dqn * scale)
                dqs.append(dqx)
                dgq = dgq + dgp
            dq_ref[lo:hi, :] = jnp.concatenate(dqs, axis=1).astype(BF16)
        dks, dgk = [], jnp.zeros((1, DH), F32)
        for hh in range(2):
            dkx, dgp = _rms_bwd(nk[hh], rk[hh], gk_ref[...], dkn_acc[hh])
            dks.append(dkx)
            dgk = dgk + dgp
        dk_ref[...] = jnp.concatenate(dks, axis=1).astype(BF16)
        dv_ref[...] = jnp.concatenate([dv_acc[0], dv_acc[1]], axis=1).astype(BF16)
        dck_ref[0] = dc_acc[...]

        @pl.when(pl.program_id(0) == 0)
        def _():
            dgq_ref[...] = jnp.zeros_like(dgq_ref)
            dgk_ref[...] = jnp.zeros_like(dgk_ref)
        dgq_ref[...] += dgq
        dgk_ref[...] += dgk

        @pl.when(pl.program_id(0) == _HP - 1)
        def _():
            for cp in _chip_swap_copies(p_refs, fc_refs, *sems):
                cp.wait()

    fixed = lambda p: (0, 0)
    pair = pl.BlockSpec((T, LANES), lambda p: (0, p))
    gsp = pl.BlockSpec((1, DH), fixed)
    ckspec = pl.BlockSpec((1, 2, T), lambda p: (p, 0, 0))
    anyspec = pl.BlockSpec(memory_space=pl.ANY)
    res = pl.pallas_call(
        body, name="attn_bwd", grid=(_HP,),
        in_specs=_qkv_specs() + [ckspec, gsp, gsp, pl.BlockSpec((1, T, 2), lambda p: (p, 0, 0)), pair] + [anyspec] * n,
        out_specs=[pair, pair, pair, ckspec, gsp, gsp] + [anyspec] * n,
        out_shape=[_sds((T, AW), BF16)] * 3 + [_sds((_HP, 2, T)), _sds((1, DH)), _sds((1, DH))]
        + [_sds((3, *p.shape[1:]), p.dtype) for p in parts],
        scratch_shapes=[pltpu.VMEM((2, T, DH), F32), pltpu.VMEM((2, T, DH), F32), pltpu.VMEM((2, T), F32),
                        pltpu.SemaphoreType.DMA((3 * n,)), pltpu.SemaphoreType.DMA((3 * n,))],
        compiler_params=_cp(("arbitrary",)),
    )(z, z, z, ck, g_q, g_k, lse, datt, *parts)
    return (*res[:6], res[6:])


def _disc(lr, li, ls, br_t, bi_t):
    step = jnp.exp(ls)
    er = jnp.exp(lr * step)
    ab_re = er * jnp.cos(li * step)
    ab_im = er * jnp.sin(li * step)
    num_re = ab_re - 1.0
    num_im = ab_im
    den = lr * lr + li * li
    f_re = (num_re * lr + num_im * li) / den
    f_im = (num_im * lr - num_re * li) / den
    bb_re = f_re[:, None, :] * br_t - f_im[:, None, :] * bi_t
    bb_im = f_re[:, None, :] * bi_t + f_im[:, None, :] * br_t
    return ab_re, ab_im, bb_re, bb_im


def _disc_fwd(lr, li, ls, br_t, bi_t):
    def body(lr_ref, li_ref, ls_ref, br_ref, bi_ref, ar_ref, ai_ref, bbr_ref, bbi_ref):
        ar, ai, bbr, bbi = _disc(lr_ref[...], li_ref[...], ls_ref[...], br_ref[...], bi_ref[...])
        ar_ref[...] = ar
        ai_ref[...] = ai
        bbr_ref[...] = bbr
        bbi_ref[...] = bbi

    return pl.pallas_call(
        body, name="ssm_disc_fwd",
        out_shape=[_sds((NG, NP)), _sds((NG, NP)), _sds((NG, GS, NP)), _sds((NG, GS, NP))],
        compiler_params=_cp())(lr, li, ls, br_t, bi_t)


def _disc_bwd(lr, li, ls, br_t, bi_t, dar, dai, dbbr, dbbi):
    def body(lr_ref, li_ref, ls_ref, br_ref, bi_ref, dar_ref, dai_ref, dbbr_ref, dbbi_ref,
             dlr_ref, dli_ref, dls_ref, dbr_ref, dbi_ref):
        _, vjp = jax.vjp(_disc, lr_ref[...], li_ref[...], ls_ref[...], br_ref[...], bi_ref[...])
        dlr, dli, dls, dbr, dbi = vjp((dar_ref[...], dai_ref[...], dbbr_ref[...], dbbi_ref[...]))
        dlr_ref[...] = dlr
        dli_ref[...] = dli
        dls_ref[...] = dls
        dbr_ref[...] = dbr
        dbi_ref[...] = dbi

    return pl.pallas_call(
        body, name="ssm_disc_bwd",
        out_shape=[_sds((NG, NP)), _sds((NG, NP)), _sds((NG, 1)), _sds((NG, GS, NP)), _sds((NG, GS, NP))],
        compiler_params=_cp())(lr, li, ls, br_t, bi_t, dar, dai, dbbr, dbbi)


def _cmul(ar, ai, br, bi):
    return ar * br - ai * bi, ar * bi + ai * br


def _scan(buf, a_re, a_im, reverse, other=None):
    ar = [jnp.broadcast_to(a_re[:, LANES * k:LANES * (k + 1)], (NSEG, LANES)) for k in range(4)]
    ai = [jnp.broadcast_to(a_im[:, LANES * k:LANES * (k + 1)], (NSEG, LANES)) for k in range(4)]

    def tile(ref, i, k):
        return ref[pl.ds(pl.multiple_of(i * NSEG, NSEG), NSEG), LANES * k:LANES * (k + 1)]

    def advance(i, xr, xi):
        nr = [ar[k] * xr[k] - ai[k] * xi[k] + tile(buf, i, k) for k in range(4)]
        ni = [ar[k] * xi[k] + ai[k] * xr[k] + tile(buf, i, 4 + k) for k in range(4)]
        return nr, ni

    def index(i):
        return (SEG - 1 - i) if reverse else i

    zeros = tuple(jnp.zeros((NSEG, LANES), F32) for _ in range(4))

    def sweep1(i, carry):
        nr, ni = advance(index(i), carry[0], carry[1])
        return tuple(nr), tuple(ni)

    er, ei = lax.fori_loop(0, SEG, sweep1, (zeros, zeros), unroll=4)

    row = lax.broadcasted_iota(jnp.int32, (NSEG, LANES), 0)
    sr, si = [], []
    for k in range(4):
        pr, pi = ar[k], ai[k]
        for _ in range(SEG.bit_length() - 1):
            pr, pi = _cmul(pr, pi, pr, pi)
        ir, ii = er[k], ei[k]
        for d in (1, 2, 4):
            shift = (NSEG - d) if reverse else d
            ok = (row < NSEG - d) if reverse else (row >= d)
            mr, mi = _cmul(pr, pi, pltpu.roll(ir, shift, 0), pltpu.roll(ii, shift, 0))
            ir = ir + jnp.where(ok, mr, 0.0)
            ii = ii + jnp.where(ok, mi, 0.0)
            pr, pi = _cmul(pr, pi, pr, pi)
        shift = (NSEG - 1) if reverse else 1
        ok = (row < NSEG - 1) if reverse else (row >= 1)
        sr.append(jnp.where(ok, pltpu.roll(ir, shift, 0), 0.0))
        si.append(jnp.where(ok, pltpu.roll(ii, shift, 0), 0.0))

    def sweep2(i, carry):
        xr, xi, dar, dai = carry
        idx = index(i)
        if other is not None:
            orr = [tile(other, idx, k) for k in range(4)]
            oi = [tile(other, idx, 4 + k) for k in range(4)]
            dar = tuple(dar[k] + xr[k] * orr[k] + xi[k] * oi[k] for k in range(4))
            dai = tuple(dai[k] + xi[k] * orr[k] - xr[k] * oi[k] for k in range(4))
        nr, ni = advance(idx, xr, xi)
        start = pl.multiple_of(idx * NSEG, NSEG)
        for k in range(4):
            buf[pl.ds(start, NSEG), LANES * k:LANES * (k + 1)] = nr[k]
            buf[pl.ds(start, NSEG), LANES * (4 + k):LANES * (5 + k)] = ni[k]
        return tuple(nr), tuple(ni), dar, dai

    _, _, dar, dai = lax.fori_loop(0, SEG, sweep2, (tuple(sr), tuple(si), zeros, zeros), unroll=4)
    if other is None:
        return None
    da_re = jnp.concatenate([jnp.sum(d, axis=0, keepdims=True) for d in dar], axis=1)
    da_im = jnp.concatenate([jnp.sum(d, axis=0, keepdims=True) for d in dai], axis=1)
    return da_re, da_im


_SSM_BLOCKS = 4


def _ssm_fwd(u, bb, ab, cc, dsk, shards):
    n = len(shards)

    def body(u_ref, bb_ref, a_ref, cc_ref, d_ref, *rest):
        x_refs, y_ref, xs_ref, out_refs, sems = rest[:n], rest[n], rest[n + 1], rest[n + 2:2 * n + 2], rest[2 * n + 2:]

        @pl.when(pl.program_id(0) == 0)
        def _():
            local, remote = _gather_first_copies(x_refs, out_refs, *sems)
            for cp in local + remote:
                cp.start()

        ub = u_ref[...]
        xb = xs_ref.at[0]
        xb[...] = _nn(ub.astype(BF16), bb_ref[0])
        _scan(xb, a_ref[0, 0:1, :], a_ref[0, 1:2, :], reverse=False)
        y_ref[...] = _nn(xb[...].astype(BF16), cc_ref[0]) + d_ref[0] * ub

        @pl.when(pl.program_id(0) == _SSM_BLOCKS - 1)
        def _():
            local, remote = _gather_first_copies(x_refs, out_refs, *sems)
            for cp in remote + local:
                cp.wait()

    blk = lambda j: (j, 0, 0)
    col = pl.BlockSpec((T, LANES), lambda j: (0, j))
    anyspec = pl.BlockSpec(memory_space=pl.ANY)
    res = pl.pallas_call(
        body, name="ssm_fwd", grid=(_SSM_BLOCKS,),
        in_specs=[col, pl.BlockSpec((1, LANES, 1024), blk), pl.BlockSpec((1, 2, 512), blk),
                  pl.BlockSpec((1, 1024, LANES), blk), pl.BlockSpec((1, 1, LANES), blk)] + [anyspec] * n,
        out_specs=[col, pl.BlockSpec((1, T, 1024), blk)] + [anyspec] * n,
        out_shape=[_sds((T, SW)), _sds((_SSM_BLOCKS, T, 1024))] + [_sds((NDEV, *s.shape), s.dtype) for s in shards],
        scratch_shapes=_gather_first_scratch(n),
        compiler_params=_cp(("arbitrary",)),
    )(u, bb, ab, cc, dsk, *shards)
    return res[0], res[1], res[2:]


def _ssm_bwd(u, dy, xs, bb, ab, cc, dsk, arrs):
    n = len(arrs)

    def body(u_ref, dy_ref, x_ref, bb_ref, a_ref, cc_ref, d_ref, *rest):
        a_refs, rest = rest[:n], rest[n:]
        du_ref, dbb_ref, da_ref, dcc_ref, dd_ref = rest[:5]
        fs_refs, gb, sems = rest[5:5 + n], rest[5 + n], rest[6 + n:]

        @pl.when(pl.program_id(0) == 0)
        def _():
            for cp in _sibling_swap_copies(a_refs, fs_refs, *sems):
                cp.start()

        ub, dyv = u_ref[...], dy_ref[...]
        ubb, dyb = ub.astype(BF16), dyv.astype(BF16)
        a_re, a_im = a_ref[0, 0:1, :], a_ref[0, 1:2, :]
        dcc_ref[0] = _tn(x_ref[0].astype(BF16), dyb)
        gb[...] = _nt(dyb, cc_ref[0])
        da_re, da_im = _scan(gb, a_re, -a_im, reverse=True, other=x_ref.at[0])
        da_ref[0] = jnp.concatenate([da_re, da_im], axis=0)
        gc = gb[...].astype(BF16)
        du_ref[...] = _nt(gc, bb_ref[0]) + d_ref[0] * dyv
        dbb_ref[0] = _tn(ubb, gc)
        dd_ref[0] = jnp.sum(dyv * ub, axis=0, keepdims=True)

        @pl.when(pl.program_id(0) == _SSM_BLOCKS - 1)
        def _():
            for cp in _sibling_swap_copies(a_refs, fs_refs, *sems):
                cp.wait()

    blk = lambda j: (j, 0, 0)
    col = pl.BlockSpec((T, LANES), lambda j: (0, j))
    anyspec = pl.BlockSpec(memory_space=pl.ANY)
    res = pl.pallas_call(
        body, name="ssm_bwd", grid=(_SSM_BLOCKS,),
        in_specs=[col, col, pl.BlockSpec((1, T, 1024), blk), pl.BlockSpec((1, LANES, 1024), blk),
                  pl.BlockSpec((1, 2, 512), blk), pl.BlockSpec((1, 1024, LANES), blk), pl.BlockSpec((1, 1, LANES), blk)]
        + [anyspec] * n,
        out_specs=[col, pl.BlockSpec((1, LANES, 1024), blk), pl.BlockSpec((1, 2, 512), blk),
                   pl.BlockSpec((1, 1024, LANES), blk), pl.BlockSpec((1, 1, LANES), blk)] + [anyspec] * n,
        out_shape=[_sds((T, SW)), _sds((4, LANES, 1024)), _sds((4, 2, 512)), _sds((4, 1024, LANES)), _sds((4, 1, LANES))]
        + [_sds((4, *a.shape[2:]), a.dtype) for a in arrs],
        scratch_shapes=[pltpu.VMEM((T, 1024), F32), pltpu.SemaphoreType.DMA((4 * n,)), pltpu.SemaphoreType.DMA((4 * n,))],
        compiler_params=_cp(("arbitrary",)),
    )(u, dy, xs, bb, ab, cc, dsk, *arrs)
    return (*res[:5], res[5:])


def _interleave(a):
    return a.reshape(NSEG, SEG, a.shape[1]).transpose(1, 0, 2).reshape(a.shape)


def _deinterleave(a):
    return a.reshape(SEG, NSEG, a.shape[1]).transpose(1, 0, 2).reshape(a.shape)


_TM_MIX = 512


def _mix_parts(att, y, wg, bg):
    y2 = jax.nn.gelu(y)
    y2b = y2.astype(BF16)
    sg = jax.nn.sigmoid(_nn(y2b, wg) + bg)
    ssm = y2 * sg
    ra, rs = _rms_r(att), _rms_r(ssm)
    return y2, y2b, sg, ssm, ra, rs


def _mix_fwd(att, y, x, wg, bg, ga, gs, wo, bufs):
    m = len(bufs)

    def body(att_ref, y_ref, x_ref, wg_ref, bg_ref, ga_ref, gs_ref, wo_ref, *rest):
        x1_ref, buf_refs, sems = rest[m], rest[m + 1:2 * m + 1], rest[2 * m + 1:]

        @pl.when(pl.program_id(0) == 0)
        def _():
            for cp in _pass_on_copies(buf_refs, *sems):
                cp.start()

        attv = att_ref[...]
        _, _, _, ssm, ra, rs = _mix_parts(attv, y_ref[...], wg_ref[...], bg_ref[...])
        ma = (attv * ra * ga_ref[...]).astype(BF16)
        ms = (ssm * rs * gs_ref[...]).astype(BF16)
        x1_ref[...] = x_ref[...] + _nn(ma, wo_ref[0:AW, :]) + _nn(ms, wo_ref[AW:D, :])

        @pl.when(pl.program_id(0) == T // _TM_MIX - 1)
        def _():
            for cp in _pass_on_copies(buf_refs, *sems):
                cp.wait()

    row = lambda i: (i, 0)
    fixed = lambda i: (0, 0)
    half = pl.BlockSpec((_TM_MIX, AW), row)
    vec = pl.BlockSpec((1, AW), fixed)
    anyspec = pl.BlockSpec(memory_space=pl.ANY)
    res = pl.pallas_call(
        body, name="mix_fwd", grid=(T // _TM_MIX,),
        in_specs=[half, half, pl.BlockSpec((_TM_MIX, D), row), pl.BlockSpec((SW, SW), fixed), vec, vec, vec,
                  pl.BlockSpec((D, D), fixed)] + [anyspec] * m,
        out_specs=[pl.BlockSpec((_TM_MIX, D), row)] + [anyspec] * m,
        out_shape=[_sds((T, D))] + [_sds(b.shape, b.dtype) for b in bufs],
        input_output_aliases={8 + i: 1 + i for i in range(m)},
        scratch_shapes=[pltpu.SemaphoreType.DMA((3 * m,)), pltpu.SemaphoreType.DMA((3 * m,))],
        compiler_params=_cp(("arbitrary",)),
    )(att, y, x, wg, bg, ga, gs, wo, *bufs)
    return res[0], res[1:]


def _mix_bwd(att, y, dx1, wg, bg, ga, gs, wo):
    last = T // _TM_MIX - 1

    def body(att_ref, y_ref, d_ref, wg_ref, bg_ref, ga_ref, gs_ref, wo_ref,
             datt_ref, dy_ref, dwg_ref, dbg_ref, dga_ref, dgs_ref, dwo_ref, dwg_acc, dwo_acc):
        attv, yv, db = att_ref[...], y_ref[...], d_ref[...].astype(BF16)
        y2, y2b, sg, ssm, ra, rs = _mix_parts(attv, yv, wg_ref[...], bg_ref[...])
        na, ns = attv * ra, ssm * rs
        ma = (na * ga_ref[...]).astype(BF16)
        ms = (ns * gs_ref[...]).astype(BF16)
        dma = _nt(db, wo_ref[0:AW, :])
        dms = _nt(db, wo_ref[AW:D, :])
        datt, dga = _rms_bwd(na, ra, ga_ref[...], dma)
        dssm, dgs = _rms_bwd(ns, rs, gs_ref[...], dms)
        dgl = dssm * y2 * sg * (1.0 - sg)
        dglb = dgl.astype(BF16)
        dy2 = dssm * sg + _nt(dglb, wg_ref[...])
        _, gelu_vjp = jax.vjp(jax.nn.gelu, yv)
        datt_ref[...] = datt
        dy_ref[...] = gelu_vjp(dy2)[0]

        @pl.when(pl.program_id(0) == 0)
        def _():
            dwg_acc[...] = jnp.zeros_like(dwg_acc)
            dwo_acc[...] = jnp.zeros_like(dwo_acc)
            dbg_ref[...] = jnp.zeros_like(dbg_ref)
            dga_ref[...] = jnp.zeros_like(dga_ref)
            dgs_ref[...] = jnp.zeros_like(dgs_ref)
        dwg_acc[...] += _tn(y2b, dglb)
        dbg_ref[...] += jnp.sum(dgl, axis=0, keepdims=True)
        dga_ref[...] += dga
        dgs_ref[...] += dgs
        dwo_acc[0:AW, :] += _tn(ma, db)
        dwo_acc[AW:D, :] += _tn(ms, db)

        @pl.when(pl.program_id(0) == last)
        def _():
            dwg_ref[...] = dwg_acc[...].astype(BF16)
            dwo_ref[...] = dwo_acc[...].astype(BF16)

    row = lambda i: (i, 0)
    fixed = lambda i: (0, 0)
    half = pl.BlockSpec((_TM_MIX, AW), row)
    vec = pl.BlockSpec((1, AW), fixed)
    sq = pl.BlockSpec((SW, SW), fixed)
    big = pl.BlockSpec((D, D), fixed)
    return pl.pallas_call(
        body, name="mix_bwd", grid=(T // _TM_MIX,),
        in_specs=[half, half, pl.BlockSpec((_TM_MIX, D), row), sq, vec, vec, vec, big],
        out_specs=[half, half, sq, vec, vec, vec, big],
        out_shape=[_sds((T, AW)), _sds((T, SW)), _sds((SW, SW), BF16), _sds((1, SW)), _sds((1, AW)), _sds((1, SW)),
                   _sds((D, D), BF16)],
        scratch_shapes=[pltpu.VMEM((SW, SW), F32), pltpu.VMEM((D, D), F32)],
        compiler_params=_cp(("arbitrary",)),
    )(att, y, dx1, wg, bg, ga, gs, wo)


_NCB = -(-CS // LANES)


def _ffn_up(x1, g, wu4):
    tm = 1024

    def body(x_ref, g_ref, w_ref, h_ref, up_ref):
        @pl.when(pl.program_id(1) == 0)
        def _():
            xv = x_ref[...]
            h_ref[...] = (xv * _rms_r(xv) * g_ref[...]).astype(BF16)
        up_ref[0, 0] = _nt(h_ref[...], w_ref[0, 0])

    return pl.pallas_call(
        body, name="ffn_up", grid=(T // tm, NDEV),
        in_specs=[pl.BlockSpec((tm, D), lambda i, e: (i, 0)), pl.BlockSpec((1, D), lambda i, e: (0, 0)),
                  pl.BlockSpec((1, 1, CS, D), lambda i, e: (e // 4, e % 4, 0, 0))],
        out_specs=[pl.BlockSpec((tm, D), lambda i, e: (i, 0)),
                   pl.BlockSpec((1, 1, tm, CS), lambda i, e: (e // 4, e % 4, i, 0))],
        out_shape=[_sds((T, D), BF16), _sds((2, 4, T, CS))],
        compiler_params=_cp(("parallel", "arbitrary")),
    )(x1, g, wu4)


def _shift_down(h, k):
    row = lax.broadcasted_iota(jnp.int32, h.shape, 0)
    return jnp.where(row >= k, pltpu.roll(h, k, 0), 0.0)


def _shift_up(h, k):
    row = lax.broadcasted_iota(jnp.int32, h.shape, 0)
    return jnp.where(row < T - k, pltpu.roll(h, T - k, 0), 0.0)


def _conv(h, w, b):
    return w[0:1, :] * _shift_down(h, 2) + w[1:2, :] * _shift_down(h, 1) + w[2:3, :] * h + b


def _chan(gv, rows):
    return pl.BlockSpec((1, 1, rows, LANES), lambda d, j: (gv, d, 0, j))


def _ffn_act(up4, cw4, cb4):
    def body(ug_ref, uv_ref, wg_ref, wv_ref, bg_ref, bv_ref, act_ref):
        g = _conv(ug_ref[0, 0], wg_ref[0, 0], bg_ref[0, 0])
        v = _conv(uv_ref[0, 0], wv_ref[0, 0], bv_ref[0, 0])
        act_ref[0] = (g * jax.nn.sigmoid(g) * v).astype(BF16)

    return pl.pallas_call(
        body, name="ffn_act", grid=(4, _NCB),
        in_specs=[_chan(0, T), _chan(1, T), _chan(0, 3), _chan(1, 3), _chan(0, 1), _chan(1, 1)],
        out_specs=pl.BlockSpec((1, T, LANES), lambda d, j: (d, 0, j)), out_shape=_sds((4, T, CS), BF16),
        compiler_params=_cp(("parallel", "parallel")),
    )(up4, up4, cw4, cw4, cb4, cb4)


def _ffn_act_bwd(up4, dact, cw4, cb4):
    def conv_bwd(h, w, d):
        dup = w[2:3, :] * d + w[1:2, :] * _shift_up(d, 1) + w[0:1, :] * _shift_up(d, 2)
        dw = jnp.concatenate([jnp.sum(d * _shift_down(h, 2), axis=0, keepdims=True),
                              jnp.sum(d * _shift_down(h, 1), axis=0, keepdims=True),
                              jnp.sum(d * h, axis=0, keepdims=True)], axis=0)
        return dup, dw, jnp.sum(d, axis=0, keepdims=True)

    def body(ug_ref, uv_ref, da_ref, wg_ref, wv_ref, bg_ref, bv_ref, dup_ref, dw_ref, db_ref):
        ug, uv, da = ug_ref[0, 0], uv_ref[0, 0], da_ref[0]
        g = _conv(ug, wg_ref[0, 0], bg_ref[0, 0])
        v = _conv(uv, wv_ref[0, 0], bv_ref[0, 0])
        sg = jax.nn.sigmoid(g)
        dg = da * v * (sg * (1.0 + g * (1.0 - sg)))
        dv = da * (g * sg)
        dug, dwg, dbg = conv_bwd(ug, wg_ref[0, 0], dg)
        duv, dwv, dbv = conv_bwd(uv, wv_ref[0, 0], dv)
        dup_ref[0, 0] = dug.astype(BF16)
        dup_ref[1, 0] = duv.astype(BF16)
        dw_ref[0, 0] = dwg
        dw_ref[1, 0] = dwv
        db_ref[0, 0] = dbg
        db_ref[1, 0] = dbv

    both = lambda rows: pl.BlockSpec((2, 1, rows, LANES), lambda d, j: (0, d, 0, j))
    return pl.pallas_call(
        body, name="ffn_act_bwd", grid=(4, _NCB),
        in_specs=[_chan(0, T), _chan(1, T), pl.BlockSpec((1, T, LANES), lambda d, j: (d, 0, j)),
                  _chan(0, 3), _chan(1, 3), _chan(0, 1), _chan(1, 1)],
        out_specs=[both(T), both(3), both(1)],
        out_shape=[_sds((2, 4, T, CS), BF16), _sds((2, 4, 3, CS)), _sds((2, 4, 1, CS))],
        compiler_params=_cp(("parallel", "parallel")),
    )(up4, up4, dact, cw4, cw4, cb4, cb4)


def _ffn_down_loss(act, wd4, x1, target):
    tm = 512

    def body(a_ref, w_ref, x1_ref, t_ref, loss_ref, dx_ref, dxb_ref):
        x2 = x1_ref[...]
        for d in range(4):
            x2 = x2 + _nn(a_ref[d], w_ref[d])
        err = x2 - t_ref[...]
        dx = err * (1.0 / D)
        dx_ref[...] = dx
        dxb_ref[...] = dx.astype(BF16)

        @pl.when(pl.program_id(0) == 0)
        def _():
            loss_ref[...] = jnp.zeros_like(loss_ref)
        loss_ref[...] += 0.5 * jnp.sum(jnp.mean(err * err, axis=-1, keepdims=True), axis=0, keepdims=True)

    row = lambda i: (i, 0)
    fixed = lambda i: (0, 0)
    return pl.pallas_call(
        body, name="ffn_down_loss", grid=(T // tm,),
        in_specs=[pl.BlockSpec((4, tm, CS), lambda i: (0, i, 0)), pl.BlockSpec((4, CS, D), lambda i: (0, 0, 0)),
                  pl.BlockSpec((tm, D), row), pl.BlockSpec((tm, D), row)],
        out_specs=[pl.BlockSpec((1, 1), fixed), pl.BlockSpec((tm, D), row), pl.BlockSpec((tm, D), row)],
        out_shape=[_sds((1, 1)), _sds((T, D)), _sds((T, D), BF16)],
        compiler_params=_cp(("arbitrary",)),
    )(act, wd4, x1, target)


def _ffn_dact(dx2b, wd4):
    tm = 1024

    def body(d_ref, w_ref, o_ref):
        o_ref[0] = _nt(d_ref[...], w_ref[0])

    return pl.pallas_call(
        body, name="ffn_dact", grid=(4, T // tm),
        in_specs=[pl.BlockSpec((tm, D), lambda d, i: (i, 0)), pl.BlockSpec((1, CS, D), lambda d, i: (d, 0, 0))],
        out_specs=pl.BlockSpec((1, tm, CS), lambda d, i: (d, i, 0)), out_shape=_sds((4, T, CS)),
        compiler_params=_cp(("parallel", "parallel")),
    )(dx2b, wd4)


def _ffn_dwd(act, dx2b):
    def body(a_ref, d_ref, o_ref):
        o_ref[0] = _tn(a_ref[0], d_ref[...]).astype(BF16)

    return pl.pallas_call(
        body, name="ffn_dwd", grid=(4,),
        in_specs=[pl.BlockSpec((1, T, CS), lambda d: (d, 0, 0)), pl.BlockSpec((T, D), lambda d: (0, 0))],
        out_specs=pl.BlockSpec((1, CS, D), lambda d: (d, 0, 0)), out_shape=_sds((4, CS, D), BF16),
        compiler_params=_cp(("parallel",)),
    )(act, dx2b)


def _ffn_dwu(h2, dup4):
    tn = 1024

    def body(h_ref, d_ref, o_ref):
        o_ref[0, 0] = _tn(d_ref[0, 0], h_ref[...]).astype(BF16)

    return pl.pallas_call(
        body, name="ffn_dwu", grid=(NDEV, D // tn),
        in_specs=[pl.BlockSpec((T, tn), lambda e, i: (0, i)),
                  pl.BlockSpec((1, 1, T, CS), lambda e, i: (e // 4, e % 4, 0, 0))],
        out_specs=pl.BlockSpec((1, 1, CS, tn), lambda e, i: (e // 4, e % 4, 0, i)),
        out_shape=_sds((2, 4, CS, D), BF16),
        compiler_params=_cp(("parallel", "parallel")),
    )(h2, dup4)


def _ffn_up_bwd(dup4, wu4, x1, g, dres):
    tm = 256

    def body(d_ref, w_ref, x_ref, g_ref, dres_ref, dx_ref, dg_ref):
        dh = jnp.zeros((tm, D), F32)
        for gv in range(2):
            for d in range(4):
                dh = dh + _nn(d_ref[gv, d], w_ref[gv, d])
        xv = x_ref[...]
        r = _rms_r(xv)
        dxr, dgp = _rms_bwd(xv * r, r, g_ref[...], dh)
        dx_ref[...] = dres_ref[...] + dxr

        @pl.when(pl.program_id(0) == 0)
        def _():
            dg_ref[...] = jnp.zeros_like(dg_ref)
        dg_ref[...] += dgp

    row = lambda i: (i, 0)
    fixed = lambda i: (0, 0)
    return pl.pallas_call(
        body, name="ffn_up_bwd", grid=(T // tm,),
        in_specs=[pl.BlockSpec((2, 4, tm, CS), lambda i: (0, 0, i, 0)), pl.BlockSpec((2, 4, CS, D), lambda i: (0, 0, 0, 0)),
                  pl.BlockSpec((tm, D), row), pl.BlockSpec((1, D), fixed), pl.BlockSpec((tm, D), row)],
        out_specs=[pl.BlockSpec((tm, D), row), pl.BlockSpec((1, D), fixed)],
        out_shape=[_sds((T, D)), _sds((1, D))],
        compiler_params=_cp(("arbitrary",)),
    )(dup4, wu4, x1, g, dres)


def _adamw_math(w, g, m, v):
    m = ADAM_B1 * m + (1.0 - ADAM_B1) * g
    v = ADAM_B2 * v + (1.0 - ADAM_B2) * jnp.square(g)
    m_hat = m / (1.0 - ADAM_B1 ** ADAM_STEP)
    v_hat = v / (1.0 - ADAM_B2 ** ADAM_STEP)
    delta = -ADAM_LR * (m_hat / (jnp.sqrt(v_hat) + ADAM_EPS) + ADAM_WD * w)
    return delta, m, v


def _where_am_i():
    x, y, c = _place()
    return jnp.stack([c, 2 * x + y]).astype(jnp.int32)


def _pair_sum(a4, recv, name):
    _, _, rows, cols = a4.shape
    tr = _row_tile(rows, cols, 3 << 20)

    def body(sel_ref, own_ref, recv_ref, out_ref):
        out_ref[...] = (own_ref[0].astype(F32) + recv_ref[...].astype(F32)).astype(out_ref.dtype)

    grid_spec = pltpu.PrefetchScalarGridSpec(
        num_scalar_prefetch=1, grid=(4, rows // tr),
        in_specs=[pl.BlockSpec((1, 1, tr, cols), lambda k, i, s: (k, s[0], i, 0)),
                  pl.BlockSpec((1, tr, cols), lambda k, i, s: (k, i, 0))],
        out_specs=pl.BlockSpec((1, tr, cols), lambda k, i, s: (k, i, 0)),
    )
    return pl.pallas_call(
        body, name=name, grid_spec=grid_spec, out_shape=_sds((4, rows, cols), a4.dtype),
        compiler_params=_cp(("arbitrary", "arbitrary")),
    )(_where_am_i(), a4, recv)


def _carried_copies(kind, in_refs, out_refs, sems):
    if kind == "chip_swap":
        return _chip_swap_copies(in_refs, out_refs, *sems)
    if kind == "gather_first":
        local, remote = _gather_first_copies(in_refs, out_refs, *sems)
        return local + remote
    return _pass_on_copies(out_refs, *sems)


def _adamw_rs(a4, recv, from_chips, w, m, v, name, carry=None):
    rows, cols = w.shape
    tr = _row_tile(rows, cols, 3 << 19)
    steps = rows // tr
    kind, extra = carry if carry else (None, [])
    n = len(extra)

    def body(sel_ref, own_ref, recv_ref, fc_ref, w_ref, m_ref, v_ref, *rest):
        in_refs, (g_ref, d_ref, nm_ref, nv_ref) = rest[:n], rest[n:n + 4]
        out_refs, sems = rest[n + 4:2 * n + 4], rest[2 * n + 4:]

        if n:
            @pl.when(pl.program_id(0) == 0)
            def _():
                for cp in _carried_copies(kind, in_refs, out_refs, sems):
                    cp.start()

        g = own_ref[0, 0].astype(F32) + recv_ref[0].astype(F32)
        for j in range(3):
            g = g + fc_ref[j].astype(F32)
        d, nm, nv = _adamw_math(w_ref[...], g, m_ref[...], v_ref[...])
        g_ref[...] = g
        d_ref[...] = d
        nm_ref[...] = nm
        nv_ref[...] = nv

        if n:
            @pl.when(pl.program_id(0) == steps - 1)
            def _():
                for cp in _carried_copies(kind, in_refs, out_refs, sems):
                    cp.wait()

    if kind == "chip_swap":
        extra_shapes = [_sds((3, *p.shape[1:]), p.dtype) for p in extra]
    elif kind == "gather_first":
        extra_shapes = [_sds((NDEV, *s.shape), s.dtype) for s in extra]
    else:
        extra_shapes = [_sds(b.shape, b.dtype) for b in extra]
    if kind == "gather_first":
        sem_shapes = _gather_first_scratch(n)
    else:
        sem_shapes = [pltpu.SemaphoreType.DMA((3 * n,)), pltpu.SemaphoreType.DMA((3 * n,))] if n else []
    flat = pl.BlockSpec((tr, cols), lambda i, s: (i, 0))
    anyspec = pl.BlockSpec(memory_space=pl.ANY)
    grid_spec = pltpu.PrefetchScalarGridSpec(
        num_scalar_prefetch=1, grid=(steps,),
        in_specs=[pl.BlockSpec((1, 1, tr, cols), lambda i, s: (s[1], s[0], i, 0)),
                  pl.BlockSpec((1, tr, cols), lambda i, s: (s[1], i, 0)),
                  pl.BlockSpec((3, tr, cols), lambda i, s: (0, i, 0)), flat, flat, flat] + [anyspec] * n,
        out_specs=[flat] * 4 + [anyspec] * n,
        scratch_shapes=sem_shapes,
    )
    res = pl.pallas_call(
        body, name=name, grid_spec=grid_spec, out_shape=[_sds((rows, cols))] * 4 + extra_shapes,
        input_output_aliases={7 + i: 4 + i for i in range(n)} if kind == "pass_on" else {},
        compiler_params=_cp(("arbitrary",)),
    )(_where_am_i(), a4, recv, from_chips, w, m, v, *extra)
    return res[:4], res[4:]


_SMALL_F32 = (("g_mix", (8, 128)), ("b_f", (1, 8)), ("g_q", (1, 64)), ("g_k", (1, 64)), ("lambda_re", (32, 64)),
              ("lambda_im", (32, 64)), ("log_step", (1, 32)), ("d_skip", (32, 16)), ("b_glu", (4, 128)),
              ("g_attn_out", (4, 128)), ("g_ssm_out", (4, 128)), ("g_ffn", (8, 128)), ("conv_b", (44, 128)))
_SMALL_PAIRS = (("b_re", "b_im"), ("c_re", "c_im"))
_PAIR_SHAPE = (NG * GS, NP)


def _ceil8(r):
    return -(-r // 8) * 8


def _small_names():
    return [n for n, _ in _SMALL_F32] + [n for pair in _SMALL_PAIRS for n in pair]


def _pack_small_grads(g, loss_dev):
    parts = [jnp.pad(g[n].reshape(r, c), ((0, _ceil8(r) - r), (0, LANES - c))) for n, (r, c) in _SMALL_F32]
    parts.append(jnp.pad(loss_dev, ((0, 7), (0, LANES - 1))))
    pairs = [jnp.concatenate([g[a].reshape(_PAIR_SHAPE), g[b].reshape(_PAIR_SHAPE)], axis=1) for a, b in _SMALL_PAIRS]
    return jnp.concatenate(parts, axis=0), jnp.concatenate(pairs, axis=0).astype(BF16)


def _adamw_small(gall, gall_b, wmv):
    names = _small_names()
    shapes = [s for _, s in _SMALL_F32] + [_PAIR_SHAPE] * (2 * len(_SMALL_PAIRS))
    npar = len(names)

    def body(ga_ref, gb_ref, *rest):
        ins, loss_ref, outs = rest[:3 * npar], rest[3 * npar], rest[3 * npar + 1:]
        ga, gb = ga_ref[0], gb_ref[0].astype(F32)
        for dev in range(1, NDEV):
            ga = ga + ga_ref[dev]
            gb = gb + gb_ref[dev].astype(F32)
        grads, off = [], 0
        for _, (r, c) in _SMALL_F32:
            grads.append(ga[off:off + r, 0:c])
            off += _ceil8(r)
        loss_ref[...] = ga[off:off + 8]
        rows, cols = _PAIR_SHAPE
        for j in range(len(_SMALL_PAIRS)):
            for h in range(2):
                grads.append(gb[rows * j:rows * (j + 1), cols * h:cols * (h + 1)])
        for i, g in enumerate(grads):
            d, nm, nv = _adamw_math(ins[3 * i][...], g, ins[3 * i + 1][...], ins[3 * i + 2][...])
            outs[4 * i][...] = g
            outs[4 * i + 1][...] = d
            outs[4 * i + 2][...] = nm
            outs[4 * i + 3][...] = nv

    flat = [a for n in names for a in wmv[n]]
    out_shape = [_sds((8, LANES))] + [_sds(s) for s in shapes for _ in range(4)]
    res = pl.pallas_call(body, name="adamw_small", out_shape=out_shape, compiler_params=_cp())(gall, gall_b, *flat)
    return res[0], {n: res[1 + 4 * i:5 + 4 * i] for i, n in enumerate(names)}


def _block_diag(m):
    eye = jnp.eye(8, dtype=m.dtype)
    r, c = m.shape[2], m.shape[3]
    return (m[:, :, :, None, :] * eye[None, :, None, :, None]).reshape(4, 8 * r, 8 * c)


def _diag_blocks(dense, r, c):
    eye = jnp.eye(8, dtype=dense.dtype)
    return jnp.sum(dense.reshape(4, 8, r, 8, c) * eye[None, :, None, :, None], axis=3)


def kernel(x, g_mix, w_in, b_f, g_q, g_k, lambda_re, lambda_im, log_step, b_re, b_im, c_re, c_im, d_skip, w_glu, b_glu, g_attn_out, g_ssm_out, w_out, g_ffn, w_up, conv_w, conv_b, w_down, loss_target, m_g_mix, m_w_in, m_b_f, m_g_q, m_g_k, m_lambda_re, m_lambda_im, m_log_step, m_b_re, m_b_im, m_c_re, m_c_im, m_d_skip, m_w_glu, m_b_glu, m_g_attn_out, m_g_ssm_out, m_w_out, m_g_ffn, m_w_up, m_conv_w, m_conv_b, m_w_down, v_g_mix, v_w_in, v_b_f, v_g_q, v_g_k, v_lambda_re, v_lambda_im, v_log_step, v_b_re, v_b_im, v_c_re, v_c_im, v_d_skip, v_w_glu, v_b_glu, v_g_attn_out, v_g_ssm_out, v_w_out, v_g_ffn, v_w_up, v_conv_w, v_conv_b, v_w_down):
    weights = dict(g_mix=g_mix, w_in=w_in, b_f=b_f, g_q=g_q, g_k=g_k, lambda_re=lambda_re, lambda_im=lambda_im,
                   log_step=log_step, b_re=b_re, b_im=b_im, c_re=c_re, c_im=c_im, d_skip=d_skip, w_glu=w_glu,
                   b_glu=b_glu, g_attn_out=g_attn_out, g_ssm_out=g_ssm_out, w_out=w_out, g_ffn=g_ffn, w_up=w_up,
                   conv_w=conv_w, conv_b=conv_b, w_down=w_down)
    mom_m = dict(g_mix=m_g_mix, w_in=m_w_in, b_f=m_b_f, g_q=m_g_q, g_k=m_g_k, lambda_re=m_lambda_re,
                 lambda_im=m_lambda_im, log_step=m_log_step, b_re=m_b_re, b_im=m_b_im, c_re=m_c_re, c_im=m_c_im,
                 d_skip=m_d_skip, w_glu=m_w_glu, b_glu=m_b_glu, g_attn_out=m_g_attn_out, g_ssm_out=m_g_ssm_out,
                 w_out=m_w_out, g_ffn=m_g_ffn, w_up=m_w_up, conv_w=m_conv_w, conv_b=m_conv_b, w_down=m_w_down)
    mom_v = dict(g_mix=v_g_mix, w_in=v_w_in, b_f=v_b_f, g_q=v_g_q, g_k=v_g_k, lambda_re=v_lambda_re,
                 lambda_im=v_lambda_im, log_step=v_log_step, b_re=v_b_re, b_im=v_b_im, c_re=v_c_re, c_im=v_c_im,
                 d_skip=v_d_skip, w_glu=v_w_glu, b_glu=v_b_glu, g_attn_out=v_g_attn_out, g_ssm_out=v_g_ssm_out,
                 w_out=v_w_out, g_ffn=v_g_ffn, w_up=v_w_up, conv_w=v_conv_w, conv_b=v_conv_b, w_down=v_w_down)
    names = list(weights)
    big = ("w_in", "w_glu", "w_out", "w_up", "w_down", "conv_w")

    xs = x[0]
    target = loss_target[0]
    row = lambda a: a.reshape(1, -1)

    (win_g,) = _all_gather([w_in.T.astype(BF16)], "gather_w_in")
    w_in_t = win_g.reshape(8 * 257, D)
    wcat = jnp.concatenate([w_in_t[0:1536], w_in_t[1544:2056], jnp.pad(w_in_t[1536:1544], ((0, LANES - NH), (0, 0)))],
                           axis=0)
    cb4 = conv_b.reshape(2, 4, 1, CS)

    h1, z, f_t, (wg_g, wo_g) = _in_proj(xs, row(g_mix), wcat, [w_glu.astype(BF16), w_out.astype(BF16)])
    bf_col = b_f.reshape(NH, 1)
    ck = _gates_fwd(f_t, bf_col).reshape(_HP, 2, T)
    att, lse, (wu_g, cw_g), (wg_g, wo_g) = _attn_fwd(z, ck, row(g_q), row(g_k), [w_up.T.astype(BF16), conv_w],
                                                       [wg_g, wo_g])

    ls_col = log_step.reshape(NG, 1)
    br_t, bi_t = b_re.transpose(0, 2, 1), b_im.transpose(0, 2, 1)
    ab_re, ab_im, bb_re, bb_im = _disc_fwd(lambda_re, lambda_im, ls_col, br_t, bi_t)
    bb = jnp.concatenate([_block_diag(bb_re.reshape(4, 8, GS, NP)), _block_diag(bb_im.reshape(4, 8, GS, NP))],
                         axis=2).astype(BF16)
    ab = jnp.stack([ab_re.reshape(4, 512), ab_im.reshape(4, 512)], axis=1)
    cdiag = lambda cm: _block_diag(cm.reshape(4, 8, GS, NP).transpose(0, 1, 3, 2))
    cc = jnp.concatenate([cdiag(c_re), -cdiag(c_im)], axis=1).astype(BF16)
    dsk = d_skip.reshape(4, 1, LANES)
    u_il = _interleave(z[:, 1536:2048])
    y_il, xs_il, (wd_g,) = _ssm_fwd(u_il, bb, ab, cc, dsk, [w_down.astype(BF16)])
    y = _deinterleave(y_il)
    wg_f = wg_g.reshape(SW, SW)
    wo_f = wo_g.reshape(D, D)
    x1, (wu_g, cw_g, wd_g) = _mix_fwd(att, y, xs, wg_f, row(b_glu), row(g_attn_out), row(g_ssm_out), wo_f,
                                      [wu_g, cw_g, wd_g])
    wu4 = wu_g.reshape(2, 4, CS, D)
    wd4 = wd_g.reshape(4, CS, D)
    cw4 = cw_g.reshape(2, 4, 3, CS)
    h2, up4 = _ffn_up(x1, row(g_ffn), wu4)
    act = _ffn_act(up4, cw4, cb4)
    loss_dev, dx2, dx2b = _ffn_down_loss(act, wd4, x1, target)

    dact = _ffn_dact(dx2b, wd4)
    d_wd = _ffn_dwd(act, dx2b)
    dup4, dcw4, dcb4 = _ffn_act_bwd(up4, dact, cw4, cb4)
    d_wu = _ffn_dwu(h2, dup4)
    dx1, dg_ffn = _ffn_up_bwd(dup4, wu4, x1, row(g_ffn), dx2)
    datt, dy, d_wg, d_bg, d_ga, d_gs, d_wo = _mix_bwd(att, y, dx1, wg_f, row(b_glu), row(g_attn_out),
                                                       row(g_ssm_out), wo_f)
    early = ("w_up", "w_down", "conv_w", "w_out", "w_glu")
    by_dest = {"w_up": d_wu.reshape(4, 2, CS, D), "w_down": d_wd.reshape(4, 2, DFF // NDEV, D),
               "conv_w": dcw4.reshape(4, 2, 3, CS), "w_out": d_wo.reshape(4, 2, D // NDEV, D),
               "w_glu": d_wg.reshape(4, 2, SW // NDEV, SW)}
    du_il, dbb, dab, dcc, ddsk, fs = _ssm_bwd(u_il, _interleave(dy), xs_il, bb, ab, cc, dsk, [by_dest[n] for n in early])
    from_sibling = dict(zip(early, fs))
    chip_sums = {n: _pair_sum(by_dest[n], from_sibling[n], "rs_pair_sum_" + n) for n in early}
    du = _deinterleave(du_il).astype(BF16)
    dbb_re = _diag_blocks(dbb[:, :, 0:512], GS, NP).reshape(NG, GS, NP)
    dbb_im = _diag_blocks(dbb[:, :, 512:1024], GS, NP).reshape(NG, GS, NP)
    dlr, dli, dls, dbr_t, dbi_t = _disc_bwd(lambda_re, lambda_im, ls_col, br_t, bi_t,
                                            dab[:, 0].reshape(NG, NP), dab[:, 1].reshape(NG, NP), dbb_re, dbb_im)
    d_cre = _diag_blocks(dcc[:, 0:512], NP, GS).transpose(0, 1, 3, 2).reshape(NG, GS, NP)
    d_cim = -_diag_blocks(dcc[:, 512:1024], NP, GS).transpose(0, 1, 3, 2).reshape(NG, GS, NP)

    dq, dk, dv, dck, d_gq, d_gk, fc = _attn_bwd(z, ck, row(g_q), row(g_k), lse, datt, [chip_sums[n] for n in early])
    from_chips = dict(zip(early, fc))
    df_t, d_bf = _gates_bwd(dck.reshape(NH, T), f_t, bf_col)
    df = jnp.concatenate([df_t.T, jnp.zeros((T, LANES - NH), F32)], axis=1).astype(BF16)
    d_wcat = _in_proj_dw(h1, dq, dk, dv, du, df)
    d_win = jnp.concatenate([d_wcat[0:1536], d_wcat[2048:2048 + NH], d_wcat[1536:2048]], axis=0)
    by_dest["w_in"] = d_win.astype(BF16).reshape(4, 2, 257, D)
    grad_x, dg_mix = _in_proj_bwd(dq, dk, dv, du, df, wcat, xs, row(g_mix), dx1)
    (from_sibling["w_in"],) = _swap_sibling([by_dest["w_in"]], "rs_pair_swap_w_in")
    chip_sums["w_in"] = _pair_sum(by_dest["w_in"], from_sibling["w_in"], "rs_pair_sum_w_in")

    small = dict(g_mix=dg_mix, b_f=d_bf, g_q=d_gq, g_k=d_gk, lambda_re=dlr, lambda_im=dli, log_step=dls,
                 b_re=dbr_t, b_im=dbi_t, c_re=d_cre, c_im=d_cim, d_skip=ddsk, b_glu=d_bg, g_attn_out=d_ga,
                 g_ssm_out=d_gs, g_ffn=dg_ffn, conv_b=dcb4)
    carried = {"w_up": ("chip_swap", [chip_sums["w_in"]]), "w_down": ("gather_first", list(_pack_small_grads(small, loss_dev)))}
    grads, deltas, new_m, new_v = {}, {}, {}, {}
    for n in early + ("w_in",):
        flip = (lambda a: a.T) if n in ("w_up", "w_in") else (lambda a: a)
        outs, extra = _adamw_rs(by_dest[n], from_sibling[n], from_chips[n], flip(weights[n]), flip(mom_m[n]),
                                flip(mom_v[n]), "adamw_" + n, carried.get(n))
        grads[n], deltas[n], new_m[n], new_v[n] = (flip(o) for o in outs)
        if n == "w_up":
            (from_chips["w_in"],) = extra
        elif n == "w_down":
            carried["w_out"] = ("pass_on", list(extra))
        elif n == "w_out":
            small_all, small_all_b = extra
    views = dict(_SMALL_F32)
    swapped = ("b_re", "b_im")

    def view(n, a):
        return a.transpose(0, 2, 1).reshape(_PAIR_SHAPE) if n in swapped else a.reshape(views.get(n, _PAIR_SHAPE))

    loss_p, small_out = _adamw_small(small_all, small_all_b,
                                     {n: (view(n, weights[n]), view(n, mom_m[n]), view(n, mom_v[n])) for n in small})
    loss = loss_p[0, 0]
    for n in small:
        back = (lambda a: a.reshape(NG, GS, NP).transpose(0, 2, 1)) if n in swapped else (lambda a: a.reshape(weights[n].shape))
        grads[n], deltas[n], new_m[n], new_v[n] = (back(o) for o in small_out[n])

    return (loss, grad_x[None], *[grads[n] for n in names], *[deltas[n] for n in names],
            *[new_m[n] for n in names], *[new_v[n] for n in names])
```

```python
import jax
import jax.numpy as jnp
from jax import lax
from jax.experimental import pallas as pl
from jax.experimental.pallas import tpu as pltpu

F32, BF16 = jnp.float32, jnp.bfloat16
T, D = 2048, 1024
NH, DH = 8, 64
AW, SW = 512, 512
NG, GS, NP = 32, 16, 64
DFF = 2816
NDEV = 8
CS = 2 * DFF // NDEV
EPS = 1e-6
NEG = -1e30
ZC = 4 * 512 + 128
SEG = 256
NSEG = T // SEG
BQ = 512
VMEM_LIMIT = 56 * 1024 * 1024
LANES = 128
MESH = pl.DeviceIdType.MESH
HI = lax.Precision.HIGHEST

ADAM_LR, ADAM_B1, ADAM_B2, ADAM_EPS, ADAM_WD, ADAM_STEP = 0.001, 0.9, 0.999, 1e-08, 0.01, 10


def _cp(dims=None):
    if dims is None:
        return pltpu.CompilerParams(vmem_limit_bytes=VMEM_LIMIT)
    return pltpu.CompilerParams(dimension_semantics=dims, vmem_limit_bytes=VMEM_LIMIT)


def _sds(shape, dtype=F32):
    return jax.ShapeDtypeStruct(shape, dtype)


def _nt(a, b):
    return lax.dot_general(a, b, (((1,), (1,)), ((), ())), preferred_element_type=F32)


def _tn(a, b):
    return lax.dot_general(a, b, (((0,), (0,)), ((), ())), preferred_element_type=F32)


def _nn(a, b):
    return jnp.dot(a, b, preferred_element_type=F32)


def _rms_r(x):
    return lax.rsqrt(jnp.mean(x * x, axis=-1, keepdims=True) + EPS)


def _rms_bwd(n, r, g, dh):
    dn = dh * g
    dx = r * (dn - n * jnp.mean(dn * n, axis=-1, keepdims=True))
    return dx, jnp.sum(dh * n, axis=0, keepdims=True)


def _row_tile(rows, cols, limit):
    tr = rows
    while tr * cols * 4 > limit and tr % 32 == 0:
        tr //= 2
    return tr


def _place():
    return lax.axis_index("x"), lax.axis_index("y"), lax.axis_index("c")


def _all_gather(shards, name):
    n = len(shards)

    def body(*refs):
        x_refs, out_refs = refs[:n], refs[n:2 * n]
        send_sems, recv_sems, local_sems = refs[2 * n:]
        x, y, c = _place()
        me, sibling = (x, y, c), (x, y, 1 - c)
        chips = [(1 - x, y), (x, 1 - y), (1 - x, 1 - y)]

        def copy(a, k, block, to, src=None):
            px, py, pc = block
            slot = out_refs[a].at[4 * px + 2 * py + pc]
            return pltpu.make_async_remote_copy(
                src_ref=slot if src is None else src, dst_ref=slot,
                send_sem=send_sems.at[7 * a + k], recv_sem=recv_sems.at[7 * a + k], device_id=to, device_id_type=MESH)

        mine, first, passed = [], [], []
        for a in range(n):
            cp = pltpu.make_async_copy(x_refs[a], out_refs[a].at[4 * x + 2 * y + c], local_sems.at[a])
            cp.start()
            mine.append(cp)
            cps = [copy(a, 0, me, sibling, src=x_refs[a])]
            cps += [copy(a, 1 + j, me, (*chip, c), src=x_refs[a]) for j, chip in enumerate(chips)]
            for cp in cps:
                cp.start()
            first += cps
        for a in range(n):
            for j, chip in enumerate(chips):
                copy(a, 1 + j, (*chip, c), me).wait_recv()
                fwd = copy(a, 4 + j, (*chip, c), sibling)
                fwd.start()
                passed.append(fwd)
        for a in range(n):
            copy(a, 0, sibling, me).wait_recv()
            for j, chip in enumerate(chips):
                copy(a, 4 + j, (*chip, 1 - c), me).wait_recv()
        for cp in first + passed:
            cp.wait_send()
        for cp in mine:
            cp.wait()

    anyspec = pl.BlockSpec(memory_space=pl.ANY)
    return pl.pallas_call(
        body, name=name,
        out_shape=[_sds((NDEV, *s.shape), s.dtype) for s in shards],
        in_specs=[anyspec] * n, out_specs=[anyspec] * n,
        scratch_shapes=[pltpu.SemaphoreType.DMA((7 * n,)), pltpu.SemaphoreType.DMA((7 * n,)),
                        pltpu.SemaphoreType.DMA((n,))],
    )(*shards)


def _gather_first_copies(x_refs, out_refs, send_sems, recv_sems, local_sems):
    x, y, c = _place()
    slot = 4 * x + 2 * y + c
    peers = [(x, y, 1 - c), (1 - x, y, c), (x, 1 - y, c), (1 - x, 1 - y, c)]
    local, remote = [], []
    for a, (x_ref, out_ref) in enumerate(zip(x_refs, out_refs)):
        local.append(pltpu.make_async_copy(x_ref, out_ref.at[slot], local_sems.at[a]))
        for k, peer in enumerate(peers):
            remote.append(pltpu.make_async_remote_copy(
                src_ref=x_ref, dst_ref=out_ref.at[slot], send_sem=send_sems.at[4 * a + k],
                recv_sem=recv_sems.at[4 * a + k], device_id=peer, device_id_type=MESH))
    return local, remote


def _gather_first_scratch(n):
    return [pltpu.SemaphoreType.DMA((4 * n,)), pltpu.SemaphoreType.DMA((4 * n,)), pltpu.SemaphoreType.DMA((n,))]


def _chip_swap_copies(p_refs, out_refs, send_sems, recv_sems):
    x, y, c = _place()
    chips = [(1 - x, y), (x, 1 - y), (1 - x, 1 - y)]
    cps = []
    for a, (p_ref, out_ref) in enumerate(zip(p_refs, out_refs)):
        for j, (px, py) in enumerate(chips):
            cps.append(pltpu.make_async_remote_copy(
                src_ref=p_ref.at[2 * px + py], dst_ref=out_ref.at[j],
                send_sem=send_sems.at[3 * a + j], recv_sem=recv_sems.at[3 * a + j],
                device_id=(px, py, c), device_id_type=MESH))
    return cps


def _sibling_swap_copies(a_refs, out_refs, send_sems, recv_sems):
    x, y, c = _place()
    cps = []
    for a, (a_ref, out_ref) in enumerate(zip(a_refs, out_refs)):
        for k in range(4):
            cps.append(pltpu.make_async_remote_copy(
                src_ref=a_ref.at[k, 1 - c], dst_ref=out_ref.at[k],
                send_sem=send_sems.at[4 * a + k], recv_sem=recv_sems.at[4 * a + k],
                device_id=(x, y, 1 - c), device_id_type=MESH))
    return cps


def _pass_on_copies(buf_refs, send_sems, recv_sems):
    x, y, c = _place()
    chips = [(1 - x, y), (x, 1 - y), (1 - x, 1 - y)]
    cps = []
    for a, buf in enumerate(buf_refs):
        for j, (px, py) in enumerate(chips):
            block = buf.at[4 * px + 2 * py + c]
            cps.append(pltpu.make_async_remote_copy(
                src_ref=block, dst_ref=block, send_sem=send_sems.at[3 * a + j], recv_sem=recv_sems.at[3 * a + j],
                device_id=(x, y, 1 - c), device_id_type=MESH))
    return cps


def _swap_sibling(arrs, name):
    n = len(arrs)

    def body(*refs):
        cps = _sibling_swap_copies(refs[:n], refs[n:2 * n], *refs[2 * n:])
        for cp in cps:
            cp.start()
        for cp in cps:
            cp.wait()

    anyspec = pl.BlockSpec(memory_space=pl.ANY)
    return pl.pallas_call(
        body, name=name,
        out_shape=[_sds((4, *a.shape[2:]), a.dtype) for a in arrs],
        in_specs=[anyspec] * n, out_specs=[anyspec] * n,
        scratch_shapes=[pltpu.SemaphoreType.DMA((4 * n,)), pltpu.SemaphoreType.DMA((4 * n,))],
    )(*arrs)


def _in_proj(x, g, wcat, shards):
    tm = 256
    n = len(shards)

    def body(x_ref, g_ref, w_ref, *rest):
        x_refs, (h_ref, z_ref, ft_ref), out_refs, sems = rest[:n], rest[n:n + 3], rest[n + 3:2 * n + 3], rest[2 * n + 3:]

        @pl.when(pl.program_id(0) == 0)
        def _():
            local, remote = _gather_first_copies(x_refs, out_refs, *sems)
            for cp in local + remote:
                cp.start()

        xv = x_ref[...]
        h = (xv * _rms_r(xv) * g_ref[...]).astype(BF16)
        h_ref[...] = h
        z = _nt(h, w_ref[...])
        z_ref[...] = z
        ft_ref[...] = z[:, 2048:ZC].T[0:NH, :]

        @pl.when(pl.program_id(0) == T // tm - 1)
        def _():
            local, remote = _gather_first_copies(x_refs, out_refs, *sems)
            for cp in remote + local:
                cp.wait()

    row = lambda i: (i, 0)
    fixed = lambda i: (0, 0)
    anyspec = pl.BlockSpec(memory_space=pl.ANY)
    res = pl.pallas_call(
        body, name="in_proj", grid=(T // tm,),
        in_specs=[pl.BlockSpec((tm, D), row), pl.BlockSpec((1, D), fixed), pl.BlockSpec((ZC, D), fixed)] + [anyspec] * n,
        out_specs=[pl.BlockSpec((tm, D), row), pl.BlockSpec((tm, ZC), row), pl.BlockSpec((NH, tm), lambda i: (0, i))]
        + [anyspec] * n,
        out_shape=[_sds((T, D), BF16), _sds((T, ZC)), _sds((NH, T))] + [_sds((NDEV, *s.shape), s.dtype) for s in shards],
        scratch_shapes=_gather_first_scratch(n),
        compiler_params=_cp(("arbitrary",)),
    )(x, g, wcat, *shards)
    return res[0], res[1], res[2], res[3:]


def _in_proj_dw(h, dq, dk, dv, du, df):
    tm = 512

    def body(h_ref, dq_ref, dk_ref, dv_ref, du_ref, df_ref, o_ref):
        hv = h_ref[...]
        for j, d_ref in enumerate((dq_ref, dk_ref, dv_ref, du_ref)):
            o_ref[512 * j:512 * (j + 1), :] = _tn(d_ref[...], hv)
        o_ref[2048:ZC, :] = _tn(df_ref[...], hv)

    full = lambda w: pl.BlockSpec((T, w), lambda i: (0, 0))
    return pl.pallas_call(
        body, name="in_proj_dw", grid=(D // tm,),
        in_specs=[pl.BlockSpec((T, tm), lambda i: (0, i)), full(512), full(512), full(512), full(512), full(LANES)],
        out_specs=pl.BlockSpec((ZC, tm), lambda i: (0, i)), out_shape=_sds((ZC, D)),
        compiler_params=_cp(("parallel",)),
    )(h, dq, dk, dv, du, df)


def _in_proj_bwd(dq, dk, dv, du, df, wcat, x, g, dres, parts):
    tm = 256
    n = len(parts)

    def body(dq_ref, dk_ref, dv_ref, du_ref, df_ref, w_ref, x_ref, g_ref, dres_ref, *rest):
        p_refs, (dx_ref, dg_ref), fc_refs, sems = rest[:n], rest[n:n + 2], rest[n + 2:2 * n + 2], rest[2 * n + 2:]

        @pl.when(pl.program_id(0) == 0)
        def _():
            for cp in _chip_swap_copies(p_refs, fc_refs, *sems):
                cp.start()

        dh = _nn(df_ref[...], w_ref[2048:ZC, :])
        for j, d_ref in enumerate((dq_ref, dk_ref, dv_ref, du_ref)):
            dh = dh + _nn(d_ref[...], w_ref[512 * j:512 * (j + 1), :])
        xv = x_ref[...]
        r = _rms_r(xv)
        dxr, dgp = _rms_bwd(xv * r, r, g_ref[...], dh)
        dx_ref[...] = dres_ref[...] + dxr

        @pl.when(pl.program_id(0) == 0)
        def _():
            dg_ref[...] = jnp.zeros_like(dg_ref)
        dg_ref[...] += dgp

        @pl.when(pl.program_id(0) == T // tm - 1)
        def _():
            for cp in _chip_swap_copies(p_refs, fc_refs, *sems):
                cp.wait()

    row = lambda i: (i, 0)
    fixed = lambda i: (0, 0)
    part = pl.BlockSpec((tm, 512), row)
    anyspec = pl.BlockSpec(memory_space=pl.ANY)
    res = pl.pallas_call(
        body, name="in_proj_bwd", grid=(T // tm,),
        in_specs=[part, part, part, part, pl.BlockSpec((tm, LANES), row), pl.BlockSpec((ZC, D), fixed),
                  pl.BlockSpec((tm, D), row), pl.BlockSpec((1, D), fixed), pl.BlockSpec((tm, D), row)] + [anyspec] * n,
        out_specs=[pl.BlockSpec((tm, D), row), pl.BlockSpec((1, D), fixed)] + [anyspec] * n,
        out_shape=[_sds((T, D)), _sds((1, D))] + [_sds((3, *p.shape[1:]), p.dtype) for p in parts],
        scratch_shapes=[pltpu.SemaphoreType.DMA((3 * n,)), pltpu.SemaphoreType.DMA((3 * n,))],
        compiler_params=_cp(("arbitrary",)),
    )(dq, dk, dv, du, df, wcat, x, g, dres, *parts)
    return res[0], res[1], res[2:]


def _stack_lanes(v):
    return jnp.concatenate([v[:, LANES * b:LANES * (b + 1)] for b in range(T // LANES)], axis=0)


def _unstack_lanes(s):
    return jnp.concatenate([s[8 * b:8 * b + 8, :] for b in range(T // LANES)], axis=1)


def _cumsum_lanes(v, reverse):
    st = _stack_lanes(v)
    ri = lax.broadcasted_iota(jnp.int32, (LANES, LANES), 0)
    ci = lax.broadcasted_iota(jnp.int32, (LANES, LANES), 1)
    tri = (ri >= ci) if reverse else (ri <= ci)
    intra = jnp.dot(st, tri.astype(F32), precision=HI, preferred_element_type=F32)
    tot = intra[:, 0:1] if reverse else intra[:, LANES - 1:LANES]
    same_head = (ci % 8) == (ri % 8)
    other = (ci // 8 > ri // 8) if reverse else (ci // 8 < ri // 8)
    off = jnp.dot((same_head & other).astype(F32), jnp.broadcast_to(tot, (LANES, LANES)), precision=HI,
                  preferred_element_type=F32)
    return _unstack_lanes(intra + off)


def _gates_fwd(f_t, b_f):
    def body(f_ref, b_ref, c_ref):
        z = f_ref[...] + b_ref[...]
        lf = jnp.minimum(z, 0.0) - jnp.log(1.0 + jnp.exp(-jnp.abs(z)))
        c_ref[...] = _cumsum_lanes(lf, reverse=False)

    return pl.pallas_call(body, name="gates_fwd", out_shape=_sds((NH, T)), compiler_params=_cp())(f_t, b_f)


def _gates_bwd(dck, f_t, b_f):
    def body(d_ref, f_ref, b_ref, df_ref, db_ref):
        z = f_ref[...] + b_ref[...]
        dlf = _cumsum_lanes(d_ref[...], reverse=True)
        df = dlf * jax.nn.sigmoid(-z)
        df_ref[...] = df
        db_ref[...] = jnp.sum(df, axis=1, keepdims=True)

    return pl.pallas_call(body, name="gates_bwd", out_shape=[_sds((NH, T)), _sds((NH, 1))],
                          compiler_params=_cp())(dck, f_t, b_f)


def _lower_triangle():
    rows = lax.broadcasted_iota(jnp.int32, (BQ, BQ), 0)
    cols = lax.broadcasted_iota(jnp.int32, (BQ, BQ), 1)
    return cols <= rows


def _logit_parts(qb, kb, ck_row, lo, hi, tri):
    parts = []
    if lo > 0:
        parts.append((_nt(qb[lo:hi], kb[0:lo]) - ck_row[:, 0:lo], 0, lo))
    parts.append((jnp.where(tri, _nt(qb[lo:hi], kb[lo:hi]) - ck_row[:, lo:hi], NEG), lo, hi))
    return parts


_HP = NH // 2


def _qkv_specs():
    return [pl.BlockSpec((T, LANES), lambda p: (0, p)), pl.BlockSpec((T, LANES), lambda p: (0, _HP + p)),
            pl.BlockSpec((T, LANES), lambda p: (0, 2 * _HP + p))]


def _attn_fwd(z, ck, g_q, g_k, shards, bufs):
    scale = DH ** -0.5
    n, m = len(shards), len(bufs)

    def body(q_ref, k_ref, v_ref, ck_ref, gq_ref, gk_ref, *rest):
        x_refs, (o_ref, lse_ref) = rest[:n], rest[n + m:n + m + 2]
        out_refs, buf_refs, sems = rest[n + m + 2:2 * n + m + 2], rest[2 * n + m + 2:2 * n + 2 * m + 2], rest[2 * n + 2 * m + 2:]

        @pl.when(pl.program_id(0) == 0)
        def _():
            local, remote = _gather_first_copies(x_refs, out_refs, *sems[:3])
            for cp in local + remote + _pass_on_copies(buf_refs, *sems[3:]):
                cp.start()

        qb, kb, vb = [], [], []
        for hh in range(2):
            cols = slice(DH * hh, DH * (hh + 1))
            qv, kv = q_ref[:, cols], k_ref[:, cols]
            qb.append((qv * _rms_r(qv) * gq_ref[...] * scale).astype(BF16))
            kb.append((kv * _rms_r(kv) * gk_ref[...]).astype(BF16))
            vb.append(v_ref[:, cols].astype(BF16))
        tri = _lower_triangle()
        for i in range(T // BQ):
            lo, hi = i * BQ, (i + 1) * BQ
            outs = []
            for hh in range(2):
                parts = _logit_parts(qb[hh], kb[hh], ck_ref[0, hh:hh + 1, :], lo, hi, tri)
                top = jnp.max(parts[-1][0], axis=-1, keepdims=True)
                if lo > 0:
                    top = jnp.maximum(top, jnp.max(parts[0][0], axis=-1, keepdims=True))
                l, o = 0.0, 0.0
                for s, k0, k1 in parts:
                    p = jnp.exp(s - top)
                    l = l + jnp.sum(p, axis=-1, keepdims=True)
                    o = o + _nn(p.astype(BF16), vb[hh][k0:k1])
                outs.append(o / l)
                lse_ref[0, lo:hi, hh:hh + 1] = top + jnp.log(l)
            o_ref[lo:hi, :] = jnp.concatenate(outs, axis=1)

        @pl.when(pl.program_id(0) == _HP - 1)
        def _():
            local, remote = _gather_first_copies(x_refs, out_refs, *sems[:3])
            for cp in remote + local + _pass_on_copies(buf_refs, *sems[3:]):
                cp.wait()

    fixed = lambda p: (0, 0)
    anyspec = pl.BlockSpec(memory_space=pl.ANY)
    res = pl.pallas_call(
        body, name="attn_fwd", grid=(_HP,),
        in_specs=_qkv_specs() + [pl.BlockSpec((1, 2, T), lambda p: (p, 0, 0)), pl.BlockSpec((1, DH), fixed),
                                 pl.BlockSpec((1, DH), fixed)] + [anyspec] * (n + m),
        out_specs=[pl.BlockSpec((T, LANES), lambda p: (0, p)), pl.BlockSpec((1, T, 2), lambda p: (p, 0, 0))]
        + [anyspec] * (n + m),
        out_shape=[_sds((T, AW)), _sds((_HP, T, 2))] + [_sds((NDEV, *s.shape), s.dtype) for s in shards]
        + [_sds(b.shape, b.dtype) for b in bufs],
        input_output_aliases={6 + n + i: 2 + n + i for i in range(m)},
        scratch_shapes=_gather_first_scratch(n) + [pltpu.SemaphoreType.DMA((3 * m,)), pltpu.SemaphoreType.DMA((3 * m,))],
        compiler_params=_cp(("arbitrary",)),
    )(z, z, z, ck, g_q, g_k, *shards, *bufs)
    return res[0], res[1], res[2:2 + n], res[2 + n:]


def _attn_bwd(z, ck, g_q, g_k, lse, datt, parts):
    scale = DH ** -0.5
    n = len(parts)

    def body(q_ref, k_ref, v_ref, ck_ref, gq_ref, gk_ref, lse_ref, do_ref, *rest):
        p_refs, rest = rest[:n], rest[n:]
        dq_ref, dk_ref, dv_ref, dck_ref, dgq_ref, dgk_ref = rest[:6]
        fc_refs, (dkn_acc, dv_acc, dc_acc), sems = rest[6:6 + n], rest[6 + n:9 + n], rest[9 + n:]

        @pl.when(pl.program_id(0) == 0)
        def _():
            for cp in _chip_swap_copies(p_refs, fc_refs, *sems):
                cp.start()

        nq, nk, rq, rk, qb, kb, vb, dob = [], [], [], [], [], [], [], []
        for hh in range(2):
            cols = slice(DH * hh, DH * (hh + 1))
            qv, kv = q_ref[:, cols], k_ref[:, cols]
            rq.append(_rms_r(qv))
            rk.append(_rms_r(kv))
            nq.append(qv * rq[hh])
            nk.append(kv * rk[hh])
            qb.append((nq[hh] * gq_ref[...] * scale).astype(BF16))
            kb.append((nk[hh] * gk_ref[...]).astype(BF16))
            vb.append(v_ref[:, cols].astype(BF16))
            dob.append(do_ref[:, cols].astype(BF16))
        dkn_acc[...] = jnp.zeros_like(dkn_acc)
        dv_acc[...] = jnp.zeros_like(dv_acc)
        dc_acc[...] = jnp.zeros_like(dc_acc)
        dgq = jnp.zeros((1, DH), F32)
        tri = _lower_triangle()
        for i in range(T // BQ):
            lo, hi = i * BQ, (i + 1) * BQ
            dqs = []
            for hh in range(2):
                lse = lse_ref[0, lo:hi, hh:hh + 1]
                pd = []
                for s, k0, k1 in _logit_parts(qb[hh], kb[hh], ck_ref[0, hh:hh + 1, :], lo, hi, tri):
                    pd.append((jnp.exp(s - lse), _nt(dob[hh][lo:hi], vb[hh][k0:k1]), k0, k1))
                mean = sum(jnp.sum(p * dp, axis=-1, keepdims=True) for p, dp, _, _ in pd)
                dqn = 0.0
                for p, dp, k0, k1 in pd:
                    ds = p * (dp - mean)
                    dsb = ds.astype(BF16)
                    dv_acc[hh, k0:k1, :] += _tn(p.astype(BF16), dob[hh][lo:hi])
                    dkn_acc[hh, k0:k1, :] += _tn(dsb, qb[hh][lo:hi])
                    dc_acc[hh:hh + 1, k0:k1] -= jnp.sum(ds, axis=0, keepdims=True)
                    dqn = dqn + _nn(dsb, kb[hh][k0:k1])
                dqx, dgp = _rms_bwd(nq[hh][lo:hi], rq[hh][lo:hi], gq_ref[...], dqn * scale)
                dqs.append(dqx)
                dgq = dgq + dgp
            dq_ref[lo:hi, :] = jnp.concatenate(dqs, axis=1).astype(BF16)
        dks, dgk = [], jnp.zeros((1, DH), F32)
        for hh in range(2):
            dkx, dgp = _rms_bwd(nk[hh], rk[hh], gk_ref[...], dkn_acc[hh])
            dks.append(dkx)
            dgk = dgk + dgp
        dk_ref[...] = jnp.concatenate(dks, axis=1).astype(BF16)
        dv_ref[...] = jnp.concatenate([dv_acc[0], dv_acc[1]], axis=1).astype(BF16)
        dck_ref[0] = dc_acc[...]

        @pl.when(pl.program_id(0) == 0)
        def _():
            dgq_ref[...] = jnp.zeros_like(dgq_ref)
            dgk_ref[...] = jnp.zeros_like(dgk_ref)
        dgq_ref[...] += dgq
        dgk_ref[...] += dgk

        @pl.when(pl.program_id(0) == _HP - 1)
        def _():
            for cp in _chip_swap_copies(p_refs, fc_refs, *sems):
                cp.wait()

    fixed = lambda p: (0, 0)
    pair = pl.BlockSpec((T, LANES), lambda p: (0, p))
    gsp = pl.BlockSpec((1, DH), fixed)
    ckspec = pl.BlockSpec((1, 2, T), lambda p: (p, 0, 0))
    anyspec = pl.BlockSpec(memory_space=pl.ANY)
    res = pl.pallas_call(
        body, name="attn_bwd", grid=(_HP,),
        in_specs=_qkv_specs() + [ckspec, gsp, gsp, pl.BlockSpec((1, T, 2), lambda p: (p, 0, 0)), pair] + [anyspec] * n,
        out_specs=[pair, pair, pair, ckspec, gsp, gsp] + [anyspec] * n,
        out_shape=[_sds((T, AW), BF16)] * 3 + [_sds((_HP, 2, T)), _sds((1, DH)), _sds((1, DH))]
        + [_sds((3, *p.shape[1:]), p.dtype) for p in parts],
        scratch_shapes=[pltpu.VMEM((2, T, DH), F32), pltpu.VMEM((2, T, DH), F32), pltpu.VMEM((2, T), F32),
                        pltpu.SemaphoreType.DMA((3 * n,)), pltpu.SemaphoreType.DMA((3 * n,))],
        compiler_params=_cp(("arbitrary",)),
    )(z, z, z, ck, g_q, g_k, lse, datt, *parts)
    return (*res[:6], res[6:])


def _disc(lr, li, ls, br_t, bi_t):
    step = jnp.exp(ls)
    er = jnp.exp(lr * step)
    ab_re = er * jnp.cos(li * step)
    ab_im = er * jnp.sin(li * step)
    num_re = ab_re - 1.0
    num_im = ab_im
    den = lr * lr + li * li
    f_re = (num_re * lr + num_im * li) / den
    f_im = (num_im * lr - num_re * li) / den
    bb_re = f_re[:, None, :] * br_t - f_im[:, None, :] * bi_t
    bb_im = f_re[:, None, :] * bi_t + f_im[:, None, :] * br_t
    return ab_re, ab_im, bb_re, bb_im


def _disc_fwd(lr, li, ls, br_t, bi_t):
    def body(lr_ref, li_ref, ls_ref, br_ref, bi_ref, ar_ref, ai_ref, bbr_ref, bbi_ref):
        ar, ai, bbr, bbi = _disc(lr_ref[...], li_ref[...], ls_ref[...], br_ref[...], bi_ref[...])
        ar_ref[...] = ar
        ai_ref[...] = ai
        bbr_ref[...] = bbr
        bbi_ref[...] = bbi

    return pl.pallas_call(
        body, name="ssm_disc_fwd",
        out_shape=[_sds((NG, NP)), _sds((NG, NP)), _sds((NG, GS, NP)), _sds((NG, GS, NP))],
        compiler_params=_cp())(lr, li, ls, br_t, bi_t)


def _disc_bwd(lr, li, ls, br_t, bi_t, dar, dai, dbbr, dbbi):
    def body(lr_ref, li_ref, ls_ref, br_ref, bi_ref, dar_ref, dai_ref, dbbr_ref, dbbi_ref,
             dlr_ref, dli_ref, dls_ref, dbr_ref, dbi_ref):
        _, vjp = jax.vjp(_disc, lr_ref[...], li_ref[...], ls_ref[...], br_ref[...], bi_ref[...])
        dlr, dli, dls, dbr, dbi = vjp((dar_ref[...], dai_ref[...], dbbr_ref[...], dbbi_ref[...]))
        dlr_ref[...] = dlr
        dli_ref[...] = dli
        dls_ref[...] = dls
        dbr_ref[...] = dbr
        dbi_ref[...] = dbi

    return pl.pallas_call(
        body, name="ssm_disc_bwd",
        out_shape=[_sds((NG, NP)), _sds((NG, NP)), _sds((NG, 1)), _sds((NG, GS, NP)), _sds((NG, GS, NP))],
        compiler_params=_cp())(lr, li, ls, br_t, bi_t, dar, dai, dbbr, dbbi)


def _cmul(ar, ai, br, bi):
    return ar * br - ai * bi, ar * bi + ai * br


def _scan(buf, a_re, a_im, reverse, other=None):
    ar = [jnp.broadcast_to(a_re[:, LANES * k:LANES * (k + 1)], (NSEG, LANES)) for k in range(4)]
    ai = [jnp.broadcast_to(a_im[:, LANES * k:LANES * (k + 1)], (NSEG, LANES)) for k in range(4)]

    def tile(ref, i, k):
        return ref[pl.ds(pl.multiple_of(i * NSEG, NSEG), NSEG), LANES * k:LANES * (k + 1)]

    def advance(i, xr, xi):
        nr = [ar[k] * xr[k] - ai[k] * xi[k] + tile(buf, i, k) for k in range(4)]
        ni = [ar[k] * xi[k] + ai[k] * xr[k] + tile(buf, i, 4 + k) for k in range(4)]
        return nr, ni

    def index(i):
        return (SEG - 1 - i) if reverse else i

    zeros = tuple(jnp.zeros((NSEG, LANES), F32) for _ in range(4))

    def sweep1(i, carry):
        nr, ni = advance(index(i), carry[0], carry[1])
        return tuple(nr), tuple(ni)

    er, ei = lax.fori_loop(0, SEG, sweep1, (zeros, zeros), unroll=4)

    row = lax.broadcasted_iota(jnp.int32, (NSEG, LANES), 0)
    sr, si = [], []
    for k in range(4):
        pr, pi = ar[k], ai[k]
        for _ in range(SEG.bit_length() - 1):
            pr, pi = _cmul(pr, pi, pr, pi)
        ir, ii = er[k], ei[k]
        for d in (1, 2, 4):
            shift = (NSEG - d) if reverse else d
            ok = (row < NSEG - d) if reverse else (row >= d)
            mr, mi = _cmul(pr, pi, pltpu.roll(ir, shift, 0), pltpu.roll(ii, shift, 0))
            ir = ir + jnp.where(ok, mr, 0.0)
            ii = ii + jnp.where(ok, mi, 0.0)
            pr, pi = _cmul(pr, pi, pr, pi)
        shift = (NSEG - 1) if reverse else 1
        ok = (row < NSEG - 1) if reverse else (row >= 1)
        sr.append(jnp.where(ok, pltpu.roll(ir, shift, 0), 0.0))
        si.append(jnp.where(ok, pltpu.roll(ii, shift, 0), 0.0))

    def sweep2(i, carry):
        xr, xi, dar, dai = carry
        idx = index(i)
        if other is not None:
            orr = [tile(other, idx, k) for k in range(4)]
            oi = [tile(other, idx, 4 + k) for k in range(4)]
            dar = tuple(dar[k] + xr[k] * orr[k] + xi[k] * oi[k] for k in range(4))
            dai = tuple(dai[k] + xi[k] * orr[k] - xr[k] * oi[k] for k in range(4))
        nr, ni = advance(idx, xr, xi)
        start = pl.multiple_of(idx * NSEG, NSEG)
        for k in range(4):
            buf[pl.ds(start, NSEG), LANES * k:LANES * (k + 1)] = nr[k]
            buf[pl.ds(start, NSEG), LANES * (4 + k):LANES * (5 + k)] = ni[k]
        return tuple(nr), tuple(ni), dar, dai

    _, _, dar, dai = lax.fori_loop(0, SEG, sweep2, (tuple(sr), tuple(si), zeros, zeros), unroll=4)
    if other is None:
        return None
    da_re = jnp.concatenate([jnp.sum(d, axis=0, keepdims=True) for d in dar], axis=1)
    da_im = jnp.concatenate([jnp.sum(d, axis=0, keepdims=True) for d in dai], axis=1)
    return da_re, da_im


_SSM_BLOCKS = 4


def _ssm_fwd(u, bb, ab, cc, dsk, shards):
    n = len(shards)

    def body(u_ref, bb_ref, a_ref, cc_ref, d_ref, *rest):
        x_refs, y_ref, xs_ref, out_refs, sems = rest[:n], rest[n], rest[n + 1], rest[n + 2:2 * n + 2], rest[2 * n + 2:]

        @pl.when(pl.program_id(0) == 0)
        def _():
            local, remote = _gather_first_copies(x_refs, out_refs, *sems)
            for cp in local + remote:
                cp.start()

        ub = u_ref[...]
        xb = xs_ref.at[0]
        xb[...] = _nn(ub.astype(BF16), bb_ref[0])
        _scan(xb, a_ref[0, 0:1, :], a_ref[0, 1:2, :], reverse=False)
        y_ref[...] = _nn(xb[...].astype(BF16), cc_ref[0]) + d_ref[0] * ub

        @pl.when(pl.program_id(0) == _SSM_BLOCKS - 1)
        def _():
            local, remote = _gather_first_copies(x_refs, out_refs, *sems)
            for cp in remote + local:
                cp.wait()

    blk = lambda j: (j, 0, 0)
    col = pl.BlockSpec((T, LANES), lambda j: (0, j))
    anyspec = pl.BlockSpec(memory_space=pl.ANY)
    res = pl.pallas_call(
        body, name="ssm_fwd", grid=(_SSM_BLOCKS,),
        in_specs=[col, pl.BlockSpec((1, LANES, 1024), blk), pl.BlockSpec((1, 2, 512), blk),
                  pl.BlockSpec((1, 1024, LANES), blk), pl.BlockSpec((1, 1, LANES), blk)] + [anyspec] * n,
        out_specs=[col, pl.BlockSpec((1, T, 1024), blk)] + [anyspec] * n,
        out_shape=[_sds((T, SW)), _sds((_SSM_BLOCKS, T, 1024))] + [_sds((NDEV, *s.shape), s.dtype) for s in shards],
        scratch_shapes=_gather_first_scratch(n),
        compiler_params=_cp(("arbitrary",)),
    )(u, bb, ab, cc, dsk, *shards)
    return res[0], res[1], res[2:]


def _ssm_bwd(u, dy, xs, bb, ab, cc, dsk, arrs):
    n = len(arrs)

    def body(u_ref, dy_ref, x_ref, bb_ref, a_ref, cc_ref, d_ref, *rest):
        a_refs, rest = rest[:n], rest[n:]
        du_ref, dbb_ref, da_ref, dcc_ref, dd_ref = rest[:5]
        fs_refs, gb, sems = rest[5:5 + n], rest[5 + n], rest[6 + n:]

        @pl.when(pl.program_id(0) == 0)
        def _():
            for cp in _sibling_swap_copies(a_refs, fs_refs, *sems):
                cp.start()

        ub, dyv = u_ref[...], dy_ref[...]
        ubb, dyb = ub.astype(BF16), dyv.astype(BF16)
        a_re, a_im = a_ref[0, 0:1, :], a_ref[0, 1:2, :]
        dcc_ref[0] = _tn(x_ref[0].astype(BF16), dyb)
        gb[...] = _nt(dyb, cc_ref[0])
        da_re, da_im = _scan(gb, a_re, -a_im, reverse=True, other=x_ref.at[0])
        da_ref[0] = jnp.concatenate([da_re, da_im], axis=0)
        gc = gb[...].astype(BF16)
        du_ref[...] = _nt(gc, bb_ref[0]) + d_ref[0] * dyv
        dbb_ref[0] = _tn(ubb, gc)
        dd_ref[0] = jnp.sum(dyv * ub, axis=0, keepdims=True)

        @pl.when(pl.program_id(0) == _SSM_BLOCKS - 1)
        def _():
            for cp in _sibling_swap_copies(a_refs, fs_refs, *sems):
                cp.wait()

    blk = lambda j: (j, 0, 0)
    col = pl.BlockSpec((T, LANES), lambda j: (0, j))
    anyspec = pl.BlockSpec(memory_space=pl.ANY)
    res = pl.pallas_call(
        body, name="ssm_bwd", grid=(_SSM_BLOCKS,),
        in_specs=[col, col, pl.BlockSpec((1, T, 1024), blk), pl.BlockSpec((1, LANES, 1024), blk),
                  pl.BlockSpec((1, 2, 512), blk), pl.BlockSpec((1, 1024, LANES), blk), pl.BlockSpec((1, 1, LANES), blk)]
        + [anyspec] * n,
        out_specs=[col, pl.BlockSpec((1, LANES, 1024), blk), pl.BlockSpec((1, 2, 512), blk),
                   pl.BlockSpec((1, 1024, LANES), blk), pl.BlockSpec((1, 1, LANES), blk)] + [anyspec] * n,
        out_shape=[_sds((T, SW)), _sds((4, LANES, 1024)), _sds((4, 2, 512)), _sds((4, 1024, LANES)), _sds((4, 1, LANES))]
        + [_sds((4, *a.shape[2:]), a.dtype) for a in arrs],
        scratch_shapes=[pltpu.VMEM((T, 1024), F32), pltpu.SemaphoreType.DMA((4 * n,)), pltpu.SemaphoreType.DMA((4 * n,))],
        compiler_params=_cp(("arbitrary",)),
    )(u, dy, xs, bb, ab, cc, dsk, *arrs)
    return (*res[:5], res[5:])


def _interleave(a):
    return a.reshape(NSEG, SEG, a.shape[1]).transpose(1, 0, 2).reshape(a.shape)


def _deinterleave(a):
    return a.reshape(SEG, NSEG, a.shape[1]).transpose(1, 0, 2).reshape(a.shape)


_TM_MIX = 512


def _mix_parts(att, y, wg, bg):
    y2 = jax.nn.gelu(y)
    y2b = y2.astype(BF16)
    sg = jax.nn.sigmoid(_nn(y2b, wg) + bg)
    ssm = y2 * sg
    ra, rs = _rms_r(att), _rms_r(ssm)
    return y2, y2b, sg, ssm, ra, rs


def _mix_fwd(att, y, x, wg, bg, ga, gs, wo, bufs):
    m = len(bufs)

    def body(att_ref, y_ref, x_ref, wg_ref, bg_ref, ga_ref, gs_ref, wo_ref, *rest):
        x1_ref, buf_refs, sems = rest[m], rest[m + 1:2 * m + 1], rest[2 * m + 1:]

        @pl.when(pl.program_id(0) == 0)
        def _():
            for cp in _pass_on_copies(buf_refs, *sems):
                cp.start()

        attv = att_ref[...]
        _, _, _, ssm, ra, rs = _mix_parts(attv, y_ref[...], wg_ref[...], bg_ref[...])
        ma = (attv * ra * ga_ref[...]).astype(BF16)
        ms = (ssm * rs * gs_ref[...]).astype(BF16)
        x1_ref[...] = x_ref[...] + _nn(ma, wo_ref[0:AW, :]) + _nn(ms, wo_ref[AW:D, :])

        @pl.when(pl.program_id(0) == T // _TM_MIX - 1)
        def _():
            for cp in _pass_on_copies(buf_refs, *sems):
                cp.wait()

    row = lambda i: (i, 0)
    fixed = lambda i: (0, 0)
    half = pl.BlockSpec((_TM_MIX, AW), row)
    vec = pl.BlockSpec((1, AW), fixed)
    anyspec = pl.BlockSpec(memory_space=pl.ANY)
    res = pl.pallas_call(
        body, name="mix_fwd", grid=(T // _TM_MIX,),
        in_specs=[half, half, pl.BlockSpec((_TM_MIX, D), row), pl.BlockSpec((SW, SW), fixed), vec, vec, vec,
                  pl.BlockSpec((D, D), fixed)] + [anyspec] * m,
        out_specs=[pl.BlockSpec((_TM_MIX, D), row)] + [anyspec] * m,
        out_shape=[_sds((T, D))] + [_sds(b.shape, b.dtype) for b in bufs],
        input_output_aliases={8 + i: 1 + i for i in range(m)},
        scratch_shapes=[pltpu.SemaphoreType.DMA((3 * m,)), pltpu.SemaphoreType.DMA((3 * m,))],
        compiler_params=_cp(("arbitrary",)),
    )(att, y, x, wg, bg, ga, gs, wo, *bufs)
    return res[0], res[1:]


def _mix_bwd(att, y, dx1, wg, bg, ga, gs, wo):
    last = T // _TM_MIX - 1

    def body(att_ref, y_ref, d_ref, wg_ref, bg_ref, ga_ref, gs_ref, wo_ref,
             datt_ref, dy_ref, dwg_ref, dbg_ref, dga_ref, dgs_ref, dwo_ref, dwg_acc, dwo_acc):
        attv, yv, db = att_ref[...], y_ref[...], d_ref[...].astype(BF16)
        y2, y2b, sg, ssm, ra, rs = _mix_parts(attv, yv, wg_ref[...], bg_ref[...])
        na, ns = attv * ra, ssm * rs
        ma = (na * ga_ref[...]).astype(BF16)
        ms = (ns * gs_ref[...]).astype(BF16)
        dma = _nt(db, wo_ref[0:AW, :])
        dms = _nt(db, wo_ref[AW:D, :])
        datt, dga = _rms_bwd(na, ra, ga_ref[...], dma)
        dssm, dgs = _rms_bwd(ns, rs, gs_ref[...], dms)
        dgl = dssm * y2 * sg * (1.0 - sg)
        dglb = dgl.astype(BF16)
        dy2 = dssm * sg + _nt(dglb, wg_ref[...])
        _, gelu_vjp = jax.vjp(jax.nn.gelu, yv)
        datt_ref[...] = datt
        dy_ref[...] = gelu_vjp(dy2)[0]

        @pl.when(pl.program_id(0) == 0)
        def _():
            dwg_acc[...] = jnp.zeros_like(dwg_acc)
            dwo_acc[...] = jnp.zeros_like(dwo_acc)
            dbg_ref[...] = jnp.zeros_like(dbg_ref)
            dga_ref[...] = jnp.zeros_like(dga_ref)
            dgs_ref[...] = jnp.zeros_like(dgs_ref)
        dwg_acc[...] += _tn(y2b, dglb)
        dbg_ref[...] += jnp.sum(dgl, axis=0, keepdims=True)
        dga_ref[...] += dga
        dgs_ref[...] += dgs
        dwo_acc[0:AW, :] += _tn(ma, db)
        dwo_acc[AW:D, :] += _tn(ms, db)

        @pl.when(pl.program_id(0) == last)
        def _():
            dwg_ref[...] = dwg_acc[...].astype(BF16)
            dwo_ref[...] = dwo_acc[...].astype(BF16)

    row = lambda i: (i, 0)
    fixed = lambda i: (0, 0)
    half = pl.BlockSpec((_TM_MIX, AW), row)
    vec = pl.BlockSpec((1, AW), fixed)
    sq = pl.BlockSpec((SW, SW), fixed)
    big = pl.BlockSpec((D, D), fixed)
    return pl.pallas_call(
        body, name="mix_bwd", grid=(T // _TM_MIX,),
        in_specs=[half, half, pl.BlockSpec((_TM_MIX, D), row), sq, vec, vec, vec, big],
        out_specs=[half, half, sq, vec, vec, vec, big],
        out_shape=[_sds((T, AW)), _sds((T, SW)), _sds((SW, SW), BF16), _sds((1, SW)), _sds((1, AW)), _sds((1, SW)),
                   _sds((D, D), BF16)],
        scratch_shapes=[pltpu.VMEM((SW, SW), F32), pltpu.VMEM((D, D), F32)],
        compiler_params=_cp(("arbitrary",)),
    )(att, y, dx1, wg, bg, ga, gs, wo)


_NCB = -(-CS // LANES)


def _ffn_up(x1, g, wu4):
    tm = 1024

    def body(x_ref, g_ref, w_ref, h_ref, up_ref):
        @pl.when(pl.program_id(1) == 0)
        def _():
            xv = x_ref[...]
            h_ref[...] = (xv * _rms_r(xv) * g_ref[...]).astype(BF16)
        up_ref[0, 0] = _nt(h_ref[...], w_ref[0, 0])

    return pl.pallas_call(
        body, name="ffn_up", grid=(T // tm, NDEV),
        in_specs=[pl.BlockSpec((tm, D), lambda i, e: (i, 0)), pl.BlockSpec((1, D), lambda i, e: (0, 0)),
                  pl.BlockSpec((1, 1, CS, D), lambda i, e: (e // 4, e % 4, 0, 0))],
        out_specs=[pl.BlockSpec((tm, D), lambda i, e: (i, 0)),
                   pl.BlockSpec((1, 1, tm, CS), lambda i, e: (e // 4, e % 4, i, 0))],
        out_shape=[_sds((T, D), BF16), _sds((2, 4, T, CS))],
        compiler_params=_cp(("parallel", "arbitrary")),
    )(x1, g, wu4)


def _shift_down(h, k):
    row = lax.broadcasted_iota(jnp.int32, h.shape, 0)
    return jnp.where(row >= k, pltpu.roll(h, k, 0), 0.0)


def _shift_up(h, k):
    row = lax.broadcasted_iota(jnp.int32, h.shape, 0)
    return jnp.where(row < T - k, pltpu.roll(h, T - k, 0), 0.0)


def _conv(h, w, b):
    return w[0:1, :] * _shift_down(h, 2) + w[1:2, :] * _shift_down(h, 1) + w[2:3, :] * h + b


def _chan(gv, rows):
    return pl.BlockSpec((1, 1, rows, LANES), lambda d, j: (gv, d, 0, j))


def _ffn_act(up4, cw4, cb4):
    def body(ug_ref, uv_ref, wg_ref, wv_ref, bg_ref, bv_ref, act_ref):
        g = _conv(ug_ref[0, 0], wg_ref[0, 0], bg_ref[0, 0])
        v = _conv(uv_ref[0, 0], wv_ref[0, 0], bv_ref[0, 0])
        act_ref[0] = (g * jax.nn.sigmoid(g) * v).astype(BF16)

    return pl.pallas_call(
        body, name="ffn_act", grid=(4, _NCB),
        in_specs=[_chan(0, T), _chan(1, T), _chan(0, 3), _chan(1, 3), _chan(0, 1), _chan(1, 1)],
        out_specs=pl.BlockSpec((1, T, LANES), lambda d, j: (d, 0, j)), out_shape=_sds((4, T, CS), BF16),
        compiler_params=_cp(("parallel", "parallel")),
    )(up4, up4, cw4, cw4, cb4, cb4)


def _ffn_act_bwd(up4, dact, cw4, cb4):
    def conv_bwd(h, w, d):
        dup = w[2:3, :] * d + w[1:2, :] * _shift_up(d, 1) + w[0:1, :] * _shift_up(d, 2)
        dw = jnp.concatenate([jnp.sum(d * _shift_down(h, 2), axis=0, keepdims=True),
                              jnp.sum(d * _shift_down(h, 1), axis=0, keepdims=True),
                              jnp.sum(d * h, axis=0, keepdims=True)], axis=0)
        return dup, dw, jnp.sum(d, axis=0, keepdims=True)

    def body(ug_ref, uv_ref, da_ref, wg_ref, wv_ref, bg_ref, bv_ref, dup_ref, dw_ref, db_ref):
        ug, uv, da = ug_ref[0, 0], uv_ref[0, 0], da_ref[0]
        g = _conv(ug, wg_ref[0, 0], bg_ref[0, 0])
        v = _conv(uv, wv_ref[0, 0], bv_ref[0, 0])
        sg = jax.nn.sigmoid(g)
        dg = da * v * (sg * (1.0 + g * (1.0 - sg)))
        dv = da * (g * sg)
        dug, dwg, dbg = conv_bwd(ug, wg_ref[0, 0], dg)
        duv, dwv, dbv = conv_bwd(uv, wv_ref[0, 0], dv)
        dup_ref[0, 0] = dug.astype(BF16)
        dup_ref[1, 0] = duv.astype(BF16)
        dw_ref[0, 0] = dwg
        dw_ref[1, 0] = dwv
        db_ref[0, 0] = dbg
        db_ref[1, 0] = dbv

    both = lambda rows: pl.BlockSpec((2, 1, rows, LANES), lambda d, j: (0, d, 0, j))
    return pl.pallas_call(
        body, name="ffn_act_bwd", grid=(4, _NCB),
        in_specs=[_chan(0, T), _chan(1, T), pl.BlockSpec((1, T, LANES), lambda d, j: (d, 0, j)),
                  _chan(0, 3), _chan(1, 3), _chan(0, 1), _chan(1, 1)],
        out_specs=[both(T), both(3), both(1)],
        out_shape=[_sds((2, 4, T, CS), BF16), _sds((2, 4, 3, CS)), _sds((2, 4, 1, CS))],
        compiler_params=_cp(("parallel", "parallel")),
    )(up4, up4, dact, cw4, cw4, cb4, cb4)


def _ffn_down_loss(act, wd4, x1, target):
    tm = 512

    def body(a_ref, w_ref, x1_ref, t_ref, loss_ref, dx_ref, dxb_ref):
        x2 = x1_ref[...]
        for d in range(4):
            x2 = x2 + _nn(a_ref[d], w_ref[d])
        err = x2 - t_ref[...]
        dx = err * (1.0 / D)
        dx_ref[...] = dx
        dxb_ref[...] = dx.astype(BF16)

        @pl.when(pl.program_id(0) == 0)
        def _():
            loss_ref[...] = jnp.zeros_like(loss_ref)
        loss_ref[...] += 0.5 * jnp.sum(jnp.mean(err * err, axis=-1, keepdims=True), axis=0, keepdims=True)

    row = lambda i: (i, 0)
    fixed = lambda i: (0, 0)
    return pl.pallas_call(
        body, name="ffn_down_loss", grid=(T // tm,),
        in_specs=[pl.BlockSpec((4, tm, CS), lambda i: (0, i, 0)), pl.BlockSpec((4, CS, D), lambda i: (0, 0, 0)),
                  pl.BlockSpec((tm, D), row), pl.BlockSpec((tm, D), row)],
        out_specs=[pl.BlockSpec((1, 1), fixed), pl.BlockSpec((tm, D), row), pl.BlockSpec((tm, D), row)],
        out_shape=[_sds((1, 1)), _sds((T, D)), _sds((T, D), BF16)],
        compiler_params=_cp(("arbitrary",)),
    )(act, wd4, x1, target)


def _ffn_dact(dx2b, wd4):
    tm = 1024

    def body(d_ref, w_ref, o_ref):
        o_ref[0] = _nt(d_ref[...], w_ref[0])

    return pl.pallas_call(
        body, name="ffn_dact", grid=(4, T // tm),
        in_specs=[pl.BlockSpec((tm, D), lambda d, i: (i, 0)), pl.BlockSpec((1, CS, D), lambda d, i: (d, 0, 0))],
        out_specs=pl.BlockSpec((1, tm, CS), lambda d, i: (d, i, 0)), out_shape=_sds((4, T, CS)),
        compiler_params=_cp(("parallel", "parallel")),
    )(dx2b, wd4)


def _ffn_dwd(act, dx2b):
    def body(a_ref, d_ref, o_ref):
        o_ref[0] = _tn(a_ref[0], d_ref[...]).astype(BF16)

    return pl.pallas_call(
        body, name="ffn_dwd", grid=(4,),
        in_specs=[pl.BlockSpec((1, T, CS), lambda d: (d, 0, 0)), pl.BlockSpec((T, D), lambda d: (0, 0))],
        out_specs=pl.BlockSpec((1, CS, D), lambda d: (d, 0, 0)), out_shape=_sds((4, CS, D), BF16),
        compiler_params=_cp(("parallel",)),
    )(act, dx2b)


def _ffn_dwu(h2, dup4):
    tn = 1024

    def body(h_ref, d_ref, o_ref):
        o_ref[0, 0] = _tn(d_ref[0, 0], h_ref[...]).astype(BF16)

    return pl.pallas_call(
        body, name="ffn_dwu", grid=(NDEV, D // tn),
        in_specs=[pl.BlockSpec((T, tn), lambda e, i: (0, i)),
                  pl.BlockSpec((1, 1, T, CS), lambda e, i: (e // 4, e % 4, 0, 0))],
        out_specs=pl.BlockSpec((1, 1, CS, tn), lambda e, i: (e // 4, e % 4, 0, i)),
        out_shape=_sds((2, 4, CS, D), BF16),
        compiler_params=_cp(("parallel", "parallel")),
    )(h2, dup4)


def _ffn_up_bwd(dup4, wu4, x1, g, dres):
    tm = 256

    def body(d_ref, w_ref, x_ref, g_ref, dres_ref, dx_ref, dg_ref):
        dh = jnp.zeros((tm, D), F32)
        for gv in range(2):
            for d in range(4):
                dh = dh + _nn(d_ref[gv, d], w_ref[gv, d])
        xv = x_ref[...]
        r = _rms_r(xv)
        dxr, dgp = _rms_bwd(xv * r, r, g_ref[...], dh)
        dx_ref[...] = dres_ref[...] + dxr

        @pl.when(pl.program_id(0) == 0)
        def _():
            dg_ref[...] = jnp.zeros_like(dg_ref)
        dg_ref[...] += dgp

    row = lambda i: (i, 0)
    fixed = lambda i: (0, 0)
    return pl.pallas_call(
        body, name="ffn_up_bwd", grid=(T // tm,),
        in_specs=[pl.BlockSpec((2, 4, tm, CS), lambda i: (0, 0, i, 0)), pl.BlockSpec((2, 4, CS, D), lambda i: (0, 0, 0, 0)),
                  pl.BlockSpec((tm, D), row), pl.BlockSpec((1, D), fixed), pl.BlockSpec((tm, D), row)],
        out_specs=[pl.BlockSpec((tm, D), row), pl.BlockSpec((1, D), fixed)],
        out_shape=[_sds((T, D)), _sds((1, D))],
        compiler_params=_cp(("arbitrary",)),
    )(dup4, wu4, x1, g, dres)


def _adamw_math(w, g, m, v):
    m = ADAM_B1 * m + (1.0 - ADAM_B1) * g
    v = ADAM_B2 * v + (1.0 - ADAM_B2) * jnp.square(g)
    m_hat = m / (1.0 - ADAM_B1 ** ADAM_STEP)
    v_hat = v / (1.0 - ADAM_B2 ** ADAM_STEP)
    delta = -ADAM_LR * (m_hat / (jnp.sqrt(v_hat) + ADAM_EPS) + ADAM_WD * w)
    return delta, m, v


def _where_am_i():
    x, y, c = _place()
    return jnp.stack([c, 2 * x + y]).astype(jnp.int32)


def _pair_sum(a4, recv, name):
    _, _, rows, cols = a4.shape
    tr = _row_tile(rows, cols, 3 << 20)

    def body(sel_ref, own_ref, recv_ref, out_ref):
        out_ref[...] = (own_ref[0].astype(F32) + recv_ref[...].astype(F32)).astype(out_ref.dtype)

    grid_spec = pltpu.PrefetchScalarGridSpec(
        num_scalar_prefetch=1, grid=(4, rows // tr),
        in_specs=[pl.BlockSpec((1, 1, tr, cols), lambda k, i, s: (k, s[0], i, 0)),
                  pl.BlockSpec((1, tr, cols), lambda k, i, s: (k, i, 0))],
        out_specs=pl.BlockSpec((1, tr, cols), lambda k, i, s: (k, i, 0)),
    )
    return pl.pallas_call(
        body, name=name, grid_spec=grid_spec, out_shape=_sds((4, rows, cols), a4.dtype),
        compiler_params=_cp(("arbitrary", "arbitrary")),
    )(_where_am_i(), a4, recv)


def _adamw_rs(a4, recv, from_chips, w, m, v, name):
    rows, cols = w.shape
    tr = _row_tile(rows, cols, 3 << 19)

    def body(sel_ref, own_ref, recv_ref, fc_ref, w_ref, m_ref, v_ref, g_ref, d_ref, nm_ref, nv_ref):
        g = own_ref[0, 0].astype(F32) + recv_ref[0].astype(F32)
        for j in range(3):
            g = g + fc_ref[j].astype(F32)
        d, nm, nv = _adamw_math(w_ref[...], g, m_ref[...], v_ref[...])
        g_ref[...] = g
        d_ref[...] = d
        nm_ref[...] = nm
        nv_ref[...] = nv

    flat = pl.BlockSpec((tr, cols), lambda i, s: (i, 0))
    grid_spec = pltpu.PrefetchScalarGridSpec(
        num_scalar_prefetch=1, grid=(rows // tr,),
        in_specs=[pl.BlockSpec((1, 1, tr, cols), lambda i, s: (s[1], s[0], i, 0)),
                  pl.BlockSpec((1, tr, cols), lambda i, s: (s[1], i, 0)),
                  pl.BlockSpec((3, tr, cols), lambda i, s: (0, i, 0)), flat, flat, flat],
        out_specs=[flat] * 4,
    )
    return pl.pallas_call(
        body, name=name, grid_spec=grid_spec, out_shape=[_sds((rows, cols))] * 4,
        compiler_params=_cp(("arbitrary",)),
    )(_where_am_i(), a4, recv, from_chips, w, m, v)


_SMALL_F32 = (("g_mix", (8, 128)), ("b_f", (1, 8)), ("g_q", (1, 64)), ("g_k", (1, 64)), ("lambda_re", (32, 64)),
              ("lambda_im", (32, 64)), ("log_step", (1, 32)), ("d_skip", (32, 16)), ("b_glu", (4, 128)),
              ("g_attn_out", (4, 128)), ("g_ssm_out", (4, 128)), ("g_ffn", (8, 128)), ("conv_b", (44, 128)))
_SMALL_PAIRS = (("b_re", "b_im"), ("c_re", "c_im"))
_PAIR_SHAPE = (NG * GS, NP)


def _ceil8(r):
    return -(-r // 8) * 8


def _small_names():
    return [n for n, _ in _SMALL_F32] + [n for pair in _SMALL_PAIRS for n in pair]


def _pack_small_grads(g, loss_dev):
    parts = [jnp.pad(g[n].reshape(r, c), ((0, _ceil8(r) - r), (0, LANES - c))) for n, (r, c) in _SMALL_F32]
    parts.append(jnp.pad(loss_dev, ((0, 7), (0, LANES - 1))))
    pairs = [jnp.concatenate([g[a].reshape(_PAIR_SHAPE), g[b].reshape(_PAIR_SHAPE)], axis=1) for a, b in _SMALL_PAIRS]
    return jnp.concatenate(parts, axis=0), jnp.concatenate(pairs, axis=0).astype(BF16)


def _adamw_small(gall, gall_b, wmv):
    names = _small_names()
    shapes = [s for _, s in _SMALL_F32] + [_PAIR_SHAPE] * (2 * len(_SMALL_PAIRS))
    npar = len(names)

    def body(ga_ref, gb_ref, *rest):
        ins, loss_ref, outs = rest[:3 * npar], rest[3 * npar], rest[3 * npar + 1:]
        ga, gb = ga_ref[0], gb_ref[0].astype(F32)
        for dev in range(1, NDEV):
            ga = ga + ga_ref[dev]
            gb = gb + gb_ref[dev].astype(F32)
        grads, off = [], 0
        for _, (r, c) in _SMALL_F32:
            grads.append(ga[off:off + r, 0:c])
            off += _ceil8(r)
        loss_ref[...] = ga[off:off + 8]
        rows, cols = _PAIR_SHAPE
        for j in range(len(_SMALL_PAIRS)):
            for h in range(2):
                grads.append(gb[rows * j:rows * (j + 1), cols * h:cols * (h + 1)])
        for i, g in enumerate(grads):
            d, nm, nv = _adamw_math(ins[3 * i][...], g, ins[3 * i + 1][...], ins[3 * i + 2][...])
            outs[4 * i][...] = g
            outs[4 * i + 1][...] = d
            outs[4 * i + 2][...] = nm
            outs[4 * i + 3][...] = nv

    flat = [a for n in names for a in wmv[n]]
    out_shape = [_sds((8, LANES))] + [_sds(s) for s in shapes for _ in range(4)]
    res = pl.pallas_call(body, name="adamw_small", out_shape=out_shape, compiler_params=_cp())(gall, gall_b, *flat)
    return res[0], {n: res[1 + 4 * i:5 + 4 * i] for i, n in enumerate(names)}


def _block_diag(m):
    eye = jnp.eye(8, dtype=m.dtype)
    r, c = m.shape[2], m.shape[3]
    return (m[:, :, :, None, :] * eye[None, :, None, :, None]).reshape(4, 8 * r, 8 * c)


def _diag_blocks(dense, r, c):
    eye = jnp.eye(8, dtype=dense.dtype)
    return jnp.sum(dense.reshape(4, 8, r, 8, c) * eye[None, :, None, :, None], axis=3)


def kernel(x, g_mix, w_in, b_f, g_q, g_k, lambda_re, lambda_im, log_step, b_re, b_im, c_re, c_im, d_skip, w_glu, b_glu, g_attn_out, g_ssm_out, w_out, g_ffn, w_up, conv_w, conv_b, w_down, loss_target, m_g_mix, m_w_in, m_b_f, m_g_q, m_g_k, m_lambda_re, m_lambda_im, m_log_step, m_b_re, m_b_im, m_c_re, m_c_im, m_d_skip, m_w_glu, m_b_glu, m_g_attn_out, m_g_ssm_out, m_w_out, m_g_ffn, m_w_up, m_conv_w, m_conv_b, m_w_down, v_g_mix, v_w_in, v_b_f, v_g_q, v_g_k, v_lambda_re, v_lambda_im, v_log_step, v_b_re, v_b_im, v_c_re, v_c_im, v_d_skip, v_w_glu, v_b_glu, v_g_attn_out, v_g_ssm_out, v_w_out, v_g_ffn, v_w_up, v_conv_w, v_conv_b, v_w_down):
    weights = dict(g_mix=g_mix, w_in=w_in, b_f=b_f, g_q=g_q, g_k=g_k, lambda_re=lambda_re, lambda_im=lambda_im,
                   log_step=log_step, b_re=b_re, b_im=b_im, c_re=c_re, c_im=c_im, d_skip=d_skip, w_glu=w_glu,
                   b_glu=b_glu, g_attn_out=g_attn_out, g_ssm_out=g_ssm_out, w_out=w_out, g_ffn=g_ffn, w_up=w_up,
                   conv_w=conv_w, conv_b=conv_b, w_down=w_down)
    mom_m = dict(g_mix=m_g_mix, w_in=m_w_in, b_f=m_b_f, g_q=m_g_q, g_k=m_g_k, lambda_re=m_lambda_re,
                 lambda_im=m_lambda_im, log_step=m_log_step, b_re=m_b_re, b_im=m_b_im, c_re=m_c_re, c_im=m_c_im,
                 d_skip=m_d_skip, w_glu=m_w_glu, b_glu=m_b_glu, g_attn_out=m_g_attn_out, g_ssm_out=m_g_ssm_out,
                 w_out=m_w_out, g_ffn=m_g_ffn, w_up=m_w_up, conv_w=m_conv_w, conv_b=m_conv_b, w_down=m_w_down)
    mom_v = dict(g_mix=v_g_mix, w_in=v_w_in, b_f=v_b_f, g_q=v_g_q, g_k=v_g_k, lambda_re=v_lambda_re,
                 lambda_im=v_lambda_im, log_step=v_log_step, b_re=v_b_re, b_im=v_b_im, c_re=v_c_re, c_im=v_c_im,
                 d_skip=v_d_skip, w_glu=v_w_glu, b_glu=v_b_glu, g_attn_out=v_g_attn_out, g_ssm_out=v_g_ssm_out,
                 w_out=v_w_out, g_ffn=v_g_ffn, w_up=v_w_up, conv_w=v_conv_w, conv_b=v_conv_b, w_down=v_w_down)
    names = list(weights)
    big = ("w_in", "w_glu", "w_out", "w_up", "w_down", "conv_w")

    xs = x[0]
    target = loss_target[0]
    row = lambda a: a.reshape(1, -1)

    (win_g,) = _all_gather([w_in.T.astype(BF16)], "gather_w_in")
    w_in_t = win_g.reshape(8 * 257, D)
    wcat = jnp.concatenate([w_in_t[0:1536], w_in_t[1544:2056], jnp.pad(w_in_t[1536:1544], ((0, LANES - NH), (0, 0)))],
                           axis=0)
    cb4 = conv_b.reshape(2, 4, 1, CS)

    h1, z, f_t, (wg_g, wo_g) = _in_proj(xs, row(g_mix), wcat, [w_glu.astype(BF16), w_out.astype(BF16)])
    bf_col = b_f.reshape(NH, 1)
    ck = _gates_fwd(f_t, bf_col).reshape(_HP, 2, T)
    att, lse, (wu_g, cw_g), (wg_g, wo_g) = _attn_fwd(z, ck, row(g_q), row(g_k), [w_up.T.astype(BF16), conv_w],
                                                       [wg_g, wo_g])

    ls_col = log_step.reshape(NG, 1)
    br_t, bi_t = b_re.transpose(0, 2, 1), b_im.transpose(0, 2, 1)
    ab_re, ab_im, bb_re, bb_im = _disc_fwd(lambda_re, lambda_im, ls_col, br_t, bi_t)
    bb = jnp.concatenate([_block_diag(bb_re.reshape(4, 8, GS, NP)), _block_diag(bb_im.reshape(4, 8, GS, NP))],
                         axis=2).astype(BF16)
    ab = jnp.stack([ab_re.reshape(4, 512), ab_im.reshape(4, 512)], axis=1)
    cdiag = lambda cm: _block_diag(cm.reshape(4, 8, GS, NP).transpose(0, 1, 3, 2))
    cc = jnp.concatenate([cdiag(c_re), -cdiag(c_im)], axis=1).astype(BF16)
    dsk = d_skip.reshape(4, 1, LANES)
    u_il = _interleave(z[:, 1536:2048])
    y_il, xs_il, (wd_g,) = _ssm_fwd(u_il, bb, ab, cc, dsk, [w_down.astype(BF16)])
    y = _deinterleave(y_il)
    wg_f = wg_g.reshape(SW, SW)
    wo_f = wo_g.reshape(D, D)
    x1, (wu_g, cw_g, wd_g) = _mix_fwd(att, y, xs, wg_f, row(b_glu), row(g_attn_out), row(g_ssm_out), wo_f,
                                      [wu_g, cw_g, wd_g])
    wu4 = wu_g.reshape(2, 4, CS, D)
    wd4 = wd_g.reshape(4, CS, D)
    cw4 = cw_g.reshape(2, 4, 3, CS)
    h2, up4 = _ffn_up(x1, row(g_ffn), wu4)
    act = _ffn_act(up4, cw4, cb4)
    loss_dev, dx2, dx2b = _ffn_down_loss(act, wd4, x1, target)

    dact = _ffn_dact(dx2b, wd4)
    d_wd = _ffn_dwd(act, dx2b)
    dup4, dcw4, dcb4 = _ffn_act_bwd(up4, dact, cw4, cb4)
    d_wu = _ffn_dwu(h2, dup4)
    dx1, dg_ffn = _ffn_up_bwd(dup4, wu4, x1, row(g_ffn), dx2)
    datt, dy, d_wg, d_bg, d_ga, d_gs, d_wo = _mix_bwd(att, y, dx1, wg_f, row(b_glu), row(g_attn_out),
                                                       row(g_ssm_out), wo_f)
    early = ("w_up", "w_down", "conv_w", "w_out", "w_glu")
    by_dest = {"w_up": d_wu.reshape(4, 2, CS, D), "w_down": d_wd.reshape(4, 2, DFF // NDEV, D),
               "conv_w": dcw4.reshape(4, 2, 3, CS), "w_out": d_wo.reshape(4, 2, D // NDEV, D),
               "w_glu": d_wg.reshape(4, 2, SW // NDEV, SW)}
    du_il, dbb, dab, dcc, ddsk, fs = _ssm_bwd(u_il, _interleave(dy), xs_il, bb, ab, cc, dsk, [by_dest[n] for n in early])
    from_sibling = dict(zip(early, fs))
    chip_sums = {n: _pair_sum(by_dest[n], from_sibling[n], "rs_pair_sum_" + n) for n in early}
    du = _deinterleave(du_il).astype(BF16)
    dbb_re = _diag_blocks(dbb[:, :, 0:512], GS, NP).reshape(NG, GS, NP)
    dbb_im = _diag_blocks(dbb[:, :, 512:1024], GS, NP).reshape(NG, GS, NP)
    dlr, dli, dls, dbr_t, dbi_t = _disc_bwd(lambda_re, lambda_im, ls_col, br_t, bi_t,
                                            dab[:, 0].reshape(NG, NP), dab[:, 1].reshape(NG, NP), dbb_re, dbb_im)
    d_cre = _diag_blocks(dcc[:, 0:512], NP, GS).transpose(0, 1, 3, 2).reshape(NG, GS, NP)
    d_cim = -_diag_blocks(dcc[:, 512:1024], NP, GS).transpose(0, 1, 3, 2).reshape(NG, GS, NP)

    dq, dk, dv, dck, d_gq, d_gk, fc = _attn_bwd(z, ck, row(g_q), row(g_k), lse, datt, [chip_sums[n] for n in early])
    from_chips = dict(zip(early, fc))
    df_t, d_bf = _gates_bwd(dck.reshape(NH, T), f_t, bf_col)
    df = jnp.concatenate([df_t.T, jnp.zeros((T, LANES - NH), F32)], axis=1).astype(BF16)
    d_wcat = _in_proj_dw(h1, dq, dk, dv, du, df)
    d_win = jnp.concatenate([d_wcat[0:1536], d_wcat[2048:2048 + NH], d_wcat[1536:2048]], axis=0)
    by_dest["w_in"] = d_win.astype(BF16).reshape(4, 2, 257, D)
    (from_sibling["w_in"],) = _swap_sibling([by_dest["w_in"]], "rs_pair_swap_w_in")
    chip_sums["w_in"] = _pair_sum(by_dest["w_in"], from_sibling["w_in"], "rs_pair_sum_w_in")
    grad_x, dg_mix, (from_chips["w_in"],) = _in_proj_bwd(dq, dk, dv, du, df, wcat, xs, row(g_mix), dx1, [chip_sums["w_in"]])

    grads, deltas, new_m, new_v = {}, {}, {}, {}
    for n in early + ("w_in",):
        flip = (lambda a: a.T) if n in ("w_up", "w_in") else (lambda a: a)
        outs = _adamw_rs(by_dest[n], from_sibling[n], from_chips[n], flip(weights[n]), flip(mom_m[n]), flip(mom_v[n]),
                         "adamw_" + n)
        grads[n], deltas[n], new_m[n], new_v[n] = (flip(o) for o in outs)

    small = dict(g_mix=dg_mix, b_f=d_bf, g_q=d_gq, g_k=d_gk, lambda_re=dlr, lambda_im=dli, log_step=dls,
                 b_re=dbr_t, b_im=dbi_t, c_re=d_cre, c_im=d_cim, d_skip=ddsk, b_glu=d_bg, g_attn_out=d_ga,
                 g_ssm_out=d_gs, g_ffn=dg_ffn, conv_b=dcb4)
    small_all, small_all_b = _all_gather(list(_pack_small_grads(small, loss_dev)), "gather_small_grads")
    views = dict(_SMALL_F32)
    swapped = ("b_re", "b_im")

    def view(n, a):
        return a.transpose(0, 2, 1).reshape(_PAIR_SHAPE) if n in swapped else a.reshape(views.get(n, _PAIR_SHAPE))

    loss_p, small_out = _adamw_small(small_all, small_all_b,
                                     {n: (view(n, weights[n]), view(n, mom_m[n]), view(n, mom_v[n])) for n in small})
    loss = loss_p[0, 0]
    for n in small:
        back = (lambda a: a.reshape(NG, GS, NP).transpose(0, 2, 1)) if n in swapped else (lambda a: a.reshape(weights[n].shape))
        grads[n], deltas[n], new_m[n], new_v[n] = (back(o) for o in small_out[n])

    return (loss, grad_x[None], *[grads[n] for n in names], *[deltas[n] for n in names],
            *[new_m[n] for n in names], *[new_v[n] for n in names])
```

```python
import jax
import jax.numpy as jnp
from jax import lax
from jax.experimental import pallas as pl
from jax.experimental.pallas import tpu as pltpu

F32, BF16 = jnp.float32, jnp.bfloat16
T, D = 2048, 1024
NH, DH = 8, 64
AW, SW = 512, 512
NG, GS, NP = 32, 16, 64
DFF = 2816
NDEV = 8
CS = 2 * DFF // NDEV
EPS = 1e-6
NEG = -1e30
ZC = 4 * 512 + 128
SEG = 256
NSEG = T // SEG
_SCAN_UNROLL = 4
BQ = 512
VMEM_LIMIT = 56 * 1024 * 1024
LANES = 128
MESH = pl.DeviceIdType.MESH
HI = lax.Precision.HIGHEST

ADAM_LR, ADAM_B1, ADAM_B2, ADAM_EPS, ADAM_WD, ADAM_STEP = 0.001, 0.9, 0.999, 1e-08, 0.01, 10


def _cp(dims=None):
    if dims is None:
        return pltpu.CompilerParams(vmem_limit_bytes=VMEM_LIMIT)
    return pltpu.CompilerParams(dimension_semantics=dims, vmem_limit_bytes=VMEM_LIMIT)


def _sds(shape, dtype=F32):
    return jax.ShapeDtypeStruct(shape, dtype)


def _nt(a, b):
    return lax.dot_general(a, b, (((1,), (1,)), ((), ())), preferred_element_type=F32)


def _tn(a, b):
    return lax.dot_general(a, b, (((0,), (0,)), ((), ())), preferred_element_type=F32)


def _nn(a, b):
    return jnp.dot(a, b, preferred_element_type=F32)


def _rms_r(x):
    return lax.rsqrt(jnp.mean(x * x, axis=-1, keepdims=True) + EPS)


def _rms_bwd(n, r, g, dh):
    dn = dh * g
    dx = r * (dn - n * jnp.mean(dn * n, axis=-1, keepdims=True))
    return dx, jnp.sum(dh * n, axis=0, keepdims=True)


def _row_tile(rows, cols, limit):
    tr = rows
    while tr * cols * 4 > limit and tr % 32 == 0:
        tr //= 2
    return tr


def _place():
    return lax.axis_index("x"), lax.axis_index("y"), lax.axis_index("c")


def _all_gather(shards, name):
    n = len(shards)

    def body(*refs):
        x_refs, out_refs = refs[:n], refs[n:2 * n]
        send_sems, recv_sems, local_sems = refs[2 * n:]
        x, y, c = _place()
        me, sibling = (x, y, c), (x, y, 1 - c)
        chips = [(1 - x, y), (x, 1 - y), (1 - x, 1 - y)]

        def copy(a, k, block, to, src=None):
            px, py, pc = block
            slot = out_refs[a].at[4 * px + 2 * py + pc]
            return pltpu.make_async_remote_copy(
                src_ref=slot if src is None else src, dst_ref=slot,
                send_sem=send_sems.at[7 * a + k], recv_sem=recv_sems.at[7 * a + k], device_id=to, device_id_type=MESH)

        mine, first, passed = [], [], []
        for a in range(n):
            cp = pltpu.make_async_copy(x_refs[a], out_refs[a].at[4 * x + 2 * y + c], local_sems.at[a])
            cp.start()
            mine.append(cp)
            cps = [copy(a, 0, me, sibling, src=x_refs[a])]
            cps += [copy(a, 1 + j, me, (*chip, c), src=x_refs[a]) for j, chip in enumerate(chips)]
            for cp in cps:
                cp.start()
            first += cps
        for a in range(n):
            for j, chip in enumerate(chips):
                copy(a, 1 + j, (*chip, c), me).wait_recv()
                fwd = copy(a, 4 + j, (*chip, c), sibling)
                fwd.start()
                passed.append(fwd)
        for a in range(n):
            copy(a, 0, sibling, me).wait_recv()
            for j, chip in enumerate(chips):
                copy(a, 4 + j, (*chip, 1 - c), me).wait_recv()
        for cp in first + passed:
            cp.wait_send()
        for cp in mine:
            cp.wait()

    anyspec = pl.BlockSpec(memory_space=pl.ANY)
    return pl.pallas_call(
        body, name=name,
        out_shape=[_sds((NDEV, *s.shape), s.dtype) for s in shards],
        in_specs=[anyspec] * n, out_specs=[anyspec] * n,
        scratch_shapes=[pltpu.SemaphoreType.DMA((7 * n,)), pltpu.SemaphoreType.DMA((7 * n,)),
                        pltpu.SemaphoreType.DMA((n,))],
    )(*shards)


def _gather_first_copies(x_refs, out_refs, send_sems, recv_sems, local_sems):
    x, y, c = _place()
    slot = 4 * x + 2 * y + c
    peers = [(x, y, 1 - c), (1 - x, y, c), (x, 1 - y, c), (1 - x, 1 - y, c)]
    local, remote = [], []
    for a, (x_ref, out_ref) in enumerate(zip(x_refs, out_refs)):
        local.append(pltpu.make_async_copy(x_ref, out_ref.at[slot], local_sems.at[a]))
        for k, peer in enumerate(peers):
            remote.append(pltpu.make_async_remote_copy(
                src_ref=x_ref, dst_ref=out_ref.at[slot], send_sem=send_sems.at[4 * a + k],
                recv_sem=recv_sems.at[4 * a + k], device_id=peer, device_id_type=MESH))
    return local, remote


def _gather_first_scratch(n):
    return [pltpu.SemaphoreType.DMA((4 * n,)), pltpu.SemaphoreType.DMA((4 * n,)), pltpu.SemaphoreType.DMA((n,))]


def _chip_swap_copies(p_refs, out_refs, send_sems, recv_sems):
    x, y, c = _place()
    chips = [(1 - x, y), (x, 1 - y), (1 - x, 1 - y)]
    cps = []
    for a, (p_ref, out_ref) in enumerate(zip(p_refs, out_refs)):
        for j, (px, py) in enumerate(chips):
            cps.append(pltpu.make_async_remote_copy(
                src_ref=p_ref.at[2 * px + py], dst_ref=out_ref.at[j],
                send_sem=send_sems.at[3 * a + j], recv_sem=recv_sems.at[3 * a + j],
                device_id=(px, py, c), device_id_type=MESH))
    return cps


def _sibling_swap_copies(a_refs, out_refs, send_sems, recv_sems):
    x, y, c = _place()
    cps = []
    for a, (a_ref, out_ref) in enumerate(zip(a_refs, out_refs)):
        for k in range(4):
            cps.append(pltpu.make_async_remote_copy(
                src_ref=a_ref.at[k, 1 - c], dst_ref=out_ref.at[k],
                send_sem=send_sems.at[4 * a + k], recv_sem=recv_sems.at[4 * a + k],
                device_id=(x, y, 1 - c), device_id_type=MESH))
    return cps


def _pass_on_copies(buf_refs, send_sems, recv_sems):
    x, y, c = _place()
    chips = [(1 - x, y), (x, 1 - y), (1 - x, 1 - y)]
    cps = []
    for a, buf in enumerate(buf_refs):
        for j, (px, py) in enumerate(chips):
            block = buf.at[4 * px + 2 * py + c]
            cps.append(pltpu.make_async_remote_copy(
                src_ref=block, dst_ref=block, send_sem=send_sems.at[3 * a + j], recv_sem=recv_sems.at[3 * a + j],
                device_id=(x, y, 1 - c), device_id_type=MESH))
    return cps


def _swap_sibling(arrs, name):
    n = len(arrs)

    def body(*refs):
        cps = _sibling_swap_copies(refs[:n], refs[n:2 * n], *refs[2 * n:])
        for cp in cps:
            cp.start()
        for cp in cps:
            cp.wait()

    anyspec = pl.BlockSpec(memory_space=pl.ANY)
    return pl.pallas_call(
        body, name=name,
        out_shape=[_sds((4, *a.shape[2:]), a.dtype) for a in arrs],
        in_specs=[anyspec] * n, out_specs=[anyspec] * n,
        scratch_shapes=[pltpu.SemaphoreType.DMA((4 * n,)), pltpu.SemaphoreType.DMA((4 * n,))],
    )(*arrs)


def _in_proj(x, g, wcat, shards):
    tm = 256
    n = len(shards)

    def body(x_ref, g_ref, w_ref, *rest):
        x_refs, (h_ref, z_ref, ft_ref), out_refs, sems = rest[:n], rest[n:n + 3], rest[n + 3:2 * n + 3], rest[2 * n + 3:]

        @pl.when(pl.program_id(0) == 0)
        def _():
            local, remote = _gather_first_copies(x_refs, out_refs, *sems)
            for cp in local + remote:
                cp.start()

        xv = x_ref[...]
        h = (xv * _rms_r(xv) * g_ref[...]).astype(BF16)
        h_ref[...] = h
        z = _nt(h, w_ref[...])
        z_ref[...] = z
        ft_ref[...] = z[:, 2048:ZC].T[0:NH, :]

        @pl.when(pl.program_id(0) == T // tm - 1)
        def _():
            local, remote = _gather_first_copies(x_refs, out_refs, *sems)
            for cp in remote + local:
                cp.wait()

    row = lambda i: (i, 0)
    fixed = lambda i: (0, 0)
    anyspec = pl.BlockSpec(memory_space=pl.ANY)
    res = pl.pallas_call(
        body, name="in_proj", grid=(T // tm,),
        in_specs=[pl.BlockSpec((tm, D), row), pl.BlockSpec((1, D), fixed), pl.BlockSpec((ZC, D), fixed)] + [anyspec] * n,
        out_specs=[pl.BlockSpec((tm, D), row), pl.BlockSpec((tm, ZC), row), pl.BlockSpec((NH, tm), lambda i: (0, i))]
        + [anyspec] * n,
        out_shape=[_sds((T, D), BF16), _sds((T, ZC)), _sds((NH, T))] + [_sds((NDEV, *s.shape), s.dtype) for s in shards],
        scratch_shapes=_gather_first_scratch(n),
        compiler_params=_cp(("arbitrary",)),
    )(x, g, wcat, *shards)
    return res[0], res[1], res[2], res[3:]


def _in_proj_dw(h, dq, dk, dv, du, df):
    tm = 512

    def body(h_ref, dq_ref, dk_ref, dv_ref, du_ref, df_ref, o_ref):
        hv = h_ref[...]
        for j, d_ref in enumerate((dq_ref, dk_ref, dv_ref, du_ref)):
            o_ref[512 * j:512 * (j + 1), :] = _tn(d_ref[...], hv)
        o_ref[2048:ZC, :] = _tn(df_ref[...], hv)

    full = lambda w: pl.BlockSpec((T, w), lambda i: (0, 0))
    return pl.pallas_call(
        body, name="in_proj_dw", grid=(D // tm,),
        in_specs=[pl.BlockSpec((T, tm), lambda i: (0, i)), full(512), full(512), full(512), full(512), full(LANES)],
        out_specs=pl.BlockSpec((ZC, tm), lambda i: (0, i)), out_shape=_sds((ZC, D)),
        compiler_params=_cp(("parallel",)),
    )(h, dq, dk, dv, du, df)


def _in_proj_bwd(dq, dk, dv, du, df, wcat, x, g, dres, parts):
    tm = 256
    n = len(parts)

    def body(dq_ref, dk_ref, dv_ref, du_ref, df_ref, w_ref, x_ref, g_ref, dres_ref, *rest):
        p_refs, (dx_ref, dg_ref), fc_refs, sems = rest[:n], rest[n:n + 2], rest[n + 2:2 * n + 2], rest[2 * n + 2:]

        @pl.when(pl.program_id(0) == 0)
        def _():
            for cp in _chip_swap_copies(p_refs, fc_refs, *sems):
                cp.start()

        dh = _nn(df_ref[...], w_ref[2048:ZC, :])
        for j, d_ref in enumerate((dq_ref, dk_ref, dv_ref, du_ref)):
            dh = dh + _nn(d_ref[...], w_ref[512 * j:512 * (j + 1), :])
        xv = x_ref[...]
        r = _rms_r(xv)
        dxr, dgp = _rms_bwd(xv * r, r, g_ref[...], dh)
        dx_ref[...] = dres_ref[...] + dxr

        @pl.when(pl.program_id(0) == 0)
        def _():
            dg_ref[...] = jnp.zeros_like(dg_ref)
        dg_ref[...] += dgp

        @pl.when(pl.program_id(0) == T // tm - 1)
        def _():
            for cp in _chip_swap_copies(p_refs, fc_refs, *sems):
                cp.wait()

    row = lambda i: (i, 0)
    fixed = lambda i: (0, 0)
    part = pl.BlockSpec((tm, 512), row)
    anyspec = pl.BlockSpec(memory_space=pl.ANY)
    res = pl.pallas_call(
        body, name="in_proj_bwd", grid=(T // tm,),
        in_specs=[part, part, part, part, pl.BlockSpec((tm, LANES), row), pl.BlockSpec((ZC, D), fixed),
                  pl.BlockSpec((tm, D), row), pl.BlockSpec((1, D), fixed), pl.BlockSpec((tm, D), row)] + [anyspec] * n,
        out_specs=[pl.BlockSpec((tm, D), row), pl.BlockSpec((1, D), fixed)] + [anyspec] * n,
        out_shape=[_sds((T, D)), _sds((1, D))] + [_sds((3, *p.shape[1:]), p.dtype) for p in parts],
        scratch_shapes=[pltpu.SemaphoreType.DMA((3 * n,)), pltpu.SemaphoreType.DMA((3 * n,))],
        compiler_params=_cp(("arbitrary",)),
    )(dq, dk, dv, du, df, wcat, x, g, dres, *parts)
    return res[0], res[1], res[2:]


def _stack_lanes(v):
    return jnp.concatenate([v[:, LANES * b:LANES * (b + 1)] for b in range(T // LANES)], axis=0)


def _unstack_lanes(s):
    return jnp.concatenate([s[8 * b:8 * b + 8, :] for b in range(T // LANES)], axis=1)


def _cumsum_lanes(v, reverse):
    st = _stack_lanes(v)
    ri = lax.broadcasted_iota(jnp.int32, (LANES, LANES), 0)
    ci = lax.broadcasted_iota(jnp.int32, (LANES, LANES), 1)
    tri = (ri >= ci) if reverse else (ri <= ci)
    intra = jnp.dot(st, tri.astype(F32), precision=HI, preferred_element_type=F32)
    tot = intra[:, 0:1] if reverse else intra[:, LANES - 1:LANES]
    same_head = (ci % 8) == (ri % 8)
    other = (ci // 8 > ri // 8) if reverse else (ci // 8 < ri // 8)
    off = jnp.dot((same_head & other).astype(F32), jnp.broadcast_to(tot, (LANES, LANES)), precision=HI,
                  preferred_element_type=F32)
    return _unstack_lanes(intra + off)


def _gates_fwd(f_t, b_f):
    def body(f_ref, b_ref, c_ref):
        z = f_ref[...] + b_ref[...]
        lf = jnp.minimum(z, 0.0) - jnp.log(1.0 + jnp.exp(-jnp.abs(z)))
        c_ref[...] = _cumsum_lanes(lf, reverse=False)

    return pl.pallas_call(body, name="gates_fwd", out_shape=_sds((NH, T)), compiler_params=_cp())(f_t, b_f)


def _gates_bwd(dck, f_t, b_f):
    def body(d_ref, f_ref, b_ref, df_ref, db_ref):
        z = f_ref[...] + b_ref[...]
        dlf = _cumsum_lanes(d_ref[...], reverse=True)
        df = dlf * jax.nn.sigmoid(-z)
        df_ref[...] = df
        db_ref[...] = jnp.sum(df, axis=1, keepdims=True)

    return pl.pallas_call(body, name="gates_bwd", out_shape=[_sds((NH, T)), _sds((NH, 1))],
                          compiler_params=_cp())(dck, f_t, b_f)


def _lower_triangle():
    rows = lax.broadcasted_iota(jnp.int32, (BQ, BQ), 0)
    cols = lax.broadcasted_iota(jnp.int32, (BQ, BQ), 1)
    return cols <= rows


def _logit_parts(qb, kb, ck_row, lo, hi, tri):
    parts = []
    if lo > 0:
        parts.append((_nt(qb[lo:hi], kb[0:lo]) - ck_row[:, 0:lo], 0, lo))
    parts.append((jnp.where(tri, _nt(qb[lo:hi], kb[lo:hi]) - ck_row[:, lo:hi], NEG), lo, hi))
    return parts


_HP = NH // 2


def _qkv_specs():
    return [pl.BlockSpec((T, LANES), lambda p: (0, p)), pl.BlockSpec((T, LANES), lambda p: (0, _HP + p)),
            pl.BlockSpec((T, LANES), lambda p: (0, 2 * _HP + p))]


def _attn_fwd(z, ck, g_q, g_k, shards, bufs):
    scale = DH ** -0.5
    n, m = len(shards), len(bufs)

    def body(q_ref, k_ref, v_ref, ck_ref, gq_ref, gk_ref, *rest):
        x_refs, (o_ref, lse_ref) = rest[:n], rest[n + m:n + m + 2]
        out_refs, buf_refs, sems = rest[n + m + 2:2 * n + m + 2], rest[2 * n + m + 2:2 * n + 2 * m + 2], rest[2 * n + 2 * m + 2:]

        @pl.when(pl.program_id(0) == 0)
        def _():
            local, remote = _gather_first_copies(x_refs, out_refs, *sems[:3])
            for cp in local + remote + _pass_on_copies(buf_refs, *sems[3:]):
                cp.start()

        qb, kb, vb = [], [], []
        for hh in range(2):
            cols = slice(DH * hh, DH * (hh + 1))
            qv, kv = q_ref[:, cols], k_ref[:, cols]
            qb.append((qv * _rms_r(qv) * gq_ref[...] * scale).astype(BF16))
            kb.append((kv * _rms_r(kv) * gk_ref[...]).astype(BF16))
            vb.append(v_ref[:, cols].astype(BF16))
        tri = _lower_triangle()
        for i in range(T // BQ):
            lo, hi = i * BQ, (i + 1) * BQ
            outs = []
            for hh in range(2):
                parts = _logit_parts(qb[hh], kb[hh], ck_ref[0, hh:hh + 1, :], lo, hi, tri)
                top = jnp.max(parts[-1][0], axis=-1, keepdims=True)
                if lo > 0:
                    top = jnp.maximum(top, jnp.max(parts[0][0], axis=-1, keepdims=True))
                l, o = 0.0, 0.0
                for s, k0, k1 in parts:
                    p = jnp.exp(s - top)
                    l = l + jnp.sum(p, axis=-1, keepdims=True)
                    o = o + _nn(p.astype(BF16), vb[hh][k0:k1])
                outs.append(o / l)
                lse_ref[0, lo:hi, hh:hh + 1] = top + jnp.log(l)
            o_ref[lo:hi, :] = jnp.concatenate(outs, axis=1)

        @pl.when(pl.program_id(0) == _HP - 1)
        def _():
            local, remote = _gather_first_copies(x_refs, out_refs, *sems[:3])
            for cp in remote + local + _pass_on_copies(buf_refs, *sems[3:]):
                cp.wait()

    fixed = lambda p: (0, 0)
    anyspec = pl.BlockSpec(memory_space=pl.ANY)
    res = pl.pallas_call(
        body, name="attn_fwd", grid=(_HP,),
        in_specs=_qkv_specs() + [pl.BlockSpec((1, 2, T), lambda p: (p, 0, 0)), pl.BlockSpec((1, DH), fixed),
                                 pl.BlockSpec((1, DH), fixed)] + [anyspec] * (n + m),
        out_specs=[pl.BlockSpec((T, LANES), lambda p: (0, p)), pl.BlockSpec((1, T, 2), lambda p: (p, 0, 0))]
        + [anyspec] * (n + m),
        out_shape=[_sds((T, AW)), _sds((_HP, T, 2))] + [_sds((NDEV, *s.shape), s.dtype) for s in shards]
        + [_sds(b.shape, b.dtype) for b in bufs],
        input_output_aliases={6 + n + i: 2 + n + i for i in range(m)},
        scratch_shapes=_gather_first_scratch(n) + [pltpu.SemaphoreType.DMA((3 * m,)), pltpu.SemaphoreType.DMA((3 * m,))],
        compiler_params=_cp(("arbitrary",)),
    )(z, z, z, ck, g_q, g_k, *shards, *bufs)
    return res[0], res[1], res[2:2 + n], res[2 + n:]


def _attn_bwd(z, ck, g_q, g_k, lse, datt, parts):
    scale = DH ** -0.5
    n = len(parts)

    def body(q_ref, k_ref, v_ref, ck_ref, gq_ref, gk_ref, lse_ref, do_ref, *rest):
        p_refs, rest = rest[:n], rest[n:]
        dq_ref, dk_ref, dv_ref, dck_ref, dgq_ref, dgk_ref = rest[:6]
        fc_refs, (dkn_acc, dv_acc, dc_acc), sems = rest[6:6 + n], rest[6 + n:9 + n], rest[9 + n:]

        @pl.when(pl.program_id(0) == 0)
        def _():
            for cp in _chip_swap_copies(p_refs, fc_refs, *sems):
                cp.start()

        nq, nk, rq, rk, qb, kb, vb, dob = [], [], [], [], [], [], [], []
        for hh in range(2):
            cols = slice(DH * hh, DH * (hh + 1))
            qv, kv = q_ref[:, cols], k_ref[:, cols]
            rq.append(_rms_r(qv))
            rk.append(_rms_r(kv))
            nq.append(qv * rq[hh])
            nk.append(kv * rk[hh])
            qb.append((nq[hh] * gq_ref[...] * scale).astype(BF16))
            kb.append((nk[hh] * gk_ref[...]).astype(BF16))
            vb.append(v_ref[:, cols].astype(BF16))
            dob.append(do_ref[:, cols].astype(BF16))
        dkn_acc[...] = jnp.zeros_like(dkn_acc)
        dv_acc[...] = jnp.zeros_like(dv_acc)
        dc_acc[...] = jnp.zeros_like(dc_acc)
        dgq = jnp.zeros((1, DH), F32)
        tri = _lower_triangle()
        for i in range(T // BQ):
            lo, hi = i * BQ, (i + 1) * BQ
            dqs = []
            for hh in range(2):
                lse = lse_ref[0, lo:hi, hh:hh + 1]
                pd = []
                for s, k0, k1 in _logit_parts(qb[hh], kb[hh], ck_ref[0, hh:hh + 1, :], lo, hi, tri):
                    pd.append((jnp.exp(s - lse), _nt(dob[hh][lo:hi], vb[hh][k0:k1]), k0, k1))
                mean = sum(jnp.sum(p * dp, axis=-1, keepdims=True) for p, dp, _, _ in pd)
                dqn = 0.0
                for p, dp, k0, k1 in pd:
                    ds = p * (dp - mean)
                    dsb = ds.astype(BF16)
                    dv_acc[hh, k0:k1, :] += _tn(p.astype(BF16), dob[hh][lo:hi])
                    dkn_acc[hh, k0:k1, :] += _tn(dsb, qb[hh][lo:hi])
                    dc_acc[hh:hh + 1, k0:k1] -= jnp.sum(ds, axis=0, keepdims=True)
                    dqn = dqn + _nn(dsb, kb[hh][k0:k1])
                dqx, dgp = _rms_bwd(nq[hh][lo:hi], rq[hh][lo:hi], gq_ref[...], dqn * scale)
                dqs.append(dqx)
                dgq = dgq + dgp
            dq_ref[lo:hi, :] = jnp.concatenate(dqs, axis=1).astype(BF16)
        dks, dgk = [], jnp.zeros((1, DH), F32)
        for hh in range(2):
            dkx, dgp = _rms_bwd(nk[hh], rk[hh], gk_ref[...], dkn_acc[hh])
            dks.append(dkx)
            dgk = dgk + dgp
        dk_ref[...] = jnp.concatenate(dks, axis=1).astype(BF16)
        dv_ref[...] = jnp.concatenate([dv_acc[0], dv_acc[1]], axis=1).astype(BF16)
        dck_ref[0] = dc_acc[...]

        @pl.when(pl.program_id(0) == 0)
        def _():
            dgq_ref[...] = jnp.zeros_like(dgq_ref)
            dgk_ref[...] = jnp.zeros_like(dgk_ref)
        dgq_ref[...] += dgq
        dgk_ref[...] += dgk

        @pl.when(pl.program_id(0) == _HP - 1)
        def _():
            for cp in _chip_swap_copies(p_refs, fc_refs, *sems):
                cp.wait()

    fixed = lambda p: (0, 0)
    pair = pl.BlockSpec((T, LANES), lambda p: (0, p))
    gsp = pl.BlockSpec((1, DH), fixed)
    ckspec = pl.BlockSpec((1, 2, T), lambda p: (p, 0, 0))
    anyspec = pl.BlockSpec(memory_space=pl.ANY)
    res = pl.pallas_call(
        body, name="attn_bwd", grid=(_HP,),
        in_specs=_qkv_specs() + [ckspec, gsp, gsp, pl.BlockSpec((1, T, 2), lambda p: (p, 0, 0)), pair] + [anyspec] * n,
        out_specs=[pair, pair, pair, ckspec, gsp, gsp] + [anyspec] * n,
        out_shape=[_sds((T, AW), BF16)] * 3 + [_sds((_HP, 2, T)), _sds((1, DH)), _sds((1, DH))]
        + [_sds((3, *p.shape[1:]), p.dtype) for p in parts],
        scratch_shapes=[pltpu.VMEM((2, T, DH), F32), pltpu.VMEM((2, T, DH), F32), pltpu.VMEM((2, T), F32),
                        pltpu.SemaphoreType.DMA((3 * n,)), pltpu.SemaphoreType.DMA((3 * n,))],
        compiler_params=_cp(("arbitrary",)),
    )(z, z, z, ck, g_q, g_k, lse, datt, *parts)
    return (*res[:6], res[6:])


def _disc(lr, li, ls, br_t, bi_t):
    step = jnp.exp(ls)
    er = jnp.exp(lr * step)
    ab_re = er * jnp.cos(li * step)
    ab_im = er * jnp.sin(li * step)
    num_re = ab_re - 1.0
    num_im = ab_im
    den = lr * lr + li * li
    f_re = (num_re * lr + num_im * li) / den
    f_im = (num_im * lr - num_re * li) / den
    bb_re = f_re[:, None, :] * br_t - f_im[:, None, :] * bi_t
    bb_im = f_re[:, None, :] * bi_t + f_im[:, None, :] * br_t
    return ab_re, ab_im, bb_re, bb_im


def _disc_fwd(lr, li, ls, br_t, bi_t):
    def body(lr_ref, li_ref, ls_ref, br_ref, bi_ref, ar_ref, ai_ref, bbr_ref, bbi_ref):
        ar, ai, bbr, bbi = _disc(lr_ref[...], li_ref[...], ls_ref[...], br_ref[...], bi_ref[...])
        ar_ref[...] = ar
        ai_ref[...] = ai
        bbr_ref[...] = bbr
        bbi_ref[...] = bbi

    return pl.pallas_call(
        body, name="ssm_disc_fwd",
        out_shape=[_sds((NG, NP)), _sds((NG, NP)), _sds((NG, GS, NP)), _sds((NG, GS, NP))],
        compiler_params=_cp())(lr, li, ls, br_t, bi_t)


def _disc_bwd(lr, li, ls, br_t, bi_t, dar, dai, dbbr, dbbi):
    def body(lr_ref, li_ref, ls_ref, br_ref, bi_ref, dar_ref, dai_ref, dbbr_ref, dbbi_ref,
             dlr_ref, dli_ref, dls_ref, dbr_ref, dbi_ref):
        _, vjp = jax.vjp(_disc, lr_ref[...], li_ref[...], ls_ref[...], br_ref[...], bi_ref[...])
        dlr, dli, dls, dbr, dbi = vjp((dar_ref[...], dai_ref[...], dbbr_ref[...], dbbi_ref[...]))
        dlr_ref[...] = dlr
        dli_ref[...] = dli
        dls_ref[...] = dls
        dbr_ref[...] = dbr
        dbi_ref[...] = dbi

    return pl.pallas_call(
        body, name="ssm_disc_bwd",
        out_shape=[_sds((NG, NP)), _sds((NG, NP)), _sds((NG, 1)), _sds((NG, GS, NP)), _sds((NG, GS, NP))],
        compiler_params=_cp())(lr, li, ls, br_t, bi_t, dar, dai, dbbr, dbbi)


def _cmul(ar, ai, br, bi):
    return ar * br - ai * bi, ar * bi + ai * br


def _scan(buf, a_re, a_im, reverse, other=None):
    ar = [jnp.broadcast_to(a_re[:, LANES * k:LANES * (k + 1)], (NSEG, LANES)) for k in range(4)]
    ai = [jnp.broadcast_to(a_im[:, LANES * k:LANES * (k + 1)], (NSEG, LANES)) for k in range(4)]

    def tile(ref, i, k):
        return ref[pl.ds(pl.multiple_of(i * NSEG, NSEG), NSEG), LANES * k:LANES * (k + 1)]

    def advance(i, xr, xi):
        nr = [ar[k] * xr[k] - ai[k] * xi[k] + tile(buf, i, k) for k in range(4)]
        ni = [ar[k] * xi[k] + ai[k] * xr[k] + tile(buf, i, 4 + k) for k in range(4)]
        return nr, ni

    def index(i):
        return (SEG - 1 - i) if reverse else i

    zeros = tuple(jnp.zeros((NSEG, LANES), F32) for _ in range(4))

    def sweep1(i, carry):
        xr, xi = carry
        for sub in range(_SCAN_UNROLL):
            xr, xi = advance(index(i * _SCAN_UNROLL + sub), xr, xi)
        return tuple(xr), tuple(xi)

    er, ei = lax.fori_loop(0, SEG // _SCAN_UNROLL, sweep1, (zeros, zeros))

    row = lax.broadcasted_iota(jnp.int32, (NSEG, LANES), 0)
    sr, si = [], []
    for k in range(4):
        pr, pi = ar[k], ai[k]
        for _ in range(SEG.bit_length() - 1):
            pr, pi = _cmul(pr, pi, pr, pi)
        ir, ii = er[k], ei[k]
        for d in (1, 2, 4):
            shift = (NSEG - d) if reverse else d
            ok = (row < NSEG - d) if reverse else (row >= d)
            mr, mi = _cmul(pr, pi, pltpu.roll(ir, shift, 0), pltpu.roll(ii, shift, 0))
            ir = ir + jnp.where(ok, mr, 0.0)
            ii = ii + jnp.where(ok, mi, 0.0)
            pr, pi = _cmul(pr, pi, pr, pi)
        shift = (NSEG - 1) if reverse else 1
        ok = (row < NSEG - 1) if reverse else (row >= 1)
        sr.append(jnp.where(ok, pltpu.roll(ir, shift, 0), 0.0))
        si.append(jnp.where(ok, pltpu.roll(ii, shift, 0), 0.0))

    def sweep2(i, carry):
        xr, xi, dar, dai = carry
        for sub in range(_SCAN_UNROLL):
            idx = index(i * _SCAN_UNROLL + sub)
            if other is not None:
                orr = [tile(other, idx, k) for k in range(4)]
                oi = [tile(other, idx, 4 + k) for k in range(4)]
                dar = tuple(dar[k] + xr[k] * orr[k] + xi[k] * oi[k] for k in range(4))
                dai = tuple(dai[k] + xi[k] * orr[k] - xr[k] * oi[k] for k in range(4))
            xr, xi = advance(idx, xr, xi)
            start = pl.multiple_of(idx * NSEG, NSEG)
            for k in range(4):
                buf[pl.ds(start, NSEG), LANES * k:LANES * (k + 1)] = xr[k]
                buf[pl.ds(start, NSEG), LANES * (4 + k):LANES * (5 + k)] = xi[k]
        return tuple(xr), tuple(xi), dar, dai

    _, _, dar, dai = lax.fori_loop(0, SEG // _SCAN_UNROLL, sweep2, (tuple(sr), tuple(si), zeros, zeros))
    if other is None:
        return None
    da_re = jnp.concatenate([jnp.sum(d, axis=0, keepdims=True) for d in dar], axis=1)
    da_im = jnp.concatenate([jnp.sum(d, axis=0, keepdims=True) for d in dai], axis=1)
    return da_re, da_im


_SSM_BLOCKS = 4


def _ssm_fwd(u, bb, ab, cc, dsk, shards):
    n = len(shards)

    def body(u_ref, bb_ref, a_ref, cc_ref, d_ref, *rest):
        x_refs, y_ref, xs_ref, out_refs, sems = rest[:n], rest[n], rest[n + 1], rest[n + 2:2 * n + 2], rest[2 * n + 2:]

        @pl.when(pl.program_id(0) == 0)
        def _():
            local, remote = _gather_first_copies(x_refs, out_refs, *sems)
            for cp in local + remote:
                cp.start()

        ub = u_ref[...]
        xb = xs_ref.at[0]
        xb[...] = _nn(ub.astype(BF16), bb_ref[0])
        _scan(xb, a_ref[0, 0:1, :], a_ref[0, 1:2, :], reverse=False)
        y_ref[...] = _nn(xb[...].astype(BF16), cc_ref[0]) + d_ref[0] * ub

        @pl.when(pl.program_id(0) == _SSM_BLOCKS - 1)
        def _():
            local, remote = _gather_first_copies(x_refs, out_refs, *sems)
            for cp in remote + local:
                cp.wait()

    blk = lambda j: (j, 0, 0)
    col = pl.BlockSpec((T, LANES), lambda j: (0, j))
    anyspec = pl.BlockSpec(memory_space=pl.ANY)
    res = pl.pallas_call(
        body, name="ssm_fwd", grid=(_SSM_BLOCKS,),
        in_specs=[col, pl.BlockSpec((1, LANES, 1024), blk), pl.BlockSpec((1, 2, 512), blk),
                  pl.BlockSpec((1, 1024, LANES), blk), pl.BlockSpec((1, 1, LANES), blk)] + [anyspec] * n,
        out_specs=[col, pl.BlockSpec((1, T, 1024), blk)] + [anyspec] * n,
        out_shape=[_sds((T, SW)), _sds((_SSM_BLOCKS, T, 1024))] + [_sds((NDEV, *s.shape), s.dtype) for s in shards],
        scratch_shapes=_gather_first_scratch(n),
        compiler_params=_cp(("arbitrary",)),
    )(u, bb, ab, cc, dsk, *shards)
    return res[0], res[1], res[2:]


def _ssm_bwd(u, dy, xs, bb, ab, cc, dsk, arrs):
    n = len(arrs)

    def body(u_ref, dy_ref, x_ref, bb_ref, a_ref, cc_ref, d_ref, *rest):
        a_refs, rest = rest[:n], rest[n:]
        du_ref, dbb_ref, da_ref, dcc_ref, dd_ref = rest[:5]
        fs_refs, gb, sems = rest[5:5 + n], rest[5 + n], rest[6 + n:]

        @pl.when(pl.program_id(0) == 0)
        def _():
            for cp in _sibling_swap_copies(a_refs, fs_refs, *sems):
                cp.start()

        ub, dyv = u_ref[...], dy_ref[...]
        ubb, dyb = ub.astype(BF16), dyv.astype(BF16)
        a_re, a_im = a_ref[0, 0:1, :], a_ref[0, 1:2, :]
        dcc_ref[0] = _tn(x_ref[0].astype(BF16), dyb)
        gb[...] = _nt(dyb, cc_ref[0])
        da_re, da_im = _scan(gb, a_re, -a_im, reverse=True, other=x_ref.at[0])
        da_ref[0] = jnp.concatenate([da_re, da_im], axis=0)
        gc = gb[...].astype(BF16)
        du_ref[...] = _nt(gc, bb_ref[0]) + d_ref[0] * dyv
        dbb_ref[0] = _tn(ubb, gc)
        dd_ref[0] = jnp.sum(dyv * ub, axis=0, keepdims=True)

        @pl.when(pl.program_id(0) == _SSM_BLOCKS - 1)
        def _():
            for cp in _sibling_swap_copies(a_refs, fs_refs, *sems):
                cp.wait()

    blk = lambda j: (j, 0, 0)
    col = pl.BlockSpec((T, LANES), lambda j: (0, j))
    anyspec = pl.BlockSpec(memory_space=pl.ANY)
    res = pl.pallas_call(
        body, name="ssm_bwd", grid=(_SSM_BLOCKS,),
        in_specs=[col, col, pl.BlockSpec((1, T, 1024), blk), pl.BlockSpec((1, LANES, 1024), blk),
                  pl.BlockSpec((1, 2, 512), blk), pl.BlockSpec((1, 1024, LANES), blk), pl.BlockSpec((1, 1, LANES), blk)]
        + [anyspec] * n,
        out_specs=[col, pl.BlockSpec((1, LANES, 1024), blk), pl.BlockSpec((1, 2, 512), blk),
                   pl.BlockSpec((1, 1024, LANES), blk), pl.BlockSpec((1, 1, LANES), blk)] + [anyspec] * n,
        out_shape=[_sds((T, SW)), _sds((4, LANES, 1024)), _sds((4, 2, 512)), _sds((4, 1024, LANES)), _sds((4, 1, LANES))]
        + [_sds((4, *a.shape[2:]), a.dtype) for a in arrs],
        scratch_shapes=[pltpu.VMEM((T, 1024), F32), pltpu.SemaphoreType.DMA((4 * n,)), pltpu.SemaphoreType.DMA((4 * n,))],
        compiler_params=_cp(("arbitrary",)),
    )(u, dy, xs, bb, ab, cc, dsk, *arrs)
    return (*res[:5], res[5:])


def _interleave(a):
    return a.reshape(NSEG, SEG, a.shape[1]).transpose(1, 0, 2).reshape(a.shape)


def _deinterleave(a):
    return a.reshape(SEG, NSEG, a.shape[1]).transpose(1, 0, 2).reshape(a.shape)


_TM_MIX = 512


def _mix_parts(att, y, wg, bg):
    y2 = jax.nn.gelu(y)
    y2b = y2.astype(BF16)
    sg = jax.nn.sigmoid(_nn(y2b, wg) + bg)
    ssm = y2 * sg
    ra, rs = _rms_r(att), _rms_r(ssm)
    return y2, y2b, sg, ssm, ra, rs


def _mix_fwd(att, y, x, wg, bg, ga, gs, wo, bufs):
    m = len(bufs)

    def body(att_ref, y_ref, x_ref, wg_ref, bg_ref, ga_ref, gs_ref, wo_ref, *rest):
        x1_ref, buf_refs, sems = rest[m], rest[m + 1:2 * m + 1], rest[2 * m + 1:]

        @pl.when(pl.program_id(0) == 0)
        def _():
            for cp in _pass_on_copies(buf_refs, *sems):
                cp.start()

        attv = att_ref[...]
        _, _, _, ssm, ra, rs = _mix_parts(attv, y_ref[...], wg_ref[...], bg_ref[...])
        ma = (attv * ra * ga_ref[...]).astype(BF16)
        ms = (ssm * rs * gs_ref[...]).astype(BF16)
        x1_ref[...] = x_ref[...] + _nn(ma, wo_ref[0:AW, :]) + _nn(ms, wo_ref[AW:D, :])

        @pl.when(pl.program_id(0) == T // _TM_MIX - 1)
        def _():
            for cp in _pass_on_copies(buf_refs, *sems):
                cp.wait()

    row = lambda i: (i, 0)
    fixed = lambda i: (0, 0)
    half = pl.BlockSpec((_TM_MIX, AW), row)
    vec = pl.BlockSpec((1, AW), fixed)
    anyspec = pl.BlockSpec(memory_space=pl.ANY)
    res = pl.pallas_call(
        body, name="mix_fwd", grid=(T // _TM_MIX,),
        in_specs=[half, half, pl.BlockSpec((_TM_MIX, D), row), pl.BlockSpec((SW, SW), fixed), vec, vec, vec,
                  pl.BlockSpec((D, D), fixed)] + [anyspec] * m,
        out_specs=[pl.BlockSpec((_TM_MIX, D), row)] + [anyspec] * m,
        out_shape=[_sds((T, D))] + [_sds(b.shape, b.dtype) for b in bufs],
        input_output_aliases={8 + i: 1 + i for i in range(m)},
        scratch_shapes=[pltpu.SemaphoreType.DMA((3 * m,)), pltpu.SemaphoreType.DMA((3 * m,))],
        compiler_params=_cp(("arbitrary",)),
    )(att, y, x, wg, bg, ga, gs, wo, *bufs)
    return res[0], res[1:]


def _mix_bwd(att, y, dx1, wg, bg, ga, gs, wo):
    last = T // _TM_MIX - 1

    def body(att_ref, y_ref, d_ref, wg_ref, bg_ref, ga_ref, gs_ref, wo_ref,
             datt_ref, dy_ref, dwg_ref, dbg_ref, dga_ref, dgs_ref, dwo_ref, dwg_acc, dwo_acc):
        attv, yv, db = att_ref[...], y_ref[...], d_ref[...].astype(BF16)
        y2, y2b, sg, ssm, ra, rs = _mix_parts(attv, yv, wg_ref[...], bg_ref[...])
        na, ns = attv * ra, ssm * rs
        ma = (na * ga_ref[...]).astype(BF16)
        ms = (ns * gs_ref[...]).astype(BF16)
        dma = _nt(db, wo_ref[0:AW, :])
        dms = _nt(db, wo_ref[AW:D, :])
        datt, dga = _rms_bwd(na, ra, ga_ref[...], dma)
        dssm, dgs = _rms_bwd(ns, rs, gs_ref[...], dms)
        dgl = dssm * y2 * sg * (1.0 - sg)
        dglb = dgl.astype(BF16)
        dy2 = dssm * sg + _nt(dglb, wg_ref[...])
        _, gelu_vjp = jax.vjp(jax.nn.gelu, yv)
        datt_ref[...] = datt
        dy_ref[...] = gelu_vjp(dy2)[0]

        @pl.when(pl.program_id(0) == 0)
        def _():
            dwg_acc[...] = jnp.zeros_like(dwg_acc)
            dwo_acc[...] = jnp.zeros_like(dwo_acc)
            dbg_ref[...] = jnp.zeros_like(dbg_ref)
            dga_ref[...] = jnp.zeros_like(dga_ref)
            dgs_ref[...] = jnp.zeros_like(dgs_ref)
        dwg_acc[...] += _tn(y2b, dglb)
        dbg_ref[...] += jnp.sum(dgl, axis=0, keepdims=True)
        dga_ref[...] += dga
        dgs_ref[...] += dgs
        dwo_acc[0:AW, :] += _tn(ma, db)
        dwo_acc[AW:D, :] += _tn(ms, db)

        @pl.when(pl.program_id(0) == last)
        def _():
            dwg_ref[...] = dwg_acc[...].astype(BF16)
            dwo_ref[...] = dwo_acc[...].astype(BF16)

    row = lambda i: (i, 0)
    fixed = lambda i: (0, 0)
    half = pl.BlockSpec((_TM_MIX, AW), row)
    vec = pl.BlockSpec((1, AW), fixed)
    sq = pl.BlockSpec((SW, SW), fixed)
    big = pl.BlockSpec((D, D), fixed)
    return pl.pallas_call(
        body, name="mix_bwd", grid=(T // _TM_MIX,),
        in_specs=[half, half, pl.BlockSpec((_TM_MIX, D), row), sq, vec, vec, vec, big],
        out_specs=[half, half, sq, vec, vec, vec, big],
        out_shape=[_sds((T, AW)), _sds((T, SW)), _sds((SW, SW), BF16), _sds((1, SW)), _sds((1, AW)), _sds((1, SW)),
                   _sds((D, D), BF16)],
        scratch_shapes=[pltpu.VMEM((SW, SW), F32), pltpu.VMEM((D, D), F32)],
        compiler_params=_cp(("arbitrary",)),
    )(att, y, dx1, wg, bg, ga, gs, wo)


_NCB = -(-CS // LANES)


def _ffn_up(x1, g, wu4):
    tm = 1024

    def body(x_ref, g_ref, w_ref, h_ref, up_ref):
        @pl.when(pl.program_id(1) == 0)
        def _():
            xv = x_ref[...]
            h_ref[...] = (xv * _rms_r(xv) * g_ref[...]).astype(BF16)
        up_ref[0, 0] = _nt(h_ref[...], w_ref[0, 0])

    return pl.pallas_call(
        body, name="ffn_up", grid=(T // tm, NDEV),
        in_specs=[pl.BlockSpec((tm, D), lambda i, e: (i, 0)), pl.BlockSpec((1, D), lambda i, e: (0, 0)),
                  pl.BlockSpec((1, 1, CS, D), lambda i, e: (e // 4, e % 4, 0, 0))],
        out_specs=[pl.BlockSpec((tm, D), lambda i, e: (i, 0)),
                   pl.BlockSpec((1, 1, tm, CS), lambda i, e: (e // 4, e % 4, i, 0))],
        out_shape=[_sds((T, D), BF16), _sds((2, 4, T, CS))],
        compiler_params=_cp(("parallel", "arbitrary")),
    )(x1, g, wu4)


def _shift_down(h, k):
    row = lax.broadcasted_iota(jnp.int32, h.shape, 0)
    return jnp.where(row >= k, pltpu.roll(h, k, 0), 0.0)


def _shift_up(h, k):
    row = lax.broadcasted_iota(jnp.int32, h.shape, 0)
    return jnp.where(row < T - k, pltpu.roll(h, T - k, 0), 0.0)


def _conv(h, w, b):
    return w[0:1, :] * _shift_down(h, 2) + w[1:2, :] * _shift_down(h, 1) + w[2:3, :] * h + b


def _chan(gv, rows):
    return pl.BlockSpec((1, 1, rows, LANES), lambda d, j: (gv, d, 0, j))


def _ffn_act(up4, cw4, cb4):
    def body(ug_ref, uv_ref, wg_ref, wv_ref, bg_ref, bv_ref, act_ref):
        g = _conv(ug_ref[0, 0], wg_ref[0, 0], bg_ref[0, 0])
        v = _conv(uv_ref[0, 0], wv_ref[0, 0], bv_ref[0, 0])
        act_ref[0] = (g * jax.nn.sigmoid(g) * v).astype(BF16)

    return pl.pallas_call(
        body, name="ffn_act", grid=(4, _NCB),
        in_specs=[_chan(0, T), _chan(1, T), _chan(0, 3), _chan(1, 3), _chan(0, 1), _chan(1, 1)],
        out_specs=pl.BlockSpec((1, T, LANES), lambda d, j: (d, 0, j)), out_shape=_sds((4, T, CS), BF16),
        compiler_params=_cp(("parallel", "parallel")),
    )(up4, up4, cw4, cw4, cb4, cb4)


def _ffn_act_bwd(up4, dact, cw4, cb4):
    def conv_bwd(h, w, d):
        dup = w[2:3, :] * d + w[1:2, :] * _shift_up(d, 1) + w[0:1, :] * _shift_up(d, 2)
        dw = jnp.concatenate([jnp.sum(d * _shift_down(h, 2), axis=0, keepdims=True),
                              jnp.sum(d * _shift_down(h, 1), axis=0, keepdims=True),
                              jnp.sum(d * h, axis=0, keepdims=True)], axis=0)
        return dup, dw, jnp.sum(d, axis=0, keepdims=True)

    def body(ug_ref, uv_ref, da_ref, wg_ref, wv_ref, bg_ref, bv_ref, dup_ref, dw_ref, db_ref):
        ug, uv, da = ug_ref[0, 0], uv_ref[0, 0], da_ref[0]
        g = _conv(ug, wg_ref[0, 0], bg_ref[0, 0])
        v = _conv(uv, wv_ref[0, 0], bv_ref[0, 0])
        sg = jax.nn.sigmoid(g)
        dg = da * v * (sg * (1.0 + g * (1.0 - sg)))
        dv = da * (g * sg)
        dug, dwg, dbg = conv_bwd(ug, wg_ref[0, 0], dg)
        duv, dwv, dbv = conv_bwd(uv, wv_ref[0, 0], dv)
        dup_ref[0, 0] = dug.astype(BF16)
        dup_ref[1, 0] = duv.astype(BF16)
        dw_ref[0, 0] = dwg
        dw_ref[1, 0] = dwv
        db_ref[0, 0] = dbg
        db_ref[1, 0] = dbv

    both = lambda rows: pl.BlockSpec((2, 1, rows, LANES), lambda d, j: (0, d, 0, j))
    return pl.pallas_call(
        body, name="ffn_act_bwd", grid=(4, _NCB),
        in_specs=[_chan(0, T), _chan(1, T), pl.BlockSpec((1, T, LANES), lambda d, j: (d, 0, j)),
                  _chan(0, 3), _chan(1, 3), _chan(0, 1), _chan(1, 1)],
        out_specs=[both(T), both(3), both(1)],
        out_shape=[_sds((2, 4, T, CS), BF16), _sds((2, 4, 3, CS)), _sds((2, 4, 1, CS))],
        compiler_params=_cp(("parallel", "parallel")),
    )(up4, up4, dact, cw4, cw4, cb4, cb4)


def _ffn_down_loss(act, wd4, x1, target):
    tm = 512

    def body(a_ref, w_ref, x1_ref, t_ref, loss_ref, dx_ref, dxb_ref):
        x2 = x1_ref[...]
        for d in range(4):
            x2 = x2 + _nn(a_ref[d], w_ref[d])
        err = x2 - t_ref[...]
        dx = err * (1.0 / D)
        dx_ref[...] = dx
        dxb_ref[...] = dx.astype(BF16)

        @pl.when(pl.program_id(0) == 0)
        def _():
            loss_ref[...] = jnp.zeros_like(loss_ref)
        loss_ref[...] += 0.5 * jnp.sum(jnp.mean(err * err, axis=-1, keepdims=True), axis=0, keepdims=True)

    row = lambda i: (i, 0)
    fixed = lambda i: (0, 0)
    return pl.pallas_call(
        body, name="ffn_down_loss", grid=(T // tm,),
        in_specs=[pl.BlockSpec((4, tm, CS), lambda i: (0, i, 0)), pl.BlockSpec((4, CS, D), lambda i: (0, 0, 0)),
                  pl.BlockSpec((tm, D), row), pl.BlockSpec((tm, D), row)],
        out_specs=[pl.BlockSpec((1, 1), fixed), pl.BlockSpec((tm, D), row), pl.BlockSpec((tm, D), row)],
        out_shape=[_sds((1, 1)), _sds((T, D)), _sds((T, D), BF16)],
        compiler_params=_cp(("arbitrary",)),
    )(act, wd4, x1, target)


def _ffn_dact(dx2b, wd4):
    tm = 1024

    def body(d_ref, w_ref, o_ref):
        o_ref[0] = _nt(d_ref[...], w_ref[0])

    return pl.pallas_call(
        body, name="ffn_dact", grid=(4, T // tm),
        in_specs=[pl.BlockSpec((tm, D), lambda d, i: (i, 0)), pl.BlockSpec((1, CS, D), lambda d, i: (d, 0, 0))],
        out_specs=pl.BlockSpec((1, tm, CS), lambda d, i: (d, i, 0)), out_shape=_sds((4, T, CS)),
        compiler_params=_cp(("parallel", "parallel")),
    )(dx2b, wd4)


def _ffn_dwd(act, dx2b):
    def body(a_ref, d_ref, o_ref):
        o_ref[0] = _tn(a_ref[0], d_ref[...]).astype(BF16)

    return pl.pallas_call(
        body, name="ffn_dwd", grid=(4,),
        in_specs=[pl.BlockSpec((1, T, CS), lambda d: (d, 0, 0)), pl.BlockSpec((T, D), lambda d: (0, 0))],
        out_specs=pl.BlockSpec((1, CS, D), lambda d: (d, 0, 0)), out_shape=_sds((4, CS, D), BF16),
        compiler_params=_cp(("parallel",)),
    )(act, dx2b)


def _ffn_dwu(h2, dup4):
    tn = 1024

    def body(h_ref, d_ref, o_ref):
        o_ref[0, 0] = _tn(d_ref[0, 0], h_ref[...]).astype(BF16)

    return pl.pallas_call(
        body, name="ffn_dwu", grid=(NDEV, D // tn),
        in_specs=[pl.BlockSpec((T, tn), lambda e, i: (0, i)),
                  pl.BlockSpec((1, 1, T, CS), lambda e, i: (e // 4, e % 4, 0, 0))],
        out_specs=pl.BlockSpec((1, 1, CS, tn), lambda e, i: (e // 4, e % 4, 0, i)),
        out_shape=_sds((2, 4, CS, D), BF16),
        compiler_params=_cp(("parallel", "parallel")),
    )(h2, dup4)


def _ffn_up_bwd(dup4, wu4, x1, g, dres):
    tm = 256

    def body(d_ref, w_ref, x_ref, g_ref, dres_ref, dx_ref, dg_ref):
        dh = jnp.zeros((tm, D), F32)
        for gv in range(2):
            for d in range(4):
                dh = dh + _nn(d_ref[gv, d], w_ref[gv, d])
        xv = x_ref[...]
        r = _rms_r(xv)
        dxr, dgp = _rms_bwd(xv * r, r, g_ref[...], dh)
        dx_ref[...] = dres_ref[...] + dxr

        @pl.when(pl.program_id(0) == 0)
        def _():
            dg_ref[...] = jnp.zeros_like(dg_ref)
        dg_ref[...] += dgp

    row = lambda i: (i, 0)
    fixed = lambda i: (0, 0)
    return pl.pallas_call(
        body, name="ffn_up_bwd", grid=(T // tm,),
        in_specs=[pl.BlockSpec((2, 4, tm, CS), lambda i: (0, 0, i, 0)), pl.BlockSpec((2, 4, CS, D), lambda i: (0, 0, 0, 0)),
                  pl.BlockSpec((tm, D), row), pl.BlockSpec((1, D), fixed), pl.BlockSpec((tm, D), row)],
        out_specs=[pl.BlockSpec((tm, D), row), pl.BlockSpec((1, D), fixed)],
        out_shape=[_sds((T, D)), _sds((1, D))],
        compiler_params=_cp(("arbitrary",)),
    )(dup4, wu4, x1, g, dres)


def _adamw_math(w, g, m, v):
    m = ADAM_B1 * m + (1.0 - ADAM_B1) * g
    v = ADAM_B2 * v + (1.0 - ADAM_B2) * jnp.square(g)
    m_hat = m / (1.0 - ADAM_B1 ** ADAM_STEP)
    v_hat = v / (1.0 - ADAM_B2 ** ADAM_STEP)
    delta = -ADAM_LR * (m_hat / (jnp.sqrt(v_hat) + ADAM_EPS) + ADAM_WD * w)
    return delta, m, v


def _where_am_i():
    x, y, c = _place()
    return jnp.stack([c, 2 * x + y]).astype(jnp.int32)


def _pair_sum(a4, recv, name):
    _, _, rows, cols = a4.shape
    tr = _row_tile(rows, cols, 3 << 20)

    def body(sel_ref, own_ref, recv_ref, out_ref):
        out_ref[...] = (own_ref[0].astype(F32) + recv_ref[...].astype(F32)).astype(out_ref.dtype)

    grid_spec = pltpu.PrefetchScalarGridSpec(
        num_scalar_prefetch=1, grid=(4, rows // tr),
        in_specs=[pl.BlockSpec((1, 1, tr, cols), lambda k, i, s: (k, s[0], i, 0)),
                  pl.BlockSpec((1, tr, cols), lambda k, i, s: (k, i, 0))],
        out_specs=pl.BlockSpec((1, tr, cols), lambda k, i, s: (k, i, 0)),
    )
    return pl.pallas_call(
        body, name=name, grid_spec=grid_spec, out_shape=_sds((4, rows, cols), a4.dtype),
        compiler_params=_cp(("arbitrary", "arbitrary")),
    )(_where_am_i(), a4, recv)


def _adamw_rs(a4, recv, from_chips, w, m, v, name):
    rows, cols = w.shape
    tr = _row_tile(rows, cols, 3 << 19)

    def body(sel_ref, own_ref, recv_ref, fc_ref, w_ref, m_ref, v_ref, g_ref, d_ref, nm_ref, nv_ref):
        g = own_ref[0, 0].astype(F32) + recv_ref[0].astype(F32)
        for j in range(3):
            g = g + fc_ref[j].astype(F32)
        d, nm, nv = _adamw_math(w_ref[...], g, m_ref[...], v_ref[...])
        g_ref[...] = g
        d_ref[...] = d
        nm_ref[...] = nm
        nv_ref[...] = nv

    flat = pl.BlockSpec((tr, cols), lambda i, s: (i, 0))
    grid_spec = pltpu.PrefetchScalarGridSpec(
        num_scalar_prefetch=1, grid=(rows // tr,),
        in_specs=[pl.BlockSpec((1, 1, tr, cols), lambda i, s: (s[1], s[0], i, 0)),
                  pl.BlockSpec((1, tr, cols), lambda i, s: (s[1], i, 0)),
                  pl.BlockSpec((3, tr, cols), lambda i, s: (0, i, 0)), flat, flat, flat],
        out_specs=[flat] * 4,
    )
    return pl.pallas_call(
        body, name=name, grid_spec=grid_spec, out_shape=[_sds((rows, cols))] * 4,
        compiler_params=_cp(("arbitrary",)),
    )(_where_am_i(), a4, recv, from_chips, w, m, v)


_SMALL_F32 = (("g_mix", (8, 128)), ("b_f", (1, 8)), ("g_q", (1, 64)), ("g_k", (1, 64)), ("lambda_re", (32, 64)),
              ("lambda_im", (32, 64)), ("log_step", (1, 32)), ("d_skip", (32, 16)), ("b_glu", (4, 128)),
              ("g_attn_out", (4, 128)), ("g_ssm_out", (4, 128)), ("g_ffn", (8, 128)), ("conv_b", (44, 128)))
_SMALL_PAIRS = (("b_re", "b_im"), ("c_re", "c_im"))
_PAIR_SHAPE = (NG * GS, NP)


def _ceil8(r):
    return -(-r // 8) * 8


def _small_names():
    return [n for n, _ in _SMALL_F32] + [n for pair in _SMALL_PAIRS for n in pair]


def _pack_small_grads(g, loss_dev):
    parts = [jnp.pad(g[n].reshape(r, c), ((0, _ceil8(r) - r), (0, LANES - c))) for n, (r, c) in _SMALL_F32]
    parts.append(jnp.pad(loss_dev, ((0, 7), (0, LANES - 1))))
    pairs = [jnp.concatenate([g[a].reshape(_PAIR_SHAPE), g[b].reshape(_PAIR_SHAPE)], axis=1) for a, b in _SMALL_PAIRS]
    return jnp.concatenate(parts, axis=0), jnp.concatenate(pairs, axis=0).astype(BF16)


def _adamw_small(gall, gall_b, wmv):
    names = _small_names()
    shapes = [s for _, s in _SMALL_F32] + [_PAIR_SHAPE] * (2 * len(_SMALL_PAIRS))
    npar = len(names)

    def body(ga_ref, gb_ref, *rest):
        ins, loss_ref, outs = rest[:3 * npar], rest[3 * npar], rest[3 * npar + 1:]
        ga, gb = ga_ref[0], gb_ref[0].astype(F32)
        for dev in range(1, NDEV):
            ga = ga + ga_ref[dev]
            gb = gb + gb_ref[dev].astype(F32)
        grads, off = [], 0
        for _, (r, c) in _SMALL_F32:
            grads.append(ga[off:off + r, 0:c])
            off += _ceil8(r)
        loss_ref[...] = ga[off:off + 8]
        rows, cols = _PAIR_SHAPE
        for j in range(len(_SMALL_PAIRS)):
            for h in range(2):
                grads.append(gb[rows * j:rows * (j + 1), cols * h:cols * (h + 1)])
        for i, g in enumerate(grads):
            d, nm, nv = _adamw_math(ins[3 * i][...], g, ins[3 * i + 1][...], ins[3 * i + 2][...])
            outs[4 * i][...] = g
            outs[4 * i + 1][...] = d
            outs[4 * i + 2][...] = nm
            outs[4 * i + 3][...] = nv

    flat = [a for n in names for a in wmv[n]]
    out_shape = [_sds((8, LANES))] + [_sds(s) for s in shapes for _ in range(4)]
    res = pl.pallas_call(body, name="adamw_small", out_shape=out_shape, compiler_params=_cp())(gall, gall_b, *flat)
    return res[0], {n: res[1 + 4 * i:5 + 4 * i] for i, n in enumerate(names)}


def _block_diag(m):
    eye = jnp.eye(8, dtype=m.dtype)
    r, c = m.shape[2], m.shape[3]
    return (m[:, :, :, None, :] * eye[None, :, None, :, None]).reshape(4, 8 * r, 8 * c)


def _diag_blocks(dense, r, c):
    eye = jnp.eye(8, dtype=dense.dtype)
    return jnp.sum(dense.reshape(4, 8, r, 8, c) * eye[None, :, None, :, None], axis=3)


def kernel(x, g_mix, w_in, b_f, g_q, g_k, lambda_re, lambda_im, log_step, b_re, b_im, c_re, c_im, d_skip, w_glu, b_glu, g_attn_out, g_ssm_out, w_out, g_ffn, w_up, conv_w, conv_b, w_down, loss_target, m_g_mix, m_w_in, m_b_f, m_g_q, m_g_k, m_lambda_re, m_lambda_im, m_log_step, m_b_re, m_b_im, m_c_re, m_c_im, m_d_skip, m_w_glu, m_b_glu, m_g_attn_out, m_g_ssm_out, m_w_out, m_g_ffn, m_w_up, m_conv_w, m_conv_b, m_w_down, v_g_mix, v_w_in, v_b_f, v_g_q, v_g_k, v_lambda_re, v_lambda_im, v_log_step, v_b_re, v_b_im, v_c_re, v_c_im, v_d_skip, v_w_glu, v_b_glu, v_g_attn_out, v_g_ssm_out, v_w_out, v_g_ffn, v_w_up, v_conv_w, v_conv_b, v_w_down):
    weights = dict(g_mix=g_mix, w_in=w_in, b_f=b_f, g_q=g_q, g_k=g_k, lambda_re=lambda_re, lambda_im=lambda_im,
                   log_step=log_step, b_re=b_re, b_im=b_im, c_re=c_re, c_im=c_im, d_skip=d_skip, w_glu=w_glu,
                   b_glu=b_glu, g_attn_out=g_attn_out, g_ssm_out=g_ssm_out, w_out=w_out, g_ffn=g_ffn, w_up=w_up,
                   conv_w=conv_w, conv_b=conv_b, w_down=w_down)
    mom_m = dict(g_mix=m_g_mix, w_in=m_w_in, b_f=m_b_f, g_q=m_g_q, g_k=m_g_k, lambda_re=m_lambda_re,
                 lambda_im=m_lambda_im, log_step=m_log_step, b_re=m_b_re, b_im=m_b_im, c_re=m_c_re, c_im=m_c_im,
                 d_skip=m_d_skip, w_glu=m_w_glu, b_glu=m_b_glu, g_attn_out=m_g_attn_out, g_ssm_out=m_g_ssm_out,
                 w_out=m_w_out, g_ffn=m_g_ffn, w_up=m_w_up, conv_w=m_conv_w, conv_b=m_conv_b, w_down=m_w_down)
    mom_v = dict(g_mix=v_g_mix, w_in=v_w_in, b_f=v_b_f, g_q=v_g_q, g_k=v_g_k, lambda_re=v_lambda_re,
                 lambda_im=v_lambda_im, log_step=v_log_step, b_re=v_b_re, b_im=v_b_im, c_re=v_c_re, c_im=v_c_im,
                 d_skip=v_d_skip, w_glu=v_w_glu, b_glu=v_b_glu, g_attn_out=v_g_attn_out, g_ssm_out=v_g_ssm_out,
                 w_out=v_w_out, g_ffn=v_g_ffn, w_up=v_w_up, conv_w=v_conv_w, conv_b=v_conv_b, w_down=v_w_down)
    names = list(weights)
    big = ("w_in", "w_glu", "w_out", "w_up", "w_down", "conv_w")

    xs = x[0]
    target = loss_target[0]
    row = lambda a: a.reshape(1, -1)

    (win_g,) = _all_gather([w_in.T.astype(BF16)], "gather_w_in")
    w_in_t = win_g.reshape(8 * 257, D)
    wcat = jnp.concatenate([w_in_t[0:1536], w_in_t[1544:2056], jnp.pad(w_in_t[1536:1544], ((0, LANES - NH), (0, 0)))],
                           axis=0)
    cb4 = conv_b.reshape(2, 4, 1, CS)

    h1, z, f_t, (wg_g, wo_g) = _in_proj(xs, row(g_mix), wcat, [w_glu.astype(BF16), w_out.astype(BF16)])
    bf_col = b_f.reshape(NH, 1)
    ck = _gates_fwd(f_t, bf_col).reshape(_HP, 2, T)
    att, lse, (wu_g, cw_g), (wg_g, wo_g) = _attn_fwd(z, ck, row(g_q), row(g_k), [w_up.T.astype(BF16), conv_w],
                                                       [wg_g, wo_g])

    ls_col = log_step.reshape(NG, 1)
    br_t, bi_t = b_re.transpose(0, 2, 1), b_im.transpose(0, 2, 1)
    ab_re, ab_im, bb_re, bb_im = _disc_fwd(lambda_re, lambda_im, ls_col, br_t, bi_t)
    bb = jnp.concatenate([_block_diag(bb_re.reshape(4, 8, GS, NP)), _block_diag(bb_im.reshape(4, 8, GS, NP))],
                         axis=2).astype(BF16)
    ab = jnp.stack([ab_re.reshape(4, 512), ab_im.reshape(4, 512)], axis=1)
    cdiag = lambda cm: _block_diag(cm.reshape(4, 8, GS, NP).transpose(0, 1, 3, 2))
    cc = jnp.concatenate([cdiag(c_re), -cdiag(c_im)], axis=1).astype(BF16)
    dsk = d_skip.reshape(4, 1, LANES)
    u_il = _interleave(z[:, 1536:2048])
    y_il, xs_il, (wd_g,) = _ssm_fwd(u_il, bb, ab, cc, dsk, [w_down.astype(BF16)])
    y = _deinterleave(y_il)
    wg_f = wg_g.reshape(SW, SW)
    wo_f = wo_g.reshape(D, D)
    x1, (wu_g, cw_g, wd_g) = _mix_fwd(att, y, xs, wg_f, row(b_glu), row(g_attn_out), row(g_ssm_out), wo_f,
                                      [wu_g, cw_g, wd_g])
    wu4 = wu_g.reshape(2, 4, CS, D)
    wd4 = wd_g.reshape(4, CS, D)
    cw4 = cw_g.reshape(2, 4, 3, CS)
    h2, up4 = _ffn_up(x1, row(g_ffn), wu4)
    act = _ffn_act(up4, cw4, cb4)
    loss_dev, dx2, dx2b = _ffn_down_loss(act, wd4, x1, target)

    dact = _ffn_dact(dx2b, wd4)
    d_wd = _ffn_dwd(act, dx2b)
    dup4, dcw4, dcb4 = _ffn_act_bwd(up4, dact, cw4, cb4)
    d_wu = _ffn_dwu(h2, dup4)
    dx1, dg_ffn = _ffn_up_bwd(dup4, wu4, x1, row(g_ffn), dx2)
    datt, dy, d_wg, d_bg, d_ga, d_gs, d_wo = _mix_bwd(att, y, dx1, wg_f, row(b_glu), row(g_attn_out),
                                                       row(g_ssm_out), wo_f)
    early = ("w_up", "w_down", "conv_w", "w_out", "w_glu")
    by_dest = {"w_up": d_wu.reshape(4, 2, CS, D), "w_down": d_wd.reshape(4, 2, DFF // NDEV, D),
               "conv_w": dcw4.reshape(4, 2, 3, CS), "w_out": d_wo.reshape(4, 2, D // NDEV, D),
               "w_glu": d_wg.reshape(4, 2, SW // NDEV, SW)}
    du_il, dbb, dab, dcc, ddsk, fs = _ssm_bwd(u_il, _interleave(dy), xs_il, bb, ab, cc, dsk, [by_dest[n] for n in early])
    from_sibling = dict(zip(early, fs))
    chip_sums = {n: _pair_sum(by_dest[n], from_sibling[n], "rs_pair_sum_" + n) for n in early}
    du = _deinterleave(du_il).astype(BF16)
    dbb_re = _diag_blocks(dbb[:, :, 0:512], GS, NP).reshape(NG, GS, NP)
    dbb_im = _diag_blocks(dbb[:, :, 512:1024], GS, NP).reshape(NG, GS, NP)
    dlr, dli, dls, dbr_t, dbi_t = _disc_bwd(lambda_re, lambda_im, ls_col, br_t, bi_t,
                                            dab[:, 0].reshape(NG, NP), dab[:, 1].reshape(NG, NP), dbb_re, dbb_im)
    d_cre = _diag_blocks(dcc[:, 0:512], NP, GS).transpose(0, 1, 3, 2).reshape(NG, GS, NP)
    d_cim = -_diag_blocks(dcc[:, 512:1024], NP, GS).transpose(0, 1, 3, 2).reshape(NG, GS, NP)

    dq, dk, dv, dck, d_gq, d_gk, fc = _attn_bwd(z, ck, row(g_q), row(g_k), lse, datt, [chip_sums[n] for n in early])
    from_chips = dict(zip(early, fc))
    df_t, d_bf = _gates_bwd(dck.reshape(NH, T), f_t, bf_col)
    df = jnp.concatenate([df_t.T, jnp.zeros((T, LANES - NH), F32)], axis=1).astype(BF16)
    d_wcat = _in_proj_dw(h1, dq, dk, dv, du, df)
    d_win = jnp.concatenate([d_wcat[0:1536], d_wcat[2048:2048 + NH], d_wcat[1536:2048]], axis=0)
    by_dest["w_in"] = d_win.astype(BF16).reshape(4, 2, 257, D)
    (from_sibling["w_in"],) = _swap_sibling([by_dest["w_in"]], "rs_pair_swap_w_in")
    chip_sums["w_in"] = _pair_sum(by_dest["w_in"], from_sibling["w_in"], "rs_pair_sum_w_in")
    grad_x, dg_mix, (from_chips["w_in"],) = _in_proj_bwd(dq, dk, dv, du, df, wcat, xs, row(g_mix), dx1, [chip_sums["w_in"]])

    grads, deltas, new_m, new_v = {}, {}, {}, {}
    for n in early + ("w_in",):
        flip = (lambda a: a.T) if n in ("w_up", "w_in") else (lambda a: a)
        outs = _adamw_rs(by_dest[n], from_sibling[n], from_chips[n], flip(weights[n]), flip(mom_m[n]), flip(mom_v[n]),
                         "adamw_" + n)
        grads[n], deltas[n], new_m[n], new_v[n] = (flip(o) for o in outs)

    small = dict(g_mix=dg_mix, b_f=d_bf, g_q=d_gq, g_k=d_gk, lambda_re=dlr, lambda_im=dli, log_step=dls,
                 b_re=dbr_t, b_im=dbi_t, c_re=d_cre, c_im=d_cim, d_skip=ddsk, b_glu=d_bg, g_attn_out=d_ga,
                 g_ssm_out=d_gs, g_ffn=dg_ffn, conv_b=dcb4)
    small_all, small_all_b = _all_gather(list(_pack_small_grads(small, loss_dev)), "gather_small_grads")
    views = dict(_SMALL_F32)
    swapped = ("b_re", "b_im")

    def view(n, a):
        return a.transpose(0, 2, 1).reshape(_PAIR_SHAPE) if n in swapped else a.reshape(views.get(n, _PAIR_SHAPE))

    loss_p, small_out = _adamw_small(small_all, small_all_b,
                                     {n: (view(n, weights[n]), view(n, mom_m[n]), view(n, mom_v[n])) for n in small})
    loss = loss_p[0, 0]
    for n in small:
        back = (lambda a: a.reshape(NG, GS, NP).transpose(0, 2, 1)) if n in swapped else (lambda a: a.reshape(weights[n].shape))
        grads[n], deltas[n], new_m[n], new_v[n] = (back(o) for o in small_out[n])

    return (loss, grad_x[None], *[grads[n] for n in names], *[deltas[n] for n in names],
            *[new_m[n] for n in names], *[new_v[n] for n in names])
```

```python
import jax
import jax.numpy as jnp
from jax import lax
from jax.experimental import pallas as pl
from jax.experimental.pallas import tpu as pltpu

F32, BF16 = jnp.float32, jnp.bfloat16
T, D = 2048, 1024
NH, DH = 8, 64
AW, SW = 512, 512
NG, GS, NP = 32, 16, 64
DFF = 2816
NDEV = 8
CS = 2 * DFF // NDEV
EPS = 1e-6
NEG = -1e30
ZC = 4 * 512 + 128
SEG = 256
NSEG = T // SEG
_SCAN_UNROLL = 4
BQ = 512
VMEM_LIMIT = 56 * 1024 * 1024
LANES = 128
MESH = pl.DeviceIdType.MESH
HI = lax.Precision.HIGHEST

ADAM_LR, ADAM_B1, ADAM_B2, ADAM_EPS, ADAM_WD, ADAM_STEP = 0.001, 0.9, 0.999, 1e-08, 0.01, 10


def _cp(dims=None):
    if dims is None:
        return pltpu.CompilerParams(vmem_limit_bytes=VMEM_LIMIT)
    return pltpu.CompilerParams(dimension_semantics=dims, vmem_limit_bytes=VMEM_LIMIT)


def _sds(shape, dtype=F32):
    return jax.ShapeDtypeStruct(shape, dtype)


def _nt(a, b):
    return lax.dot_general(a, b, (((1,), (1,)), ((), ())), preferred_element_type=F32)


def _tn(a, b):
    return lax.dot_general(a, b, (((0,), (0,)), ((), ())), preferred_element_type=F32)


def _nn(a, b):
    return jnp.dot(a, b, preferred_element_type=F32)


def _rms_r(x):
    return lax.rsqrt(jnp.mean(x * x, axis=-1, keepdims=True) + EPS)


def _rms_bwd(n, r, g, dh):
    dn = dh * g
    dx = r * (dn - n * jnp.mean(dn * n, axis=-1, keepdims=True))
    return dx, jnp.sum(dh * n, axis=0, keepdims=True)


def _row_tile(rows, cols, limit):
    tr = rows
    while tr * cols * 4 > limit and tr % 32 == 0:
        tr //= 2
    return tr


def _place():
    return lax.axis_index("x"), lax.axis_index("y"), lax.axis_index("c")


def _all_gather(shards, name):
    n = len(shards)

    def body(*refs):
        x_refs, out_refs = refs[:n], refs[n:2 * n]
        send_sems, recv_sems, local_sems = refs[2 * n:]
        x, y, c = _place()
        me, sibling = (x, y, c), (x, y, 1 - c)
        chips = [(1 - x, y), (x, 1 - y), (1 - x, 1 - y)]

        def copy(a, k, block, to, src=None):
            px, py, pc = block
            slot = out_refs[a].at[4 * px + 2 * py + pc]
            return pltpu.make_async_remote_copy(
                src_ref=slot if src is None else src, dst_ref=slot,
                send_sem=send_sems.at[7 * a + k], recv_sem=recv_sems.at[7 * a + k], device_id=to, device_id_type=MESH)

        mine, first, passed = [], [], []
        for a in range(n):
            cp = pltpu.make_async_copy(x_refs[a], out_refs[a].at[4 * x + 2 * y + c], local_sems.at[a])
            cp.start()
            mine.append(cp)
            cps = [copy(a, 0, me, sibling, src=x_refs[a])]
            cps += [copy(a, 1 + j, me, (*chip, c), src=x_refs[a]) for j, chip in enumerate(chips)]
            for cp in cps:
                cp.start()
            first += cps
        for a in range(n):
            for j, chip in enumerate(chips):
                copy(a, 1 + j, (*chip, c), me).wait_recv()
                fwd = copy(a, 4 + j, (*chip, c), sibling)
                fwd.start()
                passed.append(fwd)
        for a in range(n):
            copy(a, 0, sibling, me).wait_recv()
            for j, chip in enumerate(chips):
                copy(a, 4 + j, (*chip, 1 - c), me).wait_recv()
        for cp in first + passed:
            cp.wait_send()
        for cp in mine:
            cp.wait()

    anyspec = pl.BlockSpec(memory_space=pl.ANY)
    return pl.pallas_call(
        body, name=name,
        out_shape=[_sds((NDEV, *s.shape), s.dtype) for s in shards],
        in_specs=[anyspec] * n, out_specs=[anyspec] * n,
        scratch_shapes=[pltpu.SemaphoreType.DMA((7 * n,)), pltpu.SemaphoreType.DMA((7 * n,)),
                        pltpu.SemaphoreType.DMA((n,))],
    )(*shards)


def _gather_first_copies(x_refs, out_refs, send_sems, recv_sems, local_sems):
    x, y, c = _place()
    slot = 4 * x + 2 * y + c
    peers = [(x, y, 1 - c), (1 - x, y, c), (x, 1 - y, c), (1 - x, 1 - y, c)]
    local, remote = [], []
    for a, (x_ref, out_ref) in enumerate(zip(x_refs, out_refs)):
        local.append(pltpu.make_async_copy(x_ref, out_ref.at[slot], local_sems.at[a]))
        for k, peer in enumerate(peers):
            remote.append(pltpu.make_async_remote_copy(
                src_ref=x_ref, dst_ref=out_ref.at[slot], send_sem=send_sems.at[4 * a + k],
                recv_sem=recv_sems.at[4 * a + k], device_id=peer, device_id_type=MESH))
    return local, remote


def _gather_first_scratch(n):
    return [pltpu.SemaphoreType.DMA((4 * n,)), pltpu.SemaphoreType.DMA((4 * n,)), pltpu.SemaphoreType.DMA((n,))]


def _chip_swap_copies(p_refs, out_refs, send_sems, recv_sems):
    x, y, c = _place()
    chips = [(1 - x, y), (x, 1 - y), (1 - x, 1 - y)]
    cps = []
    for a, (p_ref, out_ref) in enumerate(zip(p_refs, out_refs)):
        for j, (px, py) in enumerate(chips):
            cps.append(pltpu.make_async_remote_copy(
                src_ref=p_ref.at[2 * px + py], dst_ref=out_ref.at[j],
                send_sem=send_sems.at[3 * a + j], recv_sem=recv_sems.at[3 * a + j],
                device_id=(px, py, c), device_id_type=MESH))
    return cps


def _sibling_swap_copies(a_refs, out_refs, send_sems, recv_sems):
    x, y, c = _place()
    cps = []
    for a, (a_ref, out_ref) in enumerate(zip(a_refs, out_refs)):
        for k in range(4):
            cps.append(pltpu.make_async_remote_copy(
                src_ref=a_ref.at[k, 1 - c], dst_ref=out_ref.at[k],
                send_sem=send_sems.at[4 * a + k], recv_sem=recv_sems.at[4 * a + k],
                device_id=(x, y, 1 - c), device_id_type=MESH))
    return cps


def _pass_on_copies(buf_refs, send_sems, recv_sems):
    x, y, c = _place()
    chips = [(1 - x, y), (x, 1 - y), (1 - x, 1 - y)]
    cps = []
    for a, buf in enumerate(buf_refs):
        for j, (px, py) in enumerate(chips):
            block = buf.at[4 * px + 2 * py + c]
            cps.append(pltpu.make_async_remote_copy(
                src_ref=block, dst_ref=block, send_sem=send_sems.at[3 * a + j], recv_sem=recv_sems.at[3 * a + j],
                device_id=(x, y, 1 - c), device_id_type=MESH))
    return cps


def _swap_sibling(arrs, name):
    n = len(arrs)

    def body(*refs):
        cps = _sibling_swap_copies(refs[:n], refs[n:2 * n], *refs[2 * n:])
        for cp in cps:
            cp.start()
        for cp in cps:
            cp.wait()

    anyspec = pl.BlockSpec(memory_space=pl.ANY)
    return pl.pallas_call(
        body, name=name,
        out_shape=[_sds((4, *a.shape[2:]), a.dtype) for a in arrs],
        in_specs=[anyspec] * n, out_specs=[anyspec] * n,
        scratch_shapes=[pltpu.SemaphoreType.DMA((4 * n,)), pltpu.SemaphoreType.DMA((4 * n,))],
    )(*arrs)


def _in_proj(x, g, wcat, shards):
    tm = 256
    n = len(shards)

    def body(x_ref, g_ref, w_ref, *rest):
        x_refs, (h_ref, z_ref, ft_ref), out_refs, sems = rest[:n], rest[n:n + 3], rest[n + 3:2 * n + 3], rest[2 * n + 3:]

        @pl.when(pl.program_id(0) == 0)
        def _():
            local, remote = _gather_first_copies(x_refs, out_refs, *sems)
            for cp in local + remote:
                cp.start()

        xv = x_ref[...]
        h = (xv * _rms_r(xv) * g_ref[...]).astype(BF16)
        h_ref[...] = h
        z = _nt(h, w_ref[...])
        z_ref[...] = z
        ft_ref[...] = z[:, 2048:ZC].T[0:NH, :]

        @pl.when(pl.program_id(0) == T // tm - 1)
        def _():
            local, remote = _gather_first_copies(x_refs, out_refs, *sems)
            for cp in remote + local:
                cp.wait()

    row = lambda i: (i, 0)
    fixed = lambda i: (0, 0)
    anyspec = pl.BlockSpec(memory_space=pl.ANY)
    res = pl.pallas_call(
        body, name="in_proj", grid=(T // tm,),
        in_specs=[pl.BlockSpec((tm, D), row), pl.BlockSpec((1, D), fixed), pl.BlockSpec((ZC, D), fixed)] + [anyspec] * n,
        out_specs=[pl.BlockSpec((tm, D), row), pl.BlockSpec((tm, ZC), row), pl.BlockSpec((NH, tm), lambda i: (0, i))]
        + [anyspec] * n,
        out_shape=[_sds((T, D), BF16), _sds((T, ZC)), _sds((NH, T))] + [_sds((NDEV, *s.shape), s.dtype) for s in shards],
        scratch_shapes=_gather_first_scratch(n),
        compiler_params=_cp(("arbitrary",)),
    )(x, g, wcat, *shards)
    return res[0], res[1], res[2], res[3:]


def _in_proj_dw(h, dq, dk, dv, du, df):
    tm = 512

    def body(h_ref, dq_ref, dk_ref, dv_ref, du_ref, df_ref, o_ref):
        hv = h_ref[...]
        for j, d_ref in enumerate((dq_ref, dk_ref, dv_ref, du_ref)):
            o_ref[512 * j:512 * (j + 1), :] = _tn(d_ref[...], hv)
        o_ref[2048:ZC, :] = _tn(df_ref[...], hv)

    full = lambda w: pl.BlockSpec((T, w), lambda i: (0, 0))
    return pl.pallas_call(
        body, name="in_proj_dw", grid=(D // tm,),
        in_specs=[pl.BlockSpec((T, tm), lambda i: (0, i)), full(512), full(512), full(512), full(512), full(LANES)],
        out_specs=pl.BlockSpec((ZC, tm), lambda i: (0, i)), out_shape=_sds((ZC, D)),
        compiler_params=_cp(("parallel",)),
    )(h, dq, dk, dv, du, df)


def _in_proj_bwd(dq, dk, dv, du, df, wcat, x, g, dres, parts):
    tm = 256
    n = len(parts)

    def body(dq_ref, dk_ref, dv_ref, du_ref, df_ref, w_ref, x_ref, g_ref, dres_ref, *rest):
        p_refs, (dx_ref, dg_ref), fc_refs, sems = rest[:n], rest[n:n + 2], rest[n + 2:2 * n + 2], rest[2 * n + 2:]

        @pl.when(pl.program_id(0) == 0)
        def _():
            for cp in _chip_swap_copies(p_refs, fc_refs, *sems):
                cp.start()

        dh = _nn(df_ref[...], w_ref[2048:ZC, :])
        for j, d_ref in enumerate((dq_ref, dk_ref, dv_ref, du_ref)):
            dh = dh + _nn(d_ref[...], w_ref[512 * j:512 * (j + 1), :])
        xv = x_ref[...]
        r = _rms_r(xv)
        dxr, dgp = _rms_bwd(xv * r, r, g_ref[...], dh)
        dx_ref[...] = dres_ref[...] + dxr

        @pl.when(pl.program_id(0) == 0)
        def _():
            dg_ref[...] = jnp.zeros_like(dg_ref)
        dg_ref[...] += dgp

        @pl.when(pl.program_id(0) == T // tm - 1)
        def _():
            for cp in _chip_swap_copies(p_refs, fc_refs, *sems):
                cp.wait()

    row = lambda i: (i, 0)
    fixed = lambda i: (0, 0)
    part = pl.BlockSpec((tm, 512), row)
    anyspec = pl.BlockSpec(memory_space=pl.ANY)
    res = pl.pallas_call(
        body, name="in_proj_bwd", grid=(T // tm,),
        in_specs=[part, part, part, part, pl.BlockSpec((tm, LANES), row), pl.BlockSpec((ZC, D), fixed),
                  pl.BlockSpec((tm, D), row), pl.BlockSpec((1, D), fixed), pl.BlockSpec((tm, D), row)] + [anyspec] * n,
        out_specs=[pl.BlockSpec((tm, D), row), pl.BlockSpec((1, D), fixed)] + [anyspec] * n,
        out_shape=[_sds((T, D)), _sds((1, D))] + [_sds((3, *p.shape[1:]), p.dtype) for p in parts],
        scratch_shapes=[pltpu.SemaphoreType.DMA((3 * n,)), pltpu.SemaphoreType.DMA((3 * n,))],
        compiler_params=_cp(("arbitrary",)),
    )(dq, dk, dv, du, df, wcat, x, g, dres, *parts)
    return res[0], res[1], res[2:]


def _stack_lanes(v):
    return jnp.concatenate([v[:, LANES * b:LANES * (b + 1)] for b in range(T // LANES)], axis=0)


def _unstack_lanes(s):
    return jnp.concatenate([s[8 * b:8 * b + 8, :] for b in range(T // LANES)], axis=1)


def _cumsum_lanes(v, reverse):
    st = _stack_lanes(v)
    ri = lax.broadcasted_iota(jnp.int32, (LANES, LANES), 0)
    ci = lax.broadcasted_iota(jnp.int32, (LANES, LANES), 1)
    tri = (ri >= ci) if reverse else (ri <= ci)
    intra = jnp.dot(st, tri.astype(F32), precision=HI, preferred_element_type=F32)
    tot = intra[:, 0:1] if reverse else intra[:, LANES - 1:LANES]
    same_head = (ci % 8) == (ri % 8)
    other = (ci // 8 > ri // 8) if reverse else (ci // 8 < ri // 8)
    off = jnp.dot((same_head & other).astype(F32), jnp.broadcast_to(tot, (LANES, LANES)), precision=HI,
                  preferred_element_type=F32)
    return _unstack_lanes(intra + off)


def _gates_fwd(f_t, b_f):
    def body(f_ref, b_ref, c_ref):
        z = f_ref[...] + b_ref[...]
        lf = jnp.minimum(z, 0.0) - jnp.log(1.0 + jnp.exp(-jnp.abs(z)))
        c_ref[...] = _cumsum_lanes(lf, reverse=False)

    return pl.pallas_call(body, name="gates_fwd", out_shape=_sds((NH, T)), compiler_params=_cp())(f_t, b_f)


def _gates_bwd(dck, f_t, b_f):
    def body(d_ref, f_ref, b_ref, df_ref, db_ref):
        z = f_ref[...] + b_ref[...]
        dlf = _cumsum_lanes(d_ref[...], reverse=True)
        df = dlf * jax.nn.sigmoid(-z)
        df_ref[...] = df
        db_ref[...] = jnp.sum(df, axis=1, keepdims=True)

    return pl.pallas_call(body, name="gates_bwd", out_shape=[_sds((NH, T)), _sds((NH, 1))],
                          compiler_params=_cp())(dck, f_t, b_f)


def _lower_triangle():
    rows = lax.broadcasted_iota(jnp.int32, (BQ, BQ), 0)
    cols = lax.broadcasted_iota(jnp.int32, (BQ, BQ), 1)
    return cols <= rows


def _logit_parts(qb, kb, ck_row, lo, hi, tri):
    parts = []
    if lo > 0:
        parts.append((_nt(qb[lo:hi], kb[0:lo]) - ck_row[:, 0:lo], 0, lo))
    parts.append((jnp.where(tri, _nt(qb[lo:hi], kb[lo:hi]) - ck_row[:, lo:hi], NEG), lo, hi))
    return parts


_HP = NH // 2


def _qkv_specs():
    return [pl.BlockSpec((T, LANES), lambda p: (0, p)), pl.BlockSpec((T, LANES), lambda p: (0, _HP + p)),
            pl.BlockSpec((T, LANES), lambda p: (0, 2 * _HP + p))]


def _attn_fwd(z, ck, g_q, g_k, shards, bufs):
    scale = DH ** -0.5
    n, m = len(shards), len(bufs)

    def body(q_ref, k_ref, v_ref, ck_ref, gq_ref, gk_ref, *rest):
        x_refs, (o_ref, lse_ref) = rest[:n], rest[n + m:n + m + 2]
        out_refs, buf_refs, sems = rest[n + m + 2:2 * n + m + 2], rest[2 * n + m + 2:2 * n + 2 * m + 2], rest[2 * n + 2 * m + 2:]

        @pl.when(pl.program_id(0) == 0)
        def _():
            local, remote = _gather_first_copies(x_refs, out_refs, *sems[:3])
            for cp in local + remote + _pass_on_copies(buf_refs, *sems[3:]):
                cp.start()

        qb, kb, vb = [], [], []
        for hh in range(2):
            cols = slice(DH * hh, DH * (hh + 1))
            qv, kv = q_ref[:, cols], k_ref[:, cols]
            qb.append((qv * _rms_r(qv) * gq_ref[...] * scale).astype(BF16))
            kb.append((kv * _rms_r(kv) * gk_ref[...]).astype(BF16))
            vb.append(v_ref[:, cols].astype(BF16))
        tri = _lower_triangle()
        for i in range(T // BQ):
            lo, hi = i * BQ, (i + 1) * BQ
            outs = []
            for hh in range(2):
                parts = _logit_parts(qb[hh], kb[hh], ck_ref[0, hh:hh + 1, :], lo, hi, tri)
                top = jnp.max(parts[-1][0], axis=-1, keepdims=True)
                if lo > 0:
                    top = jnp.maximum(top, jnp.max(parts[0][0], axis=-1, keepdims=True))
                l, o = 0.0, 0.0
                for s, k0, k1 in parts:
                    p = jnp.exp(s - top)
                    l = l + jnp.sum(p, axis=-1, keepdims=True)
                    o = o + _nn(p.astype(BF16), vb[hh][k0:k1])
                outs.append(o / l)
                lse_ref[0, lo:hi, hh:hh + 1] = top + jnp.log(l)
            o_ref[lo:hi, :] = jnp.concatenate(outs, axis=1)

        @pl.when(pl.program_id(0) == _HP - 1)
        def _():
            local, remote = _gather_first_copies(x_refs, out_refs, *sems[:3])
            for cp in remote + local + _pass_on_copies(buf_refs, *sems[3:]):
                cp.wait()

    fixed = lambda p: (0, 0)
    anyspec = pl.BlockSpec(memory_space=pl.ANY)
    res = pl.pallas_call(
        body, name="attn_fwd", grid=(_HP,),
        in_specs=_qkv_specs() + [pl.BlockSpec((1, 2, T), lambda p: (p, 0, 0)), pl.BlockSpec((1, DH), fixed),
                                 pl.BlockSpec((1, DH), fixed)] + [anyspec] * (n + m),
        out_specs=[pl.BlockSpec((T, LANES), lambda p: (0, p)), pl.BlockSpec((1, T, 2), lambda p: (p, 0, 0))]
        + [anyspec] * (n + m),
        out_shape=[_sds((T, AW)), _sds((_HP, T, 2))] + [_sds((NDEV, *s.shape), s.dtype) for s in shards]
        + [_sds(b.shape, b.dtype) for b in bufs],
        input_output_aliases={6 + n + i: 2 + n + i for i in range(m)},
        scratch_shapes=_gather_first_scratch(n) + [pltpu.SemaphoreType.DMA((3 * m,)), pltpu.SemaphoreType.DMA((3 * m,))],
        compiler_params=_cp(("arbitrary",)),
    )(z, z, z, ck, g_q, g_k, *shards, *bufs)
    return res[0], res[1], res[2:2 + n], res[2 + n:]


def _attn_bwd(z, ck, g_q, g_k, lse, datt, parts):
    scale = DH ** -0.5
    n = len(parts)

    def body(q_ref, k_ref, v_ref, ck_ref, gq_ref, gk_ref, lse_ref, do_ref, *rest):
        p_refs, rest = rest[:n], rest[n:]
        dq_ref, dk_ref, dv_ref, dck_ref, dgq_ref, dgk_ref = rest[:6]
        fc_refs, (dkn_acc, dv_acc, dc_acc), sems = rest[6:6 + n], rest[6 + n:9 + n], rest[9 + n:]

        @pl.when(pl.program_id(0) == 0)
        def _():
            for cp in _chip_swap_copies(p_refs, fc_refs, *sems):
                cp.start()

        nq, nk, rq, rk, qb, kb, vb, dob = [], [], [], [], [], [], [], []
        for hh in range(2):
            cols = slice(DH * hh, DH * (hh + 1))
            qv, kv = q_ref[:, cols], k_ref[:, cols]
            rq.append(_rms_r(qv))
            rk.append(_rms_r(kv))
            nq.append(qv * rq[hh])
            nk.append(kv * rk[hh])
            qb.append((nq[hh] * gq_ref[...] * scale).astype(BF16))
            kb.append((nk[hh] * gk_ref[...]).astype(BF16))
            vb.append(v_ref[:, cols].astype(BF16))
            dob.append(do_ref[:, cols].astype(BF16))
        dkn_acc[...] = jnp.zeros_like(dkn_acc)
        dv_acc[...] = jnp.zeros_like(dv_acc)
        dc_acc[...] = jnp.zeros_like(dc_acc)
        dgq = jnp.zeros((1, DH), F32)
        tri = _lower_triangle()
        for i in range(T // BQ):
            lo, hi = i * BQ, (i + 1) * BQ
            dqs = []
            for hh in range(2):
                lse = lse_ref[0, lo:hi, hh:hh + 1]
                pd = []
                for s, k0, k1 in _logit_parts(qb[hh], kb[hh], ck_ref[0, hh:hh + 1, :], lo, hi, tri):
                    pd.append((jnp.exp(s - lse), _nt(dob[hh][lo:hi], vb[hh][k0:k1]), k0, k1))
                mean = sum(jnp.sum(p * dp, axis=-1, keepdims=True) for p, dp, _, _ in pd)
                dqn = 0.0
                for p, dp, k0, k1 in pd:
                    ds = p * (dp - mean)
                    dsb = ds.astype(BF16)
                    dv_acc[hh, k0:k1, :] += _tn(p.astype(BF16), dob[hh][lo:hi])
                    dkn_acc[hh, k0:k1, :] += _tn(dsb, qb[hh][lo:hi])
                    dc_acc[hh:hh + 1, k0:k1] -= jnp.sum(ds, axis=0, keepdims=True)
                    dqn = dqn + _nn(dsb, kb[hh][k0:k1])
                dqx, dgp = _rms_bwd(nq[hh][lo:hi], rq[hh][lo:hi], gq_ref[...], dqn * scale)
                dqs.append(dqx)
                dgq = dgq + dgp
            dq_ref[lo:hi, :] = jnp.concatenate(dqs, axis=1).astype(BF16)
        dks, dgk = [], jnp.zeros((1, DH), F32)
        for hh in range(2):
            dkx, dgp = _rms_bwd(nk[hh], rk[hh], gk_ref[...], dkn_acc[hh])
            dks.append(dkx)
            dgk = dgk + dgp
        dk_ref[...] = jnp.concatenate(dks, axis=1).astype(BF16)
        dv_ref[...] = jnp.concatenate([dv_acc[0], dv_acc[1]], axis=1).astype(BF16)
        dck_ref[0] = dc_acc[...]

        @pl.when(pl.program_id(0) == 0)
        def _():
            dgq_ref[...] = jnp.zeros_like(dgq_ref)
            dgk_ref[...] = jnp.zeros_like(dgk_ref)
        dgq_ref[...] += dgq
        dgk_ref[...] += dgk

        @pl.when(pl.program_id(0) == _HP - 1)
        def _():
            for cp in _chip_swap_copies(p_refs, fc_refs, *sems):
                cp.wait()

    fixed = lambda p: (0, 0)
    pair = pl.BlockSpec((T, LANES), lambda p: (0, p))
    gsp = pl.BlockSpec((1, DH), fixed)
    ckspec = pl.BlockSpec((1, 2, T), lambda p: (p, 0, 0))
    anyspec = pl.BlockSpec(memory_space=pl.ANY)
    res = pl.pallas_call(
        body, name="attn_bwd", grid=(_HP,),
        in_specs=_qkv_specs() + [ckspec, gsp, gsp, pl.BlockSpec((1, T, 2), lambda p: (p, 0, 0)), pair] + [anyspec] * n,
        out_specs=[pair, pair, pair, ckspec, gsp, gsp] + [anyspec] * n,
        out_shape=[_sds((T, AW), BF16)] * 3 + [_sds((_HP, 2, T)), _sds((1, DH)), _sds((1, DH))]
        + [_sds((3, *p.shape[1:]), p.dtype) for p in parts],
        scratch_shapes=[pltpu.VMEM((2, T, DH), F32), pltpu.VMEM((2, T, DH), F32), pltpu.VMEM((2, T), F32),
                        pltpu.SemaphoreType.DMA((3 * n,)), pltpu.SemaphoreType.DMA((3 * n,))],
        compiler_params=_cp(("arbitrary",)),
    )(z, z, z, ck, g_q, g_k, lse, datt, *parts)
    return (*res[:6], res[6:])


def _disc(lr, li, ls, br_t, bi_t):
    step = jnp.exp(ls)
    er = jnp.exp(lr * step)
    ab_re = er * jnp.cos(li * step)
    ab_im = er * jnp.sin(li * step)
    num_re = ab_re - 1.0
    num_im = ab_im
    den = lr * lr + li * li
    f_re = (num_re * lr + num_im * li) / den
    f_im = (num_im * lr - num_re * li) / den
    bb_re = f_re[:, None, :] * br_t - f_im[:, None, :] * bi_t
    bb_im = f_re[:, None, :] * bi_t + f_im[:, None, :] * br_t
    return ab_re, ab_im, bb_re, bb_im


def _disc_fwd(lr, li, ls, br_t, bi_t):
    def body(lr_ref, li_ref, ls_ref, br_ref, bi_ref, ar_ref, ai_ref, bbr_ref, bbi_ref):
        ar, ai, bbr, bbi = _disc(lr_ref[...], li_ref[...], ls_ref[...], br_ref[...], bi_ref[...])
        ar_ref[...] = ar
        ai_ref[...] = ai
        bbr_ref[...] = bbr
        bbi_ref[...] = bbi

    return pl.pallas_call(
        body, name="ssm_disc_fwd",
        out_shape=[_sds((NG, NP)), _sds((NG, NP)), _sds((NG, GS, NP)), _sds((NG, GS, NP))],
        compiler_params=_cp())(lr, li, ls, br_t, bi_t)


def _disc_bwd(lr, li, ls, br_t, bi_t, dar, dai, dbbr, dbbi):
    def body(lr_ref, li_ref, ls_ref, br_ref, bi_ref, dar_ref, dai_ref, dbbr_ref, dbbi_ref,
             dlr_ref, dli_ref, dls_ref, dbr_ref, dbi_ref):
        _, vjp = jax.vjp(_disc, lr_ref[...], li_ref[...], ls_ref[...], br_ref[...], bi_ref[...])
        dlr, dli, dls, dbr, dbi = vjp((dar_ref[...], dai_ref[...], dbbr_ref[...], dbbi_ref[...]))
        dlr_ref[...] = dlr
        dli_ref[...] = dli
        dls_ref[...] = dls
        dbr_ref[...] = dbr
        dbi_ref[...] = dbi

    return pl.pallas_call(
        body, name="ssm_disc_bwd",
        out_shape=[_sds((NG, NP)), _sds((NG, NP)), _sds((NG, 1)), _sds((NG, GS, NP)), _sds((NG, GS, NP))],
        compiler_params=_cp())(lr, li, ls, br_t, bi_t, dar, dai, dbbr, dbbi)


def _cmul(ar, ai, br, bi):
    return ar * br - ai * bi, ar * bi + ai * br


def _scan(buf, a_re, a_im, reverse, other=None):
    ar = [jnp.broadcast_to(a_re[:, LANES * k:LANES * (k + 1)], (NSEG, LANES)) for k in range(4)]
    ai = [jnp.broadcast_to(a_im[:, LANES * k:LANES * (k + 1)], (NSEG, LANES)) for k in range(4)]

    def tile(ref, i, k):
        return ref[pl.ds(pl.multiple_of(i * NSEG, NSEG), NSEG), LANES * k:LANES * (k + 1)]

    def advance(i, xr, xi):
        nr = [ar[k] * xr[k] - ai[k] * xi[k] + tile(buf, i, k) for k in range(4)]
        ni = [ar[k] * xi[k] + ai[k] * xr[k] + tile(buf, i, 4 + k) for k in range(4)]
        return nr, ni

    def index(i):
        return (SEG - 1 - i) if reverse else i

    zeros = tuple(jnp.zeros((NSEG, LANES), F32) for _ in range(4))

    def sweep1(i, carry):
        xr, xi = carry
        for sub in range(_SCAN_UNROLL):
            xr, xi = advance(index(i * _SCAN_UNROLL + sub), xr, xi)
        return tuple(xr), tuple(xi)

    er, ei = lax.fori_loop(0, SEG // _SCAN_UNROLL, sweep1, (zeros, zeros))

    row = lax.broadcasted_iota(jnp.int32, (NSEG, LANES), 0)
    sr, si = [], []
    for k in range(4):
        pr, pi = ar[k], ai[k]
        for _ in range(SEG.bit_length() - 1):
            pr, pi = _cmul(pr, pi, pr, pi)
        ir, ii = er[k], ei[k]
        for d in (1, 2, 4):
            shift = (NSEG - d) if reverse else d
            ok = (row < NSEG - d) if reverse else (row >= d)
            mr, mi = _cmul(pr, pi, pltpu.roll(ir, shift, 0), pltpu.roll(ii, shift, 0))
            ir = ir + jnp.where(ok, mr, 0.0)
            ii = ii + jnp.where(ok, mi, 0.0)
            pr, pi = _cmul(pr, pi, pr, pi)
        shift = (NSEG - 1) if reverse else 1
        ok = (row < NSEG - 1) if reverse else (row >= 1)
        sr.append(jnp.where(ok, pltpu.roll(ir, shift, 0), 0.0))
        si.append(jnp.where(ok, pltpu.roll(ii, shift, 0), 0.0))

    def sweep2(i, carry):
        xr, xi, dar, dai = carry
        for sub in range(_SCAN_UNROLL):
            idx = index(i * _SCAN_UNROLL + sub)
            if other is not None:
                orr = [tile(other, idx, k) for k in range(4)]
                oi = [tile(other, idx, 4 + k) for k in range(4)]
                dar = tuple(dar[k] + xr[k] * orr[k] + xi[k] * oi[k] for k in range(4))
                dai = tuple(dai[k] + xi[k] * orr[k] - xr[k] * oi[k] for k in range(4))
            xr, xi = advance(idx, xr, xi)
            start = pl.multiple_of(idx * NSEG, NSEG)
            for k in range(4):
                buf[pl.ds(start, NSEG), LANES * k:LANES * (k + 1)] = xr[k]
                buf[pl.ds(start, NSEG), LANES * (4 + k):LANES * (5 + k)] = xi[k]
        return tuple(xr), tuple(xi), dar, dai

    _, _, dar, dai = lax.fori_loop(0, SEG // _SCAN_UNROLL, sweep2, (tuple(sr), tuple(si), zeros, zeros))
    if other is None:
        return None
    da_re = jnp.concatenate([jnp.sum(d, axis=0, keepdims=True) for d in dar], axis=1)
    da_im = jnp.concatenate([jnp.sum(d, axis=0, keepdims=True) for d in dai], axis=1)
    return da_re, da_im


_SSM_BLOCKS = 4


def _seg_copies(hbm_ref, col0, buf, block, sems, to_hbm):
    cps = []
    for j in range(NSEG):
        hbm = hbm_ref.at[pl.ds(SEG * j, SEG), pl.ds(col0, LANES)]
        vmem = buf.at[block, :, j, :]
        cps.append(pltpu.make_async_copy(vmem, hbm, sems.at[block, j]) if to_hbm
                   else pltpu.make_async_copy(hbm, vmem, sems.at[block, j]))
    return cps


def _seg_scratch():
    return [pltpu.VMEM((_SSM_BLOCKS, SEG, NSEG, LANES), F32), pltpu.SemaphoreType.DMA((_SSM_BLOCKS, NSEG))]


_U_COL = 3 * AW


def _ssm_fwd(z, bb, ab, cc, dsk, shards):
    n = len(shards)
    last = _SSM_BLOCKS - 1

    def body(z_ref, bb_ref, a_ref, cc_ref, d_ref, *rest):
        x_refs, y_ref, xs_ref, out_refs = rest[:n], rest[n], rest[n + 1], rest[n + 2:2 * n + 2]
        ubuf, lsem, ybuf, ssem = rest[2 * n + 2:2 * n + 6]
        sems = rest[2 * n + 6:]
        step = pl.program_id(0)

        @pl.when(step == 0)
        def _():
            local, remote = _gather_first_copies(x_refs, out_refs, *sems)
            for cp in local + remote:
                cp.start()
            for b in range(_SSM_BLOCKS):
                for cp in _seg_copies(z_ref, _U_COL + LANES * b, ubuf, b, lsem, False):
                    cp.start()

        for cp in _seg_copies(z_ref, _U_COL, ubuf, step, lsem, False):
            cp.wait()
        ub = ubuf[step].reshape(T, LANES)
        xb = xs_ref.at[0]
        xb[...] = _nn(ub.astype(BF16), bb_ref[0])
        _scan(xb, a_ref[0, 0:1, :], a_ref[0, 1:2, :], reverse=False)
        y = _nn(xb[...].astype(BF16), cc_ref[0]) + d_ref[0] * ub
        ybuf[step] = y.reshape(SEG, NSEG, LANES)
        for cp in _seg_copies(y_ref, pl.multiple_of(step * LANES, LANES), ybuf, step, ssem, True):
            cp.start()

        @pl.when(step == last)
        def _():
            for b in range(_SSM_BLOCKS):
                for cp in _seg_copies(y_ref, LANES * b, ybuf, b, ssem, True):
                    cp.wait()
            local, remote = _gather_first_copies(x_refs, out_refs, *sems)
            for cp in remote + local:
                cp.wait()

    blk = lambda j: (j, 0, 0)
    anyspec = pl.BlockSpec(memory_space=pl.ANY)
    res = pl.pallas_call(
        body, name="ssm_fwd", grid=(_SSM_BLOCKS,),
        in_specs=[anyspec, pl.BlockSpec((1, LANES, 1024), blk), pl.BlockSpec((1, 2, 512), blk),
                  pl.BlockSpec((1, 1024, LANES), blk), pl.BlockSpec((1, 1, LANES), blk)] + [anyspec] * n,
        out_specs=[anyspec, pl.BlockSpec((1, T, 1024), blk)] + [anyspec] * n,
        out_shape=[_sds((T, SW)), _sds((_SSM_BLOCKS, T, 1024))] + [_sds((NDEV, *s.shape), s.dtype) for s in shards],
        scratch_shapes=_seg_scratch() + _seg_scratch() + _gather_first_scratch(n),
        compiler_params=_cp(("arbitrary",)),
    )(z, bb, ab, cc, dsk, *shards)
    return res[0], res[1], res[2:]


def _ssm_bwd(z, dy, xs, bb, ab, cc, dsk, arrs):
    n = len(arrs)
    last = _SSM_BLOCKS - 1

    def body(z_ref, dy_ref, x_ref, bb_ref, a_ref, cc_ref, d_ref, *rest):
        a_refs, rest = rest[:n], rest[n:]
        du_ref, dbb_ref, da_ref, dcc_ref, dd_ref = rest[:5]
        fs_refs, gb = rest[5:5 + n], rest[5 + n]
        ubuf, usem, dybuf, dysem, dubuf, dusem = rest[6 + n:12 + n]
        sems = rest[12 + n:]
        step = pl.program_id(0)

        @pl.when(step == 0)
        def _():
            for cp in _sibling_swap_copies(a_refs, fs_refs, *sems):
                cp.start()
            for b in range(_SSM_BLOCKS):
                for cp in (_seg_copies(z_ref, _U_COL + LANES * b, ubuf, b, usem, False)
                           + _seg_copies(dy_ref, LANES * b, dybuf, b, dysem, False)):
                    cp.start()

        for cp in _seg_copies(z_ref, _U_COL, ubuf, step, usem, False) + _seg_copies(dy_ref, 0, dybuf, step, dysem, False):
            cp.wait()
        ub, dyv = ubuf[step].reshape(T, LANES), dybuf[step].reshape(T, LANES)
        ubb, dyb = ub.astype(BF16), dyv.astype(BF16)
        a_re, a_im = a_ref[0, 0:1, :], a_ref[0, 1:2, :]
        dcc_ref[0] = _tn(x_ref[0].astype(BF16), dyb)
        gb[...] = _nt(dyb, cc_ref[0])
        da_re, da_im = _scan(gb, a_re, -a_im, reverse=True, other=x_ref.at[0])
        da_ref[0] = jnp.concatenate([da_re, da_im], axis=0)
        gc = gb[...].astype(BF16)
        du = _nt(gc, bb_ref[0]) + d_ref[0] * dyv
        dubuf[step] = du.reshape(SEG, NSEG, LANES)
        for cp in _seg_copies(du_ref, pl.multiple_of(step * LANES, LANES), dubuf, step, dusem, True):
            cp.start()
        dbb_ref[0] = _tn(ubb, gc)
        dd_ref[0] = jnp.sum(dyv * ub, axis=0, keepdims=True)

        @pl.when(step == last)
        def _():
            for b in range(_SSM_BLOCKS):
                for cp in _seg_copies(du_ref, LANES * b, dubuf, b, dusem, True):
                    cp.wait()
            for cp in _sibling_swap_copies(a_refs, fs_refs, *sems):
                cp.wait()

    blk = lambda j: (j, 0, 0)
    anyspec = pl.BlockSpec(memory_space=pl.ANY)
    res = pl.pallas_call(
        body, name="ssm_bwd", grid=(_SSM_BLOCKS,),
        in_specs=[anyspec, anyspec, pl.BlockSpec((1, T, 1024), blk), pl.BlockSpec((1, LANES, 1024), blk),
                  pl.BlockSpec((1, 2, 512), blk), pl.BlockSpec((1, 1024, LANES), blk), pl.BlockSpec((1, 1, LANES), blk)]
        + [anyspec] * n,
        out_specs=[anyspec, pl.BlockSpec((1, LANES, 1024), blk), pl.BlockSpec((1, 2, 512), blk),
                   pl.BlockSpec((1, 1024, LANES), blk), pl.BlockSpec((1, 1, LANES), blk)] + [anyspec] * n,
        out_shape=[_sds((T, SW)), _sds((4, LANES, 1024)), _sds((4, 2, 512)), _sds((4, 1024, LANES)), _sds((4, 1, LANES))]
        + [_sds((4, *a.shape[2:]), a.dtype) for a in arrs],
        scratch_shapes=[pltpu.VMEM((T, 1024), F32)] + _seg_scratch() + _seg_scratch() + _seg_scratch()
        + [pltpu.SemaphoreType.DMA((4 * n,)), pltpu.SemaphoreType.DMA((4 * n,))],
        compiler_params=_cp(("arbitrary",)),
    )(z, dy, xs, bb, ab, cc, dsk, *arrs)
    return (*res[:5], res[5:])


_TM_MIX = 512


def _mix_parts(att, y, wg, bg):
    y2 = jax.nn.gelu(y)
    y2b = y2.astype(BF16)
    sg = jax.nn.sigmoid(_nn(y2b, wg) + bg)
    ssm = y2 * sg
    ra, rs = _rms_r(att), _rms_r(ssm)
    return y2, y2b, sg, ssm, ra, rs


def _mix_fwd(att, y, x, wg, bg, ga, gs, wo, bufs):
    m = len(bufs)

    def body(att_ref, y_ref, x_ref, wg_ref, bg_ref, ga_ref, gs_ref, wo_ref, *rest):
        x1_ref, buf_refs, sems = rest[m], rest[m + 1:2 * m + 1], rest[2 * m + 1:]

        @pl.when(pl.program_id(0) == 0)
        def _():
            for cp in _pass_on_copies(buf_refs, *sems):
                cp.start()

        attv = att_ref[...]
        _, _, _, ssm, ra, rs = _mix_parts(attv, y_ref[...], wg_ref[...], bg_ref[...])
        ma = (attv * ra * ga_ref[...]).astype(BF16)
        ms = (ssm * rs * gs_ref[...]).astype(BF16)
        x1_ref[...] = x_ref[...] + _nn(ma, wo_ref[0:AW, :]) + _nn(ms, wo_ref[AW:D, :])

        @pl.when(pl.program_id(0) == T // _TM_MIX - 1)
        def _():
            for cp in _pass_on_copies(buf_refs, *sems):
                cp.wait()

    row = lambda i: (i, 0)
    fixed = lambda i: (0, 0)
    half = pl.BlockSpec((_TM_MIX, AW), row)
    vec = pl.BlockSpec((1, AW), fixed)
    anyspec = pl.BlockSpec(memory_space=pl.ANY)
    res = pl.pallas_call(
        body, name="mix_fwd", grid=(T // _TM_MIX,),
        in_specs=[half, half, pl.BlockSpec((_TM_MIX, D), row), pl.BlockSpec((SW, SW), fixed), vec, vec, vec,
                  pl.BlockSpec((D, D), fixed)] + [anyspec] * m,
        out_specs=[pl.BlockSpec((_TM_MIX, D), row)] + [anyspec] * m,
        out_shape=[_sds((T, D))] + [_sds(b.shape, b.dtype) for b in bufs],
        input_output_aliases={8 + i: 1 + i for i in range(m)},
        scratch_shapes=[pltpu.SemaphoreType.DMA((3 * m,)), pltpu.SemaphoreType.DMA((3 * m,))],
        compiler_params=_cp(("arbitrary",)),
    )(att, y, x, wg, bg, ga, gs, wo, *bufs)
    return res[0], res[1:]


def _mix_bwd(att, y, dx1, wg, bg, ga, gs, wo):
    last = T // _TM_MIX - 1

    def body(att_ref, y_ref, d_ref, wg_ref, bg_ref, ga_ref, gs_ref, wo_ref,
             datt_ref, dy_ref, dwg_ref, dbg_ref, dga_ref, dgs_ref, dwo_ref, dwg_acc, dwo_acc):
        attv, yv, db = att_ref[...], y_ref[...], d_ref[...].astype(BF16)
        y2, y2b, sg, ssm, ra, rs = _mix_parts(attv, yv, wg_ref[...], bg_ref[...])
        na, ns = attv * ra, ssm * rs
        ma = (na * ga_ref[...]).astype(BF16)
        ms = (ns * gs_ref[...]).astype(BF16)
        dma = _nt(db, wo_ref[0:AW, :])
        dms = _nt(db, wo_ref[AW:D, :])
        datt, dga = _rms_bwd(na, ra, ga_ref[...], dma)
        dssm, dgs = _rms_bwd(ns, rs, gs_ref[...], dms)
        dgl = dssm * y2 * sg * (1.0 - sg)
        dglb = dgl.astype(BF16)
        dy2 = dssm * sg + _nt(dglb, wg_ref[...])
        _, gelu_vjp = jax.vjp(jax.nn.gelu, yv)
        datt_ref[...] = datt
        dy_ref[...] = gelu_vjp(dy2)[0]

        @pl.when(pl.program_id(0) == 0)
        def _():
            dwg_acc[...] = jnp.zeros_like(dwg_acc)
            dwo_acc[...] = jnp.zeros_like(dwo_acc)
            dbg_ref[...] = jnp.zeros_like(dbg_ref)
            dga_ref[...] = jnp.zeros_like(dga_ref)
            dgs_ref[...] = jnp.zeros_like(dgs_ref)
        dwg_acc[...] += _tn(y2b, dglb)
        dbg_ref[...] += jnp.sum(dgl, axis=0, keepdims=True)
        dga_ref[...] += dga
        dgs_ref[...] += dgs
        dwo_acc[0:AW, :] += _tn(ma, db)
        dwo_acc[AW:D, :] += _tn(ms, db)

        @pl.when(pl.program_id(0) == last)
        def _():
            dwg_ref[...] = dwg_acc[...].astype(BF16)
            dwo_ref[...] = dwo_acc[...].astype(BF16)

    row = lambda i: (i, 0)
    fixed = lambda i: (0, 0)
    half = pl.BlockSpec((_TM_MIX, AW), row)
    vec = pl.BlockSpec((1, AW), fixed)
    sq = pl.BlockSpec((SW, SW), fixed)
    big = pl.BlockSpec((D, D), fixed)
    return pl.pallas_call(
        body, name="mix_bwd", grid=(T // _TM_MIX,),
        in_specs=[half, half, pl.BlockSpec((_TM_MIX, D), row), sq, vec, vec, vec, big],
        out_specs=[half, half, sq, vec, vec, vec, big],
        out_shape=[_sds((T, AW)), _sds((T, SW)), _sds((SW, SW), BF16), _sds((1, SW)), _sds((1, AW)), _sds((1, SW)),
                   _sds((D, D), BF16)],
        scratch_shapes=[pltpu.VMEM((SW, SW), F32), pltpu.VMEM((D, D), F32)],
        compiler_params=_cp(("arbitrary",)),
    )(att, y, dx1, wg, bg, ga, gs, wo)


_NCB = -(-CS // LANES)


def _ffn_up(x1, g, wu4):
    tm = 1024

    def body(x_ref, g_ref, w_ref, h_ref, up_ref):
        @pl.when(pl.program_id(1) == 0)
        def _():
            xv = x_ref[...]
            h_ref[...] = (xv * _rms_r(xv) * g_ref[...]).astype(BF16)
        up_ref[0, 0] = _nt(h_ref[...], w_ref[0, 0])

    return pl.pallas_call(
        body, name="ffn_up", grid=(T // tm, NDEV),
        in_specs=[pl.BlockSpec((tm, D), lambda i, e: (i, 0)), pl.BlockSpec((1, D), lambda i, e: (0, 0)),
                  pl.BlockSpec((1, 1, CS, D), lambda i, e: (e // 4, e % 4, 0, 0))],
        out_specs=[pl.BlockSpec((tm, D), lambda i, e: (i, 0)),
                   pl.BlockSpec((1, 1, tm, CS), lambda i, e: (e // 4, e % 4, i, 0))],
        out_shape=[_sds((T, D), BF16), _sds((2, 4, T, CS))],
        compiler_params=_cp(("parallel", "arbitrary")),
    )(x1, g, wu4)


def _shift_down(h, k):
    row = lax.broadcasted_iota(jnp.int32, h.shape, 0)
    return jnp.where(row >= k, pltpu.roll(h, k, 0), 0.0)


def _shift_up(h, k):
    row = lax.broadcasted_iota(jnp.int32, h.shape, 0)
    return jnp.where(row < T - k, pltpu.roll(h, T - k, 0), 0.0)


def _conv(h, w, b):
    return w[0:1, :] * _shift_down(h, 2) + w[1:2, :] * _shift_down(h, 1) + w[2:3, :] * h + b


def _chan(gv, rows):
    return pl.BlockSpec((1, 1, rows, LANES), lambda d, j: (gv, d, 0, j))


def _ffn_act(up4, cw4, cb4):
    def body(ug_ref, uv_ref, wg_ref, wv_ref, bg_ref, bv_ref, act_ref):
        g = _conv(ug_ref[0, 0], wg_ref[0, 0], bg_ref[0, 0])
        v = _conv(uv_ref[0, 0], wv_ref[0, 0], bv_ref[0, 0])
        act_ref[0] = (g * jax.nn.sigmoid(g) * v).astype(BF16)

    return pl.pallas_call(
        body, name="ffn_act", grid=(4, _NCB),
        in_specs=[_chan(0, T), _chan(1, T), _chan(0, 3), _chan(1, 3), _chan(0, 1), _chan(1, 1)],
        out_specs=pl.BlockSpec((1, T, LANES), lambda d, j: (d, 0, j)), out_shape=_sds((4, T, CS), BF16),
        compiler_params=_cp(("parallel", "parallel")),
    )(up4, up4, cw4, cw4, cb4, cb4)


def _ffn_act_bwd(up4, dact, cw4, cb4):
    def conv_bwd(h, w, d):
        dup = w[2:3, :] * d + w[1:2, :] * _shift_up(d, 1) + w[0:1, :] * _shift_up(d, 2)
        dw = jnp.concatenate([jnp.sum(d * _shift_down(h, 2), axis=0, keepdims=True),
                              jnp.sum(d * _shift_down(h, 1), axis=0, keepdims=True),
                              jnp.sum(d * h, axis=0, keepdims=True)], axis=0)
        return dup, dw, jnp.sum(d, axis=0, keepdims=True)

    def body(ug_ref, uv_ref, da_ref, wg_ref, wv_ref, bg_ref, bv_ref, dup_ref, dw_ref, db_ref):
        ug, uv, da = ug_ref[0, 0], uv_ref[0, 0], da_ref[0]
        g = _conv(ug, wg_ref[0, 0], bg_ref[0, 0])
        v = _conv(uv, wv_ref[0, 0], bv_ref[0, 0])
        sg = jax.nn.sigmoid(g)
        dg = da * v * (sg * (1.0 + g * (1.0 - sg)))
        dv = da * (g * sg)
        dug, dwg, dbg = conv_bwd(ug, wg_ref[0, 0], dg)
        duv, dwv, dbv = conv_bwd(uv, wv_ref[0, 0], dv)
        dup_ref[0, 0] = dug.astype(BF16)
        dup_ref[1, 0] = duv.astype(BF16)
        dw_ref[0, 0] = dwg
        dw_ref[1, 0] = dwv
        db_ref[0, 0] = dbg
        db_ref[1, 0] = dbv

    both = lambda rows: pl.BlockSpec((2, 1, rows, LANES), lambda d, j: (0, d, 0, j))
    return pl.pallas_call(
        body, name="ffn_act_bwd", grid=(4, _NCB),
        in_specs=[_chan(0, T), _chan(1, T), pl.BlockSpec((1, T, LANES), lambda d, j: (d, 0, j)),
                  _chan(0, 3), _chan(1, 3), _chan(0, 1), _chan(1, 1)],
        out_specs=[both(T), both(3), both(1)],
        out_shape=[_sds((2, 4, T, CS), BF16), _sds((2, 4, 3, CS)), _sds((2, 4, 1, CS))],
        compiler_params=_cp(("parallel", "parallel")),
    )(up4, up4, dact, cw4, cw4, cb4, cb4)


def _ffn_down_loss(act, wd4, x1, target):
    tm = 512

    def body(a_ref, w_ref, x1_ref, t_ref, loss_ref, dx_ref, dxb_ref):
        x2 = x1_ref[...]
        for d in range(4):
            x2 = x2 + _nn(a_ref[d], w_ref[d])
        err = x2 - t_ref[...]
        dx = err * (1.0 / D)
        dx_ref[...] = dx
        dxb_ref[...] = dx.astype(BF16)

        @pl.when(pl.program_id(0) == 0)
        def _():
            loss_ref[...] = jnp.zeros_like(loss_ref)
        loss_ref[...] += 0.5 * jnp.sum(jnp.mean(err * err, axis=-1, keepdims=True), axis=0, keepdims=True)

    row = lambda i: (i, 0)
    fixed = lambda i: (0, 0)
    return pl.pallas_call(
        body, name="ffn_down_loss", grid=(T // tm,),
        in_specs=[pl.BlockSpec((4, tm, CS), lambda i: (0, i, 0)), pl.BlockSpec((4, CS, D), lambda i: (0, 0, 0)),
                  pl.BlockSpec((tm, D), row), pl.BlockSpec((tm, D), row)],
        out_specs=[pl.BlockSpec((1, 1), fixed), pl.BlockSpec((tm, D), row), pl.BlockSpec((tm, D), row)],
        out_shape=[_sds((1, 1)), _sds((T, D)), _sds((T, D), BF16)],
        compiler_params=_cp(("arbitrary",)),
    )(act, wd4, x1, target)


def _ffn_dact(dx2b, wd4):
    tm = 1024

    def body(d_ref, w_ref, o_ref):
        o_ref[0] = _nt(d_ref[...], w_ref[0])

    return pl.pallas_call(
        body, name="ffn_dact", grid=(4, T // tm),
        in_specs=[pl.BlockSpec((tm, D), lambda d, i: (i, 0)), pl.BlockSpec((1, CS, D), lambda d, i: (d, 0, 0))],
        out_specs=pl.BlockSpec((1, tm, CS), lambda d, i: (d, i, 0)), out_shape=_sds((4, T, CS)),
        compiler_params=_cp(("parallel", "parallel")),
    )(dx2b, wd4)


def _ffn_dwd(act, dx2b):
    def body(a_ref, d_ref, o_ref):
        o_ref[0] = _tn(a_ref[0], d_ref[...]).astype(BF16)

    return pl.pallas_call(
        body, name="ffn_dwd", grid=(4,),
        in_specs=[pl.BlockSpec((1, T, CS), lambda d: (d, 0, 0)), pl.BlockSpec((T, D), lambda d: (0, 0))],
        out_specs=pl.BlockSpec((1, CS, D), lambda d: (d, 0, 0)), out_shape=_sds((4, CS, D), BF16),
        compiler_params=_cp(("parallel",)),
    )(act, dx2b)


def _ffn_dwu(h2, dup4):
    tn = 1024

    def body(h_ref, d_ref, o_ref):
        o_ref[0, 0] = _tn(d_ref[0, 0], h_ref[...]).astype(BF16)

    return pl.pallas_call(
        body, name="ffn_dwu", grid=(NDEV, D // tn),
        in_specs=[pl.BlockSpec((T, tn), lambda e, i: (0, i)),
                  pl.BlockSpec((1, 1, T, CS), lambda e, i: (e // 4, e % 4, 0, 0))],
        out_specs=pl.BlockSpec((1, 1, CS, tn), lambda e, i: (e // 4, e % 4, 0, i)),
        out_shape=_sds((2, 4, CS, D), BF16),
        compiler_params=_cp(("parallel", "parallel")),
    )(h2, dup4)


def _ffn_up_bwd(dup4, wu4, x1, g, dres):
    tm = 256

    def body(d_ref, w_ref, x_ref, g_ref, dres_ref, dx_ref, dg_ref):
        dh = jnp.zeros((tm, D), F32)
        for gv in range(2):
            for d in range(4):
                dh = dh + _nn(d_ref[gv, d], w_ref[gv, d])
        xv = x_ref[...]
        r = _rms_r(xv)
        dxr, dgp = _rms_bwd(xv * r, r, g_ref[...], dh)
        dx_ref[...] = dres_ref[...] + dxr

        @pl.when(pl.program_id(0) == 0)
        def _():
            dg_ref[...] = jnp.zeros_like(dg_ref)
        dg_ref[...] += dgp

    row = lambda i: (i, 0)
    fixed = lambda i: (0, 0)
    return pl.pallas_call(
        body, name="ffn_up_bwd", grid=(T // tm,),
        in_specs=[pl.BlockSpec((2, 4, tm, CS), lambda i: (0, 0, i, 0)), pl.BlockSpec((2, 4, CS, D), lambda i: (0, 0, 0, 0)),
                  pl.BlockSpec((tm, D), row), pl.BlockSpec((1, D), fixed), pl.BlockSpec((tm, D), row)],
        out_specs=[pl.BlockSpec((tm, D), row), pl.BlockSpec((1, D), fixed)],
        out_shape=[_sds((T, D)), _sds((1, D))],
        compiler_params=_cp(("arbitrary",)),
    )(dup4, wu4, x1, g, dres)


def _adamw_math(w, g, m, v):
    m = ADAM_B1 * m + (1.0 - ADAM_B1) * g
    v = ADAM_B2 * v + (1.0 - ADAM_B2) * jnp.square(g)
    m_hat = m / (1.0 - ADAM_B1 ** ADAM_STEP)
    v_hat = v / (1.0 - ADAM_B2 ** ADAM_STEP)
    delta = -ADAM_LR * (m_hat / (jnp.sqrt(v_hat) + ADAM_EPS) + ADAM_WD * w)
    return delta, m, v


def _where_am_i():
    x, y, c = _place()
    return jnp.stack([c, 2 * x + y]).astype(jnp.int32)


def _pair_sum(a4, recv, name):
    _, _, rows, cols = a4.shape
    tr = _row_tile(rows, cols, 3 << 20)

    def body(sel_ref, own_ref, recv_ref, out_ref):
        out_ref[...] = (own_ref[0].astype(F32) + recv_ref[...].astype(F32)).astype(out_ref.dtype)

    grid_spec = pltpu.PrefetchScalarGridSpec(
        num_scalar_prefetch=1, grid=(4, rows // tr),
        in_specs=[pl.BlockSpec((1, 1, tr, cols), lambda k, i, s: (k, s[0], i, 0)),
                  pl.BlockSpec((1, tr, cols), lambda k, i, s: (k, i, 0))],
        out_specs=pl.BlockSpec((1, tr, cols), lambda k, i, s: (k, i, 0)),
    )
    return pl.pallas_call(
        body, name=name, grid_spec=grid_spec, out_shape=_sds((4, rows, cols), a4.dtype),
        compiler_params=_cp(("arbitrary", "arbitrary")),
    )(_where_am_i(), a4, recv)


def _adamw_rs(a4, recv, from_chips, w, m, v, name):
    rows, cols = w.shape
    tr = _row_tile(rows, cols, 3 << 19)

    def body(sel_ref, own_ref, recv_ref, fc_ref, w_ref, m_ref, v_ref, g_ref, d_ref, nm_ref, nv_ref):
        g = own_ref[0, 0].astype(F32) + recv_ref[0].astype(F32)
        for j in range(3):
            g = g + fc_ref[j].astype(F32)
        d, nm, nv = _adamw_math(w_ref[...], g, m_ref[...], v_ref[...])
        g_ref[...] = g
        d_ref[...] = d
        nm_ref[...] = nm
        nv_ref[...] = nv

    flat = pl.BlockSpec((tr, cols), lambda i, s: (i, 0))
    grid_spec = pltpu.PrefetchScalarGridSpec(
        num_scalar_prefetch=1, grid=(rows // tr,),
        in_specs=[pl.BlockSpec((1, 1, tr, cols), lambda i, s: (s[1], s[0], i, 0)),
                  pl.BlockSpec((1, tr, cols), lambda i, s: (s[1], i, 0)),
                  pl.BlockSpec((3, tr, cols), lambda i, s: (0, i, 0)), flat, flat, flat],
        out_specs=[flat] * 4,
    )
    return pl.pallas_call(
        body, name=name, grid_spec=grid_spec, out_shape=[_sds((rows, cols))] * 4,
        compiler_params=_cp(("arbitrary",)),
    )(_where_am_i(), a4, recv, from_chips, w, m, v)


_SMALL_F32 = (("g_mix", (8, 128)), ("b_f", (1, 8)), ("g_q", (1, 64)), ("g_k", (1, 64)), ("lambda_re", (32, 64)),
              ("lambda_im", (32, 64)), ("log_step", (1, 32)), ("d_skip", (32, 16)), ("b_glu", (4, 128)),
              ("g_attn_out", (4, 128)), ("g_ssm_out", (4, 128)), ("g_ffn", (8, 128)), ("conv_b", (44, 128)))
_SMALL_PAIRS = (("b_re", "b_im"), ("c_re", "c_im"))
_PAIR_SHAPE = (NG * GS, NP)


def _ceil8(r):
    return -(-r // 8) * 8


def _small_names():
    return [n for n, _ in _SMALL_F32] + [n for pair in _SMALL_PAIRS for n in pair]


def _pack_small_grads(g, loss_dev):
    parts = [jnp.pad(g[n].reshape(r, c), ((0, _ceil8(r) - r), (0, LANES - c))) for n, (r, c) in _SMALL_F32]
    parts.append(jnp.pad(loss_dev, ((0, 7), (0, LANES - 1))))
    pairs = [jnp.concatenate([g[a].reshape(_PAIR_SHAPE), g[b].reshape(_PAIR_SHAPE)], axis=1) for a, b in _SMALL_PAIRS]
    return jnp.concatenate(parts, axis=0), jnp.concatenate(pairs, axis=0).astype(BF16)


def _adamw_small(gall, gall_b, wmv):
    names = _small_names()
    shapes = [s for _, s in _SMALL_F32] + [_PAIR_SHAPE] * (2 * len(_SMALL_PAIRS))
    npar = len(names)

    def body(ga_ref, gb_ref, *rest):
        ins, loss_ref, outs = rest[:3 * npar], rest[3 * npar], rest[3 * npar + 1:]
        ga, gb = ga_ref[0], gb_ref[0].astype(F32)
        for dev in range(1, NDEV):
            ga = ga + ga_ref[dev]
            gb = gb + gb_ref[dev].astype(F32)
        grads, off = [], 0
        for _, (r, c) in _SMALL_F32:
            grads.append(ga[off:off + r, 0:c])
            off += _ceil8(r)
        loss_ref[...] = ga[off:off + 8]
        rows, cols = _PAIR_SHAPE
        for j in range(len(_SMALL_PAIRS)):
            for h in range(2):
                grads.append(gb[rows * j:rows * (j + 1), cols * h:cols * (h + 1)])
        for i, g in enumerate(grads):
            d, nm, nv = _adamw_math(ins[3 * i][...], g, ins[3 * i + 1][...], ins[3 * i + 2][...])
            outs[4 * i][...] = g
            outs[4 * i + 1][...] = d
            outs[4 * i + 2][...] = nm
            outs[4 * i + 3][...] = nv

    flat = [a for n in names for a in wmv[n]]
    out_shape = [_sds((8, LANES))] + [_sds(s) for s in shapes for _ in range(4)]
    res = pl.pallas_call(body, name="adamw_small", out_shape=out_shape, compiler_params=_cp())(gall, gall_b, *flat)
    return res[0], {n: res[1 + 4 * i:5 + 4 * i] for i, n in enumerate(names)}


def _block_diag(m):
    eye = jnp.eye(8, dtype=m.dtype)
    r, c = m.shape[2], m.shape[3]
    return (m[:, :, :, None, :] * eye[None, :, None, :, None]).reshape(4, 8 * r, 8 * c)


def _diag_blocks(dense, r, c):
    eye = jnp.eye(8, dtype=dense.dtype)
    return jnp.sum(dense.reshape(4, 8, r, 8, c) * eye[None, :, None, :, None], axis=3)


def kernel(x, g_mix, w_in, b_f, g_q, g_k, lambda_re, lambda_im, log_step, b_re, b_im, c_re, c_im, d_skip, w_glu, b_glu, g_attn_out, g_ssm_out, w_out, g_ffn, w_up, conv_w, conv_b, w_down, loss_target, m_g_mix, m_w_in, m_b_f, m_g_q, m_g_k, m_lambda_re, m_lambda_im, m_log_step, m_b_re, m_b_im, m_c_re, m_c_im, m_d_skip, m_w_glu, m_b_glu, m_g_attn_out, m_g_ssm_out, m_w_out, m_g_ffn, m_w_up, m_conv_w, m_conv_b, m_w_down, v_g_mix, v_w_in, v_b_f, v_g_q, v_g_k, v_lambda_re, v_lambda_im, v_log_step, v_b_re, v_b_im, v_c_re, v_c_im, v_d_skip, v_w_glu, v_b_glu, v_g_attn_out, v_g_ssm_out, v_w_out, v_g_ffn, v_w_up, v_conv_w, v_conv_b, v_w_down):
    weights = dict(g_mix=g_mix, w_in=w_in, b_f=b_f, g_q=g_q, g_k=g_k, lambda_re=lambda_re, lambda_im=lambda_im,
                   log_step=log_step, b_re=b_re, b_im=b_im, c_re=c_re, c_im=c_im, d_skip=d_skip, w_glu=w_glu,
                   b_glu=b_glu, g_attn_out=g_attn_out, g_ssm_out=g_ssm_out, w_out=w_out, g_ffn=g_ffn, w_up=w_up,
                   conv_w=conv_w, conv_b=conv_b, w_down=w_down)
    mom_m = dict(g_mix=m_g_mix, w_in=m_w_in, b_f=m_b_f, g_q=m_g_q, g_k=m_g_k, lambda_re=m_lambda_re,
                 lambda_im=m_lambda_im, log_step=m_log_step, b_re=m_b_re, b_im=m_b_im, c_re=m_c_re, c_im=m_c_im,
                 d_skip=m_d_skip, w_glu=m_w_glu, b_glu=m_b_glu, g_attn_out=m_g_attn_out, g_ssm_out=m_g_ssm_out,
                 w_out=m_w_out, g_ffn=m_g_ffn, w_up=m_w_up, conv_w=m_conv_w, conv_b=m_conv_b, w_down=m_w_down)
    mom_v = dict(g_mix=v_g_mix, w_in=v_w_in, b_f=v_b_f, g_q=v_g_q, g_k=v_g_k, lambda_re=v_lambda_re,
                 lambda_im=v_lambda_im, log_step=v_log_step, b_re=v_b_re, b_im=v_b_im, c_re=v_c_re, c_im=v_c_im,
                 d_skip=v_d_skip, w_glu=v_w_glu, b_glu=v_b_glu, g_attn_out=v_g_attn_out, g_ssm_out=v_g_ssm_out,
                 w_out=v_w_out, g_ffn=v_g_ffn, w_up=v_w_up, conv_w=v_conv_w, conv_b=v_conv_b, w_down=v_w_down)
    names = list(weights)
    big = ("w_in", "w_glu", "w_out", "w_up", "w_down", "conv_w")

    xs = x[0]
    target = loss_target[0]
    row = lambda a: a.reshape(1, -1)

    (win_g,) = _all_gather([w_in.T.astype(BF16)], "gather_w_in")
    w_in_t = win_g.reshape(8 * 257, D)
    wcat = jnp.concatenate([w_in_t[0:1536], w_in_t[1544:2056], jnp.pad(w_in_t[1536:1544], ((0, LANES - NH), (0, 0)))],
                           axis=0)
    cb4 = conv_b.reshape(2, 4, 1, CS)

    h1, z, f_t, (wg_g, wo_g) = _in_proj(xs, row(g_mix), wcat, [w_glu.astype(BF16), w_out.astype(BF16)])
    bf_col = b_f.reshape(NH, 1)
    ck = _gates_fwd(f_t, bf_col).reshape(_HP, 2, T)
    att, lse, (wu_g, cw_g), (wg_g, wo_g) = _attn_fwd(z, ck, row(g_q), row(g_k), [w_up.T.astype(BF16), conv_w],
                                                       [wg_g, wo_g])

    ls_col = log_step.reshape(NG, 1)
    br_t, bi_t = b_re.transpose(0, 2, 1), b_im.transpose(0, 2, 1)
    ab_re, ab_im, bb_re, bb_im = _disc_fwd(lambda_re, lambda_im, ls_col, br_t, bi_t)
    bb = jnp.concatenate([_block_diag(bb_re.reshape(4, 8, GS, NP)), _block_diag(bb_im.reshape(4, 8, GS, NP))],
                         axis=2).astype(BF16)
    ab = jnp.stack([ab_re.reshape(4, 512), ab_im.reshape(4, 512)], axis=1)
    cdiag = lambda cm: _block_diag(cm.reshape(4, 8, GS, NP).transpose(0, 1, 3, 2))
    cc = jnp.concatenate([cdiag(c_re), -cdiag(c_im)], axis=1).astype(BF16)
    dsk = d_skip.reshape(4, 1, LANES)
    y, xs_il, (wd_g,) = _ssm_fwd(z, bb, ab, cc, dsk, [w_down.astype(BF16)])
    wg_f = wg_g.reshape(SW, SW)
    wo_f = wo_g.reshape(D, D)
    x1, (wu_g, cw_g, wd_g) = _mix_fwd(att, y, xs, wg_f, row(b_glu), row(g_attn_out), row(g_ssm_out), wo_f,
                                      [wu_g, cw_g, wd_g])
    wu4 = wu_g.reshape(2, 4, CS, D)
    wd4 = wd_g.reshape(4, CS, D)
    cw4 = cw_g.reshape(2, 4, 3, CS)
    h2, up4 = _ffn_up(x1, row(g_ffn), wu4)
    act = _ffn_act(up4, cw4, cb4)
    loss_dev, dx2, dx2b = _ffn_down_loss(act, wd4, x1, target)

    dact = _ffn_dact(dx2b, wd4)
    d_wd = _ffn_dwd(act, dx2b)
    dup4, dcw4, dcb4 = _ffn_act_bwd(up4, dact, cw4, cb4)
    d_wu = _ffn_dwu(h2, dup4)
    dx1, dg_ffn = _ffn_up_bwd(dup4, wu4, x1, row(g_ffn), dx2)
    datt, dy, d_wg, d_bg, d_ga, d_gs, d_wo = _mix_bwd(att, y, dx1, wg_f, row(b_glu), row(g_attn_out),
                                                       row(g_ssm_out), wo_f)
    early = ("w_up", "w_down", "conv_w", "w_out", "w_glu")
    by_dest = {"w_up": d_wu.reshape(4, 2, CS, D), "w_down": d_wd.reshape(4, 2, DFF // NDEV, D),
               "conv_w": dcw4.reshape(4, 2, 3, CS), "w_out": d_wo.reshape(4, 2, D // NDEV, D),
               "w_glu": d_wg.reshape(4, 2, SW // NDEV, SW)}
    du, dbb, dab, dcc, ddsk, fs = _ssm_bwd(z, dy, xs_il, bb, ab, cc, dsk, [by_dest[n] for n in early])
    from_sibling = dict(zip(early, fs))
    chip_sums = {n: _pair_sum(by_dest[n], from_sibling[n], "rs_pair_sum_" + n) for n in early}
    du = du.astype(BF16)
    dbb_re = _diag_blocks(dbb[:, :, 0:512], GS, NP).reshape(NG, GS, NP)
    dbb_im = _diag_blocks(dbb[:, :, 512:1024], GS, NP).reshape(NG, GS, NP)
    dlr, dli, dls, dbr_t, dbi_t = _disc_bwd(lambda_re, lambda_im, ls_col, br_t, bi_t,
                                            dab[:, 0].reshape(NG, NP), dab[:, 1].reshape(NG, NP), dbb_re, dbb_im)
    d_cre = _diag_blocks(dcc[:, 0:512], NP, GS).transpose(0, 1, 3, 2).reshape(NG, GS, NP)
    d_cim = -_diag_blocks(dcc[:, 512:1024], NP, GS).transpose(0, 1, 3, 2).reshape(NG, GS, NP)

    dq, dk, dv, dck, d_gq, d_gk, fc = _attn_bwd(z, ck, row(g_q), row(g_k), lse, datt, [chip_sums[n] for n in early])
    from_chips = dict(zip(early, fc))
    df_t, d_bf = _gates_bwd(dck.reshape(NH, T), f_t, bf_col)
    df = jnp.concatenate([df_t.T, jnp.zeros((T, LANES - NH), F32)], axis=1).astype(BF16)
    d_wcat = _in_proj_dw(h1, dq, dk, dv, du, df)
    d_win = jnp.concatenate([d_wcat[0:1536], d_wcat[2048:2048 + NH], d_wcat[1536:2048]], axis=0)
    by_dest["w_in"] = d_win.astype(BF16).reshape(4, 2, 257, D)
    (from_sibling["w_in"],) = _swap_sibling([by_dest["w_in"]], "rs_pair_swap_w_in")
    chip_sums["w_in"] = _pair_sum(by_dest["w_in"], from_sibling["w_in"], "rs_pair_sum_w_in")
    grad_x, dg_mix, (from_chips["w_in"],) = _in_proj_bwd(dq, dk, dv, du, df, wcat, xs, row(g_mix), dx1, [chip_sums["w_in"]])

    grads, deltas, new_m, new_v = {}, {}, {}, {}
    for n in early + ("w_in",):
        flip = (lambda a: a.T) if n in ("w_up", "w_in") else (lambda a: a)
        outs = _adamw_rs(by_dest[n], from_sibling[n], from_chips[n], flip(weights[n]), flip(mom_m[n]), flip(mom_v[n]),
                         "adamw_" + n)
        grads[n], deltas[n], new_m[n], new_v[n] = (flip(o) for o in outs)

    small = dict(g_mix=dg_mix, b_f=d_bf, g_q=d_gq, g_k=d_gk, lambda_re=dlr, lambda_im=dli, log_step=dls,
                 b_re=dbr_t, b_im=dbi_t, c_re=d_cre, c_im=d_cim, d_skip=ddsk, b_glu=d_bg, g_attn_out=d_ga,
                 g_ssm_out=d_gs, g_ffn=dg_ffn, conv_b=dcb4)
    small_all, small_all_b = _all_gather(list(_pack_small_grads(small, loss_dev)), "gather_small_grads")
    views = dict(_SMALL_F32)
    swapped = ("b_re", "b_im")

    def view(n, a):
        return a.transpose(0, 2, 1).reshape(_PAIR_SHAPE) if n in swapped else a.reshape(views.get(n, _PAIR_SHAPE))

    loss_p, small_out = _adamw_small(small_all, small_all_b,
                                     {n: (view(n, weights[n]), view(n, mom_m[n]), view(n, mom_v[n])) for n in small})
    loss = loss_p[0, 0]
    for n in small:
        back = (lambda a: a.reshape(NG, GS, NP).transpose(0, 2, 1)) if n in swapped else (lambda a: a.reshape(weights[n].shape))
        grads[n], deltas[n], new_m[n], new_v[n] = (back(o) for o in small_out[n])

    return (loss, grad_x[None], *[grads[n] for n in names], *[deltas[n] for n in names],
            *[new_m[n] for n in names], *[new_v[n] for n in names])
```

```python
import jax
import jax.numpy as jnp
from jax import lax
from jax.experimental import pallas as pl
from jax.experimental.pallas import tpu as pltpu

F32, BF16 = jnp.float32, jnp.bfloat16
T, D = 2048, 1024
NH, DH = 8, 64
AW, SW = 512, 512
NG, GS, NP = 32, 16, 64
DFF = 2816
NDEV = 8
CS = 2 * DFF // NDEV
EPS = 1e-6
NEG = -1e30
ZC = 4 * 512 + 128
SEG = 256
NSEG = T // SEG
_SCAN_UNROLL = 4
BQ = 512
VMEM_LIMIT = 56 * 1024 * 1024
LANES = 128
MESH = pl.DeviceIdType.MESH
HI = lax.Precision.HIGHEST

ADAM_LR, ADAM_B1, ADAM_B2, ADAM_EPS, ADAM_WD, ADAM_STEP = 0.001, 0.9, 0.999, 1e-08, 0.01, 10


def _cp(dims=None):
    if dims is None:
        return pltpu.CompilerParams(vmem_limit_bytes=VMEM_LIMIT)
    return pltpu.CompilerParams(dimension_semantics=dims, vmem_limit_bytes=VMEM_LIMIT)


def _sds(shape, dtype=F32):
    return jax.ShapeDtypeStruct(shape, dtype)


def _nt(a, b):
    return lax.dot_general(a, b, (((1,), (1,)), ((), ())), preferred_element_type=F32)


def _tn(a, b):
    return lax.dot_general(a, b, (((0,), (0,)), ((), ())), preferred_element_type=F32)


def _nn(a, b):
    return jnp.dot(a, b, preferred_element_type=F32)


def _rms_r(x):
    return lax.rsqrt(jnp.mean(x * x, axis=-1, keepdims=True) + EPS)


def _rms_bwd(n, r, g, dh):
    dn = dh * g
    dx = r * (dn - n * jnp.mean(dn * n, axis=-1, keepdims=True))
    return dx, jnp.sum(dh * n, axis=0, keepdims=True)


def _row_tile(rows, cols, limit):
    tr = rows
    while tr * cols * 4 > limit and tr % 32 == 0:
        tr //= 2
    return tr


def _place():
    return lax.axis_index("x"), lax.axis_index("y"), lax.axis_index("c")


def _all_gather(shards, name):
    n = len(shards)

    def body(*refs):
        x_refs, out_refs = refs[:n], refs[n:2 * n]
        send_sems, recv_sems, local_sems = refs[2 * n:]
        x, y, c = _place()
        me, sibling = (x, y, c), (x, y, 1 - c)
        chips = [(1 - x, y), (x, 1 - y), (1 - x, 1 - y)]

        def copy(a, k, block, to, src=None):
            px, py, pc = block
            slot = out_refs[a].at[4 * px + 2 * py + pc]
            return pltpu.make_async_remote_copy(
                src_ref=slot if src is None else src, dst_ref=slot,
                send_sem=send_sems.at[7 * a + k], recv_sem=recv_sems.at[7 * a + k], device_id=to, device_id_type=MESH)

        mine, first, passed = [], [], []
        for a in range(n):
            cp = pltpu.make_async_copy(x_refs[a], out_refs[a].at[4 * x + 2 * y + c], local_sems.at[a])
            cp.start()
            mine.append(cp)
            cps = [copy(a, 0, me, sibling, src=x_refs[a])]
            cps += [copy(a, 1 + j, me, (*chip, c), src=x_refs[a]) for j, chip in enumerate(chips)]
            for cp in cps:
                cp.start()
            first += cps
        for a in range(n):
            for j, chip in enumerate(chips):
                copy(a, 1 + j, (*chip, c), me).wait_recv()
                fwd = copy(a, 4 + j, (*chip, c), sibling)
                fwd.start()
                passed.append(fwd)
        for a in range(n):
            copy(a, 0, sibling, me).wait_recv()
            for j, chip in enumerate(chips):
                copy(a, 4 + j, (*chip, 1 - c), me).wait_recv()
        for cp in first + passed:
            cp.wait_send()
        for cp in mine:
            cp.wait()

    anyspec = pl.BlockSpec(memory_space=pl.ANY)
    return pl.pallas_call(
        body, name=name,
        out_shape=[_sds((NDEV, *s.shape), s.dtype) for s in shards],
        in_specs=[anyspec] * n, out_specs=[anyspec] * n,
        scratch_shapes=[pltpu.SemaphoreType.DMA((7 * n,)), pltpu.SemaphoreType.DMA((7 * n,)),
                        pltpu.SemaphoreType.DMA((n,))],
    )(*shards)


def _gather_first_copies(x_refs, out_refs, send_sems, recv_sems, local_sems):
    x, y, c = _place()
    slot = 4 * x + 2 * y + c
    peers = [(x, y, 1 - c), (1 - x, y, c), (x, 1 - y, c), (1 - x, 1 - y, c)]
    local, remote = [], []
    for a, (x_ref, out_ref) in enumerate(zip(x_refs, out_refs)):
        local.append(pltpu.make_async_copy(x_ref, out_ref.at[slot], local_sems.at[a]))
        for k, peer in enumerate(peers):
            remote.append(pltpu.make_async_remote_copy(
                src_ref=x_ref, dst_ref=out_ref.at[slot], send_sem=send_sems.at[4 * a + k],
                recv_sem=recv_sems.at[4 * a + k], device_id=peer, device_id_type=MESH))
    return local, remote


def _gather_first_scratch(n):
    return [pltpu.SemaphoreType.DMA((4 * n,)), pltpu.SemaphoreType.DMA((4 * n,)), pltpu.SemaphoreType.DMA((n,))]


def _chip_swap_copies(p_refs, out_refs, send_sems, recv_sems):
    x, y, c = _place()
    chips = [(1 - x, y), (x, 1 - y), (1 - x, 1 - y)]
    cps = []
    for a, (p_ref, out_ref) in enumerate(zip(p_refs, out_refs)):
        for j, (px, py) in enumerate(chips):
            cps.append(pltpu.make_async_remote_copy(
                src_ref=p_ref.at[2 * px + py], dst_ref=out_ref.at[j],
                send_sem=send_sems.at[3 * a + j], recv_sem=recv_sems.at[3 * a + j],
                device_id=(px, py, c), device_id_type=MESH))
    return cps


def _sibling_swap_copies(a_refs, out_refs, send_sems, recv_sems):
    x, y, c = _place()
    cps = []
    for a, (a_ref, out_ref) in enumerate(zip(a_refs, out_refs)):
        for k in range(4):
            cps.append(pltpu.make_async_remote_copy(
                src_ref=a_ref.at[k, 1 - c], dst_ref=out_ref.at[k],
                send_sem=send_sems.at[4 * a + k], recv_sem=recv_sems.at[4 * a + k],
                device_id=(x, y, 1 - c), device_id_type=MESH))
    return cps


def _pass_on_copies(buf_refs, send_sems, recv_sems):
    x, y, c = _place()
    chips = [(1 - x, y), (x, 1 - y), (1 - x, 1 - y)]
    cps = []
    for a, buf in enumerate(buf_refs):
        for j, (px, py) in enumerate(chips):
            block = buf.at[4 * px + 2 * py + c]
            cps.append(pltpu.make_async_remote_copy(
                src_ref=block, dst_ref=block, send_sem=send_sems.at[3 * a + j], recv_sem=recv_sems.at[3 * a + j],
                device_id=(x, y, 1 - c), device_id_type=MESH))
    return cps


def _swap_sibling(arrs, name):
    n = len(arrs)

    def body(*refs):
        cps = _sibling_swap_copies(refs[:n], refs[n:2 * n], *refs[2 * n:])
        for cp in cps:
            cp.start()
        for cp in cps:
            cp.wait()

    anyspec = pl.BlockSpec(memory_space=pl.ANY)
    return pl.pallas_call(
        body, name=name,
        out_shape=[_sds((4, *a.shape[2:]), a.dtype) for a in arrs],
        in_specs=[anyspec] * n, out_specs=[anyspec] * n,
        scratch_shapes=[pltpu.SemaphoreType.DMA((4 * n,)), pltpu.SemaphoreType.DMA((4 * n,))],
    )(*arrs)


def _in_proj(x, g, wcat, shards):
    tm = 256
    n = len(shards)

    def body(x_ref, g_ref, w_ref, *rest):
        x_refs, (h_ref, z_ref, ft_ref), out_refs, sems = rest[:n], rest[n:n + 3], rest[n + 3:2 * n + 3], rest[2 * n + 3:]

        @pl.when(pl.program_id(0) == 0)
        def _():
            local, remote = _gather_first_copies(x_refs, out_refs, *sems)
            for cp in local + remote:
                cp.start()

        xv = x_ref[...]
        h = (xv * _rms_r(xv) * g_ref[...]).astype(BF16)
        h_ref[...] = h
        z = _nt(h, w_ref[...])
        z_ref[...] = z
        ft_ref[...] = z[:, 2048:ZC].T[0:NH, :]

        @pl.when(pl.program_id(0) == T // tm - 1)
        def _():
            local, remote = _gather_first_copies(x_refs, out_refs, *sems)
            for cp in remote + local:
                cp.wait()

    row = lambda i: (i, 0)
    fixed = lambda i: (0, 0)
    anyspec = pl.BlockSpec(memory_space=pl.ANY)
    res = pl.pallas_call(
        body, name="in_proj", grid=(T // tm,),
        in_specs=[pl.BlockSpec((tm, D), row), pl.BlockSpec((1, D), fixed), pl.BlockSpec((ZC, D), fixed)] + [anyspec] * n,
        out_specs=[pl.BlockSpec((tm, D), row), pl.BlockSpec((tm, ZC), row), pl.BlockSpec((NH, tm), lambda i: (0, i))]
        + [anyspec] * n,
        out_shape=[_sds((T, D), BF16), _sds((T, ZC)), _sds((NH, T))] + [_sds((NDEV, *s.shape), s.dtype) for s in shards],
        scratch_shapes=_gather_first_scratch(n),
        compiler_params=_cp(("arbitrary",)),
    )(x, g, wcat, *shards)
    return res[0], res[1], res[2], res[3:]


def _in_proj_dw(h, dq, dk, dv, du, df):
    tm = 512

    def body(h_ref, dq_ref, dk_ref, dv_ref, du_ref, df_ref, o_ref):
        hv = h_ref[...]
        for j, d_ref in enumerate((dq_ref, dk_ref, dv_ref, du_ref)):
            o_ref[512 * j:512 * (j + 1), :] = _tn(d_ref[...], hv)
        o_ref[2048:ZC, :] = _tn(df_ref[...], hv)

    full = lambda w: pl.BlockSpec((T, w), lambda i: (0, 0))
    return pl.pallas_call(
        body, name="in_proj_dw", grid=(D // tm,),
        in_specs=[pl.BlockSpec((T, tm), lambda i: (0, i)), full(512), full(512), full(512), full(512), full(LANES)],
        out_specs=pl.BlockSpec((ZC, tm), lambda i: (0, i)), out_shape=_sds((ZC, D)),
        compiler_params=_cp(("parallel",)),
    )(h, dq, dk, dv, du, df)


def _in_proj_bwd(dq, dk, dv, du, df, wcat, x, g, dres, parts):
    tm = 256
    n = len(parts)

    def body(dq_ref, dk_ref, dv_ref, du_ref, df_ref, w_ref, x_ref, g_ref, dres_ref, *rest):
        p_refs, (dx_ref, dg_ref), fc_refs, sems = rest[:n], rest[n:n + 2], rest[n + 2:2 * n + 2], rest[2 * n + 2:]

        @pl.when(pl.program_id(0) == 0)
        def _():
            for cp in _chip_swap_copies(p_refs, fc_refs, *sems):
                cp.start()

        dh = _nn(df_ref[...], w_ref[2048:ZC, :])
        for j, d_ref in enumerate((dq_ref, dk_ref, dv_ref, du_ref)):
            dh = dh + _nn(d_ref[...], w_ref[512 * j:512 * (j + 1), :])
        xv = x_ref[...]
        r = _rms_r(xv)
        dxr, dgp = _rms_bwd(xv * r, r, g_ref[...], dh)
        dx_ref[...] = dres_ref[...] + dxr

        @pl.when(pl.program_id(0) == 0)
        def _():
            dg_ref[...] = jnp.zeros_like(dg_ref)
        dg_ref[...] += dgp

        @pl.when(pl.program_id(0) == T // tm - 1)
        def _():
            for cp in _chip_swap_copies(p_refs, fc_refs, *sems):
                cp.wait()

    row = lambda i: (i, 0)
    fixed = lambda i: (0, 0)
    part = pl.BlockSpec((tm, 512), row)
    anyspec = pl.BlockSpec(memory_space=pl.ANY)
    res = pl.pallas_call(
        body, name="in_proj_bwd", grid=(T // tm,),
        in_specs=[part, part, part, part, pl.BlockSpec((tm, LANES), row), pl.BlockSpec((ZC, D), fixed),
                  pl.BlockSpec((tm, D), row), pl.BlockSpec((1, D), fixed), pl.BlockSpec((tm, D), row)] + [anyspec] * n,
        out_specs=[pl.BlockSpec((tm, D), row), pl.BlockSpec((1, D), fixed)] + [anyspec] * n,
        out_shape=[_sds((T, D)), _sds((1, D))] + [_sds((3, *p.shape[1:]), p.dtype) for p in parts],
        scratch_shapes=[pltpu.SemaphoreType.DMA((3 * n,)), pltpu.SemaphoreType.DMA((3 * n,))],
        compiler_params=_cp(("arbitrary",)),
    )(dq, dk, dv, du, df, wcat, x, g, dres, *parts)
    return res[0], res[1], res[2:]


def _stack_lanes(v):
    return jnp.concatenate([v[:, LANES * b:LANES * (b + 1)] for b in range(T // LANES)], axis=0)


def _unstack_lanes(s):
    return jnp.concatenate([s[8 * b:8 * b + 8, :] for b in range(T // LANES)], axis=1)


def _cumsum_lanes(v, reverse):
    st = _stack_lanes(v)
    ri = lax.broadcasted_iota(jnp.int32, (LANES, LANES), 0)
    ci = lax.broadcasted_iota(jnp.int32, (LANES, LANES), 1)
    tri = (ri >= ci) if reverse else (ri <= ci)
    intra = jnp.dot(st, tri.astype(F32), precision=HI, preferred_element_type=F32)
    tot = intra[:, 0:1] if reverse else intra[:, LANES - 1:LANES]
    same_head = (ci % 8) == (ri % 8)
    other = (ci // 8 > ri // 8) if reverse else (ci // 8 < ri // 8)
    off = jnp.dot((same_head & other).astype(F32), jnp.broadcast_to(tot, (LANES, LANES)), precision=HI,
                  preferred_element_type=F32)
    return _unstack_lanes(intra + off)


def _gates_fwd(f_t, b_f):
    def body(f_ref, b_ref, c_ref):
        z = f_ref[...] + b_ref[...]
        lf = jnp.minimum(z, 0.0) - jnp.log(1.0 + jnp.exp(-jnp.abs(z)))
        c_ref[...] = _cumsum_lanes(lf, reverse=False)

    return pl.pallas_call(body, name="gates_fwd", out_shape=_sds((NH, T)), compiler_params=_cp())(f_t, b_f)


def _gates_bwd(dck, f_t, b_f):
    def body(d_ref, f_ref, b_ref, df_ref, db_ref):
        z = f_ref[...] + b_ref[...]
        dlf = _cumsum_lanes(d_ref[...], reverse=True)
        df = dlf * jax.nn.sigmoid(-z)
        df_ref[...] = df
        db_ref[...] = jnp.sum(df, axis=1, keepdims=True)

    return pl.pallas_call(body, name="gates_bwd", out_shape=[_sds((NH, T)), _sds((NH, 1))],
                          compiler_params=_cp())(dck, f_t, b_f)


def _lower_triangle():
    rows = lax.broadcasted_iota(jnp.int32, (BQ, BQ), 0)
    cols = lax.broadcasted_iota(jnp.int32, (BQ, BQ), 1)
    return cols <= rows


def _logit_parts(qb, kb, ck_row, lo, hi, tri):
    parts = []
    if lo > 0:
        parts.append((_nt(qb[lo:hi], kb[0:lo]) - ck_row[:, 0:lo], 0, lo))
    parts.append((jnp.where(tri, _nt(qb[lo:hi], kb[lo:hi]) - ck_row[:, lo:hi], NEG), lo, hi))
    return parts


_HP = NH // 2


def _qkv_specs():
    return [pl.BlockSpec((T, LANES), lambda p: (0, p)), pl.BlockSpec((T, LANES), lambda p: (0, _HP + p)),
            pl.BlockSpec((T, LANES), lambda p: (0, 2 * _HP + p))]


def _attn_fwd(z, ck, g_q, g_k, shards, bufs):
    scale = DH ** -0.5
    n, m = len(shards), len(bufs)

    def body(q_ref, k_ref, v_ref, ck_ref, gq_ref, gk_ref, *rest):
        x_refs, (o_ref, lse_ref) = rest[:n], rest[n + m:n + m + 2]
        out_refs, buf_refs, sems = rest[n + m + 2:2 * n + m + 2], rest[2 * n + m + 2:2 * n + 2 * m + 2], rest[2 * n + 2 * m + 2:]

        @pl.when(pl.program_id(0) == 0)
        def _():
            local, remote = _gather_first_copies(x_refs, out_refs, *sems[:3])
            for cp in local + remote + _pass_on_copies(buf_refs, *sems[3:]):
                cp.start()

        qb, kb, vb = [], [], []
        for hh in range(2):
            cols = slice(DH * hh, DH * (hh + 1))
            qv, kv = q_ref[:, cols], k_ref[:, cols]
            qb.append((qv * _rms_r(qv) * gq_ref[...] * scale).astype(BF16))
            kb.append((kv * _rms_r(kv) * gk_ref[...]).astype(BF16))
            vb.append(v_ref[:, cols].astype(BF16))
        tri = _lower_triangle()
        for i in range(T // BQ):
            lo, hi = i * BQ, (i + 1) * BQ
            outs = []
            for hh in range(2):
                parts = _logit_parts(qb[hh], kb[hh], ck_ref[0, hh:hh + 1, :], lo, hi, tri)
                top = jnp.max(parts[-1][0], axis=-1, keepdims=True)
                if lo > 0:
                    top = jnp.maximum(top, jnp.max(parts[0][0], axis=-1, keepdims=True))
                l, o = 0.0, 0.0
                for s, k0, k1 in parts:
                    p = jnp.exp(s - top)
                    l = l + jnp.sum(p, axis=-1, keepdims=True)
                    o = o + _nn(p.astype(BF16), vb[hh][k0:k1])
                outs.append(o / l)
                lse_ref[0, lo:hi, hh:hh + 1] = top + jnp.log(l)
            o_ref[lo:hi, :] = jnp.concatenate(outs, axis=1)

        @pl.when(pl.program_id(0) == _HP - 1)
        def _():
            local, remote = _gather_first_copies(x_refs, out_refs, *sems[:3])
            for cp in remote + local + _pass_on_copies(buf_refs, *sems[3:]):
                cp.wait()

    fixed = lambda p: (0, 0)
    anyspec = pl.BlockSpec(memory_space=pl.ANY)
    res = pl.pallas_call(
        body, name="attn_fwd", grid=(_HP,),
        in_specs=_qkv_specs() + [pl.BlockSpec((1, 2, T), lambda p: (p, 0, 0)), pl.BlockSpec((1, DH), fixed),
                                 pl.BlockSpec((1, DH), fixed)] + [anyspec] * (n + m),
        out_specs=[pl.BlockSpec((T, LANES), lambda p: (0, p)), pl.BlockSpec((1, T, 2), lambda p: (p, 0, 0))]
        + [anyspec] * (n + m),
        out_shape=[_sds((T, AW)), _sds((_HP, T, 2))] + [_sds((NDEV, *s.shape), s.dtype) for s in shards]
        + [_sds(b.shape, b.dtype) for b in bufs],
        input_output_aliases={6 + n + i: 2 + n + i for i in range(m)},
        scratch_shapes=_gather_first_scratch(n) + [pltpu.SemaphoreType.DMA((3 * m,)), pltpu.SemaphoreType.DMA((3 * m,))],
        compiler_params=_cp(("arbitrary",)),
    )(z, z, z, ck, g_q, g_k, *shards, *bufs)
    return res[0], res[1], res[2:2 + n], res[2 + n:]


def _attn_bwd(z, ck, g_q, g_k, lse, datt, parts):
    scale = DH ** -0.5
    n = len(parts)

    def body(q_ref, k_ref, v_ref, ck_ref, gq_ref, gk_ref, lse_ref, do_ref, *rest):
        p_refs, rest = rest[:n], rest[n:]
        dq_ref, dk_ref, dv_ref, dck_ref, dgq_ref, dgk_ref = rest[:6]
        fc_refs, (dkn_acc, dv_acc, dc_acc), sems = rest[6:6 + n], rest[6 + n:9 + n], rest[9 + n:]

        @pl.when(pl.program_id(0) == 0)
        def _():
            for cp in _chip_swap_copies(p_refs, fc_refs, *sems):
                cp.start()

        nq, nk, rq, rk, qb, kb, vb, dob = [], [], [], [], [], [], [], []
        for hh in range(2):
            cols = slice(DH * hh, DH * (hh + 1))
            qv, kv = q_ref[:, cols], k_ref[:, cols]
            rq.append(_rms_r(qv))
            rk.append(_rms_r(kv))
            nq.append(qv * rq[hh])
            nk.append(kv * rk[hh])
            qb.append((nq[hh] * gq_ref[...] * scale).astype(BF16))
            kb.append((nk[hh] * gk_ref[...]).astype(BF16))
            vb.append(v_ref[:, cols].astype(BF16))
            dob.append(do_ref[:, cols].astype(BF16))
        dkn_acc[...] = jnp.zeros_like(dkn_acc)
        dv_acc[...] = jnp.zeros_like(dv_acc)
        dc_acc[...] = jnp.zeros_like(dc_acc)
        dgq = jnp.zeros((1, DH), F32)
        tri = _lower_triangle()
        for i in range(T // BQ):
            lo, hi = i * BQ, (i + 1) * BQ
            dqs = []
            for hh in range(2):
                lse = lse_ref[0, lo:hi, hh:hh + 1]
                pd = []
                for s, k0, k1 in _logit_parts(qb[hh], kb[hh], ck_ref[0, hh:hh + 1, :], lo, hi, tri):
                    pd.append((jnp.exp(s - lse), _nt(dob[hh][lo:hi], vb[hh][k0:k1]), k0, k1))
                mean = sum(jnp.sum(p * dp, axis=-1, keepdims=True) for p, dp, _, _ in pd)
                dqn = 0.0
                for p, dp, k0, k1 in pd:
                    ds = p * (dp - mean)
                    dsb = ds.astype(BF16)
                    dv_acc[hh, k0:k1, :] += _tn(p.astype(BF16), dob[hh][lo:hi])
                    dkn_acc[hh, k0:k1, :] += _tn(dsb, qb[hh][lo:hi])
                    dc_acc[hh:hh + 1, k0:k1] -= jnp.sum(ds, axis=0, keepdims=True)
                    dqn = dqn + _nn(dsb, kb[hh][k0:k1])
                dqx, dgp = _rms_bwd(nq[hh][lo:hi], rq[hh][lo:hi], gq_ref[...], dqn * scale)
                dqs.append(dqx)
                dgq = dgq + dgp
            dq_ref[lo:hi, :] = jnp.concatenate(dqs, axis=1).astype(BF16)
        dks, dgk = [], jnp.zeros((1, DH), F32)
        for hh in range(2):
            dkx, dgp = _rms_bwd(nk[hh], rk[hh], gk_ref[...], dkn_acc[hh])
            dks.append(dkx)
            dgk = dgk + dgp
        dk_ref[...] = jnp.concatenate(dks, axis=1).astype(BF16)
        dv_ref[...] = jnp.concatenate([dv_acc[0], dv_acc[1]], axis=1).astype(BF16)
        dck_ref[0] = dc_acc[...]

        @pl.when(pl.program_id(0) == 0)
        def _():
            dgq_ref[...] = jnp.zeros_like(dgq_ref)
            dgk_ref[...] = jnp.zeros_like(dgk_ref)
        dgq_ref[...] += dgq
        dgk_ref[...] += dgk

        @pl.when(pl.program_id(0) == _HP - 1)
        def _():
            for cp in _chip_swap_copies(p_refs, fc_refs, *sems):
                cp.wait()

    fixed = lambda p: (0, 0)
    pair = pl.BlockSpec((T, LANES), lambda p: (0, p))
    gsp = pl.BlockSpec((1, DH), fixed)
    ckspec = pl.BlockSpec((1, 2, T), lambda p: (p, 0, 0))
    anyspec = pl.BlockSpec(memory_space=pl.ANY)
    res = pl.pallas_call(
        body, name="attn_bwd", grid=(_HP,),
        in_specs=_qkv_specs() + [ckspec, gsp, gsp, pl.BlockSpec((1, T, 2), lambda p: (p, 0, 0)), pair] + [anyspec] * n,
        out_specs=[pair, pair, pair, ckspec, gsp, gsp] + [anyspec] * n,
        out_shape=[_sds((T, AW), BF16)] * 3 + [_sds((_HP, 2, T)), _sds((1, DH)), _sds((1, DH))]
        + [_sds((3, *p.shape[1:]), p.dtype) for p in parts],
        scratch_shapes=[pltpu.VMEM((2, T, DH), F32), pltpu.VMEM((2, T, DH), F32), pltpu.VMEM((2, T), F32),
                        pltpu.SemaphoreType.DMA((3 * n,)), pltpu.SemaphoreType.DMA((3 * n,))],
        compiler_params=_cp(("arbitrary",)),
    )(z, z, z, ck, g_q, g_k, lse, datt, *parts)
    return (*res[:6], res[6:])


def _disc(lr, li, ls, br_t, bi_t):
    step = jnp.exp(ls)
    er = jnp.exp(lr * step)
    ab_re = er * jnp.cos(li * step)
    ab_im = er * jnp.sin(li * step)
    num_re = ab_re - 1.0
    num_im = ab_im
    den = lr * lr + li * li
    f_re = (num_re * lr + num_im * li) / den
    f_im = (num_im * lr - num_re * li) / den
    bb_re = f_re[:, None, :] * br_t - f_im[:, None, :] * bi_t
    bb_im = f_re[:, None, :] * bi_t + f_im[:, None, :] * br_t
    return ab_re, ab_im, bb_re, bb_im


def _disc_fwd(lr, li, ls, br_t, bi_t):
    def body(lr_ref, li_ref, ls_ref, br_ref, bi_ref, ar_ref, ai_ref, bbr_ref, bbi_ref):
        ar, ai, bbr, bbi = _disc(lr_ref[...], li_ref[...], ls_ref[...], br_ref[...], bi_ref[...])
        ar_ref[...] = ar
        ai_ref[...] = ai
        bbr_ref[...] = bbr
        bbi_ref[...] = bbi

    return pl.pallas_call(
        body, name="ssm_disc_fwd",
        out_shape=[_sds((NG, NP)), _sds((NG, NP)), _sds((NG, GS, NP)), _sds((NG, GS, NP))],
        compiler_params=_cp())(lr, li, ls, br_t, bi_t)


def _disc_bwd(lr, li, ls, br_t, bi_t, dar, dai, dbbr, dbbi):
    def body(lr_ref, li_ref, ls_ref, br_ref, bi_ref, dar_ref, dai_ref, dbbr_ref, dbbi_ref,
             dlr_ref, dli_ref, dls_ref, dbr_ref, dbi_ref):
        _, vjp = jax.vjp(_disc, lr_ref[...], li_ref[...], ls_ref[...], br_ref[...], bi_ref[...])
        dlr, dli, dls, dbr, dbi = vjp((dar_ref[...], dai_ref[...], dbbr_ref[...], dbbi_ref[...]))
        dlr_ref[...] = dlr
        dli_ref[...] = dli
        dls_ref[...] = dls
        dbr_ref[...] = dbr
        dbi_ref[...] = dbi

    return pl.pallas_call(
        body, name="ssm_disc_bwd",
        out_shape=[_sds((NG, NP)), _sds((NG, NP)), _sds((NG, 1)), _sds((NG, GS, NP)), _sds((NG, GS, NP))],
        compiler_params=_cp())(lr, li, ls, br_t, bi_t, dar, dai, dbbr, dbbi)


def _cmul(ar, ai, br, bi):
    return ar * br - ai * bi, ar * bi + ai * br


def _scan(buf, a_re, a_im, reverse, other=None):
    ar = [jnp.broadcast_to(a_re[:, LANES * k:LANES * (k + 1)], (NSEG, LANES)) for k in range(4)]
    ai = [jnp.broadcast_to(a_im[:, LANES * k:LANES * (k + 1)], (NSEG, LANES)) for k in range(4)]

    def tile(ref, i, k):
        return ref[pl.ds(pl.multiple_of(i * NSEG, NSEG), NSEG), LANES * k:LANES * (k + 1)]

    def advance(i, xr, xi):
        nr = [ar[k] * xr[k] - ai[k] * xi[k] + tile(buf, i, k) for k in range(4)]
        ni = [ar[k] * xi[k] + ai[k] * xr[k] + tile(buf, i, 4 + k) for k in range(4)]
        return nr, ni

    def index(i):
        return (SEG - 1 - i) if reverse else i

    zeros = tuple(jnp.zeros((NSEG, LANES), F32) for _ in range(4))

    def sweep1(i, carry):
        xr, xi = carry
        for sub in range(_SCAN_UNROLL):
            xr, xi = advance(index(i * _SCAN_UNROLL + sub), xr, xi)
        return tuple(xr), tuple(xi)

    er, ei = lax.fori_loop(0, SEG // _SCAN_UNROLL, sweep1, (zeros, zeros))

    row = lax.broadcasted_iota(jnp.int32, (NSEG, LANES), 0)
    sr, si = [], []
    for k in range(4):
        pr, pi = ar[k], ai[k]
        for _ in range(SEG.bit_length() - 1):
            pr, pi = _cmul(pr, pi, pr, pi)
        ir, ii = er[k], ei[k]
        for d in (1, 2, 4):
            shift = (NSEG - d) if reverse else d
            ok = (row < NSEG - d) if reverse else (row >= d)
            mr, mi = _cmul(pr, pi, pltpu.roll(ir, shift, 0), pltpu.roll(ii, shift, 0))
            ir = ir + jnp.where(ok, mr, 0.0)
            ii = ii + jnp.where(ok, mi, 0.0)
            pr, pi = _cmul(pr, pi, pr, pi)
        shift = (NSEG - 1) if reverse else 1
        ok = (row < NSEG - 1) if reverse else (row >= 1)
        sr.append(jnp.where(ok, pltpu.roll(ir, shift, 0), 0.0))
        si.append(jnp.where(ok, pltpu.roll(ii, shift, 0), 0.0))

    def sweep2(i, carry):
        xr, xi, dar, dai = carry
        for sub in range(_SCAN_UNROLL):
            idx = index(i * _SCAN_UNROLL + sub)
            if other is not None:
                orr = [tile(other, idx, k) for k in range(4)]
                oi = [tile(other, idx, 4 + k) for k in range(4)]
                dar = tuple(dar[k] + xr[k] * orr[k] + xi[k] * oi[k] for k in range(4))
                dai = tuple(dai[k] + xi[k] * orr[k] - xr[k] * oi[k] for k in range(4))
            xr, xi = advance(idx, xr, xi)
            start = pl.multiple_of(idx * NSEG, NSEG)
            for k in range(4):
                buf[pl.ds(start, NSEG), LANES * k:LANES * (k + 1)] = xr[k]
                buf[pl.ds(start, NSEG), LANES * (4 + k):LANES * (5 + k)] = xi[k]
        return tuple(xr), tuple(xi), dar, dai

    _, _, dar, dai = lax.fori_loop(0, SEG // _SCAN_UNROLL, sweep2, (tuple(sr), tuple(si), zeros, zeros))
    if other is None:
        return None
    da_re = jnp.concatenate([jnp.sum(d, axis=0, keepdims=True) for d in dar], axis=1)
    da_im = jnp.concatenate([jnp.sum(d, axis=0, keepdims=True) for d in dai], axis=1)
    return da_re, da_im


_SSM_BLOCKS = 4


def _seg_copies(hbm_ref, col0, buf, block, sems, to_hbm):
    cps = []
    for j in range(NSEG):
        hbm = hbm_ref.at[pl.ds(SEG * j, SEG), pl.ds(col0, LANES)]
        vmem = buf.at[block, :, j, :]
        cps.append(pltpu.make_async_copy(vmem, hbm, sems.at[block, j]) if to_hbm
                   else pltpu.make_async_copy(hbm, vmem, sems.at[block, j]))
    return cps


def _seg_scratch():
    return [pltpu.VMEM((_SSM_BLOCKS, SEG, NSEG, LANES), F32), pltpu.SemaphoreType.DMA((_SSM_BLOCKS, NSEG))]


_U_COL = 3 * AW


def _ssm_fwd(z, bb, ab, cc, dsk, shards):
    n = len(shards)
    last = _SSM_BLOCKS - 1

    def body(z_ref, bb_ref, a_ref, cc_ref, d_ref, *rest):
        x_refs, y_ref, xs_ref, out_refs = rest[:n], rest[n], rest[n + 1], rest[n + 2:2 * n + 2]
        ubuf, lsem, ybuf, ssem = rest[2 * n + 2:2 * n + 6]
        sems = rest[2 * n + 6:]
        step = pl.program_id(0)

        @pl.when(step == 0)
        def _():
            local, remote = _gather_first_copies(x_refs, out_refs, *sems)
            for cp in local + remote:
                cp.start()
            for b in range(_SSM_BLOCKS):
                for cp in _seg_copies(z_ref, _U_COL + LANES * b, ubuf, b, lsem, False):
                    cp.start()

        for cp in _seg_copies(z_ref, _U_COL, ubuf, step, lsem, False):
            cp.wait()
        ub = ubuf[step].reshape(T, LANES)
        xb = xs_ref.at[0]
        xb[...] = _nn(ub.astype(BF16), bb_ref[0])
        _scan(xb, a_ref[0, 0:1, :], a_ref[0, 1:2, :], reverse=False)
        y = _nn(xb[...].astype(BF16), cc_ref[0]) + d_ref[0] * ub
        ybuf[step] = y.reshape(SEG, NSEG, LANES)
        for cp in _seg_copies(y_ref, pl.multiple_of(step * LANES, LANES), ybuf, step, ssem, True):
            cp.start()

        @pl.when(step == last)
        def _():
            for b in range(_SSM_BLOCKS):
                for cp in _seg_copies(y_ref, LANES * b, ybuf, b, ssem, True):
                    cp.wait()
            local, remote = _gather_first_copies(x_refs, out_refs, *sems)
            for cp in remote + local:
                cp.wait()

    blk = lambda j: (j, 0, 0)
    anyspec = pl.BlockSpec(memory_space=pl.ANY)
    res = pl.pallas_call(
        body, name="ssm_fwd", grid=(_SSM_BLOCKS,),
        in_specs=[anyspec, pl.BlockSpec((1, LANES, 1024), blk), pl.BlockSpec((1, 2, 512), blk),
                  pl.BlockSpec((1, 1024, LANES), blk), pl.BlockSpec((1, 1, LANES), blk)] + [anyspec] * n,
        out_specs=[anyspec, pl.BlockSpec((1, T, 1024), blk)] + [anyspec] * n,
        out_shape=[_sds((T, SW)), _sds((_SSM_BLOCKS, T, 1024))] + [_sds((NDEV, *s.shape), s.dtype) for s in shards],
        scratch_shapes=_seg_scratch() + _seg_scratch() + _gather_first_scratch(n),
        compiler_params=_cp(("arbitrary",)),
    )(z, bb, ab, cc, dsk, *shards)
    return res[0], res[1], res[2:]


def _ssm_bwd(z, dy, xs, bb, ab, cc, dsk, arrs, parts):
    n, m = len(arrs), len(parts)
    last = _SSM_BLOCKS - 1

    def body(z_ref, dy_ref, x_ref, bb_ref, a_ref, cc_ref, d_ref, *rest):
        a_refs, p_refs, rest = rest[:n], rest[n:n + m], rest[n + m:]
        du_ref, dbb_ref, da_ref, dcc_ref, dd_ref = rest[:5]
        fs_refs, fc_refs, gb = rest[5:5 + n], rest[5 + n:5 + n + m], rest[5 + n + m]
        ubuf, usem, dybuf, dysem, dubuf, dusem = rest[6 + n + m:12 + n + m]
        sems = rest[12 + n + m:]

        def carried():
            return _sibling_swap_copies(a_refs, fs_refs, *sems[:2]) + _chip_swap_copies(p_refs, fc_refs, *sems[2:])

        step = pl.program_id(0)

        @pl.when(step == 0)
        def _():
            for cp in carried():
                cp.start()
            for b in range(_SSM_BLOCKS):
                for cp in (_seg_copies(z_ref, _U_COL + LANES * b, ubuf, b, usem, False)
                           + _seg_copies(dy_ref, LANES * b, dybuf, b, dysem, False)):
                    cp.start()

        for cp in _seg_copies(z_ref, _U_COL, ubuf, step, usem, False) + _seg_copies(dy_ref, 0, dybuf, step, dysem, False):
            cp.wait()
        ub, dyv = ubuf[step].reshape(T, LANES), dybuf[step].reshape(T, LANES)
        ubb, dyb = ub.astype(BF16), dyv.astype(BF16)
        a_re, a_im = a_ref[0, 0:1, :], a_ref[0, 1:2, :]
        dcc_ref[0] = _tn(x_ref[0].astype(BF16), dyb)
        gb[...] = _nt(dyb, cc_ref[0])
        da_re, da_im = _scan(gb, a_re, -a_im, reverse=True, other=x_ref.at[0])
        da_ref[0] = jnp.concatenate([da_re, da_im], axis=0)
        gc = gb[...].astype(BF16)
        du = _nt(gc, bb_ref[0]) + d_ref[0] * dyv
        dubuf[step] = du.reshape(SEG, NSEG, LANES)
        for cp in _seg_copies(du_ref, pl.multiple_of(step * LANES, LANES), dubuf, step, dusem, True):
            cp.start()
        dbb_ref[0] = _tn(ubb, gc)
        dd_ref[0] = jnp.sum(dyv * ub, axis=0, keepdims=True)

        @pl.when(step == last)
        def _():
            for b in range(_SSM_BLOCKS):
                for cp in _seg_copies(du_ref, LANES * b, dubuf, b, dusem, True):
                    cp.wait()
            for cp in carried():
                cp.wait()

    blk = lambda j: (j, 0, 0)
    anyspec = pl.BlockSpec(memory_space=pl.ANY)
    res = pl.pallas_call(
        body, name="ssm_bwd", grid=(_SSM_BLOCKS,),
        in_specs=[anyspec, anyspec, pl.BlockSpec((1, T, 1024), blk), pl.BlockSpec((1, LANES, 1024), blk),
                  pl.BlockSpec((1, 2, 512), blk), pl.BlockSpec((1, 1024, LANES), blk), pl.BlockSpec((1, 1, LANES), blk)]
        + [anyspec] * (n + m),
        out_specs=[anyspec, pl.BlockSpec((1, LANES, 1024), blk), pl.BlockSpec((1, 2, 512), blk),
                   pl.BlockSpec((1, 1024, LANES), blk), pl.BlockSpec((1, 1, LANES), blk)] + [anyspec] * (n + m),
        out_shape=[_sds((T, SW)), _sds((4, LANES, 1024)), _sds((4, 2, 512)), _sds((4, 1024, LANES)), _sds((4, 1, LANES))]
        + [_sds((4, *a.shape[2:]), a.dtype) for a in arrs] + [_sds((3, *p.shape[1:]), p.dtype) for p in parts],
        scratch_shapes=[pltpu.VMEM((T, 1024), F32)] + _seg_scratch() + _seg_scratch() + _seg_scratch()
        + [pltpu.SemaphoreType.DMA((4 * n,)), pltpu.SemaphoreType.DMA((4 * n,)),
           pltpu.SemaphoreType.DMA((3 * m,)), pltpu.SemaphoreType.DMA((3 * m,))],
        compiler_params=_cp(("arbitrary",)),
    )(z, dy, xs, bb, ab, cc, dsk, *arrs, *parts)
    return (*res[:5], res[5:5 + n], res[5 + n:])


_TM_MIX = 512


def _mix_parts(att, y, wg, bg):
    y2 = jax.nn.gelu(y)
    y2b = y2.astype(BF16)
    sg = jax.nn.sigmoid(_nn(y2b, wg) + bg)
    ssm = y2 * sg
    ra, rs = _rms_r(att), _rms_r(ssm)
    return y2, y2b, sg, ssm, ra, rs


def _mix_fwd(att, y, x, wg, bg, ga, gs, wo, bufs):
    m = len(bufs)

    def body(att_ref, y_ref, x_ref, wg_ref, bg_ref, ga_ref, gs_ref, wo_ref, *rest):
        x1_ref, buf_refs, sems = rest[m], rest[m + 1:2 * m + 1], rest[2 * m + 1:]

        @pl.when(pl.program_id(0) == 0)
        def _():
            for cp in _pass_on_copies(buf_refs, *sems):
                cp.start()

        attv = att_ref[...]
        _, _, _, ssm, ra, rs = _mix_parts(attv, y_ref[...], wg_ref[...], bg_ref[...])
        ma = (attv * ra * ga_ref[...]).astype(BF16)
        ms = (ssm * rs * gs_ref[...]).astype(BF16)
        x1_ref[...] = x_ref[...] + _nn(ma, wo_ref[0:AW, :]) + _nn(ms, wo_ref[AW:D, :])

        @pl.when(pl.program_id(0) == T // _TM_MIX - 1)
        def _():
            for cp in _pass_on_copies(buf_refs, *sems):
                cp.wait()

    row = lambda i: (i, 0)
    fixed = lambda i: (0, 0)
    half = pl.BlockSpec((_TM_MIX, AW), row)
    vec = pl.BlockSpec((1, AW), fixed)
    anyspec = pl.BlockSpec(memory_space=pl.ANY)
    res = pl.pallas_call(
        body, name="mix_fwd", grid=(T // _TM_MIX,),
        in_specs=[half, half, pl.BlockSpec((_TM_MIX, D), row), pl.BlockSpec((SW, SW), fixed), vec, vec, vec,
                  pl.BlockSpec((D, D), fixed)] + [anyspec] * m,
        out_specs=[pl.BlockSpec((_TM_MIX, D), row)] + [anyspec] * m,
        out_shape=[_sds((T, D))] + [_sds(b.shape, b.dtype) for b in bufs],
        input_output_aliases={8 + i: 1 + i for i in range(m)},
        scratch_shapes=[pltpu.SemaphoreType.DMA((3 * m,)), pltpu.SemaphoreType.DMA((3 * m,))],
        compiler_params=_cp(("arbitrary",)),
    )(att, y, x, wg, bg, ga, gs, wo, *bufs)
    return res[0], res[1:]


def _mix_bwd(att, y, dx1, wg, bg, ga, gs, wo):
    last = T // _TM_MIX - 1

    def body(att_ref, y_ref, d_ref, wg_ref, bg_ref, ga_ref, gs_ref, wo_ref,
             datt_ref, dy_ref, dwg_ref, dbg_ref, dga_ref, dgs_ref, dwo_ref, dwg_acc, dwo_acc):
        attv, yv, db = att_ref[...], y_ref[...], d_ref[...].astype(BF16)
        y2, y2b, sg, ssm, ra, rs = _mix_parts(attv, yv, wg_ref[...], bg_ref[...])
        na, ns = attv * ra, ssm * rs
        ma = (na * ga_ref[...]).astype(BF16)
        ms = (ns * gs_ref[...]).astype(BF16)
        dma = _nt(db, wo_ref[0:AW, :])
        dms = _nt(db, wo_ref[AW:D, :])
        datt, dga = _rms_bwd(na, ra, ga_ref[...], dma)
        dssm, dgs = _rms_bwd(ns, rs, gs_ref[...], dms)
        dgl = dssm * y2 * sg * (1.0 - sg)
        dglb = dgl.astype(BF16)
        dy2 = dssm * sg + _nt(dglb, wg_ref[...])
        _, gelu_vjp = jax.vjp(jax.nn.gelu, yv)
        datt_ref[...] = datt
        dy_ref[...] = gelu_vjp(dy2)[0]

        @pl.when(pl.program_id(0) == 0)
        def _():
            dwg_acc[...] = jnp.zeros_like(dwg_acc)
            dwo_acc[...] = jnp.zeros_like(dwo_acc)
            dbg_ref[...] = jnp.zeros_like(dbg_ref)
            dga_ref[...] = jnp.zeros_like(dga_ref)
            dgs_ref[...] = jnp.zeros_like(dgs_ref)
        dwg_acc[...] += _tn(y2b, dglb)
        dbg_ref[...] += jnp.sum(dgl, axis=0, keepdims=True)
        dga_ref[...] += dga
        dgs_ref[...] += dgs
        dwo_acc[0:AW, :] += _tn(ma, db)
        dwo_acc[AW:D, :] += _tn(ms, db)

        @pl.when(pl.program_id(0) == last)
        def _():
            dwg_ref[...] = dwg_acc[...].astype(BF16)
            dwo_ref[...] = dwo_acc[...].astype(BF16)

    row = lambda i: (i, 0)
    fixed = lambda i: (0, 0)
    half = pl.BlockSpec((_TM_MIX, AW), row)
    vec = pl.BlockSpec((1, AW), fixed)
    sq = pl.BlockSpec((SW, SW), fixed)
    big = pl.BlockSpec((D, D), fixed)
    return pl.pallas_call(
        body, name="mix_bwd", grid=(T // _TM_MIX,),
        in_specs=[half, half, pl.BlockSpec((_TM_MIX, D), row), sq, vec, vec, vec, big],
        out_specs=[half, half, sq, vec, vec, vec, big],
        out_shape=[_sds((T, AW)), _sds((T, SW)), _sds((SW, SW), BF16), _sds((1, SW)), _sds((1, AW)), _sds((1, SW)),
                   _sds((D, D), BF16)],
        scratch_shapes=[pltpu.VMEM((SW, SW), F32), pltpu.VMEM((D, D), F32)],
        compiler_params=_cp(("arbitrary",)),
    )(att, y, dx1, wg, bg, ga, gs, wo)


_NCB = -(-CS // LANES)


def _ffn_up(x1, g, wu4):
    tm = 1024

    def body(x_ref, g_ref, w_ref, h_ref, up_ref):
        @pl.when(pl.program_id(1) == 0)
        def _():
            xv = x_ref[...]
            h_ref[...] = (xv * _rms_r(xv) * g_ref[...]).astype(BF16)
        up_ref[0, 0] = _nt(h_ref[...], w_ref[0, 0])

    return pl.pallas_call(
        body, name="ffn_up", grid=(T // tm, NDEV),
        in_specs=[pl.BlockSpec((tm, D), lambda i, e: (i, 0)), pl.BlockSpec((1, D), lambda i, e: (0, 0)),
                  pl.BlockSpec((1, 1, CS, D), lambda i, e: (e // 4, e % 4, 0, 0))],
        out_specs=[pl.BlockSpec((tm, D), lambda i, e: (i, 0)),
                   pl.BlockSpec((1, 1, tm, CS), lambda i, e: (e // 4, e % 4, i, 0))],
        out_shape=[_sds((T, D), BF16), _sds((2, 4, T, CS))],
        compiler_params=_cp(("parallel", "arbitrary")),
    )(x1, g, wu4)


def _shift_down(h, k):
    row = lax.broadcasted_iota(jnp.int32, h.shape, 0)
    return jnp.where(row >= k, pltpu.roll(h, k, 0), 0.0)


def _shift_up(h, k):
    row = lax.broadcasted_iota(jnp.int32, h.shape, 0)
    return jnp.where(row < T - k, pltpu.roll(h, T - k, 0), 0.0)


def _conv(h, w, b):
    return w[0:1, :] * _shift_down(h, 2) + w[1:2, :] * _shift_down(h, 1) + w[2:3, :] * h + b


def _chan(gv, rows):
    return pl.BlockSpec((1, 1, rows, LANES), lambda d, j: (gv, d, 0, j))


def _ffn_act(up4, cw4, cb4):
    def body(ug_ref, uv_ref, wg_ref, wv_ref, bg_ref, bv_ref, act_ref):
        g = _conv(ug_ref[0, 0], wg_ref[0, 0], bg_ref[0, 0])
        v = _conv(uv_ref[0, 0], wv_ref[0, 0], bv_ref[0, 0])
        act_ref[0] = (g * jax.nn.sigmoid(g) * v).astype(BF16)

    return pl.pallas_call(
        body, name="ffn_act", grid=(4, _NCB),
        in_specs=[_chan(0, T), _chan(1, T), _chan(0, 3), _chan(1, 3), _chan(0, 1), _chan(1, 1)],
        out_specs=pl.BlockSpec((1, T, LANES), lambda d, j: (d, 0, j)), out_shape=_sds((4, T, CS), BF16),
        compiler_params=_cp(("parallel", "parallel")),
    )(up4, up4, cw4, cw4, cb4, cb4)


def _ffn_act_bwd(up4, dact, cw4, cb4):
    def conv_bwd(h, w, d):
        dup = w[2:3, :] * d + w[1:2, :] * _shift_up(d, 1) + w[0:1, :] * _shift_up(d, 2)
        dw = jnp.concatenate([jnp.sum(d * _shift_down(h, 2), axis=0, keepdims=True),
                              jnp.sum(d * _shift_down(h, 1), axis=0, keepdims=True),
                              jnp.sum(d * h, axis=0, keepdims=True)], axis=0)
        return dup, dw, jnp.sum(d, axis=0, keepdims=True)

    def body(ug_ref, uv_ref, da_ref, wg_ref, wv_ref, bg_ref, bv_ref, dup_ref, dw_ref, db_ref):
        ug, uv, da = ug_ref[0, 0], uv_ref[0, 0], da_ref[0]
        g = _conv(ug, wg_ref[0, 0], bg_ref[0, 0])
        v = _conv(uv, wv_ref[0, 0], bv_ref[0, 0])
        sg = jax.nn.sigmoid(g)
        dg = da * v * (sg * (1.0 + g * (1.0 - sg)))
        dv = da * (g * sg)
        dug, dwg, dbg = conv_bwd(ug, wg_ref[0, 0], dg)
        duv, dwv, dbv = conv_bwd(uv, wv_ref[0, 0], dv)
        dup_ref[0, 0] = dug.astype(BF16)
        dup_ref[1, 0] = duv.astype(BF16)
        dw_ref[0, 0] = dwg
        dw_ref[1, 0] = dwv
        db_ref[0, 0] = dbg
        db_ref[1, 0] = dbv

    both = lambda rows: pl.BlockSpec((2, 1, rows, LANES), lambda d, j: (0, d, 0, j))
    return pl.pallas_call(
        body, name="ffn_act_bwd", grid=(4, _NCB),
        in_specs=[_chan(0, T), _chan(1, T), pl.BlockSpec((1, T, LANES), lambda d, j: (d, 0, j)),
                  _chan(0, 3), _chan(1, 3), _chan(0, 1), _chan(1, 1)],
        out_specs=[both(T), both(3), both(1)],
        out_shape=[_sds((2, 4, T, CS), BF16), _sds((2, 4, 3, CS)), _sds((2, 4, 1, CS))],
        compiler_params=_cp(("parallel", "parallel")),
    )(up4, up4, dact, cw4, cw4, cb4, cb4)


def _ffn_down_loss(act, wd4, x1, target):
    tm = 512

    def body(a_ref, w_ref, x1_ref, t_ref, loss_ref, dx_ref, dxb_ref):
        x2 = x1_ref[...]
        for d in range(4):
            x2 = x2 + _nn(a_ref[d], w_ref[d])
        err = x2 - t_ref[...]
        dx = err * (1.0 / D)
        dx_ref[...] = dx
        dxb_ref[...] = dx.astype(BF16)

        @pl.when(pl.program_id(0) == 0)
        def _():
            loss_ref[...] = jnp.zeros_like(loss_ref)
        loss_ref[...] += 0.5 * jnp.sum(jnp.mean(err * err, axis=-1, keepdims=True), axis=0, keepdims=True)

    row = lambda i: (i, 0)
    fixed = lambda i: (0, 0)
    return pl.pallas_call(
        body, name="ffn_down_loss", grid=(T // tm,),
        in_specs=[pl.BlockSpec((4, tm, CS), lambda i: (0, i, 0)), pl.BlockSpec((4, CS, D), lambda i: (0, 0, 0)),
                  pl.BlockSpec((tm, D), row), pl.BlockSpec((tm, D), row)],
        out_specs=[pl.BlockSpec((1, 1), fixed), pl.BlockSpec((tm, D), row), pl.BlockSpec((tm, D), row)],
        out_shape=[_sds((1, 1)), _sds((T, D)), _sds((T, D), BF16)],
        compiler_params=_cp(("arbitrary",)),
    )(act, wd4, x1, target)


def _ffn_dact(dx2b, wd4):
    tm = 1024

    def body(d_ref, w_ref, o_ref):
        o_ref[0] = _nt(d_ref[...], w_ref[0])

    return pl.pallas_call(
        body, name="ffn_dact", grid=(4, T // tm),
        in_specs=[pl.BlockSpec((tm, D), lambda d, i: (i, 0)), pl.BlockSpec((1, CS, D), lambda d, i: (d, 0, 0))],
        out_specs=pl.BlockSpec((1, tm, CS), lambda d, i: (d, i, 0)), out_shape=_sds((4, T, CS)),
        compiler_params=_cp(("parallel", "parallel")),
    )(dx2b, wd4)


def _ffn_dwd(act, dx2b):
    def body(a_ref, d_ref, o_ref):
        o_ref[0] = _tn(a_ref[0], d_ref[...]).astype(BF16)

    return pl.pallas_call(
        body, name="ffn_dwd", grid=(4,),
        in_specs=[pl.BlockSpec((1, T, CS), lambda d: (d, 0, 0)), pl.BlockSpec((T, D), lambda d: (0, 0))],
        out_specs=pl.BlockSpec((1, CS, D), lambda d: (d, 0, 0)), out_shape=_sds((4, CS, D), BF16),
        compiler_params=_cp(("parallel",)),
    )(act, dx2b)


def _ffn_dwu(h2, dup4):
    tn = 1024

    def body(h_ref, d_ref, o_ref):
        o_ref[0, 0] = _tn(d_ref[0, 0], h_ref[...]).astype(BF16)

    return pl.pallas_call(
        body, name="ffn_dwu", grid=(NDEV, D // tn),
        in_specs=[pl.BlockSpec((T, tn), lambda e, i: (0, i)),
                  pl.BlockSpec((1, 1, T, CS), lambda e, i: (e // 4, e % 4, 0, 0))],
        out_specs=pl.BlockSpec((1, 1, CS, tn), lambda e, i: (e // 4, e % 4, 0, i)),
        out_shape=_sds((2, 4, CS, D), BF16),
        compiler_params=_cp(("parallel", "parallel")),
    )(h2, dup4)


def _ffn_up_bwd(dup4, wu4, x1, g, dres, arrs):
    tm = 256
    n = len(arrs)

    def body(d_ref, w_ref, x_ref, g_ref, dres_ref, *rest):
        a_refs, (dx_ref, dg_ref), fs_refs, sems = rest[:n], rest[n:n + 2], rest[n + 2:2 * n + 2], rest[2 * n + 2:]

        @pl.when(pl.program_id(0) == 0)
        def _():
            for cp in _sibling_swap_copies(a_refs, fs_refs, *sems):
                cp.start()

        dh = jnp.zeros((tm, D), F32)
        for gv in range(2):
            for d in range(4):
                dh = dh + _nn(d_ref[gv, d], w_ref[gv, d])
        xv = x_ref[...]
        r = _rms_r(xv)
        dxr, dgp = _rms_bwd(xv * r, r, g_ref[...], dh)
        dx_ref[...] = dres_ref[...] + dxr

        @pl.when(pl.program_id(0) == 0)
        def _():
            dg_ref[...] = jnp.zeros_like(dg_ref)
        dg_ref[...] += dgp

        @pl.when(pl.program_id(0) == T // tm - 1)
        def _():
            for cp in _sibling_swap_copies(a_refs, fs_refs, *sems):
                cp.wait()

    row = lambda i: (i, 0)
    fixed = lambda i: (0, 0)
    anyspec = pl.BlockSpec(memory_space=pl.ANY)
    res = pl.pallas_call(
        body, name="ffn_up_bwd", grid=(T // tm,),
        in_specs=[pl.BlockSpec((2, 4, tm, CS), lambda i: (0, 0, i, 0)), pl.BlockSpec((2, 4, CS, D), lambda i: (0, 0, 0, 0)),
                  pl.BlockSpec((tm, D), row), pl.BlockSpec((1, D), fixed), pl.BlockSpec((tm, D), row)] + [anyspec] * n,
        out_specs=[pl.BlockSpec((tm, D), row), pl.BlockSpec((1, D), fixed)] + [anyspec] * n,
        out_shape=[_sds((T, D)), _sds((1, D))] + [_sds((4, *a.shape[2:]), a.dtype) for a in arrs],
        scratch_shapes=[pltpu.SemaphoreType.DMA((4 * n,)), pltpu.SemaphoreType.DMA((4 * n,))],
        compiler_params=_cp(("arbitrary",)),
    )(dup4, wu4, x1, g, dres, *arrs)
    return res[0], res[1], res[2:]


def _adamw_math(w, g, m, v):
    m = ADAM_B1 * m + (1.0 - ADAM_B1) * g
    v = ADAM_B2 * v + (1.0 - ADAM_B2) * jnp.square(g)
    m_hat = m / (1.0 - ADAM_B1 ** ADAM_STEP)
    v_hat = v / (1.0 - ADAM_B2 ** ADAM_STEP)
    delta = -ADAM_LR * (m_hat / (jnp.sqrt(v_hat) + ADAM_EPS) + ADAM_WD * w)
    return delta, m, v


def _where_am_i():
    x, y, c = _place()
    return jnp.stack([c, 2 * x + y]).astype(jnp.int32)


def _pair_sum(a4, recv, name):
    _, _, rows, cols = a4.shape
    tr = _row_tile(rows, cols, 3 << 20)

    def body(sel_ref, own_ref, recv_ref, out_ref):
        out_ref[...] = (own_ref[0].astype(F32) + recv_ref[...].astype(F32)).astype(out_ref.dtype)

    grid_spec = pltpu.PrefetchScalarGridSpec(
        num_scalar_prefetch=1, grid=(4, rows // tr),
        in_specs=[pl.BlockSpec((1, 1, tr, cols), lambda k, i, s: (k, s[0], i, 0)),
                  pl.BlockSpec((1, tr, cols), lambda k, i, s: (k, i, 0))],
        out_specs=pl.BlockSpec((1, tr, cols), lambda k, i, s: (k, i, 0)),
    )
    return pl.pallas_call(
        body, name=name, grid_spec=grid_spec, out_shape=_sds((4, rows, cols), a4.dtype),
        compiler_params=_cp(("arbitrary", "arbitrary")),
    )(_where_am_i(), a4, recv)


def _adamw_rs(a4, recv, from_chips, w, m, v, name):
    rows, cols = w.shape
    tr = _row_tile(rows, cols, 3 << 19)

    def body(sel_ref, own_ref, recv_ref, fc_ref, w_ref, m_ref, v_ref, g_ref, d_ref, nm_ref, nv_ref):
        g = own_ref[0, 0].astype(F32) + recv_ref[0].astype(F32)
        for j in range(3):
            g = g + fc_ref[j].astype(F32)
        d, nm, nv = _adamw_math(w_ref[...], g, m_ref[...], v_ref[...])
        g_ref[...] = g
        d_ref[...] = d
        nm_ref[...] = nm
        nv_ref[...] = nv

    flat = pl.BlockSpec((tr, cols), lambda i, s: (i, 0))
    grid_spec = pltpu.PrefetchScalarGridSpec(
        num_scalar_prefetch=1, grid=(rows // tr,),
        in_specs=[pl.BlockSpec((1, 1, tr, cols), lambda i, s: (s[1], s[0], i, 0)),
                  pl.BlockSpec((1, tr, cols), lambda i, s: (s[1], i, 0)),
                  pl.BlockSpec((3, tr, cols), lambda i, s: (0, i, 0)), flat, flat, flat],
        out_specs=[flat] * 4,
    )
    return pl.pallas_call(
        body, name=name, grid_spec=grid_spec, out_shape=[_sds((rows, cols))] * 4,
        compiler_params=_cp(("arbitrary",)),
    )(_where_am_i(), a4, recv, from_chips, w, m, v)


_SMALL_F32 = (("g_mix", (8, 128)), ("b_f", (1, 8)), ("g_q", (1, 64)), ("g_k", (1, 64)), ("lambda_re", (32, 64)),
              ("lambda_im", (32, 64)), ("log_step", (1, 32)), ("d_skip", (32, 16)), ("b_glu", (4, 128)),
              ("g_attn_out", (4, 128)), ("g_ssm_out", (4, 128)), ("g_ffn", (8, 128)), ("conv_b", (44, 128)))
_SMALL_PAIRS = (("b_re", "b_im"), ("c_re", "c_im"))
_PAIR_SHAPE = (NG * GS, NP)


def _ceil8(r):
    return -(-r // 8) * 8


def _small_names():
    return [n for n, _ in _SMALL_F32] + [n for pair in _SMALL_PAIRS for n in pair]


def _pack_small_grads(g, loss_dev):
    parts = [jnp.pad(g[n].reshape(r, c), ((0, _ceil8(r) - r), (0, LANES - c))) for n, (r, c) in _SMALL_F32]
    parts.append(jnp.pad(loss_dev, ((0, 7), (0, LANES - 1))))
    pairs = [jnp.concatenate([g[a].reshape(_PAIR_SHAPE), g[b].reshape(_PAIR_SHAPE)], axis=1) for a, b in _SMALL_PAIRS]
    return jnp.concatenate(parts, axis=0), jnp.concatenate(pairs, axis=0).astype(BF16)


def _adamw_small(gall, gall_b, wmv):
    names = _small_names()
    shapes = [s for _, s in _SMALL_F32] + [_PAIR_SHAPE] * (2 * len(_SMALL_PAIRS))
    npar = len(names)

    def body(ga_ref, gb_ref, *rest):
        ins, loss_ref, outs = rest[:3 * npar], rest[3 * npar], rest[3 * npar + 1:]
        ga, gb = ga_ref[0], gb_ref[0].astype(F32)
        for dev in range(1, NDEV):
            ga = ga + ga_ref[dev]
            gb = gb + gb_ref[dev].astype(F32)
        grads, off = [], 0
        for _, (r, c) in _SMALL_F32:
            grads.append(ga[off:off + r, 0:c])
            off += _ceil8(r)
        loss_ref[...] = ga[off:off + 8]
        rows, cols = _PAIR_SHAPE
        for j in range(len(_SMALL_PAIRS)):
            for h in range(2):
                grads.append(gb[rows * j:rows * (j + 1), cols * h:cols * (h + 1)])
        for i, g in enumerate(grads):
            d, nm, nv = _adamw_math(ins[3 * i][...], g, ins[3 * i + 1][...], ins[3 * i + 2][...])
            outs[4 * i][...] = g
            outs[4 * i + 1][...] = d
            outs[4 * i + 2][...] = nm
            outs[4 * i + 3][...] = nv

    flat = [a for n in names for a in wmv[n]]
    out_shape = [_sds((8, LANES))] + [_sds(s) for s in shapes for _ in range(4)]
    res = pl.pallas_call(body, name="adamw_small", out_shape=out_shape, compiler_params=_cp())(gall, gall_b, *flat)
    return res[0], {n: res[1 + 4 * i:5 + 4 * i] for i, n in enumerate(names)}


def _block_diag(m):
    eye = jnp.eye(8, dtype=m.dtype)
    r, c = m.shape[2], m.shape[3]
    return (m[:, :, :, None, :] * eye[None, :, None, :, None]).reshape(4, 8 * r, 8 * c)


def _diag_blocks(dense, r, c):
    eye = jnp.eye(8, dtype=dense.dtype)
    return jnp.sum(dense.reshape(4, 8, r, 8, c) * eye[None, :, None, :, None], axis=3)


def kernel(x, g_mix, w_in, b_f, g_q, g_k, lambda_re, lambda_im, log_step, b_re, b_im, c_re, c_im, d_skip, w_glu, b_glu, g_attn_out, g_ssm_out, w_out, g_ffn, w_up, conv_w, conv_b, w_down, loss_target, m_g_mix, m_w_in, m_b_f, m_g_q, m_g_k, m_lambda_re, m_lambda_im, m_log_step, m_b_re, m_b_im, m_c_re, m_c_im, m_d_skip, m_w_glu, m_b_glu, m_g_attn_out, m_g_ssm_out, m_w_out, m_g_ffn, m_w_up, m_conv_w, m_conv_b, m_w_down, v_g_mix, v_w_in, v_b_f, v_g_q, v_g_k, v_lambda_re, v_lambda_im, v_log_step, v_b_re, v_b_im, v_c_re, v_c_im, v_d_skip, v_w_glu, v_b_glu, v_g_attn_out, v_g_ssm_out, v_w_out, v_g_ffn, v_w_up, v_conv_w, v_conv_b, v_w_down):
    weights = dict(g_mix=g_mix, w_in=w_in, b_f=b_f, g_q=g_q, g_k=g_k, lambda_re=lambda_re, lambda_im=lambda_im,
                   log_step=log_step, b_re=b_re, b_im=b_im, c_re=c_re, c_im=c_im, d_skip=d_skip, w_glu=w_glu,
                   b_glu=b_glu, g_attn_out=g_attn_out, g_ssm_out=g_ssm_out, w_out=w_out, g_ffn=g_ffn, w_up=w_up,
                   conv_w=conv_w, conv_b=conv_b, w_down=w_down)
    mom_m = dict(g_mix=m_g_mix, w_in=m_w_in, b_f=m_b_f, g_q=m_g_q, g_k=m_g_k, lambda_re=m_lambda_re,
                 lambda_im=m_lambda_im, log_step=m_log_step, b_re=m_b_re, b_im=m_b_im, c_re=m_c_re, c_im=m_c_im,
                 d_skip=m_d_skip, w_glu=m_w_glu, b_glu=m_b_glu, g_attn_out=m_g_attn_out, g_ssm_out=m_g_ssm_out,
                 w_out=m_w_out, g_ffn=m_g_ffn, w_up=m_w_up, conv_w=m_conv_w, conv_b=m_conv_b, w_down=m_w_down)
    mom_v = dict(g_mix=v_g_mix, w_in=v_w_in, b_f=v_b_f, g_q=v_g_q, g_k=v_g_k, lambda_re=v_lambda_re,
                 lambda_im=v_lambda_im, log_step=v_log_step, b_re=v_b_re, b_im=v_b_im, c_re=v_c_re, c_im=v_c_im,
                 d_skip=v_d_skip, w_glu=v_w_glu, b_glu=v_b_glu, g_attn_out=v_g_attn_out, g_ssm_out=v_g_ssm_out,
                 w_out=v_w_out, g_ffn=v_g_ffn, w_up=v_w_up, conv_w=v_conv_w, conv_b=v_conv_b, w_down=v_w_down)
    names = list(weights)
    big = ("w_in", "w_glu", "w_out", "w_up", "w_down", "conv_w")

    xs = x[0]
    target = loss_target[0]
    row = lambda a: a.reshape(1, -1)

    (win_g,) = _all_gather([w_in.T.astype(BF16)], "gather_w_in")
    w_in_t = win_g.reshape(8 * 257, D)
    wcat = jnp.concatenate([w_in_t[0:1536], w_in_t[1544:2056], jnp.pad(w_in_t[1536:1544], ((0, LANES - NH), (0, 0)))],
                           axis=0)
    cb4 = conv_b.reshape(2, 4, 1, CS)

    h1, z, f_t, (wg_g, wo_g) = _in_proj(xs, row(g_mix), wcat, [w_glu.astype(BF16), w_out.astype(BF16)])
    bf_col = b_f.reshape(NH, 1)
    ck = _gates_fwd(f_t, bf_col).reshape(_HP, 2, T)
    att, lse, (wu_g, cw_g), (wg_g, wo_g) = _attn_fwd(z, ck, row(g_q), row(g_k), [w_up.T.astype(BF16), conv_w],
                                                       [wg_g, wo_g])

    ls_col = log_step.reshape(NG, 1)
    br_t, bi_t = b_re.transpose(0, 2, 1), b_im.transpose(0, 2, 1)
    ab_re, ab_im, bb_re, bb_im = _disc_fwd(lambda_re, lambda_im, ls_col, br_t, bi_t)
    bb = jnp.concatenate([_block_diag(bb_re.reshape(4, 8, GS, NP)), _block_diag(bb_im.reshape(4, 8, GS, NP))],
                         axis=2).astype(BF16)
    ab = jnp.stack([ab_re.reshape(4, 512), ab_im.reshape(4, 512)], axis=1)
    cdiag = lambda cm: _block_diag(cm.reshape(4, 8, GS, NP).transpose(0, 1, 3, 2))
    cc = jnp.concatenate([cdiag(c_re), -cdiag(c_im)], axis=1).astype(BF16)
    dsk = d_skip.reshape(4, 1, LANES)
    y, xs_il, (wd_g,) = _ssm_fwd(z, bb, ab, cc, dsk, [w_down.astype(BF16)])
    wg_f = wg_g.reshape(SW, SW)
    wo_f = wo_g.reshape(D, D)
    x1, (wu_g, cw_g, wd_g) = _mix_fwd(att, y, xs, wg_f, row(b_glu), row(g_attn_out), row(g_ssm_out), wo_f,
                                      [wu_g, cw_g, wd_g])
    wu4 = wu_g.reshape(2, 4, CS, D)
    wd4 = wd_g.reshape(4, CS, D)
    cw4 = cw_g.reshape(2, 4, 3, CS)
    h2, up4 = _ffn_up(x1, row(g_ffn), wu4)
    act = _ffn_act(up4, cw4, cb4)
    loss_dev, dx2, dx2b = _ffn_down_loss(act, wd4, x1, target)

    dact = _ffn_dact(dx2b, wd4)
    d_wd = _ffn_dwd(act, dx2b)
    dup4, dcw4, dcb4 = _ffn_act_bwd(up4, dact, cw4, cb4)
    d_wu = _ffn_dwu(h2, dup4)
    mlp, mixer = ("w_up", "w_down", "conv_w"), ("w_out", "w_glu")
    early = mlp + mixer
    by_dest = {"w_up": d_wu.reshape(4, 2, CS, D), "w_down": d_wd.reshape(4, 2, DFF // NDEV, D),
               "conv_w": dcw4.reshape(4, 2, 3, CS)}
    dx1, dg_ffn, fs = _ffn_up_bwd(dup4, wu4, x1, row(g_ffn), dx2, [by_dest[n] for n in mlp])
    from_sibling = dict(zip(mlp, fs))
    chip_sums = {n: _pair_sum(by_dest[n], from_sibling[n], "rs_pair_sum_" + n) for n in mlp}
    datt, dy, d_wg, d_bg, d_ga, d_gs, d_wo = _mix_bwd(att, y, dx1, wg_f, row(b_glu), row(g_attn_out),
                                                       row(g_ssm_out), wo_f)
    by_dest.update({"w_out": d_wo.reshape(4, 2, D // NDEV, D), "w_glu": d_wg.reshape(4, 2, SW // NDEV, SW)})
    under_ssm = ("w_down", "conv_w")
    du, dbb, dab, dcc, ddsk, fs, fc = _ssm_bwd(z, dy, xs_il, bb, ab, cc, dsk, [by_dest[n] for n in mixer],
                                               [chip_sums[n] for n in under_ssm])
    from_sibling.update(zip(mixer, fs))
    from_chips = dict(zip(under_ssm, fc))
    chip_sums.update({n: _pair_sum(by_dest[n], from_sibling[n], "rs_pair_sum_" + n) for n in mixer})
    du = du.astype(BF16)
    dbb_re = _diag_blocks(dbb[:, :, 0:512], GS, NP).reshape(NG, GS, NP)
    dbb_im = _diag_blocks(dbb[:, :, 512:1024], GS, NP).reshape(NG, GS, NP)
    dlr, dli, dls, dbr_t, dbi_t = _disc_bwd(lambda_re, lambda_im, ls_col, br_t, bi_t,
                                            dab[:, 0].reshape(NG, NP), dab[:, 1].reshape(NG, NP), dbb_re, dbb_im)
    d_cre = _diag_blocks(dcc[:, 0:512], NP, GS).transpose(0, 1, 3, 2).reshape(NG, GS, NP)
    d_cim = -_diag_blocks(dcc[:, 512:1024], NP, GS).transpose(0, 1, 3, 2).reshape(NG, GS, NP)

    under_attn = ("w_up",) + mixer
    dq, dk, dv, dck, d_gq, d_gk, fc = _attn_bwd(z, ck, row(g_q), row(g_k), lse, datt, [chip_sums[n] for n in under_attn])
    from_chips.update(zip(under_attn, fc))
    df_t, d_bf = _gates_bwd(dck.reshape(NH, T), f_t, bf_col)
    df = jnp.concatenate([df_t.T, jnp.zeros((T, LANES - NH), F32)], axis=1).astype(BF16)
    d_wcat = _in_proj_dw(h1, dq, dk, dv, du, df)
    d_win = jnp.concatenate([d_wcat[0:1536], d_wcat[2048:2048 + NH], d_wcat[1536:2048]], axis=0)
    by_dest["w_in"] = d_win.astype(BF16).reshape(4, 2, 257, D)
    (from_sibling["w_in"],) = _swap_sibling([by_dest["w_in"]], "rs_pair_swap_w_in")
    chip_sums["w_in"] = _pair_sum(by_dest["w_in"], from_sibling["w_in"], "rs_pair_sum_w_in")
    grad_x, dg_mix, (from_chips["w_in"],) = _in_proj_bwd(dq, dk, dv, du, df, wcat, xs, row(g_mix), dx1, [chip_sums["w_in"]])

    grads, deltas, new_m, new_v = {}, {}, {}, {}
    for n in early + ("w_in",):
        flip = (lambda a: a.T) if n in ("w_up", "w_in") else (lambda a: a)
        outs = _adamw_rs(by_dest[n], from_sibling[n], from_chips[n], flip(weights[n]), flip(mom_m[n]), flip(mom_v[n]),
                         "adamw_" + n)
        grads[n], deltas[n], new_m[n], new_v[n] = (flip(o) for o in outs)

    small = dict(g_mix=dg_mix, b_f=d_bf, g_q=d_gq, g_k=d_gk, lambda_re=dlr, lambda_im=dli, log_step=dls,
                 b_re=dbr_t, b_im=dbi_t, c_re=d_cre, c_im=d_cim, d_skip=ddsk, b_glu=d_bg, g_attn_out=d_ga,
                 g_ssm_out=d_gs, g_ffn=dg_ffn, conv_b=dcb4)
    small_all, small_all_b = _all_gather(list(_pack_small_grads(small, loss_dev)), "gather_small_grads")
    views = dict(_SMALL_F32)
    swapped = ("b_re", "b_im")

    def view(n, a):
        return a.transpose(0, 2, 1).reshape(_PAIR_SHAPE) if n in swapped else a.reshape(views.get(n, _PAIR_SHAPE))

    loss_p, small_out = _adamw_small(small_all, small_all_b,
                                     {n: (view(n, weights[n]), view(n, mom_m[n]), view(n, mom_v[n])) for n in small})
    loss = loss_p[0, 0]
    for n in small:
        back = (lambda a: a.reshape(NG, GS, NP).transpose(0, 2, 1)) if n in swapped else (lambda a: a.reshape(weights[n].shape))
        grads[n], deltas[n], new_m[n], new_v[n] = (back(o) for o in small_out[n])

    return (loss, grad_x[None], *[grads[n] for n in names], *[deltas[n] for n in names],
            *[new_m[n] for n in names], *[new_v[n] for n in names])
```

```python
import jax
import jax.numpy as jnp
from jax import lax
from jax.experimental import pallas as pl
from jax.experimental.pallas import tpu as pltpu

F32, BF16 = jnp.float32, jnp.bfloat16
T, D = 2048, 1024
NH, DH = 8, 64
AW, SW = 512, 512
NG, GS, NP = 32, 16, 64
DFF = 2816
NDEV = 8
CS = 2 * DFF // NDEV
EPS = 1e-6
NEG = -1e30
ZC = 4 * 512 + 128
SEG = 256
NSEG = T // SEG
_SCAN_UNROLL = 4
BQ = 512
VMEM_LIMIT = 56 * 1024 * 1024
LANES = 128
MESH = pl.DeviceIdType.MESH
HI = lax.Precision.HIGHEST

ADAM_LR, ADAM_B1, ADAM_B2, ADAM_EPS, ADAM_WD, ADAM_STEP = 0.001, 0.9, 0.999, 1e-08, 0.01, 10


def _cp(dims=None):
    if dims is None:
        return pltpu.CompilerParams(vmem_limit_bytes=VMEM_LIMIT)
    return pltpu.CompilerParams(dimension_semantics=dims, vmem_limit_bytes=VMEM_LIMIT)


def _sds(shape, dtype=F32):
    return jax.ShapeDtypeStruct(shape, dtype)


def _nt(a, b):
    return lax.dot_general(a, b, (((1,), (1,)), ((), ())), preferred_element_type=F32)


def _tn(a, b):
    return lax.dot_general(a, b, (((0,), (0,)), ((), ())), preferred_element_type=F32)


def _nn(a, b):
    return jnp.dot(a, b, preferred_element_type=F32)


def _rms_r(x):
    return lax.rsqrt(jnp.mean(x * x, axis=-1, keepdims=True) + EPS)


def _rms_bwd(n, r, g, dh):
    dn = dh * g
    dx = r * (dn - n * jnp.mean(dn * n, axis=-1, keepdims=True))
    return dx, jnp.sum(dh * n, axis=0, keepdims=True)


def _row_tile(rows, cols, limit):
    tr = rows
    while tr * cols * 4 > limit and tr % 32 == 0:
        tr //= 2
    return tr


def _place():
    return lax.axis_index("x"), lax.axis_index("y"), lax.axis_index("c")


def _all_gather(shards, name):
    n = len(shards)

    def body(*refs):
        x_refs, out_refs = refs[:n], refs[n:2 * n]
        send_sems, recv_sems, local_sems = refs[2 * n:]
        x, y, c = _place()
        me, sibling = (x, y, c), (x, y, 1 - c)
        chips = [(1 - x, y), (x, 1 - y), (1 - x, 1 - y)]

        def copy(a, k, block, to, src=None):
            px, py, pc = block
            slot = out_refs[a].at[4 * px + 2 * py + pc]
            return pltpu.make_async_remote_copy(
                src_ref=slot if src is None else src, dst_ref=slot,
                send_sem=send_sems.at[7 * a + k], recv_sem=recv_sems.at[7 * a + k], device_id=to, device_id_type=MESH)

        mine, first, passed = [], [], []
        for a in range(n):
            cp = pltpu.make_async_copy(x_refs[a], out_refs[a].at[4 * x + 2 * y + c], local_sems.at[a])
            cp.start()
            mine.append(cp)
            cps = [copy(a, 0, me, sibling, src=x_refs[a])]
            cps += [copy(a, 1 + j, me, (*chip, c), src=x_refs[a]) for j, chip in enumerate(chips)]
            for cp in cps:
                cp.start()
            first += cps
        for a in range(n):
            for j, chip in enumerate(chips):
                copy(a, 1 + j, (*chip, c), me).wait_recv()
                fwd = copy(a, 4 + j, (*chip, c), sibling)
                fwd.start()
                passed.append(fwd)
        for a in range(n):
            copy(a, 0, sibling, me).wait_recv()
            for j, chip in enumerate(chips):
                copy(a, 4 + j, (*chip, 1 - c), me).wait_recv()
        for cp in first + passed:
            cp.wait_send()
        for cp in mine:
            cp.wait()

    anyspec = pl.BlockSpec(memory_space=pl.ANY)
    return pl.pallas_call(
        body, name=name,
        out_shape=[_sds((NDEV, *s.shape), s.dtype) for s in shards],
        in_specs=[anyspec] * n, out_specs=[anyspec] * n,
        scratch_shapes=[pltpu.SemaphoreType.DMA((7 * n,)), pltpu.SemaphoreType.DMA((7 * n,)),
                        pltpu.SemaphoreType.DMA((n,))],
    )(*shards)


def _gather_first_copies(x_refs, out_refs, send_sems, recv_sems, local_sems):
    x, y, c = _place()
    slot = 4 * x + 2 * y + c
    peers = [(x, y, 1 - c), (1 - x, y, c), (x, 1 - y, c), (1 - x, 1 - y, c)]
    local, remote = [], []
    for a, (x_ref, out_ref) in enumerate(zip(x_refs, out_refs)):
        local.append(pltpu.make_async_copy(x_ref, out_ref.at[slot], local_sems.at[a]))
        for k, peer in enumerate(peers):
            remote.append(pltpu.make_async_remote_copy(
                src_ref=x_ref, dst_ref=out_ref.at[slot], send_sem=send_sems.at[4 * a + k],
                recv_sem=recv_sems.at[4 * a + k], device_id=peer, device_id_type=MESH))
    return local, remote


def _gather_first_scratch(n):
    return [pltpu.SemaphoreType.DMA((4 * n,)), pltpu.SemaphoreType.DMA((4 * n,)), pltpu.SemaphoreType.DMA((n,))]


def _chip_swap_copies(p_refs, out_refs, send_sems, recv_sems):
    x, y, c = _place()
    chips = [(1 - x, y), (x, 1 - y), (1 - x, 1 - y)]
    cps = []
    for a, (p_ref, out_ref) in enumerate(zip(p_refs, out_refs)):
        for j, (px, py) in enumerate(chips):
            cps.append(pltpu.make_async_remote_copy(
                src_ref=p_ref.at[2 * px + py], dst_ref=out_ref.at[j],
                send_sem=send_sems.at[3 * a + j], recv_sem=recv_sems.at[3 * a + j],
                device_id=(px, py, c), device_id_type=MESH))
    return cps


def _sibling_swap_copies(a_refs, out_refs, send_sems, recv_sems):
    x, y, c = _place()
    cps = []
    for a, (a_ref, out_ref) in enumerate(zip(a_refs, out_refs)):
        for k in range(4):
            cps.append(pltpu.make_async_remote_copy(
                src_ref=a_ref.at[k, 1 - c], dst_ref=out_ref.at[k],
                send_sem=send_sems.at[4 * a + k], recv_sem=recv_sems.at[4 * a + k],
                device_id=(x, y, 1 - c), device_id_type=MESH))
    return cps


def _pass_on_copies(buf_refs, send_sems, recv_sems):
    x, y, c = _place()
    chips = [(1 - x, y), (x, 1 - y), (1 - x, 1 - y)]
    cps = []
    for a, buf in enumerate(buf_refs):
        for j, (px, py) in enumerate(chips):
            block = buf.at[4 * px + 2 * py + c]
            cps.append(pltpu.make_async_remote_copy(
                src_ref=block, dst_ref=block, send_sem=send_sems.at[3 * a + j], recv_sem=recv_sems.at[3 * a + j],
                device_id=(x, y, 1 - c), device_id_type=MESH))
    return cps


def _swap_sibling(arrs, name):
    n = len(arrs)

    def body(*refs):
        cps = _sibling_swap_copies(refs[:n], refs[n:2 * n], *refs[2 * n:])
        for cp in cps:
            cp.start()
        for cp in cps:
            cp.wait()

    anyspec = pl.BlockSpec(memory_space=pl.ANY)
    return pl.pallas_call(
        body, name=name,
        out_shape=[_sds((4, *a.shape[2:]), a.dtype) for a in arrs],
        in_specs=[anyspec] * n, out_specs=[anyspec] * n,
        scratch_shapes=[pltpu.SemaphoreType.DMA((4 * n,)), pltpu.SemaphoreType.DMA((4 * n,))],
    )(*arrs)


def _in_proj(x, g, wcat, shards):
    tm = 256
    n = len(shards)

    def body(x_ref, g_ref, w_ref, *rest):
        x_refs, (h_ref, z_ref, ft_ref), out_refs, sems = rest[:n], rest[n:n + 3], rest[n + 3:2 * n + 3], rest[2 * n + 3:]

        @pl.when(pl.program_id(0) == 0)
        def _():
            local, remote = _gather_first_copies(x_refs, out_refs, *sems)
            for cp in local + remote:
                cp.start()

        xv = x_ref[...]
        h = (xv * _rms_r(xv) * g_ref[...]).astype(BF16)
        h_ref[...] = h
        z = _nt(h, w_ref[...])
        z_ref[...] = z
        ft_ref[...] = z[:, 2048:ZC].T[0:NH, :]

        @pl.when(pl.program_id(0) == T // tm - 1)
        def _():
            local, remote = _gather_first_copies(x_refs, out_refs, *sems)
            for cp in remote + local:
                cp.wait()

    row = lambda i: (i, 0)
    fixed = lambda i: (0, 0)
    anyspec = pl.BlockSpec(memory_space=pl.ANY)
    res = pl.pallas_call(
        body, name="in_proj", grid=(T // tm,),
        in_specs=[pl.BlockSpec((tm, D), row), pl.BlockSpec((1, D), fixed), pl.BlockSpec((ZC, D), fixed)] + [anyspec] * n,
        out_specs=[pl.BlockSpec((tm, D), row), pl.BlockSpec((tm, ZC), row), pl.BlockSpec((NH, tm), lambda i: (0, i))]
        + [anyspec] * n,
        out_shape=[_sds((T, D), BF16), _sds((T, ZC)), _sds((NH, T))] + [_sds((NDEV, *s.shape), s.dtype) for s in shards],
        scratch_shapes=_gather_first_scratch(n),
        compiler_params=_cp(("arbitrary",)),
    )(x, g, wcat, *shards)
    return res[0], res[1], res[2], res[3:]


def _in_proj_dw(h, dq, dk, dv, du, df):
    tm = 512

    def body(h_ref, dq_ref, dk_ref, dv_ref, du_ref, df_ref, o_ref):
        hv = h_ref[...]
        for j, d_ref in enumerate((dq_ref, dk_ref, dv_ref, du_ref)):
            o_ref[512 * j:512 * (j + 1), :] = _tn(d_ref[...], hv)
        o_ref[2048:ZC, :] = _tn(df_ref[...], hv)

    full = lambda w: pl.BlockSpec((T, w), lambda i: (0, 0))
    return pl.pallas_call(
        body, name="in_proj_dw", grid=(D // tm,),
        in_specs=[pl.BlockSpec((T, tm), lambda i: (0, i)), full(512), full(512), full(512), full(512), full(LANES)],
        out_specs=pl.BlockSpec((ZC, tm), lambda i: (0, i)), out_shape=_sds((ZC, D)),
        compiler_params=_cp(("parallel",)),
    )(h, dq, dk, dv, du, df)


def _in_proj_bwd(dq, dk, dv, du, df, wcat, x, g, dres, parts):
    tm = 256
    n = len(parts)

    def body(dq_ref, dk_ref, dv_ref, du_ref, df_ref, w_ref, x_ref, g_ref, dres_ref, *rest):
        p_refs, (dx_ref, dg_ref), fc_refs, sems = rest[:n], rest[n:n + 2], rest[n + 2:2 * n + 2], rest[2 * n + 2:]

        @pl.when(pl.program_id(0) == 0)
        def _():
            for cp in _chip_swap_copies(p_refs, fc_refs, *sems):
                cp.start()

        dh = _nn(df_ref[...], w_ref[2048:ZC, :])
        for j, d_ref in enumerate((dq_ref, dk_ref, dv_ref, du_ref)):
            dh = dh + _nn(d_ref[...], w_ref[512 * j:512 * (j + 1), :])
        xv = x_ref[...]
        r = _rms_r(xv)
        dxr, dgp = _rms_bwd(xv * r, r, g_ref[...], dh)
        dx_ref[...] = dres_ref[...] + dxr

        @pl.when(pl.program_id(0) == 0)
        def _():
            dg_ref[...] = jnp.zeros_like(dg_ref)
        dg_ref[...] += dgp

        @pl.when(pl.program_id(0) == T // tm - 1)
        def _():
            for cp in _chip_swap_copies(p_refs, fc_refs, *sems):
                cp.wait()

    row = lambda i: (i, 0)
    fixed = lambda i: (0, 0)
    part = pl.BlockSpec((tm, 512), row)
    anyspec = pl.BlockSpec(memory_space=pl.ANY)
    res = pl.pallas_call(
        body, name="in_proj_bwd", grid=(T // tm,),
        in_specs=[part, part, part, part, pl.BlockSpec((tm, LANES), row), pl.BlockSpec((ZC, D), fixed),
                  pl.BlockSpec((tm, D), row), pl.BlockSpec((1, D), fixed), pl.BlockSpec((tm, D), row)] + [anyspec] * n,
        out_specs=[pl.BlockSpec((tm, D), row), pl.BlockSpec((1, D), fixed)] + [anyspec] * n,
        out_shape=[_sds((T, D)), _sds((1, D))] + [_sds((3, *p.shape[1:]), p.dtype) for p in parts],
        scratch_shapes=[pltpu.SemaphoreType.DMA((3 * n,)), pltpu.SemaphoreType.DMA((3 * n,))],
        compiler_params=_cp(("arbitrary",)),
    )(dq, dk, dv, du, df, wcat, x, g, dres, *parts)
    return res[0], res[1], res[2:]


def _stack_lanes(v):
    return jnp.concatenate([v[:, LANES * b:LANES * (b + 1)] for b in range(T // LANES)], axis=0)


def _unstack_lanes(s):
    return jnp.concatenate([s[8 * b:8 * b + 8, :] for b in range(T // LANES)], axis=1)


def _cumsum_lanes(v, reverse):
    st = _stack_lanes(v)
    ri = lax.broadcasted_iota(jnp.int32, (LANES, LANES), 0)
    ci = lax.broadcasted_iota(jnp.int32, (LANES, LANES), 1)
    tri = (ri >= ci) if reverse else (ri <= ci)
    intra = jnp.dot(st, tri.astype(F32), precision=HI, preferred_element_type=F32)
    tot = intra[:, 0:1] if reverse else intra[:, LANES - 1:LANES]
    same_head = (ci % 8) == (ri % 8)
    other = (ci // 8 > ri // 8) if reverse else (ci // 8 < ri // 8)
    off = jnp.dot((same_head & other).astype(F32), jnp.broadcast_to(tot, (LANES, LANES)), precision=HI,
                  preferred_element_type=F32)
    return _unstack_lanes(intra + off)


def _gates_fwd(f_t, b_f):
    def body(f_ref, b_ref, c_ref):
        z = f_ref[...] + b_ref[...]
        lf = jnp.minimum(z, 0.0) - jnp.log(1.0 + jnp.exp(-jnp.abs(z)))
        c_ref[...] = _cumsum_lanes(lf, reverse=False)

    return pl.pallas_call(body, name="gates_fwd", out_shape=_sds((NH, T)), compiler_params=_cp())(f_t, b_f)


def _gates_bwd(dck, f_t, b_f):
    def body(d_ref, f_ref, b_ref, df_ref, db_ref):
        z = f_ref[...] + b_ref[...]
        dlf = _cumsum_lanes(d_ref[...], reverse=True)
        df = dlf * jax.nn.sigmoid(-z)
        df_ref[...] = df
        db_ref[...] = jnp.sum(df, axis=1, keepdims=True)

    return pl.pallas_call(body, name="gates_bwd", out_shape=[_sds((NH, T)), _sds((NH, 1))],
                          compiler_params=_cp())(dck, f_t, b_f)


def _lower_triangle():
    rows = lax.broadcasted_iota(jnp.int32, (BQ, BQ), 0)
    cols = lax.broadcasted_iota(jnp.int32, (BQ, BQ), 1)
    return cols <= rows


def _logit_parts(qb, kb, ck_row, lo, hi, tri):
    parts = []
    if lo > 0:
        parts.append((_nt(qb[lo:hi], kb[0:lo]) - ck_row[:, 0:lo], 0, lo))
    parts.append((jnp.where(tri, _nt(qb[lo:hi], kb[lo:hi]) - ck_row[:, lo:hi], NEG), lo, hi))
    return parts


_HP = NH // 2


def _qkv_specs():
    return [pl.BlockSpec((T, LANES), lambda p: (0, p)), pl.BlockSpec((T, LANES), lambda p: (0, _HP + p)),
            pl.BlockSpec((T, LANES), lambda p: (0, 2 * _HP + p))]


def _attn_fwd(z, ck, g_q, g_k, shards, bufs):
    scale = DH ** -0.5
    n, m = len(shards), len(bufs)

    def body(q_ref, k_ref, v_ref, ck_ref, gq_ref, gk_ref, *rest):
        x_refs, (o_ref, lse_ref) = rest[:n], rest[n + m:n + m + 2]
        out_refs, buf_refs, sems = rest[n + m + 2:2 * n + m + 2], rest[2 * n + m + 2:2 * n + 2 * m + 2], rest[2 * n + 2 * m + 2:]

        @pl.when(pl.program_id(0) == 0)
        def _():
            local, remote = _gather_first_copies(x_refs, out_refs, *sems[:3])
            for cp in local + remote + _pass_on_copies(buf_refs, *sems[3:]):
                cp.start()

        qb, kb, vb = [], [], []
        for hh in range(2):
            cols = slice(DH * hh, DH * (hh + 1))
            qv, kv = q_ref[:, cols], k_ref[:, cols]
            qb.append((qv * _rms_r(qv) * gq_ref[...] * scale).astype(BF16))
            kb.append((kv * _rms_r(kv) * gk_ref[...]).astype(BF16))
            vb.append(v_ref[:, cols].astype(BF16))
        tri = _lower_triangle()
        for i in range(T // BQ):
            lo, hi = i * BQ, (i + 1) * BQ
            outs = []
            for hh in range(2):
                parts = _logit_parts(qb[hh], kb[hh], ck_ref[0, hh:hh + 1, :], lo, hi, tri)
                top = jnp.max(parts[-1][0], axis=-1, keepdims=True)
                if lo > 0:
                    top = jnp.maximum(top, jnp.max(parts[0][0], axis=-1, keepdims=True))
                l, o = 0.0, 0.0
                for s, k0, k1 in parts:
                    p = jnp.exp(s - top)
                    l = l + jnp.sum(p, axis=-1, keepdims=True)
                    o = o + _nn(p.astype(BF16), vb[hh][k0:k1])
                outs.append(o / l)
                lse_ref[0, lo:hi, hh:hh + 1] = top + jnp.log(l)
            o_ref[lo:hi, :] = jnp.concatenate(outs, axis=1)

        @pl.when(pl.program_id(0) == _HP - 1)
        def _():
            local, remote = _gather_first_copies(x_refs, out_refs, *sems[:3])
            for cp in remote + local + _pass_on_copies(buf_refs, *sems[3:]):
                cp.wait()

    fixed = lambda p: (0, 0)
    anyspec = pl.BlockSpec(memory_space=pl.ANY)
    res = pl.pallas_call(
        body, name="attn_fwd", grid=(_HP,),
        in_specs=_qkv_specs() + [pl.BlockSpec((1, 2, T), lambda p: (p, 0, 0)), pl.BlockSpec((1, DH), fixed),
                                 pl.BlockSpec((1, DH), fixed)] + [anyspec] * (n + m),
        out_specs=[pl.BlockSpec((T, LANES), lambda p: (0, p)), pl.BlockSpec((1, T, 2), lambda p: (p, 0, 0))]
        + [anyspec] * (n + m),
        out_shape=[_sds((T, AW)), _sds((_HP, T, 2))] + [_sds((NDEV, *s.shape), s.dtype) for s in shards]
        + [_sds(b.shape, b.dtype) for b in bufs],
        input_output_aliases={6 + n + i: 2 + n + i for i in range(m)},
        scratch_shapes=_gather_first_scratch(n) + [pltpu.SemaphoreType.DMA((3 * m,)), pltpu.SemaphoreType.DMA((3 * m,))],
        compiler_params=_cp(("arbitrary",)),
    )(z, z, z, ck, g_q, g_k, *shards, *bufs)
    return res[0], res[1], res[2:2 + n], res[2 + n:]


def _attn_bwd(z, ck, g_q, g_k, lse, datt, parts):
    scale = DH ** -0.5
    n = len(parts)

    def body(q_ref, k_ref, v_ref, ck_ref, gq_ref, gk_ref, lse_ref, do_ref, *rest):
        p_refs, rest = rest[:n], rest[n:]
        dq_ref, dk_ref, dv_ref, dck_ref, dgq_ref, dgk_ref = rest[:6]
        fc_refs, (dkn_acc, dv_acc, dc_acc), sems = rest[6:6 + n], rest[6 + n:9 + n], rest[9 + n:]

        @pl.when(pl.program_id(0) == 0)
        def _():
            for cp in _chip_swap_copies(p_refs, fc_refs, *sems):
                cp.start()

        nq, nk, rq, rk, qb, kb, vb, dob = [], [], [], [], [], [], [], []
        for hh in range(2):
            cols = slice(DH * hh, DH * (hh + 1))
            qv, kv = q_ref[:, cols], k_ref[:, cols]
            rq.append(_rms_r(qv))
            rk.append(_rms_r(kv))
            nq.append(qv * rq[hh])
            nk.append(kv * rk[hh])
            qb.append((nq[hh] * gq_ref[...] * scale).astype(BF16))
            kb.append((nk[hh] * gk_ref[...]).astype(BF16))
            vb.append(v_ref[:, cols].astype(BF16))
            dob.append(do_ref[:, cols].astype(BF16))
        dkn_acc[...] = jnp.zeros_like(dkn_acc)
        dv_acc[...] = jnp.zeros_like(dv_acc)
        dc_acc[...] = jnp.zeros_like(dc_acc)
        dgq = jnp.zeros((1, DH), F32)
        tri = _lower_triangle()
        for i in range(T // BQ):
            lo, hi = i * BQ, (i + 1) * BQ
            dqs = []
            for hh in range(2):
                lse = lse_ref[0, lo:hi, hh:hh + 1]
                pd = []
                for s, k0, k1 in _logit_parts(qb[hh], kb[hh], ck_ref[0, hh:hh + 1, :], lo, hi, tri):
                    pd.append((jnp.exp(s - lse), _nt(dob[hh][lo:hi], vb[hh][k0:k1]), k0, k1))
                mean = sum(jnp.sum(p * dp, axis=-1, keepdims=True) for p, dp, _, _ in pd)
                dqn = 0.0
                for p, dp, k0, k1 in pd:
                    ds = p * (dp - mean)
                    dsb = ds.astype(BF16)
                    dv_acc[hh, k0:k1, :] += _tn(p.astype(BF16), dob[hh][lo:hi])
                    dkn_acc[hh, k0:k1, :] += _tn(dsb, qb[hh][lo:hi])
                    dc_acc[hh:hh + 1, k0:k1] -= jnp.sum(ds, axis=0, keepdims=True)
                    dqn = dqn + _nn(dsb, kb[hh][k0:k1])
                dqx, dgp = _rms_bwd(nq[hh][lo:hi], rq[hh][lo:hi], gq_ref[...], dqn * scale)
                dqs.append(dqx)
                dgq = dgq + dgp
            dq_ref[lo:hi, :] = jnp.concatenate(dqs, axis=1).astype(BF16)
        dks, dgk = [], jnp.zeros((1, DH), F32)
        for hh in range(2):
            dkx, dgp = _rms_bwd(nk[hh], rk[hh], gk_ref[...], dkn_acc[hh])
            dks.append(dkx)
            dgk = dgk + dgp
        dk_ref[...] = jnp.concatenate(dks, axis=1).astype(BF16)
        dv_ref[...] = jnp.concatenate([dv_acc[0], dv_acc[1]], axis=1).astype(BF16)
        dck_ref[0] = dc_acc[...]

        @pl.when(pl.program_id(0) == 0)
        def _():
            dgq_ref[...] = jnp.zeros_like(dgq_ref)
            dgk_ref[...] = jnp.zeros_like(dgk_ref)
        dgq_ref[...] += dgq
        dgk_ref[...] += dgk

        @pl.when(pl.program_id(0) == _HP - 1)
        def _():
            for cp in _chip_swap_copies(p_refs, fc_refs, *sems):
                cp.wait()

    fixed = lambda p: (0, 0)
    pair = pl.BlockSpec((T, LANES), lambda p: (0, p))
    gsp = pl.BlockSpec((1, DH), fixed)
    ckspec = pl.BlockSpec((1, 2, T), lambda p: (p, 0, 0))
    anyspec = pl.BlockSpec(memory_space=pl.ANY)
    res = pl.pallas_call(
        body, name="attn_bwd", grid=(_HP,),
        in_specs=_qkv_specs() + [ckspec, gsp, gsp, pl.BlockSpec((1, T, 2), lambda p: (p, 0, 0)), pair] + [anyspec] * n,
        out_specs=[pair, pair, pair, ckspec, gsp, gsp] + [anyspec] * n,
        out_shape=[_sds((T, AW), BF16)] * 3 + [_sds((_HP, 2, T)), _sds((1, DH)), _sds((1, DH))]
        + [_sds((3, *p.shape[1:]), p.dtype) for p in parts],
        scratch_shapes=[pltpu.VMEM((2, T, DH), F32), pltpu.VMEM((2, T, DH), F32), pltpu.VMEM((2, T), F32),
                        pltpu.SemaphoreType.DMA((3 * n,)), pltpu.SemaphoreType.DMA((3 * n,))],
        compiler_params=_cp(("arbitrary",)),
    )(z, z, z, ck, g_q, g_k, lse, datt, *parts)
    return (*res[:6], res[6:])


def _disc(lr, li, ls, br_t, bi_t):
    step = jnp.exp(ls)
    er = jnp.exp(lr * step)
    ab_re = er * jnp.cos(li * step)
    ab_im = er * jnp.sin(li * step)
    num_re = ab_re - 1.0
    num_im = ab_im
    den = lr * lr + li * li
    f_re = (num_re * lr + num_im * li) / den
    f_im = (num_im * lr - num_re * li) / den
    bb_re = f_re[:, None, :] * br_t - f_im[:, None, :] * bi_t
    bb_im = f_re[:, None, :] * bi_t + f_im[:, None, :] * br_t
    return ab_re, ab_im, bb_re, bb_im


def _disc_fwd(lr, li, ls, br_t, bi_t):
    def body(lr_ref, li_ref, ls_ref, br_ref, bi_ref, ar_ref, ai_ref, bbr_ref, bbi_ref):
        ar, ai, bbr, bbi = _disc(lr_ref[...], li_ref[...], ls_ref[...], br_ref[...], bi_ref[...])
        ar_ref[...] = ar
        ai_ref[...] = ai
        bbr_ref[...] = bbr
        bbi_ref[...] = bbi

    return pl.pallas_call(
        body, name="ssm_disc_fwd",
        out_shape=[_sds((NG, NP)), _sds((NG, NP)), _sds((NG, GS, NP)), _sds((NG, GS, NP))],
        compiler_params=_cp())(lr, li, ls, br_t, bi_t)


def _disc_bwd(lr, li, ls, br_t, bi_t, dar, dai, dbbr, dbbi):
    def body(lr_ref, li_ref, ls_ref, br_ref, bi_ref, dar_ref, dai_ref, dbbr_ref, dbbi_ref,
             dlr_ref, dli_ref, dls_ref, dbr_ref, dbi_ref):
        _, vjp = jax.vjp(_disc, lr_ref[...], li_ref[...], ls_ref[...], br_ref[...], bi_ref[...])
        dlr, dli, dls, dbr, dbi = vjp((dar_ref[...], dai_ref[...], dbbr_ref[...], dbbi_ref[...]))
        dlr_ref[...] = dlr
        dli_ref[...] = dli
        dls_ref[...] = dls
        dbr_ref[...] = dbr
        dbi_ref[...] = dbi

    return pl.pallas_call(
        body, name="ssm_disc_bwd",
        out_shape=[_sds((NG, NP)), _sds((NG, NP)), _sds((NG, 1)), _sds((NG, GS, NP)), _sds((NG, GS, NP))],
        compiler_params=_cp())(lr, li, ls, br_t, bi_t, dar, dai, dbbr, dbbi)


def _cmul(ar, ai, br, bi):
    return ar * br - ai * bi, ar * bi + ai * br


def _scan(buf, a_re, a_im, reverse, other=None):
    ar = [jnp.broadcast_to(a_re[:, LANES * k:LANES * (k + 1)], (NSEG, LANES)) for k in range(4)]
    ai = [jnp.broadcast_to(a_im[:, LANES * k:LANES * (k + 1)], (NSEG, LANES)) for k in range(4)]

    def tile(ref, i, k):
        return ref[pl.ds(pl.multiple_of(i * NSEG, NSEG), NSEG), LANES * k:LANES * (k + 1)]

    def advance(i, xr, xi):
        nr = [ar[k] * xr[k] - ai[k] * xi[k] + tile(buf, i, k) for k in range(4)]
        ni = [ar[k] * xi[k] + ai[k] * xr[k] + tile(buf, i, 4 + k) for k in range(4)]
        return nr, ni

    def index(i):
        return (SEG - 1 - i) if reverse else i

    zeros = tuple(jnp.zeros((NSEG, LANES), F32) for _ in range(4))

    def sweep1(i, carry):
        xr, xi = carry
        for sub in range(_SCAN_UNROLL):
            xr, xi = advance(index(i * _SCAN_UNROLL + sub), xr, xi)
        return tuple(xr), tuple(xi)

    er, ei = lax.fori_loop(0, SEG // _SCAN_UNROLL, sweep1, (zeros, zeros))

    row = lax.broadcasted_iota(jnp.int32, (NSEG, LANES), 0)
    sr, si = [], []
    for k in range(4):
        pr, pi = ar[k], ai[k]
        for _ in range(SEG.bit_length() - 1):
            pr, pi = _cmul(pr, pi, pr, pi)
        ir, ii = er[k], ei[k]
        for d in (1, 2, 4):
            shift = (NSEG - d) if reverse else d
            ok = (row < NSEG - d) if reverse else (row >= d)
            mr, mi = _cmul(pr, pi, pltpu.roll(ir, shift, 0), pltpu.roll(ii, shift, 0))
            ir = ir + jnp.where(ok, mr, 0.0)
            ii = ii + jnp.where(ok, mi, 0.0)
            pr, pi = _cmul(pr, pi, pr, pi)
        shift = (NSEG - 1) if reverse else 1
        ok = (row < NSEG - 1) if reverse else (row >= 1)
        sr.append(jnp.where(ok, pltpu.roll(ir, shift, 0), 0.0))
        si.append(jnp.where(ok, pltpu.roll(ii, shift, 0), 0.0))

    def sweep2(i, carry):
        xr, xi, dar, dai = carry
        for sub in range(_SCAN_UNROLL):
            idx = index(i * _SCAN_UNROLL + sub)
            if other is not None:
                orr = [tile(other, idx, k) for k in range(4)]
                oi = [tile(other, idx, 4 + k) for k in range(4)]
                dar = tuple(dar[k] + xr[k] * orr[k] + xi[k] * oi[k] for k in range(4))
                dai = tuple(dai[k] + xi[k] * orr[k] - xr[k] * oi[k] for k in range(4))
            xr, xi = advance(idx, xr, xi)
            start = pl.multiple_of(idx * NSEG, NSEG)
            for k in range(4):
                buf[pl.ds(start, NSEG), LANES * k:LANES * (k + 1)] = xr[k]
                buf[pl.ds(start, NSEG), LANES * (4 + k):LANES * (5 + k)] = xi[k]
        return tuple(xr), tuple(xi), dar, dai

    _, _, dar, dai = lax.fori_loop(0, SEG // _SCAN_UNROLL, sweep2, (tuple(sr), tuple(si), zeros, zeros))
    if other is None:
        return None
    da_re = jnp.concatenate([jnp.sum(d, axis=0, keepdims=True) for d in dar], axis=1)
    da_im = jnp.concatenate([jnp.sum(d, axis=0, keepdims=True) for d in dai], axis=1)
    return da_re, da_im


_SSM_BLOCKS = 4


def _seg_copies(hbm_ref, col0, buf, block, sems, to_hbm):
    cps = []
    for j in range(NSEG):
        hbm = hbm_ref.at[pl.ds(SEG * j, SEG), pl.ds(col0, LANES)]
        vmem = buf.at[block, :, j, :]
        cps.append(pltpu.make_async_copy(vmem, hbm, sems.at[block, j]) if to_hbm
                   else pltpu.make_async_copy(hbm, vmem, sems.at[block, j]))
    return cps


def _seg_scratch():
    return [pltpu.VMEM((_SSM_BLOCKS, SEG, NSEG, LANES), F32), pltpu.SemaphoreType.DMA((_SSM_BLOCKS, NSEG))]


_U_COL = 3 * AW


def _ssm_fwd(z, bb, ab, cc, dsk):
    last = _SSM_BLOCKS - 1

    def body(z_ref, bb_ref, a_ref, cc_ref, d_ref, y_ref, xs_ref, ubuf, lsem, ybuf, ssem):
        step = pl.program_id(0)

        @pl.when(step == 0)
        def _():
            for b in range(_SSM_BLOCKS):
                for cp in _seg_copies(z_ref, _U_COL + LANES * b, ubuf, b, lsem, False):
                    cp.start()

        for cp in _seg_copies(z_ref, _U_COL, ubuf, step, lsem, False):
            cp.wait()
        ub = ubuf[step].reshape(T, LANES)
        xb = xs_ref.at[0]
        xb[...] = _nn(ub.astype(BF16), bb_ref[0])
        _scan(xb, a_ref[0, 0:1, :], a_ref[0, 1:2, :], reverse=False)
        y = _nn(xb[...].astype(BF16), cc_ref[0]) + d_ref[0] * ub
        ybuf[step] = y.reshape(SEG, NSEG, LANES)
        for cp in _seg_copies(y_ref, pl.multiple_of(step * LANES, LANES), ybuf, step, ssem, True):
            cp.start()

        @pl.when(step == last)
        def _():
            for b in range(_SSM_BLOCKS):
                for cp in _seg_copies(y_ref, LANES * b, ybuf, b, ssem, True):
                    cp.wait()

    blk = lambda j: (j, 0, 0)
    anyspec = pl.BlockSpec(memory_space=pl.ANY)
    return pl.pallas_call(
        body, name="ssm_fwd", grid=(_SSM_BLOCKS,),
        in_specs=[anyspec, pl.BlockSpec((1, LANES, 1024), blk), pl.BlockSpec((1, 2, 512), blk),
                  pl.BlockSpec((1, 1024, LANES), blk), pl.BlockSpec((1, 1, LANES), blk)],
        out_specs=[anyspec, pl.BlockSpec((1, T, 1024), blk)],
        out_shape=[_sds((T, SW)), _sds((_SSM_BLOCKS, T, 1024))],
        scratch_shapes=_seg_scratch() + _seg_scratch(),
        compiler_params=_cp(("arbitrary",)),
    )(z, bb, ab, cc, dsk)


def _ssm_bwd(z, dy, xs, bb, ab, cc, dsk, arrs, parts):
    n, m = len(arrs), len(parts)
    last = _SSM_BLOCKS - 1

    def body(z_ref, dy_ref, x_ref, bb_ref, a_ref, cc_ref, d_ref, *rest):
        a_refs, p_refs, rest = rest[:n], rest[n:n + m], rest[n + m:]
        du_ref, dbb_ref, da_ref, dcc_ref, dd_ref = rest[:5]
        fs_refs, fc_refs, gb = rest[5:5 + n], rest[5 + n:5 + n + m], rest[5 + n + m]
        ubuf, usem, dybuf, dysem, dubuf, dusem = rest[6 + n + m:12 + n + m]
        sems = rest[12 + n + m:]

        def carried():
            return _sibling_swap_copies(a_refs, fs_refs, *sems[:2]) + _chip_swap_copies(p_refs, fc_refs, *sems[2:])

        step = pl.program_id(0)

        @pl.when(step == 0)
        def _():
            for cp in carried():
                cp.start()
            for b in range(_SSM_BLOCKS):
                for cp in (_seg_copies(z_ref, _U_COL + LANES * b, ubuf, b, usem, False)
                           + _seg_copies(dy_ref, LANES * b, dybuf, b, dysem, False)):
                    cp.start()

        for cp in _seg_copies(z_ref, _U_COL, ubuf, step, usem, False) + _seg_copies(dy_ref, 0, dybuf, step, dysem, False):
            cp.wait()
        ub, dyv = ubuf[step].reshape(T, LANES), dybuf[step].reshape(T, LANES)
        ubb, dyb = ub.astype(BF16), dyv.astype(BF16)
        a_re, a_im = a_ref[0, 0:1, :], a_ref[0, 1:2, :]
        dcc_ref[0] = _tn(x_ref[0].astype(BF16), dyb)
        gb[...] = _nt(dyb, cc_ref[0])
        da_re, da_im = _scan(gb, a_re, -a_im, reverse=True, other=x_ref.at[0])
        da_ref[0] = jnp.concatenate([da_re, da_im], axis=0)
        gc = gb[...].astype(BF16)
        du = _nt(gc, bb_ref[0]) + d_ref[0] * dyv
        dubuf[step] = du.reshape(SEG, NSEG, LANES)
        for cp in _seg_copies(du_ref, pl.multiple_of(step * LANES, LANES), dubuf, step, dusem, True):
            cp.start()
        dbb_ref[0] = _tn(ubb, gc)
        dd_ref[0] = jnp.sum(dyv * ub, axis=0, keepdims=True)

        @pl.when(step == last)
        def _():
            for b in range(_SSM_BLOCKS):
                for cp in _seg_copies(du_ref, LANES * b, dubuf, b, dusem, True):
                    cp.wait()
            for cp in carried():
                cp.wait()

    blk = lambda j: (j, 0, 0)
    anyspec = pl.BlockSpec(memory_space=pl.ANY)
    res = pl.pallas_call(
        body, name="ssm_bwd", grid=(_SSM_BLOCKS,),
        in_specs=[anyspec, anyspec, pl.BlockSpec((1, T, 1024), blk), pl.BlockSpec((1, LANES, 1024), blk),
                  pl.BlockSpec((1, 2, 512), blk), pl.BlockSpec((1, 1024, LANES), blk), pl.BlockSpec((1, 1, LANES), blk)]
        + [anyspec] * (n + m),
        out_specs=[anyspec, pl.BlockSpec((1, LANES, 1024), blk), pl.BlockSpec((1, 2, 512), blk),
                   pl.BlockSpec((1, 1024, LANES), blk), pl.BlockSpec((1, 1, LANES), blk)] + [anyspec] * (n + m),
        out_shape=[_sds((T, SW)), _sds((4, LANES, 1024)), _sds((4, 2, 512)), _sds((4, 1024, LANES)), _sds((4, 1, LANES))]
        + [_sds((4, *a.shape[2:]), a.dtype) for a in arrs] + [_sds((3, *p.shape[1:]), p.dtype) for p in parts],
        scratch_shapes=[pltpu.VMEM((T, 1024), F32)] + _seg_scratch() + _seg_scratch() + _seg_scratch()
        + [pltpu.SemaphoreType.DMA((4 * n,)), pltpu.SemaphoreType.DMA((4 * n,)),
           pltpu.SemaphoreType.DMA((3 * m,)), pltpu.SemaphoreType.DMA((3 * m,))],
        compiler_params=_cp(("arbitrary",)),
    )(z, dy, xs, bb, ab, cc, dsk, *arrs, *parts)
    return (*res[:5], res[5:5 + n], res[5 + n:])


_TM_MIX = 512


def _mix_parts(att, y, wg, bg):
    y2 = jax.nn.gelu(y)
    y2b = y2.astype(BF16)
    sg = jax.nn.sigmoid(_nn(y2b, wg) + bg)
    ssm = y2 * sg
    ra, rs = _rms_r(att), _rms_r(ssm)
    return y2, y2b, sg, ssm, ra, rs


def _mix_fwd(att, y, x, wg, bg, ga, gs, wo, bufs):
    m = len(bufs)

    def body(att_ref, y_ref, x_ref, wg_ref, bg_ref, ga_ref, gs_ref, wo_ref, *rest):
        x1_ref, buf_refs, sems = rest[m], rest[m + 1:2 * m + 1], rest[2 * m + 1:]

        @pl.when(pl.program_id(0) == 0)
        def _():
            for cp in _pass_on_copies(buf_refs, *sems):
                cp.start()

        attv = att_ref[...]
        _, _, _, ssm, ra, rs = _mix_parts(attv, y_ref[...], wg_ref[...], bg_ref[...])
        ma = (attv * ra * ga_ref[...]).astype(BF16)
        ms = (ssm * rs * gs_ref[...]).astype(BF16)
        x1_ref[...] = x_ref[...] + _nn(ma, wo_ref[0:AW, :]) + _nn(ms, wo_ref[AW:D, :])

        @pl.when(pl.program_id(0) == T // _TM_MIX - 1)
        def _():
            for cp in _pass_on_copies(buf_refs, *sems):
                cp.wait()

    row = lambda i: (i, 0)
    fixed = lambda i: (0, 0)
    half = pl.BlockSpec((_TM_MIX, AW), row)
    vec = pl.BlockSpec((1, AW), fixed)
    anyspec = pl.BlockSpec(memory_space=pl.ANY)
    res = pl.pallas_call(
        body, name="mix_fwd", grid=(T // _TM_MIX,),
        in_specs=[half, half, pl.BlockSpec((_TM_MIX, D), row), pl.BlockSpec((SW, SW), fixed), vec, vec, vec,
                  pl.BlockSpec((D, D), fixed)] + [anyspec] * m,
        out_specs=[pl.BlockSpec((_TM_MIX, D), row)] + [anyspec] * m,
        out_shape=[_sds((T, D))] + [_sds(b.shape, b.dtype) for b in bufs],
        input_output_aliases={8 + i: 1 + i for i in range(m)},
        scratch_shapes=[pltpu.SemaphoreType.DMA((3 * m,)), pltpu.SemaphoreType.DMA((3 * m,))],
        compiler_params=_cp(("arbitrary",)),
    )(att, y, x, wg, bg, ga, gs, wo, *bufs)
    return res[0], res[1:]


def _mix_bwd(att, y, dx1, wg, bg, ga, gs, wo):
    last = T // _TM_MIX - 1

    def body(att_ref, y_ref, d_ref, wg_ref, bg_ref, ga_ref, gs_ref, wo_ref,
             datt_ref, dy_ref, dwg_ref, dbg_ref, dga_ref, dgs_ref, dwo_ref, dwg_acc, dwo_acc):
        attv, yv, db = att_ref[...], y_ref[...], d_ref[...].astype(BF16)
        y2, y2b, sg, ssm, ra, rs = _mix_parts(attv, yv, wg_ref[...], bg_ref[...])
        na, ns = attv * ra, ssm * rs
        ma = (na * ga_ref[...]).astype(BF16)
        ms = (ns * gs_ref[...]).astype(BF16)
        dma = _nt(db, wo_ref[0:AW, :])
        dms = _nt(db, wo_ref[AW:D, :])
        datt, dga = _rms_bwd(na, ra, ga_ref[...], dma)
        dssm, dgs = _rms_bwd(ns, rs, gs_ref[...], dms)
        dgl = dssm * y2 * sg * (1.0 - sg)
        dglb = dgl.astype(BF16)
        dy2 = dssm * sg + _nt(dglb, wg_ref[...])
        _, gelu_vjp = jax.vjp(jax.nn.gelu, yv)
        datt_ref[...] = datt
        dy_ref[...] = gelu_vjp(dy2)[0]

        @pl.when(pl.program_id(0) == 0)
        def _():
            dwg_acc[...] = jnp.zeros_like(dwg_acc)
            dwo_acc[...] = jnp.zeros_like(dwo_acc)
            dbg_ref[...] = jnp.zeros_like(dbg_ref)
            dga_ref[...] = jnp.zeros_like(dga_ref)
            dgs_ref[...] = jnp.zeros_like(dgs_ref)
        dwg_acc[...] += _tn(y2b, dglb)
        dbg_ref[...] += jnp.sum(dgl, axis=0, keepdims=True)
        dga_ref[...] += dga
        dgs_ref[...] += dgs
        dwo_acc[0:AW, :] += _tn(ma, db)
        dwo_acc[AW:D, :] += _tn(ms, db)

        @pl.when(pl.program_id(0) == last)
        def _():
            dwg_ref[...] = dwg_acc[...].astype(BF16)
            dwo_ref[...] = dwo_acc[...].astype(BF16)

    row = lambda i: (i, 0)
    fixed = lambda i: (0, 0)
    half = pl.BlockSpec((_TM_MIX, AW), row)
    vec = pl.BlockSpec((1, AW), fixed)
    sq = pl.BlockSpec((SW, SW), fixed)
    big = pl.BlockSpec((D, D), fixed)
    return pl.pallas_call(
        body, name="mix_bwd", grid=(T // _TM_MIX,),
        in_specs=[half, half, pl.BlockSpec((_TM_MIX, D), row), sq, vec, vec, vec, big],
        out_specs=[half, half, sq, vec, vec, vec, big],
        out_shape=[_sds((T, AW)), _sds((T, SW)), _sds((SW, SW), BF16), _sds((1, SW)), _sds((1, AW)), _sds((1, SW)),
                   _sds((D, D), BF16)],
        scratch_shapes=[pltpu.VMEM((SW, SW), F32), pltpu.VMEM((D, D), F32)],
        compiler_params=_cp(("arbitrary",)),
    )(att, y, dx1, wg, bg, ga, gs, wo)


_NCB = -(-CS // LANES)


def _ffn_up(x1, g, wu4, shards):
    tm = 1024
    n = len(shards)
    steps = T // tm

    def body(x_ref, g_ref, w_ref, *rest):
        x_refs, (h_ref, up_ref), out_refs, sems = rest[:n], rest[n:n + 2], rest[n + 2:2 * n + 2], rest[2 * n + 2:]
        i, e = pl.program_id(0), pl.program_id(1)

        @pl.when((i == 0) & (e == 0))
        def _():
            local, remote = _gather_first_copies(x_refs, out_refs, *sems)
            for cp in local + remote:
                cp.start()

        @pl.when(e == 0)
        def _():
            xv = x_ref[...]
            h_ref[...] = (xv * _rms_r(xv) * g_ref[...]).astype(BF16)
        up_ref[0, 0] = _nt(h_ref[...], w_ref[0, 0])

        @pl.when((i == steps - 1) & (e == NDEV - 1))
        def _():
            local, remote = _gather_first_copies(x_refs, out_refs, *sems)
            for cp in remote + local:
                cp.wait()

    anyspec = pl.BlockSpec(memory_space=pl.ANY)
    res = pl.pallas_call(
        body, name="ffn_up", grid=(steps, NDEV),
        in_specs=[pl.BlockSpec((tm, D), lambda i, e: (i, 0)), pl.BlockSpec((1, D), lambda i, e: (0, 0)),
                  pl.BlockSpec((1, 1, CS, D), lambda i, e: (e // 4, e % 4, 0, 0))] + [anyspec] * n,
        out_specs=[pl.BlockSpec((tm, D), lambda i, e: (i, 0)),
                   pl.BlockSpec((1, 1, tm, CS), lambda i, e: (e // 4, e % 4, i, 0))] + [anyspec] * n,
        out_shape=[_sds((T, D), BF16), _sds((2, 4, T, CS))] + [_sds((NDEV, *s.shape), s.dtype) for s in shards],
        scratch_shapes=_gather_first_scratch(n),
        compiler_params=_cp(("arbitrary", "arbitrary")),
    )(x1, g, wu4, *shards)
    return res[0], res[1], res[2:]


def _shift_down(h, k):
    row = lax.broadcasted_iota(jnp.int32, h.shape, 0)
    return jnp.where(row >= k, pltpu.roll(h, k, 0), 0.0)


def _shift_up(h, k):
    row = lax.broadcasted_iota(jnp.int32, h.shape, 0)
    return jnp.where(row < T - k, pltpu.roll(h, T - k, 0), 0.0)


def _conv(h, w, b):
    return w[0:1, :] * _shift_down(h, 2) + w[1:2, :] * _shift_down(h, 1) + w[2:3, :] * h + b


def _chan(gv, rows):
    return pl.BlockSpec((1, 1, rows, LANES), lambda d, j: (gv, d, 0, j))


def _ffn_act(up4, cw4, cb4, bufs):
    m = len(bufs)

    def body(ug_ref, uv_ref, wg_ref, wv_ref, bg_ref, bv_ref, *rest):
        act_ref, buf_refs, sems = rest[m], rest[m + 1:2 * m + 1], rest[2 * m + 1:]
        d, j = pl.program_id(0), pl.program_id(1)

        @pl.when((d == 0) & (j == 0))
        def _():
            for cp in _pass_on_copies(buf_refs, *sems):
                cp.start()

        g = _conv(ug_ref[0, 0], wg_ref[0, 0], bg_ref[0, 0])
        v = _conv(uv_ref[0, 0], wv_ref[0, 0], bv_ref[0, 0])
        act_ref[0] = (g * jax.nn.sigmoid(g) * v).astype(BF16)

        @pl.when((d == 3) & (j == _NCB - 1))
        def _():
            for cp in _pass_on_copies(buf_refs, *sems):
                cp.wait()

    anyspec = pl.BlockSpec(memory_space=pl.ANY)
    res = pl.pallas_call(
        body, name="ffn_act", grid=(4, _NCB),
        in_specs=[_chan(0, T), _chan(1, T), _chan(0, 3), _chan(1, 3), _chan(0, 1), _chan(1, 1)] + [anyspec] * m,
        out_specs=[pl.BlockSpec((1, T, LANES), lambda d, j: (d, 0, j))] + [anyspec] * m,
        out_shape=[_sds((4, T, CS), BF16)] + [_sds(b.shape, b.dtype) for b in bufs],
        input_output_aliases={6 + i: 1 + i for i in range(m)},
        scratch_shapes=[pltpu.SemaphoreType.DMA((3 * m,)), pltpu.SemaphoreType.DMA((3 * m,))],
        compiler_params=_cp(("arbitrary", "arbitrary")),
    )(up4, up4, cw4, cw4, cb4, cb4, *bufs)
    return res[0], res[1:]


def _ffn_act_bwd(up4, dact, cw4, cb4):
    def conv_bwd(h, w, d):
        dup = w[2:3, :] * d + w[1:2, :] * _shift_up(d, 1) + w[0:1, :] * _shift_up(d, 2)
        dw = jnp.concatenate([jnp.sum(d * _shift_down(h, 2), axis=0, keepdims=True),
                              jnp.sum(d * _shift_down(h, 1), axis=0, keepdims=True),
                              jnp.sum(d * h, axis=0, keepdims=True)], axis=0)
        return dup, dw, jnp.sum(d, axis=0, keepdims=True)

    def body(ug_ref, uv_ref, da_ref, wg_ref, wv_ref, bg_ref, bv_ref, dup_ref, dw_ref, db_ref):
        ug, uv, da = ug_ref[0, 0], uv_ref[0, 0], da_ref[0]
        g = _conv(ug, wg_ref[0, 0], bg_ref[0, 0])
        v = _conv(uv, wv_ref[0, 0], bv_ref[0, 0])
        sg = jax.nn.sigmoid(g)
        dg = da * v * (sg * (1.0 + g * (1.0 - sg)))
        dv = da * (g * sg)
        dug, dwg, dbg = conv_bwd(ug, wg_ref[0, 0], dg)
        duv, dwv, dbv = conv_bwd(uv, wv_ref[0, 0], dv)
        dup_ref[0, 0] = dug.astype(BF16)
        dup_ref[1, 0] = duv.astype(BF16)
        dw_ref[0, 0] = dwg
        dw_ref[1, 0] = dwv
        db_ref[0, 0] = dbg
        db_ref[1, 0] = dbv

    both = lambda rows: pl.BlockSpec((2, 1, rows, LANES), lambda d, j: (0, d, 0, j))
    return pl.pallas_call(
        body, name="ffn_act_bwd", grid=(4, _NCB),
        in_specs=[_chan(0, T), _chan(1, T), pl.BlockSpec((1, T, LANES), lambda d, j: (d, 0, j)),
                  _chan(0, 3), _chan(1, 3), _chan(0, 1), _chan(1, 1)],
        out_specs=[both(T), both(3), both(1)],
        out_shape=[_sds((2, 4, T, CS), BF16), _sds((2, 4, 3, CS)), _sds((2, 4, 1, CS))],
        compiler_params=_cp(("parallel", "parallel")),
    )(up4, up4, dact, cw4, cw4, cb4, cb4)


def _ffn_down_loss(act, wd4, x1, target):
    tm = 512

    def body(a_ref, w_ref, x1_ref, t_ref, loss_ref, dx_ref, dxb_ref):
        x2 = x1_ref[...]
        for d in range(4):
            x2 = x2 + _nn(a_ref[d], w_ref[d])
        err = x2 - t_ref[...]
        dx = err * (1.0 / D)
        dx_ref[...] = dx
        dxb_ref[...] = dx.astype(BF16)

        @pl.when(pl.program_id(0) == 0)
        def _():
            loss_ref[...] = jnp.zeros_like(loss_ref)
        loss_ref[...] += 0.5 * jnp.sum(jnp.mean(err * err, axis=-1, keepdims=True), axis=0, keepdims=True)

    row = lambda i: (i, 0)
    fixed = lambda i: (0, 0)
    return pl.pallas_call(
        body, name="ffn_down_loss", grid=(T // tm,),
        in_specs=[pl.BlockSpec((4, tm, CS), lambda i: (0, i, 0)), pl.BlockSpec((4, CS, D), lambda i: (0, 0, 0)),
                  pl.BlockSpec((tm, D), row), pl.BlockSpec((tm, D), row)],
        out_specs=[pl.BlockSpec((1, 1), fixed), pl.BlockSpec((tm, D), row), pl.BlockSpec((tm, D), row)],
        out_shape=[_sds((1, 1)), _sds((T, D)), _sds((T, D), BF16)],
        compiler_params=_cp(("arbitrary",)),
    )(act, wd4, x1, target)


def _ffn_dact(dx2b, wd4):
    tm = 1024

    def body(d_ref, w_ref, o_ref):
        o_ref[0] = _nt(d_ref[...], w_ref[0])

    return pl.pallas_call(
        body, name="ffn_dact", grid=(4, T // tm),
        in_specs=[pl.BlockSpec((tm, D), lambda d, i: (i, 0)), pl.BlockSpec((1, CS, D), lambda d, i: (d, 0, 0))],
        out_specs=pl.BlockSpec((1, tm, CS), lambda d, i: (d, i, 0)), out_shape=_sds((4, T, CS)),
        compiler_params=_cp(("parallel", "parallel")),
    )(dx2b, wd4)


def _ffn_dwd(act, dx2b):
    def body(a_ref, d_ref, o_ref):
        o_ref[0] = _tn(a_ref[0], d_ref[...]).astype(BF16)

    return pl.pallas_call(
        body, name="ffn_dwd", grid=(4,),
        in_specs=[pl.BlockSpec((1, T, CS), lambda d: (d, 0, 0)), pl.BlockSpec((T, D), lambda d: (0, 0))],
        out_specs=pl.BlockSpec((1, CS, D), lambda d: (d, 0, 0)), out_shape=_sds((4, CS, D), BF16),
        compiler_params=_cp(("parallel",)),
    )(act, dx2b)


def _ffn_dwu(h2, dup4):
    tn = 1024

    def body(h_ref, d_ref, o_ref):
        o_ref[0, 0] = _tn(d_ref[0, 0], h_ref[...]).astype(BF16)

    return pl.pallas_call(
        body, name="ffn_dwu", grid=(NDEV, D // tn),
        in_specs=[pl.BlockSpec((T, tn), lambda e, i: (0, i)),
                  pl.BlockSpec((1, 1, T, CS), lambda e, i: (e // 4, e % 4, 0, 0))],
        out_specs=pl.BlockSpec((1, 1, CS, tn), lambda e, i: (e // 4, e % 4, 0, i)),
        out_shape=_sds((2, 4, CS, D), BF16),
        compiler_params=_cp(("parallel", "parallel")),
    )(h2, dup4)


def _ffn_up_bwd(dup4, wu4, x1, g, dres, arrs):
    tm = 256
    n = len(arrs)

    def body(d_ref, w_ref, x_ref, g_ref, dres_ref, *rest):
        a_refs, (dx_ref, dg_ref), fs_refs, sems = rest[:n], rest[n:n + 2], rest[n + 2:2 * n + 2], rest[2 * n + 2:]

        @pl.when(pl.program_id(0) == 0)
        def _():
            for cp in _sibling_swap_copies(a_refs, fs_refs, *sems):
                cp.start()

        dh = jnp.zeros((tm, D), F32)
        for gv in range(2):
            for d in range(4):
                dh = dh + _nn(d_ref[gv, d], w_ref[gv, d])
        xv = x_ref[...]
        r = _rms_r(xv)
        dxr, dgp = _rms_bwd(xv * r, r, g_ref[...], dh)
        dx_ref[...] = dres_ref[...] + dxr

        @pl.when(pl.program_id(0) == 0)
        def _():
            dg_ref[...] = jnp.zeros_like(dg_ref)
        dg_ref[...] += dgp

        @pl.when(pl.program_id(0) == T // tm - 1)
        def _():
            for cp in _sibling_swap_copies(a_refs, fs_refs, *sems):
                cp.wait()

    row = lambda i: (i, 0)
    fixed = lambda i: (0, 0)
    anyspec = pl.BlockSpec(memory_space=pl.ANY)
    res = pl.pallas_call(
        body, name="ffn_up_bwd", grid=(T // tm,),
        in_specs=[pl.BlockSpec((2, 4, tm, CS), lambda i: (0, 0, i, 0)), pl.BlockSpec((2, 4, CS, D), lambda i: (0, 0, 0, 0)),
                  pl.BlockSpec((tm, D), row), pl.BlockSpec((1, D), fixed), pl.BlockSpec((tm, D), row)] + [anyspec] * n,
        out_specs=[pl.BlockSpec((tm, D), row), pl.BlockSpec((1, D), fixed)] + [anyspec] * n,
        out_shape=[_sds((T, D)), _sds((1, D))] + [_sds((4, *a.shape[2:]), a.dtype) for a in arrs],
        scratch_shapes=[pltpu.SemaphoreType.DMA((4 * n,)), pltpu.SemaphoreType.DMA((4 * n,))],
        compiler_params=_cp(("arbitrary",)),
    )(dup4, wu4, x1, g, dres, *arrs)
    return res[0], res[1], res[2:]


def _adamw_math(w, g, m, v):
    m = ADAM_B1 * m + (1.0 - ADAM_B1) * g
    v = ADAM_B2 * v + (1.0 - ADAM_B2) * jnp.square(g)
    m_hat = m / (1.0 - ADAM_B1 ** ADAM_STEP)
    v_hat = v / (1.0 - ADAM_B2 ** ADAM_STEP)
    delta = -ADAM_LR * (m_hat / (jnp.sqrt(v_hat) + ADAM_EPS) + ADAM_WD * w)
    return delta, m, v


def _where_am_i():
    x, y, c = _place()
    return jnp.stack([c, 2 * x + y]).astype(jnp.int32)


def _pair_sum(a4, recv, name):
    _, _, rows, cols = a4.shape
    tr = _row_tile(rows, cols, 3 << 20)

    def body(sel_ref, own_ref, recv_ref, out_ref):
        out_ref[...] = (own_ref[0].astype(F32) + recv_ref[...].astype(F32)).astype(out_ref.dtype)

    grid_spec = pltpu.PrefetchScalarGridSpec(
        num_scalar_prefetch=1, grid=(4, rows // tr),
        in_specs=[pl.BlockSpec((1, 1, tr, cols), lambda k, i, s: (k, s[0], i, 0)),
                  pl.BlockSpec((1, tr, cols), lambda k, i, s: (k, i, 0))],
        out_specs=pl.BlockSpec((1, tr, cols), lambda k, i, s: (k, i, 0)),
    )
    return pl.pallas_call(
        body, name=name, grid_spec=grid_spec, out_shape=_sds((4, rows, cols), a4.dtype),
        compiler_params=_cp(("arbitrary", "arbitrary")),
    )(_where_am_i(), a4, recv)


def _adamw_rs(a4, recv, from_chips, w, m, v, name):
    rows, cols = w.shape
    tr = _row_tile(rows, cols, 3 << 19)

    def body(sel_ref, own_ref, recv_ref, fc_ref, w_ref, m_ref, v_ref, g_ref, d_ref, nm_ref, nv_ref):
        g = own_ref[0, 0].astype(F32) + recv_ref[0].astype(F32)
        for j in range(3):
            g = g + fc_ref[j].astype(F32)
        d, nm, nv = _adamw_math(w_ref[...], g, m_ref[...], v_ref[...])
        g_ref[...] = g
        d_ref[...] = d
        nm_ref[...] = nm
        nv_ref[...] = nv

    flat = pl.BlockSpec((tr, cols), lambda i, s: (i, 0))
    grid_spec = pltpu.PrefetchScalarGridSpec(
        num_scalar_prefetch=1, grid=(rows // tr,),
        in_specs=[pl.BlockSpec((1, 1, tr, cols), lambda i, s: (s[1], s[0], i, 0)),
                  pl.BlockSpec((1, tr, cols), lambda i, s: (s[1], i, 0)),
                  pl.BlockSpec((3, tr, cols), lambda i, s: (0, i, 0)), flat, flat, flat],
        out_specs=[flat] * 4,
    )
    return pl.pallas_call(
        body, name=name, grid_spec=grid_spec, out_shape=[_sds((rows, cols))] * 4,
        compiler_params=_cp(("arbitrary",)),
    )(_where_am_i(), a4, recv, from_chips, w, m, v)


_SMALL_F32 = (("g_mix", (8, 128)), ("b_f", (1, 8)), ("g_q", (1, 64)), ("g_k", (1, 64)), ("lambda_re", (32, 64)),
              ("lambda_im", (32, 64)), ("log_step", (1, 32)), ("d_skip", (32, 16)), ("b_glu", (4, 128)),
              ("g_attn_out", (4, 128)), ("g_ssm_out", (4, 128)), ("g_ffn", (8, 128)), ("conv_b", (44, 128)))
_SMALL_PAIRS = (("b_re", "b_im"), ("c_re", "c_im"))
_PAIR_SHAPE = (NG * GS, NP)


def _ceil8(r):
    return -(-r // 8) * 8


def _small_names():
    return [n for n, _ in _SMALL_F32] + [n for pair in _SMALL_PAIRS for n in pair]


def _pack_small_grads(g, loss_dev):
    parts = [jnp.pad(g[n].reshape(r, c), ((0, _ceil8(r) - r), (0, LANES - c))) for n, (r, c) in _SMALL_F32]
    parts.append(jnp.pad(loss_dev, ((0, 7), (0, LANES - 1))))
    pairs = [jnp.concatenate([g[a].reshape(_PAIR_SHAPE), g[b].reshape(_PAIR_SHAPE)], axis=1) for a, b in _SMALL_PAIRS]
    return jnp.concatenate(parts, axis=0), jnp.concatenate(pairs, axis=0).astype(BF16)


def _adamw_small(gall, gall_b, wmv):
    names = _small_names()
    shapes = [s for _, s in _SMALL_F32] + [_PAIR_SHAPE] * (2 * len(_SMALL_PAIRS))
    npar = len(names)

    def body(ga_ref, gb_ref, *rest):
        ins, loss_ref, outs = rest[:3 * npar], rest[3 * npar], rest[3 * npar + 1:]
        ga, gb = ga_ref[0], gb_ref[0].astype(F32)
        for dev in range(1, NDEV):
            ga = ga + ga_ref[dev]
            gb = gb + gb_ref[dev].astype(F32)
        grads, off = [], 0
        for _, (r, c) in _SMALL_F32:
            grads.append(ga[off:off + r, 0:c])
            off += _ceil8(r)
        loss_ref[...] = ga[off:off + 8]
        rows, cols = _PAIR_SHAPE
        for j in range(len(_SMALL_PAIRS)):
            for h in range(2):
                grads.append(gb[rows * j:rows * (j + 1), cols * h:cols * (h + 1)])
        for i, g in enumerate(grads):
            d, nm, nv = _adamw_math(ins[3 * i][...], g, ins[3 * i + 1][...], ins[3 * i + 2][...])
            outs[4 * i][...] = g
            outs[4 * i + 1][...] = d
            outs[4 * i + 2][...] = nm
            outs[4 * i + 3][...] = nv

    flat = [a for n in names for a in wmv[n]]
    out_shape = [_sds((8, LANES))] + [_sds(s) for s in shapes for _ in range(4)]
    res = pl.pallas_call(body, name="adamw_small", out_shape=out_shape, compiler_params=_cp())(gall, gall_b, *flat)
    return res[0], {n: res[1 + 4 * i:5 + 4 * i] for i, n in enumerate(names)}


def _block_diag(m):
    eye = jnp.eye(8, dtype=m.dtype)
    r, c = m.shape[2], m.shape[3]
    return (m[:, :, :, None, :] * eye[None, :, None, :, None]).reshape(4, 8 * r, 8 * c)


def _diag_blocks(dense, r, c):
    eye = jnp.eye(8, dtype=dense.dtype)
    return jnp.sum(dense.reshape(4, 8, r, 8, c) * eye[None, :, None, :, None], axis=3)


def kernel(x, g_mix, w_in, b_f, g_q, g_k, lambda_re, lambda_im, log_step, b_re, b_im, c_re, c_im, d_skip, w_glu, b_glu, g_attn_out, g_ssm_out, w_out, g_ffn, w_up, conv_w, conv_b, w_down, loss_target, m_g_mix, m_w_in, m_b_f, m_g_q, m_g_k, m_lambda_re, m_lambda_im, m_log_step, m_b_re, m_b_im, m_c_re, m_c_im, m_d_skip, m_w_glu, m_b_glu, m_g_attn_out, m_g_ssm_out, m_w_out, m_g_ffn, m_w_up, m_conv_w, m_conv_b, m_w_down, v_g_mix, v_w_in, v_b_f, v_g_q, v_g_k, v_lambda_re, v_lambda_im, v_log_step, v_b_re, v_b_im, v_c_re, v_c_im, v_d_skip, v_w_glu, v_b_glu, v_g_attn_out, v_g_ssm_out, v_w_out, v_g_ffn, v_w_up, v_conv_w, v_conv_b, v_w_down):
    weights = dict(g_mix=g_mix, w_in=w_in, b_f=b_f, g_q=g_q, g_k=g_k, lambda_re=lambda_re, lambda_im=lambda_im,
                   log_step=log_step, b_re=b_re, b_im=b_im, c_re=c_re, c_im=c_im, d_skip=d_skip, w_glu=w_glu,
                   b_glu=b_glu, g_attn_out=g_attn_out, g_ssm_out=g_ssm_out, w_out=w_out, g_ffn=g_ffn, w_up=w_up,
                   conv_w=conv_w, conv_b=conv_b, w_down=w_down)
    mom_m = dict(g_mix=m_g_mix, w_in=m_w_in, b_f=m_b_f, g_q=m_g_q, g_k=m_g_k, lambda_re=m_lambda_re,
                 lambda_im=m_lambda_im, log_step=m_log_step, b_re=m_b_re, b_im=m_b_im, c_re=m_c_re, c_im=m_c_im,
                 d_skip=m_d_skip, w_glu=m_w_glu, b_glu=m_b_glu, g_attn_out=m_g_attn_out, g_ssm_out=m_g_ssm_out,
                 w_out=m_w_out, g_ffn=m_g_ffn, w_up=m_w_up, conv_w=m_conv_w, conv_b=m_conv_b, w_down=m_w_down)
    mom_v = dict(g_mix=v_g_mix, w_in=v_w_in, b_f=v_b_f, g_q=v_g_q, g_k=v_g_k, lambda_re=v_lambda_re,
                 lambda_im=v_lambda_im, log_step=v_log_step, b_re=v_b_re, b_im=v_b_im, c_re=v_c_re, c_im=v_c_im,
                 d_skip=v_d_skip, w_glu=v_w_glu, b_glu=v_b_glu, g_attn_out=v_g_attn_out, g_ssm_out=v_g_ssm_out,
                 w_out=v_w_out, g_ffn=v_g_ffn, w_up=v_w_up, conv_w=v_conv_w, conv_b=v_conv_b, w_down=v_w_down)
    names = list(weights)
    big = ("w_in", "w_glu", "w_out", "w_up", "w_down", "conv_w")

    xs = x[0]
    target = loss_target[0]
    row = lambda a: a.reshape(1, -1)

    (win_g,) = _all_gather([w_in.T.astype(BF16)], "gather_w_in")
    w_in_t = win_g.reshape(8 * 257, D)
    wcat = jnp.concatenate([w_in_t[0:1536], w_in_t[1544:2056], jnp.pad(w_in_t[1536:1544], ((0, LANES - NH), (0, 0)))],
                           axis=0)
    cb4 = conv_b.reshape(2, 4, 1, CS)

    h1, z, f_t, (wg_g, wo_g) = _in_proj(xs, row(g_mix), wcat, [w_glu.astype(BF16), w_out.astype(BF16)])
    bf_col = b_f.reshape(NH, 1)
    ck = _gates_fwd(f_t, bf_col).reshape(_HP, 2, T)
    att, lse, (wu_g, cw_g), (wg_g, wo_g) = _attn_fwd(z, ck, row(g_q), row(g_k), [w_up.T.astype(BF16), conv_w],
                                                       [wg_g, wo_g])

    ls_col = log_step.reshape(NG, 1)
    br_t, bi_t = b_re.transpose(0, 2, 1), b_im.transpose(0, 2, 1)
    ab_re, ab_im, bb_re, bb_im = _disc_fwd(lambda_re, lambda_im, ls_col, br_t, bi_t)
    bb = jnp.concatenate([_block_diag(bb_re.reshape(4, 8, GS, NP)), _block_diag(bb_im.reshape(4, 8, GS, NP))],
                         axis=2).astype(BF16)
    ab = jnp.stack([ab_re.reshape(4, 512), ab_im.reshape(4, 512)], axis=1)
    cdiag = lambda cm: _block_diag(cm.reshape(4, 8, GS, NP).transpose(0, 1, 3, 2))
    cc = jnp.concatenate([cdiag(c_re), -cdiag(c_im)], axis=1).astype(BF16)
    dsk = d_skip.reshape(4, 1, LANES)
    y, xs_il = _ssm_fwd(z, bb, ab, cc, dsk)
    wg_f = wg_g.reshape(SW, SW)
    wo_f = wo_g.reshape(D, D)
    x1, (wu_g, cw_g) = _mix_fwd(att, y, xs, wg_f, row(b_glu), row(g_attn_out), row(g_ssm_out), wo_f, [wu_g, cw_g])
    wu4 = wu_g.reshape(2, 4, CS, D)
    cw4 = cw_g.reshape(2, 4, 3, CS)
    h2, up4, (wd_g,) = _ffn_up(x1, row(g_ffn), wu4, [w_down.astype(BF16)])
    act, (wd_g,) = _ffn_act(up4, cw4, cb4, [wd_g])
    wd4 = wd_g.reshape(4, CS, D)
    loss_dev, dx2, dx2b = _ffn_down_loss(act, wd4, x1, target)

    dact = _ffn_dact(dx2b, wd4)
    d_wd = _ffn_dwd(act, dx2b)
    dup4, dcw4, dcb4 = _ffn_act_bwd(up4, dact, cw4, cb4)
    d_wu = _ffn_dwu(h2, dup4)
    mlp, mixer = ("w_up", "w_down", "conv_w"), ("w_out", "w_glu")
    early = mlp + mixer
    by_dest = {"w_up": d_wu.reshape(4, 2, CS, D), "w_down": d_wd.reshape(4, 2, DFF // NDEV, D),
               "conv_w": dcw4.reshape(4, 2, 3, CS)}
    dx1, dg_ffn, fs = _ffn_up_bwd(dup4, wu4, x1, row(g_ffn), dx2, [by_dest[n] for n in mlp])
    from_sibling = dict(zip(mlp, fs))
    chip_sums = {n: _pair_sum(by_dest[n], from_sibling[n], "rs_pair_sum_" + n) for n in mlp}
    datt, dy, d_wg, d_bg, d_ga, d_gs, d_wo = _mix_bwd(att, y, dx1, wg_f, row(b_glu), row(g_attn_out),
                                                       row(g_ssm_out), wo_f)
    by_dest.update({"w_out": d_wo.reshape(4, 2, D // NDEV, D), "w_glu": d_wg.reshape(4, 2, SW // NDEV, SW)})
    under_ssm = ("w_down", "conv_w")
    du, dbb, dab, dcc, ddsk, fs, fc = _ssm_bwd(z, dy, xs_il, bb, ab, cc, dsk, [by_dest[n] for n in mixer],
                                               [chip_sums[n] for n in under_ssm])
    from_sibling.update(zip(mixer, fs))
    from_chips = dict(zip(under_ssm, fc))
    chip_sums.update({n: _pair_sum(by_dest[n], from_sibling[n], "rs_pair_sum_" + n) for n in mixer})
    du = du.astype(BF16)
    dbb_re = _diag_blocks(dbb[:, :, 0:512], GS, NP).reshape(NG, GS, NP)
    dbb_im = _diag_blocks(dbb[:, :, 512:1024], GS, NP).reshape(NG, GS, NP)
    dlr, dli, dls, dbr_t, dbi_t = _disc_bwd(lambda_re, lambda_im, ls_col, br_t, bi_t,
                                            dab[:, 0].reshape(NG, NP), dab[:, 1].reshape(NG, NP), dbb_re, dbb_im)
    d_cre = _diag_blocks(dcc[:, 0:512], NP, GS).transpose(0, 1, 3, 2).reshape(NG, GS, NP)
    d_cim = -_diag_blocks(dcc[:, 512:1024], NP, GS).transpose(0, 1, 3, 2).reshape(NG, GS, NP)

    under_attn = ("w_up",) + mixer
    dq, dk, dv, dck, d_gq, d_gk, fc = _attn_bwd(z, ck, row(g_q), row(g_k), lse, datt, [chip_sums[n] for n in under_attn])
    from_chips.update(zip(under_attn, fc))
    df_t, d_bf = _gates_bwd(dck.reshape(NH, T), f_t, bf_col)
    df = jnp.concatenate([df_t.T, jnp.zeros((T, LANES - NH), F32)], axis=1).astype(BF16)
    d_wcat = _in_proj_dw(h1, dq, dk, dv, du, df)
    d_win = jnp.concatenate([d_wcat[0:1536], d_wcat[2048:2048 + NH], d_wcat[1536:2048]], axis=0)
    by_dest["w_in"] = d_win.astype(BF16).reshape(4, 2, 257, D)
    (from_sibling["w_in"],) = _swap_sibling([by_dest["w_in"]], "rs_pair_swap_w_in")
    chip_sums["w_in"] = _pair_sum(by_dest["w_in"], from_sibling["w_in"], "rs_pair_sum_w_in")
    grad_x, dg_mix, (from_chips["w_in"],) = _in_proj_bwd(dq, dk, dv, du, df, wcat, xs, row(g_mix), dx1, [chip_sums["w_in"]])

    grads, deltas, new_m, new_v = {}, {}, {}, {}
    for n in early + ("w_in",):
        flip = (lambda a: a.T) if n in ("w_up", "w_in") else (lambda a: a)
        outs = _adamw_rs(by_dest[n], from_sibling[n], from_chips[n], flip(weights[n]), flip(mom_m[n]), flip(mom_v[n]),
                         "adamw_" + n)
        grads[n], deltas[n], new_m[n], new_v[n] = (flip(o) for o in outs)

    small = dict(g_mix=dg_mix, b_f=d_bf, g_q=d_gq, g_k=d_gk, lambda_re=dlr, lambda_im=dli, log_step=dls,
                 b_re=dbr_t, b_im=dbi_t, c_re=d_cre, c_im=d_cim, d_skip=ddsk, b_glu=d_bg, g_attn_out=d_ga,
                 g_ssm_out=d_gs, g_ffn=dg_ffn, conv_b=dcb4)
    small_all, small_all_b = _all_gather(list(_pack_small_grads(small, loss_dev)), "gather_small_grads")
    views = dict(_SMALL_F32)
    swapped = ("b_re", "b_im")

    def view(n, a):
        return a.transpose(0, 2, 1).reshape(_PAIR_SHAPE) if n in swapped else a.reshape(views.get(n, _PAIR_SHAPE))

    loss_p, small_out = _adamw_small(small_all, small_all_b,
                                     {n: (view(n, weights[n]), view(n, mom_m[n]), view(n, mom_v[n])) for n in small})
    loss = loss_p[0, 0]
    for n in small:
        back = (lambda a: a.reshape(NG, GS, NP).transpose(0, 2, 1)) if n in swapped else (lambda a: a.reshape(weights[n].shape))
        grads[n], deltas[n], new_m[n], new_v[n] = (back(o) for o in small_out[n])

    return (loss, grad_x[None], *[grads[n] for n in names], *[deltas[n] for n in names],
            *[new_m[n] for n in names], *[new_v[n] for n in names])
```

```python
import jax
import jax.numpy as jnp
from jax import lax
from jax.experimental import pallas as pl
from jax.experimental.pallas import tpu as pltpu

F32, BF16 = jnp.float32, jnp.bfloat16
T, D = 2048, 1024
NH, DH = 8, 64
AW, SW = 512, 512
NG, GS, NP = 32, 16, 64
DFF = 2816
NDEV = 8
CS = 2 * DFF // NDEV
EPS = 1e-6
NEG = -1e30
ZC = 4 * 512 + 128
SEG = 256
NSEG = T // SEG
_SCAN_UNROLL = 4
BQ = 512
VMEM_LIMIT = 56 * 1024 * 1024
LANES = 128
MESH = pl.DeviceIdType.MESH
HI = lax.Precision.HIGHEST

ADAM_LR, ADAM_B1, ADAM_B2, ADAM_EPS, ADAM_WD, ADAM_STEP = 0.001, 0.9, 0.999, 1e-08, 0.01, 10


def _cp(dims=None):
    if dims is None:
        return pltpu.CompilerParams(vmem_limit_bytes=VMEM_LIMIT)
    return pltpu.CompilerParams(dimension_semantics=dims, vmem_limit_bytes=VMEM_LIMIT)


def _sds(shape, dtype=F32):
    return jax.ShapeDtypeStruct(shape, dtype)


def _nt(a, b):
    return lax.dot_general(a, b, (((1,), (1,)), ((), ())), preferred_element_type=F32)


def _tn(a, b):
    return lax.dot_general(a, b, (((0,), (0,)), ((), ())), preferred_element_type=F32)


def _nn(a, b):
    return jnp.dot(a, b, preferred_element_type=F32)


def _rms_r(x):
    return lax.rsqrt(jnp.mean(x * x, axis=-1, keepdims=True) + EPS)


def _rms_bwd(n, r, g, dh):
    dn = dh * g
    dx = r * (dn - n * jnp.mean(dn * n, axis=-1, keepdims=True))
    return dx, jnp.sum(dh * n, axis=0, keepdims=True)


def _row_tile(rows, cols, limit):
    tr = rows
    while tr * cols * 4 > limit and tr % 32 == 0:
        tr //= 2
    return tr


def _place():
    return lax.axis_index("x"), lax.axis_index("y"), lax.axis_index("c")


def _all_gather(shards, name):
    n = len(shards)

    def body(*refs):
        x_refs, out_refs = refs[:n], refs[n:2 * n]
        send_sems, recv_sems, local_sems = refs[2 * n:]
        x, y, c = _place()
        me, sibling = (x, y, c), (x, y, 1 - c)
        chips = [(1 - x, y), (x, 1 - y), (1 - x, 1 - y)]

        def copy(a, k, block, to, src=None):
            px, py, pc = block
            slot = out_refs[a].at[4 * px + 2 * py + pc]
            return pltpu.make_async_remote_copy(
                src_ref=slot if src is None else src, dst_ref=slot,
                send_sem=send_sems.at[7 * a + k], recv_sem=recv_sems.at[7 * a + k], device_id=to, device_id_type=MESH)

        mine, first, passed = [], [], []
        for a in range(n):
            cp = pltpu.make_async_copy(x_refs[a], out_refs[a].at[4 * x + 2 * y + c], local_sems.at[a])
            cp.start()
            mine.append(cp)
            cps = [copy(a, 0, me, sibling, src=x_refs[a])]
            cps += [copy(a, 1 + j, me, (*chip, c), src=x_refs[a]) for j, chip in enumerate(chips)]
            for cp in cps:
                cp.start()
            first += cps
        for a in range(n):
            for j, chip in enumerate(chips):
                copy(a, 1 + j, (*chip, c), me).wait_recv()
                fwd = copy(a, 4 + j, (*chip, c), sibling)
                fwd.start()
                passed.append(fwd)
        for a in range(n):
            copy(a, 0, sibling, me).wait_recv()
            for j, chip in enumerate(chips):
                copy(a, 4 + j, (*chip, 1 - c), me).wait_recv()
        for cp in first + passed:
            cp.wait_send()
        for cp in mine:
            cp.wait()

    anyspec = pl.BlockSpec(memory_space=pl.ANY)
    return pl.pallas_call(
        body, name=name,
        out_shape=[_sds((NDEV, *s.shape), s.dtype) for s in shards],
        in_specs=[anyspec] * n, out_specs=[anyspec] * n,
        scratch_shapes=[pltpu.SemaphoreType.DMA((7 * n,)), pltpu.SemaphoreType.DMA((7 * n,)),
                        pltpu.SemaphoreType.DMA((n,))],
    )(*shards)


_NEAR, _FAR = (0, 1, 2), (3,)


def _gather_first_copies(x_refs, out_refs, send_sems, recv_sems, local_sems, which=_NEAR + _FAR):
    x, y, c = _place()
    slot = 4 * x + 2 * y + c
    peers = [(x, y, 1 - c), (1 - x, y, c), (x, 1 - y, c), (1 - x, 1 - y, c)]
    local, remote = [], []
    for a, (x_ref, out_ref) in enumerate(zip(x_refs, out_refs)):
        if 0 in which:
            local.append(pltpu.make_async_copy(x_ref, out_ref.at[slot], local_sems.at[a]))
        for k in which:
            remote.append(pltpu.make_async_remote_copy(
                src_ref=x_ref, dst_ref=out_ref.at[slot], send_sem=send_sems.at[4 * a + k],
                recv_sem=recv_sems.at[4 * a + k], device_id=peers[k], device_id_type=MESH))
    return local, remote


def _gather_first_scratch(n):
    return [pltpu.SemaphoreType.DMA((4 * n,)), pltpu.SemaphoreType.DMA((4 * n,)), pltpu.SemaphoreType.DMA((n,))]


def _chip_swap_copies(p_refs, out_refs, send_sems, recv_sems):
    x, y, c = _place()
    chips = [(1 - x, y), (x, 1 - y), (1 - x, 1 - y)]
    cps = []
    for a, (p_ref, out_ref) in enumerate(zip(p_refs, out_refs)):
        for j, (px, py) in enumerate(chips):
            cps.append(pltpu.make_async_remote_copy(
                src_ref=p_ref.at[2 * px + py], dst_ref=out_ref.at[j],
                send_sem=send_sems.at[3 * a + j], recv_sem=recv_sems.at[3 * a + j],
                device_id=(px, py, c), device_id_type=MESH))
    return cps


def _sibling_swap_copies(a_refs, out_refs, send_sems, recv_sems):
    x, y, c = _place()
    cps = []
    for a, (a_ref, out_ref) in enumerate(zip(a_refs, out_refs)):
        for k in range(4):
            cps.append(pltpu.make_async_remote_copy(
                src_ref=a_ref.at[k, 1 - c], dst_ref=out_ref.at[k],
                send_sem=send_sems.at[4 * a + k], recv_sem=recv_sems.at[4 * a + k],
                device_id=(x, y, 1 - c), device_id_type=MESH))
    return cps


def _pass_on_copies(buf_refs, send_sems, recv_sems):
    x, y, c = _place()
    chips = [(1 - x, y), (x, 1 - y), (1 - x, 1 - y)]
    cps = []
    for a, buf in enumerate(buf_refs):
        for j, (px, py) in enumerate(chips):
            block = buf.at[4 * px + 2 * py + c]
            cps.append(pltpu.make_async_remote_copy(
                src_ref=block, dst_ref=block, send_sem=send_sems.at[3 * a + j], recv_sem=recv_sems.at[3 * a + j],
                device_id=(x, y, 1 - c), device_id_type=MESH))
    return cps


def _swap_sibling(arrs, name):
    n = len(arrs)

    def body(*refs):
        cps = _sibling_swap_copies(refs[:n], refs[n:2 * n], *refs[2 * n:])
        for cp in cps:
            cp.start()
        for cp in cps:
            cp.wait()

    anyspec = pl.BlockSpec(memory_space=pl.ANY)
    return pl.pallas_call(
        body, name=name,
        out_shape=[_sds((4, *a.shape[2:]), a.dtype) for a in arrs],
        in_specs=[anyspec] * n, out_specs=[anyspec] * n,
        scratch_shapes=[pltpu.SemaphoreType.DMA((4 * n,)), pltpu.SemaphoreType.DMA((4 * n,))],
    )(*arrs)


def _in_proj(x, g, wcat, shards):
    tm = 256
    n = len(shards)

    def body(x_ref, g_ref, w_ref, *rest):
        x_refs, (h_ref, z_ref, ft_ref), out_refs, sems = rest[:n], rest[n:n + 3], rest[n + 3:2 * n + 3], rest[2 * n + 3:]

        @pl.when(pl.program_id(0) == 0)
        def _():
            local, remote = _gather_first_copies(x_refs, out_refs, *sems)
            for cp in local + remote:
                cp.start()

        xv = x_ref[...]
        h = (xv * _rms_r(xv) * g_ref[...]).astype(BF16)
        h_ref[...] = h
        z = _nt(h, w_ref[...])
        z_ref[...] = z
        ft_ref[...] = z[:, 2048:ZC].T[0:NH, :]

        @pl.when(pl.program_id(0) == T // tm - 1)
        def _():
            local, remote = _gather_first_copies(x_refs, out_refs, *sems)
            for cp in remote + local:
                cp.wait()

    row = lambda i: (i, 0)
    fixed = lambda i: (0, 0)
    anyspec = pl.BlockSpec(memory_space=pl.ANY)
    res = pl.pallas_call(
        body, name="in_proj", grid=(T // tm,),
        in_specs=[pl.BlockSpec((tm, D), row), pl.BlockSpec((1, D), fixed), pl.BlockSpec((ZC, D), fixed)] + [anyspec] * n,
        out_specs=[pl.BlockSpec((tm, D), row), pl.BlockSpec((tm, ZC), row), pl.BlockSpec((NH, tm), lambda i: (0, i))]
        + [anyspec] * n,
        out_shape=[_sds((T, D), BF16), _sds((T, ZC)), _sds((NH, T))] + [_sds((NDEV, *s.shape), s.dtype) for s in shards],
        scratch_shapes=_gather_first_scratch(n),
        compiler_params=_cp(("arbitrary",)),
    )(x, g, wcat, *shards)
    return res[0], res[1], res[2], res[3:]


def _in_proj_dw(h, dq, dk, dv, du, df):
    tm = 512

    def body(h_ref, dq_ref, dk_ref, dv_ref, du_ref, df_ref, o_ref):
        hv = h_ref[...]
        for j, d_ref in enumerate((dq_ref, dk_ref, dv_ref, du_ref)):
            o_ref[512 * j:512 * (j + 1), :] = _tn(d_ref[...], hv)
        o_ref[2048:ZC, :] = _tn(df_ref[...], hv)

    full = lambda w: pl.BlockSpec((T, w), lambda i: (0, 0))
    return pl.pallas_call(
        body, name="in_proj_dw", grid=(D // tm,),
        in_specs=[pl.BlockSpec((T, tm), lambda i: (0, i)), full(512), full(512), full(512), full(512), full(LANES)],
        out_specs=pl.BlockSpec((ZC, tm), lambda i: (0, i)), out_shape=_sds((ZC, D)),
        compiler_params=_cp(("parallel",)),
    )(h, dq, dk, dv, du, df)


def _in_proj_bwd(dq, dk, dv, du, df, wcat, x, g, dres, parts):
    tm = 256
    n = len(parts)

    def body(dq_ref, dk_ref, dv_ref, du_ref, df_ref, w_ref, x_ref, g_ref, dres_ref, *rest):
        p_refs, (dx_ref, dg_ref), fc_refs, sems = rest[:n], rest[n:n + 2], rest[n + 2:2 * n + 2], rest[2 * n + 2:]

        @pl.when(pl.program_id(0) == 0)
        def _():
            for cp in _chip_swap_copies(p_refs, fc_refs, *sems):
                cp.start()

        dh = _nn(df_ref[...], w_ref[2048:ZC, :])
        for j, d_ref in enumerate((dq_ref, dk_ref, dv_ref, du_ref)):
            dh = dh + _nn(d_ref[...], w_ref[512 * j:512 * (j + 1), :])
        xv = x_ref[...]
        r = _rms_r(xv)
        dxr, dgp = _rms_bwd(xv * r, r, g_ref[...], dh)
        dx_ref[...] = dres_ref[...] + dxr

        @pl.when(pl.program_id(0) == 0)
        def _():
            dg_ref[...] = jnp.zeros_like(dg_ref)
        dg_ref[...] += dgp

        @pl.when(pl.program_id(0) == T // tm - 1)
        def _():
            for cp in _chip_swap_copies(p_refs, fc_refs, *sems):
                cp.wait()

    row = lambda i: (i, 0)
    fixed = lambda i: (0, 0)
    part = pl.BlockSpec((tm, 512), row)
    anyspec = pl.BlockSpec(memory_space=pl.ANY)
    res = pl.pallas_call(
        body, name="in_proj_bwd", grid=(T // tm,),
        in_specs=[part, part, part, part, pl.BlockSpec((tm, LANES), row), pl.BlockSpec((ZC, D), fixed),
                  pl.BlockSpec((tm, D), row), pl.BlockSpec((1, D), fixed), pl.BlockSpec((tm, D), row)] + [anyspec] * n,
        out_specs=[pl.BlockSpec((tm, D), row), pl.BlockSpec((1, D), fixed)] + [anyspec] * n,
        out_shape=[_sds((T, D)), _sds((1, D))] + [_sds((3, *p.shape[1:]), p.dtype) for p in parts],
        scratch_shapes=[pltpu.SemaphoreType.DMA((3 * n,)), pltpu.SemaphoreType.DMA((3 * n,))],
        compiler_params=_cp(("arbitrary",)),
    )(dq, dk, dv, du, df, wcat, x, g, dres, *parts)
    return res[0], res[1], res[2:]


def _stack_lanes(v):
    return jnp.concatenate([v[:, LANES * b:LANES * (b + 1)] for b in range(T // LANES)], axis=0)


def _unstack_lanes(s):
    return jnp.concatenate([s[8 * b:8 * b + 8, :] for b in range(T // LANES)], axis=1)


def _cumsum_lanes(v, reverse):
    st = _stack_lanes(v)
    ri = lax.broadcasted_iota(jnp.int32, (LANES, LANES), 0)
    ci = lax.broadcasted_iota(jnp.int32, (LANES, LANES), 1)
    tri = (ri >= ci) if reverse else (ri <= ci)
    intra = jnp.dot(st, tri.astype(F32), precision=HI, preferred_element_type=F32)
    tot = intra[:, 0:1] if reverse else intra[:, LANES - 1:LANES]
    same_head = (ci % 8) == (ri % 8)
    other = (ci // 8 > ri // 8) if reverse else (ci // 8 < ri // 8)
    off = jnp.dot((same_head & other).astype(F32), jnp.broadcast_to(tot, (LANES, LANES)), precision=HI,
                  preferred_element_type=F32)
    return _unstack_lanes(intra + off)


def _gates_fwd(f_t, b_f):
    def body(f_ref, b_ref, c_ref):
        z = f_ref[...] + b_ref[...]
        lf = jnp.minimum(z, 0.0) - jnp.log(1.0 + jnp.exp(-jnp.abs(z)))
        c_ref[...] = _cumsum_lanes(lf, reverse=False)

    return pl.pallas_call(body, name="gates_fwd", out_shape=_sds((NH, T)), compiler_params=_cp())(f_t, b_f)


def _gates_bwd(dck, f_t, b_f):
    def body(d_ref, f_ref, b_ref, df_ref, db_ref):
        z = f_ref[...] + b_ref[...]
        dlf = _cumsum_lanes(d_ref[...], reverse=True)
        df = dlf * jax.nn.sigmoid(-z)
        df_ref[...] = df
        db_ref[...] = jnp.sum(df, axis=1, keepdims=True)

    return pl.pallas_call(body, name="gates_bwd", out_shape=[_sds((NH, T)), _sds((NH, 1))],
                          compiler_params=_cp())(dck, f_t, b_f)


def _lower_triangle():
    rows = lax.broadcasted_iota(jnp.int32, (BQ, BQ), 0)
    cols = lax.broadcasted_iota(jnp.int32, (BQ, BQ), 1)
    return cols <= rows


def _logit_parts(qb, kb, ck_row, lo, hi, tri):
    parts = []
    if lo > 0:
        parts.append((_nt(qb[lo:hi], kb[0:lo]) - ck_row[:, 0:lo], 0, lo))
    parts.append((jnp.where(tri, _nt(qb[lo:hi], kb[lo:hi]) - ck_row[:, lo:hi], NEG), lo, hi))
    return parts


_HP = NH // 2


def _qkv_specs():
    return [pl.BlockSpec((T, LANES), lambda p: (0, p)), pl.BlockSpec((T, LANES), lambda p: (0, _HP + p)),
            pl.BlockSpec((T, LANES), lambda p: (0, 2 * _HP + p))]


def _attn_fwd(z, ck, g_q, g_k, shards, bufs):
    scale = DH ** -0.5
    n, m = len(shards), len(bufs)

    def body(q_ref, k_ref, v_ref, ck_ref, gq_ref, gk_ref, *rest):
        x_refs, (o_ref, lse_ref) = rest[:n], rest[n + m:n + m + 2]
        out_refs, buf_refs, sems = rest[n + m + 2:2 * n + m + 2], rest[2 * n + m + 2:2 * n + 2 * m + 2], rest[2 * n + 2 * m + 2:]

        @pl.when(pl.program_id(0) == 0)
        def _():
            local, remote = _gather_first_copies(x_refs, out_refs, *sems[:3], which=_NEAR)
            for cp in local + remote + _pass_on_copies(buf_refs, *sems[3:]):
                cp.start()

        qb, kb, vb = [], [], []
        for hh in range(2):
            cols = slice(DH * hh, DH * (hh + 1))
            qv, kv = q_ref[:, cols], k_ref[:, cols]
            qb.append((qv * _rms_r(qv) * gq_ref[...] * scale).astype(BF16))
            kb.append((kv * _rms_r(kv) * gk_ref[...]).astype(BF16))
            vb.append(v_ref[:, cols].astype(BF16))
        tri = _lower_triangle()
        for i in range(T // BQ):
            lo, hi = i * BQ, (i + 1) * BQ
            outs = []
            for hh in range(2):
                parts = _logit_parts(qb[hh], kb[hh], ck_ref[0, hh:hh + 1, :], lo, hi, tri)
                top = jnp.max(parts[-1][0], axis=-1, keepdims=True)
                if lo > 0:
                    top = jnp.maximum(top, jnp.max(parts[0][0], axis=-1, keepdims=True))
                l, o = 0.0, 0.0
                for s, k0, k1 in parts:
                    p = jnp.exp(s - top)
                    l = l + jnp.sum(p, axis=-1, keepdims=True)
                    o = o + _nn(p.astype(BF16), vb[hh][k0:k1])
                outs.append(o / l)
                lse_ref[0, lo:hi, hh:hh + 1] = top + jnp.log(l)
            o_ref[lo:hi, :] = jnp.concatenate(outs, axis=1)

        @pl.when(pl.program_id(0) == _HP - 1)
        def _():
            local, remote = _gather_first_copies(x_refs, out_refs, *sems[:3], which=_NEAR)
            for cp in remote + local + _pass_on_copies(buf_refs, *sems[3:]):
                cp.wait()

    fixed = lambda p: (0, 0)
    anyspec = pl.BlockSpec(memory_space=pl.ANY)
    res = pl.pallas_call(
        body, name="attn_fwd", grid=(_HP,),
        in_specs=_qkv_specs() + [pl.BlockSpec((1, 2, T), lambda p: (p, 0, 0)), pl.BlockSpec((1, DH), fixed),
                                 pl.BlockSpec((1, DH), fixed)] + [anyspec] * (n + m),
        out_specs=[pl.BlockSpec((T, LANES), lambda p: (0, p)), pl.BlockSpec((1, T, 2), lambda p: (p, 0, 0))]
        + [anyspec] * (n + m),
        out_shape=[_sds((T, AW)), _sds((_HP, T, 2))] + [_sds((NDEV, *s.shape), s.dtype) for s in shards]
        + [_sds(b.shape, b.dtype) for b in bufs],
        input_output_aliases={6 + n + i: 2 + n + i for i in range(m)},
        scratch_shapes=_gather_first_scratch(n) + [pltpu.SemaphoreType.DMA((3 * m,)), pltpu.SemaphoreType.DMA((3 * m,))],
        compiler_params=_cp(("arbitrary",)),
    )(z, z, z, ck, g_q, g_k, *shards, *bufs)
    return res[0], res[1], res[2:2 + n], res[2 + n:]


def _attn_bwd(z, ck, g_q, g_k, lse, datt, parts):
    scale = DH ** -0.5
    n = len(parts)

    def body(q_ref, k_ref, v_ref, ck_ref, gq_ref, gk_ref, lse_ref, do_ref, *rest):
        p_refs, rest = rest[:n], rest[n:]
        dq_ref, dk_ref, dv_ref, dck_ref, dgq_ref, dgk_ref = rest[:6]
        fc_refs, (dkn_acc, dv_acc, dc_acc), sems = rest[6:6 + n], rest[6 + n:9 + n], rest[9 + n:]

        @pl.when(pl.program_id(0) == 0)
        def _():
            for cp in _chip_swap_copies(p_refs, fc_refs, *sems):
                cp.start()

        nq, nk, rq, rk, qb, kb, vb, dob = [], [], [], [], [], [], [], []
        for hh in range(2):
            cols = slice(DH * hh, DH * (hh + 1))
            qv, kv = q_ref[:, cols], k_ref[:, cols]
            rq.append(_rms_r(qv))
            rk.append(_rms_r(kv))
            nq.append(qv * rq[hh])
            nk.append(kv * rk[hh])
            qb.append((nq[hh] * gq_ref[...] * scale).astype(BF16))
            kb.append((nk[hh] * gk_ref[...]).astype(BF16))
            vb.append(v_ref[:, cols].astype(BF16))
            dob.append(do_ref[:, cols].astype(BF16))
        dkn_acc[...] = jnp.zeros_like(dkn_acc)
        dv_acc[...] = jnp.zeros_like(dv_acc)
        dc_acc[...] = jnp.zeros_like(dc_acc)
        dgq = jnp.zeros((1, DH), F32)
        tri = _lower_triangle()
        for i in range(T // BQ):
            lo, hi = i * BQ, (i + 1) * BQ
            dqs = []
            for hh in range(2):
                lse = lse_ref[0, lo:hi, hh:hh + 1]
                pd = []
                for s, k0, k1 in _logit_parts(qb[hh], kb[hh], ck_ref[0, hh:hh + 1, :], lo, hi, tri):
                    pd.append((jnp.exp(s - lse), _nt(dob[hh][lo:hi], vb[hh][k0:k1]), k0, k1))
                mean = sum(jnp.sum(p * dp, axis=-1, keepdims=True) for p, dp, _, _ in pd)
                dqn = 0.0
                for p, dp, k0, k1 in pd:
                    ds = p * (dp - mean)
                    dsb = ds.astype(BF16)
                    dv_acc[hh, k0:k1, :] += _tn(p.astype(BF16), dob[hh][lo:hi])
                    dkn_acc[hh, k0:k1, :] += _tn(dsb, qb[hh][lo:hi])
                    dc_acc[hh:hh + 1, k0:k1] -= jnp.sum(ds, axis=0, keepdims=True)
                    dqn = dqn + _nn(dsb, kb[hh][k0:k1])
                dqx, dgp = _rms_bwd(nq[hh][lo:hi], rq[hh][lo:hi], gq_ref[...], dqn * scale)
                dqs.append(dqx)
                dgq = dgq + dgp
            dq_ref[lo:hi, :] = jnp.concatenate(dqs, axis=1).astype(BF16)
        dks, dgk = [], jnp.zeros((1, DH), F32)
        for hh in range(2):
            dkx, dgp = _rms_bwd(nk[hh], rk[hh], gk_ref[...], dkn_acc[hh])
            dks.append(dkx)
            dgk = dgk + dgp
        dk_ref[...] = jnp.concatenate(dks, axis=1).astype(BF16)
        dv_ref[...] = jnp.concatenate([dv_acc[0], dv_acc[1]], axis=1).astype(BF16)
        dck_ref[0] = dc_acc[...]

        @pl.when(pl.program_id(0) == 0)
        def _():
            dgq_ref[...] = jnp.zeros_like(dgq_ref)
            dgk_ref[...] = jnp.zeros_like(dgk_ref)
        dgq_ref[...] += dgq
        dgk_ref[...] += dgk

        @pl.when(pl.program_id(0) == _HP - 1)
        def _():
            for cp in _chip_swap_copies(p_refs, fc_refs, *sems):
                cp.wait()

    fixed = lambda p: (0, 0)
    pair = pl.BlockSpec((T, LANES), lambda p: (0, p))
    gsp = pl.BlockSpec((1, DH), fixed)
    ckspec = pl.BlockSpec((1, 2, T), lambda p: (p, 0, 0))
    anyspec = pl.BlockSpec(memory_space=pl.ANY)
    res = pl.pallas_call(
        body, name="attn_bwd", grid=(_HP,),
        in_specs=_qkv_specs() + [ckspec, gsp, gsp, pl.BlockSpec((1, T, 2), lambda p: (p, 0, 0)), pair] + [anyspec] * n,
        out_specs=[pair, pair, pair, ckspec, gsp, gsp] + [anyspec] * n,
        out_shape=[_sds((T, AW), BF16)] * 3 + [_sds((_HP, 2, T)), _sds((1, DH)), _sds((1, DH))]
        + [_sds((3, *p.shape[1:]), p.dtype) for p in parts],
        scratch_shapes=[pltpu.VMEM((2, T, DH), F32), pltpu.VMEM((2, T, DH), F32), pltpu.VMEM((2, T), F32),
                        pltpu.SemaphoreType.DMA((3 * n,)), pltpu.SemaphoreType.DMA((3 * n,))],
        compiler_params=_cp(("arbitrary",)),
    )(z, z, z, ck, g_q, g_k, lse, datt, *parts)
    return (*res[:6], res[6:])


def _disc(lr, li, ls, br_t, bi_t):
    step = jnp.exp(ls)
    er = jnp.exp(lr * step)
    ab_re = er * jnp.cos(li * step)
    ab_im = er * jnp.sin(li * step)
    num_re = ab_re - 1.0
    num_im = ab_im
    den = lr * lr + li * li
    f_re = (num_re * lr + num_im * li) / den
    f_im = (num_im * lr - num_re * li) / den
    bb_re = f_re[:, None, :] * br_t - f_im[:, None, :] * bi_t
    bb_im = f_re[:, None, :] * bi_t + f_im[:, None, :] * br_t
    return ab_re, ab_im, bb_re, bb_im


def _disc_fwd(lr, li, ls, br_t, bi_t):
    def body(lr_ref, li_ref, ls_ref, br_ref, bi_ref, ar_ref, ai_ref, bbr_ref, bbi_ref):
        ar, ai, bbr, bbi = _disc(lr_ref[...], li_ref[...], ls_ref[...], br_ref[...], bi_ref[...])
        ar_ref[...] = ar
        ai_ref[...] = ai
        bbr_ref[...] = bbr
        bbi_ref[...] = bbi

    return pl.pallas_call(
        body, name="ssm_disc_fwd",
        out_shape=[_sds((NG, NP)), _sds((NG, NP)), _sds((NG, GS, NP)), _sds((NG, GS, NP))],
        compiler_params=_cp())(lr, li, ls, br_t, bi_t)


def _disc_bwd(lr, li, ls, br_t, bi_t, dar, dai, dbbr, dbbi):
    def body(lr_ref, li_ref, ls_ref, br_ref, bi_ref, dar_ref, dai_ref, dbbr_ref, dbbi_ref,
             dlr_ref, dli_ref, dls_ref, dbr_ref, dbi_ref):
        _, vjp = jax.vjp(_disc, lr_ref[...], li_ref[...], ls_ref[...], br_ref[...], bi_ref[...])
        dlr, dli, dls, dbr, dbi = vjp((dar_ref[...], dai_ref[...], dbbr_ref[...], dbbi_ref[...]))
        dlr_ref[...] = dlr
        dli_ref[...] = dli
        dls_ref[...] = dls
        dbr_ref[...] = dbr
        dbi_ref[...] = dbi

    return pl.pallas_call(
        body, name="ssm_disc_bwd",
        out_shape=[_sds((NG, NP)), _sds((NG, NP)), _sds((NG, 1)), _sds((NG, GS, NP)), _sds((NG, GS, NP))],
        compiler_params=_cp())(lr, li, ls, br_t, bi_t, dar, dai, dbbr, dbbi)


def _cmul(ar, ai, br, bi):
    return ar * br - ai * bi, ar * bi + ai * br


def _scan(buf, a_re, a_im, reverse, other=None):
    ar = [jnp.broadcast_to(a_re[:, LANES * k:LANES * (k + 1)], (NSEG, LANES)) for k in range(4)]
    ai = [jnp.broadcast_to(a_im[:, LANES * k:LANES * (k + 1)], (NSEG, LANES)) for k in range(4)]

    def tile(ref, i, k):
        return ref[pl.ds(pl.multiple_of(i * NSEG, NSEG), NSEG), LANES * k:LANES * (k + 1)]

    def advance(i, xr, xi):
        nr = [ar[k] * xr[k] - ai[k] * xi[k] + tile(buf, i, k) for k in range(4)]
        ni = [ar[k] * xi[k] + ai[k] * xr[k] + tile(buf, i, 4 + k) for k in range(4)]
        return nr, ni

    def index(i):
        return (SEG - 1 - i) if reverse else i

    zeros = tuple(jnp.zeros((NSEG, LANES), F32) for _ in range(4))

    def sweep1(i, carry):
        xr, xi = carry
        for sub in range(_SCAN_UNROLL):
            xr, xi = advance(index(i * _SCAN_UNROLL + sub), xr, xi)
        return tuple(xr), tuple(xi)

    er, ei = lax.fori_loop(0, SEG // _SCAN_UNROLL, sweep1, (zeros, zeros))

    row = lax.broadcasted_iota(jnp.int32, (NSEG, LANES), 0)
    sr, si = [], []
    for k in range(4):
        pr, pi = ar[k], ai[k]
        for _ in range(SEG.bit_length() - 1):
            pr, pi = _cmul(pr, pi, pr, pi)
        ir, ii = er[k], ei[k]
        for d in (1, 2, 4):
            shift = (NSEG - d) if reverse else d
            ok = (row < NSEG - d) if reverse else (row >= d)
            mr, mi = _cmul(pr, pi, pltpu.roll(ir, shift, 0), pltpu.roll(ii, shift, 0))
            ir = ir + jnp.where(ok, mr, 0.0)
            ii = ii + jnp.where(ok, mi, 0.0)
            pr, pi = _cmul(pr, pi, pr, pi)
        shift = (NSEG - 1) if reverse else 1
        ok = (row < NSEG - 1) if reverse else (row >= 1)
        sr.append(jnp.where(ok, pltpu.roll(ir, shift, 0), 0.0))
        si.append(jnp.where(ok, pltpu.roll(ii, shift, 0), 0.0))

    def sweep2(i, carry):
        xr, xi, dar, dai = carry
        for sub in range(_SCAN_UNROLL):
            idx = index(i * _SCAN_UNROLL + sub)
            if other is not None:
                orr = [tile(other, idx, k) for k in range(4)]
                oi = [tile(other, idx, 4 + k) for k in range(4)]
                dar = tuple(dar[k] + xr[k] * orr[k] + xi[k] * oi[k] for k in range(4))
                dai = tuple(dai[k] + xi[k] * orr[k] - xr[k] * oi[k] for k in range(4))
            xr, xi = advance(idx, xr, xi)
            start = pl.multiple_of(idx * NSEG, NSEG)
            for k in range(4):
                buf[pl.ds(start, NSEG), LANES * k:LANES * (k + 1)] = xr[k]
                buf[pl.ds(start, NSEG), LANES * (4 + k):LANES * (5 + k)] = xi[k]
        return tuple(xr), tuple(xi), dar, dai

    _, _, dar, dai = lax.fori_loop(0, SEG // _SCAN_UNROLL, sweep2, (tuple(sr), tuple(si), zeros, zeros))
    if other is None:
        return None
    da_re = jnp.concatenate([jnp.sum(d, axis=0, keepdims=True) for d in dar], axis=1)
    da_im = jnp.concatenate([jnp.sum(d, axis=0, keepdims=True) for d in dai], axis=1)
    return da_re, da_im


_SSM_BLOCKS = 4


def _seg_copies(hbm_ref, col0, buf, block, sems, to_hbm):
    cps = []
    for j in range(NSEG):
        hbm = hbm_ref.at[pl.ds(SEG * j, SEG), pl.ds(col0, LANES)]
        vmem = buf.at[block, :, j, :]
        cps.append(pltpu.make_async_copy(vmem, hbm, sems.at[block, j]) if to_hbm
                   else pltpu.make_async_copy(hbm, vmem, sems.at[block, j]))
    return cps


def _seg_scratch():
    return [pltpu.VMEM((_SSM_BLOCKS, SEG, NSEG, LANES), F32), pltpu.SemaphoreType.DMA((_SSM_BLOCKS, NSEG))]


_U_COL = 3 * AW


def _ssm_fwd(z, bb, ab, cc, dsk, shards, bufs):
    n = len(shards)
    last = _SSM_BLOCKS - 1

    def body(z_ref, bb_ref, a_ref, cc_ref, d_ref, *rest):
        x_refs, y_ref, xs_ref, out_refs = rest[:n], rest[2 * n], rest[2 * n + 1], rest[2 * n + 2:3 * n + 2]
        ubuf, lsem, ybuf, ssem = rest[3 * n + 2:3 * n + 6]
        sems = rest[3 * n + 6:]
        step = pl.program_id(0)

        @pl.when(step == 0)
        def _():
            for cp in _gather_first_copies(x_refs, out_refs, *sems, which=_FAR)[1]:
                cp.start()
            for b in range(_SSM_BLOCKS):
                for cp in _seg_copies(z_ref, _U_COL + LANES * b, ubuf, b, lsem, False):
                    cp.start()

        for cp in _seg_copies(z_ref, _U_COL, ubuf, step, lsem, False):
            cp.wait()
        ub = ubuf[step].reshape(T, LANES)
        xb = xs_ref.at[0]
        xb[...] = _nn(ub.astype(BF16), bb_ref[0])
        _scan(xb, a_ref[0, 0:1, :], a_ref[0, 1:2, :], reverse=False)
        y = _nn(xb[...].astype(BF16), cc_ref[0]) + d_ref[0] * ub
        ybuf[step] = y.reshape(SEG, NSEG, LANES)
        for cp in _seg_copies(y_ref, pl.multiple_of(step * LANES, LANES), ybuf, step, ssem, True):
            cp.start()

        @pl.when(step == last)
        def _():
            for b in range(_SSM_BLOCKS):
                for cp in _seg_copies(y_ref, LANES * b, ybuf, b, ssem, True):
                    cp.wait()
            for cp in _gather_first_copies(x_refs, out_refs, *sems, which=_FAR)[1]:
                cp.wait()

    blk = lambda j: (j, 0, 0)
    anyspec = pl.BlockSpec(memory_space=pl.ANY)
    res = pl.pallas_call(
        body, name="ssm_fwd", grid=(_SSM_BLOCKS,),
        in_specs=[anyspec, pl.BlockSpec((1, LANES, 1024), blk), pl.BlockSpec((1, 2, 512), blk),
                  pl.BlockSpec((1, 1024, LANES), blk), pl.BlockSpec((1, 1, LANES), blk)] + [anyspec] * (2 * n),
        out_specs=[anyspec, pl.BlockSpec((1, T, 1024), blk)] + [anyspec] * n,
        out_shape=[_sds((T, SW)), _sds((_SSM_BLOCKS, T, 1024))] + [_sds(b.shape, b.dtype) for b in bufs],
        input_output_aliases={5 + n + i: 2 + i for i in range(n)},
        scratch_shapes=_seg_scratch() + _seg_scratch() + _gather_first_scratch(n),
        compiler_params=_cp(("arbitrary",)),
    )(z, bb, ab, cc, dsk, *shards, *bufs)
    return res[0], res[1], res[2:]


def _ssm_bwd(z, dy, xs, bb, ab, cc, dsk, arrs, parts):
    n, m = len(arrs), len(parts)
    last = _SSM_BLOCKS - 1

    def body(z_ref, dy_ref, x_ref, bb_ref, a_ref, cc_ref, d_ref, *rest):
        a_refs, p_refs, rest = rest[:n], rest[n:n + m], rest[n + m:]
        du_ref, dbb_ref, da_ref, dcc_ref, dd_ref = rest[:5]
        fs_refs, fc_refs, gb = rest[5:5 + n], rest[5 + n:5 + n + m], rest[5 + n + m]
        ubuf, usem, dybuf, dysem, dubuf, dusem = rest[6 + n + m:12 + n + m]
        sems = rest[12 + n + m:]

        def carried():
            return _sibling_swap_copies(a_refs, fs_refs, *sems[:2]) + _chip_swap_copies(p_refs, fc_refs, *sems[2:])

        step = pl.program_id(0)

        @pl.when(step == 0)
        def _():
            for cp in carried():
                cp.start()
            for b in range(_SSM_BLOCKS):
                for cp in (_seg_copies(z_ref, _U_COL + LANES * b, ubuf, b, usem, False)
                           + _seg_copies(dy_ref, LANES * b, dybuf, b, dysem, False)):
                    cp.start()

        for cp in _seg_copies(z_ref, _U_COL, ubuf, step, usem, False) + _seg_copies(dy_ref, 0, dybuf, step, dysem, False):
            cp.wait()
        ub, dyv = ubuf[step].reshape(T, LANES), dybuf[step].reshape(T, LANES)
        ubb, dyb = ub.astype(BF16), dyv.astype(BF16)
        a_re, a_im = a_ref[0, 0:1, :], a_ref[0, 1:2, :]
        dcc_ref[0] = _tn(x_ref[0].astype(BF16), dyb)
        gb[...] = _nt(dyb, cc_ref[0])
        da_re, da_im = _scan(gb, a_re, -a_im, reverse=True, other=x_ref.at[0])
        da_ref[0] = jnp.concatenate([da_re, da_im], axis=0)
        gc = gb[...].astype(BF16)
        du = _nt(gc, bb_ref[0]) + d_ref[0] * dyv
        dubuf[step] = du.reshape(SEG, NSEG, LANES)
        for cp in _seg_copies(du_ref, pl.multiple_of(step * LANES, LANES), dubuf, step, dusem, True):
            cp.start()
        dbb_ref[0] = _tn(ubb, gc)
        dd_ref[0] = jnp.sum(dyv * ub, axis=0, keepdims=True)

        @pl.when(step == last)
        def _():
            for b in range(_SSM_BLOCKS):
                for cp in _seg_copies(du_ref, LANES * b, dubuf, b, dusem, True):
                    cp.wait()
            for cp in carried():
                cp.wait()

    blk = lambda j: (j, 0, 0)
    anyspec = pl.BlockSpec(memory_space=pl.ANY)
    res = pl.pallas_call(
        body, name="ssm_bwd", grid=(_SSM_BLOCKS,),
        in_specs=[anyspec, anyspec, pl.BlockSpec((1, T, 1024), blk), pl.BlockSpec((1, LANES, 1024), blk),
                  pl.BlockSpec((1, 2, 512), blk), pl.BlockSpec((1, 1024, LANES), blk), pl.BlockSpec((1, 1, LANES), blk)]
        + [anyspec] * (n + m),
        out_specs=[anyspec, pl.BlockSpec((1, LANES, 1024), blk), pl.BlockSpec((1, 2, 512), blk),
                   pl.BlockSpec((1, 1024, LANES), blk), pl.BlockSpec((1, 1, LANES), blk)] + [anyspec] * (n + m),
        out_shape=[_sds((T, SW)), _sds((4, LANES, 1024)), _sds((4, 2, 512)), _sds((4, 1024, LANES)), _sds((4, 1, LANES))]
        + [_sds((4, *a.shape[2:]), a.dtype) for a in arrs] + [_sds((3, *p.shape[1:]), p.dtype) for p in parts],
        scratch_shapes=[pltpu.VMEM((T, 1024), F32)] + _seg_scratch() + _seg_scratch() + _seg_scratch()
        + [pltpu.SemaphoreType.DMA((4 * n,)), pltpu.SemaphoreType.DMA((4 * n,)),
           pltpu.SemaphoreType.DMA((3 * m,)), pltpu.SemaphoreType.DMA((3 * m,))],
        compiler_params=_cp(("arbitrary",)),
    )(z, dy, xs, bb, ab, cc, dsk, *arrs, *parts)
    return (*res[:5], res[5:5 + n], res[5 + n:])


_TM_MIX = 512


def _mix_parts(att, y, wg, bg):
    y2 = jax.nn.gelu(y)
    y2b = y2.astype(BF16)
    sg = jax.nn.sigmoid(_nn(y2b, wg) + bg)
    ssm = y2 * sg
    ra, rs = _rms_r(att), _rms_r(ssm)
    return y2, y2b, sg, ssm, ra, rs


def _mix_fwd(att, y, x, wg, bg, ga, gs, wo, bufs):
    m = len(bufs)

    def body(att_ref, y_ref, x_ref, wg_ref, bg_ref, ga_ref, gs_ref, wo_ref, *rest):
        x1_ref, buf_refs, sems = rest[m], rest[m + 1:2 * m + 1], rest[2 * m + 1:]

        @pl.when(pl.program_id(0) == 0)
        def _():
            for cp in _pass_on_copies(buf_refs, *sems):
                cp.start()

        attv = att_ref[...]
        _, _, _, ssm, ra, rs = _mix_parts(attv, y_ref[...], wg_ref[...], bg_ref[...])
        ma = (attv * ra * ga_ref[...]).astype(BF16)
        ms = (ssm * rs * gs_ref[...]).astype(BF16)
        x1_ref[...] = x_ref[...] + _nn(ma, wo_ref[0:AW, :]) + _nn(ms, wo_ref[AW:D, :])

        @pl.when(pl.program_id(0) == T // _TM_MIX - 1)
        def _():
            for cp in _pass_on_copies(buf_refs, *sems):
                cp.wait()

    row = lambda i: (i, 0)
    fixed = lambda i: (0, 0)
    half = pl.BlockSpec((_TM_MIX, AW), row)
    vec = pl.BlockSpec((1, AW), fixed)
    anyspec = pl.BlockSpec(memory_space=pl.ANY)
    res = pl.pallas_call(
        body, name="mix_fwd", grid=(T // _TM_MIX,),
        in_specs=[half, half, pl.BlockSpec((_TM_MIX, D), row), pl.BlockSpec((SW, SW), fixed), vec, vec, vec,
                  pl.BlockSpec((D, D), fixed)] + [anyspec] * m,
        out_specs=[pl.BlockSpec((_TM_MIX, D), row)] + [anyspec] * m,
        out_shape=[_sds((T, D))] + [_sds(b.shape, b.dtype) for b in bufs],
        input_output_aliases={8 + i: 1 + i for i in range(m)},
        scratch_shapes=[pltpu.SemaphoreType.DMA((3 * m,)), pltpu.SemaphoreType.DMA((3 * m,))],
        compiler_params=_cp(("arbitrary",)),
    )(att, y, x, wg, bg, ga, gs, wo, *bufs)
    return res[0], res[1:]


def _mix_bwd(att, y, dx1, wg, bg, ga, gs, wo):
    last = T // _TM_MIX - 1

    def body(att_ref, y_ref, d_ref, wg_ref, bg_ref, ga_ref, gs_ref, wo_ref,
             datt_ref, dy_ref, dwg_ref, dbg_ref, dga_ref, dgs_ref, dwo_ref, dwg_acc, dwo_acc):
        attv, yv, db = att_ref[...], y_ref[...], d_ref[...].astype(BF16)
        y2, y2b, sg, ssm, ra, rs = _mix_parts(attv, yv, wg_ref[...], bg_ref[...])
        na, ns = attv * ra, ssm * rs
        ma = (na * ga_ref[...]).astype(BF16)
        ms = (ns * gs_ref[...]).astype(BF16)
        dma = _nt(db, wo_ref[0:AW, :])
        dms = _nt(db, wo_ref[AW:D, :])
        datt, dga = _rms_bwd(na, ra, ga_ref[...], dma)
        dssm, dgs = _rms_bwd(ns, rs, gs_ref[...], dms)
        dgl = dssm * y2 * sg * (1.0 - sg)
        dglb = dgl.astype(BF16)
        dy2 = dssm * sg + _nt(dglb, wg_ref[...])
        _, gelu_vjp = jax.vjp(jax.nn.gelu, yv)
        datt_ref[...] = datt
        dy_ref[...] = gelu_vjp(dy2)[0]

        @pl.when(pl.program_id(0) == 0)
        def _():
            dwg_acc[...] = jnp.zeros_like(dwg_acc)
            dwo_acc[...] = jnp.zeros_like(dwo_acc)
            dbg_ref[...] = jnp.zeros_like(dbg_ref)
            dga_ref[...] = jnp.zeros_like(dga_ref)
            dgs_ref[...] = jnp.zeros_like(dgs_ref)
        dwg_acc[...] += _tn(y2b, dglb)
        dbg_ref[...] += jnp.sum(dgl, axis=0, keepdims=True)
        dga_ref[...] += dga
        dgs_ref[...] += dgs
        dwo_acc[0:AW, :] += _tn(ma, db)
        dwo_acc[AW:D, :] += _tn(ms, db)

        @pl.when(pl.program_id(0) == last)
        def _():
            dwg_ref[...] = dwg_acc[...].astype(BF16)
            dwo_ref[...] = dwo_acc[...].astype(BF16)

    row = lambda i: (i, 0)
    fixed = lambda i: (0, 0)
    half = pl.BlockSpec((_TM_MIX, AW), row)
    vec = pl.BlockSpec((1, AW), fixed)
    sq = pl.BlockSpec((SW, SW), fixed)
    big = pl.BlockSpec((D, D), fixed)
    return pl.pallas_call(
        body, name="mix_bwd", grid=(T // _TM_MIX,),
        in_specs=[half, half, pl.BlockSpec((_TM_MIX, D), row), sq, vec, vec, vec, big],
        out_specs=[half, half, sq, vec, vec, vec, big],
        out_shape=[_sds((T, AW)), _sds((T, SW)), _sds((SW, SW), BF16), _sds((1, SW)), _sds((1, AW)), _sds((1, SW)),
                   _sds((D, D), BF16)],
        scratch_shapes=[pltpu.VMEM((SW, SW), F32), pltpu.VMEM((D, D), F32)],
        compiler_params=_cp(("arbitrary",)),
    )(att, y, dx1, wg, bg, ga, gs, wo)


_NCB = -(-CS // LANES)


def _ffn_up(x1, g, wu4, shards):
    tm = 1024
    n = len(shards)
    steps = T // tm

    def body(x_ref, g_ref, w_ref, *rest):
        x_refs, (h_ref, up_ref), out_refs, sems = rest[:n], rest[n:n + 2], rest[n + 2:2 * n + 2], rest[2 * n + 2:]
        i, e = pl.program_id(0), pl.program_id(1)

        @pl.when((i == 0) & (e == 0))
        def _():
            local, remote = _gather_first_copies(x_refs, out_refs, *sems)
            for cp in local + remote:
                cp.start()

        @pl.when(e == 0)
        def _():
            xv = x_ref[...]
            h_ref[...] = (xv * _rms_r(xv) * g_ref[...]).astype(BF16)
        up_ref[0, 0] = _nt(h_ref[...], w_ref[0, 0])

        @pl.when((i == steps - 1) & (e == NDEV - 1))
        def _():
            local, remote = _gather_first_copies(x_refs, out_refs, *sems)
            for cp in remote + local:
                cp.wait()

    anyspec = pl.BlockSpec(memory_space=pl.ANY)
    res = pl.pallas_call(
        body, name="ffn_up", grid=(steps, NDEV),
        in_specs=[pl.BlockSpec((tm, D), lambda i, e: (i, 0)), pl.BlockSpec((1, D), lambda i, e: (0, 0)),
                  pl.BlockSpec((1, 1, CS, D), lambda i, e: (e // 4, e % 4, 0, 0))] + [anyspec] * n,
        out_specs=[pl.BlockSpec((tm, D), lambda i, e: (i, 0)),
                   pl.BlockSpec((1, 1, tm, CS), lambda i, e: (e // 4, e % 4, i, 0))] + [anyspec] * n,
        out_shape=[_sds((T, D), BF16), _sds((2, 4, T, CS))] + [_sds((NDEV, *s.shape), s.dtype) for s in shards],
        scratch_shapes=_gather_first_scratch(n),
        compiler_params=_cp(("arbitrary", "arbitrary")),
    )(x1, g, wu4, *shards)
    return res[0], res[1], res[2:]


def _shift_down(h, k):
    row = lax.broadcasted_iota(jnp.int32, h.shape, 0)
    return jnp.where(row >= k, pltpu.roll(h, k, 0), 0.0)


def _shift_up(h, k):
    row = lax.broadcasted_iota(jnp.int32, h.shape, 0)
    return jnp.where(row < T - k, pltpu.roll(h, T - k, 0), 0.0)


def _conv(h, w, b):
    return w[0:1, :] * _shift_down(h, 2) + w[1:2, :] * _shift_down(h, 1) + w[2:3, :] * h + b


def _chan(gv, rows):
    return pl.BlockSpec((1, 1, rows, LANES), lambda d, j: (gv, d, 0, j))


def _ffn_act(up4, cw4, cb4, bufs):
    m = len(bufs)

    def body(ug_ref, uv_ref, wg_ref, wv_ref, bg_ref, bv_ref, *rest):
        act_ref, buf_refs, sems = rest[m], rest[m + 1:2 * m + 1], rest[2 * m + 1:]
        d, j = pl.program_id(0), pl.program_id(1)

        @pl.when((d == 0) & (j == 0))
        def _():
            for cp in _pass_on_copies(buf_refs, *sems):
                cp.start()

        g = _conv(ug_ref[0, 0], wg_ref[0, 0], bg_ref[0, 0])
        v = _conv(uv_ref[0, 0], wv_ref[0, 0], bv_ref[0, 0])
        act_ref[0] = (g * jax.nn.sigmoid(g) * v).astype(BF16)

        @pl.when((d == 3) & (j == _NCB - 1))
        def _():
            for cp in _pass_on_copies(buf_refs, *sems):
                cp.wait()

    anyspec = pl.BlockSpec(memory_space=pl.ANY)
    res = pl.pallas_call(
        body, name="ffn_act", grid=(4, _NCB),
        in_specs=[_chan(0, T), _chan(1, T), _chan(0, 3), _chan(1, 3), _chan(0, 1), _chan(1, 1)] + [anyspec] * m,
        out_specs=[pl.BlockSpec((1, T, LANES), lambda d, j: (d, 0, j))] + [anyspec] * m,
        out_shape=[_sds((4, T, CS), BF16)] + [_sds(b.shape, b.dtype) for b in bufs],
        input_output_aliases={6 + i: 1 + i for i in range(m)},
        scratch_shapes=[pltpu.SemaphoreType.DMA((3 * m,)), pltpu.SemaphoreType.DMA((3 * m,))],
        compiler_params=_cp(("arbitrary", "arbitrary")),
    )(up4, up4, cw4, cw4, cb4, cb4, *bufs)
    return res[0], res[1:]


def _ffn_act_bwd(up4, dact, cw4, cb4):
    def conv_bwd(h, w, d):
        dup = w[2:3, :] * d + w[1:2, :] * _shift_up(d, 1) + w[0:1, :] * _shift_up(d, 2)
        dw = jnp.concatenate([jnp.sum(d * _shift_down(h, 2), axis=0, keepdims=True),
                              jnp.sum(d * _shift_down(h, 1), axis=0, keepdims=True),
                              jnp.sum(d * h, axis=0, keepdims=True)], axis=0)
        return dup, dw, jnp.sum(d, axis=0, keepdims=True)

    def body(ug_ref, uv_ref, da_ref, wg_ref, wv_ref, bg_ref, bv_ref, dup_ref, dw_ref, db_ref):
        ug, uv, da = ug_ref[0, 0], uv_ref[0, 0], da_ref[0]
        g = _conv(ug, wg_ref[0, 0], bg_ref[0, 0])
        v = _conv(uv, wv_ref[0, 0], bv_ref[0, 0])
        sg = jax.nn.sigmoid(g)
        dg = da * v * (sg * (1.0 + g * (1.0 - sg)))
        dv = da * (g * sg)
        dug, dwg, dbg = conv_bwd(ug, wg_ref[0, 0], dg)
        duv, dwv, dbv = conv_bwd(uv, wv_ref[0, 0], dv)
        dup_ref[0, 0] = dug.astype(BF16)
        dup_ref[1, 0] = duv.astype(BF16)
        dw_ref[0, 0] = dwg
        dw_ref[1, 0] = dwv
        db_ref[0, 0] = dbg
        db_ref[1, 0] = dbv

    both = lambda rows: pl.BlockSpec((2, 1, rows, LANES), lambda d, j: (0, d, 0, j))
    return pl.pallas_call(
        body, name="ffn_act_bwd", grid=(4, _NCB),
        in_specs=[_chan(0, T), _chan(1, T), pl.BlockSpec((1, T, LANES), lambda d, j: (d, 0, j)),
                  _chan(0, 3), _chan(1, 3), _chan(0, 1), _chan(1, 1)],
        out_specs=[both(T), both(3), both(1)],
        out_shape=[_sds((2, 4, T, CS), BF16), _sds((2, 4, 3, CS)), _sds((2, 4, 1, CS))],
        compiler_params=_cp(("parallel", "parallel")),
    )(up4, up4, dact, cw4, cw4, cb4, cb4)


def _ffn_down_loss(act, wd4, x1, target):
    tm = 512

    def body(a_ref, w_ref, x1_ref, t_ref, loss_ref, dx_ref, dxb_ref):
        x2 = x1_ref[...]
        for d in range(4):
            x2 = x2 + _nn(a_ref[d], w_ref[d])
        err = x2 - t_ref[...]
        dx = err * (1.0 / D)
        dx_ref[...] = dx
        dxb_ref[...] = dx.astype(BF16)

        @pl.when(pl.program_id(0) == 0)
        def _():
            loss_ref[...] = jnp.zeros_like(loss_ref)
        loss_ref[...] += 0.5 * jnp.sum(jnp.mean(err * err, axis=-1, keepdims=True), axis=0, keepdims=True)

    row = lambda i: (i, 0)
    fixed = lambda i: (0, 0)
    return pl.pallas_call(
        body, name="ffn_down_loss", grid=(T // tm,),
        in_specs=[pl.BlockSpec((4, tm, CS), lambda i: (0, i, 0)), pl.BlockSpec((4, CS, D), lambda i: (0, 0, 0)),
                  pl.BlockSpec((tm, D), row), pl.BlockSpec((tm, D), row)],
        out_specs=[pl.BlockSpec((1, 1), fixed), pl.BlockSpec((tm, D), row), pl.BlockSpec((tm, D), row)],
        out_shape=[_sds((1, 1)), _sds((T, D)), _sds((T, D), BF16)],
        compiler_params=_cp(("arbitrary",)),
    )(act, wd4, x1, target)


def _ffn_dact(dx2b, wd4):
    tm = 1024

    def body(d_ref, w_ref, o_ref):
        o_ref[0] = _nt(d_ref[...], w_ref[0])

    return pl.pallas_call(
        body, name="ffn_dact", grid=(4, T // tm),
        in_specs=[pl.BlockSpec((tm, D), lambda d, i: (i, 0)), pl.BlockSpec((1, CS, D), lambda d, i: (d, 0, 0))],
        out_specs=pl.BlockSpec((1, tm, CS), lambda d, i: (d, i, 0)), out_shape=_sds((4, T, CS)),
        compiler_params=_cp(("parallel", "parallel")),
    )(dx2b, wd4)


def _ffn_dwd(act, dx2b):
    def body(a_ref, d_ref, o_ref):
        o_ref[0] = _tn(a_ref[0], d_ref[...]).astype(BF16)

    return pl.pallas_call(
        body, name="ffn_dwd", grid=(4,),
        in_specs=[pl.BlockSpec((1, T, CS), lambda d: (d, 0, 0)), pl.BlockSpec((T, D), lambda d: (0, 0))],
        out_specs=pl.BlockSpec((1, CS, D), lambda d: (d, 0, 0)), out_shape=_sds((4, CS, D), BF16),
        compiler_params=_cp(("parallel",)),
    )(act, dx2b)


def _ffn_dwu(h2, dup4):
    tn = 1024

    def body(h_ref, d_ref, o_ref):
        o_ref[0, 0] = _tn(d_ref[0, 0], h_ref[...]).astype(BF16)

    return pl.pallas_call(
        body, name="ffn_dwu", grid=(NDEV, D // tn),
        in_specs=[pl.BlockSpec((T, tn), lambda e, i: (0, i)),
                  pl.BlockSpec((1, 1, T, CS), lambda e, i: (e // 4, e % 4, 0, 0))],
        out_specs=pl.BlockSpec((1, 1, CS, tn), lambda e, i: (e // 4, e % 4, 0, i)),
        out_shape=_sds((2, 4, CS, D), BF16),
        compiler_params=_cp(("parallel", "parallel")),
    )(h2, dup4)


def _ffn_up_bwd(dup4, wu4, x1, g, dres, arrs):
    tm = 256
    n = len(arrs)

    def body(d_ref, w_ref, x_ref, g_ref, dres_ref, *rest):
        a_refs, (dx_ref, dg_ref), fs_refs, sems = rest[:n], rest[n:n + 2], rest[n + 2:2 * n + 2], rest[2 * n + 2:]

        @pl.when(pl.program_id(0) == 0)
        def _():
            for cp in _sibling_swap_copies(a_refs, fs_refs, *sems):
                cp.start()

        dh = jnp.zeros((tm, D), F32)
        for gv in range(2):
            for d in range(4):
                dh = dh + _nn(d_ref[gv, d], w_ref[gv, d])
        xv = x_ref[...]
        r = _rms_r(xv)
        dxr, dgp = _rms_bwd(xv * r, r, g_ref[...], dh)
        dx_ref[...] = dres_ref[...] + dxr

        @pl.when(pl.program_id(0) == 0)
        def _():
            dg_ref[...] = jnp.zeros_like(dg_ref)
        dg_ref[...] += dgp

        @pl.when(pl.program_id(0) == T // tm - 1)
        def _():
            for cp in _sibling_swap_copies(a_refs, fs_refs, *sems):
                cp.wait()

    row = lambda i: (i, 0)
    fixed = lambda i: (0, 0)
    anyspec = pl.BlockSpec(memory_space=pl.ANY)
    res = pl.pallas_call(
        body, name="ffn_up_bwd", grid=(T // tm,),
        in_specs=[pl.BlockSpec((2, 4, tm, CS), lambda i: (0, 0, i, 0)), pl.BlockSpec((2, 4, CS, D), lambda i: (0, 0, 0, 0)),
                  pl.BlockSpec((tm, D), row), pl.BlockSpec((1, D), fixed), pl.BlockSpec((tm, D), row)] + [anyspec] * n,
        out_specs=[pl.BlockSpec((tm, D), row), pl.BlockSpec((1, D), fixed)] + [anyspec] * n,
        out_shape=[_sds((T, D)), _sds((1, D))] + [_sds((4, *a.shape[2:]), a.dtype) for a in arrs],
        scratch_shapes=[pltpu.SemaphoreType.DMA((4 * n,)), pltpu.SemaphoreType.DMA((4 * n,))],
        compiler_params=_cp(("arbitrary",)),
    )(dup4, wu4, x1, g, dres, *arrs)
    return res[0], res[1], res[2:]


def _adamw_math(w, g, m, v):
    m = ADAM_B1 * m + (1.0 - ADAM_B1) * g
    v = ADAM_B2 * v + (1.0 - ADAM_B2) * jnp.square(g)
    m_hat = m / (1.0 - ADAM_B1 ** ADAM_STEP)
    v_hat = v / (1.0 - ADAM_B2 ** ADAM_STEP)
    delta = -ADAM_LR * (m_hat / (jnp.sqrt(v_hat) + ADAM_EPS) + ADAM_WD * w)
    return delta, m, v


def _where_am_i():
    x, y, c = _place()
    return jnp.stack([c, 2 * x + y]).astype(jnp.int32)


def _pair_sum(a4, recv, name):
    _, _, rows, cols = a4.shape
    tr = _row_tile(rows, cols, 3 << 20)

    def body(sel_ref, own_ref, recv_ref, out_ref):
        out_ref[...] = (own_ref[0].astype(F32) + recv_ref[...].astype(F32)).astype(out_ref.dtype)

    grid_spec = pltpu.PrefetchScalarGridSpec(
        num_scalar_prefetch=1, grid=(4, rows // tr),
        in_specs=[pl.BlockSpec((1, 1, tr, cols), lambda k, i, s: (k, s[0], i, 0)),
                  pl.BlockSpec((1, tr, cols), lambda k, i, s: (k, i, 0))],
        out_specs=pl.BlockSpec((1, tr, cols), lambda k, i, s: (k, i, 0)),
    )
    return pl.pallas_call(
        body, name=name, grid_spec=grid_spec, out_shape=_sds((4, rows, cols), a4.dtype),
        compiler_params=_cp(("arbitrary", "arbitrary")),
    )(_where_am_i(), a4, recv)


def _adamw_rs(a4, recv, from_chips, w, m, v, name):
    rows, cols = w.shape
    tr = _row_tile(rows, cols, 3 << 19)

    def body(sel_ref, own_ref, recv_ref, fc_ref, w_ref, m_ref, v_ref, g_ref, d_ref, nm_ref, nv_ref):
        g = own_ref[0, 0].astype(F32) + recv_ref[0].astype(F32)
        for j in range(3):
            g = g + fc_ref[j].astype(F32)
        d, nm, nv = _adamw_math(w_ref[...], g, m_ref[...], v_ref[...])
        g_ref[...] = g
        d_ref[...] = d
        nm_ref[...] = nm
        nv_ref[...] = nv

    flat = pl.BlockSpec((tr, cols), lambda i, s: (i, 0))
    grid_spec = pltpu.PrefetchScalarGridSpec(
        num_scalar_prefetch=1, grid=(rows // tr,),
        in_specs=[pl.BlockSpec((1, 1, tr, cols), lambda i, s: (s[1], s[0], i, 0)),
                  pl.BlockSpec((1, tr, cols), lambda i, s: (s[1], i, 0)),
                  pl.BlockSpec((3, tr, cols), lambda i, s: (0, i, 0)), flat, flat, flat],
        out_specs=[flat] * 4,
    )
    return pl.pallas_call(
        body, name=name, grid_spec=grid_spec, out_shape=[_sds((rows, cols))] * 4,
        compiler_params=_cp(("arbitrary",)),
    )(_where_am_i(), a4, recv, from_chips, w, m, v)


_SMALL_F32 = (("g_mix", (8, 128)), ("b_f", (1, 8)), ("g_q", (1, 64)), ("g_k", (1, 64)), ("lambda_re", (32, 64)),
              ("lambda_im", (32, 64)), ("log_step", (1, 32)), ("d_skip", (32, 16)), ("b_glu", (4, 128)),
              ("g_attn_out", (4, 128)), ("g_ssm_out", (4, 128)), ("g_ffn", (8, 128)), ("conv_b", (44, 128)))
_SMALL_PAIRS = (("b_re", "b_im"), ("c_re", "c_im"))
_PAIR_SHAPE = (NG * GS, NP)


def _ceil8(r):
    return -(-r // 8) * 8


def _small_names():
    return [n for n, _ in _SMALL_F32] + [n for pair in _SMALL_PAIRS for n in pair]


def _pack_small_grads(g, loss_dev):
    parts = [jnp.pad(g[n].reshape(r, c), ((0, _ceil8(r) - r), (0, LANES - c))) for n, (r, c) in _SMALL_F32]
    parts.append(jnp.pad(loss_dev, ((0, 7), (0, LANES - 1))))
    pairs = [jnp.concatenate([g[a].reshape(_PAIR_SHAPE), g[b].reshape(_PAIR_SHAPE)], axis=1) for a, b in _SMALL_PAIRS]
    return jnp.concatenate(parts, axis=0), jnp.concatenate(pairs, axis=0).astype(BF16)


def _adamw_small(gall, gall_b, wmv):
    names = _small_names()
    shapes = [s for _, s in _SMALL_F32] + [_PAIR_SHAPE] * (2 * len(_SMALL_PAIRS))
    npar = len(names)

    def body(ga_ref, gb_ref, *rest):
        ins, loss_ref, outs = rest[:3 * npar], rest[3 * npar], rest[3 * npar + 1:]
        ga, gb = ga_ref[0], gb_ref[0].astype(F32)
        for dev in range(1, NDEV):
            ga = ga + ga_ref[dev]
            gb = gb + gb_ref[dev].astype(F32)
        grads, off = [], 0
        for _, (r, c) in _SMALL_F32:
            grads.append(ga[off:off + r, 0:c])
            off += _ceil8(r)
        loss_ref[...] = ga[off:off + 8]
        rows, cols = _PAIR_SHAPE
        for j in range(len(_SMALL_PAIRS)):
            for h in range(2):
                grads.append(gb[rows * j:rows * (j + 1), cols * h:cols * (h + 1)])
        for i, g in enumerate(grads):
            d, nm, nv = _adamw_math(ins[3 * i][...], g, ins[3 * i + 1][...], ins[3 * i + 2][...])
            outs[4 * i][...] = g
            outs[4 * i + 1][...] = d
            outs[4 * i + 2][...] = nm
            outs[4 * i + 3][...] = nv

    flat = [a for n in names for a in wmv[n]]
    out_shape = [_sds((8, LANES))] + [_sds(s) for s in shapes for _ in range(4)]
    res = pl.pallas_call(body, name="adamw_small", out_shape=out_shape, compiler_params=_cp())(gall, gall_b, *flat)
    return res[0], {n: res[1 + 4 * i:5 + 4 * i] for i, n in enumerate(names)}


def _block_diag(m):
    eye = jnp.eye(8, dtype=m.dtype)
    r, c = m.shape[2], m.shape[3]
    return (m[:, :, :, None, :] * eye[None, :, None, :, None]).reshape(4, 8 * r, 8 * c)


def _diag_blocks(dense, r, c):
    eye = jnp.eye(8, dtype=dense.dtype)
    return jnp.sum(dense.reshape(4, 8, r, 8, c) * eye[None, :, None, :, None], axis=3)


def kernel(x, g_mix, w_in, b_f, g_q, g_k, lambda_re, lambda_im, log_step, b_re, b_im, c_re, c_im, d_skip, w_glu, b_glu, g_attn_out, g_ssm_out, w_out, g_ffn, w_up, conv_w, conv_b, w_down, loss_target, m_g_mix, m_w_in, m_b_f, m_g_q, m_g_k, m_lambda_re, m_lambda_im, m_log_step, m_b_re, m_b_im, m_c_re, m_c_im, m_d_skip, m_w_glu, m_b_glu, m_g_attn_out, m_g_ssm_out, m_w_out, m_g_ffn, m_w_up, m_conv_w, m_conv_b, m_w_down, v_g_mix, v_w_in, v_b_f, v_g_q, v_g_k, v_lambda_re, v_lambda_im, v_log_step, v_b_re, v_b_im, v_c_re, v_c_im, v_d_skip, v_w_glu, v_b_glu, v_g_attn_out, v_g_ssm_out, v_w_out, v_g_ffn, v_w_up, v_conv_w, v_conv_b, v_w_down):
    weights = dict(g_mix=g_mix, w_in=w_in, b_f=b_f, g_q=g_q, g_k=g_k, lambda_re=lambda_re, lambda_im=lambda_im,
                   log_step=log_step, b_re=b_re, b_im=b_im, c_re=c_re, c_im=c_im, d_skip=d_skip, w_glu=w_glu,
                   b_glu=b_glu, g_attn_out=g_attn_out, g_ssm_out=g_ssm_out, w_out=w_out, g_ffn=g_ffn, w_up=w_up,
                   conv_w=conv_w, conv_b=conv_b, w_down=w_down)
    mom_m = dict(g_mix=m_g_mix, w_in=m_w_in, b_f=m_b_f, g_q=m_g_q, g_k=m_g_k, lambda_re=m_lambda_re,
                 lambda_im=m_lambda_im, log_step=m_log_step, b_re=m_b_re, b_im=m_b_im, c_re=m_c_re, c_im=m_c_im,
                 d_skip=m_d_skip, w_glu=m_w_glu, b_glu=m_b_glu, g_attn_out=m_g_attn_out, g_ssm_out=m_g_ssm_out,
                 w_out=m_w_out, g_ffn=m_g_ffn, w_up=m_w_up, conv_w=m_conv_w, conv_b=m_conv_b, w_down=m_w_down)
    mom_v = dict(g_mix=v_g_mix, w_in=v_w_in, b_f=v_b_f, g_q=v_g_q, g_k=v_g_k, lambda_re=v_lambda_re,
                 lambda_im=v_lambda_im, log_step=v_log_step, b_re=v_b_re, b_im=v_b_im, c_re=v_c_re, c_im=v_c_im,
                 d_skip=v_d_skip, w_glu=v_w_glu, b_glu=v_b_glu, g_attn_out=v_g_attn_out, g_ssm_out=v_g_ssm_out,
                 w_out=v_w_out, g_ffn=v_g_ffn, w_up=v_w_up, conv_w=v_conv_w, conv_b=v_conv_b, w_down=v_w_down)
    names = list(weights)
    big = ("w_in", "w_glu", "w_out", "w_up", "w_down", "conv_w")

    xs = x[0]
    target = loss_target[0]
    row = lambda a: a.reshape(1, -1)

    (win_g,) = _all_gather([w_in.T.astype(BF16)], "gather_w_in")
    w_in_t = win_g.reshape(8 * 257, D)
    wcat = jnp.concatenate([w_in_t[0:1536], w_in_t[1544:2056], jnp.pad(w_in_t[1536:1544], ((0, LANES - NH), (0, 0)))],
                           axis=0)
    cb4 = conv_b.reshape(2, 4, 1, CS)

    h1, z, f_t, (wg_g, wo_g) = _in_proj(xs, row(g_mix), wcat, [w_glu.astype(BF16), w_out.astype(BF16)])
    bf_col = b_f.reshape(NH, 1)
    ck = _gates_fwd(f_t, bf_col).reshape(_HP, 2, T)
    mlp_shards = [w_up.T.astype(BF16), conv_w]
    att, lse, (wu_g, cw_g), (wg_g, wo_g) = _attn_fwd(z, ck, row(g_q), row(g_k), mlp_shards, [wg_g, wo_g])

    ls_col = log_step.reshape(NG, 1)
    br_t, bi_t = b_re.transpose(0, 2, 1), b_im.transpose(0, 2, 1)
    ab_re, ab_im, bb_re, bb_im = _disc_fwd(lambda_re, lambda_im, ls_col, br_t, bi_t)
    bb = jnp.concatenate([_block_diag(bb_re.reshape(4, 8, GS, NP)), _block_diag(bb_im.reshape(4, 8, GS, NP))],
                         axis=2).astype(BF16)
    ab = jnp.stack([ab_re.reshape(4, 512), ab_im.reshape(4, 512)], axis=1)
    cdiag = lambda cm: _block_diag(cm.reshape(4, 8, GS, NP).transpose(0, 1, 3, 2))
    cc = jnp.concatenate([cdiag(c_re), -cdiag(c_im)], axis=1).astype(BF16)
    dsk = d_skip.reshape(4, 1, LANES)
    y, xs_il, (wu_g, cw_g) = _ssm_fwd(z, bb, ab, cc, dsk, mlp_shards, [wu_g, cw_g])
    wg_f = wg_g.reshape(SW, SW)
    wo_f = wo_g.reshape(D, D)
    x1, (wu_g, cw_g) = _mix_fwd(att, y, xs, wg_f, row(b_glu), row(g_attn_out), row(g_ssm_out), wo_f, [wu_g, cw_g])
    wu4 = wu_g.reshape(2, 4, CS, D)
    cw4 = cw_g.reshape(2, 4, 3, CS)
    h2, up4, (wd_g,) = _ffn_up(x1, row(g_ffn), wu4, [w_down.astype(BF16)])
    act, (wd_g,) = _ffn_act(up4, cw4, cb4, [wd_g])
    wd4 = wd_g.reshape(4, CS, D)
    loss_dev, dx2, dx2b = _ffn_down_loss(act, wd4, x1, target)

    dact = _ffn_dact(dx2b, wd4)
    d_wd = _ffn_dwd(act, dx2b)
    dup4, dcw4, dcb4 = _ffn_act_bwd(up4, dact, cw4, cb4)
    d_wu = _ffn_dwu(h2, dup4)
    mlp, mixer = ("w_up", "w_down", "conv_w"), ("w_out", "w_glu")
    early = mlp + mixer
    by_dest = {"w_up": d_wu.reshape(4, 2, CS, D), "w_down": d_wd.reshape(4, 2, DFF // NDEV, D),
               "conv_w": dcw4.reshape(4, 2, 3, CS)}
    dx1, dg_ffn, fs = _ffn_up_bwd(dup4, wu4, x1, row(g_ffn), dx2, [by_dest[n] for n in mlp])
    from_sibling = dict(zip(mlp, fs))
    chip_sums = {n: _pair_sum(by_dest[n], from_sibling[n], "rs_pair_sum_" + n) for n in mlp}
    datt, dy, d_wg, d_bg, d_ga, d_gs, d_wo = _mix_bwd(att, y, dx1, wg_f, row(b_glu), row(g_attn_out),
                                                       row(g_ssm_out), wo_f)
    by_dest.update({"w_out": d_wo.reshape(4, 2, D // NDEV, D), "w_glu": d_wg.reshape(4, 2, SW // NDEV, SW)})
    under_ssm = ("w_down", "conv_w")
    du, dbb, dab, dcc, ddsk, fs, fc = _ssm_bwd(z, dy, xs_il, bb, ab, cc, dsk, [by_dest[n] for n in mixer],
                                               [chip_sums[n] for n in under_ssm])
    from_sibling.update(zip(mixer, fs))
    from_chips = dict(zip(under_ssm, fc))
    chip_sums.update({n: _pair_sum(by_dest[n], from_sibling[n], "rs_pair_sum_" + n) for n in mixer})
    du = du.astype(BF16)
    dbb_re = _diag_blocks(dbb[:, :, 0:512], GS, NP).reshape(NG, GS, NP)
    dbb_im = _diag_blocks(dbb[:, :, 512:1024], GS, NP).reshape(NG, GS, NP)
    dlr, dli, dls, dbr_t, dbi_t = _disc_bwd(lambda_re, lambda_im, ls_col, br_t, bi_t,
                                            dab[:, 0].reshape(NG, NP), dab[:, 1].reshape(NG, NP), dbb_re, dbb_im)
    d_cre = _diag_blocks(dcc[:, 0:512], NP, GS).transpose(0, 1, 3, 2).reshape(NG, GS, NP)
    d_cim = -_diag_blocks(dcc[:, 512:1024], NP, GS).transpose(0, 1, 3, 2).reshape(NG, GS, NP)

    under_attn = ("w_up",) + mixer
    dq, dk, dv, dck, d_gq, d_gk, fc = _attn_bwd(z, ck, row(g_q), row(g_k), lse, datt, [chip_sums[n] for n in under_attn])
    from_chips.update(zip(under_attn, fc))
    df_t, d_bf = _gates_bwd(dck.reshape(NH, T), f_t, bf_col)
    df = jnp.concatenate([df_t.T, jnp.zeros((T, LANES - NH), F32)], axis=1).astype(BF16)
    d_wcat = _in_proj_dw(h1, dq, dk, dv, du, df)
    d_win = jnp.concatenate([d_wcat[0:1536], d_wcat[2048:2048 + NH], d_wcat[1536:2048]], axis=0)
    by_dest["w_in"] = d_win.astype(BF16).reshape(4, 2, 257, D)
    (from_sibling["w_in"],) = _swap_sibling([by_dest["w_in"]], "rs_pair_swap_w_in")
    chip_sums["w_in"] = _pair_sum(by_dest["w_in"], from_sibling["w_in"], "rs_pair_sum_w_in")
    grad_x, dg_mix, (from_chips["w_in"],) = _in_proj_bwd(dq, dk, dv, du, df, wcat, xs, row(g_mix), dx1, [chip_sums["w_in"]])

    grads, deltas, new_m, new_v = {}, {}, {}, {}
    for n in early + ("w_in",):
        flip = (lambda a: a.T) if n in ("w_up", "w_in") else (lambda a: a)
        outs = _adamw_rs(by_dest[n], from_sibling[n], from_chips[n], flip(weights[n]), flip(mom_m[n]), flip(mom_v[n]),
                         "adamw_" + n)
        grads[n], deltas[n], new_m[n], new_v[n] = (flip(o) for o in outs)

    small = dict(g_mix=dg_mix, b_f=d_bf, g_q=d_gq, g_k=d_gk, lambda_re=dlr, lambda_im=dli, log_step=dls,
                 b_re=dbr_t, b_im=dbi_t, c_re=d_cre, c_im=d_cim, d_skip=ddsk, b_glu=d_bg, g_attn_out=d_ga,
                 g_ssm_out=d_gs, g_ffn=dg_ffn, conv_b=dcb4)
    small_all, small_all_b = _all_gather(list(_pack_small_grads(small, loss_dev)), "gather_small_grads")
    views = dict(_SMALL_F32)
    swapped = ("b_re", "b_im")

    def view(n, a):
        return a.transpose(0, 2, 1).reshape(_PAIR_SHAPE) if n in swapped else a.reshape(views.get(n, _PAIR_SHAPE))

    loss_p, small_out = _adamw_small(small_all, small_all_b,
                                     {n: (view(n, weights[n]), view(n, mom_m[n]), view(n, mom_v[n])) for n in small})
    loss = loss_p[0, 0]
    for n in small:
        back = (lambda a: a.reshape(NG, GS, NP).transpose(0, 2, 1)) if n in swapped else (lambda a: a.reshape(weights[n].shape))
        grads[n], deltas[n], new_m[n], new_v[n] = (back(o) for o in small_out[n])

    return (loss, grad_x[None], *[grads[n] for n in names], *[deltas[n] for n in names],
            *[new_m[n] for n in names], *[new_v[n] for n in names])
```

```python
import jax
import jax.numpy as jnp
from jax import lax
from jax.experimental import pallas as pl
from jax.experimental.pallas import tpu as pltpu

F32, BF16 = jnp.float32, jnp.bfloat16
T, D = 2048, 1024
NH, DH = 8, 64
AW, SW = 512, 512
NG, GS, NP = 32, 16, 64
DFF = 2816
NDEV = 8
CS = 2 * DFF // NDEV
EPS = 1e-6
NEG = -1e30
ZC = 4 * 512 + 128
SEG = 256
NSEG = T // SEG
_SCAN_UNROLL = 4
BQ = 512
VMEM_LIMIT = 56 * 1024 * 1024
LANES = 128
MESH = pl.DeviceIdType.MESH
HI = lax.Precision.HIGHEST

ADAM_LR, ADAM_B1, ADAM_B2, ADAM_EPS, ADAM_WD, ADAM_STEP = 0.001, 0.9, 0.999, 1e-08, 0.01, 10


def _cp(dims=None):
    if dims is None:
        return pltpu.CompilerParams(vmem_limit_bytes=VMEM_LIMIT)
    return pltpu.CompilerParams(dimension_semantics=dims, vmem_limit_bytes=VMEM_LIMIT)


def _sds(shape, dtype=F32):
    return jax.ShapeDtypeStruct(shape, dtype)


def _nt(a, b):
    return lax.dot_general(a, b, (((1,), (1,)), ((), ())), preferred_element_type=F32)


def _tn(a, b):
    return lax.dot_general(a, b, (((0,), (0,)), ((), ())), preferred_element_type=F32)


def _nn(a, b):
    return jnp.dot(a, b, preferred_element_type=F32)


def _rms_r(x):
    return lax.rsqrt(jnp.mean(x * x, axis=-1, keepdims=True) + EPS)


def _rms_bwd(n, r, g, dh):
    dn = dh * g
    dx = r * (dn - n * jnp.mean(dn * n, axis=-1, keepdims=True))
    return dx, jnp.sum(dh * n, axis=0, keepdims=True)


def _row_tile(rows, cols, limit):
    tr = rows
    while tr * cols * 4 > limit and tr % 32 == 0:
        tr //= 2
    return tr


def _place():
    return lax.axis_index("x"), lax.axis_index("y"), lax.axis_index("c")


def _all_gather(shards, name):
    n = len(shards)

    def body(*refs):
        x_refs, out_refs = refs[:n], refs[n:2 * n]
        send_sems, recv_sems, local_sems = refs[2 * n:]
        x, y, c = _place()
        me, sibling = (x, y, c), (x, y, 1 - c)
        chips = [(1 - x, y), (x, 1 - y), (1 - x, 1 - y)]

        def copy(a, k, block, to, src=None):
            px, py, pc = block
            slot = out_refs[a].at[4 * px + 2 * py + pc]
            return pltpu.make_async_remote_copy(
                src_ref=slot if src is None else src, dst_ref=slot,
                send_sem=send_sems.at[7 * a + k], recv_sem=recv_sems.at[7 * a + k], device_id=to, device_id_type=MESH)

        mine, first, passed = [], [], []
        for a in range(n):
            cp = pltpu.make_async_copy(x_refs[a], out_refs[a].at[4 * x + 2 * y + c], local_sems.at[a])
            cp.start()
            mine.append(cp)
            cps = [copy(a, 0, me, sibling, src=x_refs[a])]
            cps += [copy(a, 1 + j, me, (*chip, c), src=x_refs[a]) for j, chip in enumerate(chips)]
            for cp in cps:
                cp.start()
            first += cps
        for a in range(n):
            for j, chip in enumerate(chips):
                copy(a, 1 + j, (*chip, c), me).wait_recv()
                fwd = copy(a, 4 + j, (*chip, c), sibling)
                fwd.start()
                passed.append(fwd)
        for a in range(n):
            copy(a, 0, sibling, me).wait_recv()
            for j, chip in enumerate(chips):
                copy(a, 4 + j, (*chip, 1 - c), me).wait_recv()
        for cp in first + passed:
            cp.wait_send()
        for cp in mine:
            cp.wait()

    anyspec = pl.BlockSpec(memory_space=pl.ANY)
    return pl.pallas_call(
        body, name=name,
        out_shape=[_sds((NDEV, *s.shape), s.dtype) for s in shards],
        in_specs=[anyspec] * n, out_specs=[anyspec] * n,
        scratch_shapes=[pltpu.SemaphoreType.DMA((7 * n,)), pltpu.SemaphoreType.DMA((7 * n,)),
                        pltpu.SemaphoreType.DMA((n,))],
    )(*shards)


_NEAR, _FAR = (0, 1, 2), (3,)


def _gather_first_copies(x_refs, out_refs, send_sems, recv_sems, local_sems, which=_NEAR + _FAR):
    x, y, c = _place()
    slot = 4 * x + 2 * y + c
    peers = [(x, y, 1 - c), (1 - x, y, c), (x, 1 - y, c), (1 - x, 1 - y, c)]
    local, remote = [], []
    for a, (x_ref, out_ref) in enumerate(zip(x_refs, out_refs)):
        if 0 in which:
            local.append(pltpu.make_async_copy(x_ref, out_ref.at[slot], local_sems.at[a]))
        for k in which:
            remote.append(pltpu.make_async_remote_copy(
                src_ref=x_ref, dst_ref=out_ref.at[slot], send_sem=send_sems.at[4 * a + k],
                recv_sem=recv_sems.at[4 * a + k], device_id=peers[k], device_id_type=MESH))
    return local, remote


def _gather_first_scratch(n):
    return [pltpu.SemaphoreType.DMA((4 * n,)), pltpu.SemaphoreType.DMA((4 * n,)), pltpu.SemaphoreType.DMA((n,))]


def _chip_swap_copies(p_refs, out_refs, send_sems, recv_sems):
    x, y, c = _place()
    chips = [(1 - x, y), (x, 1 - y), (1 - x, 1 - y)]
    cps = []
    for a, (p_ref, out_ref) in enumerate(zip(p_refs, out_refs)):
        for j, (px, py) in enumerate(chips):
            cps.append(pltpu.make_async_remote_copy(
                src_ref=p_ref.at[2 * px + py], dst_ref=out_ref.at[j],
                send_sem=send_sems.at[3 * a + j], recv_sem=recv_sems.at[3 * a + j],
                device_id=(px, py, c), device_id_type=MESH))
    return cps


def _sibling_swap_copies(a_refs, out_refs, send_sems, recv_sems):
    x, y, c = _place()
    cps = []
    for a, (a_ref, out_ref) in enumerate(zip(a_refs, out_refs)):
        for k in range(4):
            cps.append(pltpu.make_async_remote_copy(
                src_ref=a_ref.at[k, 1 - c], dst_ref=out_ref.at[k],
                send_sem=send_sems.at[4 * a + k], recv_sem=recv_sems.at[4 * a + k],
                device_id=(x, y, 1 - c), device_id_type=MESH))
    return cps


def _pass_on_copies(buf_refs, send_sems, recv_sems):
    x, y, c = _place()
    chips = [(1 - x, y), (x, 1 - y), (1 - x, 1 - y)]
    cps = []
    for a, buf in enumerate(buf_refs):
        for j, (px, py) in enumerate(chips):
            block = buf.at[4 * px + 2 * py + c]
            cps.append(pltpu.make_async_remote_copy(
                src_ref=block, dst_ref=block, send_sem=send_sems.at[3 * a + j], recv_sem=recv_sems.at[3 * a + j],
                device_id=(x, y, 1 - c), device_id_type=MESH))
    return cps


def _swap_sibling(arrs, name):
    n = len(arrs)

    def body(*refs):
        cps = _sibling_swap_copies(refs[:n], refs[n:2 * n], *refs[2 * n:])
        for cp in cps:
            cp.start()
        for cp in cps:
            cp.wait()

    anyspec = pl.BlockSpec(memory_space=pl.ANY)
    return pl.pallas_call(
        body, name=name,
        out_shape=[_sds((4, *a.shape[2:]), a.dtype) for a in arrs],
        in_specs=[anyspec] * n, out_specs=[anyspec] * n,
        scratch_shapes=[pltpu.SemaphoreType.DMA((4 * n,)), pltpu.SemaphoreType.DMA((4 * n,))],
    )(*arrs)


def _in_proj(x, g, wcat, shards):
    tm = 256
    n = len(shards)

    def body(x_ref, g_ref, w_ref, *rest):
        x_refs, (h_ref, z_ref, ft_ref), out_refs, sems = rest[:n], rest[n:n + 3], rest[n + 3:2 * n + 3], rest[2 * n + 3:]

        @pl.when(pl.program_id(0) == 0)
        def _():
            local, remote = _gather_first_copies(x_refs, out_refs, *sems)
            for cp in local + remote:
                cp.start()

        xv = x_ref[...]
        h = (xv * _rms_r(xv) * g_ref[...]).astype(BF16)
        h_ref[...] = h
        z = _nt(h, w_ref[...])
        z_ref[...] = z
        ft_ref[...] = z[:, 2048:ZC].T[0:NH, :]

        @pl.when(pl.program_id(0) == T // tm - 1)
        def _():
            local, remote = _gather_first_copies(x_refs, out_refs, *sems)
            for cp in remote + local:
                cp.wait()

    row = lambda i: (i, 0)
    fixed = lambda i: (0, 0)
    anyspec = pl.BlockSpec(memory_space=pl.ANY)
    res = pl.pallas_call(
        body, name="in_proj", grid=(T // tm,),
        in_specs=[pl.BlockSpec((tm, D), row), pl.BlockSpec((1, D), fixed), pl.BlockSpec((ZC, D), fixed)] + [anyspec] * n,
        out_specs=[pl.BlockSpec((tm, D), row), pl.BlockSpec((tm, ZC), row), pl.BlockSpec((NH, tm), lambda i: (0, i))]
        + [anyspec] * n,
        out_shape=[_sds((T, D), BF16), _sds((T, ZC)), _sds((NH, T))] + [_sds((NDEV, *s.shape), s.dtype) for s in shards],
        scratch_shapes=_gather_first_scratch(n),
        compiler_params=_cp(("arbitrary",)),
    )(x, g, wcat, *shards)
    return res[0], res[1], res[2], res[3:]


def _in_proj_dw(h, dq, dk, dv, du, df):
    tm = 512

    def body(h_ref, dq_ref, dk_ref, dv_ref, du_ref, df_ref, o_ref):
        hv = h_ref[...]
        for j, d_ref in enumerate((dq_ref, dk_ref, dv_ref, du_ref)):
            o_ref[512 * j:512 * (j + 1), :] = _tn(d_ref[...], hv)
        o_ref[2048:ZC, :] = _tn(df_ref[...], hv)

    full = lambda w: pl.BlockSpec((T, w), lambda i: (0, 0))
    return pl.pallas_call(
        body, name="in_proj_dw", grid=(D // tm,),
        in_specs=[pl.BlockSpec((T, tm), lambda i: (0, i)), full(512), full(512), full(512), full(512), full(LANES)],
        out_specs=pl.BlockSpec((ZC, tm), lambda i: (0, i)), out_shape=_sds((ZC, D)),
        compiler_params=_cp(("parallel",)),
    )(h, dq, dk, dv, du, df)


def _in_proj_bwd(dq, dk, dv, du, df, wcat, x, g, dres, parts, bufs):
    tm = 256
    n, nb = len(parts), len(bufs)

    def body(dq_ref, dk_ref, dv_ref, du_ref, df_ref, w_ref, x_ref, g_ref, dres_ref, *rest):
        p_refs, (dx_ref, dg_ref) = rest[:n], rest[n + nb:n + nb + 2]
        fc_refs, buf_refs, sems = rest[n + nb + 2:2 * n + nb + 2], rest[2 * n + nb + 2:2 * n + 2 * nb + 2], rest[2 * n + 2 * nb + 2:]

        def carried():
            return _chip_swap_copies(p_refs, fc_refs, *sems[:2]) + _pass_on_copies(buf_refs, *sems[2:])

        @pl.when(pl.program_id(0) == 0)
        def _():
            for cp in carried():
                cp.start()

        dh = _nn(df_ref[...], w_ref[2048:ZC, :])
        for j, d_ref in enumerate((dq_ref, dk_ref, dv_ref, du_ref)):
            dh = dh + _nn(d_ref[...], w_ref[512 * j:512 * (j + 1), :])
        xv = x_ref[...]
        r = _rms_r(xv)
        dxr, dgp = _rms_bwd(xv * r, r, g_ref[...], dh)
        dx_ref[...] = dres_ref[...] + dxr

        @pl.when(pl.program_id(0) == 0)
        def _():
            dg_ref[...] = jnp.zeros_like(dg_ref)
        dg_ref[...] += dgp

        @pl.when(pl.program_id(0) == T // tm - 1)
        def _():
            for cp in carried():
                cp.wait()

    row = lambda i: (i, 0)
    fixed = lambda i: (0, 0)
    part = pl.BlockSpec((tm, 512), row)
    anyspec = pl.BlockSpec(memory_space=pl.ANY)
    res = pl.pallas_call(
        body, name="in_proj_bwd", grid=(T // tm,),
        in_specs=[part, part, part, part, pl.BlockSpec((tm, LANES), row), pl.BlockSpec((ZC, D), fixed),
                  pl.BlockSpec((tm, D), row), pl.BlockSpec((1, D), fixed), pl.BlockSpec((tm, D), row)]
        + [anyspec] * (n + nb),
        out_specs=[pl.BlockSpec((tm, D), row), pl.BlockSpec((1, D), fixed)] + [anyspec] * (n + nb),
        out_shape=[_sds((T, D)), _sds((1, D))] + [_sds((3, *p.shape[1:]), p.dtype) for p in parts]
        + [_sds(b.shape, b.dtype) for b in bufs],
        input_output_aliases={9 + n + i: 2 + n + i for i in range(nb)},
        scratch_shapes=[pltpu.SemaphoreType.DMA((3 * n,)), pltpu.SemaphoreType.DMA((3 * n,)),
                        pltpu.SemaphoreType.DMA((3 * nb,)), pltpu.SemaphoreType.DMA((3 * nb,))],
        compiler_params=_cp(("arbitrary",)),
    )(dq, dk, dv, du, df, wcat, x, g, dres, *parts, *bufs)
    return res[0], res[1], res[2:2 + n], res[2 + n:]


def _stack_lanes(v):
    return jnp.concatenate([v[:, LANES * b:LANES * (b + 1)] for b in range(T // LANES)], axis=0)


def _unstack_lanes(s):
    return jnp.concatenate([s[8 * b:8 * b + 8, :] for b in range(T // LANES)], axis=1)


def _cumsum_lanes(v, reverse):
    st = _stack_lanes(v)
    ri = lax.broadcasted_iota(jnp.int32, (LANES, LANES), 0)
    ci = lax.broadcasted_iota(jnp.int32, (LANES, LANES), 1)
    tri = (ri >= ci) if reverse else (ri <= ci)
    intra = jnp.dot(st, tri.astype(F32), precision=HI, preferred_element_type=F32)
    tot = intra[:, 0:1] if reverse else intra[:, LANES - 1:LANES]
    same_head = (ci % 8) == (ri % 8)
    other = (ci // 8 > ri // 8) if reverse else (ci // 8 < ri // 8)
    off = jnp.dot((same_head & other).astype(F32), jnp.broadcast_to(tot, (LANES, LANES)), precision=HI,
                  preferred_element_type=F32)
    return _unstack_lanes(intra + off)


def _gates_fwd(f_t, b_f):
    def body(f_ref, b_ref, c_ref):
        z = f_ref[...] + b_ref[...]
        lf = jnp.minimum(z, 0.0) - jnp.log(1.0 + jnp.exp(-jnp.abs(z)))
        c_ref[...] = _cumsum_lanes(lf, reverse=False)

    return pl.pallas_call(body, name="gates_fwd", out_shape=_sds((NH, T)), compiler_params=_cp())(f_t, b_f)


def _gates_bwd(dck, f_t, b_f):
    def body(d_ref, f_ref, b_ref, df_ref, db_ref):
        z = f_ref[...] + b_ref[...]
        dlf = _cumsum_lanes(d_ref[...], reverse=True)
        df = dlf * jax.nn.sigmoid(-z)
        df_ref[...] = df
        db_ref[...] = jnp.sum(df, axis=1, keepdims=True)

    return pl.pallas_call(body, name="gates_bwd", out_shape=[_sds((NH, T)), _sds((NH, 1))],
                          compiler_params=_cp())(dck, f_t, b_f)


def _lower_triangle():
    rows = lax.broadcasted_iota(jnp.int32, (BQ, BQ), 0)
    cols = lax.broadcasted_iota(jnp.int32, (BQ, BQ), 1)
    return cols <= rows


def _logit_parts(qb, kb, ck_row, lo, hi, tri):
    parts = []
    if lo > 0:
        parts.append((_nt(qb[lo:hi], kb[0:lo]) - ck_row[:, 0:lo], 0, lo))
    parts.append((jnp.where(tri, _nt(qb[lo:hi], kb[lo:hi]) - ck_row[:, lo:hi], NEG), lo, hi))
    return parts


_HP = NH // 2


def _qkv_specs():
    return [pl.BlockSpec((T, LANES), lambda p: (0, p)), pl.BlockSpec((T, LANES), lambda p: (0, _HP + p)),
            pl.BlockSpec((T, LANES), lambda p: (0, 2 * _HP + p))]


def _attn_fwd(z, ck, g_q, g_k, shards, bufs):
    scale = DH ** -0.5
    n, m = len(shards), len(bufs)

    def body(q_ref, k_ref, v_ref, ck_ref, gq_ref, gk_ref, *rest):
        x_refs, (o_ref, lse_ref) = rest[:n], rest[n + m:n + m + 2]
        out_refs, buf_refs, sems = rest[n + m + 2:2 * n + m + 2], rest[2 * n + m + 2:2 * n + 2 * m + 2], rest[2 * n + 2 * m + 2:]

        @pl.when(pl.program_id(0) == 0)
        def _():
            local, remote = _gather_first_copies(x_refs, out_refs, *sems[:3], which=_NEAR)
            for cp in local + remote + _pass_on_copies(buf_refs, *sems[3:]):
                cp.start()

        qb, kb, vb = [], [], []
        for hh in range(2):
            cols = slice(DH * hh, DH * (hh + 1))
            qv, kv = q_ref[:, cols], k_ref[:, cols]
            qb.append((qv * _rms_r(qv) * gq_ref[...] * scale).astype(BF16))
            kb.append((kv * _rms_r(kv) * gk_ref[...]).astype(BF16))
            vb.append(v_ref[:, cols].astype(BF16))
        tri = _lower_triangle()
        for i in range(T // BQ):
            lo, hi = i * BQ, (i + 1) * BQ
            outs = []
            for hh in range(2):
                parts = _logit_parts(qb[hh], kb[hh], ck_ref[0, hh:hh + 1, :], lo, hi, tri)
                top = jnp.max(parts[-1][0], axis=-1, keepdims=True)
                if lo > 0:
                    top = jnp.maximum(top, jnp.max(parts[0][0], axis=-1, keepdims=True))
                l, o = 0.0, 0.0
                for s, k0, k1 in parts:
                    p = jnp.exp(s - top)
                    l = l + jnp.sum(p, axis=-1, keepdims=True)
                    o = o + _nn(p.astype(BF16), vb[hh][k0:k1])
                outs.append(o / l)
                lse_ref[0, lo:hi, hh:hh + 1] = top + jnp.log(l)
            o_ref[lo:hi, :] = jnp.concatenate(outs, axis=1)

        @pl.when(pl.program_id(0) == _HP - 1)
        def _():
            local, remote = _gather_first_copies(x_refs, out_refs, *sems[:3], which=_NEAR)
            for cp in remote + local + _pass_on_copies(buf_refs, *sems[3:]):
                cp.wait()

    fixed = lambda p: (0, 0)
    anyspec = pl.BlockSpec(memory_space=pl.ANY)
    res = pl.pallas_call(
        body, name="attn_fwd", grid=(_HP,),
        in_specs=_qkv_specs() + [pl.BlockSpec((1, 2, T), lambda p: (p, 0, 0)), pl.BlockSpec((1, DH), fixed),
                                 pl.BlockSpec((1, DH), fixed)] + [anyspec] * (n + m),
        out_specs=[pl.BlockSpec((T, LANES), lambda p: (0, p)), pl.BlockSpec((1, T, 2), lambda p: (p, 0, 0))]
        + [anyspec] * (n + m),
        out_shape=[_sds((T, AW)), _sds((_HP, T, 2))] + [_sds((NDEV, *s.shape), s.dtype) for s in shards]
        + [_sds(b.shape, b.dtype) for b in bufs],
        input_output_aliases={6 + n + i: 2 + n + i for i in range(m)},
        scratch_shapes=_gather_first_scratch(n) + [pltpu.SemaphoreType.DMA((3 * m,)), pltpu.SemaphoreType.DMA((3 * m,))],
        compiler_params=_cp(("arbitrary",)),
    )(z, z, z, ck, g_q, g_k, *shards, *bufs)
    return res[0], res[1], res[2:2 + n], res[2 + n:]


def _attn_bwd(z, ck, g_q, g_k, lse, datt, parts, shards):
    scale = DH ** -0.5
    n, ns = len(parts), len(shards)

    def body(q_ref, k_ref, v_ref, ck_ref, gq_ref, gk_ref, lse_ref, do_ref, *rest):
        p_refs, x_refs, rest = rest[:n], rest[n:n + ns], rest[n + ns:]
        dq_ref, dk_ref, dv_ref, dck_ref, dgq_ref, dgk_ref = rest[:6]
        fc_refs, out_refs = rest[6:6 + n], rest[6 + n:6 + n + ns]
        (dkn_acc, dv_acc, dc_acc), sems = rest[6 + n + ns:9 + n + ns], rest[9 + n + ns:]

        def carried():
            local, remote = _gather_first_copies(x_refs, out_refs, *sems[2:])
            return _chip_swap_copies(p_refs, fc_refs, *sems[:2]) + local + remote

        @pl.when(pl.program_id(0) == 0)
        def _():
            for cp in carried():
                cp.start()

        nq, nk, rq, rk, qb, kb, vb, dob = [], [], [], [], [], [], [], []
        for hh in range(2):
            cols = slice(DH * hh, DH * (hh + 1))
            qv, kv = q_ref[:, cols], k_ref[:, cols]
            rq.append(_rms_r(qv))
            rk.append(_rms_r(kv))
            nq.append(qv * rq[hh])
            nk.append(kv * rk[hh])
            qb.append((nq[hh] * gq_ref[...] * scale).astype(BF16))
            kb.append((nk[hh] * gk_ref[...]).astype(BF16))
            vb.append(v_ref[:, cols].astype(BF16))
            dob.append(do_ref[:, cols].astype(BF16))
        dkn_acc[...] = jnp.zeros_like(dkn_acc)
        dv_acc[...] = jnp.zeros_like(dv_acc)
        dc_acc[...] = jnp.zeros_like(dc_acc)
        dgq = jnp.zeros((1, DH), F32)
        tri = _lower_triangle()
        for i in range(T // BQ):
            lo, hi = i * BQ, (i + 1) * BQ
            dqs = []
            for hh in range(2):
                lse = lse_ref[0, lo:hi, hh:hh + 1]
                pd = []
                for s, k0, k1 in _logit_parts(qb[hh], kb[hh], ck_ref[0, hh:hh + 1, :], lo, hi, tri):
                    pd.append((jnp.exp(s - lse), _nt(dob[hh][lo:hi], vb[hh][k0:k1]), k0, k1))
                mean = sum(jnp.sum(p * dp, axis=-1, keepdims=True) for p, dp, _, _ in pd)
                dqn = 0.0
                for p, dp, k0, k1 in pd:
                    ds = p * (dp - mean)
                    dsb = ds.astype(BF16)
                    dv_acc[hh, k0:k1, :] += _tn(p.astype(BF16), dob[hh][lo:hi])
                    dkn_acc[hh, k0:k1, :] += _tn(dsb, qb[hh][lo:hi])
                    dc_acc[hh:hh + 1, k0:k1] -= jnp.sum(ds, axis=0, keepdims=True)
                    dqn = dqn + _nn(dsb, kb[hh][k0:k1])
                dqx, dgp = _rms_bwd(nq[hh][lo:hi], rq[hh][lo:hi], gq_ref[...], dqn * scale)
                dqs.append(dqx)
                dgq = dgq + dgp
            dq_ref[lo:hi, :] = jnp.concatenate(dqs, axis=1).astype(BF16)
        dks, dgk = [], jnp.zeros((1, DH), F32)
        for hh in range(2):
            dkx, dgp = _rms_bwd(nk[hh], rk[hh], gk_ref[...], dkn_acc[hh])
            dks.append(dkx)
            dgk = dgk + dgp
        dk_ref[...] = jnp.concatenate(dks, axis=1).astype(BF16)
        dv_ref[...] = jnp.concatenate([dv_acc[0], dv_acc[1]], axis=1).astype(BF16)
        dck_ref[0] = dc_acc[...]

        @pl.when(pl.program_id(0) == 0)
        def _():
            dgq_ref[...] = jnp.zeros_like(dgq_ref)
            dgk_ref[...] = jnp.zeros_like(dgk_ref)
        dgq_ref[...] += dgq
        dgk_ref[...] += dgk

        @pl.when(pl.program_id(0) == _HP - 1)
        def _():
            for cp in carried():
                cp.wait()

    fixed = lambda p: (0, 0)
    pair = pl.BlockSpec((T, LANES), lambda p: (0, p))
    gsp = pl.BlockSpec((1, DH), fixed)
    ckspec = pl.BlockSpec((1, 2, T), lambda p: (p, 0, 0))
    anyspec = pl.BlockSpec(memory_space=pl.ANY)
    res = pl.pallas_call(
        body, name="attn_bwd", grid=(_HP,),
        in_specs=_qkv_specs() + [ckspec, gsp, gsp, pl.BlockSpec((1, T, 2), lambda p: (p, 0, 0)), pair]
        + [anyspec] * (n + ns),
        out_specs=[pair, pair, pair, ckspec, gsp, gsp] + [anyspec] * (n + ns),
        out_shape=[_sds((T, AW), BF16)] * 3 + [_sds((_HP, 2, T)), _sds((1, DH)), _sds((1, DH))]
        + [_sds((3, *p.shape[1:]), p.dtype) for p in parts] + [_sds((NDEV, *s.shape), s.dtype) for s in shards],
        scratch_shapes=[pltpu.VMEM((2, T, DH), F32), pltpu.VMEM((2, T, DH), F32), pltpu.VMEM((2, T), F32),
                        pltpu.SemaphoreType.DMA((3 * n,)), pltpu.SemaphoreType.DMA((3 * n,))] + _gather_first_scratch(ns),
        compiler_params=_cp(("arbitrary",)),
    )(z, z, z, ck, g_q, g_k, lse, datt, *parts, *shards)
    return (*res[:6], res[6:6 + n], res[6 + n:])


def _disc(lr, li, ls, br_t, bi_t):
    step = jnp.exp(ls)
    er = jnp.exp(lr * step)
    ab_re = er * jnp.cos(li * step)
    ab_im = er * jnp.sin(li * step)
    num_re = ab_re - 1.0
    num_im = ab_im
    den = lr * lr + li * li
    f_re = (num_re * lr + num_im * li) / den
    f_im = (num_im * lr - num_re * li) / den
    bb_re = f_re[:, None, :] * br_t - f_im[:, None, :] * bi_t
    bb_im = f_re[:, None, :] * bi_t + f_im[:, None, :] * br_t
    return ab_re, ab_im, bb_re, bb_im


def _disc_fwd(lr, li, ls, br_t, bi_t):
    def body(lr_ref, li_ref, ls_ref, br_ref, bi_ref, ar_ref, ai_ref, bbr_ref, bbi_ref):
        ar, ai, bbr, bbi = _disc(lr_ref[...], li_ref[...], ls_ref[...], br_ref[...], bi_ref[...])
        ar_ref[...] = ar
        ai_ref[...] = ai
        bbr_ref[...] = bbr
        bbi_ref[...] = bbi

    return pl.pallas_call(
        body, name="ssm_disc_fwd",
        out_shape=[_sds((NG, NP)), _sds((NG, NP)), _sds((NG, GS, NP)), _sds((NG, GS, NP))],
        compiler_params=_cp())(lr, li, ls, br_t, bi_t)


def _disc_bwd(lr, li, ls, br_t, bi_t, dar, dai, dbbr, dbbi):
    def body(lr_ref, li_ref, ls_ref, br_ref, bi_ref, dar_ref, dai_ref, dbbr_ref, dbbi_ref,
             dlr_ref, dli_ref, dls_ref, dbr_ref, dbi_ref):
        _, vjp = jax.vjp(_disc, lr_ref[...], li_ref[...], ls_ref[...], br_ref[...], bi_ref[...])
        dlr, dli, dls, dbr, dbi = vjp((dar_ref[...], dai_ref[...], dbbr_ref[...], dbbi_ref[...]))
        dlr_ref[...] = dlr
        dli_ref[...] = dli
        dls_ref[...] = dls
        dbr_ref[...] = dbr
        dbi_ref[...] = dbi

    return pl.pallas_call(
        body, name="ssm_disc_bwd",
        out_shape=[_sds((NG, NP)), _sds((NG, NP)), _sds((NG, 1)), _sds((NG, GS, NP)), _sds((NG, GS, NP))],
        compiler_params=_cp())(lr, li, ls, br_t, bi_t, dar, dai, dbbr, dbbi)


def _cmul(ar, ai, br, bi):
    return ar * br - ai * bi, ar * bi + ai * br


def _scan(buf, a_re, a_im, reverse, other=None):
    ar = [jnp.broadcast_to(a_re[:, LANES * k:LANES * (k + 1)], (NSEG, LANES)) for k in range(4)]
    ai = [jnp.broadcast_to(a_im[:, LANES * k:LANES * (k + 1)], (NSEG, LANES)) for k in range(4)]

    def tile(ref, i, k):
        return ref[pl.ds(pl.multiple_of(i * NSEG, NSEG), NSEG), LANES * k:LANES * (k + 1)]

    def advance(i, xr, xi):
        nr = [ar[k] * xr[k] - ai[k] * xi[k] + tile(buf, i, k) for k in range(4)]
        ni = [ar[k] * xi[k] + ai[k] * xr[k] + tile(buf, i, 4 + k) for k in range(4)]
        return nr, ni

    def index(i):
        return (SEG - 1 - i) if reverse else i

    zeros = tuple(jnp.zeros((NSEG, LANES), F32) for _ in range(4))

    def sweep1(i, carry):
        xr, xi = carry
        for sub in range(_SCAN_UNROLL):
            xr, xi = advance(index(i * _SCAN_UNROLL + sub), xr, xi)
        return tuple(xr), tuple(xi)

    er, ei = lax.fori_loop(0, SEG // _SCAN_UNROLL, sweep1, (zeros, zeros))

    row = lax.broadcasted_iota(jnp.int32, (NSEG, LANES), 0)
    sr, si = [], []
    for k in range(4):
        pr, pi = ar[k], ai[k]
        for _ in range(SEG.bit_length() - 1):
            pr, pi = _cmul(pr, pi, pr, pi)
        ir, ii = er[k], ei[k]
        for d in (1, 2, 4):
            shift = (NSEG - d) if reverse else d
            ok = (row < NSEG - d) if reverse else (row >= d)
            mr, mi = _cmul(pr, pi, pltpu.roll(ir, shift, 0), pltpu.roll(ii, shift, 0))
            ir = ir + jnp.where(ok, mr, 0.0)
            ii = ii + jnp.where(ok, mi, 0.0)
            pr, pi = _cmul(pr, pi, pr, pi)
        shift = (NSEG - 1) if reverse else 1
        ok = (row < NSEG - 1) if reverse else (row >= 1)
        sr.append(jnp.where(ok, pltpu.roll(ir, shift, 0), 0.0))
        si.append(jnp.where(ok, pltpu.roll(ii, shift, 0), 0.0))

    def sweep2(i, carry):
        xr, xi, dar, dai = carry
        for sub in range(_SCAN_UNROLL):
            idx = index(i * _SCAN_UNROLL + sub)
            if other is not None:
                orr = [tile(other, idx, k) for k in range(4)]
                oi = [tile(other, idx, 4 + k) for k in range(4)]
                dar = tuple(dar[k] + xr[k] * orr[k] + xi[k] * oi[k] for k in range(4))
                dai = tuple(dai[k] + xi[k] * orr[k] - xr[k] * oi[k] for k in range(4))
            xr, xi = advance(idx, xr, xi)
            start = pl.multiple_of(idx * NSEG, NSEG)
            for k in range(4):
                buf[pl.ds(start, NSEG), LANES * k:LANES * (k + 1)] = xr[k]
                buf[pl.ds(start, NSEG), LANES * (4 + k):LANES * (5 + k)] = xi[k]
        return tuple(xr), tuple(xi), dar, dai

    _, _, dar, dai = lax.fori_loop(0, SEG // _SCAN_UNROLL, sweep2, (tuple(sr), tuple(si), zeros, zeros))
    if other is None:
        return None
    da_re = jnp.concatenate([jnp.sum(d, axis=0, keepdims=True) for d in dar], axis=1)
    da_im = jnp.concatenate([jnp.sum(d, axis=0, keepdims=True) for d in dai], axis=1)
    return da_re, da_im


_SSM_BLOCKS = 4


def _seg_copies(hbm_ref, col0, buf, block, sems, to_hbm):
    cps = []
    for j in range(NSEG):
        hbm = hbm_ref.at[pl.ds(SEG * j, SEG), pl.ds(col0, LANES)]
        vmem = buf.at[block, :, j, :]
        cps.append(pltpu.make_async_copy(vmem, hbm, sems.at[block, j]) if to_hbm
                   else pltpu.make_async_copy(hbm, vmem, sems.at[block, j]))
    return cps


def _seg_scratch():
    return [pltpu.VMEM((_SSM_BLOCKS, SEG, NSEG, LANES), F32), pltpu.SemaphoreType.DMA((_SSM_BLOCKS, NSEG))]


_U_COL = 3 * AW


def _ssm_fwd(z, bb, ab, cc, dsk, shards, bufs):
    n = len(shards)
    last = _SSM_BLOCKS - 1

    def body(z_ref, bb_ref, a_ref, cc_ref, d_ref, *rest):
        x_refs, y_ref, xs_ref, out_refs = rest[:n], rest[2 * n], rest[2 * n + 1], rest[2 * n + 2:3 * n + 2]
        ubuf, lsem, ybuf, ssem = rest[3 * n + 2:3 * n + 6]
        sems = rest[3 * n + 6:]
        step = pl.program_id(0)

        @pl.when(step == 0)
        def _():
            for cp in _gather_first_copies(x_refs, out_refs, *sems, which=_FAR)[1]:
                cp.start()
            for b in range(_SSM_BLOCKS):
                for cp in _seg_copies(z_ref, _U_COL + LANES * b, ubuf, b, lsem, False):
                    cp.start()

        for cp in _seg_copies(z_ref, _U_COL, ubuf, step, lsem, False):
            cp.wait()
        ub = ubuf[step].reshape(T, LANES)
        xb = xs_ref.at[0]
        xb[...] = _nn(ub.astype(BF16), bb_ref[0])
        _scan(xb, a_ref[0, 0:1, :], a_ref[0, 1:2, :], reverse=False)
        y = _nn(xb[...].astype(BF16), cc_ref[0]) + d_ref[0] * ub
        ybuf[step] = y.reshape(SEG, NSEG, LANES)
        for cp in _seg_copies(y_ref, pl.multiple_of(step * LANES, LANES), ybuf, step, ssem, True):
            cp.start()

        @pl.when(step == last)
        def _():
            for b in range(_SSM_BLOCKS):
                for cp in _seg_copies(y_ref, LANES * b, ybuf, b, ssem, True):
                    cp.wait()
            for cp in _gather_first_copies(x_refs, out_refs, *sems, which=_FAR)[1]:
                cp.wait()

    blk = lambda j: (j, 0, 0)
    anyspec = pl.BlockSpec(memory_space=pl.ANY)
    res = pl.pallas_call(
        body, name="ssm_fwd", grid=(_SSM_BLOCKS,),
        in_specs=[anyspec, pl.BlockSpec((1, LANES, 1024), blk), pl.BlockSpec((1, 2, 512), blk),
                  pl.BlockSpec((1, 1024, LANES), blk), pl.BlockSpec((1, 1, LANES), blk)] + [anyspec] * (2 * n),
        out_specs=[anyspec, pl.BlockSpec((1, T, 1024), blk)] + [anyspec] * n,
        out_shape=[_sds((T, SW)), _sds((_SSM_BLOCKS, T, 1024))] + [_sds(b.shape, b.dtype) for b in bufs],
        input_output_aliases={5 + n + i: 2 + i for i in range(n)},
        scratch_shapes=_seg_scratch() + _seg_scratch() + _gather_first_scratch(n),
        compiler_params=_cp(("arbitrary",)),
    )(z, bb, ab, cc, dsk, *shards, *bufs)
    return res[0], res[1], res[2:]


def _ssm_bwd(z, dy, xs, bb, ab, cc, dsk, arrs, parts):
    n, m = len(arrs), len(parts)
    last = _SSM_BLOCKS - 1

    def body(z_ref, dy_ref, x_ref, bb_ref, a_ref, cc_ref, d_ref, *rest):
        a_refs, p_refs, rest = rest[:n], rest[n:n + m], rest[n + m:]
        du_ref, dbb_ref, da_ref, dcc_ref, dd_ref = rest[:5]
        fs_refs, fc_refs, gb = rest[5:5 + n], rest[5 + n:5 + n + m], rest[5 + n + m]
        ubuf, usem, dybuf, dysem, dubuf, dusem = rest[6 + n + m:12 + n + m]
        sems = rest[12 + n + m:]

        def carried():
            return _sibling_swap_copies(a_refs, fs_refs, *sems[:2]) + _chip_swap_copies(p_refs, fc_refs, *sems[2:])

        step = pl.program_id(0)

        @pl.when(step == 0)
        def _():
            for cp in carried():
                cp.start()
            for b in range(_SSM_BLOCKS):
                for cp in (_seg_copies(z_ref, _U_COL + LANES * b, ubuf, b, usem, False)
                           + _seg_copies(dy_ref, LANES * b, dybuf, b, dysem, False)):
                    cp.start()

        for cp in _seg_copies(z_ref, _U_COL, ubuf, step, usem, False) + _seg_copies(dy_ref, 0, dybuf, step, dysem, False):
            cp.wait()
        ub, dyv = ubuf[step].reshape(T, LANES), dybuf[step].reshape(T, LANES)
        ubb, dyb = ub.astype(BF16), dyv.astype(BF16)
        a_re, a_im = a_ref[0, 0:1, :], a_ref[0, 1:2, :]
        dcc_ref[0] = _tn(x_ref[0].astype(BF16), dyb)
        gb[...] = _nt(dyb, cc_ref[0])
        da_re, da_im = _scan(gb, a_re, -a_im, reverse=True, other=x_ref.at[0])
        da_ref[0] = jnp.concatenate([da_re, da_im], axis=0)
        gc = gb[...].astype(BF16)
        du = _nt(gc, bb_ref[0]) + d_ref[0] * dyv
        dubuf[step] = du.reshape(SEG, NSEG, LANES)
        for cp in _seg_copies(du_ref, pl.multiple_of(step * LANES, LANES), dubuf, step, dusem, True):
            cp.start()
        dbb_ref[0] = _tn(ubb, gc)
        dd_ref[0] = jnp.sum(dyv * ub, axis=0, keepdims=True)

        @pl.when(step == last)
        def _():
            for b in range(_SSM_BLOCKS):
                for cp in _seg_copies(du_ref, LANES * b, dubuf, b, dusem, True):
                    cp.wait()
            for cp in carried():
                cp.wait()

    blk = lambda j: (j, 0, 0)
    anyspec = pl.BlockSpec(memory_space=pl.ANY)
    res = pl.pallas_call(
        body, name="ssm_bwd", grid=(_SSM_BLOCKS,),
        in_specs=[anyspec, anyspec, pl.BlockSpec((1, T, 1024), blk), pl.BlockSpec((1, LANES, 1024), blk),
                  pl.BlockSpec((1, 2, 512), blk), pl.BlockSpec((1, 1024, LANES), blk), pl.BlockSpec((1, 1, LANES), blk)]
        + [anyspec] * (n + m),
        out_specs=[anyspec, pl.BlockSpec((1, LANES, 1024), blk), pl.BlockSpec((1, 2, 512), blk),
                   pl.BlockSpec((1, 1024, LANES), blk), pl.BlockSpec((1, 1, LANES), blk)] + [anyspec] * (n + m),
        out_shape=[_sds((T, SW)), _sds((4, LANES, 1024)), _sds((4, 2, 512)), _sds((4, 1024, LANES)), _sds((4, 1, LANES))]
        + [_sds((4, *a.shape[2:]), a.dtype) for a in arrs] + [_sds((3, *p.shape[1:]), p.dtype) for p in parts],
        scratch_shapes=[pltpu.VMEM((T, 1024), F32)] + _seg_scratch() + _seg_scratch() + _seg_scratch()
        + [pltpu.SemaphoreType.DMA((4 * n,)), pltpu.SemaphoreType.DMA((4 * n,)),
           pltpu.SemaphoreType.DMA((3 * m,)), pltpu.SemaphoreType.DMA((3 * m,))],
        compiler_params=_cp(("arbitrary",)),
    )(z, dy, xs, bb, ab, cc, dsk, *arrs, *parts)
    return (*res[:5], res[5:5 + n], res[5 + n:])


_TM_MIX = 512


def _mix_parts(att, y, wg, bg):
    y2 = jax.nn.gelu(y)
    y2b = y2.astype(BF16)
    sg = jax.nn.sigmoid(_nn(y2b, wg) + bg)
    ssm = y2 * sg
    ra, rs = _rms_r(att), _rms_r(ssm)
    return y2, y2b, sg, ssm, ra, rs


def _mix_fwd(att, y, x, wg, bg, ga, gs, wo, bufs):
    m = len(bufs)

    def body(att_ref, y_ref, x_ref, wg_ref, bg_ref, ga_ref, gs_ref, wo_ref, *rest):
        x1_ref, buf_refs, sems = rest[m], rest[m + 1:2 * m + 1], rest[2 * m + 1:]

        @pl.when(pl.program_id(0) == 0)
        def _():
            for cp in _pass_on_copies(buf_refs, *sems):
                cp.start()

        attv = att_ref[...]
        _, _, _, ssm, ra, rs = _mix_parts(attv, y_ref[...], wg_ref[...], bg_ref[...])
        ma = (attv * ra * ga_ref[...]).astype(BF16)
        ms = (ssm * rs * gs_ref[...]).astype(BF16)
        x1_ref[...] = x_ref[...] + _nn(ma, wo_ref[0:AW, :]) + _nn(ms, wo_ref[AW:D, :])

        @pl.when(pl.program_id(0) == T // _TM_MIX - 1)
        def _():
            for cp in _pass_on_copies(buf_refs, *sems):
                cp.wait()

    row = lambda i: (i, 0)
    fixed = lambda i: (0, 0)
    half = pl.BlockSpec((_TM_MIX, AW), row)
    vec = pl.BlockSpec((1, AW), fixed)
    anyspec = pl.BlockSpec(memory_space=pl.ANY)
    res = pl.pallas_call(
        body, name="mix_fwd", grid=(T // _TM_MIX,),
        in_specs=[half, half, pl.BlockSpec((_TM_MIX, D), row), pl.BlockSpec((SW, SW), fixed), vec, vec, vec,
                  pl.BlockSpec((D, D), fixed)] + [anyspec] * m,
        out_specs=[pl.BlockSpec((_TM_MIX, D), row)] + [anyspec] * m,
        out_shape=[_sds((T, D))] + [_sds(b.shape, b.dtype) for b in bufs],
        input_output_aliases={8 + i: 1 + i for i in range(m)},
        scratch_shapes=[pltpu.SemaphoreType.DMA((3 * m,)), pltpu.SemaphoreType.DMA((3 * m,))],
        compiler_params=_cp(("arbitrary",)),
    )(att, y, x, wg, bg, ga, gs, wo, *bufs)
    return res[0], res[1:]


def _mix_bwd(att, y, dx1, wg, bg, ga, gs, wo):
    last = T // _TM_MIX - 1

    def body(att_ref, y_ref, d_ref, wg_ref, bg_ref, ga_ref, gs_ref, wo_ref,
             datt_ref, dy_ref, dwg_ref, dbg_ref, dga_ref, dgs_ref, dwo_ref, dwg_acc, dwo_acc):
        attv, yv, db = att_ref[...], y_ref[...], d_ref[...].astype(BF16)
        y2, y2b, sg, ssm, ra, rs = _mix_parts(attv, yv, wg_ref[...], bg_ref[...])
        na, ns = attv * ra, ssm * rs
        ma = (na * ga_ref[...]).astype(BF16)
        ms = (ns * gs_ref[...]).astype(BF16)
        dma = _nt(db, wo_ref[0:AW, :])
        dms = _nt(db, wo_ref[AW:D, :])
        datt, dga = _rms_bwd(na, ra, ga_ref[...], dma)
        dssm, dgs = _rms_bwd(ns, rs, gs_ref[...], dms)
        dgl = dssm * y2 * sg * (1.0 - sg)
        dglb = dgl.astype(BF16)
        dy2 = dssm * sg + _nt(dglb, wg_ref[...])
        _, gelu_vjp = jax.vjp(jax.nn.gelu, yv)
        datt_ref[...] = datt
        dy_ref[...] = gelu_vjp(dy2)[0]

        @pl.when(pl.program_id(0) == 0)
        def _():
            dwg_acc[...] = jnp.zeros_like(dwg_acc)
            dwo_acc[...] = jnp.zeros_like(dwo_acc)
            dbg_ref[...] = jnp.zeros_like(dbg_ref)
            dga_ref[...] = jnp.zeros_like(dga_ref)
            dgs_ref[...] = jnp.zeros_like(dgs_ref)
        dwg_acc[...] += _tn(y2b, dglb)
        dbg_ref[...] += jnp.sum(dgl, axis=0, keepdims=True)
        dga_ref[...] += dga
        dgs_ref[...] += dgs
        dwo_acc[0:AW, :] += _tn(ma, db)
        dwo_acc[AW:D, :] += _tn(ms, db)

        @pl.when(pl.program_id(0) == last)
        def _():
            dwg_ref[...] = dwg_acc[...].astype(BF16)
            dwo_ref[...] = dwo_acc[...].astype(BF16)

    row = lambda i: (i, 0)
    fixed = lambda i: (0, 0)
    half = pl.BlockSpec((_TM_MIX, AW), row)
    vec = pl.BlockSpec((1, AW), fixed)
    sq = pl.BlockSpec((SW, SW), fixed)
    big = pl.BlockSpec((D, D), fixed)
    return pl.pallas_call(
        body, name="mix_bwd", grid=(T // _TM_MIX,),
        in_specs=[half, half, pl.BlockSpec((_TM_MIX, D), row), sq, vec, vec, vec, big],
        out_specs=[half, half, sq, vec, vec, vec, big],
        out_shape=[_sds((T, AW)), _sds((T, SW)), _sds((SW, SW), BF16), _sds((1, SW)), _sds((1, AW)), _sds((1, SW)),
                   _sds((D, D), BF16)],
        scratch_shapes=[pltpu.VMEM((SW, SW), F32), pltpu.VMEM((D, D), F32)],
        compiler_params=_cp(("arbitrary",)),
    )(att, y, dx1, wg, bg, ga, gs, wo)


_NCB = -(-CS // LANES)


def _ffn_up(x1, g, wu4, shards):
    tm = 1024
    n = len(shards)
    steps = T // tm

    def body(x_ref, g_ref, w_ref, *rest):
        x_refs, (h_ref, up_ref), out_refs, sems = rest[:n], rest[n:n + 2], rest[n + 2:2 * n + 2], rest[2 * n + 2:]
        i, e = pl.program_id(0), pl.program_id(1)

        @pl.when((i == 0) & (e == 0))
        def _():
            local, remote = _gather_first_copies(x_refs, out_refs, *sems)
            for cp in local + remote:
                cp.start()

        @pl.when(e == 0)
        def _():
            xv = x_ref[...]
            h_ref[...] = (xv * _rms_r(xv) * g_ref[...]).astype(BF16)
        up_ref[0, 0] = _nt(h_ref[...], w_ref[0, 0])

        @pl.when((i == steps - 1) & (e == NDEV - 1))
        def _():
            local, remote = _gather_first_copies(x_refs, out_refs, *sems)
            for cp in remote + local:
                cp.wait()

    anyspec = pl.BlockSpec(memory_space=pl.ANY)
    res = pl.pallas_call(
        body, name="ffn_up", grid=(steps, NDEV),
        in_specs=[pl.BlockSpec((tm, D), lambda i, e: (i, 0)), pl.BlockSpec((1, D), lambda i, e: (0, 0)),
                  pl.BlockSpec((1, 1, CS, D), lambda i, e: (e // 4, e % 4, 0, 0))] + [anyspec] * n,
        out_specs=[pl.BlockSpec((tm, D), lambda i, e: (i, 0)),
                   pl.BlockSpec((1, 1, tm, CS), lambda i, e: (e // 4, e % 4, i, 0))] + [anyspec] * n,
        out_shape=[_sds((T, D), BF16), _sds((2, 4, T, CS))] + [_sds((NDEV, *s.shape), s.dtype) for s in shards],
        scratch_shapes=_gather_first_scratch(n),
        compiler_params=_cp(("arbitrary", "arbitrary")),
    )(x1, g, wu4, *shards)
    return res[0], res[1], res[2:]


def _shift_down(h, k):
    row = lax.broadcasted_iota(jnp.int32, h.shape, 0)
    return jnp.where(row >= k, pltpu.roll(h, k, 0), 0.0)


def _shift_up(h, k):
    row = lax.broadcasted_iota(jnp.int32, h.shape, 0)
    return jnp.where(row < T - k, pltpu.roll(h, T - k, 0), 0.0)


def _conv(h, w, b):
    return w[0:1, :] * _shift_down(h, 2) + w[1:2, :] * _shift_down(h, 1) + w[2:3, :] * h + b


def _chan(gv, rows):
    return pl.BlockSpec((1, 1, rows, LANES), lambda d, j: (gv, d, 0, j))


def _ffn_act(up4, cw4, cb4, bufs):
    m = len(bufs)

    def body(ug_ref, uv_ref, wg_ref, wv_ref, bg_ref, bv_ref, *rest):
        act_ref, buf_refs, sems = rest[m], rest[m + 1:2 * m + 1], rest[2 * m + 1:]
        d, j = pl.program_id(0), pl.program_id(1)

        @pl.when((d == 0) & (j == 0))
        def _():
            for cp in _pass_on_copies(buf_refs, *sems):
                cp.start()

        g = _conv(ug_ref[0, 0], wg_ref[0, 0], bg_ref[0, 0])
        v = _conv(uv_ref[0, 0], wv_ref[0, 0], bv_ref[0, 0])
        act_ref[0] = (g * jax.nn.sigmoid(g) * v).astype(BF16)

        @pl.when((d == 3) & (j == _NCB - 1))
        def _():
            for cp in _pass_on_copies(buf_refs, *sems):
                cp.wait()

    anyspec = pl.BlockSpec(memory_space=pl.ANY)
    res = pl.pallas_call(
        body, name="ffn_act", grid=(4, _NCB),
        in_specs=[_chan(0, T), _chan(1, T), _chan(0, 3), _chan(1, 3), _chan(0, 1), _chan(1, 1)] + [anyspec] * m,
        out_specs=[pl.BlockSpec((1, T, LANES), lambda d, j: (d, 0, j))] + [anyspec] * m,
        out_shape=[_sds((4, T, CS), BF16)] + [_sds(b.shape, b.dtype) for b in bufs],
        input_output_aliases={6 + i: 1 + i for i in range(m)},
        scratch_shapes=[pltpu.SemaphoreType.DMA((3 * m,)), pltpu.SemaphoreType.DMA((3 * m,))],
        compiler_params=_cp(("arbitrary", "arbitrary")),
    )(up4, up4, cw4, cw4, cb4, cb4, *bufs)
    return res[0], res[1:]


def _ffn_act_bwd(up4, dact, cw4, cb4):
    def conv_bwd(h, w, d):
        dup = w[2:3, :] * d + w[1:2, :] * _shift_up(d, 1) + w[0:1, :] * _shift_up(d, 2)
        dw = jnp.concatenate([jnp.sum(d * _shift_down(h, 2), axis=0, keepdims=True),
                              jnp.sum(d * _shift_down(h, 1), axis=0, keepdims=True),
                              jnp.sum(d * h, axis=0, keepdims=True)], axis=0)
        return dup, dw, jnp.sum(d, axis=0, keepdims=True)

    def body(ug_ref, uv_ref, da_ref, wg_ref, wv_ref, bg_ref, bv_ref, dup_ref, dw_ref, db_ref):
        ug, uv, da = ug_ref[0, 0], uv_ref[0, 0], da_ref[0]
        g = _conv(ug, wg_ref[0, 0], bg_ref[0, 0])
        v = _conv(uv, wv_ref[0, 0], bv_ref[0, 0])
        sg = jax.nn.sigmoid(g)
        dg = da * v * (sg * (1.0 + g * (1.0 - sg)))
        dv = da * (g * sg)
        dug, dwg, dbg = conv_bwd(ug, wg_ref[0, 0], dg)
        duv, dwv, dbv = conv_bwd(uv, wv_ref[0, 0], dv)
        dup_ref[0, 0] = dug.astype(BF16)
        dup_ref[1, 0] = duv.astype(BF16)
        dw_ref[0, 0] = dwg
        dw_ref[1, 0] = dwv
        db_ref[0, 0] = dbg
        db_ref[1, 0] = dbv

    both = lambda rows: pl.BlockSpec((2, 1, rows, LANES), lambda d, j: (0, d, 0, j))
    return pl.pallas_call(
        body, name="ffn_act_bwd", grid=(4, _NCB),
        in_specs=[_chan(0, T), _chan(1, T), pl.BlockSpec((1, T, LANES), lambda d, j: (d, 0, j)),
                  _chan(0, 3), _chan(1, 3), _chan(0, 1), _chan(1, 1)],
        out_specs=[both(T), both(3), both(1)],
        out_shape=[_sds((2, 4, T, CS), BF16), _sds((2, 4, 3, CS)), _sds((2, 4, 1, CS))],
        compiler_params=_cp(("parallel", "parallel")),
    )(up4, up4, dact, cw4, cw4, cb4, cb4)


def _ffn_down_loss(act, wd4, x1, target):
    tm = 512

    def body(a_ref, w_ref, x1_ref, t_ref, loss_ref, dx_ref, dxb_ref):
        x2 = x1_ref[...]
        for d in range(4):
            x2 = x2 + _nn(a_ref[d], w_ref[d])
        err = x2 - t_ref[...]
        dx = err * (1.0 / D)
        dx_ref[...] = dx
        dxb_ref[...] = dx.astype(BF16)

        @pl.when(pl.program_id(0) == 0)
        def _():
            loss_ref[...] = jnp.zeros_like(loss_ref)
        loss_ref[...] += 0.5 * jnp.sum(jnp.mean(err * err, axis=-1, keepdims=True), axis=0, keepdims=True)

    row = lambda i: (i, 0)
    fixed = lambda i: (0, 0)
    return pl.pallas_call(
        body, name="ffn_down_loss", grid=(T // tm,),
        in_specs=[pl.BlockSpec((4, tm, CS), lambda i: (0, i, 0)), pl.BlockSpec((4, CS, D), lambda i: (0, 0, 0)),
                  pl.BlockSpec((tm, D), row), pl.BlockSpec((tm, D), row)],
        out_specs=[pl.BlockSpec((1, 1), fixed), pl.BlockSpec((tm, D), row), pl.BlockSpec((tm, D), row)],
        out_shape=[_sds((1, 1)), _sds((T, D)), _sds((T, D), BF16)],
        compiler_params=_cp(("arbitrary",)),
    )(act, wd4, x1, target)


def _ffn_dact(dx2b, wd4):
    tm = 1024

    def body(d_ref, w_ref, o_ref):
        o_ref[0] = _nt(d_ref[...], w_ref[0])

    return pl.pallas_call(
        body, name="ffn_dact", grid=(4, T // tm),
        in_specs=[pl.BlockSpec((tm, D), lambda d, i: (i, 0)), pl.BlockSpec((1, CS, D), lambda d, i: (d, 0, 0))],
        out_specs=pl.BlockSpec((1, tm, CS), lambda d, i: (d, i, 0)), out_shape=_sds((4, T, CS)),
        compiler_params=_cp(("parallel", "parallel")),
    )(dx2b, wd4)


def _ffn_dwd(act, dx2b):
    def body(a_ref, d_ref, o_ref):
        o_ref[0] = _tn(a_ref[0], d_ref[...]).astype(BF16)

    return pl.pallas_call(
        body, name="ffn_dwd", grid=(4,),
        in_specs=[pl.BlockSpec((1, T, CS), lambda d: (d, 0, 0)), pl.BlockSpec((T, D), lambda d: (0, 0))],
        out_specs=pl.BlockSpec((1, CS, D), lambda d: (d, 0, 0)), out_shape=_sds((4, CS, D), BF16),
        compiler_params=_cp(("parallel",)),
    )(act, dx2b)


def _ffn_dwu(h2, dup4):
    tn = 1024

    def body(h_ref, d_ref, o_ref):
        o_ref[0, 0] = _tn(d_ref[0, 0], h_ref[...]).astype(BF16)

    return pl.pallas_call(
        body, name="ffn_dwu", grid=(NDEV, D // tn),
        in_specs=[pl.BlockSpec((T, tn), lambda e, i: (0, i)),
                  pl.BlockSpec((1, 1, T, CS), lambda e, i: (e // 4, e % 4, 0, 0))],
        out_specs=pl.BlockSpec((1, 1, CS, tn), lambda e, i: (e // 4, e % 4, 0, i)),
        out_shape=_sds((2, 4, CS, D), BF16),
        compiler_params=_cp(("parallel", "parallel")),
    )(h2, dup4)


def _ffn_up_bwd(dup4, wu4, x1, g, dres, arrs):
    tm = 256
    n = len(arrs)

    def body(d_ref, w_ref, x_ref, g_ref, dres_ref, *rest):
        a_refs, (dx_ref, dg_ref), fs_refs, sems = rest[:n], rest[n:n + 2], rest[n + 2:2 * n + 2], rest[2 * n + 2:]

        @pl.when(pl.program_id(0) == 0)
        def _():
            for cp in _sibling_swap_copies(a_refs, fs_refs, *sems):
                cp.start()

        dh = jnp.zeros((tm, D), F32)
        for gv in range(2):
            for d in range(4):
                dh = dh + _nn(d_ref[gv, d], w_ref[gv, d])
        xv = x_ref[...]
        r = _rms_r(xv)
        dxr, dgp = _rms_bwd(xv * r, r, g_ref[...], dh)
        dx_ref[...] = dres_ref[...] + dxr

        @pl.when(pl.program_id(0) == 0)
        def _():
            dg_ref[...] = jnp.zeros_like(dg_ref)
        dg_ref[...] += dgp

        @pl.when(pl.program_id(0) == T // tm - 1)
        def _():
            for cp in _sibling_swap_copies(a_refs, fs_refs, *sems):
                cp.wait()

    row = lambda i: (i, 0)
    fixed = lambda i: (0, 0)
    anyspec = pl.BlockSpec(memory_space=pl.ANY)
    res = pl.pallas_call(
        body, name="ffn_up_bwd", grid=(T // tm,),
        in_specs=[pl.BlockSpec((2, 4, tm, CS), lambda i: (0, 0, i, 0)), pl.BlockSpec((2, 4, CS, D), lambda i: (0, 0, 0, 0)),
                  pl.BlockSpec((tm, D), row), pl.BlockSpec((1, D), fixed), pl.BlockSpec((tm, D), row)] + [anyspec] * n,
        out_specs=[pl.BlockSpec((tm, D), row), pl.BlockSpec((1, D), fixed)] + [anyspec] * n,
        out_shape=[_sds((T, D)), _sds((1, D))] + [_sds((4, *a.shape[2:]), a.dtype) for a in arrs],
        scratch_shapes=[pltpu.SemaphoreType.DMA((4 * n,)), pltpu.SemaphoreType.DMA((4 * n,))],
        compiler_params=_cp(("arbitrary",)),
    )(dup4, wu4, x1, g, dres, *arrs)
    return res[0], res[1], res[2:]


def _adamw_math(w, g, m, v):
    m = ADAM_B1 * m + (1.0 - ADAM_B1) * g
    v = ADAM_B2 * v + (1.0 - ADAM_B2) * jnp.square(g)
    m_hat = m / (1.0 - ADAM_B1 ** ADAM_STEP)
    v_hat = v / (1.0 - ADAM_B2 ** ADAM_STEP)
    delta = -ADAM_LR * (m_hat / (jnp.sqrt(v_hat) + ADAM_EPS) + ADAM_WD * w)
    return delta, m, v


def _where_am_i():
    x, y, c = _place()
    return jnp.stack([c, 2 * x + y]).astype(jnp.int32)


def _pair_sum(a4, recv, name):
    _, _, rows, cols = a4.shape
    tr = _row_tile(rows, cols, 3 << 20)

    def body(sel_ref, own_ref, recv_ref, out_ref):
        out_ref[...] = (own_ref[0].astype(F32) + recv_ref[...].astype(F32)).astype(out_ref.dtype)

    grid_spec = pltpu.PrefetchScalarGridSpec(
        num_scalar_prefetch=1, grid=(4, rows // tr),
        in_specs=[pl.BlockSpec((1, 1, tr, cols), lambda k, i, s: (k, s[0], i, 0)),
                  pl.BlockSpec((1, tr, cols), lambda k, i, s: (k, i, 0))],
        out_specs=pl.BlockSpec((1, tr, cols), lambda k, i, s: (k, i, 0)),
    )
    return pl.pallas_call(
        body, name=name, grid_spec=grid_spec, out_shape=_sds((4, rows, cols), a4.dtype),
        compiler_params=_cp(("arbitrary", "arbitrary")),
    )(_where_am_i(), a4, recv)


def _adamw_rs(a4, recv, from_chips, w, m, v, name):
    rows, cols = w.shape
    tr = _row_tile(rows, cols, 3 << 19)

    def body(sel_ref, own_ref, recv_ref, fc_ref, w_ref, m_ref, v_ref, g_ref, d_ref, nm_ref, nv_ref):
        g = own_ref[0, 0].astype(F32) + recv_ref[0].astype(F32)
        for j in range(3):
            g = g + fc_ref[j].astype(F32)
        d, nm, nv = _adamw_math(w_ref[...], g, m_ref[...], v_ref[...])
        g_ref[...] = g
        d_ref[...] = d
        nm_ref[...] = nm
        nv_ref[...] = nv

    flat = pl.BlockSpec((tr, cols), lambda i, s: (i, 0))
    grid_spec = pltpu.PrefetchScalarGridSpec(
        num_scalar_prefetch=1, grid=(rows // tr,),
        in_specs=[pl.BlockSpec((1, 1, tr, cols), lambda i, s: (s[1], s[0], i, 0)),
                  pl.BlockSpec((1, tr, cols), lambda i, s: (s[1], i, 0)),
                  pl.BlockSpec((3, tr, cols), lambda i, s: (0, i, 0)), flat, flat, flat],
        out_specs=[flat] * 4,
    )
    return pl.pallas_call(
        body, name=name, grid_spec=grid_spec, out_shape=[_sds((rows, cols))] * 4,
        compiler_params=_cp(("arbitrary",)),
    )(_where_am_i(), a4, recv, from_chips, w, m, v)


_SMALL_F32 = (("g_mix", (8, 128)), ("b_f", (1, 8)), ("g_q", (1, 64)), ("g_k", (1, 64)), ("lambda_re", (32, 64)),
              ("lambda_im", (32, 64)), ("log_step", (1, 32)), ("d_skip", (32, 16)), ("b_glu", (4, 128)),
              ("g_attn_out", (4, 128)), ("g_ssm_out", (4, 128)), ("g_ffn", (8, 128)), ("conv_b", (44, 128)))
_SMALL_PAIRS = (("b_re", "b_im"), ("c_re", "c_im"))
_PAIR_SHAPE = (NG * GS, NP)


def _ceil8(r):
    return -(-r // 8) * 8


def _small_names():
    return [n for n, _ in _SMALL_F32] + [n for pair in _SMALL_PAIRS for n in pair]


def _pack_small_grads(g, loss_dev):
    parts = [jnp.pad(g[n].reshape(r, c), ((0, _ceil8(r) - r), (0, LANES - c))) for n, (r, c) in _SMALL_F32]
    parts.append(jnp.pad(loss_dev, ((0, 7), (0, LANES - 1))))
    return jnp.concatenate(parts, axis=0)


def _pack_pair_grads(g):
    pairs = [jnp.concatenate([g[a].reshape(_PAIR_SHAPE), g[b].reshape(_PAIR_SHAPE)], axis=1) for a, b in _SMALL_PAIRS]
    return jnp.concatenate(pairs, axis=0).astype(BF16)


def _adamw_small(gall, gall_b, wmv):
    names = _small_names()
    shapes = [s for _, s in _SMALL_F32] + [_PAIR_SHAPE] * (2 * len(_SMALL_PAIRS))
    npar = len(names)

    def body(ga_ref, gb_ref, *rest):
        ins, loss_ref, outs = rest[:3 * npar], rest[3 * npar], rest[3 * npar + 1:]
        ga, gb = ga_ref[0], gb_ref[0].astype(F32)
        for dev in range(1, NDEV):
            ga = ga + ga_ref[dev]
            gb = gb + gb_ref[dev].astype(F32)
        grads, off = [], 0
        for _, (r, c) in _SMALL_F32:
            grads.append(ga[off:off + r, 0:c])
            off += _ceil8(r)
        loss_ref[...] = ga[off:off + 8]
        rows, cols = _PAIR_SHAPE
        for j in range(len(_SMALL_PAIRS)):
            for h in range(2):
                grads.append(gb[rows * j:rows * (j + 1), cols * h:cols * (h + 1)])
        for i, g in enumerate(grads):
            d, nm, nv = _adamw_math(ins[3 * i][...], g, ins[3 * i + 1][...], ins[3 * i + 2][...])
            outs[4 * i][...] = g
            outs[4 * i + 1][...] = d
            outs[4 * i + 2][...] = nm
            outs[4 * i + 3][...] = nv

    flat = [a for n in names for a in wmv[n]]
    out_shape = [_sds((8, LANES))] + [_sds(s) for s in shapes for _ in range(4)]
    res = pl.pallas_call(body, name="adamw_small", out_shape=out_shape, compiler_params=_cp())(gall, gall_b, *flat)
    return res[0], {n: res[1 + 4 * i:5 + 4 * i] for i, n in enumerate(names)}


def _block_diag(m):
    eye = jnp.eye(8, dtype=m.dtype)
    r, c = m.shape[2], m.shape[3]
    return (m[:, :, :, None, :] * eye[None, :, None, :, None]).reshape(4, 8 * r, 8 * c)


def _diag_blocks(dense, r, c):
    eye = jnp.eye(8, dtype=dense.dtype)
    return jnp.sum(dense.reshape(4, 8, r, 8, c) * eye[None, :, None, :, None], axis=3)


def kernel(x, g_mix, w_in, b_f, g_q, g_k, lambda_re, lambda_im, log_step, b_re, b_im, c_re, c_im, d_skip, w_glu, b_glu, g_attn_out, g_ssm_out, w_out, g_ffn, w_up, conv_w, conv_b, w_down, loss_target, m_g_mix, m_w_in, m_b_f, m_g_q, m_g_k, m_lambda_re, m_lambda_im, m_log_step, m_b_re, m_b_im, m_c_re, m_c_im, m_d_skip, m_w_glu, m_b_glu, m_g_attn_out, m_g_ssm_out, m_w_out, m_g_ffn, m_w_up, m_conv_w, m_conv_b, m_w_down, v_g_mix, v_w_in, v_b_f, v_g_q, v_g_k, v_lambda_re, v_lambda_im, v_log_step, v_b_re, v_b_im, v_c_re, v_c_im, v_d_skip, v_w_glu, v_b_glu, v_g_attn_out, v_g_ssm_out, v_w_out, v_g_ffn, v_w_up, v_conv_w, v_conv_b, v_w_down):
    weights = dict(g_mix=g_mix, w_in=w_in, b_f=b_f, g_q=g_q, g_k=g_k, lambda_re=lambda_re, lambda_im=lambda_im,
                   log_step=log_step, b_re=b_re, b_im=b_im, c_re=c_re, c_im=c_im, d_skip=d_skip, w_glu=w_glu,
                   b_glu=b_glu, g_attn_out=g_attn_out, g_ssm_out=g_ssm_out, w_out=w_out, g_ffn=g_ffn, w_up=w_up,
                   conv_w=conv_w, conv_b=conv_b, w_down=w_down)
    mom_m = dict(g_mix=m_g_mix, w_in=m_w_in, b_f=m_b_f, g_q=m_g_q, g_k=m_g_k, lambda_re=m_lambda_re,
                 lambda_im=m_lambda_im, log_step=m_log_step, b_re=m_b_re, b_im=m_b_im, c_re=m_c_re, c_im=m_c_im,
                 d_skip=m_d_skip, w_glu=m_w_glu, b_glu=m_b_glu, g_attn_out=m_g_attn_out, g_ssm_out=m_g_ssm_out,
                 w_out=m_w_out, g_ffn=m_g_ffn, w_up=m_w_up, conv_w=m_conv_w, conv_b=m_conv_b, w_down=m_w_down)
    mom_v = dict(g_mix=v_g_mix, w_in=v_w_in, b_f=v_b_f, g_q=v_g_q, g_k=v_g_k, lambda_re=v_lambda_re,
                 lambda_im=v_lambda_im, log_step=v_log_step, b_re=v_b_re, b_im=v_b_im, c_re=v_c_re, c_im=v_c_im,
                 d_skip=v_d_skip, w_glu=v_w_glu, b_glu=v_b_glu, g_attn_out=v_g_attn_out, g_ssm_out=v_g_ssm_out,
                 w_out=v_w_out, g_ffn=v_g_ffn, w_up=v_w_up, conv_w=v_conv_w, conv_b=v_conv_b, w_down=v_w_down)
    names = list(weights)
    big = ("w_in", "w_glu", "w_out", "w_up", "w_down", "conv_w")

    xs = x[0]
    target = loss_target[0]
    row = lambda a: a.reshape(1, -1)

    (win_g,) = _all_gather([w_in.T.astype(BF16)], "gather_w_in")
    w_in_t = win_g.reshape(8 * 257, D)
    wcat = jnp.concatenate([w_in_t[0:1536], w_in_t[1544:2056], jnp.pad(w_in_t[1536:1544], ((0, LANES - NH), (0, 0)))],
                           axis=0)
    cb4 = conv_b.reshape(2, 4, 1, CS)

    h1, z, f_t, (wg_g, wo_g) = _in_proj(xs, row(g_mix), wcat, [w_glu.astype(BF16), w_out.astype(BF16)])
    bf_col = b_f.reshape(NH, 1)
    ck = _gates_fwd(f_t, bf_col).reshape(_HP, 2, T)
    mlp_shards = [w_up.T.astype(BF16), conv_w]
    att, lse, (wu_g, cw_g), (wg_g, wo_g) = _attn_fwd(z, ck, row(g_q), row(g_k), mlp_shards, [wg_g, wo_g])

    ls_col = log_step.reshape(NG, 1)
    br_t, bi_t = b_re.transpose(0, 2, 1), b_im.transpose(0, 2, 1)
    ab_re, ab_im, bb_re, bb_im = _disc_fwd(lambda_re, lambda_im, ls_col, br_t, bi_t)
    bb = jnp.concatenate([_block_diag(bb_re.reshape(4, 8, GS, NP)), _block_diag(bb_im.reshape(4, 8, GS, NP))],
                         axis=2).astype(BF16)
    ab = jnp.stack([ab_re.reshape(4, 512), ab_im.reshape(4, 512)], axis=1)
    cdiag = lambda cm: _block_diag(cm.reshape(4, 8, GS, NP).transpose(0, 1, 3, 2))
    cc = jnp.concatenate([cdiag(c_re), -cdiag(c_im)], axis=1).astype(BF16)
    dsk = d_skip.reshape(4, 1, LANES)
    y, xs_il, (wu_g, cw_g) = _ssm_fwd(z, bb, ab, cc, dsk, mlp_shards, [wu_g, cw_g])
    wg_f = wg_g.reshape(SW, SW)
    wo_f = wo_g.reshape(D, D)
    x1, (wu_g, cw_g) = _mix_fwd(att, y, xs, wg_f, row(b_glu), row(g_attn_out), row(g_ssm_out), wo_f, [wu_g, cw_g])
    wu4 = wu_g.reshape(2, 4, CS, D)
    cw4 = cw_g.reshape(2, 4, 3, CS)
    h2, up4, (wd_g,) = _ffn_up(x1, row(g_ffn), wu4, [w_down.astype(BF16)])
    act, (wd_g,) = _ffn_act(up4, cw4, cb4, [wd_g])
    wd4 = wd_g.reshape(4, CS, D)
    loss_dev, dx2, dx2b = _ffn_down_loss(act, wd4, x1, target)

    dact = _ffn_dact(dx2b, wd4)
    d_wd = _ffn_dwd(act, dx2b)
    dup4, dcw4, dcb4 = _ffn_act_bwd(up4, dact, cw4, cb4)
    d_wu = _ffn_dwu(h2, dup4)
    mlp, mixer = ("w_up", "w_down", "conv_w"), ("w_out", "w_glu")
    early = mlp + mixer
    by_dest = {"w_up": d_wu.reshape(4, 2, CS, D), "w_down": d_wd.reshape(4, 2, DFF // NDEV, D),
               "conv_w": dcw4.reshape(4, 2, 3, CS)}
    dx1, dg_ffn, fs = _ffn_up_bwd(dup4, wu4, x1, row(g_ffn), dx2, [by_dest[n] for n in mlp])
    from_sibling = dict(zip(mlp, fs))
    chip_sums = {n: _pair_sum(by_dest[n], from_sibling[n], "rs_pair_sum_" + n) for n in mlp}
    datt, dy, d_wg, d_bg, d_ga, d_gs, d_wo = _mix_bwd(att, y, dx1, wg_f, row(b_glu), row(g_attn_out),
                                                       row(g_ssm_out), wo_f)
    by_dest.update({"w_out": d_wo.reshape(4, 2, D // NDEV, D), "w_glu": d_wg.reshape(4, 2, SW // NDEV, SW)})
    under_ssm = ("w_down", "conv_w")
    du, dbb, dab, dcc, ddsk, fs, fc = _ssm_bwd(z, dy, xs_il, bb, ab, cc, dsk, [by_dest[n] for n in mixer],
                                               [chip_sums[n] for n in under_ssm])
    from_sibling.update(zip(mixer, fs))
    from_chips = dict(zip(under_ssm, fc))
    chip_sums.update({n: _pair_sum(by_dest[n], from_sibling[n], "rs_pair_sum_" + n) for n in mixer})
    du = du.astype(BF16)
    dbb_re = _diag_blocks(dbb[:, :, 0:512], GS, NP).reshape(NG, GS, NP)
    dbb_im = _diag_blocks(dbb[:, :, 512:1024], GS, NP).reshape(NG, GS, NP)
    dlr, dli, dls, dbr_t, dbi_t = _disc_bwd(lambda_re, lambda_im, ls_col, br_t, bi_t,
                                            dab[:, 0].reshape(NG, NP), dab[:, 1].reshape(NG, NP), dbb_re, dbb_im)
    d_cre = _diag_blocks(dcc[:, 0:512], NP, GS).transpose(0, 1, 3, 2).reshape(NG, GS, NP)
    d_cim = -_diag_blocks(dcc[:, 512:1024], NP, GS).transpose(0, 1, 3, 2).reshape(NG, GS, NP)

    under_attn = ("w_up",) + mixer
    pair_grads = _pack_pair_grads(dict(b_re=dbr_t, b_im=dbi_t, c_re=d_cre, c_im=d_cim))
    dq, dk, dv, dck, d_gq, d_gk, fc, (small_all_b,) = _attn_bwd(z, ck, row(g_q), row(g_k), lse, datt,
                                                                [chip_sums[n] for n in under_attn], [pair_grads])
    from_chips.update(zip(under_attn, fc))
    df_t, d_bf = _gates_bwd(dck.reshape(NH, T), f_t, bf_col)
    df = jnp.concatenate([df_t.T, jnp.zeros((T, LANES - NH), F32)], axis=1).astype(BF16)
    d_wcat = _in_proj_dw(h1, dq, dk, dv, du, df)
    d_win = jnp.concatenate([d_wcat[0:1536], d_wcat[2048:2048 + NH], d_wcat[1536:2048]], axis=0)
    by_dest["w_in"] = d_win.astype(BF16).reshape(4, 2, 257, D)
    (from_sibling["w_in"],) = _swap_sibling([by_dest["w_in"]], "rs_pair_swap_w_in")
    chip_sums["w_in"] = _pair_sum(by_dest["w_in"], from_sibling["w_in"], "rs_pair_sum_w_in")
    grad_x, dg_mix, (from_chips["w_in"],), (small_all_b,) = _in_proj_bwd(
        dq, dk, dv, du, df, wcat, xs, row(g_mix), dx1, [chip_sums["w_in"]], [small_all_b])

    grads, deltas, new_m, new_v = {}, {}, {}, {}
    for n in early + ("w_in",):
        flip = (lambda a: a.T) if n in ("w_up", "w_in") else (lambda a: a)
        outs = _adamw_rs(by_dest[n], from_sibling[n], from_chips[n], flip(weights[n]), flip(mom_m[n]), flip(mom_v[n]),
                         "adamw_" + n)
        grads[n], deltas[n], new_m[n], new_v[n] = (flip(o) for o in outs)

    small = dict(g_mix=dg_mix, b_f=d_bf, g_q=d_gq, g_k=d_gk, lambda_re=dlr, lambda_im=dli, log_step=dls,
                 b_re=dbr_t, b_im=dbi_t, c_re=d_cre, c_im=d_cim, d_skip=ddsk, b_glu=d_bg, g_attn_out=d_ga,
                 g_ssm_out=d_gs, g_ffn=dg_ffn, conv_b=dcb4)
    (small_all,) = _all_gather([_pack_small_grads(small, loss_dev)], "gather_small_grads")
    views = dict(_SMALL_F32)
    swapped = ("b_re", "b_im")

    def view(n, a):
        return a.transpose(0, 2, 1).reshape(_PAIR_SHAPE) if n in swapped else a.reshape(views.get(n, _PAIR_SHAPE))

    loss_p, small_out = _adamw_small(small_all, small_all_b,
                                     {n: (view(n, weights[n]), view(n, mom_m[n]), view(n, mom_v[n])) for n in small})
    loss = loss_p[0, 0]
    for n in small:
        back = (lambda a: a.reshape(NG, GS, NP).transpose(0, 2, 1)) if n in swapped else (lambda a: a.reshape(weights[n].shape))
        grads[n], deltas[n], new_m[n], new_v[n] = (back(o) for o in small_out[n])

    return (loss, grad_x[None], *[grads[n] for n in names], *[deltas[n] for n in names],
            *[new_m[n] for n in names], *[new_v[n] for n in names])
```

```python
import jax
import jax.numpy as jnp
from jax import lax
from jax.experimental import pallas as pl
from jax.experimental.pallas import tpu as pltpu

F32, BF16 = jnp.float32, jnp.bfloat16
T, D = 2048, 1024
NH, DH = 8, 64
AW, SW = 512, 512
NG, GS, NP = 32, 16, 64
DFF = 2816
NDEV = 8
CS = 2 * DFF // NDEV
EPS = 1e-6
NEG = -1e30
ZC = 4 * 512 + 128
SEG = 256
NSEG = T // SEG
_SCAN_UNROLL = 4
BQ = 512
VMEM_LIMIT = 56 * 1024 * 1024
LANES = 128
MESH = pl.DeviceIdType.MESH
HI = lax.Precision.HIGHEST

ADAM_LR, ADAM_B1, ADAM_B2, ADAM_EPS, ADAM_WD, ADAM_STEP = 0.001, 0.9, 0.999, 1e-08, 0.01, 10


def _cp(dims=None):
    if dims is None:
        return pltpu.CompilerParams(vmem_limit_bytes=VMEM_LIMIT)
    return pltpu.CompilerParams(dimension_semantics=dims, vmem_limit_bytes=VMEM_LIMIT)


def _sds(shape, dtype=F32):
    return jax.ShapeDtypeStruct(shape, dtype)


def _nt(a, b):
    return lax.dot_general(a, b, (((1,), (1,)), ((), ())), preferred_element_type=F32)


def _tn(a, b):
    return lax.dot_general(a, b, (((0,), (0,)), ((), ())), preferred_element_type=F32)


def _nn(a, b):
    return jnp.dot(a, b, preferred_element_type=F32)


def _rms_r(x):
    return lax.rsqrt(jnp.mean(x * x, axis=-1, keepdims=True) + EPS)


def _rms_bwd(n, r, g, dh):
    dn = dh * g
    dx = r * (dn - n * jnp.mean(dn * n, axis=-1, keepdims=True))
    return dx, jnp.sum(dh * n, axis=0, keepdims=True)


def _row_tile(rows, cols, limit):
    tr = rows
    while tr * cols * 4 > limit and tr % 32 == 0:
        tr //= 2
    return tr


def _place():
    return lax.axis_index("x"), lax.axis_index("y"), lax.axis_index("c")


def _all_gather(shards, name):
    n = len(shards)

    def body(*refs):
        x_refs, out_refs = refs[:n], refs[n:2 * n]
        send_sems, recv_sems, local_sems = refs[2 * n:]
        x, y, c = _place()
        me, sibling = (x, y, c), (x, y, 1 - c)
        chips = [(1 - x, y), (x, 1 - y), (1 - x, 1 - y)]

        def copy(a, k, block, to, src=None):
            px, py, pc = block
            slot = out_refs[a].at[4 * px + 2 * py + pc]
            return pltpu.make_async_remote_copy(
                src_ref=slot if src is None else src, dst_ref=slot,
                send_sem=send_sems.at[7 * a + k], recv_sem=recv_sems.at[7 * a + k], device_id=to, device_id_type=MESH)

        mine, first, passed = [], [], []
        for a in range(n):
            cp = pltpu.make_async_copy(x_refs[a], out_refs[a].at[4 * x + 2 * y + c], local_sems.at[a])
            cp.start()
            mine.append(cp)
            cps = [copy(a, 0, me, sibling, src=x_refs[a])]
            cps += [copy(a, 1 + j, me, (*chip, c), src=x_refs[a]) for j, chip in enumerate(chips)]
            for cp in cps:
                cp.start()
            first += cps
        for a in range(n):
            for j, chip in enumerate(chips):
                copy(a, 1 + j, (*chip, c), me).wait_recv()
                fwd = copy(a, 4 + j, (*chip, c), sibling)
                fwd.start()
                passed.append(fwd)
        for a in range(n):
            copy(a, 0, sibling, me).wait_recv()
            for j, chip in enumerate(chips):
                copy(a, 4 + j, (*chip, 1 - c), me).wait_recv()
        for cp in first + passed:
            cp.wait_send()
        for cp in mine:
            cp.wait()

    anyspec = pl.BlockSpec(memory_space=pl.ANY)
    return pl.pallas_call(
        body, name=name,
        out_shape=[_sds((NDEV, *s.shape), s.dtype) for s in shards],
        in_specs=[anyspec] * n, out_specs=[anyspec] * n,
        scratch_shapes=[pltpu.SemaphoreType.DMA((7 * n,)), pltpu.SemaphoreType.DMA((7 * n,)),
                        pltpu.SemaphoreType.DMA((n,))],
    )(*shards)


_NEAR, _FAR = (0, 1, 2), (3,)


def _gather_first_copies(x_refs, out_refs, send_sems, recv_sems, local_sems, which=_NEAR + _FAR):
    x, y, c = _place()
    slot = 4 * x + 2 * y + c
    peers = [(x, y, 1 - c), (1 - x, y, c), (x, 1 - y, c), (1 - x, 1 - y, c)]
    local, remote = [], []
    for a, (x_ref, out_ref) in enumerate(zip(x_refs, out_refs)):
        if 0 in which:
            local.append(pltpu.make_async_copy(x_ref, out_ref.at[slot], local_sems.at[a]))
        for k in which:
            remote.append(pltpu.make_async_remote_copy(
                src_ref=x_ref, dst_ref=out_ref.at[slot], send_sem=send_sems.at[4 * a + k],
                recv_sem=recv_sems.at[4 * a + k], device_id=peers[k], device_id_type=MESH))
    return local, remote


def _gather_first_scratch(n):
    return [pltpu.SemaphoreType.DMA((4 * n,)), pltpu.SemaphoreType.DMA((4 * n,)), pltpu.SemaphoreType.DMA((n,))]


def _chip_swap_copies(p_refs, out_refs, send_sems, recv_sems):
    x, y, c = _place()
    chips = [(1 - x, y), (x, 1 - y), (1 - x, 1 - y)]
    cps = []
    for a, (p_ref, out_ref) in enumerate(zip(p_refs, out_refs)):
        for j, (px, py) in enumerate(chips):
            cps.append(pltpu.make_async_remote_copy(
                src_ref=p_ref.at[2 * px + py], dst_ref=out_ref.at[j],
                send_sem=send_sems.at[3 * a + j], recv_sem=recv_sems.at[3 * a + j],
                device_id=(px, py, c), device_id_type=MESH))
    return cps


def _sibling_swap_copies(a_refs, out_refs, send_sems, recv_sems):
    x, y, c = _place()
    cps = []
    for a, (a_ref, out_ref) in enumerate(zip(a_refs, out_refs)):
        for k in range(4):
            cps.append(pltpu.make_async_remote_copy(
                src_ref=a_ref.at[k, 1 - c], dst_ref=out_ref.at[k],
                send_sem=send_sems.at[4 * a + k], recv_sem=recv_sems.at[4 * a + k],
                device_id=(x, y, 1 - c), device_id_type=MESH))
    return cps


def _pass_on_copies(buf_refs, send_sems, recv_sems):
    x, y, c = _place()
    chips = [(1 - x, y), (x, 1 - y), (1 - x, 1 - y)]
    cps = []
    for a, buf in enumerate(buf_refs):
        for j, (px, py) in enumerate(chips):
            block = buf.at[4 * px + 2 * py + c]
            cps.append(pltpu.make_async_remote_copy(
                src_ref=block, dst_ref=block, send_sem=send_sems.at[3 * a + j], recv_sem=recv_sems.at[3 * a + j],
                device_id=(x, y, 1 - c), device_id_type=MESH))
    return cps


def _swap_sibling(arrs, name):
    n = len(arrs)

    def body(*refs):
        cps = _sibling_swap_copies(refs[:n], refs[n:2 * n], *refs[2 * n:])
        for cp in cps:
            cp.start()
        for cp in cps:
            cp.wait()

    anyspec = pl.BlockSpec(memory_space=pl.ANY)
    return pl.pallas_call(
        body, name=name,
        out_shape=[_sds((4, *a.shape[2:]), a.dtype) for a in arrs],
        in_specs=[anyspec] * n, out_specs=[anyspec] * n,
        scratch_shapes=[pltpu.SemaphoreType.DMA((4 * n,)), pltpu.SemaphoreType.DMA((4 * n,))],
    )(*arrs)


def _in_proj(x, g, wcat, shards):
    tm = 256
    n = len(shards)

    def body(x_ref, g_ref, w_ref, *rest):
        x_refs, (h_ref, z_ref, ft_ref), out_refs, sems = rest[:n], rest[n:n + 3], rest[n + 3:2 * n + 3], rest[2 * n + 3:]

        @pl.when(pl.program_id(0) == 0)
        def _():
            local, remote = _gather_first_copies(x_refs, out_refs, *sems)
            for cp in local + remote:
                cp.start()

        xv = x_ref[...]
        h = (xv * _rms_r(xv) * g_ref[...]).astype(BF16)
        h_ref[...] = h
        z = _nt(h, w_ref[...])
        z_ref[...] = z
        ft_ref[...] = z[:, 2048:ZC].T[0:NH, :]

        @pl.when(pl.program_id(0) == T // tm - 1)
        def _():
            local, remote = _gather_first_copies(x_refs, out_refs, *sems)
            for cp in remote + local:
                cp.wait()

    row = lambda i: (i, 0)
    fixed = lambda i: (0, 0)
    anyspec = pl.BlockSpec(memory_space=pl.ANY)
    res = pl.pallas_call(
        body, name="in_proj", grid=(T // tm,),
        in_specs=[pl.BlockSpec((tm, D), row), pl.BlockSpec((1, D), fixed), pl.BlockSpec((ZC, D), fixed)] + [anyspec] * n,
        out_specs=[pl.BlockSpec((tm, D), row), pl.BlockSpec((tm, ZC), row), pl.BlockSpec((NH, tm), lambda i: (0, i))]
        + [anyspec] * n,
        out_shape=[_sds((T, D), BF16), _sds((T, ZC)), _sds((NH, T))] + [_sds((NDEV, *s.shape), s.dtype) for s in shards],
        scratch_shapes=_gather_first_scratch(n),
        compiler_params=_cp(("arbitrary",)),
    )(x, g, wcat, *shards)
    return res[0], res[1], res[2], res[3:]


def _in_proj_dw(h, dq, dk, dv, du, df):
    tm = 512

    def body(h_ref, dq_ref, dk_ref, dv_ref, du_ref, df_ref, o_ref):
        hv = h_ref[...]
        for j, d_ref in enumerate((dq_ref, dk_ref, dv_ref, du_ref)):
            o_ref[512 * j:512 * (j + 1), :] = _tn(d_ref[...], hv)
        o_ref[2048:ZC, :] = _tn(df_ref[...], hv)

    full = lambda w: pl.BlockSpec((T, w), lambda i: (0, 0))
    return pl.pallas_call(
        body, name="in_proj_dw", grid=(D // tm,),
        in_specs=[pl.BlockSpec((T, tm), lambda i: (0, i)), full(512), full(512), full(512), full(512), full(LANES)],
        out_specs=pl.BlockSpec((ZC, tm), lambda i: (0, i)), out_shape=_sds((ZC, D)),
        compiler_params=_cp(("parallel",)),
    )(h, dq, dk, dv, du, df)


def _in_proj_bwd(dq, dk, dv, du, df, wcat, x, g, dres, parts, bufs):
    tm = 256
    n, nb = len(parts), len(bufs)

    def body(dq_ref, dk_ref, dv_ref, du_ref, df_ref, w_ref, x_ref, g_ref, dres_ref, *rest):
        p_refs, (dx_ref, dg_ref) = rest[:n], rest[n + nb:n + nb + 2]
        fc_refs, buf_refs, sems = rest[n + nb + 2:2 * n + nb + 2], rest[2 * n + nb + 2:2 * n + 2 * nb + 2], rest[2 * n + 2 * nb + 2:]

        def carried():
            return _chip_swap_copies(p_refs, fc_refs, *sems[:2]) + _pass_on_copies(buf_refs, *sems[2:])

        @pl.when(pl.program_id(0) == 0)
        def _():
            for cp in carried():
                cp.start()

        dh = _nn(df_ref[...], w_ref[2048:ZC, :])
        for j, d_ref in enumerate((dq_ref, dk_ref, dv_ref, du_ref)):
            dh = dh + _nn(d_ref[...], w_ref[512 * j:512 * (j + 1), :])
        xv = x_ref[...]
        r = _rms_r(xv)
        dxr, dgp = _rms_bwd(xv * r, r, g_ref[...], dh)
        dx_ref[...] = dres_ref[...] + dxr

        @pl.when(pl.program_id(0) == 0)
        def _():
            dg_ref[...] = jnp.zeros_like(dg_ref)
        dg_ref[...] += dgp

        @pl.when(pl.program_id(0) == T // tm - 1)
        def _():
            for cp in carried():
                cp.wait()

    row = lambda i: (i, 0)
    fixed = lambda i: (0, 0)
    part = pl.BlockSpec((tm, 512), row)
    anyspec = pl.BlockSpec(memory_space=pl.ANY)
    res = pl.pallas_call(
        body, name="in_proj_bwd", grid=(T // tm,),
        in_specs=[part, part, part, part, pl.BlockSpec((tm, LANES), row), pl.BlockSpec((ZC, D), fixed),
                  pl.BlockSpec((tm, D), row), pl.BlockSpec((1, D), fixed), pl.BlockSpec((tm, D), row)]
        + [anyspec] * (n + nb),
        out_specs=[pl.BlockSpec((tm, D), row), pl.BlockSpec((1, D), fixed)] + [anyspec] * (n + nb),
        out_shape=[_sds((T, D)), _sds((1, D))] + [_sds((3, *p.shape[1:]), p.dtype) for p in parts]
        + [_sds(b.shape, b.dtype) for b in bufs],
        input_output_aliases={9 + n + i: 2 + n + i for i in range(nb)},
        scratch_shapes=[pltpu.SemaphoreType.DMA((3 * n,)), pltpu.SemaphoreType.DMA((3 * n,)),
                        pltpu.SemaphoreType.DMA((3 * nb,)), pltpu.SemaphoreType.DMA((3 * nb,))],
        compiler_params=_cp(("arbitrary",)),
    )(dq, dk, dv, du, df, wcat, x, g, dres, *parts, *bufs)
    return res[0], res[1], res[2:2 + n], res[2 + n:]


def _stack_lanes(v):
    return jnp.concatenate([v[:, LANES * b:LANES * (b + 1)] for b in range(T // LANES)], axis=0)


def _unstack_lanes(s):
    return jnp.concatenate([s[8 * b:8 * b + 8, :] for b in range(T // LANES)], axis=1)


def _cumsum_lanes(v, reverse):
    st = _stack_lanes(v)
    ri = lax.broadcasted_iota(jnp.int32, (LANES, LANES), 0)
    ci = lax.broadcasted_iota(jnp.int32, (LANES, LANES), 1)
    tri = (ri >= ci) if reverse else (ri <= ci)
    intra = jnp.dot(st, tri.astype(F32), precision=HI, preferred_element_type=F32)
    tot = intra[:, 0:1] if reverse else intra[:, LANES - 1:LANES]
    same_head = (ci % 8) == (ri % 8)
    other = (ci // 8 > ri // 8) if reverse else (ci // 8 < ri // 8)
    off = jnp.dot((same_head & other).astype(F32), jnp.broadcast_to(tot, (LANES, LANES)), precision=HI,
                  preferred_element_type=F32)
    return _unstack_lanes(intra + off)


def _gates_fwd(f_t, b_f):
    def body(f_ref, b_ref, c_ref):
        z = f_ref[...] + b_ref[...]
        lf = jnp.minimum(z, 0.0) - jnp.log(1.0 + jnp.exp(-jnp.abs(z)))
        c_ref[...] = _cumsum_lanes(lf, reverse=False)

    return pl.pallas_call(body, name="gates_fwd", out_shape=_sds((NH, T)), compiler_params=_cp())(f_t, b_f)


def _gates_bwd(dck, f_t, b_f):
    def body(d_ref, f_ref, b_ref, df_ref, db_ref):
        z = f_ref[...] + b_ref[...]
        dlf = _cumsum_lanes(d_ref[...], reverse=True)
        df = dlf * jax.nn.sigmoid(-z)
        df_ref[...] = df
        db_ref[...] = jnp.sum(df, axis=1, keepdims=True)

    return pl.pallas_call(body, name="gates_bwd", out_shape=[_sds((NH, T)), _sds((NH, 1))],
                          compiler_params=_cp())(dck, f_t, b_f)


def _lower_triangle():
    rows = lax.broadcasted_iota(jnp.int32, (BQ, BQ), 0)
    cols = lax.broadcasted_iota(jnp.int32, (BQ, BQ), 1)
    return cols <= rows


def _logit_parts(qb, kb, ck_row, lo, hi, tri):
    parts = []
    if lo > 0:
        parts.append((_nt(qb[lo:hi], kb[0:lo]) - ck_row[:, 0:lo], 0, lo))
    parts.append((jnp.where(tri, _nt(qb[lo:hi], kb[lo:hi]) - ck_row[:, lo:hi], NEG), lo, hi))
    return parts


_HP = NH // 2


def _qkv_specs():
    return [pl.BlockSpec((T, LANES), lambda p: (0, p)), pl.BlockSpec((T, LANES), lambda p: (0, _HP + p)),
            pl.BlockSpec((T, LANES), lambda p: (0, 2 * _HP + p))]


def _attn_fwd(z, ck, g_q, g_k, shards, bufs):
    scale = DH ** -0.5
    n, m = len(shards), len(bufs)

    def body(q_ref, k_ref, v_ref, ck_ref, gq_ref, gk_ref, *rest):
        x_refs, (o_ref, lse_ref) = rest[:n], rest[n + m:n + m + 2]
        out_refs, buf_refs, sems = rest[n + m + 2:2 * n + m + 2], rest[2 * n + m + 2:2 * n + 2 * m + 2], rest[2 * n + 2 * m + 2:]

        @pl.when(pl.program_id(0) == 0)
        def _():
            local, remote = _gather_first_copies(x_refs, out_refs, *sems[:3], which=_NEAR)
            for cp in local + remote + _pass_on_copies(buf_refs, *sems[3:]):
                cp.start()

        qb, kb, vb = [], [], []
        for hh in range(2):
            cols = slice(DH * hh, DH * (hh + 1))
            qv, kv = q_ref[:, cols], k_ref[:, cols]
            qb.append((qv * _rms_r(qv) * gq_ref[...] * scale).astype(BF16))
            kb.append((kv * _rms_r(kv) * gk_ref[...]).astype(BF16))
            vb.append(v_ref[:, cols].astype(BF16))
        tri = _lower_triangle()
        for i in range(T // BQ):
            lo, hi = i * BQ, (i + 1) * BQ
            outs = []
            for hh in range(2):
                parts = _logit_parts(qb[hh], kb[hh], ck_ref[0, hh:hh + 1, :], lo, hi, tri)
                top = jnp.max(parts[-1][0], axis=-1, keepdims=True)
                if lo > 0:
                    top = jnp.maximum(top, jnp.max(parts[0][0], axis=-1, keepdims=True))
                l, o = 0.0, 0.0
                for s, k0, k1 in parts:
                    p = jnp.exp(s - top)
                    l = l + jnp.sum(p, axis=-1, keepdims=True)
                    o = o + _nn(p.astype(BF16), vb[hh][k0:k1])
                outs.append(o / l)
                lse_ref[0, lo:hi, hh:hh + 1] = top + jnp.log(l)
            o_ref[lo:hi, :] = jnp.concatenate(outs, axis=1)

        @pl.when(pl.program_id(0) == _HP - 1)
        def _():
            local, remote = _gather_first_copies(x_refs, out_refs, *sems[:3], which=_NEAR)
            for cp in remote + local + _pass_on_copies(buf_refs, *sems[3:]):
                cp.wait()

    fixed = lambda p: (0, 0)
    anyspec = pl.BlockSpec(memory_space=pl.ANY)
    res = pl.pallas_call(
        body, name="attn_fwd", grid=(_HP,),
        in_specs=_qkv_specs() + [pl.BlockSpec((1, 2, T), lambda p: (p, 0, 0)), pl.BlockSpec((1, DH), fixed),
                                 pl.BlockSpec((1, DH), fixed)] + [anyspec] * (n + m),
        out_specs=[pl.BlockSpec((T, LANES), lambda p: (0, p)), pl.BlockSpec((1, T, 2), lambda p: (p, 0, 0))]
        + [anyspec] * (n + m),
        out_shape=[_sds((T, AW)), _sds((_HP, T, 2))] + [_sds((NDEV, *s.shape), s.dtype) for s in shards]
        + [_sds(b.shape, b.dtype) for b in bufs],
        input_output_aliases={6 + n + i: 2 + n + i for i in range(m)},
        scratch_shapes=_gather_first_scratch(n) + [pltpu.SemaphoreType.DMA((3 * m,)), pltpu.SemaphoreType.DMA((3 * m,))],
        compiler_params=_cp(("arbitrary",)),
    )(z, z, z, ck, g_q, g_k, *shards, *bufs)
    return res[0], res[1], res[2:2 + n], res[2 + n:]


def _attn_bwd(z, ck, g_q, g_k, lse, datt, parts, shards):
    scale = DH ** -0.5
    n, ns = len(parts), len(shards)

    def body(q_ref, k_ref, v_ref, ck_ref, gq_ref, gk_ref, lse_ref, do_ref, *rest):
        p_refs, x_refs, rest = rest[:n], rest[n:n + ns], rest[n + ns:]
        dq_ref, dk_ref, dv_ref, dck_ref, dgq_ref, dgk_ref = rest[:6]
        fc_refs, out_refs = rest[6:6 + n], rest[6 + n:6 + n + ns]
        (dkn_acc, dv_acc, dc_acc), sems = rest[6 + n + ns:9 + n + ns], rest[9 + n + ns:]

        def carried():
            local, remote = _gather_first_copies(x_refs, out_refs, *sems[2:])
            return _chip_swap_copies(p_refs, fc_refs, *sems[:2]) + local + remote

        @pl.when(pl.program_id(0) == 0)
        def _():
            for cp in carried():
                cp.start()

        nq, nk, rq, rk, qb, kb, vb, dob = [], [], [], [], [], [], [], []
        for hh in range(2):
            cols = slice(DH * hh, DH * (hh + 1))
            qv, kv = q_ref[:, cols], k_ref[:, cols]
            rq.append(_rms_r(qv))
            rk.append(_rms_r(kv))
            nq.append(qv * rq[hh])
            nk.append(kv * rk[hh])
            qb.append((nq[hh] * gq_ref[...] * scale).astype(BF16))
            kb.append((nk[hh] * gk_ref[...]).astype(BF16))
            vb.append(v_ref[:, cols].astype(BF16))
            dob.append(do_ref[:, cols].astype(BF16))
        dkn_acc[...] = jnp.zeros_like(dkn_acc)
        dv_acc[...] = jnp.zeros_like(dv_acc)
        dc_acc[...] = jnp.zeros_like(dc_acc)
        dgq = jnp.zeros((1, DH), F32)
        tri = _lower_triangle()
        for i in range(T // BQ):
            lo, hi = i * BQ, (i + 1) * BQ
            dqs = []
            for hh in range(2):
                lse = lse_ref[0, lo:hi, hh:hh + 1]
                pd = []
                for s, k0, k1 in _logit_parts(qb[hh], kb[hh], ck_ref[0, hh:hh + 1, :], lo, hi, tri):
                    pd.append((jnp.exp(s - lse), _nt(dob[hh][lo:hi], vb[hh][k0:k1]), k0, k1))
                mean = sum(jnp.sum(p * dp, axis=-1, keepdims=True) for p, dp, _, _ in pd)
                dqn = 0.0
                for p, dp, k0, k1 in pd:
                    ds = p * (dp - mean)
                    dsb = ds.astype(BF16)
                    dv_acc[hh, k0:k1, :] += _tn(p.astype(BF16), dob[hh][lo:hi])
                    dkn_acc[hh, k0:k1, :] += _tn(dsb, qb[hh][lo:hi])
                    dc_acc[hh:hh + 1, k0:k1] -= jnp.sum(ds, axis=0, keepdims=True)
                    dqn = dqn + _nn(dsb, kb[hh][k0:k1])
                dqx, dgp = _rms_bwd(nq[hh][lo:hi], rq[hh][lo:hi], gq_ref[...], dqn * scale)
                dqs.append(dqx)
                dgq = dgq + dgp
            dq_ref[lo:hi, :] = jnp.concatenate(dqs, axis=1).astype(BF16)
        dks, dgk = [], jnp.zeros((1, DH), F32)
        for hh in range(2):
            dkx, dgp = _rms_bwd(nk[hh], rk[hh], gk_ref[...], dkn_acc[hh])
            dks.append(dkx)
            dgk = dgk + dgp
        dk_ref[...] = jnp.concatenate(dks, axis=1).astype(BF16)
        dv_ref[...] = jnp.concatenate([dv_acc[0], dv_acc[1]], axis=1).astype(BF16)
        dck_ref[0] = dc_acc[...]

        @pl.when(pl.program_id(0) == 0)
        def _():
            dgq_ref[...] = jnp.zeros_like(dgq_ref)
            dgk_ref[...] = jnp.zeros_like(dgk_ref)
        dgq_ref[...] += dgq
        dgk_ref[...] += dgk

        @pl.when(pl.program_id(0) == _HP - 1)
        def _():
            for cp in carried():
                cp.wait()

    fixed = lambda p: (0, 0)
    pair = pl.BlockSpec((T, LANES), lambda p: (0, p))
    gsp = pl.BlockSpec((1, DH), fixed)
    ckspec = pl.BlockSpec((1, 2, T), lambda p: (p, 0, 0))
    anyspec = pl.BlockSpec(memory_space=pl.ANY)
    res = pl.pallas_call(
        body, name="attn_bwd", grid=(_HP,),
        in_specs=_qkv_specs() + [ckspec, gsp, gsp, pl.BlockSpec((1, T, 2), lambda p: (p, 0, 0)), pair]
        + [anyspec] * (n + ns),
        out_specs=[pair, pair, pair, ckspec, gsp, gsp] + [anyspec] * (n + ns),
        out_shape=[_sds((T, AW), BF16)] * 3 + [_sds((_HP, 2, T)), _sds((1, DH)), _sds((1, DH))]
        + [_sds((3, *p.shape[1:]), p.dtype) for p in parts] + [_sds((NDEV, *s.shape), s.dtype) for s in shards],
        scratch_shapes=[pltpu.VMEM((2, T, DH), F32), pltpu.VMEM((2, T, DH), F32), pltpu.VMEM((2, T), F32),
                        pltpu.SemaphoreType.DMA((3 * n,)), pltpu.SemaphoreType.DMA((3 * n,))] + _gather_first_scratch(ns),
        compiler_params=_cp(("arbitrary",)),
    )(z, z, z, ck, g_q, g_k, lse, datt, *parts, *shards)
    return (*res[:6], res[6:6 + n], res[6 + n:])


def _disc(lr, li, ls, br_t, bi_t):
    step = jnp.exp(ls)
    er = jnp.exp(lr * step)
    ab_re = er * jnp.cos(li * step)
    ab_im = er * jnp.sin(li * step)
    num_re = ab_re - 1.0
    num_im = ab_im
    den = lr * lr + li * li
    f_re = (num_re * lr + num_im * li) / den
    f_im = (num_im * lr - num_re * li) / den
    bb_re = f_re[:, None, :] * br_t - f_im[:, None, :] * bi_t
    bb_im = f_re[:, None, :] * bi_t + f_im[:, None, :] * br_t
    return ab_re, ab_im, bb_re, bb_im


def _disc_fwd(lr, li, ls, br_t, bi_t):
    def body(lr_ref, li_ref, ls_ref, br_ref, bi_ref, ar_ref, ai_ref, bbr_ref, bbi_ref):
        ar, ai, bbr, bbi = _disc(lr_ref[...], li_ref[...], ls_ref[...], br_ref[...], bi_ref[...])
        ar_ref[...] = ar
        ai_ref[...] = ai
        bbr_ref[...] = bbr
        bbi_ref[...] = bbi

    return pl.pallas_call(
        body, name="ssm_disc_fwd",
        out_shape=[_sds((NG, NP)), _sds((NG, NP)), _sds((NG, GS, NP)), _sds((NG, GS, NP))],
        compiler_params=_cp())(lr, li, ls, br_t, bi_t)


def _disc_bwd(lr, li, ls, br_t, bi_t, dar, dai, dbbr, dbbi):
    def body(lr_ref, li_ref, ls_ref, br_ref, bi_ref, dar_ref, dai_ref, dbbr_ref, dbbi_ref,
             dlr_ref, dli_ref, dls_ref, dbr_ref, dbi_ref):
        _, vjp = jax.vjp(_disc, lr_ref[...], li_ref[...], ls_ref[...], br_ref[...], bi_ref[...])
        dlr, dli, dls, dbr, dbi = vjp((dar_ref[...], dai_ref[...], dbbr_ref[...], dbbi_ref[...]))
        dlr_ref[...] = dlr
        dli_ref[...] = dli
        dls_ref[...] = dls
        dbr_ref[...] = dbr
        dbi_ref[...] = dbi

    return pl.pallas_call(
        body, name="ssm_disc_bwd",
        out_shape=[_sds((NG, NP)), _sds((NG, NP)), _sds((NG, 1)), _sds((NG, GS, NP)), _sds((NG, GS, NP))],
        compiler_params=_cp())(lr, li, ls, br_t, bi_t, dar, dai, dbbr, dbbi)


def _cmul(ar, ai, br, bi):
    return ar * br - ai * bi, ar * bi + ai * br


def _scan(buf, a_re, a_im, reverse, other=None):
    ar = [jnp.broadcast_to(a_re[:, LANES * k:LANES * (k + 1)], (NSEG, LANES)) for k in range(4)]
    ai = [jnp.broadcast_to(a_im[:, LANES * k:LANES * (k + 1)], (NSEG, LANES)) for k in range(4)]

    def tile(ref, i, k):
        return ref[pl.ds(pl.multiple_of(i * NSEG, NSEG), NSEG), LANES * k:LANES * (k + 1)]

    def advance(i, xr, xi):
        nr = [ar[k] * xr[k] - ai[k] * xi[k] + tile(buf, i, k) for k in range(4)]
        ni = [ar[k] * xi[k] + ai[k] * xr[k] + tile(buf, i, 4 + k) for k in range(4)]
        return nr, ni

    def index(i):
        return (SEG - 1 - i) if reverse else i

    zeros = tuple(jnp.zeros((NSEG, LANES), F32) for _ in range(4))

    def sweep1(i, carry):
        xr, xi = carry
        for sub in range(_SCAN_UNROLL):
            xr, xi = advance(index(i * _SCAN_UNROLL + sub), xr, xi)
        return tuple(xr), tuple(xi)

    er, ei = lax.fori_loop(0, SEG // _SCAN_UNROLL, sweep1, (zeros, zeros))

    row = lax.broadcasted_iota(jnp.int32, (NSEG, LANES), 0)
    sr, si = [], []
    for k in range(4):
        pr, pi = ar[k], ai[k]
        for _ in range(SEG.bit_length() - 1):
            pr, pi = _cmul(pr, pi, pr, pi)
        ir, ii = er[k], ei[k]
        for d in (1, 2, 4):
            shift = (NSEG - d) if reverse else d
            ok = (row < NSEG - d) if reverse else (row >= d)
            mr, mi = _cmul(pr, pi, pltpu.roll(ir, shift, 0), pltpu.roll(ii, shift, 0))
            ir = ir + jnp.where(ok, mr, 0.0)
            ii = ii + jnp.where(ok, mi, 0.0)
            pr, pi = _cmul(pr, pi, pr, pi)
        shift = (NSEG - 1) if reverse else 1
        ok = (row < NSEG - 1) if reverse else (row >= 1)
        sr.append(jnp.where(ok, pltpu.roll(ir, shift, 0), 0.0))
        si.append(jnp.where(ok, pltpu.roll(ii, shift, 0), 0.0))

    def sweep2(i, carry):
        xr, xi, dar, dai = carry
        for sub in range(_SCAN_UNROLL):
            idx = index(i * _SCAN_UNROLL + sub)
            if other is not None:
                orr = [tile(other, idx, k) for k in range(4)]
                oi = [tile(other, idx, 4 + k) for k in range(4)]
                dar = tuple(dar[k] + xr[k] * orr[k] + xi[k] * oi[k] for k in range(4))
                dai = tuple(dai[k] + xi[k] * orr[k] - xr[k] * oi[k] for k in range(4))
            xr, xi = advance(idx, xr, xi)
            start = pl.multiple_of(idx * NSEG, NSEG)
            for k in range(4):
                buf[pl.ds(start, NSEG), LANES * k:LANES * (k + 1)] = xr[k]
                buf[pl.ds(start, NSEG), LANES * (4 + k):LANES * (5 + k)] = xi[k]
        return tuple(xr), tuple(xi), dar, dai

    _, _, dar, dai = lax.fori_loop(0, SEG // _SCAN_UNROLL, sweep2, (tuple(sr), tuple(si), zeros, zeros))
    if other is None:
        return None
    da_re = jnp.concatenate([jnp.sum(d, axis=0, keepdims=True) for d in dar], axis=1)
    da_im = jnp.concatenate([jnp.sum(d, axis=0, keepdims=True) for d in dai], axis=1)
    return da_re, da_im


_SSM_BLOCKS = 4


def _seg_copies(hbm_ref, col0, buf, block, sems, to_hbm):
    cps = []
    for j in range(NSEG):
        hbm = hbm_ref.at[pl.ds(SEG * j, SEG), pl.ds(col0, LANES)]
        vmem = buf.at[block, :, j, :]
        cps.append(pltpu.make_async_copy(vmem, hbm, sems.at[block, j]) if to_hbm
                   else pltpu.make_async_copy(hbm, vmem, sems.at[block, j]))
    return cps


def _seg_scratch():
    return [pltpu.VMEM((_SSM_BLOCKS, SEG, NSEG, LANES), F32), pltpu.SemaphoreType.DMA((_SSM_BLOCKS, NSEG))]


_U_COL = 3 * AW


def _ssm_fwd(z, bb, ab, cc, dsk, shards, bufs):
    n = len(shards)
    last = _SSM_BLOCKS - 1

    def body(z_ref, bb_ref, a_ref, cc_ref, d_ref, *rest):
        x_refs, y_ref, xs_ref, out_refs = rest[:n], rest[2 * n], rest[2 * n + 1], rest[2 * n + 2:3 * n + 2]
        ubuf, lsem, ybuf, ssem = rest[3 * n + 2:3 * n + 6]
        sems = rest[3 * n + 6:]
        step = pl.program_id(0)

        @pl.when(step == 0)
        def _():
            for cp in _gather_first_copies(x_refs, out_refs, *sems, which=_FAR)[1]:
                cp.start()
            for b in range(_SSM_BLOCKS):
                for cp in _seg_copies(z_ref, _U_COL + LANES * b, ubuf, b, lsem, False):
                    cp.start()

        for cp in _seg_copies(z_ref, _U_COL, ubuf, step, lsem, False):
            cp.wait()
        ub = ubuf[step].reshape(T, LANES)
        xb = xs_ref.at[0]
        xb[...] = _nn(ub.astype(BF16), bb_ref[0])
        _scan(xb, a_ref[0, 0:1, :], a_ref[0, 1:2, :], reverse=False)
        y = _nn(xb[...].astype(BF16), cc_ref[0]) + d_ref[0] * ub
        ybuf[step] = y.reshape(SEG, NSEG, LANES)
        for cp in _seg_copies(y_ref, pl.multiple_of(step * LANES, LANES), ybuf, step, ssem, True):
            cp.start()

        @pl.when(step == last)
        def _():
            for b in range(_SSM_BLOCKS):
                for cp in _seg_copies(y_ref, LANES * b, ybuf, b, ssem, True):
                    cp.wait()
            for cp in _gather_first_copies(x_refs, out_refs, *sems, which=_FAR)[1]:
                cp.wait()

    blk = lambda j: (j, 0, 0)
    anyspec = pl.BlockSpec(memory_space=pl.ANY)
    res = pl.pallas_call(
        body, name="ssm_fwd", grid=(_SSM_BLOCKS,),
        in_specs=[anyspec, pl.BlockSpec((1, LANES, 1024), blk), pl.BlockSpec((1, 2, 512), blk),
                  pl.BlockSpec((1, 1024, LANES), blk), pl.BlockSpec((1, 1, LANES), blk)] + [anyspec] * (2 * n),
        out_specs=[anyspec, pl.BlockSpec((1, T, 1024), blk)] + [anyspec] * n,
        out_shape=[_sds((T, SW)), _sds((_SSM_BLOCKS, T, 1024))] + [_sds(b.shape, b.dtype) for b in bufs],
        input_output_aliases={5 + n + i: 2 + i for i in range(n)},
        scratch_shapes=_seg_scratch() + _seg_scratch() + _gather_first_scratch(n),
        compiler_params=_cp(("arbitrary",)),
    )(z, bb, ab, cc, dsk, *shards, *bufs)
    return res[0], res[1], res[2:]


def _ssm_bwd(z, dy, xs, bb, ab, cc, dsk, arrs, parts):
    n, m = len(arrs), len(parts)
    last = _SSM_BLOCKS - 1

    def body(z_ref, dy_ref, x_ref, bb_ref, a_ref, cc_ref, d_ref, *rest):
        a_refs, p_refs, rest = rest[:n], rest[n:n + m], rest[n + m:]
        du_ref, dbb_ref, da_ref, dcc_ref, dd_ref = rest[:5]
        fs_refs, fc_refs, gb = rest[5:5 + n], rest[5 + n:5 + n + m], rest[5 + n + m]
        ubuf, usem, dybuf, dysem, dubuf, dusem = rest[6 + n + m:12 + n + m]
        sems = rest[12 + n + m:]

        def carried():
            return _sibling_swap_copies(a_refs, fs_refs, *sems[:2]) + _chip_swap_copies(p_refs, fc_refs, *sems[2:])

        step = pl.program_id(0)

        @pl.when(step == 0)
        def _():
            for cp in carried():
                cp.start()
            for b in range(_SSM_BLOCKS):
                for cp in (_seg_copies(z_ref, _U_COL + LANES * b, ubuf, b, usem, False)
                           + _seg_copies(dy_ref, LANES * b, dybuf, b, dysem, False)):
                    cp.start()

        for cp in _seg_copies(z_ref, _U_COL, ubuf, step, usem, False) + _seg_copies(dy_ref, 0, dybuf, step, dysem, False):
            cp.wait()
        ub, dyv = ubuf[step].reshape(T, LANES), dybuf[step].reshape(T, LANES)
        ubb, dyb = ub.astype(BF16), dyv.astype(BF16)
        a_re, a_im = a_ref[0, 0:1, :], a_ref[0, 1:2, :]
        dcc_ref[0] = _tn(x_ref[0].astype(BF16), dyb)
        gb[...] = _nt(dyb, cc_ref[0])
        da_re, da_im = _scan(gb, a_re, -a_im, reverse=True, other=x_ref.at[0])
        da_ref[0] = jnp.concatenate([da_re, da_im], axis=0)
        gc = gb[...].astype(BF16)
        du = _nt(gc, bb_ref[0]) + d_ref[0] * dyv
        dubuf[step] = du.reshape(SEG, NSEG, LANES)
        for cp in _seg_copies(du_ref, pl.multiple_of(step * LANES, LANES), dubuf, step, dusem, True):
            cp.start()
        dbb_ref[0] = _tn(ubb, gc)
        dd_ref[0] = jnp.sum(dyv * ub, axis=0, keepdims=True)

        @pl.when(step == last)
        def _():
            for b in range(_SSM_BLOCKS):
                for cp in _seg_copies(du_ref, LANES * b, dubuf, b, dusem, True):
                    cp.wait()
            for cp in carried():
                cp.wait()

    blk = lambda j: (j, 0, 0)
    anyspec = pl.BlockSpec(memory_space=pl.ANY)
    res = pl.pallas_call(
        body, name="ssm_bwd", grid=(_SSM_BLOCKS,),
        in_specs=[anyspec, anyspec, pl.BlockSpec((1, T, 1024), blk), pl.BlockSpec((1, LANES, 1024), blk),
                  pl.BlockSpec((1, 2, 512), blk), pl.BlockSpec((1, 1024, LANES), blk), pl.BlockSpec((1, 1, LANES), blk)]
        + [anyspec] * (n + m),
        out_specs=[anyspec, pl.BlockSpec((1, LANES, 1024), blk), pl.BlockSpec((1, 2, 512), blk),
                   pl.BlockSpec((1, 1024, LANES), blk), pl.BlockSpec((1, 1, LANES), blk)] + [anyspec] * (n + m),
        out_shape=[_sds((T, SW)), _sds((4, LANES, 1024)), _sds((4, 2, 512)), _sds((4, 1024, LANES)), _sds((4, 1, LANES))]
        + [_sds((4, *a.shape[2:]), a.dtype) for a in arrs] + [_sds((3, *p.shape[1:]), p.dtype) for p in parts],
        scratch_shapes=[pltpu.VMEM((T, 1024), F32)] + _seg_scratch() + _seg_scratch() + _seg_scratch()
        + [pltpu.SemaphoreType.DMA((4 * n,)), pltpu.SemaphoreType.DMA((4 * n,)),
           pltpu.SemaphoreType.DMA((3 * m,)), pltpu.SemaphoreType.DMA((3 * m,))],
        compiler_params=_cp(("arbitrary",)),
    )(z, dy, xs, bb, ab, cc, dsk, *arrs, *parts)
    return (*res[:5], res[5:5 + n], res[5 + n:])


_TM_MIX = 512


def _mix_parts(att, y, wg, bg):
    y2 = jax.nn.gelu(y)
    y2b = y2.astype(BF16)
    sg = jax.nn.sigmoid(_nn(y2b, wg) + bg)
    ssm = y2 * sg
    ra, rs = _rms_r(att), _rms_r(ssm)
    return y2, y2b, sg, ssm, ra, rs


def _mix_fwd(att, y, x, wg, bg, ga, gs, wo, bufs):
    m = len(bufs)

    def body(att_ref, y_ref, x_ref, wg_ref, bg_ref, ga_ref, gs_ref, wo_ref, *rest):
        x1_ref, buf_refs, sems = rest[m], rest[m + 1:2 * m + 1], rest[2 * m + 1:]

        @pl.when(pl.program_id(0) == 0)
        def _():
            for cp in _pass_on_copies(buf_refs, *sems):
                cp.start()

        attv = att_ref[...]
        _, _, _, ssm, ra, rs = _mix_parts(attv, y_ref[...], wg_ref[...], bg_ref[...])
        ma = (attv * ra * ga_ref[...]).astype(BF16)
        ms = (ssm * rs * gs_ref[...]).astype(BF16)
        x1_ref[...] = x_ref[...] + _nn(ma, wo_ref[0:AW, :]) + _nn(ms, wo_ref[AW:D, :])

        @pl.when(pl.program_id(0) == T // _TM_MIX - 1)
        def _():
            for cp in _pass_on_copies(buf_refs, *sems):
                cp.wait()

    row = lambda i: (i, 0)
    fixed = lambda i: (0, 0)
    half = pl.BlockSpec((_TM_MIX, AW), row)
    vec = pl.BlockSpec((1, AW), fixed)
    anyspec = pl.BlockSpec(memory_space=pl.ANY)
    res = pl.pallas_call(
        body, name="mix_fwd", grid=(T // _TM_MIX,),
        in_specs=[half, half, pl.BlockSpec((_TM_MIX, D), row), pl.BlockSpec((SW, SW), fixed), vec, vec, vec,
                  pl.BlockSpec((D, D), fixed)] + [anyspec] * m,
        out_specs=[pl.BlockSpec((_TM_MIX, D), row)] + [anyspec] * m,
        out_shape=[_sds((T, D))] + [_sds(b.shape, b.dtype) for b in bufs],
        input_output_aliases={8 + i: 1 + i for i in range(m)},
        scratch_shapes=[pltpu.SemaphoreType.DMA((3 * m,)), pltpu.SemaphoreType.DMA((3 * m,))],
        compiler_params=_cp(("arbitrary",)),
    )(att, y, x, wg, bg, ga, gs, wo, *bufs)
    return res[0], res[1:]


def _mix_bwd(att, y, dx1, wg, bg, ga, gs, wo):
    last = T // _TM_MIX - 1

    def body(att_ref, y_ref, d_ref, wg_ref, bg_ref, ga_ref, gs_ref, wo_ref,
             datt_ref, dy_ref, dwg_ref, dbg_ref, dga_ref, dgs_ref, dwo_ref, dwg_acc, dwo_acc):
        attv, yv, db = att_ref[...], y_ref[...], d_ref[...].astype(BF16)
        y2, y2b, sg, ssm, ra, rs = _mix_parts(attv, yv, wg_ref[...], bg_ref[...])
        na, ns = attv * ra, ssm * rs
        ma = (na * ga_ref[...]).astype(BF16)
        ms = (ns * gs_ref[...]).astype(BF16)
        dma = _nt(db, wo_ref[0:AW, :])
        dms = _nt(db, wo_ref[AW:D, :])
        datt, dga = _rms_bwd(na, ra, ga_ref[...], dma)
        dssm, dgs = _rms_bwd(ns, rs, gs_ref[...], dms)
        dgl = dssm * y2 * sg * (1.0 - sg)
        dglb = dgl.astype(BF16)
        dy2 = dssm * sg + _nt(dglb, wg_ref[...])
        _, gelu_vjp = jax.vjp(jax.nn.gelu, yv)
        datt_ref[...] = datt
        dy_ref[...] = gelu_vjp(dy2)[0]

        @pl.when(pl.program_id(0) == 0)
        def _():
            dwg_acc[...] = jnp.zeros_like(dwg_acc)
            dwo_acc[...] = jnp.zeros_like(dwo_acc)
            dbg_ref[...] = jnp.zeros_like(dbg_ref)
            dga_ref[...] = jnp.zeros_like(dga_ref)
            dgs_ref[...] = jnp.zeros_like(dgs_ref)
        dwg_acc[...] += _tn(y2b, dglb)
        dbg_ref[...] += jnp.sum(dgl, axis=0, keepdims=True)
        dga_ref[...] += dga
        dgs_ref[...] += dgs
        dwo_acc[0:AW, :] += _tn(ma, db)
        dwo_acc[AW:D, :] += _tn(ms, db)

        @pl.when(pl.program_id(0) == last)
        def _():
            dwg_ref[...] = dwg_acc[...].astype(BF16)
            dwo_ref[...] = dwo_acc[...].astype(BF16)

    row = lambda i: (i, 0)
    fixed = lambda i: (0, 0)
    half = pl.BlockSpec((_TM_MIX, AW), row)
    vec = pl.BlockSpec((1, AW), fixed)
    sq = pl.BlockSpec((SW, SW), fixed)
    big = pl.BlockSpec((D, D), fixed)
    return pl.pallas_call(
        body, name="mix_bwd", grid=(T // _TM_MIX,),
        in_specs=[half, half, pl.BlockSpec((_TM_MIX, D), row), sq, vec, vec, vec, big],
        out_specs=[half, half, sq, vec, vec, vec, big],
        out_shape=[_sds((T, AW)), _sds((T, SW)), _sds((SW, SW), BF16), _sds((1, SW)), _sds((1, AW)), _sds((1, SW)),
                   _sds((D, D), BF16)],
        scratch_shapes=[pltpu.VMEM((SW, SW), F32), pltpu.VMEM((D, D), F32)],
        compiler_params=_cp(("arbitrary",)),
    )(att, y, dx1, wg, bg, ga, gs, wo)


_NCB = -(-CS // LANES)


def _ffn_up(x1, g, wu4, shards):
    tm = 2048
    n = len(shards)
    steps = T // tm

    def body(x_ref, g_ref, w_ref, *rest):
        x_refs, (h_ref, up_ref), out_refs, sems = rest[:n], rest[n:n + 2], rest[n + 2:2 * n + 2], rest[2 * n + 2:]
        i, e = pl.program_id(0), pl.program_id(1)

        @pl.when((i == 0) & (e == 0))
        def _():
            local, remote = _gather_first_copies(x_refs, out_refs, *sems)
            for cp in local + remote:
                cp.start()

        @pl.when(e == 0)
        def _():
            xv = x_ref[...]
            h_ref[...] = (xv * _rms_r(xv) * g_ref[...]).astype(BF16)
        up_ref[0, 0] = _nt(h_ref[...], w_ref[0, 0])

        @pl.when((i == steps - 1) & (e == NDEV - 1))
        def _():
            local, remote = _gather_first_copies(x_refs, out_refs, *sems)
            for cp in remote + local:
                cp.wait()

    anyspec = pl.BlockSpec(memory_space=pl.ANY)
    res = pl.pallas_call(
        body, name="ffn_up", grid=(steps, NDEV),
        in_specs=[pl.BlockSpec((tm, D), lambda i, e: (i, 0)), pl.BlockSpec((1, D), lambda i, e: (0, 0)),
                  pl.BlockSpec((1, 1, CS, D), lambda i, e: (e // 4, e % 4, 0, 0))] + [anyspec] * n,
        out_specs=[pl.BlockSpec((tm, D), lambda i, e: (i, 0)),
                   pl.BlockSpec((1, 1, tm, CS), lambda i, e: (e // 4, e % 4, i, 0))] + [anyspec] * n,
        out_shape=[_sds((T, D), BF16), _sds((2, 4, T, CS))] + [_sds((NDEV, *s.shape), s.dtype) for s in shards],
        scratch_shapes=_gather_first_scratch(n),
        compiler_params=_cp(("arbitrary", "arbitrary")),
    )(x1, g, wu4, *shards)
    return res[0], res[1], res[2:]


def _shift_down(h, k):
    row = lax.broadcasted_iota(jnp.int32, h.shape, 0)
    return jnp.where(row >= k, pltpu.roll(h, k, 0), 0.0)


def _shift_up(h, k):
    row = lax.broadcasted_iota(jnp.int32, h.shape, 0)
    return jnp.where(row < T - k, pltpu.roll(h, T - k, 0), 0.0)


def _conv(h, w, b):
    return w[0:1, :] * _shift_down(h, 2) + w[1:2, :] * _shift_down(h, 1) + w[2:3, :] * h + b


def _chan(gv, rows):
    return pl.BlockSpec((1, 1, rows, LANES), lambda d, j: (gv, d, 0, j))


def _ffn_act(up4, cw4, cb4, bufs):
    m = len(bufs)

    def body(ug_ref, uv_ref, wg_ref, wv_ref, bg_ref, bv_ref, *rest):
        act_ref, buf_refs, sems = rest[m], rest[m + 1:2 * m + 1], rest[2 * m + 1:]
        d, j = pl.program_id(0), pl.program_id(1)

        @pl.when((d == 0) & (j == 0))
        def _():
            for cp in _pass_on_copies(buf_refs, *sems):
                cp.start()

        g = _conv(ug_ref[0, 0], wg_ref[0, 0], bg_ref[0, 0])
        v = _conv(uv_ref[0, 0], wv_ref[0, 0], bv_ref[0, 0])
        act_ref[0] = (g * jax.nn.sigmoid(g) * v).astype(BF16)

        @pl.when((d == 3) & (j == _NCB - 1))
        def _():
            for cp in _pass_on_copies(buf_refs, *sems):
                cp.wait()

    anyspec = pl.BlockSpec(memory_space=pl.ANY)
    res = pl.pallas_call(
        body, name="ffn_act", grid=(4, _NCB),
        in_specs=[_chan(0, T), _chan(1, T), _chan(0, 3), _chan(1, 3), _chan(0, 1), _chan(1, 1)] + [anyspec] * m,
        out_specs=[pl.BlockSpec((1, T, LANES), lambda d, j: (d, 0, j))] + [anyspec] * m,
        out_shape=[_sds((4, T, CS), BF16)] + [_sds(b.shape, b.dtype) for b in bufs],
        input_output_aliases={6 + i: 1 + i for i in range(m)},
        scratch_shapes=[pltpu.SemaphoreType.DMA((3 * m,)), pltpu.SemaphoreType.DMA((3 * m,))],
        compiler_params=_cp(("arbitrary", "arbitrary")),
    )(up4, up4, cw4, cw4, cb4, cb4, *bufs)
    return res[0], res[1:]


def _ffn_act_bwd(up4, dact, cw4, cb4):
    def conv_bwd(h, w, d):
        dup = w[2:3, :] * d + w[1:2, :] * _shift_up(d, 1) + w[0:1, :] * _shift_up(d, 2)
        dw = jnp.concatenate([jnp.sum(d * _shift_down(h, 2), axis=0, keepdims=True),
                              jnp.sum(d * _shift_down(h, 1), axis=0, keepdims=True),
                              jnp.sum(d * h, axis=0, keepdims=True)], axis=0)
        return dup, dw, jnp.sum(d, axis=0, keepdims=True)

    def body(ug_ref, uv_ref, da_ref, wg_ref, wv_ref, bg_ref, bv_ref, dup_ref, dw_ref, db_ref):
        ug, uv, da = ug_ref[0, 0], uv_ref[0, 0], da_ref[0]
        g = _conv(ug, wg_ref[0, 0], bg_ref[0, 0])
        v = _conv(uv, wv_ref[0, 0], bv_ref[0, 0])
        sg = jax.nn.sigmoid(g)
        dg = da * v * (sg * (1.0 + g * (1.0 - sg)))
        dv = da * (g * sg)
        dug, dwg, dbg = conv_bwd(ug, wg_ref[0, 0], dg)
        duv, dwv, dbv = conv_bwd(uv, wv_ref[0, 0], dv)
        dup_ref[0, 0] = dug.astype(BF16)
        dup_ref[1, 0] = duv.astype(BF16)
        dw_ref[0, 0] = dwg
        dw_ref[1, 0] = dwv
        db_ref[0, 0] = dbg
        db_ref[1, 0] = dbv

    both = lambda rows: pl.BlockSpec((2, 1, rows, LANES), lambda d, j: (0, d, 0, j))
    return pl.pallas_call(
        body, name="ffn_act_bwd", grid=(4, _NCB),
        in_specs=[_chan(0, T), _chan(1, T), pl.BlockSpec((1, T, LANES), lambda d, j: (d, 0, j)),
                  _chan(0, 3), _chan(1, 3), _chan(0, 1), _chan(1, 1)],
        out_specs=[both(T), both(3), both(1)],
        out_shape=[_sds((2, 4, T, CS), BF16), _sds((2, 4, 3, CS)), _sds((2, 4, 1, CS))],
        compiler_params=_cp(("parallel", "parallel")),
    )(up4, up4, dact, cw4, cw4, cb4, cb4)


def _ffn_down_loss(act, wd4, x1, target):
    tm = 512

    def body(a_ref, w_ref, x1_ref, t_ref, loss_ref, dx_ref, dxb_ref):
        x2 = x1_ref[...]
        for d in range(4):
            x2 = x2 + _nn(a_ref[d], w_ref[d])
        err = x2 - t_ref[...]
        dx = err * (1.0 / D)
        dx_ref[...] = dx
        dxb_ref[...] = dx.astype(BF16)

        @pl.when(pl.program_id(0) == 0)
        def _():
            loss_ref[...] = jnp.zeros_like(loss_ref)
        loss_ref[...] += 0.5 * jnp.sum(jnp.mean(err * err, axis=-1, keepdims=True), axis=0, keepdims=True)

    row = lambda i: (i, 0)
    fixed = lambda i: (0, 0)
    return pl.pallas_call(
        body, name="ffn_down_loss", grid=(T // tm,),
        in_specs=[pl.BlockSpec((4, tm, CS), lambda i: (0, i, 0)), pl.BlockSpec((4, CS, D), lambda i: (0, 0, 0)),
                  pl.BlockSpec((tm, D), row), pl.BlockSpec((tm, D), row)],
        out_specs=[pl.BlockSpec((1, 1), fixed), pl.BlockSpec((tm, D), row), pl.BlockSpec((tm, D), row)],
        out_shape=[_sds((1, 1)), _sds((T, D)), _sds((T, D), BF16)],
        compiler_params=_cp(("arbitrary",)),
    )(act, wd4, x1, target)


def _ffn_dact(dx2b, wd4):
    tm = 2048

    def body(d_ref, w_ref, o_ref):
        o_ref[0] = _nt(d_ref[...], w_ref[0])

    return pl.pallas_call(
        body, name="ffn_dact", grid=(4, T // tm),
        in_specs=[pl.BlockSpec((tm, D), lambda d, i: (i, 0)), pl.BlockSpec((1, CS, D), lambda d, i: (d, 0, 0))],
        out_specs=pl.BlockSpec((1, tm, CS), lambda d, i: (d, i, 0)), out_shape=_sds((4, T, CS)),
        compiler_params=_cp(("parallel", "parallel")),
    )(dx2b, wd4)


def _ffn_dwd(act, dx2b):
    def body(a_ref, d_ref, o_ref):
        o_ref[0] = _tn(a_ref[0], d_ref[...]).astype(BF16)

    return pl.pallas_call(
        body, name="ffn_dwd", grid=(4,),
        in_specs=[pl.BlockSpec((1, T, CS), lambda d: (d, 0, 0)), pl.BlockSpec((T, D), lambda d: (0, 0))],
        out_specs=pl.BlockSpec((1, CS, D), lambda d: (d, 0, 0)), out_shape=_sds((4, CS, D), BF16),
        compiler_params=_cp(("parallel",)),
    )(act, dx2b)


def _ffn_dwu(h2, dup4):
    tn = 1024

    def body(h_ref, d_ref, o_ref):
        o_ref[0, 0] = _tn(d_ref[0, 0], h_ref[...]).astype(BF16)

    return pl.pallas_call(
        body, name="ffn_dwu", grid=(NDEV, D // tn),
        in_specs=[pl.BlockSpec((T, tn), lambda e, i: (0, i)),
                  pl.BlockSpec((1, 1, T, CS), lambda e, i: (e // 4, e % 4, 0, 0))],
        out_specs=pl.BlockSpec((1, 1, CS, tn), lambda e, i: (e // 4, e % 4, 0, i)),
        out_shape=_sds((2, 4, CS, D), BF16),
        compiler_params=_cp(("parallel", "parallel")),
    )(h2, dup4)


def _ffn_up_bwd(dup4, wu4, x1, g, dres, arrs):
    tm = 256
    n = len(arrs)

    def body(d_ref, w_ref, x_ref, g_ref, dres_ref, *rest):
        a_refs, (dx_ref, dg_ref), fs_refs, sems = rest[:n], rest[n:n + 2], rest[n + 2:2 * n + 2], rest[2 * n + 2:]

        @pl.when(pl.program_id(0) == 0)
        def _():
            for cp in _sibling_swap_copies(a_refs, fs_refs, *sems):
                cp.start()

        dh = jnp.zeros((tm, D), F32)
        for gv in range(2):
            for d in range(4):
                dh = dh + _nn(d_ref[gv, d], w_ref[gv, d])
        xv = x_ref[...]
        r = _rms_r(xv)
        dxr, dgp = _rms_bwd(xv * r, r, g_ref[...], dh)
        dx_ref[...] = dres_ref[...] + dxr

        @pl.when(pl.program_id(0) == 0)
        def _():
            dg_ref[...] = jnp.zeros_like(dg_ref)
        dg_ref[...] += dgp

        @pl.when(pl.program_id(0) == T // tm - 1)
        def _():
            for cp in _sibling_swap_copies(a_refs, fs_refs, *sems):
                cp.wait()

    row = lambda i: (i, 0)
    fixed = lambda i: (0, 0)
    anyspec = pl.BlockSpec(memory_space=pl.ANY)
    res = pl.pallas_call(
        body, name="ffn_up_bwd", grid=(T // tm,),
        in_specs=[pl.BlockSpec((2, 4, tm, CS), lambda i: (0, 0, i, 0)), pl.BlockSpec((2, 4, CS, D), lambda i: (0, 0, 0, 0)),
                  pl.BlockSpec((tm, D), row), pl.BlockSpec((1, D), fixed), pl.BlockSpec((tm, D), row)] + [anyspec] * n,
        out_specs=[pl.BlockSpec((tm, D), row), pl.BlockSpec((1, D), fixed)] + [anyspec] * n,
        out_shape=[_sds((T, D)), _sds((1, D))] + [_sds((4, *a.shape[2:]), a.dtype) for a in arrs],
        scratch_shapes=[pltpu.SemaphoreType.DMA((4 * n,)), pltpu.SemaphoreType.DMA((4 * n,))],
        compiler_params=_cp(("arbitrary",)),
    )(dup4, wu4, x1, g, dres, *arrs)
    return res[0], res[1], res[2:]


def _adamw_math(w, g, m, v):
    m = ADAM_B1 * m + (1.0 - ADAM_B1) * g
    v = ADAM_B2 * v + (1.0 - ADAM_B2) * jnp.square(g)
    m_hat = m / (1.0 - ADAM_B1 ** ADAM_STEP)
    v_hat = v / (1.0 - ADAM_B2 ** ADAM_STEP)
    delta = -ADAM_LR * (m_hat / (jnp.sqrt(v_hat) + ADAM_EPS) + ADAM_WD * w)
    return delta, m, v


def _where_am_i():
    x, y, c = _place()
    return jnp.stack([c, 2 * x + y]).astype(jnp.int32)


def _pair_sum(a4, recv, name):
    _, _, rows, cols = a4.shape
    tr = _row_tile(rows, cols, 3 << 20)

    def body(sel_ref, own_ref, recv_ref, out_ref):
        out_ref[...] = (own_ref[0].astype(F32) + recv_ref[...].astype(F32)).astype(out_ref.dtype)

    grid_spec = pltpu.PrefetchScalarGridSpec(
        num_scalar_prefetch=1, grid=(4, rows // tr),
        in_specs=[pl.BlockSpec((1, 1, tr, cols), lambda k, i, s: (k, s[0], i, 0)),
                  pl.BlockSpec((1, tr, cols), lambda k, i, s: (k, i, 0))],
        out_specs=pl.BlockSpec((1, tr, cols), lambda k, i, s: (k, i, 0)),
    )
    return pl.pallas_call(
        body, name=name, grid_spec=grid_spec, out_shape=_sds((4, rows, cols), a4.dtype),
        compiler_params=_cp(("arbitrary", "arbitrary")),
    )(_where_am_i(), a4, recv)


def _adamw_rs(a4, recv, from_chips, w, m, v, name):
    rows, cols = w.shape
    tr = _row_tile(rows, cols, 3 << 19)

    def body(sel_ref, own_ref, recv_ref, fc_ref, w_ref, m_ref, v_ref, g_ref, d_ref, nm_ref, nv_ref):
        g = own_ref[0, 0].astype(F32) + recv_ref[0].astype(F32)
        for j in range(3):
            g = g + fc_ref[j].astype(F32)
        d, nm, nv = _adamw_math(w_ref[...], g, m_ref[...], v_ref[...])
        g_ref[...] = g
        d_ref[...] = d
        nm_ref[...] = nm
        nv_ref[...] = nv

    flat = pl.BlockSpec((tr, cols), lambda i, s: (i, 0))
    grid_spec = pltpu.PrefetchScalarGridSpec(
        num_scalar_prefetch=1, grid=(rows // tr,),
        in_specs=[pl.BlockSpec((1, 1, tr, cols), lambda i, s: (s[1], s[0], i, 0)),
                  pl.BlockSpec((1, tr, cols), lambda i, s: (s[1], i, 0)),
                  pl.BlockSpec((3, tr, cols), lambda i, s: (0, i, 0)), flat, flat, flat],
        out_specs=[flat] * 4,
    )
    return pl.pallas_call(
        body, name=name, grid_spec=grid_spec, out_shape=[_sds((rows, cols))] * 4,
        compiler_params=_cp(("arbitrary",)),
    )(_where_am_i(), a4, recv, from_chips, w, m, v)


_SMALL_F32 = (("g_mix", (8, 128)), ("b_f", (1, 8)), ("g_q", (1, 64)), ("g_k", (1, 64)), ("lambda_re", (32, 64)),
              ("lambda_im", (32, 64)), ("log_step", (1, 32)), ("d_skip", (32, 16)), ("b_glu", (4, 128)),
              ("g_attn_out", (4, 128)), ("g_ssm_out", (4, 128)), ("g_ffn", (8, 128)), ("conv_b", (44, 128)))
_SMALL_PAIRS = (("b_re", "b_im"), ("c_re", "c_im"))
_PAIR_SHAPE = (NG * GS, NP)


def _ceil8(r):
    return -(-r // 8) * 8


def _small_names():
    return [n for n, _ in _SMALL_F32] + [n for pair in _SMALL_PAIRS for n in pair]


def _pack_small_grads(g, loss_dev):
    parts = [jnp.pad(g[n].reshape(r, c), ((0, _ceil8(r) - r), (0, LANES - c))) for n, (r, c) in _SMALL_F32]
    parts.append(jnp.pad(loss_dev, ((0, 7), (0, LANES - 1))))
    return jnp.concatenate(parts, axis=0)


def _pack_pair_grads(g):
    pairs = [jnp.concatenate([g[a].reshape(_PAIR_SHAPE), g[b].reshape(_PAIR_SHAPE)], axis=1) for a, b in _SMALL_PAIRS]
    return jnp.concatenate(pairs, axis=0).astype(BF16)


def _adamw_small(gall, gall_b, wmv):
    names = _small_names()
    shapes = [s for _, s in _SMALL_F32] + [_PAIR_SHAPE] * (2 * len(_SMALL_PAIRS))
    npar = len(names)

    def body(ga_ref, gb_ref, *rest):
        ins, loss_ref, outs = rest[:3 * npar], rest[3 * npar], rest[3 * npar + 1:]
        ga, gb = ga_ref[0], gb_ref[0].astype(F32)
        for dev in range(1, NDEV):
            ga = ga + ga_ref[dev]
            gb = gb + gb_ref[dev].astype(F32)
        grads, off = [], 0
        for _, (r, c) in _SMALL_F32:
            grads.append(ga[off:off + r, 0:c])
            off += _ceil8(r)
        loss_ref[...] = ga[off:off + 8]
        rows, cols = _PAIR_SHAPE
        for j in range(len(_SMALL_PAIRS)):
            for h in range(2):
                grads.append(gb[rows * j:rows * (j + 1), cols * h:cols * (h + 1)])
        for i, g in enumerate(grads):
            d, nm, nv = _adamw_math(ins[3 * i][...], g, ins[3 * i + 1][...], ins[3 * i + 2][...])
            outs[4 * i][...] = g
            outs[4 * i + 1][...] = d
            outs[4 * i + 2][...] = nm
            outs[4 * i + 3][...] = nv

    flat = [a for n in names for a in wmv[n]]
    out_shape = [_sds((8, LANES))] + [_sds(s) for s in shapes for _ in range(4)]
    res = pl.pallas_call(body, name="adamw_small", out_shape=out_shape, compiler_params=_cp())(gall, gall_b, *flat)
    return res[0], {n: res[1 + 4 * i:5 + 4 * i] for i, n in enumerate(names)}


def _block_diag(m):
    eye = jnp.eye(8, dtype=m.dtype)
    r, c = m.shape[2], m.shape[3]
    return (m[:, :, :, None, :] * eye[None, :, None, :, None]).reshape(4, 8 * r, 8 * c)


def _diag_blocks(dense, r, c):
    eye = jnp.eye(8, dtype=dense.dtype)
    return jnp.sum(dense.reshape(4, 8, r, 8, c) * eye[None, :, None, :, None], axis=3)


def kernel(x, g_mix, w_in, b_f, g_q, g_k, lambda_re, lambda_im, log_step, b_re, b_im, c_re, c_im, d_skip, w_glu, b_glu, g_attn_out, g_ssm_out, w_out, g_ffn, w_up, conv_w, conv_b, w_down, loss_target, m_g_mix, m_w_in, m_b_f, m_g_q, m_g_k, m_lambda_re, m_lambda_im, m_log_step, m_b_re, m_b_im, m_c_re, m_c_im, m_d_skip, m_w_glu, m_b_glu, m_g_attn_out, m_g_ssm_out, m_w_out, m_g_ffn, m_w_up, m_conv_w, m_conv_b, m_w_down, v_g_mix, v_w_in, v_b_f, v_g_q, v_g_k, v_lambda_re, v_lambda_im, v_log_step, v_b_re, v_b_im, v_c_re, v_c_im, v_d_skip, v_w_glu, v_b_glu, v_g_attn_out, v_g_ssm_out, v_w_out, v_g_ffn, v_w_up, v_conv_w, v_conv_b, v_w_down):
    weights = dict(g_mix=g_mix, w_in=w_in, b_f=b_f, g_q=g_q, g_k=g_k, lambda_re=lambda_re, lambda_im=lambda_im,
                   log_step=log_step, b_re=b_re, b_im=b_im, c_re=c_re, c_im=c_im, d_skip=d_skip, w_glu=w_glu,
                   b_glu=b_glu, g_attn_out=g_attn_out, g_ssm_out=g_ssm_out, w_out=w_out, g_ffn=g_ffn, w_up=w_up,
                   conv_w=conv_w, conv_b=conv_b, w_down=w_down)
    mom_m = dict(g_mix=m_g_mix, w_in=m_w_in, b_f=m_b_f, g_q=m_g_q, g_k=m_g_k, lambda_re=m_lambda_re,
                 lambda_im=m_lambda_im, log_step=m_log_step, b_re=m_b_re, b_im=m_b_im, c_re=m_c_re, c_im=m_c_im,
                 d_skip=m_d_skip, w_glu=m_w_glu, b_glu=m_b_glu, g_attn_out=m_g_attn_out, g_ssm_out=m_g_ssm_out,
                 w_out=m_w_out, g_ffn=m_g_ffn, w_up=m_w_up, conv_w=m_conv_w, conv_b=m_conv_b, w_down=m_w_down)
    mom_v = dict(g_mix=v_g_mix, w_in=v_w_in, b_f=v_b_f, g_q=v_g_q, g_k=v_g_k, lambda_re=v_lambda_re,
                 lambda_im=v_lambda_im, log_step=v_log_step, b_re=v_b_re, b_im=v_b_im, c_re=v_c_re, c_im=v_c_im,
                 d_skip=v_d_skip, w_glu=v_w_glu, b_glu=v_b_glu, g_attn_out=v_g_attn_out, g_ssm_out=v_g_ssm_out,
                 w_out=v_w_out, g_ffn=v_g_ffn, w_up=v_w_up, conv_w=v_conv_w, conv_b=v_conv_b, w_down=v_w_down)
    names = list(weights)

    xs = x[0]
    target = loss_target[0]
    row = lambda a: a.reshape(1, -1)

    (win_g,) = _all_gather([w_in.T.astype(BF16)], "gather_w_in")
    w_in_t = win_g.reshape(8 * 257, D)
    wcat = jnp.concatenate([w_in_t[0:1536], w_in_t[1544:2056], jnp.pad(w_in_t[1536:1544], ((0, LANES - NH), (0, 0)))],
                           axis=0)
    cb4 = conv_b.reshape(2, 4, 1, CS)

    h1, z, f_t, (wg_g, wo_g) = _in_proj(xs, row(g_mix), wcat, [w_glu.astype(BF16), w_out.astype(BF16)])
    bf_col = b_f.reshape(NH, 1)
    ck = _gates_fwd(f_t, bf_col).reshape(_HP, 2, T)
    mlp_shards = [w_up.T.astype(BF16), conv_w]
    att, lse, (wu_g, cw_g), (wg_g, wo_g) = _attn_fwd(z, ck, row(g_q), row(g_k), mlp_shards, [wg_g, wo_g])

    ls_col = log_step.reshape(NG, 1)
    br_t, bi_t = b_re.transpose(0, 2, 1), b_im.transpose(0, 2, 1)
    ab_re, ab_im, bb_re, bb_im = _disc_fwd(lambda_re, lambda_im, ls_col, br_t, bi_t)
    bb = jnp.concatenate([_block_diag(bb_re.reshape(4, 8, GS, NP)), _block_diag(bb_im.reshape(4, 8, GS, NP))],
                         axis=2).astype(BF16)
    ab = jnp.stack([ab_re.reshape(4, 512), ab_im.reshape(4, 512)], axis=1)
    cdiag = lambda cm: _block_diag(cm.reshape(4, 8, GS, NP).transpose(0, 1, 3, 2))
    cc = jnp.concatenate([cdiag(c_re), -cdiag(c_im)], axis=1).astype(BF16)
    dsk = d_skip.reshape(4, 1, LANES)
    y, xs_il, (wu_g, cw_g) = _ssm_fwd(z, bb, ab, cc, dsk, mlp_shards, [wu_g, cw_g])
    wg_f = wg_g.reshape(SW, SW)
    wo_f = wo_g.reshape(D, D)
    x1, (wu_g, cw_g) = _mix_fwd(att, y, xs, wg_f, row(b_glu), row(g_attn_out), row(g_ssm_out), wo_f, [wu_g, cw_g])
    wu4 = wu_g.reshape(2, 4, CS, D)
    cw4 = cw_g.reshape(2, 4, 3, CS)
    h2, up4, (wd_g,) = _ffn_up(x1, row(g_ffn), wu4, [w_down.astype(BF16)])
    act, (wd_g,) = _ffn_act(up4, cw4, cb4, [wd_g])
    wd4 = wd_g.reshape(4, CS, D)
    loss_dev, dx2, dx2b = _ffn_down_loss(act, wd4, x1, target)

    dact = _ffn_dact(dx2b, wd4)
    d_wd = _ffn_dwd(act, dx2b)
    dup4, dcw4, dcb4 = _ffn_act_bwd(up4, dact, cw4, cb4)
    d_wu = _ffn_dwu(h2, dup4)
    mlp, mixer = ("w_up", "w_down", "conv_w"), ("w_out", "w_glu")
    early = mlp + mixer
    by_dest = {"w_up": d_wu.reshape(4, 2, CS, D), "w_down": d_wd.reshape(4, 2, DFF // NDEV, D),
               "conv_w": dcw4.reshape(4, 2, 3, CS)}
    dx1, dg_ffn, fs = _ffn_up_bwd(dup4, wu4, x1, row(g_ffn), dx2, [by_dest[n] for n in mlp])
    from_sibling = dict(zip(mlp, fs))
    chip_sums = {n: _pair_sum(by_dest[n], from_sibling[n], "rs_pair_sum_" + n) for n in mlp}
    datt, dy, d_wg, d_bg, d_ga, d_gs, d_wo = _mix_bwd(att, y, dx1, wg_f, row(b_glu), row(g_attn_out),
                                                       row(g_ssm_out), wo_f)
    by_dest.update({"w_out": d_wo.reshape(4, 2, D // NDEV, D), "w_glu": d_wg.reshape(4, 2, SW // NDEV, SW)})
    under_ssm = ("w_down", "conv_w")
    du, dbb, dab, dcc, ddsk, fs, fc = _ssm_bwd(z, dy, xs_il, bb, ab, cc, dsk, [by_dest[n] for n in mixer],
                                               [chip_sums[n] for n in under_ssm])
    from_sibling.update(zip(mixer, fs))
    from_chips = dict(zip(under_ssm, fc))
    chip_sums.update({n: _pair_sum(by_dest[n], from_sibling[n], "rs_pair_sum_" + n) for n in mixer})
    du = du.astype(BF16)
    dbb_re = _diag_blocks(dbb[:, :, 0:512], GS, NP).reshape(NG, GS, NP)
    dbb_im = _diag_blocks(dbb[:, :, 512:1024], GS, NP).reshape(NG, GS, NP)
    dlr, dli, dls, dbr_t, dbi_t = _disc_bwd(lambda_re, lambda_im, ls_col, br_t, bi_t,
                                            dab[:, 0].reshape(NG, NP), dab[:, 1].reshape(NG, NP), dbb_re, dbb_im)
    d_cre = _diag_blocks(dcc[:, 0:512], NP, GS).transpose(0, 1, 3, 2).reshape(NG, GS, NP)
    d_cim = -_diag_blocks(dcc[:, 512:1024], NP, GS).transpose(0, 1, 3, 2).reshape(NG, GS, NP)

    under_attn = ("w_up",) + mixer
    pair_grads = _pack_pair_grads(dict(b_re=dbr_t, b_im=dbi_t, c_re=d_cre, c_im=d_cim))
    dq, dk, dv, dck, d_gq, d_gk, fc, (small_all_b,) = _attn_bwd(z, ck, row(g_q), row(g_k), lse, datt,
                                                                [chip_sums[n] for n in under_attn], [pair_grads])
    from_chips.update(zip(under_attn, fc))
    df_t, d_bf = _gates_bwd(dck.reshape(NH, T), f_t, bf_col)
    df = jnp.concatenate([df_t.T, jnp.zeros((T, LANES - NH), F32)], axis=1).astype(BF16)
    d_wcat = _in_proj_dw(h1, dq, dk, dv, du, df)
    d_win = jnp.concatenate([d_wcat[0:1536], d_wcat[2048:2048 + NH], d_wcat[1536:2048]], axis=0)
    by_dest["w_in"] = d_win.astype(BF16).reshape(4, 2, 257, D)
    (from_sibling["w_in"],) = _swap_sibling([by_dest["w_in"]], "rs_pair_swap_w_in")
    chip_sums["w_in"] = _pair_sum(by_dest["w_in"], from_sibling["w_in"], "rs_pair_sum_w_in")
    grad_x, dg_mix, (from_chips["w_in"],), (small_all_b,) = _in_proj_bwd(
        dq, dk, dv, du, df, wcat, xs, row(g_mix), dx1, [chip_sums["w_in"]], [small_all_b])

    grads, deltas, new_m, new_v = {}, {}, {}, {}
    for n in early + ("w_in",):
        flip = (lambda a: a.T) if n in ("w_up", "w_in") else (lambda a: a)
        outs = _adamw_rs(by_dest[n], from_sibling[n], from_chips[n], flip(weights[n]), flip(mom_m[n]), flip(mom_v[n]),
                         "adamw_" + n)
        grads[n], deltas[n], new_m[n], new_v[n] = (flip(o) for o in outs)

    small = dict(g_mix=dg_mix, b_f=d_bf, g_q=d_gq, g_k=d_gk, lambda_re=dlr, lambda_im=dli, log_step=dls,
                 b_re=dbr_t, b_im=dbi_t, c_re=d_cre, c_im=d_cim, d_skip=ddsk, b_glu=d_bg, g_attn_out=d_ga,
                 g_ssm_out=d_gs, g_ffn=dg_ffn, conv_b=dcb4)
    (small_all,) = _all_gather([_pack_small_grads(small, loss_dev)], "gather_small_grads")
    views = dict(_SMALL_F32)
    swapped = ("b_re", "b_im")

    def view(n, a):
        return a.transpose(0, 2, 1).reshape(_PAIR_SHAPE) if n in swapped else a.reshape(views.get(n, _PAIR_SHAPE))

    loss_p, small_out = _adamw_small(small_all, small_all_b,
                                     {n: (view(n, weights[n]), view(n, mom_m[n]), view(n, mom_v[n])) for n in small})
    loss = loss_p[0, 0]
    for n in small:
        back = (lambda a: a.reshape(NG, GS, NP).transpose(0, 2, 1)) if n in swapped else (lambda a: a.reshape(weights[n].shape))
        grads[n], deltas[n], new_m[n], new_v[n] = (back(o) for o in small_out[n])

    return (loss, grad_x[None], *[grads[n] for n in names], *[deltas[n] for n in names],
            *[new_m[n] for n in names], *[new_v[n] for n in names])
```
